```python
import math, functools
import jax, jax.numpy as jnp
from jax import lax
import numpy as np

D_MODEL = 2048
BATCH = 4
SEQ = 2048
DEPTH = 2
DEC_BATCH = 128
DEC_SEQ = 4
PAST_LEN = 2048
PAGE_SIZE = 128

MIX_WIDTH = D_MODEL
A_WIDTH = MIX_WIDTH // 2
HEAD_DIM = 128
A_HEADS = A_WIDTH // HEAD_DIM
A_KV_HEADS = 2
IDX_HEADS = 16
IDX_DIM = 64
TOPK_MAX = 256
ATT_BLOCK = 128
NUM_BUCKETS = 32
MAX_DISTANCE = 128
B_WIDTH = MIX_WIDTH // 4
B_HEADS = 4
B_DV = B_WIDTH // B_HEADS
B_DK = B_DV // 2
GATE_RANK = 16
GATE_TEMP = 16.0
GLA_CHUNK = 64
C_WIDTH = MIX_WIDTH - A_WIDTH - B_WIDTH
C_GROUPS = 4
C_GROUP_DIM = C_WIDTH // C_GROUPS
GMLP_CHUNK = 128
FF_DIM = ((8 * D_MODEL // 3 + 255) // 256) * 256
PLE_DIM = 256
EPS = 1e-6

PROJ_SIZES = (A_HEADS * HEAD_DIM, A_KV_HEADS * HEAD_DIM, A_KV_HEADS * HEAD_DIM,
              IDX_HEADS * IDX_DIM, IDX_DIM, IDX_HEADS,
              B_HEADS * B_DK, B_HEADS * B_DK, B_WIDTH, GATE_RANK, B_WIDTH,
              C_WIDTH, C_WIDTH)
PROJ_TOTAL = sum(PROJ_SIZES)

kernel_name = "hybrid_dsa_gla_gmlp_decode_step"


def rmsnorm(x, g):
    xf = x.astype(jnp.float32)
    y = xf * lax.rsqrt(jnp.mean(xf * xf, axis=-1, keepdims=True) + EPS)
    return (y * g.astype(jnp.float32)).astype(x.dtype)


def split_proj(proj):
    out, start = [], 0
    for size in PROJ_SIZES:
        out.append(proj[..., start:start + size])
        start += size
    return out


def rel_bucket(dist):
    n = jnp.maximum(dist, 0)
    max_exact = NUM_BUCKETS // 2
    large = max_exact + (jnp.log(jnp.maximum(n, 1).astype(jnp.float32) / max_exact)
                         / math.log(MAX_DISTANCE / max_exact)
                         * (NUM_BUCKETS - max_exact)).astype(jnp.int32)
    large = jnp.minimum(large, NUM_BUCKETS - 1)
    return jnp.where(n < max_exact, n, large)


def indexer_scores(qi, wi, ki_all):
    s = jnp.einsum('bthd,bld->bthl', qi.astype(jnp.float32), ki_all.astype(jnp.float32)) * IDX_DIM ** -0.5
    return jnp.einsum('bth,bthl->btl', wi.astype(jnp.float32) * IDX_HEADS ** -0.5, jax.nn.relu(s))


def select_keys(scores, t_pos, n_keys, k_sel):
    key_pos = jnp.arange(n_keys)
    admissible = key_pos[None, None, :] <= t_pos[None, :, None]
    scores = jnp.where(admissible, scores, -jnp.inf)
    _, idx = lax.top_k(scores, k_sel)
    valid = idx <= t_pos[None, :, None]
    return idx, valid


def attend_selected(q, k_sel, v_sel, t_pos, sel_pos, valid, rel_bias):
    B_, T_, H_, Dh = q.shape
    G = A_KV_HEADS
    R = H_ // G
    K_ = sel_pos.shape[-1]
    qg = q.reshape(B_, T_, G, R, Dh).astype(jnp.float32)
    logits = jnp.einsum('btgrd,btkgd->btgrk', qg, k_sel.astype(jnp.float32)) * Dh ** -0.5
    bias = rel_bias.astype(jnp.float32)[rel_bucket(t_pos[None, :, None] - sel_pos)]
    bias = bias.reshape(B_, T_, K_, G, R).transpose(0, 1, 3, 4, 2)
    logits = jnp.where(valid[:, :, None, None, :], logits + bias, -1e30)
    p = jax.nn.softmax(logits, axis=-1)
    o = jnp.einsum('btgrk,btkgd->btgrd', p, v_sel.astype(jnp.float32))
    return o.reshape(B_, T_, H_ * Dh).astype(q.dtype)


def _gather_rows(arr, ix):
    return jax.vmap(lambda a, i: a[i])(arr, ix)


def dsa_prompt(q, k, v, qi, wi, ki, rel_bias):
    B_, S_ = q.shape[0], q.shape[1]
    k_sel = min(TOPK_MAX, S_ // 4)
    nb = S_ // ATT_BLOCK

    def to_blocks(x):
        return jnp.moveaxis(x.reshape(B_, nb, ATT_BLOCK, *x.shape[2:]), 1, 0)

    def block(args):
        qb, qib, wib, tb = args
        sc = indexer_scores(qib, wib, ki)
        idx, valid = select_keys(sc, tb, S_, k_sel)
        return attend_selected(qb, _gather_rows(k, idx), _gather_rows(v, idx), tb, idx, valid, rel_bias)

    t_blocks = jnp.arange(S_).reshape(nb, ATT_BLOCK)
    out = lax.map(block, (to_blocks(q), to_blocks(qi), to_blocks(wi), t_blocks))
    return jnp.moveaxis(out, 0, 1).reshape(B_, S_, A_WIDTH)


def dsa_sample(q, k_new, v_new, qi, wi, ki_new, cache_k_l, cache_v_l, cache_ik_l, page_table, rel_bias):
    DB, T_ = q.shape[0], q.shape[1]
    n_pages = page_table.shape[1]
    past = n_pages * PAGE_SIZE
    n_keys = past + T_
    k_sel = min(TOPK_MAX, n_keys // 4)
    t_pos = past + jnp.arange(T_)
    ki_past = cache_ik_l[page_table].reshape(DB, past, IDX_DIM)
    ki_all = jnp.concatenate([ki_past, ki_new.astype(ki_past.dtype)], axis=1)
    sc = indexer_scores(qi, wi, ki_all)
    idx, valid = select_keys(sc, t_pos, n_keys, k_sel)
    in_past = idx < past
    pidx = jnp.minimum(idx, past - 1)
    phys = jnp.take_along_axis(page_table, (pidx // PAGE_SIZE).reshape(DB, -1), axis=1).reshape(idx.shape)
    off = pidx % PAGE_SIZE
    nidx = jnp.clip(idx - past, 0, T_ - 1)

    def sel(cache_l, new):
        return jnp.where(in_past[..., None, None], cache_l[phys, off],
                         _gather_rows(new, nidx).astype(cache_l.dtype))

    return attend_selected(q, sel(cache_k_l, k_new), sel(cache_v_l, v_new), t_pos, idx, valid, rel_bias)


def gla(q, k, v, log_a, s0):
    B_, S_, H_, Dk = q.shape
    Dv = v.shape[-1]
    C = math.gcd(S_, GLA_CHUNK)
    n = S_ // C

    def chunks(x):
        return jnp.moveaxis(x.astype(jnp.float32).reshape(B_, n, C, *x.shape[2:]), 1, 0)

    tri = jnp.tril(jnp.ones((C, C), dtype=bool))

    def step(state, xs):
        qc, kc, vc, gc = xs
        b = jnp.cumsum(gc, axis=1)
        inter = jnp.einsum('bthk,bhkv->bthv', qc * jnp.exp(b), state)
        diff = jnp.where(tri[None, :, :, None, None], b[:, :, None] - b[:, None, :], -jnp.inf)
        att = jnp.einsum('bthk,bshk,btshk->bhts', qc, kc, jnp.exp(diff))
        intra = jnp.einsum('bhts,bshv->bthv', att, vc)
        b_last = b[:, -1]
        new_state = state * jnp.exp(b_last)[..., None] + jnp.einsum(
            'bshk,bshv->bhkv', kc * jnp.exp(b_last[:, None] - b), vc)
        return new_state, inter + intra

    s_fin, o = lax.scan(step, s0.astype(jnp.float32),
                        (chunks(q * B_DK ** -0.5), chunks(k), chunks(v), chunks(log_a)))
    o = jnp.moveaxis(o, 0, 1).reshape(B_, S_, H_, Dv)
    return o, s_fin


def chunk_mlp(uc, vc, g_v, w_spatial, b_spatial):
    B_, S_ = uc.shape[0], uc.shape[1]
    u = jax.nn.gelu(uc)
    vn = rmsnorm(jax.nn.gelu(vc).reshape(B_, S_, C_GROUPS, C_GROUP_DIM), g_v.reshape(C_GROUPS, C_GROUP_DIM))
    pad = (-S_) % GMLP_CHUNK
    n = (S_ + pad) // GMLP_CHUNK
    vp = jnp.pad(vn, ((0, 0), (0, pad), (0, 0), (0, 0))).reshape(B_, n, GMLP_CHUNK, C_GROUPS, C_GROUP_DIM)
    w = w_spatial * jnp.tril(jnp.ones((GMLP_CHUNK, GMLP_CHUNK), w_spatial.dtype))[None]
    s = jnp.einsum('gtj,bnjgc->bntgc', w, vp) + b_spatial.T[None, None, :, :, None]
    s = s.reshape(B_, S_ + pad, C_GROUPS, C_GROUP_DIM)[:, :S_].reshape(B_, S_, C_WIDTH)
    return (u * s).astype(uc.dtype), vn.reshape(B_, S_, C_WIDTH)


def layer_forward(h, p_l, lw, gla_s0, attn_fn):
    B_, S_ = h.shape[0], h.shape[1]
    n = rmsnorm(h, lw['g_mix'])
    q, k, v, qi, ki, wi, qb, kb, vb, gb, rb, uc, vc = split_proj(n @ lw['w_in'])
    q = rmsnorm(q.reshape(B_, S_, A_HEADS, HEAD_DIM), lw['q_norm_g'])
    k = rmsnorm(k.reshape(B_, S_, A_KV_HEADS, HEAD_DIM), lw['k_norm_g'])
    v = v.reshape(B_, S_, A_KV_HEADS, HEAD_DIM)
    qi = qi.reshape(B_, S_, IDX_HEADS, IDX_DIM)
    o_a = attn_fn(q, k, v, qi, wi, ki)
    log_a = jax.nn.log_sigmoid((gb @ lw['w_gate_b'] + lw['b_gate_b']).astype(jnp.float32)) / GATE_TEMP
    o_b, s_b = gla(qb.reshape(B_, S_, B_HEADS, B_DK), kb.reshape(B_, S_, B_HEADS, B_DK),
                   vb.reshape(B_, S_, B_HEADS, B_DV), log_a.reshape(B_, S_, B_HEADS, B_DK), gla_s0)
    o_b = rmsnorm(o_b, lw['g_out_b']).reshape(B_, S_, B_WIDTH).astype(h.dtype) * jax.nn.silu(rb)
    o_c, v_rows = chunk_mlp(uc, vc, lw['g_v_c'], lw['w_spatial'], lw['b_spatial'])
    h = h + jnp.concatenate([o_a, o_b, o_c], axis=-1) @ lw['w_out']
    n2 = rmsnorm(h, lw['g_ffn'])
    h = h + (jax.nn.silu(n2 @ lw['w_ffn_gate']) * (n2 @ lw['w_ffn_up'])) @ lw['w_ffn_down']
    gate = jax.nn.sigmoid(rmsnorm(h, lw['g_ple']) @ lw['w_ple_gate'])
    h = h + gate * (p_l @ lw['w_ple_proj'])
    return h, (k, v, ki, s_b, v_rows)


def setup_inputs(seed: int = 0) -> dict:
    key = jax.random.key(seed)
    ks = jax.random.split(key, 32)
    f32 = jnp.float32

    def nrm(k, shape, scale=1.0):
        return jax.random.normal(k, shape, f32) * scale

    def gain(k, shape):
        return 1.0 + 0.1 * jax.random.normal(k, shape, f32)

    n_pages = PAST_LEN // PAGE_SIZE
    in_use = DEC_BATCH * n_pages
    n_pool = in_use + max(1, in_use // 4)
    page_table = jax.random.permutation(ks[0], n_pool)[:in_use].reshape(DEC_BATCH, n_pages).astype(jnp.int32)
    return {
        'x_prompt': nrm(ks[1], (BATCH, SEQ, D_MODEL)),
        'x_sample': nrm(ks[2], (DEC_BATCH, DEC_SEQ, D_MODEL)),
        'cache_k': nrm(ks[3], (DEPTH, n_pool, PAGE_SIZE, A_KV_HEADS, HEAD_DIM)),
        'cache_v': nrm(ks[4], (DEPTH, n_pool, PAGE_SIZE, A_KV_HEADS, HEAD_DIM)),
        'cache_idx_k': nrm(ks[5], (DEPTH, n_pool, PAGE_SIZE, IDX_DIM)),
        'state_gla': nrm(ks[6], (DEPTH, DEC_BATCH, B_HEADS, B_DK, B_DV), 0.5),
        'page_table': page_table,
        'p_prompt': nrm(ks[7], (DEPTH, BATCH, SEQ, PLE_DIM)),
        'p_sample': nrm(ks[8], (DEPTH, DEC_BATCH, DEC_SEQ, PLE_DIM)),
        'g_mix': gain(ks[9], (DEPTH, D_MODEL)),
        'w_in': nrm(ks[10], (DEPTH, D_MODEL, PROJ_TOTAL), D_MODEL ** -0.5),
        'q_norm_g': gain(ks[11], (DEPTH, HEAD_DIM)),
        'k_norm_g': gain(ks[12], (DEPTH, HEAD_DIM)),
        'rel_bias': nrm(ks[13], (NUM_BUCKETS, A_HEADS), 0.5),
        'w_gate_b': nrm(ks[14], (DEPTH, GATE_RANK, B_HEADS * B_DK), GATE_RANK ** -0.5),
        'b_gate_b': nrm(ks[15], (DEPTH, B_HEADS * B_DK), 0.1),
        'g_out_b': gain(ks[16], (DEPTH, B_DV)),
        'g_v_c': gain(ks[17], (DEPTH, C_WIDTH)),
        'w_spatial': nrm(ks[18], (DEPTH, C_GROUPS, GMLP_CHUNK, GMLP_CHUNK), GMLP_CHUNK ** -0.5),
        'b_spatial': gain(ks[19], (DEPTH, C_GROUPS, GMLP_CHUNK)),
        'w_out': nrm(ks[20], (DEPTH, MIX_WIDTH, D_MODEL), MIX_WIDTH ** -0.5),
        'g_ffn': gain(ks[21], (DEPTH, D_MODEL)),
        'w_ffn_gate': nrm(ks[22], (DEPTH, D_MODEL, FF_DIM), D_MODEL ** -0.5),
        'w_ffn_up': nrm(ks[23], (DEPTH, D_MODEL, FF_DIM), D_MODEL ** -0.5),
        'w_ffn_down': nrm(ks[24], (DEPTH, FF_DIM, D_MODEL), FF_DIM ** -0.5),
        'g_ple': gain(ks[25], (DEPTH, D_MODEL)),
        'w_ple_gate': nrm(ks[26], (DEPTH, D_MODEL, D_MODEL), D_MODEL ** -0.5),
        'w_ple_proj': nrm(ks[27], (DEPTH, PLE_DIM, D_MODEL), PLE_DIM ** -0.5),
    }


def reference(x_prompt, x_sample, cache_k, cache_v, cache_idx_k, state_gla, page_table,
              p_prompt, p_sample, g_mix, w_in, q_norm_g, k_norm_g, rel_bias, w_gate_b, b_gate_b,
              g_out_b, g_v_c, w_spatial, b_spatial, w_out, g_ffn, w_ffn_gate, w_ffn_up, w_ffn_down,
              g_ple, w_ple_gate, w_ple_proj):
    hp, hs = x_prompt, x_sample
    n_batch = x_prompt.shape[0]
    kp, vp, ikp, sp = [], [], [], []
    ksm, vsm, iks, ss, cs = [], [], [], [], []
    attn_prompt = functools.partial(dsa_prompt, rel_bias=rel_bias)
    for i in range(DEPTH):
        lw = dict(g_mix=g_mix[i], w_in=w_in[i], q_norm_g=q_norm_g[i], k_norm_g=k_norm_g[i],
                  w_gate_b=w_gate_b[i], b_gate_b=b_gate_b[i], g_out_b=g_out_b[i], g_v_c=g_v_c[i],
                  w_spatial=w_spatial[i], b_spatial=b_spatial[i], w_out=w_out[i], g_ffn=g_ffn[i],
                  w_ffn_gate=w_ffn_gate[i], w_ffn_up=w_ffn_up[i], w_ffn_down=w_ffn_down[i],
                  g_ple=g_ple[i], w_ple_gate=w_ple_gate[i], w_ple_proj=w_ple_proj[i])
        s0 = jnp.zeros((n_batch, B_HEADS, B_DK, B_DV), jnp.float32)
        hp, (k_p, v_p, ik_p, s_p, _) = layer_forward(hp, p_prompt[i], lw, s0, attn_prompt)
        kp.append(k_p); vp.append(v_p); ikp.append(ik_p); sp.append(s_p)
        attn_sample = functools.partial(dsa_sample, cache_k_l=cache_k[i], cache_v_l=cache_v[i],
                                        cache_ik_l=cache_idx_k[i], page_table=page_table, rel_bias=rel_bias)
        hs, (k_s, v_s, ik_s, s_s, c_s) = layer_forward(hs, p_sample[i], lw, state_gla[i], attn_sample)
        ksm.append(k_s); vsm.append(v_s); iks.append(ik_s); ss.append(s_s); cs.append(c_s)
    return (hp, hs, jnp.stack(kp), jnp.stack(vp), jnp.stack(ikp), jnp.stack(sp),
            jnp.stack(ksm), jnp.stack(vsm), jnp.stack(iks), jnp.stack(ss), jnp.stack(cs))
```

```python
import functools
import math

import jax
import jax.numpy as jnp
from jax import lax
from jax.experimental import pallas as pl
from jax.experimental.pallas import tpu as pltpu

F32 = jnp.float32
BF16 = jnp.bfloat16
I32 = jnp.int32
HIGHEST = lax.Precision.HIGHEST

LANES = 128
SUBLANES = 8
VMEM_LIMIT = 56 * 1024 * 1024

HEAD_DIM = 128
A_HEADS = 8
A_KV_HEADS = 2
IDX_HEADS = 16
IDX_DIM = 64
TOPK_MAX = 256
NUM_BUCKETS = 32
MAX_DISTANCE = 128
B_HEADS = 4
B_DK = 64
B_DV = 128
GATE_RANK = 16
GATE_TEMP = 16.0
C_GROUPS = 4
C_GROUP_DIM = 128
PAGE_SIZE = 128
EPS = 1e-6
NEG_BIG = -1e30
INT_MIN = -(2 ** 31)

TILE = 128

Q_OFF, QI_OFF, VB_OFF, RB_OFF, UC_OFF, VC_OFF = 0, 1024, 2048, 2560, 3072, 3584
K_OFF, V_OFF, QB_OFF, KB_OFF, MISC_OFF = 4096, 4352, 4608, 4864, 5120
PROJ_PACKED = 5248
MISC_KI, MISC_WI, MISC_GB = 0, 64, 80


def _cparams(sem):
    return pltpu.CompilerParams(dimension_semantics=sem, vmem_limit_bytes=VMEM_LIMIT)


def _rms(x, g):
    return x * lax.rsqrt(jnp.mean(x * x, axis=-1, keepdims=True) + EPS) * g


def _resident(shape):
    nd = len(shape)
    return pl.BlockSpec(shape, lambda *_: (0,) * nd, pipeline_mode=pl.Buffered(1))


def _proj_kernel(x_ref, g_ref, w_ref, o_ref):
    n = _rms(x_ref[...], g_ref[...]).astype(BF16)
    ncol = o_ref.shape[1]
    step = 512
    for c0 in range(0, ncol, step):
        c1 = min(c0 + step, ncol)
        o_ref[:, c0:c1] = jnp.dot(n, w_ref[:, c0:c1], preferred_element_type=F32)


def in_projection(h, g, w_packed, tm):
    T, D = h.shape
    tm = min(tm, T)
    N = w_packed.shape[1]
    return pl.pallas_call(
        _proj_kernel,
        out_shape=jax.ShapeDtypeStruct((T, N), F32),
        grid=(T // tm,),
        in_specs=[pl.BlockSpec((tm, D), lambda i: (i, 0)),
                  _resident((1, D)),
                  _resident((D, N))],
        out_specs=pl.BlockSpec((tm, N), lambda i: (i, 0)),
        compiler_params=_cparams(("parallel",)),
        name="in_projection",
    )(h, g.reshape(1, D), w_packed)


def _kv_kernel(k_ref, v_ref, m_ref, g_ref, ko_ref, vo_ref, io_ref):
    g = g_ref[...]
    k = k_ref[...]
    for hh in range(A_KV_HEADS):
        sl = slice(hh * HEAD_DIM, (hh + 1) * HEAD_DIM)
        ko_ref[:, sl] = _rms(k[:, sl], g)
    vo_ref[...] = v_ref[...]
    io_ref[...] = m_ref[:, MISC_KI:MISC_KI + IDX_DIM]


def kv_post(proj, k_norm_g, tm):
    T = proj.shape[0]
    tm = min(tm, T)
    kw = A_KV_HEADS * HEAD_DIM
    return pl.pallas_call(
        _kv_kernel,
        out_shape=(jax.ShapeDtypeStruct((T, kw), F32),
                   jax.ShapeDtypeStruct((T, kw), F32),
                   jax.ShapeDtypeStruct((T, IDX_DIM), F32)),
        grid=(T // tm,),
        in_specs=[pl.BlockSpec((tm, kw), lambda i: (i, K_OFF // kw)),
                  pl.BlockSpec((tm, kw), lambda i: (i, V_OFF // kw)),
                  pl.BlockSpec((tm, LANES), lambda i: (i, MISC_OFF // LANES)),
                  _resident((1, HEAD_DIM))],
        out_specs=(pl.BlockSpec((tm, kw), lambda i: (i, 0)),
                   pl.BlockSpec((tm, kw), lambda i: (i, 0)),
                   pl.BlockSpec((tm, IDX_DIM), lambda i: (i, 0))),
        compiler_params=_cparams(("parallel",)),
        name="kv_post",
    )(proj, proj, proj, k_norm_g.reshape(1, HEAD_DIM))


def _bucket(dist):
    n = jnp.maximum(dist, 0)
    max_exact = NUM_BUCKETS // 2
    large = max_exact + (jnp.log(jnp.maximum(n, 1).astype(F32) / max_exact)
                         / math.log(MAX_DISTANCE / max_exact)
                         * (NUM_BUCKETS - max_exact)).astype(I32)
    large = jnp.minimum(large, NUM_BUCKETS - 1)
    return jnp.where(n < max_exact, n, large)


def _bias_prompt_kernel(rb_ref, o_ref):
    t = lax.broadcasted_iota(I32, (TILE, TILE), 0)
    c = lax.broadcasted_iota(I32, (TILE, TILE), 1)
    for z in range(3):
        bucket = _bucket(t - c + (2 - z) * TILE)
        for h in range(A_HEADS):
            acc = jnp.zeros((TILE, TILE), F32)
            for b in range(NUM_BUCKETS):
                acc = jnp.where(bucket == b, rb_ref[b, h], acc)
            o_ref[h, z] = acc


def bias_table_prompt(rel_bias):
    return pl.pallas_call(
        _bias_prompt_kernel,
        out_shape=jax.ShapeDtypeStruct((A_HEADS, 3, TILE, TILE), F32),
        in_specs=[pl.BlockSpec(memory_space=pltpu.SMEM)],
        out_specs=pl.BlockSpec(memory_space=pltpu.VMEM),
        name="bias_table_prompt",
    )(rel_bias)


def _bias_sample_kernel(rbrows_ref, o_ref, *, past, n_tok):
    rows, L = o_ref.shape
    r = lax.broadcasted_iota(I32, (rows, L), 0)
    s = lax.broadcasted_iota(I32, (rows, L), 1)
    bucket = _bucket(past + r // A_HEADS - s)
    rbrows = rbrows_ref[...]
    acc = jnp.zeros((rows, L), F32)
    for b in range(NUM_BUCKETS):
        acc = jnp.where(bucket == b, rbrows[:, b:b + 1], acc)
    o_ref[...] = acc


def bias_table_sample(rel_bias, past, n_tok, L):
    rows = n_tok * A_HEADS
    rbrows = jnp.tile(rel_bias.T, (n_tok, 1))
    return pl.pallas_call(
        functools.partial(_bias_sample_kernel, past=past, n_tok=n_tok),
        out_shape=jax.ShapeDtypeStruct((rows, L), F32),
        name="bias_table_sample",
    )(rbrows)


def _sortable_key(x):
    b = lax.bitcast_convert_type(x, I32)
    return b ^ ((b >> 31) & 0x7FFFFFFF)


def _topk_member(skey_ref, k_sel):
    R, L = skey_ref.shape

    def body(it, ans):
        bit = 31 - it
        cand = ans | lax.shift_left(jnp.int32(1), bit)
        cand_s = cand ^ INT_MIN
        cnt = jnp.sum(jnp.where(skey_ref[...] >= cand_s, 1.0, 0.0), axis=-1, keepdims=True)
        return jnp.where(cnt >= k_sel, cand, ans)

    ans = lax.fori_loop(0, 32, body, jnp.zeros((R, 1), I32))
    tau = ans ^ INT_MIN
    skey = skey_ref[...]
    gt = skey > tau
    eq = skey == tau
    n_gt = jnp.sum(jnp.where(gt, 1.0, 0.0), axis=-1, keepdims=True)
    room = k_sel - n_gt
    r_i = lax.broadcasted_iota(I32, (LANES, LANES), 0)
    c_i = lax.broadcasted_iota(I32, (LANES, LANES), 1)
    upper = jnp.where(r_i <= c_i, 1.0, 0.0).astype(BF16)
    off = jnp.zeros((R, 1), F32)
    parts = []
    for j in range(L // LANES):
        sl = slice(j * LANES, (j + 1) * LANES)
        eq_j = eq[:, sl]
        run = jnp.dot(jnp.where(eq_j, 1.0, 0.0).astype(BF16), upper, preferred_element_type=F32) + off
        parts.append(gt[:, sl] | (eq_j & (run <= room)))
        off = run[:, LANES - 1:LANES]
    return jnp.concatenate(parts, axis=1)


def _dsa_prompt_kernel(q_ref, qi_ref, misc_ref, kn_ref, v_ref, bias_ref, qg_ref, o_ref, skey_ref, *, k_sel):
    i = pl.program_id(1)
    S = kn_ref.shape[0]
    nt = S // TILE
    row0 = pl.multiple_of(i * TILE, TILE)

    ki = misc_ref[:, MISC_KI:MISC_KI + IDX_DIM]
    wi = misc_ref[pl.ds(row0, TILE), MISC_WI:MISC_WI + IDX_HEADS]
    wi = wi * (IDX_HEADS ** -0.5 * IDX_DIM ** -0.5)
    score = jnp.zeros((TILE, S), F32)
    for h in range(IDX_HEADS):
        qh = qi_ref[:, h * IDX_DIM:(h + 1) * IDX_DIM]
        s_h = lax.dot_general(qh, ki, (((1,), (1,)), ((), ())), precision=HIGHEST,
                              preferred_element_type=F32)
        score = score + wi[:, h:h + 1] * jnp.maximum(s_h, 0.0)

    t_pos = row0 + lax.broadcasted_iota(I32, (TILE, S), 0)
    s_pos = lax.broadcasted_iota(I32, (TILE, S), 1)
    adm = s_pos <= t_pos
    skey_ref[...] = _sortable_key(jnp.where(adm, score, -jnp.inf))
    sel = _topk_member(skey_ref, k_sel) & adm

    qg = qg_ref[...]
    rep = A_HEADS // A_KV_HEADS
    for h in range(A_HEADS):
        g = h // rep
        hs = slice(h * HEAD_DIM, (h + 1) * HEAD_DIM)
        gs = slice(g * HEAD_DIM, (g + 1) * HEAD_DIM)
        qh = (_rms(q_ref[:, hs], qg) * HEAD_DIM ** -0.5).astype(BF16)
        logits = lax.dot_general(qh, kn_ref[:, gs].astype(BF16), (((1,), (1,)), ((), ())),
                                 preferred_element_type=F32)
        tiles = []
        for j in range(nt):
            zone = jnp.clip(2 - (i - j), 0, 2)
            tiles.append(bias_ref[h, zone])
        logits = jnp.where(sel, logits + jnp.concatenate(tiles, axis=1), NEG_BIG)
        m = jnp.max(logits, axis=-1, keepdims=True)
        p = jnp.exp(logits - m)
        den = jnp.sum(p, axis=-1, keepdims=True)
        o = jnp.dot(p.astype(BF16), v_ref[:, gs].astype(BF16), preferred_element_type=F32)
        o_ref[:, hs] = (o / den).astype(o_ref.dtype)


def dsa_prompt(proj, kn, bias_tab, q_norm_g, n_batch, seq):
    T = proj.shape[0]
    nb = seq // TILE
    k_sel = min(TOPK_MAX, seq // 4)
    aw = A_HEADS * HEAD_DIM
    iw = IDX_HEADS * IDX_DIM
    kw = A_KV_HEADS * HEAD_DIM
    return pl.pallas_call(
        functools.partial(_dsa_prompt_kernel, k_sel=k_sel),
        out_shape=jax.ShapeDtypeStruct((T, aw), BF16),
        grid=(n_batch, nb),
        in_specs=[pl.BlockSpec((TILE, aw), lambda b, i: (b * nb + i, Q_OFF // aw)),
                  pl.BlockSpec((TILE, iw), lambda b, i: (b * nb + i, QI_OFF // iw)),
                  pl.BlockSpec((seq, LANES), lambda b, i: (b, MISC_OFF // LANES)),
                  pl.BlockSpec((seq, kw), lambda b, i: (b, 0)),
                  pl.BlockSpec((seq, kw), lambda b, i: (b, V_OFF // kw)),
                  _resident((A_HEADS, 3, TILE, TILE)),
                  _resident((1, HEAD_DIM))],
        out_specs=pl.BlockSpec((TILE, aw), lambda b, i: (b * nb + i, 0)),
        scratch_shapes=[pltpu.VMEM((TILE, seq), I32)],
        compiler_params=_cparams(("parallel", "arbitrary")),
        name="dsa_prompt",
    )(proj, proj, proj, kn, proj, bias_tab, q_norm_g.reshape(1, HEAD_DIM))


def _dsa_sample_select_kernel(pt_ref, *refs, n_pages, n_tok, k_sel, rows_pad):
    del pt_ref
    page_refs = refs[:n_pages]
    qi_ref, wm_ref, kin_ref, mask_ref, skey_ref = refs[n_pages:]
    b = pl.program_id(0)
    nb = pl.num_programs(0)
    L = mask_ref.shape[1]
    past = n_pages * PAGE_SIZE

    qi = qi_ref[0]
    ki_new = jnp.concatenate(
        [kin_ref[0], jnp.zeros((PAGE_SIZE - n_tok, IDX_DIM), F32)], axis=0)
    pieces = []
    for p in range(n_pages + 1):
        kp = page_refs[p][0] if p < n_pages else ki_new
        s = lax.dot_general(qi, kp, (((1,), (1,)), ((), ())), precision=HIGHEST,
                            preferred_element_type=F32)
        pieces.append(jnp.maximum(s, 0.0))
    relu_s = jnp.concatenate(pieces, axis=1)
    score = jnp.dot(wm_ref[0], relu_s, precision=HIGHEST, preferred_element_type=F32)
    t_pos = past + lax.broadcasted_iota(I32, (rows_pad, L), 0)
    s_pos = lax.broadcasted_iota(I32, (rows_pad, L), 1)
    adm = s_pos <= t_pos
    r0 = pl.multiple_of(b * rows_pad, rows_pad)
    skey_ref[pl.ds(r0, rows_pad), :] = _sortable_key(jnp.where(adm, score, -jnp.inf))

    @pl.when(b == nb - 1)
    def _():
        n_blocks = skey_ref.shape[0] // TILE
        tp = past + lax.broadcasted_iota(I32, (TILE, L), 0) % rows_pad
        sp = lax.broadcasted_iota(I32, (TILE, L), 1)
        adm_blk = sp <= tp
        for rb in range(n_blocks):
            blk = skey_ref.at[rb * TILE:(rb + 1) * TILE, :]
            sel = _topk_member(blk, k_sel) & adm_blk
            mask_ref[rb * TILE:(rb + 1) * TILE, :] = jnp.where(sel, 1.0, 0.0)


def dsa_sample_select(cache_ik_l, page_table, qi_rows, wmat, ki_new, k_sel):
    DB, n_pages = page_table.shape
    n_tok = ki_new.shape[1]
    rows_pad = wmat.shape[1]
    L = (n_pages + 1) * PAGE_SIZE
    page_specs = [pl.BlockSpec((1, PAGE_SIZE, IDX_DIM), functools.partial(
        lambda b, pt, p: (pt[b, p], 0, 0), p=p)) for p in range(n_pages)]
    grid_spec = pltpu.PrefetchScalarGridSpec(
        num_scalar_prefetch=1,
        grid=(DB,),
        in_specs=page_specs + [
            pl.BlockSpec((1,) + qi_rows.shape[1:], lambda b, pt: (b, 0, 0)),
            pl.BlockSpec((1,) + wmat.shape[1:], lambda b, pt: (b, 0, 0)),
            pl.BlockSpec((1, n_tok, IDX_DIM), lambda b, pt: (b, 0, 0))],
        out_specs=pl.BlockSpec((DB * rows_pad, L), lambda b, pt: (0, 0)),
        scratch_shapes=[pltpu.VMEM((DB * rows_pad, L), I32)],
    )
    return pl.pallas_call(
        functools.partial(_dsa_sample_select_kernel, n_pages=n_pages, n_tok=n_tok, k_sel=k_sel,
                          rows_pad=rows_pad),
        out_shape=jax.ShapeDtypeStruct((DB * rows_pad, L), F32),
        grid_spec=grid_spec,
        compiler_params=_cparams(("arbitrary",)),
        name="dsa_sample_select",
    )(page_table, *([cache_ik_l] * n_pages), qi_rows, wmat, ki_new)


def _dsa_sample_attend_kernel(pt_ref, *refs, n_pages, n_tok, rows_pad):
    del pt_ref
    k_refs = refs[:n_pages]
    v_refs = refs[n_pages:2 * n_pages]
    q_ref, kn_ref, vn_ref, mask_ref, bias_ref, qg_ref, o_ref = refs[2 * n_pages:]
    rows = n_tok * A_HEADS
    L = mask_ref.shape[1]
    kw = A_KV_HEADS * HEAD_DIM
    pad = jnp.zeros((PAGE_SIZE - n_tok, kw), F32)
    k_all = jnp.concatenate([r[0] for r in k_refs] + [kn_ref[0], pad], axis=0).astype(BF16)
    v_all = jnp.concatenate([r[0] for r in v_refs] + [vn_ref[0], pad], axis=0).astype(BF16)
    q = (_rms(q_ref[0], qg_ref[...]) * HEAD_DIM ** -0.5).astype(BF16)
    rep = A_HEADS // A_KV_HEADS
    grp = (lax.broadcasted_iota(I32, (rows, 1), 0) % A_HEADS) // rep
    e_r = lax.broadcasted_iota(I32, (rows, rows_pad), 0) // A_HEADS
    e_c = lax.broadcasted_iota(I32, (rows, rows_pad), 1)
    expand = jnp.where(e_r == e_c, 1.0, 0.0).astype(BF16)
    sel = jnp.dot(expand, mask_ref[...].astype(BF16), preferred_element_type=F32) > 0.5
    logits = None
    for g in range(A_KV_HEADS):
        lg = lax.dot_general(q, k_all[:, g * HEAD_DIM:(g + 1) * HEAD_DIM], (((1,), (1,)), ((), ())),
                             preferred_element_type=F32)
        logits = lg if logits is None else jnp.where(grp == g, lg, logits)
    logits = jnp.where(sel, logits + bias_ref[...], NEG_BIG)
    m = jnp.max(logits, axis=-1, keepdims=True)
    p = jnp.exp(logits - m)
    den = jnp.sum(p, axis=-1, keepdims=True)
    pb = p.astype(BF16)
    o = None
    for g in range(A_KV_HEADS):
        og = jnp.dot(pb, v_all[:, g * HEAD_DIM:(g + 1) * HEAD_DIM], preferred_element_type=F32)
        o = og if o is None else jnp.where(grp == g, og, o)
    o_ref[0] = (o / den).astype(o_ref.dtype)


def dsa_sample_attend(cache_k_l, cache_v_l, page_table, q_rows, k_new, v_new, mask, bias_tab, q_norm_g):
    DB, n_pages = page_table.shape
    n_tok = k_new.shape[1]
    rows = n_tok * A_HEADS
    rows_pad = mask.shape[0] // DB
    L = mask.shape[1]
    kw = A_KV_HEADS * HEAD_DIM
    page_specs = [pl.BlockSpec((1, PAGE_SIZE, kw), functools.partial(
        lambda b, pt, p: (pt[b, p], 0, 0), p=p)) for p in range(n_pages)]
    grid_spec = pltpu.PrefetchScalarGridSpec(
        num_scalar_prefetch=1,
        grid=(DB,),
        in_specs=page_specs + page_specs + [
            pl.BlockSpec((1, rows, HEAD_DIM), lambda b, pt: (b, 0, 0)),
            pl.BlockSpec((1, n_tok, kw), lambda b, pt: (b, 0, 0)),
            pl.BlockSpec((1, n_tok, kw), lambda b, pt: (b, 0, 0)),
            pl.BlockSpec((rows_pad, L), lambda b, pt: (b, 0)),
            pl.BlockSpec((rows, L), lambda b, pt: (0, 0), pipeline_mode=pl.Buffered(1)),
            pl.BlockSpec((1, HEAD_DIM), lambda b, pt: (0, 0), pipeline_mode=pl.Buffered(1))],
        out_specs=pl.BlockSpec((1, rows, HEAD_DIM), lambda b, pt: (b, 0, 0)),
    )
    return pl.pallas_call(
        functools.partial(_dsa_sample_attend_kernel, n_pages=n_pages, n_tok=n_tok, rows_pad=rows_pad),
        out_shape=jax.ShapeDtypeStruct((DB, rows, HEAD_DIM), BF16),
        grid_spec=grid_spec,
        compiler_params=_cparams(("parallel",)),
        name="dsa_sample_attend",
    )(page_table, *([cache_k_l] * n_pages), *([cache_v_l] * n_pages), q_rows, k_new, v_new, mask,
      bias_tab, q_norm_g.reshape(1, HEAD_DIM))


def _log_sigmoid(z):
    return jnp.minimum(z, 0.0) - jnp.log(1.0 + jnp.exp(-jnp.abs(z)))


def _seg_masks(seg):
    r = lax.broadcasted_iota(I32, (TILE, TILE), 0)
    c = lax.broadcasted_iota(I32, (TILE, TILE), 1)
    return r, c, (r // seg) == (c // seg)


def _fdot(a, b):
    return jnp.dot(a, b, precision=HIGHEST, preferred_element_type=F32)


def _gla_common(qb_ref, kb_ref, misc_ref, wg_ref, bg_ref, seg):
    gb = misc_ref[:, MISC_GB:MISC_GB + GATE_RANK]
    z = _fdot(gb, wg_ref[...]) + bg_ref[...]
    la = _log_sigmoid(z) / GATE_TEMP
    q = qb_ref[...] * B_DK ** -0.5
    k = kb_ref[...]
    r, c, same_seg = _seg_masks(seg)
    att = [jnp.where(r == c, lax.dot_general(q[:, h * B_DK:(h + 1) * B_DK], k[:, h * B_DK:(h + 1) * B_DK],
                                             (((1,), (1,)), ((), ())), precision=HIGHEST,
                                             preferred_element_type=F32), 0.0)
           for h in range(B_HEADS)]
    w = seg // 2
    while w >= 1:
        same = (r // (2 * w)) == (c // (2 * w))
        r_right = (r % (2 * w)) >= w
        c_right = (c % (2 * w)) >= w
        m_mat = jnp.where(same & r_right & c_right & (c <= r), 1.0, 0.0)
        n_mat = jnp.where(same & (~r_right) & (~c_right) & (c > r), 1.0, 0.0)
        pair = same & r_right & (~c_right)
        qd = q * jnp.exp(_fdot(m_mat, la))
        kd = k * jnp.exp(_fdot(n_mat, la))
        for h in range(B_HEADS):
            hs = slice(h * B_DK, (h + 1) * B_DK)
            a = lax.dot_general(qd[:, hs], kd[:, hs], (((1,), (1,)), ((), ())), precision=HIGHEST,
                                preferred_element_type=F32)
            att[h] = att[h] + jnp.where(pair, a, 0.0)
        w //= 2
    tri = jnp.where(same_seg & (c <= r), 1.0, 0.0)
    suf = jnp.where(same_seg & (c > r), 1.0, 0.0)
    b_cum = _fdot(tri, la)
    rem = _fdot(suf, la)
    return q, k, att, b_cum, rem


def _gla_finish(o_heads, rb_ref, go_ref, o_ref):
    go = go_ref[...]
    for h in range(B_HEADS):
        vs = slice(h * B_DV, (h + 1) * B_DV)
        rb = rb_ref[:, vs]
        o_ref[:, vs] = (_rms(o_heads[h], go) * (rb * jax.nn.sigmoid(rb))).astype(o_ref.dtype)


def _gla_prompt_kernel(qb_ref, kb_ref, vb_ref, rb_ref, misc_ref, wg_ref, bg_ref, go_ref,
                       o_ref, s_ref, state_ref):
    ci = pl.program_id(1)

    @pl.when(ci == 0)
    def _():
        state_ref[...] = jnp.zeros_like(state_ref)

    q, k, att, b_cum, rem = _gla_common(qb_ref, kb_ref, misc_ref, wg_ref, bg_ref, TILE)
    v = vb_ref[...]
    state = state_ref[...]
    qe = q * jnp.exp(b_cum)
    o_heads = []
    for h in range(B_HEADS):
        ks = slice(h * B_DK, (h + 1) * B_DK)
        vs = slice(h * B_DV, (h + 1) * B_DV)
        o_heads.append(_fdot(qe[:, ks], state[ks, :]) + _fdot(att[h], v[:, vs]))
    _gla_finish(o_heads, rb_ref, go_ref, o_ref)

    ke_t = (k * jnp.exp(rem)).T
    e_last = jnp.exp(b_cum[TILE - 1:TILE, :])
    e_col = jnp.broadcast_to(e_last, (TILE, B_HEADS * B_DK)).T[:, 0:1]
    upd = jnp.concatenate(
        [_fdot(ke_t[h * B_DK:(h + 1) * B_DK, :], v[:, h * B_DV:(h + 1) * B_DV]) for h in range(B_HEADS)], axis=0)
    new_state = state * e_col + upd
    state_ref[...] = new_state
    s_ref[0] = new_state


def gla_prompt(proj, w_gate, b_gate, g_out, n_batch, seq):
    T = proj.shape[0]
    nc = seq // TILE
    kwid = B_HEADS * B_DK
    vwid = B_HEADS * B_DV
    o, s = pl.pallas_call(
        _gla_prompt_kernel,
        out_shape=(jax.ShapeDtypeStruct((T, vwid), BF16),
                   jax.ShapeDtypeStruct((n_batch, kwid, B_DV), F32)),
        grid=(n_batch, nc),
        in_specs=[pl.BlockSpec((TILE, kwid), lambda b, c: (b * nc + c, QB_OFF // kwid)),
                  pl.BlockSpec((TILE, kwid), lambda b, c: (b * nc + c, KB_OFF // kwid)),
                  pl.BlockSpec((TILE, vwid), lambda b, c: (b * nc + c, VB_OFF // vwid)),
                  pl.BlockSpec((TILE, vwid), lambda b, c: (b * nc + c, RB_OFF // vwid)),
                  pl.BlockSpec((TILE, LANES), lambda b, c: (b * nc + c, MISC_OFF // LANES)),
                  _resident((GATE_RANK, kwid)),
                  _resident((1, kwid)),
                  _resident((1, B_DV))],
        out_specs=(pl.BlockSpec((TILE, vwid), lambda b, c: (b * nc + c, 0)),
                   pl.BlockSpec((1, kwid, B_DV), lambda b, c: (b, 0, 0))),
        scratch_shapes=[pltpu.VMEM((kwid, B_DV), F32)],
        compiler_params=_cparams(("parallel", "arbitrary")),
        name="gla_prompt",
    )(proj, proj, proj, proj, proj, w_gate, b_gate.reshape(1, kwid), g_out.reshape(1, B_DV))
    return o, s.reshape(n_batch, B_HEADS, B_DK, B_DV)


def _gla_sample_kernel(qb_ref, kb_ref, vb_ref, rb_ref, misc_ref, wg_ref, bg_ref, go_ref, s0_ref,
                       o_ref, s_ref, *, seg):
    nbt = TILE // seg
    q, k, att, b_cum, rem = _gla_common(qb_ref, kb_ref, misc_ref, wg_ref, bg_ref, seg)
    v = vb_ref[...]
    qe = q * jnp.exp(b_cum)
    ke = k * jnp.exp(rem)
    r1 = lax.broadcasted_iota(I32, (TILE, 1), 0)
    e_last = jnp.where(r1 % seg == seg - 1, jnp.exp(b_cum), 0.0)
    wide = nbt * B_DK
    mq = (lax.broadcasted_iota(I32, (TILE, wide), 0) // seg) == (lax.broadcasted_iota(I32, (TILE, wide), 1) // B_DK)
    mk = (lax.broadcasted_iota(I32, (wide, TILE), 0) // B_DK) == (lax.broadcasted_iota(I32, (wide, TILE), 1) // seg)
    o_heads = []
    for h in range(B_HEADS):
        ks = slice(h * B_DK, (h + 1) * B_DK)
        vs = slice(h * B_DV, (h + 1) * B_DV)
        state = s0_ref[:, h].reshape(wide, B_DV)
        q_bd = jnp.where(mq, jnp.concatenate([qe[:, ks]] * nbt, axis=1), 0.0)
        o_heads.append(_fdot(q_bd, state) + _fdot(att[h], v[:, vs]))
        pair_t = jnp.concatenate([ke[:, ks], e_last[:, ks]], axis=1).T
        k_bd = jnp.where(mk, jnp.concatenate([pair_t[:B_DK]] * nbt, axis=0), 0.0)
        e_bd = jnp.where(mk, jnp.concatenate([pair_t[B_DK:]] * nbt, axis=0), 0.0)
        e_col = jnp.sum(e_bd, axis=-1, keepdims=True)
        new_state = state * e_col + _fdot(k_bd, v[:, vs])
        s_ref[:, h] = new_state.reshape(nbt, B_DK, B_DV)
    _gla_finish(o_heads, rb_ref, go_ref, o_ref)


def gla_sample(proj, w_gate, b_gate, g_out, s0, n_tok):
    T = proj.shape[0]
    DB = s0.shape[0]
    nbt = TILE // n_tok
    kwid = B_HEADS * B_DK
    vwid = B_HEADS * B_DV
    return pl.pallas_call(
        functools.partial(_gla_sample_kernel, seg=n_tok),
        out_shape=(jax.ShapeDtypeStruct((T, vwid), BF16),
                   jax.ShapeDtypeStruct(s0.shape, F32)),
        grid=(T // TILE,),
        in_specs=[pl.BlockSpec((TILE, kwid), lambda i: (i, QB_OFF // kwid)),
                  pl.BlockSpec((TILE, kwid), lambda i: (i, KB_OFF // kwid)),
                  pl.BlockSpec((TILE, vwid), lambda i: (i, VB_OFF // vwid)),
                  pl.BlockSpec((TILE, vwid), lambda i: (i, RB_OFF // vwid)),
                  pl.BlockSpec((TILE, LANES), lambda i: (i, MISC_OFF // LANES)),
                  _resident((GATE_RANK, kwid)),
                  _resident((1, kwid)),
                  _resident((1, B_DV)),
                  pl.BlockSpec((nbt, B_HEADS, B_DK, B_DV), lambda i: (i, 0, 0, 0))],
        out_specs=(pl.BlockSpec((TILE, vwid), lambda i: (i, 0)),
                   pl.BlockSpec((nbt, B_HEADS, B_DK, B_DV), lambda i: (i, 0, 0, 0))),
        compiler_params=_cparams(("parallel",)),
        name="gla_sample",
    )(proj, proj, proj, proj, proj, w_gate, b_gate.reshape(1, kwid), g_out.reshape(1, B_DV), s0)


def _gelu(x):
    return jax.nn.gelu(x)


def _gmlp_kernel(uc_ref, vc_ref, gv_ref, ws_ref, bcol_ref, o_ref, vn_ref, *, seg):
    r, c, same_seg = _seg_masks(seg)
    keep = same_seg & (c <= r)
    u = _gelu(uc_ref[...])
    vg = _gelu(vc_ref[...])
    for g in range(C_GROUPS):
        gs = slice(g * C_GROUP_DIM, (g + 1) * C_GROUP_DIM)
        vn = _rms(vg[:, gs], gv_ref[:, gs])
        vn_ref[:, gs] = vn
        w = jnp.where(keep, ws_ref[g], 0.0).astype(BF16)
        s = jnp.dot(w, vn.astype(BF16), preferred_element_type=F32) + bcol_ref[:, g:g + 1]
        o_ref[:, gs] = (u[:, gs] * s).astype(o_ref.dtype)


def gmlp(proj, g_v, w_tiles, b_cols, seg):
    T = proj.shape[0]
    cw = C_GROUPS * C_GROUP_DIM
    return pl.pallas_call(
        functools.partial(_gmlp_kernel, seg=seg),
        out_shape=(jax.ShapeDtypeStruct((T, cw), BF16),
                   jax.ShapeDtypeStruct((T, cw), F32)),
        grid=(T // TILE,),
        in_specs=[pl.BlockSpec((TILE, cw), lambda i: (i, UC_OFF // cw)),
                  pl.BlockSpec((TILE, cw), lambda i: (i, VC_OFF // cw)),
                  _resident((1, cw)),
                  _resident((C_GROUPS, TILE, TILE)),
                  _resident((TILE, C_GROUPS))],
        out_specs=(pl.BlockSpec((TILE, cw), lambda i: (i, 0)),
                   pl.BlockSpec((TILE, cw), lambda i: (i, 0))),
        compiler_params=_cparams(("parallel",)),
        name="gmlp",
    )(proj, proj, g_v.reshape(1, cw), w_tiles, b_cols)


def _out_kernel(h_ref, oa_ref, ob_ref, oc_ref, w_ref, o_ref):
    aw = oa_ref.shape[1]
    bw = ob_ref.shape[1]
    acc = jnp.dot(oa_ref[...], w_ref[0:aw, :], preferred_element_type=F32)
    acc = acc + jnp.dot(ob_ref[...], w_ref[aw:aw + bw, :], preferred_element_type=F32)
    acc = acc + jnp.dot(oc_ref[...], w_ref[aw + bw:, :], preferred_element_type=F32)
    o_ref[...] = h_ref[...] + acc


def out_projection(h, o_a, o_b, o_c, w_out, tm):
    T, D = h.shape
    tm = min(tm, T)
    return pl.pallas_call(
        _out_kernel,
        out_shape=jax.ShapeDtypeStruct((T, D), F32),
        grid=(T // tm,),
        in_specs=[pl.BlockSpec((tm, D), lambda i: (i, 0)),
                  pl.BlockSpec((tm, o_a.shape[1]), lambda i: (i, 0)),
                  pl.BlockSpec((tm, o_b.shape[1]), lambda i: (i, 0)),
                  pl.BlockSpec((tm, o_c.shape[1]), lambda i: (i, 0)),
                  _resident(w_out.shape)],
        out_specs=pl.BlockSpec((tm, D), lambda i: (i, 0)),
        compiler_params=_cparams(("parallel",)),
        name="out_projection",
    )(h, o_a, o_b, o_c, w_out)


def _ffn_kernel(h_ref, g_ref, wg_ref, wu_ref, wd_ref, o_ref, n_ref):
    j = pl.program_id(1)

    @pl.when(j == 0)
    def _():
        h = h_ref[...]
        n_ref[...] = _rms(h, g_ref[...]).astype(BF16)
        o_ref[...] = h

    n = n_ref[...]
    a = jnp.dot(n, wg_ref[...], preferred_element_type=F32)
    u = jnp.dot(n, wu_ref[...], preferred_element_type=F32)
    act = (a * jax.nn.sigmoid(a) * u).astype(BF16)
    o_ref[...] += jnp.dot(act, wd_ref[...], preferred_element_type=F32)


def ffn(h, g, w_gate, w_up, w_down, tm, tf):
    T, D = h.shape
    tm = min(tm, T)
    FF = w_gate.shape[1]
    return pl.pallas_call(
        _ffn_kernel,
        out_shape=jax.ShapeDtypeStruct((T, D), F32),
        grid=(T // tm, FF // tf),
        in_specs=[pl.BlockSpec((tm, D), lambda i, j: (i, 0)),
                  _resident((1, D)),
                  pl.BlockSpec((D, tf), lambda i, j: (0, j)),
                  pl.BlockSpec((D, tf), lambda i, j: (0, j)),
                  pl.BlockSpec((tf, D), lambda i, j: (j, 0))],
        out_specs=pl.BlockSpec((tm, D), lambda i, j: (i, 0)),
        scratch_shapes=[pltpu.VMEM((tm, D), BF16)],
        compiler_params=_cparams(("parallel", "arbitrary")),
        name="ffn",
    )(h, g.reshape(1, D), w_gate, w_up, w_down)


def _ple_kernel(h_ref, p_ref, g_ref, wgate_ref, wproj_ref, o_ref):
    h = h_ref[...]
    n = _rms(h, g_ref[...]).astype(BF16)
    gate = jax.nn.sigmoid(jnp.dot(n, wgate_ref[...], preferred_element_type=F32))
    emb = jnp.dot(p_ref[...].astype(BF16), wproj_ref[...], preferred_element_type=F32)
    o_ref[...] = h + gate * emb


def ple(h, p, g, w_gate, w_proj, tm):
    T, D = h.shape
    tm = min(tm, T)
    P = p.shape[1]
    return pl.pallas_call(
        _ple_kernel,
        out_shape=jax.ShapeDtypeStruct((T, D), F32),
        grid=(T // tm,),
        in_specs=[pl.BlockSpec((tm, D), lambda i: (i, 0)),
                  pl.BlockSpec((tm, P), lambda i: (i, 0)),
                  _resident((1, D)),
                  _resident(w_gate.shape),
                  _resident(w_proj.shape)],
        out_specs=pl.BlockSpec((tm, D), lambda i: (i, 0)),
        compiler_params=_cparams(("parallel",)),
        name="ple",
    )(h, p, g.reshape(1, D), w_gate, w_proj)


def _pack_w_in(w):
    sizes = (1024, 256, 256, 1024, 64, 16, 256, 256, 512, 16, 512, 512, 512)
    names = ("q", "k", "v", "qi", "ki", "wi", "qb", "kb", "vb", "gb", "rb", "uc", "vc")
    seg, start = {}, 0
    for nme, sz in zip(names, sizes):
        seg[nme] = w[:, start:start + sz]
        start += sz
    pad = jnp.zeros((w.shape[0], LANES - IDX_DIM - IDX_HEADS - GATE_RANK), w.dtype)
    order = ("q", "qi", "vb", "rb", "uc", "vc", "k", "v", "qb", "kb", "ki", "wi", "gb")
    return jnp.concatenate([seg[n] for n in order] + [pad], axis=1).astype(BF16)


def _mixer_tail(h, o_a, o_b, o_c, p_l, lw, tm, tm_ffn):
    h = out_projection(h, o_a, o_b, o_c, lw["w_out"], tm)
    h = ffn(h, lw["g_ffn"], lw["w_ffn_gate"], lw["w_ffn_up"], lw["w_ffn_down"], tm_ffn, 512)
    return ple(h, p_l, lw["g_ple"], lw["w_ple_gate"], lw["w_ple_proj"], tm)


def kernel(x_prompt, x_sample, cache_k, cache_v, cache_idx_k, state_gla, page_table, p_prompt, p_sample,
           g_mix, w_in, q_norm_g, k_norm_g, rel_bias, w_gate_b, b_gate_b, g_out_b, g_v_c, w_spatial,
           b_spatial, w_out, g_ffn, w_ffn_gate, w_ffn_up, w_ffn_down, g_ple, w_ple_gate, w_ple_proj):
    n_batch, seq, d_model = x_prompt.shape
    dec_batch, dec_seq, _ = x_sample.shape
    depth = w_in.shape[0]
    n_pages = page_table.shape[1]
    past = n_pages * PAGE_SIZE
    n_pool = cache_k.shape[1]
    kw = A_KV_HEADS * HEAD_DIM
    tp, ts = n_batch * seq, dec_batch * dec_seq
    rows_pad = SUBLANES
    l_sample = past + PAGE_SIZE
    k_sel_s = min(TOPK_MAX, (past + dec_seq) // 4)

    bias_p = bias_table_prompt(rel_bias)
    bias_s = bias_table_sample(rel_bias, past, dec_seq, l_sample)

    hp = x_prompt.reshape(tp, d_model)
    hs = x_sample.reshape(ts, d_model)
    outs = {k: [] for k in ("kp", "vp", "ikp", "sp", "ks", "vs", "iks", "ss", "cs")}
    eye_t = jnp.eye(rows_pad, dec_seq, dtype=F32)
    for i in range(depth):
        lw = dict(w_out=w_out[i].astype(BF16), g_ffn=g_ffn[i], w_ffn_gate=w_ffn_gate[i].astype(BF16),
                  w_ffn_up=w_ffn_up[i].astype(BF16), w_ffn_down=w_ffn_down[i].astype(BF16),
                  g_ple=g_ple[i], w_ple_gate=w_ple_gate[i].astype(BF16),
                  w_ple_proj=w_ple_proj[i].astype(BF16))
        w_packed = _pack_w_in(w_in[i])
        b_cols_p = b_spatial[i].T
        reps = TILE // dec_seq
        w_tiles_s = jnp.tile(w_spatial[i][:, :dec_seq, :dec_seq], (1, reps, reps))
        b_cols_s = jnp.tile(b_spatial[i][:, :dec_seq].T, (reps, 1))

        proj = in_projection(hp, g_mix[i], w_packed, 256)
        kn, vv, ik = kv_post(proj, k_norm_g[i], 512)
        o_a = dsa_prompt(proj, kn, bias_p, q_norm_g[i], n_batch, seq)
        o_b, s_p = gla_prompt(proj, w_gate_b[i], b_gate_b[i], g_out_b[i], n_batch, seq)
        o_c, _ = gmlp(proj, g_v_c[i], w_spatial[i], b_cols_p, TILE)
        hp = _mixer_tail(hp, o_a, o_b, o_c, p_prompt[i].reshape(tp, -1), lw, 256, 512)
        outs["kp"].append(kn.reshape(n_batch, seq, A_KV_HEADS, HEAD_DIM))
        outs["vp"].append(vv.reshape(n_batch, seq, A_KV_HEADS, HEAD_DIM))
        outs["ikp"].append(ik.reshape(n_batch, seq, IDX_DIM))
        outs["sp"].append(s_p)

        proj = in_projection(hs, g_mix[i], w_packed, 256)
        kn, vv, ik = kv_post(proj, k_norm_g[i], 512)
        qi_rows = proj[:, QI_OFF:QI_OFF + IDX_HEADS * IDX_DIM].reshape(dec_batch, dec_seq * IDX_HEADS, IDX_DIM)
        wi = proj[:, MISC_OFF + MISC_WI:MISC_OFF + MISC_WI + IDX_HEADS].reshape(dec_batch, dec_seq, IDX_HEADS)
        wi = wi * (IDX_HEADS ** -0.5 * IDX_DIM ** -0.5)
        wmat = (eye_t[None, :, :, None] * wi[:, None, :, :]).reshape(dec_batch, rows_pad, dec_seq * IDX_HEADS)
        mask = dsa_sample_select(cache_idx_k[i], page_table, qi_rows, wmat,
                                 ik.reshape(dec_batch, dec_seq, IDX_DIM), k_sel_s)
        q_rows = proj[:, Q_OFF:Q_OFF + A_HEADS * HEAD_DIM].reshape(dec_batch, dec_seq * A_HEADS, HEAD_DIM)
        o_a = dsa_sample_attend(cache_k[i].reshape(n_pool, PAGE_SIZE, kw), cache_v[i].reshape(n_pool, PAGE_SIZE, kw),
                                page_table, q_rows, kn.reshape(dec_batch, dec_seq, kw),
                                vv.reshape(dec_batch, dec_seq, kw), mask, bias_s, q_norm_g[i])
        o_a = o_a.reshape(ts, A_HEADS * HEAD_DIM)
        o_b, s_s = gla_sample(proj, w_gate_b[i], b_gate_b[i], g_out_b[i], state_gla[i], dec_seq)
        o_c, vn = gmlp(proj, g_v_c[i], w_tiles_s, b_cols_s, dec_seq)
        hs = _mixer_tail(hs, o_a, o_b, o_c, p_sample[i].reshape(ts, -1), lw, 256, 512)
        outs["ks"].append(kn.reshape(dec_batch, dec_seq, A_KV_HEADS, HEAD_DIM))
        outs["vs"].append(vv.reshape(dec_batch, dec_seq, A_KV_HEADS, HEAD_DIM))
        outs["iks"].append(ik.reshape(dec_batch, dec_seq, IDX_DIM))
        outs["ss"].append(s_s)
        outs["cs"].append(vn.reshape(dec_batch, dec_seq, -1))

    st = {k: jnp.stack(v) for k, v in outs.items()}
    return (hp.reshape(n_batch, seq, d_model), hs.reshape(dec_batch, dec_seq, d_model),
            st["kp"], st["vp"], st["ikp"], st["sp"], st["ks"], st["vs"], st["iks"], st["ss"], st["cs"])
```

```python
import functools
import math

import jax
import jax.numpy as jnp
from jax import lax
from jax.experimental import pallas as pl
from jax.experimental.pallas import tpu as pltpu

F32 = jnp.float32
BF16 = jnp.bfloat16
I32 = jnp.int32
HIGHEST = lax.Precision.HIGHEST

LANES = 128
SUBLANES = 8
VMEM_LIMIT = 56 * 1024 * 1024

HEAD_DIM = 128
A_HEADS = 8
A_KV_HEADS = 2
IDX_HEADS = 16
IDX_DIM = 64
TOPK_MAX = 256
NUM_BUCKETS = 32
MAX_DISTANCE = 128
B_HEADS = 4
B_DK = 64
B_DV = 128
GATE_RANK = 16
GATE_TEMP = 16.0
C_GROUPS = 4
C_GROUP_DIM = 128
PAGE_SIZE = 128
EPS = 1e-6
NEG_BIG = -1e30
INT_MIN = -(2 ** 31)

TILE = 128
QBLK = 256

Q_OFF, QI_OFF, VB_OFF, RB_OFF, UC_OFF, VC_OFF = 0, 1024, 2048, 2560, 3072, 3584
K_OFF, V_OFF, QB_OFF, KB_OFF, MISC_OFF = 4096, 4352, 4608, 4864, 5120
PROJ_PACKED = 5248
MISC_KI, MISC_WI, MISC_GB = 0, 64, 80


def _cparams(sem):
    return pltpu.CompilerParams(dimension_semantics=sem, vmem_limit_bytes=VMEM_LIMIT)


def _rms(x, g):
    return x * lax.rsqrt(jnp.mean(x * x, axis=-1, keepdims=True) + EPS) * g


def _resident(shape):
    nd = len(shape)
    return pl.BlockSpec(shape, lambda *_: (0,) * nd, pipeline_mode=pl.Buffered(1))


def _proj_kernel(x_ref, g_ref, w_ref, o_ref):
    n = _rms(x_ref[...], g_ref[...]).astype(BF16)
    ncol = o_ref.shape[1]
    step = 512
    for c0 in range(0, ncol, step):
        c1 = min(c0 + step, ncol)
        o_ref[:, c0:c1] = jnp.dot(n, w_ref[:, c0:c1], preferred_element_type=F32)


def in_projection(h, g, w_packed, tm):
    T, D = h.shape
    tm = min(tm, T)
    N = w_packed.shape[1]
    return pl.pallas_call(
        _proj_kernel,
        out_shape=jax.ShapeDtypeStruct((T, N), F32),
        grid=(T // tm,),
        in_specs=[pl.BlockSpec((tm, D), lambda i: (i, 0)),
                  _resident((1, D)),
                  _resident((D, N))],
        out_specs=pl.BlockSpec((tm, N), lambda i: (i, 0)),
        compiler_params=_cparams(("parallel",)),
        name="in_projection",
    )(h, g.reshape(1, D), w_packed)


def _kv_kernel(k_ref, v_ref, m_ref, g_ref, ko_ref, vo_ref, io_ref):
    g = g_ref[...]
    k = k_ref[...]
    for hh in range(A_KV_HEADS):
        sl = slice(hh * HEAD_DIM, (hh + 1) * HEAD_DIM)
        ko_ref[:, sl] = _rms(k[:, sl], g)
    vo_ref[...] = v_ref[...]
    io_ref[...] = m_ref[:, MISC_KI:MISC_KI + IDX_DIM]


def kv_post(proj, k_norm_g, tm):
    T = proj.shape[0]
    tm = min(tm, T)
    kw = A_KV_HEADS * HEAD_DIM
    return pl.pallas_call(
        _kv_kernel,
        out_shape=(jax.ShapeDtypeStruct((T, kw), F32),
                   jax.ShapeDtypeStruct((T, kw), F32),
                   jax.ShapeDtypeStruct((T, IDX_DIM), F32)),
        grid=(T // tm,),
        in_specs=[pl.BlockSpec((tm, kw), lambda i: (i, K_OFF // kw)),
                  pl.BlockSpec((tm, kw), lambda i: (i, V_OFF // kw)),
                  pl.BlockSpec((tm, LANES), lambda i: (i, MISC_OFF // LANES)),
                  _resident((1, HEAD_DIM))],
        out_specs=(pl.BlockSpec((tm, kw), lambda i: (i, 0)),
                   pl.BlockSpec((tm, kw), lambda i: (i, 0)),
                   pl.BlockSpec((tm, IDX_DIM), lambda i: (i, 0))),
        compiler_params=_cparams(("parallel",)),
        name="kv_post",
    )(proj, proj, proj, k_norm_g.reshape(1, HEAD_DIM))


def _bucket(dist):
    n = jnp.maximum(dist, 0)
    max_exact = NUM_BUCKETS // 2
    large = max_exact + (jnp.log(jnp.maximum(n, 1).astype(F32) / max_exact)
                         / math.log(MAX_DISTANCE / max_exact)
                         * (NUM_BUCKETS - max_exact)).astype(I32)
    large = jnp.minimum(large, NUM_BUCKETS - 1)
    return jnp.where(n < max_exact, n, large)


def _bias_prompt_kernel(rb_ref, o_ref):
    t = lax.broadcasted_iota(I32, (TILE, TILE), 0)
    c = lax.broadcasted_iota(I32, (TILE, TILE), 1)
    for z in range(3):
        bucket = _bucket(t - c + (2 - z) * TILE)
        for h in range(A_HEADS):
            acc = jnp.zeros((TILE, TILE), F32)
            for b in range(NUM_BUCKETS):
                acc = jnp.where(bucket == b, rb_ref[b, h], acc)
            o_ref[h, z] = acc


def bias_table_prompt(rel_bias):
    return pl.pallas_call(
        _bias_prompt_kernel,
        out_shape=jax.ShapeDtypeStruct((A_HEADS, 3, TILE, TILE), F32),
        in_specs=[pl.BlockSpec(memory_space=pltpu.SMEM)],
        out_specs=pl.BlockSpec(memory_space=pltpu.VMEM),
        name="bias_table_prompt",
    )(rel_bias)


def _bias_sample_kernel(rbrows_ref, o_ref, *, past, n_tok):
    rows, L = o_ref.shape
    r = lax.broadcasted_iota(I32, (rows, L), 0)
    s = lax.broadcasted_iota(I32, (rows, L), 1)
    bucket = _bucket(past + r // A_HEADS - s)
    rbrows = rbrows_ref[...]
    acc = jnp.zeros((rows, L), F32)
    for b in range(NUM_BUCKETS):
        acc = jnp.where(bucket == b, rbrows[:, b:b + 1], acc)
    o_ref[...] = acc


def bias_table_sample(rel_bias, past, n_tok, L):
    rows = n_tok * A_HEADS
    rbrows = jnp.tile(rel_bias.T, (n_tok, 1))
    return pl.pallas_call(
        functools.partial(_bias_sample_kernel, past=past, n_tok=n_tok),
        out_shape=jax.ShapeDtypeStruct((rows, L), F32),
        name="bias_table_sample",
    )(rbrows)


def _sortable_key(x):
    b = lax.bitcast_convert_type(x, I32)
    return b ^ ((b >> 31) & 0x7FFFFFFF)


def _topk_member(skey_ref, k_sel):
    R, L = skey_ref.shape

    def body(it, ans):
        bit = 31 - it
        cand = ans | lax.shift_left(jnp.int32(1), bit)
        cand_s = cand ^ INT_MIN
        cnt = jnp.sum(jnp.where(skey_ref[...] >= cand_s, 1.0, 0.0), axis=-1, keepdims=True)
        return jnp.where(cnt >= k_sel, cand, ans)

    ans = lax.fori_loop(0, 32, body, jnp.zeros((R, 1), I32))
    tau = ans ^ INT_MIN
    skey = skey_ref[...]
    gt = skey > tau
    eq = skey == tau
    n_gt = jnp.sum(jnp.where(gt, 1.0, 0.0), axis=-1, keepdims=True)
    room = k_sel - n_gt
    r_i = lax.broadcasted_iota(I32, (LANES, LANES), 0)
    c_i = lax.broadcasted_iota(I32, (LANES, LANES), 1)
    upper = jnp.where(r_i <= c_i, 1.0, 0.0).astype(BF16)
    off = jnp.zeros((R, 1), F32)
    parts = []
    for j in range(L // LANES):
        sl = slice(j * LANES, (j + 1) * LANES)
        eq_j = eq[:, sl]
        run = jnp.dot(jnp.where(eq_j, 1.0, 0.0).astype(BF16), upper, preferred_element_type=F32) + off
        parts.append(gt[:, sl] | (eq_j & (run <= room)))
        off = run[:, LANES - 1:LANES]
    return jnp.concatenate(parts, axis=1)


def _dsa_prompt_kernel(q_ref, qi_ref, misc_ref, kn_ref, v_ref, bias_ref, qg_ref, o_ref,
                       qst_ref, wst_ref, skey_ref, madd_ref, lg_ref, mx_ref, sm_ref, acc_ref, *, k_sel):
    i = pl.program_id(1)
    nkb = i + 1
    sub = QBLK // TILE
    rep = A_HEADS // A_KV_HEADS
    row0 = pl.multiple_of(i * QBLK, QBLK)
    r_iota = lax.broadcasted_iota(I32, (QBLK, QBLK), 0)
    c_iota = lax.broadcasted_iota(I32, (QBLK, QBLK), 1)

    def admissible(j):
        return (j * QBLK + c_iota) <= (row0 + r_iota)

    wi = misc_ref[pl.ds(row0, QBLK), MISC_WI:MISC_WI + IDX_HEADS] * (IDX_HEADS ** -0.5 * IDX_DIM ** -0.5)
    for h in range(IDX_HEADS):
        qst_ref[h * QBLK:(h + 1) * QBLK, :] = qi_ref[:, h * IDX_DIM:(h + 1) * IDX_DIM].astype(BF16)
        wst_ref[h * QBLK:(h + 1) * QBLK, :] = jnp.broadcast_to(wi[:, h:h + 1], (QBLK, QBLK))

    def score_body(j, carry):
        k0 = pl.multiple_of(j * QBLK, QBLK)
        kj = misc_ref[pl.ds(k0, QBLK), MISC_KI:MISC_KI + IDX_DIM].astype(BF16)
        s = lax.dot_general(qst_ref[...], kj, (((1,), (1,)), ((), ())), preferred_element_type=F32)
        s = jnp.maximum(s, 0.0) * wst_ref[...]
        score = jnp.sum(s.reshape(IDX_HEADS, QBLK, QBLK), axis=0)
        skey_ref[j] = _sortable_key(jnp.where(admissible(j), score, -jnp.inf))
        return carry

    lax.fori_loop(0, nkb, score_body, 0)

    def count(pred_fn):
        def body(j, acc):
            return acc + jnp.where(pred_fn(skey_ref[j]), 1.0, 0.0)
        acc = lax.fori_loop(0, nkb, body, jnp.zeros((QBLK, QBLK), F32))
        return jnp.sum(acc, axis=-1, keepdims=True)

    def bit_body(it, ans):
        cand = ans | lax.shift_left(jnp.int32(1), 31 - it)
        cand_b = jnp.broadcast_to(cand ^ INT_MIN, (QBLK, QBLK))
        cnt = count(lambda key: key >= cand_b)
        return jnp.where(cnt >= k_sel, cand, ans)

    ans = lax.fori_loop(0, 32, bit_body, jnp.zeros((QBLK, 1), I32))
    tau = jnp.broadcast_to(ans ^ INT_MIN, (QBLK, QBLK))
    room = k_sel - count(lambda key: key > tau)
    upper = jnp.where(r_iota <= c_iota, 1.0, 0.0).astype(BF16)

    def mask_body(j, off):
        key = skey_ref[j]
        eq = key == tau
        run = jnp.dot(jnp.where(eq, 1.0, 0.0).astype(BF16), upper, preferred_element_type=F32) + off
        sel = ((key > tau) | (eq & (run <= room))) & admissible(j)
        madd_ref[j] = jnp.where(sel, 0.0, NEG_BIG)
        return run[:, QBLK - 1:QBLK]

    lax.fori_loop(0, nkb, mask_body, jnp.zeros((QBLK, 1), F32))

    qg = qg_ref[...]
    for g in range(A_KV_HEADS):
        gs = slice(g * HEAD_DIM, (g + 1) * HEAD_DIM)
        heads = range(g * rep, (g + 1) * rep)
        q_stack = jnp.concatenate(
            [(_rms(q_ref[:, h * HEAD_DIM:(h + 1) * HEAD_DIM], qg) * HEAD_DIM ** -0.5).astype(BF16) for h in heads],
            axis=0)
        mx_ref[...] = jnp.full(mx_ref.shape, NEG_BIG, F32)

        def logit_body(j, carry):
            k0 = pl.multiple_of(j * QBLK, QBLK)
            kj = kn_ref[pl.ds(k0, QBLK), gs].astype(BF16)
            lg = lax.dot_general(q_stack, kj, (((1,), (1,)), ((), ())), preferred_element_type=F32)
            rows = []
            for h in heads:
                quads = []
                for u in range(sub):
                    quads.append(jnp.concatenate(
                        [bias_ref[h, jnp.clip(2 - ((i - j) * sub + u - c), 0, 2)] for c in range(sub)], axis=1))
                rows.append(jnp.concatenate(quads, axis=0))
            bias = jnp.stack(rows, axis=0)
            lg = lg.reshape(rep, QBLK, QBLK) + bias + madd_ref[j][None]
            lg_ref[j] = lg
            mx_ref[...] = jnp.maximum(mx_ref[...], lg)
            return carry

        lax.fori_loop(0, nkb, logit_body, 0)
        m = jnp.max(mx_ref[...], axis=-1, keepdims=True)
        sm_ref[...] = jnp.zeros(sm_ref.shape, F32)
        acc_ref[...] = jnp.zeros(acc_ref.shape, F32)

        def pv_body(j, carry):
            k0 = pl.multiple_of(j * QBLK, QBLK)
            p = jnp.exp(lg_ref[j] - m)
            sm_ref[...] += p
            vj = v_ref[pl.ds(k0, QBLK), gs].astype(BF16)
            acc_ref[...] += jnp.dot(p.reshape(rep * QBLK, QBLK).astype(BF16), vj, preferred_element_type=F32)
            return carry

        lax.fori_loop(0, nkb, pv_body, 0)
        den = jnp.sum(sm_ref[...], axis=-1, keepdims=True)
        o = acc_ref[...].reshape(rep, QBLK, HEAD_DIM) / den
        for r, h in enumerate(heads):
            o_ref[:, h * HEAD_DIM:(h + 1) * HEAD_DIM] = o[r].astype(o_ref.dtype)


def dsa_prompt(proj, kn, bias_tab, q_norm_g, n_batch, seq):
    T = proj.shape[0]
    nb = seq // QBLK
    k_sel = min(TOPK_MAX, seq // 4)
    aw = A_HEADS * HEAD_DIM
    iw = IDX_HEADS * IDX_DIM
    kw = A_KV_HEADS * HEAD_DIM
    rep = A_HEADS // A_KV_HEADS
    return pl.pallas_call(
        functools.partial(_dsa_prompt_kernel, k_sel=k_sel),
        out_shape=jax.ShapeDtypeStruct((T, aw), BF16),
        grid=(n_batch, nb),
        in_specs=[pl.BlockSpec((QBLK, aw), lambda b, i: (b * nb + i, Q_OFF // aw)),
                  pl.BlockSpec((QBLK, iw), lambda b, i: (b * nb + i, QI_OFF // iw)),
                  pl.BlockSpec((seq, LANES), lambda b, i: (b, MISC_OFF // LANES)),
                  pl.BlockSpec((seq, kw), lambda b, i: (b, 0)),
                  pl.BlockSpec((seq, kw), lambda b, i: (b, V_OFF // kw)),
                  _resident((A_HEADS, 3, TILE, TILE)),
                  _resident((1, HEAD_DIM))],
        out_specs=pl.BlockSpec((QBLK, aw), lambda b, i: (b * nb + i, 0)),
        scratch_shapes=[pltpu.VMEM((IDX_HEADS * QBLK, IDX_DIM), BF16),
                        pltpu.VMEM((IDX_HEADS * QBLK, QBLK), F32),
                        pltpu.VMEM((nb, QBLK, QBLK), I32),
                        pltpu.VMEM((nb, QBLK, QBLK), F32),
                        pltpu.VMEM((nb, rep, QBLK, QBLK), F32),
                        pltpu.VMEM((rep, QBLK, QBLK), F32),
                        pltpu.VMEM((rep, QBLK, QBLK), F32),
                        pltpu.VMEM((rep * QBLK, HEAD_DIM), F32)],
        compiler_params=_cparams(("parallel", "arbitrary")),
        name="dsa_prompt",
    )(proj, proj, proj, kn, proj, bias_tab, q_norm_g.reshape(1, HEAD_DIM))


def _dsa_sample_select_kernel(pt_ref, *refs, n_pages, n_tok, k_sel, rows_pad):
    del pt_ref
    page_refs = refs[:n_pages]
    qi_ref, wm_ref, kin_ref, mask_ref, skey_ref = refs[n_pages:]
    b = pl.program_id(0)
    nb = pl.num_programs(0)
    L = mask_ref.shape[1]
    past = n_pages * PAGE_SIZE

    qi = qi_ref[0]
    ki_new = jnp.concatenate(
        [kin_ref[0], jnp.zeros((PAGE_SIZE - n_tok, IDX_DIM), F32)], axis=0)
    pieces = []
    for p in range(n_pages + 1):
        kp = page_refs[p][...] if p < n_pages else ki_new
        pieces.append(jnp.maximum(_bdot_nt(qi, kp), 0.0))
    relu_s = jnp.concatenate(pieces, axis=1)
    score = jnp.dot(wm_ref[0], relu_s, precision=HIGHEST, preferred_element_type=F32)
    t_pos = past + lax.broadcasted_iota(I32, (rows_pad, L), 0)
    s_pos = lax.broadcasted_iota(I32, (rows_pad, L), 1)
    adm = s_pos <= t_pos
    r0 = pl.multiple_of(b * rows_pad, rows_pad)
    skey_ref[pl.ds(r0, rows_pad), :] = _sortable_key(jnp.where(adm, score, -jnp.inf))

    @pl.when(b == nb - 1)
    def _():
        n_blocks = skey_ref.shape[0] // TILE
        tp = past + lax.broadcasted_iota(I32, (TILE, L), 0) % rows_pad
        sp = lax.broadcasted_iota(I32, (TILE, L), 1)
        adm_blk = sp <= tp
        for rb in range(n_blocks):
            blk = skey_ref.at[rb * TILE:(rb + 1) * TILE, :]
            sel = _topk_member(blk, k_sel) & adm_blk
            mask_ref[rb * TILE:(rb + 1) * TILE, :] = jnp.where(sel, 1.0, 0.0)


def dsa_sample_select(cache_ik, layer, page_table, qi_rows, wmat, ki_new, k_sel):
    DB, n_pages = page_table.shape
    n_tok = ki_new.shape[1]
    rows_pad = wmat.shape[1]
    L = (n_pages + 1) * PAGE_SIZE
    page_specs = [pl.BlockSpec((None, None, PAGE_SIZE, IDX_DIM), functools.partial(
        lambda b, pt, p: (layer, pt[b, p], 0, 0), p=p)) for p in range(n_pages)]
    grid_spec = pltpu.PrefetchScalarGridSpec(
        num_scalar_prefetch=1,
        grid=(DB,),
        in_specs=page_specs + [
            pl.BlockSpec((1,) + qi_rows.shape[1:], lambda b, pt: (b, 0, 0)),
            pl.BlockSpec((1,) + wmat.shape[1:], lambda b, pt: (b, 0, 0)),
            pl.BlockSpec((1, n_tok, IDX_DIM), lambda b, pt: (b, 0, 0))],
        out_specs=pl.BlockSpec((DB * rows_pad, L), lambda b, pt: (0, 0)),
        scratch_shapes=[pltpu.VMEM((DB * rows_pad, L), I32)],
    )
    return pl.pallas_call(
        functools.partial(_dsa_sample_select_kernel, n_pages=n_pages, n_tok=n_tok, k_sel=k_sel,
                          rows_pad=rows_pad),
        out_shape=jax.ShapeDtypeStruct((DB * rows_pad, L), F32),
        grid_spec=grid_spec,
        compiler_params=_cparams(("arbitrary",)),
        name="dsa_sample_select",
    )(page_table, *([cache_ik] * n_pages), qi_rows, wmat, ki_new)


def _dsa_sample_attend_kernel(pt_ref, *refs, n_pages, n_tok, rows_pad):
    del pt_ref
    k_refs = refs[:n_pages]
    v_refs = refs[n_pages:2 * n_pages]
    q_ref, kn_ref, vn_ref, mask_ref, bias_ref, qg_ref, o_ref = refs[2 * n_pages:]
    rows = n_tok * A_HEADS
    pad = jnp.zeros((PAGE_SIZE - n_tok, HEAD_DIM), BF16)

    def gather(page_refs, new_ref, g):
        gs = slice(g * HEAD_DIM, (g + 1) * HEAD_DIM)
        return jnp.concatenate([r[:, g, :].astype(BF16) for r in page_refs]
                               + [new_ref[0][:, gs].astype(BF16), pad], axis=0)

    q = (_rms(q_ref[0], qg_ref[...]) * HEAD_DIM ** -0.5).astype(BF16)
    rep = A_HEADS // A_KV_HEADS
    grp = (lax.broadcasted_iota(I32, (rows, 1), 0) % A_HEADS) // rep
    e_r = lax.broadcasted_iota(I32, (rows, rows_pad), 0) // A_HEADS
    e_c = lax.broadcasted_iota(I32, (rows, rows_pad), 1)
    expand = jnp.where(e_r == e_c, 1.0, 0.0).astype(BF16)
    sel = jnp.dot(expand, mask_ref[...].astype(BF16), preferred_element_type=F32) > 0.5
    logits = None
    for g in range(A_KV_HEADS):
        lg = lax.dot_general(q, gather(k_refs, kn_ref, g), (((1,), (1,)), ((), ())),
                             preferred_element_type=F32)
        logits = lg if logits is None else jnp.where(grp == g, lg, logits)
    logits = jnp.where(sel, logits + bias_ref[...], NEG_BIG)
    m = jnp.max(logits, axis=-1, keepdims=True)
    p = jnp.exp(logits - m)
    den = jnp.sum(p, axis=-1, keepdims=True)
    pb = p.astype(BF16)
    o = None
    for g in range(A_KV_HEADS):
        og = jnp.dot(pb, gather(v_refs, vn_ref, g), preferred_element_type=F32)
        o = og if o is None else jnp.where(grp == g, og, o)
    o_ref[0] = (o / den).astype(o_ref.dtype)


def dsa_sample_attend(cache_k, cache_v, layer, page_table, q_rows, k_new, v_new, mask, bias_tab, q_norm_g):
    DB, n_pages = page_table.shape
    n_tok = k_new.shape[1]
    rows = n_tok * A_HEADS
    rows_pad = mask.shape[0] // DB
    L = mask.shape[1]
    kw = A_KV_HEADS * HEAD_DIM
    page_specs = [pl.BlockSpec((None, None, PAGE_SIZE, A_KV_HEADS, HEAD_DIM), functools.partial(
        lambda b, pt, p: (layer, pt[b, p], 0, 0, 0), p=p)) for p in range(n_pages)]
    grid_spec = pltpu.PrefetchScalarGridSpec(
        num_scalar_prefetch=1,
        grid=(DB,),
        in_specs=page_specs + page_specs + [
            pl.BlockSpec((1, rows, HEAD_DIM), lambda b, pt: (b, 0, 0)),
            pl.BlockSpec((1, n_tok, kw), lambda b, pt: (b, 0, 0)),
            pl.BlockSpec((1, n_tok, kw), lambda b, pt: (b, 0, 0)),
            pl.BlockSpec((rows_pad, L), lambda b, pt: (b, 0)),
            pl.BlockSpec((rows, L), lambda b, pt: (0, 0), pipeline_mode=pl.Buffered(1)),
            pl.BlockSpec((1, HEAD_DIM), lambda b, pt: (0, 0), pipeline_mode=pl.Buffered(1))],
        out_specs=pl.BlockSpec((1, rows, HEAD_DIM), lambda b, pt: (b, 0, 0)),
    )
    return pl.pallas_call(
        functools.partial(_dsa_sample_attend_kernel, n_pages=n_pages, n_tok=n_tok, rows_pad=rows_pad),
        out_shape=jax.ShapeDtypeStruct((DB, rows, HEAD_DIM), BF16),
        grid_spec=grid_spec,
        compiler_params=_cparams(("parallel",)),
        name="dsa_sample_attend",
    )(page_table, *([cache_k] * n_pages), *([cache_v] * n_pages), q_rows, k_new, v_new, mask,
      bias_tab, q_norm_g.reshape(1, HEAD_DIM))


def _log_sigmoid(z):
    return jnp.minimum(z, 0.0) - jnp.log(1.0 + jnp.exp(-jnp.abs(z)))


def _seg_masks(seg):
    r = lax.broadcasted_iota(I32, (TILE, TILE), 0)
    c = lax.broadcasted_iota(I32, (TILE, TILE), 1)
    return r, c, (r // seg) == (c // seg)


def _gla_levels(seg):
    w, out = seg // 2, []
    while w >= 1:
        out.append(w)
        w //= 2
    return out


def _gla_sum_matrices(seg):
    r = jnp.arange(TILE)[:, None]
    c = jnp.arange(TILE)[None, :]
    mats = []
    for w in _gla_levels(seg):
        same = (r // (2 * w)) == (c // (2 * w))
        r_right = (r % (2 * w)) >= w
        c_right = (c % (2 * w)) >= w
        mats.append(same & r_right & c_right & (c <= r))
    for w in _gla_levels(seg):
        same = (r // (2 * w)) == (c // (2 * w))
        r_right = (r % (2 * w)) >= w
        c_right = (c % (2 * w)) >= w
        mats.append(same & (~r_right) & (~c_right) & (c > r))
    same_seg = (r // seg) == (c // seg)
    mats.append(same_seg & (c <= r))
    mats.append(same_seg & (c > r))
    return jnp.concatenate(mats, axis=0).astype(BF16)


def _bdot(a, b):
    return jnp.dot(a.astype(BF16), b.astype(BF16), preferred_element_type=F32)


def _bdot_nt(a, b):
    return lax.dot_general(a.astype(BF16), b.astype(BF16), (((1,), (1,)), ((), ())), preferred_element_type=F32)


def _gla_common(qb_ref, kb_ref, misc_ref, wg_ref, bg_ref, mats_ref, seg):
    gb = misc_ref[:, MISC_GB:MISC_GB + GATE_RANK]
    z = jnp.dot(gb, wg_ref[...], precision=HIGHEST, preferred_element_type=F32) + bg_ref[...]
    la = _log_sigmoid(z) / GATE_TEMP
    la_hi = la.astype(BF16)
    la_lo = (la - la_hi.astype(F32)).astype(BF16)
    mats = mats_ref[...]
    sums = (jnp.dot(mats, la_hi, preferred_element_type=F32) + jnp.dot(mats, la_lo, preferred_element_type=F32))
    levels = _gla_levels(seg)
    nl = len(levels)
    q = qb_ref[...] * B_DK ** -0.5
    k = kb_ref[...]
    r, c, _ = _seg_masks(seg)
    att = [jnp.where(r == c, _bdot_nt(q[:, h * B_DK:(h + 1) * B_DK], k[:, h * B_DK:(h + 1) * B_DK]), 0.0)
           for h in range(B_HEADS)]
    for li, w in enumerate(levels):
        pair = ((r // (2 * w)) == (c // (2 * w))) & ((r % (2 * w)) >= w) & ((c % (2 * w)) < w)
        qd = (q * jnp.exp(sums[li * TILE:(li + 1) * TILE])).astype(BF16)
        kd = (k * jnp.exp(sums[(nl + li) * TILE:(nl + li + 1) * TILE])).astype(BF16)
        for h in range(B_HEADS):
            hs = slice(h * B_DK, (h + 1) * B_DK)
            att[h] = att[h] + jnp.where(pair, _bdot_nt(qd[:, hs], kd[:, hs]), 0.0)
    b_cum = sums[2 * nl * TILE:(2 * nl + 1) * TILE]
    rem = sums[(2 * nl + 1) * TILE:(2 * nl + 2) * TILE]
    return q, k, att, b_cum, rem


def _gla_finish(o_heads, rb_ref, go_ref, o_ref):
    go = go_ref[...]
    for h in range(B_HEADS):
        vs = slice(h * B_DV, (h + 1) * B_DV)
        rb = rb_ref[:, vs]
        o_ref[:, vs] = (_rms(o_heads[h], go) * (rb * jax.nn.sigmoid(rb))).astype(o_ref.dtype)


def _gla_prompt_kernel(qb_ref, kb_ref, vb_ref, rb_ref, misc_ref, wg_ref, bg_ref, go_ref, mats_ref,
                       o_ref, s_ref, state_ref):
    ci = pl.program_id(1)

    @pl.when(ci == 0)
    def _():
        state_ref[...] = jnp.zeros_like(state_ref)

    q, k, att, b_cum, rem = _gla_common(qb_ref, kb_ref, misc_ref, wg_ref, bg_ref, mats_ref, TILE)
    v = vb_ref[...]
    state = state_ref[...]
    qe = q * jnp.exp(b_cum)
    o_heads = []
    for h in range(B_HEADS):
        ks = slice(h * B_DK, (h + 1) * B_DK)
        vs = slice(h * B_DV, (h + 1) * B_DV)
        o_heads.append(_bdot(qe[:, ks], state[ks, :]) + _bdot(att[h], v[:, vs]))
    _gla_finish(o_heads, rb_ref, go_ref, o_ref)

    ke_t = (k * jnp.exp(rem)).T
    e_last = jnp.exp(b_cum[TILE - 1:TILE, :])
    e_col = jnp.broadcast_to(e_last, (TILE, B_HEADS * B_DK)).T[:, 0:1]
    upd = jnp.concatenate(
        [_bdot(ke_t[h * B_DK:(h + 1) * B_DK, :], v[:, h * B_DV:(h + 1) * B_DV]) for h in range(B_HEADS)], axis=0)
    new_state = state * e_col + upd
    state_ref[...] = new_state
    s_ref[0] = new_state


def gla_prompt(proj, w_gate, b_gate, g_out, n_batch, seq):
    T = proj.shape[0]
    nc = seq // TILE
    kwid = B_HEADS * B_DK
    vwid = B_HEADS * B_DV
    mats = _gla_sum_matrices(TILE)
    o, s = pl.pallas_call(
        _gla_prompt_kernel,
        out_shape=(jax.ShapeDtypeStruct((T, vwid), BF16),
                   jax.ShapeDtypeStruct((n_batch, kwid, B_DV), F32)),
        grid=(n_batch, nc),
        in_specs=[pl.BlockSpec((TILE, kwid), lambda b, c: (b * nc + c, QB_OFF // kwid)),
                  pl.BlockSpec((TILE, kwid), lambda b, c: (b * nc + c, KB_OFF // kwid)),
                  pl.BlockSpec((TILE, vwid), lambda b, c: (b * nc + c, VB_OFF // vwid)),
                  pl.BlockSpec((TILE, vwid), lambda b, c: (b * nc + c, RB_OFF // vwid)),
                  pl.BlockSpec((TILE, LANES), lambda b, c: (b * nc + c, MISC_OFF // LANES)),
                  _resident((GATE_RANK, kwid)),
                  _resident((1, kwid)),
                  _resident((1, B_DV)),
                  _resident(mats.shape)],
        out_specs=(pl.BlockSpec((TILE, vwid), lambda b, c: (b * nc + c, 0)),
                   pl.BlockSpec((1, kwid, B_DV), lambda b, c: (b, 0, 0))),
        scratch_shapes=[pltpu.VMEM((kwid, B_DV), F32)],
        compiler_params=_cparams(("parallel", "arbitrary")),
        name="gla_prompt",
    )(proj, proj, proj, proj, proj, w_gate, b_gate.reshape(1, kwid), g_out.reshape(1, B_DV), mats)
    return o, s.reshape(n_batch, B_HEADS, B_DK, B_DV)


def _gla_sample_kernel(qb_ref, kb_ref, vb_ref, rb_ref, misc_ref, wg_ref, bg_ref, go_ref, mats_ref, s0_ref,
                       o_ref, s_ref, *, seg):
    nbt = TILE // seg
    q, k, att, b_cum, rem = _gla_common(qb_ref, kb_ref, misc_ref, wg_ref, bg_ref, mats_ref, seg)
    v = vb_ref[...]
    qe = q * jnp.exp(b_cum)
    ke = k * jnp.exp(rem)
    r1 = lax.broadcasted_iota(I32, (TILE, 1), 0)
    e_last = jnp.where(r1 % seg == seg - 1, jnp.exp(b_cum), 0.0)
    wide = nbt * B_DK
    mq = (lax.broadcasted_iota(I32, (TILE, wide), 0) // seg) == (lax.broadcasted_iota(I32, (TILE, wide), 1) // B_DK)
    mk = (lax.broadcasted_iota(I32, (wide, TILE), 0) // B_DK) == (lax.broadcasted_iota(I32, (wide, TILE), 1) // seg)
    o_heads = []
    for h in range(B_HEADS):
        ks = slice(h * B_DK, (h + 1) * B_DK)
        vs = slice(h * B_DV, (h + 1) * B_DV)
        state = s0_ref[:, h].reshape(wide, B_DV)
        q_bd = jnp.where(mq, jnp.concatenate([qe[:, ks]] * nbt, axis=1), 0.0)
        o_heads.append(_bdot(q_bd, state) + _bdot(att[h], v[:, vs]))
        pair_t = jnp.concatenate([ke[:, ks], e_last[:, ks]], axis=1).T
        k_bd = jnp.where(mk, jnp.concatenate([pair_t[:B_DK]] * nbt, axis=0), 0.0)
        e_bd = jnp.where(mk, jnp.concatenate([pair_t[B_DK:]] * nbt, axis=0), 0.0)
        e_col = jnp.sum(e_bd, axis=-1, keepdims=True)
        new_state = state * e_col + _bdot(k_bd, v[:, vs])
        s_ref[:, h] = new_state.reshape(nbt, B_DK, B_DV)
    _gla_finish(o_heads, rb_ref, go_ref, o_ref)


def gla_sample(proj, w_gate, b_gate, g_out, s0, n_tok):
    T = proj.shape[0]
    DB = s0.shape[0]
    nbt = TILE // n_tok
    kwid = B_HEADS * B_DK
    vwid = B_HEADS * B_DV
    mats = _gla_sum_matrices(n_tok)
    return pl.pallas_call(
        functools.partial(_gla_sample_kernel, seg=n_tok),
        out_shape=(jax.ShapeDtypeStruct((T, vwid), BF16),
                   jax.ShapeDtypeStruct(s0.shape, F32)),
        grid=(T // TILE,),
        in_specs=[pl.BlockSpec((TILE, kwid), lambda i: (i, QB_OFF // kwid)),
                  pl.BlockSpec((TILE, kwid), lambda i: (i, KB_OFF // kwid)),
                  pl.BlockSpec((TILE, vwid), lambda i: (i, VB_OFF // vwid)),
                  pl.BlockSpec((TILE, vwid), lambda i: (i, RB_OFF // vwid)),
                  pl.BlockSpec((TILE, LANES), lambda i: (i, MISC_OFF // LANES)),
                  _resident((GATE_RANK, kwid)),
                  _resident((1, kwid)),
                  _resident((1, B_DV)),
                  _resident(mats.shape),
                  pl.BlockSpec((nbt, B_HEADS, B_DK, B_DV), lambda i: (i, 0, 0, 0))],
        out_specs=(pl.BlockSpec((TILE, vwid), lambda i: (i, 0)),
                   pl.BlockSpec((nbt, B_HEADS, B_DK, B_DV), lambda i: (i, 0, 0, 0))),
        compiler_params=_cparams(("parallel",)),
        name="gla_sample",
    )(proj, proj, proj, proj, proj, w_gate, b_gate.reshape(1, kwid), g_out.reshape(1, B_DV), mats, s0)


def _gelu(x):
    return jax.nn.gelu(x)


def _gmlp_kernel(uc_ref, vc_ref, gv_ref, ws_ref, bcol_ref, o_ref, vn_ref, *, seg):
    r, c, same_seg = _seg_masks(seg)
    keep = same_seg & (c <= r)
    u = _gelu(uc_ref[...])
    vg = _gelu(vc_ref[...])
    for g in range(C_GROUPS):
        gs = slice(g * C_GROUP_DIM, (g + 1) * C_GROUP_DIM)
        vn = _rms(vg[:, gs], gv_ref[:, gs])
        vn_ref[:, gs] = vn
        w = jnp.where(keep, ws_ref[g], 0.0).astype(BF16)
        s = jnp.dot(w, vn.astype(BF16), preferred_element_type=F32) + bcol_ref[:, g:g + 1]
        o_ref[:, gs] = (u[:, gs] * s).astype(o_ref.dtype)


def gmlp(proj, g_v, w_tiles, b_cols, seg):
    T = proj.shape[0]
    cw = C_GROUPS * C_GROUP_DIM
    return pl.pallas_call(
        functools.partial(_gmlp_kernel, seg=seg),
        out_shape=(jax.ShapeDtypeStruct((T, cw), BF16),
                   jax.ShapeDtypeStruct((T, cw), F32)),
        grid=(T // TILE,),
        in_specs=[pl.BlockSpec((TILE, cw), lambda i: (i, UC_OFF // cw)),
                  pl.BlockSpec((TILE, cw), lambda i: (i, VC_OFF // cw)),
                  _resident((1, cw)),
                  _resident((C_GROUPS, TILE, TILE)),
                  _resident((TILE, C_GROUPS))],
        out_specs=(pl.BlockSpec((TILE, cw), lambda i: (i, 0)),
                   pl.BlockSpec((TILE, cw), lambda i: (i, 0))),
        compiler_params=_cparams(("parallel",)),
        name="gmlp",
    )(proj, proj, g_v.reshape(1, cw), w_tiles, b_cols)


def _out_kernel(h_ref, oa_ref, ob_ref, oc_ref, w_ref, o_ref):
    aw = oa_ref.shape[1]
    bw = ob_ref.shape[1]
    acc = jnp.dot(oa_ref[...], w_ref[0:aw, :], preferred_element_type=F32)
    acc = acc + jnp.dot(ob_ref[...], w_ref[aw:aw + bw, :], preferred_element_type=F32)
    acc = acc + jnp.dot(oc_ref[...], w_ref[aw + bw:, :], preferred_element_type=F32)
    o_ref[...] = h_ref[...] + acc


def out_projection(h, o_a, o_b, o_c, w_out, tm):
    T, D = h.shape
    tm = min(tm, T)
    return pl.pallas_call(
        _out_kernel,
        out_shape=jax.ShapeDtypeStruct((T, D), F32),
        grid=(T // tm,),
        in_specs=[pl.BlockSpec((tm, D), lambda i: (i, 0)),
                  pl.BlockSpec((tm, o_a.shape[1]), lambda i: (i, 0)),
                  pl.BlockSpec((tm, o_b.shape[1]), lambda i: (i, 0)),
                  pl.BlockSpec((tm, o_c.shape[1]), lambda i: (i, 0)),
                  _resident(w_out.shape)],
        out_specs=pl.BlockSpec((tm, D), lambda i: (i, 0)),
        compiler_params=_cparams(("parallel",)),
        name="out_projection",
    )(h, o_a, o_b, o_c, w_out)


def _ffn_kernel(h_ref, g_ref, wg_ref, wu_ref, wd_ref, o_ref, n_ref):
    j = pl.program_id(1)

    @pl.when(j == 0)
    def _():
        h = h_ref[...]
        n_ref[...] = _rms(h, g_ref[...]).astype(BF16)
        o_ref[...] = h

    n = n_ref[...]
    a = jnp.dot(n, wg_ref[...], preferred_element_type=F32)
    u = jnp.dot(n, wu_ref[...], preferred_element_type=F32)
    act = (a * jax.nn.sigmoid(a) * u).astype(BF16)
    o_ref[...] += jnp.dot(act, wd_ref[...], preferred_element_type=F32)


def ffn(h, g, w_gate, w_up, w_down, tm, tf):
    T, D = h.shape
    tm = min(tm, T)
    FF = w_gate.shape[1]
    return pl.pallas_call(
        _ffn_kernel,
        out_shape=jax.ShapeDtypeStruct((T, D), F32),
        grid=(T // tm, FF // tf),
        in_specs=[pl.BlockSpec((tm, D), lambda i, j: (i, 0)),
                  _resident((1, D)),
                  pl.BlockSpec((D, tf), lambda i, j: (0, j)),
                  pl.BlockSpec((D, tf), lambda i, j: (0, j)),
                  pl.BlockSpec((tf, D), lambda i, j: (j, 0))],
        out_specs=pl.BlockSpec((tm, D), lambda i, j: (i, 0)),
        scratch_shapes=[pltpu.VMEM((tm, D), BF16)],
        compiler_params=_cparams(("parallel", "arbitrary")),
        name="ffn",
    )(h, g.reshape(1, D), w_gate, w_up, w_down)


def _ple_kernel(h_ref, p_ref, g_ref, wgate_ref, wproj_ref, o_ref):
    h = h_ref[...]
    n = _rms(h, g_ref[...]).astype(BF16)
    gate = jax.nn.sigmoid(jnp.dot(n, wgate_ref[...], preferred_element_type=F32))
    emb = jnp.dot(p_ref[...].astype(BF16), wproj_ref[...], preferred_element_type=F32)
    o_ref[...] = h + gate * emb


def ple(h, p, g, w_gate, w_proj, tm):
    T, D = h.shape
    tm = min(tm, T)
    P = p.shape[1]
    return pl.pallas_call(
        _ple_kernel,
        out_shape=jax.ShapeDtypeStruct((T, D), F32),
        grid=(T // tm,),
        in_specs=[pl.BlockSpec((tm, D), lambda i: (i, 0)),
                  pl.BlockSpec((tm, P), lambda i: (i, 0)),
                  _resident((1, D)),
                  _resident(w_gate.shape),
                  _resident(w_proj.shape)],
        out_specs=pl.BlockSpec((tm, D), lambda i: (i, 0)),
        compiler_params=_cparams(("parallel",)),
        name="ple",
    )(h, p, g.reshape(1, D), w_gate, w_proj)


def _pack_w_in(w):
    sizes = (1024, 256, 256, 1024, 64, 16, 256, 256, 512, 16, 512, 512, 512)
    names = ("q", "k", "v", "qi", "ki", "wi", "qb", "kb", "vb", "gb", "rb", "uc", "vc")
    seg, start = {}, 0
    for nme, sz in zip(names, sizes):
        seg[nme] = w[:, start:start + sz]
        start += sz
    pad = jnp.zeros((w.shape[0], LANES - IDX_DIM - IDX_HEADS - GATE_RANK), w.dtype)
    order = ("q", "qi", "vb", "rb", "uc", "vc", "k", "v", "qb", "kb", "ki", "wi", "gb")
    return jnp.concatenate([seg[n] for n in order] + [pad], axis=1).astype(BF16)


def _mixer_tail(h, o_a, o_b, o_c, p_l, lw, tm, tm_ffn):
    h = out_projection(h, o_a, o_b, o_c, lw["w_out"], tm)
    h = ffn(h, lw["g_ffn"], lw["w_ffn_gate"], lw["w_ffn_up"], lw["w_ffn_down"], tm_ffn, 512)
    return ple(h, p_l, lw["g_ple"], lw["w_ple_gate"], lw["w_ple_proj"], tm)


def kernel(x_prompt, x_sample, cache_k, cache_v, cache_idx_k, state_gla, page_table, p_prompt, p_sample,
           g_mix, w_in, q_norm_g, k_norm_g, rel_bias, w_gate_b, b_gate_b, g_out_b, g_v_c, w_spatial,
           b_spatial, w_out, g_ffn, w_ffn_gate, w_ffn_up, w_ffn_down, g_ple, w_ple_gate, w_ple_proj):
    n_batch, seq, d_model = x_prompt.shape
    dec_batch, dec_seq, _ = x_sample.shape
    depth = w_in.shape[0]
    n_pages = page_table.shape[1]
    past = n_pages * PAGE_SIZE
    kw = A_KV_HEADS * HEAD_DIM
    tp, ts = n_batch * seq, dec_batch * dec_seq
    rows_pad = SUBLANES
    l_sample = past + PAGE_SIZE
    k_sel_s = min(TOPK_MAX, (past + dec_seq) // 4)

    bias_p = bias_table_prompt(rel_bias)
    bias_s = bias_table_sample(rel_bias, past, dec_seq, l_sample)

    hp = x_prompt.reshape(tp, d_model)
    hs = x_sample.reshape(ts, d_model)
    outs = {k: [] for k in ("kp", "vp", "ikp", "sp", "ks", "vs", "iks", "ss", "cs")}
    eye_t = jnp.eye(rows_pad, dec_seq, dtype=F32)
    for i in range(depth):
        lw = dict(w_out=w_out[i].astype(BF16), g_ffn=g_ffn[i], w_ffn_gate=w_ffn_gate[i].astype(BF16),
                  w_ffn_up=w_ffn_up[i].astype(BF16), w_ffn_down=w_ffn_down[i].astype(BF16),
                  g_ple=g_ple[i], w_ple_gate=w_ple_gate[i].astype(BF16),
                  w_ple_proj=w_ple_proj[i].astype(BF16))
        w_packed = _pack_w_in(w_in[i])
        b_cols_p = b_spatial[i].T
        reps = TILE // dec_seq
        w_tiles_s = jnp.tile(w_spatial[i][:, :dec_seq, :dec_seq], (1, reps, reps))
        b_cols_s = jnp.tile(b_spatial[i][:, :dec_seq].T, (reps, 1))

        proj = in_projection(hp, g_mix[i], w_packed, 256)
        kn, vv, ik = kv_post(proj, k_norm_g[i], 512)
        o_a = dsa_prompt(proj, kn, bias_p, q_norm_g[i], n_batch, seq)
        o_b, s_p = gla_prompt(proj, w_gate_b[i], b_gate_b[i], g_out_b[i], n_batch, seq)
        o_c, _ = gmlp(proj, g_v_c[i], w_spatial[i], b_cols_p, TILE)
        hp = _mixer_tail(hp, o_a, o_b, o_c, p_prompt[i].reshape(tp, -1), lw, 256, 512)
        outs["kp"].append(kn.reshape(n_batch, seq, A_KV_HEADS, HEAD_DIM))
        outs["vp"].append(vv.reshape(n_batch, seq, A_KV_HEADS, HEAD_DIM))
        outs["ikp"].append(ik.reshape(n_batch, seq, IDX_DIM))
        outs["sp"].append(s_p)

        proj = in_projection(hs, g_mix[i], w_packed, 256)
        kn, vv, ik = kv_post(proj, k_norm_g[i], 512)
        qi_rows = proj[:, QI_OFF:QI_OFF + IDX_HEADS * IDX_DIM].reshape(dec_batch, dec_seq * IDX_HEADS, IDX_DIM)
        wi = proj[:, MISC_OFF + MISC_WI:MISC_OFF + MISC_WI + IDX_HEADS].reshape(dec_batch, dec_seq, IDX_HEADS)
        wi = wi * (IDX_HEADS ** -0.5 * IDX_DIM ** -0.5)
        wmat = (eye_t[None, :, :, None] * wi[:, None, :, :]).reshape(dec_batch, rows_pad, dec_seq * IDX_HEADS)
        mask = dsa_sample_select(cache_idx_k, i, page_table, qi_rows, wmat,
                                 ik.reshape(dec_batch, dec_seq, IDX_DIM), k_sel_s)
        q_rows = proj[:, Q_OFF:Q_OFF + A_HEADS * HEAD_DIM].reshape(dec_batch, dec_seq * A_HEADS, HEAD_DIM)
        o_a = dsa_sample_attend(cache_k, cache_v, i, page_table, q_rows, kn.reshape(dec_batch, dec_seq, kw),
                                vv.reshape(dec_batch, dec_seq, kw), mask, bias_s, q_norm_g[i])
        o_a = o_a.reshape(ts, A_HEADS * HEAD_DIM)
        o_b, s_s = gla_sample(proj, w_gate_b[i], b_gate_b[i], g_out_b[i], state_gla[i], dec_seq)
        o_c, vn = gmlp(proj, g_v_c[i], w_tiles_s, b_cols_s, dec_seq)
        hs = _mixer_tail(hs, o_a, o_b, o_c, p_sample[i].reshape(ts, -1), lw, 256, 512)
        outs["ks"].append(kn.reshape(dec_batch, dec_seq, A_KV_HEADS, HEAD_DIM))
        outs["vs"].append(vv.reshape(dec_batch, dec_seq, A_KV_HEADS, HEAD_DIM))
        outs["iks"].append(ik.reshape(dec_batch, dec_seq, IDX_DIM))
        outs["ss"].append(s_s)
        outs["cs"].append(vn.reshape(dec_batch, dec_seq, -1))

    st = {k: jnp.stack(v) for k, v in outs.items()}
    return (hp.reshape(n_batch, seq, d_model), hs.reshape(dec_batch, dec_seq, d_model),
            st["kp"], st["vp"], st["ikp"], st["sp"], st["ks"], st["vs"], st["iks"], st["ss"], st["cs"])
```

```python
import functools
import math

import jax
import jax.numpy as jnp
from jax import lax
from jax.experimental import pallas as pl
from jax.experimental.pallas import tpu as pltpu

F32 = jnp.float32
BF16 = jnp.bfloat16
I32 = jnp.int32
HIGHEST = lax.Precision.HIGHEST

LANES = 128
SUBLANES = 8
VMEM_LIMIT = 56 * 1024 * 1024

HEAD_DIM = 128
A_HEADS = 8
A_KV_HEADS = 2
IDX_HEADS = 16
IDX_DIM = 64
TOPK_MAX = 256
NUM_BUCKETS = 32
MAX_DISTANCE = 128
B_HEADS = 4
B_DK = 64
B_DV = 128
GATE_RANK = 16
GATE_TEMP = 16.0
C_GROUPS = 4
C_GROUP_DIM = 128
PAGE_SIZE = 128
EPS = 1e-6
NEG_BIG = -1e30
INT_MIN = -(2 ** 31)

TILE = 128
QBLK = 256

Q_OFF, QI_OFF, VB_OFF, RB_OFF, UC_OFF, VC_OFF = 0, 1024, 2048, 2560, 3072, 3584
K_OFF, V_OFF, QB_OFF, KB_OFF, MISC_OFF = 4096, 4352, 4608, 4864, 5120
PROJ_PACKED = 5248
MISC_KI, MISC_WI, MISC_GB = 0, 64, 80


def _cparams(sem):
    return pltpu.CompilerParams(dimension_semantics=sem, vmem_limit_bytes=VMEM_LIMIT)


def _rms(x, g):
    return x * lax.rsqrt(jnp.mean(x * x, axis=-1, keepdims=True) + EPS) * g


def _resident(shape):
    nd = len(shape)
    return pl.BlockSpec(shape, lambda *_: (0,) * nd, pipeline_mode=pl.Buffered(1))


def _proj_kernel(x_ref, g_ref, w_ref, o_ref):
    n = _rms(x_ref[...], g_ref[...]).astype(BF16)
    ncol = o_ref.shape[1]
    step = 512
    for c0 in range(0, ncol, step):
        c1 = min(c0 + step, ncol)
        o_ref[:, c0:c1] = jnp.dot(n, w_ref[:, c0:c1], preferred_element_type=F32)


def in_projection(h, g, w_packed, tm):
    T, D = h.shape
    tm = min(tm, T)
    N = w_packed.shape[1]
    return pl.pallas_call(
        _proj_kernel,
        out_shape=jax.ShapeDtypeStruct((T, N), F32),
        grid=(T // tm,),
        in_specs=[pl.BlockSpec((tm, D), lambda i: (i, 0)),
                  _resident((1, D)),
                  _resident((D, N))],
        out_specs=pl.BlockSpec((tm, N), lambda i: (i, 0)),
        compiler_params=_cparams(("parallel",)),
        name="in_projection",
    )(h, g.reshape(1, D), w_packed)


def _kv_kernel(k_ref, v_ref, m_ref, g_ref, ko_ref, vo_ref, io_ref, vt_ref=None):
    g = g_ref[...]
    k = k_ref[...]
    for hh in range(A_KV_HEADS):
        sl = slice(hh * HEAD_DIM, (hh + 1) * HEAD_DIM)
        ko_ref[:, sl] = _rms(k[:, sl], g)
    v = v_ref[...]
    vo_ref[...] = v
    io_ref[...] = m_ref[:, MISC_KI:MISC_KI + IDX_DIM]
    if vt_ref is not None:
        for blk in range(vt_ref.shape[0]):
            vt_ref[blk] = v[blk * QBLK:(blk + 1) * QBLK, :].T.astype(vt_ref.dtype)


def kv_post(proj, k_norm_g, tm, with_vt):
    T = proj.shape[0]
    tm = min(tm, T)
    kw = A_KV_HEADS * HEAD_DIM
    out_shape = [jax.ShapeDtypeStruct((T, kw), F32),
                 jax.ShapeDtypeStruct((T, kw), F32),
                 jax.ShapeDtypeStruct((T, IDX_DIM), F32)]
    out_specs = [pl.BlockSpec((tm, kw), lambda i: (i, 0)),
                 pl.BlockSpec((tm, kw), lambda i: (i, 0)),
                 pl.BlockSpec((tm, IDX_DIM), lambda i: (i, 0))]
    if with_vt:
        out_shape.append(jax.ShapeDtypeStruct((T // QBLK, kw, QBLK), BF16))
        out_specs.append(pl.BlockSpec((tm // QBLK, kw, QBLK), lambda i: (i, 0, 0)))
    return pl.pallas_call(
        _kv_kernel,
        out_shape=tuple(out_shape),
        grid=(T // tm,),
        in_specs=[pl.BlockSpec((tm, kw), lambda i: (i, K_OFF // kw)),
                  pl.BlockSpec((tm, kw), lambda i: (i, V_OFF // kw)),
                  pl.BlockSpec((tm, LANES), lambda i: (i, MISC_OFF // LANES)),
                  _resident((1, HEAD_DIM))],
        out_specs=tuple(out_specs),
        compiler_params=_cparams(("parallel",)),
        name="kv_post",
    )(proj, proj, proj, k_norm_g.reshape(1, HEAD_DIM))


def _bucket(dist):
    n = jnp.maximum(dist, 0)
    max_exact = NUM_BUCKETS // 2
    large = max_exact + (jnp.log(jnp.maximum(n, 1).astype(F32) / max_exact)
                         / math.log(MAX_DISTANCE / max_exact)
                         * (NUM_BUCKETS - max_exact)).astype(I32)
    large = jnp.minimum(large, NUM_BUCKETS - 1)
    return jnp.where(n < max_exact, n, large)


def _bias_prompt_kernel(rb_ref, o_ref):
    c = lax.broadcasted_iota(I32, (TILE, TILE), 0)
    t = lax.broadcasted_iota(I32, (TILE, TILE), 1)
    for z in range(3):
        bucket = _bucket(t - c + (2 - z) * TILE)
        for h in range(A_HEADS):
            acc = jnp.zeros((TILE, TILE), F32)
            for b in range(NUM_BUCKETS):
                acc = jnp.where(bucket == b, rb_ref[b, h], acc)
            o_ref[h, z] = acc


def bias_table_prompt(rel_bias):
    return pl.pallas_call(
        _bias_prompt_kernel,
        out_shape=jax.ShapeDtypeStruct((A_HEADS, 3, TILE, TILE), F32),
        in_specs=[pl.BlockSpec(memory_space=pltpu.SMEM)],
        out_specs=pl.BlockSpec(memory_space=pltpu.VMEM),
        name="bias_table_prompt",
    )(rel_bias)


def _bias_sample_kernel(rbrows_ref, o_ref, *, past, n_tok):
    rows, L = o_ref.shape
    r = lax.broadcasted_iota(I32, (rows, L), 0)
    s = lax.broadcasted_iota(I32, (rows, L), 1) // A_KV_HEADS
    bucket = _bucket(past + r // A_HEADS - s)
    rbrows = rbrows_ref[...]
    acc = jnp.zeros((rows, L), F32)
    for b in range(NUM_BUCKETS):
        acc = jnp.where(bucket == b, rbrows[:, b:b + 1], acc)
    o_ref[...] = acc


def bias_table_sample(rel_bias, past, n_tok, L):
    rows = n_tok * A_HEADS
    rbrows = jnp.tile(rel_bias.T, (n_tok, 1))
    return pl.pallas_call(
        functools.partial(_bias_sample_kernel, past=past, n_tok=n_tok),
        out_shape=jax.ShapeDtypeStruct((rows, L), F32),
        name="bias_table_sample",
    )(rbrows)


def _sortable_key(x):
    b = lax.bitcast_convert_type(x, I32)
    return b ^ ((b >> 31) & 0x7FFFFFFF)


def _topk_member(skey_ref, k_sel):
    R, L = skey_ref.shape

    def body(it, ans):
        bit = 31 - it
        cand = ans | lax.shift_left(jnp.int32(1), bit)
        cand_s = cand ^ INT_MIN
        cnt = jnp.sum(jnp.where(skey_ref[...] >= cand_s, 1.0, 0.0), axis=-1, keepdims=True)
        return jnp.where(cnt >= k_sel, cand, ans)

    ans = lax.fori_loop(0, 32, body, jnp.zeros((R, 1), I32))
    tau = ans ^ INT_MIN
    skey = skey_ref[...]
    gt = skey > tau
    eq = skey == tau
    n_gt = jnp.sum(jnp.where(gt, 1.0, 0.0), axis=-1, keepdims=True)
    room = k_sel - n_gt
    r_i = lax.broadcasted_iota(I32, (LANES, LANES), 0)
    c_i = lax.broadcasted_iota(I32, (LANES, LANES), 1)
    upper = jnp.where(r_i <= c_i, 1.0, 0.0).astype(BF16)
    off = jnp.zeros((R, 1), F32)
    parts = []
    for j in range(L // LANES):
        sl = slice(j * LANES, (j + 1) * LANES)
        eq_j = eq[:, sl]
        run = jnp.dot(jnp.where(eq_j, 1.0, 0.0).astype(BF16), upper, preferred_element_type=F32) + off
        parts.append(gt[:, sl] | (eq_j & (run <= room)))
        off = run[:, LANES - 1:LANES]
    return jnp.concatenate(parts, axis=1)


def _fold8(x, op):
    return op(x.reshape(x.shape[0] // SUBLANES, SUBLANES, x.shape[1]), axis=0)


def _dsa_prompt_kernel(q_ref, qi_ref, misc_ref, kn_ref, vt_ref, bias_ref, qg_ref, o_ref,
                       qst_ref, skey_ref, madd_ref, lg_ref, acc_ref, *, k_sel):
    i = pl.program_id(1)
    nkb = i + 1
    sub = QBLK // TILE
    rep = A_HEADS // A_KV_HEADS
    row0 = pl.multiple_of(i * QBLK, QBLK)
    s_iota = lax.broadcasted_iota(I32, (QBLK, QBLK), 0)
    t_iota = lax.broadcasted_iota(I32, (QBLK, QBLK), 1)

    def admissible(j):
        return (j * QBLK + s_iota) <= (row0 + t_iota)

    wi_t = misc_ref[pl.ds(row0, QBLK), :].T[MISC_WI:MISC_WI + IDX_HEADS, :]
    wi_t = wi_t * (IDX_HEADS ** -0.5 * IDX_DIM ** -0.5)
    for h in range(IDX_HEADS):
        qst_ref[h * QBLK:(h + 1) * QBLK, :] = qi_ref[:, h * IDX_DIM:(h + 1) * IDX_DIM].astype(BF16)

    def score_body(j, carry):
        k0 = pl.multiple_of(j * QBLK, QBLK)
        kj = misc_ref[pl.ds(k0, QBLK), MISC_KI:MISC_KI + IDX_DIM].astype(BF16)
        s = lax.dot_general(kj, qst_ref[...], (((1,), (1,)), ((), ())), preferred_element_type=F32)
        score = jnp.zeros((QBLK, QBLK), F32)
        for h in range(IDX_HEADS):
            score = score + jnp.maximum(s[:, h * QBLK:(h + 1) * QBLK], 0.0) * wi_t[h:h + 1, :]
        skey_ref[j] = _sortable_key(jnp.where(admissible(j), score, -jnp.inf))
        return carry

    lax.fori_loop(0, nkb, score_body, 0)

    def count(pred_fn):
        def body(j, acc):
            return acc + _fold8(jnp.where(pred_fn(skey_ref[j]), 1.0, 0.0), jnp.sum)
        acc = lax.fori_loop(0, nkb, body, jnp.zeros((SUBLANES, QBLK), F32))
        return jnp.sum(acc, axis=0, keepdims=True)

    def bit_body(it, ans):
        cand = ans | lax.shift_left(jnp.int32(1), 31 - it)
        cand_s = cand ^ INT_MIN
        cnt = count(lambda key: key >= cand_s)
        return jnp.where(cnt >= k_sel, cand, ans)

    ans = lax.fori_loop(0, 32, bit_body, jnp.zeros((1, QBLK), I32))
    tau = ans ^ INT_MIN
    room = k_sel - count(lambda key: key > tau)
    lower = jnp.where(t_iota <= s_iota, 1.0, 0.0).astype(BF16)

    def mask_body(j, off):
        key = skey_ref[j]
        eq = key == tau
        run = jnp.dot(lower, jnp.where(eq, 1.0, 0.0).astype(BF16), preferred_element_type=F32) + off
        sel = ((key > tau) | (eq & (run <= room))) & admissible(j)
        madd_ref[j] = jnp.where(sel, 0.0, NEG_BIG)
        return run[QBLK - 1:QBLK, :]

    lax.fori_loop(0, nkb, mask_body, jnp.zeros((1, QBLK), F32))

    qg = qg_ref[...]
    wide = rep * QBLK
    for g in range(A_KV_HEADS):
        gs = slice(g * HEAD_DIM, (g + 1) * HEAD_DIM)
        heads = list(range(g * rep, (g + 1) * rep))
        q_stack = jnp.concatenate(
            [(_rms(q_ref[:, h * HEAD_DIM:(h + 1) * HEAD_DIM], qg) * HEAD_DIM ** -0.5).astype(BF16) for h in heads],
            axis=0)

        def logit_body(j, mx):
            k0 = pl.multiple_of(j * QBLK, QBLK)
            kj = kn_ref[pl.ds(k0, QBLK), gs].astype(BF16)
            lg = lax.dot_general(kj, q_stack, (((1,), (1,)), ((), ())), preferred_element_type=F32)
            madd = madd_ref[j]
            parts = []
            for r, h in enumerate(heads):
                quads = []
                for c in range(sub):
                    quads.append(jnp.concatenate(
                        [bias_ref[h, jnp.clip(2 - ((i - j) * sub + u - c), 0, 2)] for u in range(sub)], axis=1))
                parts.append(lg[:, r * QBLK:(r + 1) * QBLK] + jnp.concatenate(quads, axis=0) + madd)
            lg = jnp.concatenate(parts, axis=1)
            lg_ref[j] = lg
            return jnp.maximum(mx, _fold8(lg, jnp.max))

        mx = lax.fori_loop(0, nkb, logit_body, jnp.full((SUBLANES, wide), NEG_BIG, F32))
        m = jnp.max(mx, axis=0, keepdims=True)
        acc_ref[...] = jnp.zeros(acc_ref.shape, F32)

        def pv_body(j, sm):
            p = jnp.exp(lg_ref[j] - m)
            acc_ref[...] += jnp.dot(vt_ref[j, gs, :], p.astype(BF16), preferred_element_type=F32)
            return sm + _fold8(p, jnp.sum)

        sm = lax.fori_loop(0, nkb, pv_body, jnp.zeros((SUBLANES, wide), F32))
        den = jnp.sum(sm, axis=0, keepdims=True)
        o = (acc_ref[...] / den).T
        for r, h in enumerate(heads):
            o_ref[:, h * HEAD_DIM:(h + 1) * HEAD_DIM] = o[r * QBLK:(r + 1) * QBLK, :].astype(o_ref.dtype)


def dsa_prompt(proj, kn, vt, bias_tab, q_norm_g, n_batch, seq):
    T = proj.shape[0]
    nb = seq // QBLK
    k_sel = min(TOPK_MAX, seq // 4)
    aw = A_HEADS * HEAD_DIM
    iw = IDX_HEADS * IDX_DIM
    kw = A_KV_HEADS * HEAD_DIM
    rep = A_HEADS // A_KV_HEADS
    return pl.pallas_call(
        functools.partial(_dsa_prompt_kernel, k_sel=k_sel),
        out_shape=jax.ShapeDtypeStruct((T, aw), BF16),
        grid=(n_batch, nb),
        in_specs=[pl.BlockSpec((QBLK, aw), lambda b, i: (b * nb + i, Q_OFF // aw)),
                  pl.BlockSpec((QBLK, iw), lambda b, i: (b * nb + i, QI_OFF // iw)),
                  pl.BlockSpec((seq, LANES), lambda b, i: (b, MISC_OFF // LANES)),
                  pl.BlockSpec((seq, kw), lambda b, i: (b, 0)),
                  pl.BlockSpec((nb, kw, QBLK), lambda b, i: (b, 0, 0)),
                  _resident((A_HEADS, 3, TILE, TILE)),
                  _resident((1, HEAD_DIM))],
        out_specs=pl.BlockSpec((QBLK, aw), lambda b, i: (b * nb + i, 0)),
        scratch_shapes=[pltpu.VMEM((IDX_HEADS * QBLK, IDX_DIM), BF16),
                        pltpu.VMEM((nb, QBLK, QBLK), I32),
                        pltpu.VMEM((nb, QBLK, QBLK), F32),
                        pltpu.VMEM((nb, QBLK, rep * QBLK), F32),
                        pltpu.VMEM((HEAD_DIM, rep * QBLK), F32)],
        compiler_params=_cparams(("parallel", "arbitrary")),
        name="dsa_prompt",
    )(proj, proj, proj, kn, vt, bias_tab, q_norm_g.reshape(1, HEAD_DIM))


def _dsa_sample_select_kernel(pt_ref, *refs, n_pages, n_tok, k_sel, rows_pad):
    del pt_ref
    page_refs = refs[:n_pages]
    qi_ref, wm_ref, kin_ref, mask_ref, skey_ref = refs[n_pages:]
    b = pl.program_id(0)
    nb = pl.num_programs(0)
    L = mask_ref.shape[1]
    past = n_pages * PAGE_SIZE

    qi = qi_ref[0]
    pieces = []
    for p in range(n_pages + 1):
        kp_t = page_refs[p][...] if p < n_pages else kin_ref[0]
        pieces.append(jnp.maximum(_bdot(qi, kp_t), 0.0))
    relu_s = jnp.concatenate(pieces, axis=1)
    score = jnp.dot(wm_ref[0], relu_s, precision=HIGHEST, preferred_element_type=F32)
    t_pos = past + lax.broadcasted_iota(I32, (rows_pad, L), 0)
    s_pos = lax.broadcasted_iota(I32, (rows_pad, L), 1)
    adm = s_pos <= t_pos
    r0 = pl.multiple_of(b * rows_pad, rows_pad)
    skey_ref[pl.ds(r0, rows_pad), :] = _sortable_key(jnp.where(adm, score, -jnp.inf))

    @pl.when(b == nb - 1)
    def _():
        n_blocks = skey_ref.shape[0] // TILE
        tp = past + lax.broadcasted_iota(I32, (TILE, L), 0) % rows_pad
        sp = lax.broadcasted_iota(I32, (TILE, L), 1)
        adm_blk = sp <= tp
        for rb in range(n_blocks):
            blk = skey_ref.at[rb * TILE:(rb + 1) * TILE, :]
            sel = _topk_member(blk, k_sel) & adm_blk
            mask_ref[rb * TILE:(rb + 1) * TILE, :] = jnp.where(sel, 1.0, 0.0)


def dsa_sample_select(cache_ik_t, layer, page_table, qi_rows, wmat, ki_new_t, n_tok, k_sel):
    DB, n_pages = page_table.shape
    rows_pad = wmat.shape[1]
    L = (n_pages + 1) * PAGE_SIZE
    page_specs = [pl.BlockSpec((None, None, IDX_DIM, PAGE_SIZE), functools.partial(
        lambda b, pt, p: (layer, pt[b, p], 0, 0), p=p)) for p in range(n_pages)]
    grid_spec = pltpu.PrefetchScalarGridSpec(
        num_scalar_prefetch=1,
        grid=(DB,),
        in_specs=page_specs + [
            pl.BlockSpec((1,) + qi_rows.shape[1:], lambda b, pt: (b, 0, 0)),
            pl.BlockSpec((1,) + wmat.shape[1:], lambda b, pt: (b, 0, 0)),
            pl.BlockSpec((1, IDX_DIM, PAGE_SIZE), lambda b, pt: (b, 0, 0))],
        out_specs=pl.BlockSpec((DB * rows_pad, L), lambda b, pt: (0, 0)),
        scratch_shapes=[pltpu.VMEM((DB * rows_pad, L), I32)],
    )
    return pl.pallas_call(
        functools.partial(_dsa_sample_select_kernel, n_pages=n_pages, n_tok=n_tok, k_sel=k_sel,
                          rows_pad=rows_pad),
        out_shape=jax.ShapeDtypeStruct((DB * rows_pad, L), F32),
        grid_spec=grid_spec,
        compiler_params=_cparams(("arbitrary",)),
        name="dsa_sample_select",
    )(page_table, *([cache_ik_t] * n_pages), qi_rows, wmat, ki_new_t)


def _dsa_sample_attend_kernel(pt_ref, *refs, n_pages, n_tok, rows_pad):
    del pt_ref
    k_refs = refs[:n_pages]
    v_refs = refs[n_pages:2 * n_pages]
    q_ref, kn_ref, vn_ref, mask_ref, bias_ref, qg_ref, o_ref = refs[2 * n_pages:]
    rows = n_tok * A_HEADS
    page_rows = PAGE_SIZE * A_KV_HEADS
    n_tiles = mask_ref.shape[1] // PAGE_SIZE
    pad = jnp.zeros((page_rows - n_tok * A_KV_HEADS, HEAD_DIM), BF16)

    def gather(page_refs, new_ref):
        return jnp.concatenate([r[...].astype(BF16) for r in page_refs]
                               + [new_ref[0].astype(BF16), pad], axis=0)

    q = (_rms(q_ref[0], qg_ref[...]) * HEAD_DIM ** -0.5).astype(BF16)
    rep = A_HEADS // A_KV_HEADS
    grp = (lax.broadcasted_iota(I32, (rows, 1), 0) % A_HEADS) // rep
    e_r = lax.broadcasted_iota(I32, (rows, rows_pad), 0) // A_HEADS
    e_c = lax.broadcasted_iota(I32, (rows, rows_pad), 1)
    expand = jnp.where(e_r == e_c, 1.0, 0.0).astype(BF16)
    sel = jnp.dot(expand, mask_ref[...].astype(BF16), preferred_element_type=F32)
    d_r = lax.broadcasted_iota(I32, (PAGE_SIZE, page_rows), 0)
    d_c = lax.broadcasted_iota(I32, (PAGE_SIZE, page_rows), 1)
    dup = jnp.where(d_c // A_KV_HEADS == d_r, 1.0, 0.0).astype(BF16)
    stacked = jnp.concatenate([sel[:, j * PAGE_SIZE:(j + 1) * PAGE_SIZE] for j in range(n_tiles)], axis=0)
    stacked = jnp.dot(stacked.astype(BF16), dup, preferred_element_type=F32)
    sel = jnp.concatenate([stacked[j * rows:(j + 1) * rows, :] for j in range(n_tiles)], axis=1)
    col_grp = lax.broadcasted_iota(I32, (rows, n_tiles * page_rows), 1) % A_KV_HEADS
    valid = (sel > 0.5) & (col_grp == grp)
    logits = lax.dot_general(q, gather(k_refs, kn_ref), (((1,), (1,)), ((), ())), preferred_element_type=F32)
    logits = jnp.where(valid, logits + bias_ref[...], NEG_BIG)
    m = jnp.max(logits, axis=-1, keepdims=True)
    p = jnp.exp(logits - m)
    den = jnp.sum(p, axis=-1, keepdims=True)
    o = jnp.dot(p.astype(BF16), gather(v_refs, vn_ref), preferred_element_type=F32)
    o_ref[0] = (o / den).astype(o_ref.dtype)


def dsa_sample_attend(cache_k, cache_v, layer, page_table, q_rows, k_new, v_new, mask, bias_tab, q_norm_g):
    DB, n_pages = page_table.shape
    n_tok = k_new.shape[1] // A_KV_HEADS
    rows = n_tok * A_HEADS
    rows_pad = mask.shape[0] // DB
    L = mask.shape[1]
    page_rows = PAGE_SIZE * A_KV_HEADS
    page_specs = [pl.BlockSpec((None, None, page_rows, HEAD_DIM), functools.partial(
        lambda b, pt, p: (layer, pt[b, p], 0, 0), p=p)) for p in range(n_pages)]
    grid_spec = pltpu.PrefetchScalarGridSpec(
        num_scalar_prefetch=1,
        grid=(DB,),
        in_specs=page_specs + page_specs + [
            pl.BlockSpec((1, rows, HEAD_DIM), lambda b, pt: (b, 0, 0)),
            pl.BlockSpec((1, n_tok * A_KV_HEADS, HEAD_DIM), lambda b, pt: (b, 0, 0)),
            pl.BlockSpec((1, n_tok * A_KV_HEADS, HEAD_DIM), lambda b, pt: (b, 0, 0)),
            pl.BlockSpec((rows_pad, L), lambda b, pt: (b, 0)),
            pl.BlockSpec((rows, A_KV_HEADS * L), lambda b, pt: (0, 0), pipeline_mode=pl.Buffered(1)),
            pl.BlockSpec((1, HEAD_DIM), lambda b, pt: (0, 0), pipeline_mode=pl.Buffered(1))],
        out_specs=pl.BlockSpec((1, rows, HEAD_DIM), lambda b, pt: (b, 0, 0)),
    )
    return pl.pallas_call(
        functools.partial(_dsa_sample_attend_kernel, n_pages=n_pages, n_tok=n_tok, rows_pad=rows_pad),
        out_shape=jax.ShapeDtypeStruct((DB, rows, HEAD_DIM), BF16),
        grid_spec=grid_spec,
        compiler_params=_cparams(("parallel",)),
        name="dsa_sample_attend",
    )(page_table, *([cache_k] * n_pages), *([cache_v] * n_pages), q_rows, k_new, v_new, mask,
      bias_tab, q_norm_g.reshape(1, HEAD_DIM))


def _log_sigmoid(z):
    return jnp.minimum(z, 0.0) - jnp.log(1.0 + jnp.exp(-jnp.abs(z)))


def _seg_masks(seg):
    r = lax.broadcasted_iota(I32, (TILE, TILE), 0)
    c = lax.broadcasted_iota(I32, (TILE, TILE), 1)
    return r, c, (r // seg) == (c // seg)


def _gla_levels(seg):
    w, out = seg // 2, []
    while w >= 1:
        out.append(w)
        w //= 2
    return out


def _gla_sum_matrices(seg):
    r = jnp.arange(TILE)[:, None]
    c = jnp.arange(TILE)[None, :]
    mats = []
    for w in _gla_levels(seg):
        same = (r // (2 * w)) == (c // (2 * w))
        r_right = (r % (2 * w)) >= w
        c_right = (c % (2 * w)) >= w
        mats.append(same & r_right & c_right & (c <= r))
    for w in _gla_levels(seg):
        same = (r // (2 * w)) == (c // (2 * w))
        r_right = (r % (2 * w)) >= w
        c_right = (c % (2 * w)) >= w
        mats.append(same & (~r_right) & (~c_right) & (c > r))
    same_seg = (r // seg) == (c // seg)
    mats.append(same_seg & (c <= r))
    mats.append(same_seg & (c > r))
    return jnp.concatenate(mats, axis=0).astype(BF16)


def _bdot(a, b):
    return jnp.dot(a.astype(BF16), b.astype(BF16), preferred_element_type=F32)


def _bdot_nt(a, b):
    return lax.dot_general(a.astype(BF16), b.astype(BF16), (((1,), (1,)), ((), ())), preferred_element_type=F32)


def _gla_common(qb_ref, kb_ref, misc_ref, wg_ref, bg_ref, mats_ref, seg):
    gb = misc_ref[:, MISC_GB:MISC_GB + GATE_RANK]
    z = jnp.dot(gb, wg_ref[...], precision=HIGHEST, preferred_element_type=F32) + bg_ref[...]
    la = _log_sigmoid(z) / GATE_TEMP
    la_hi = la.astype(BF16)
    la_lo = (la - la_hi.astype(F32)).astype(BF16)
    mats = mats_ref[...]
    sums = (jnp.dot(mats, la_hi, preferred_element_type=F32) + jnp.dot(mats, la_lo, preferred_element_type=F32))
    levels = _gla_levels(seg)
    nl = len(levels)
    q = qb_ref[...] * B_DK ** -0.5
    k = kb_ref[...]
    r, c, _ = _seg_masks(seg)
    att = [jnp.where(r == c, _bdot_nt(q[:, h * B_DK:(h + 1) * B_DK], k[:, h * B_DK:(h + 1) * B_DK]), 0.0)
           for h in range(B_HEADS)]
    for li, w in enumerate(levels):
        pair = ((r // (2 * w)) == (c // (2 * w))) & ((r % (2 * w)) >= w) & ((c % (2 * w)) < w)
        qd = (q * jnp.exp(sums[li * TILE:(li + 1) * TILE])).astype(BF16)
        kd = (k * jnp.exp(sums[(nl + li) * TILE:(nl + li + 1) * TILE])).astype(BF16)
        for h in range(B_HEADS):
            hs = slice(h * B_DK, (h + 1) * B_DK)
            att[h] = att[h] + jnp.where(pair, _bdot_nt(qd[:, hs], kd[:, hs]), 0.0)
    b_cum = sums[2 * nl * TILE:(2 * nl + 1) * TILE]
    rem = sums[(2 * nl + 1) * TILE:(2 * nl + 2) * TILE]
    return q, k, att, b_cum, rem


def _gla_finish(o_heads, rb_ref, go_ref, o_ref):
    go = go_ref[...]
    for h in range(B_HEADS):
        vs = slice(h * B_DV, (h + 1) * B_DV)
        rb = rb_ref[:, vs]
        o_ref[:, vs] = (_rms(o_heads[h], go) * (rb * jax.nn.sigmoid(rb))).astype(o_ref.dtype)


def _gla_prompt_kernel(qb_ref, kb_ref, vb_ref, rb_ref, misc_ref, wg_ref, bg_ref, go_ref, mats_ref,
                       o_ref, s_ref, state_ref):
    ci = pl.program_id(1)

    @pl.when(ci == 0)
    def _():
        state_ref[...] = jnp.zeros_like(state_ref)

    q, k, att, b_cum, rem = _gla_common(qb_ref, kb_ref, misc_ref, wg_ref, bg_ref, mats_ref, TILE)
    v = vb_ref[...]
    state = state_ref[...]
    qe = q * jnp.exp(b_cum)
    o_heads = []
    for h in range(B_HEADS):
        ks = slice(h * B_DK, (h + 1) * B_DK)
        vs = slice(h * B_DV, (h + 1) * B_DV)
        o_heads.append(_bdot(qe[:, ks], state[ks, :]) + _bdot(att[h], v[:, vs]))
    _gla_finish(o_heads, rb_ref, go_ref, o_ref)

    ke_t = (k * jnp.exp(rem)).T
    e_last = jnp.exp(b_cum[TILE - 1:TILE, :])
    e_col = jnp.broadcast_to(e_last, (TILE, B_HEADS * B_DK)).T[:, 0:1]
    upd = jnp.concatenate(
        [_bdot(ke_t[h * B_DK:(h + 1) * B_DK, :], v[:, h * B_DV:(h + 1) * B_DV]) for h in range(B_HEADS)], axis=0)
    new_state = state * e_col + upd
    state_ref[...] = new_state
    s_ref[0] = new_state


def gla_prompt(proj, w_gate, b_gate, g_out, n_batch, seq):
    T = proj.shape[0]
    nc = seq // TILE
    kwid = B_HEADS * B_DK
    vwid = B_HEADS * B_DV
    mats = _gla_sum_matrices(TILE)
    o, s = pl.pallas_call(
        _gla_prompt_kernel,
        out_shape=(jax.ShapeDtypeStruct((T, vwid), BF16),
                   jax.ShapeDtypeStruct((n_batch, kwid, B_DV), F32)),
        grid=(n_batch, nc),
        in_specs=[pl.BlockSpec((TILE, kwid), lambda b, c: (b * nc + c, QB_OFF // kwid)),
                  pl.BlockSpec((TILE, kwid), lambda b, c: (b * nc + c, KB_OFF // kwid)),
                  pl.BlockSpec((TILE, vwid), lambda b, c: (b * nc + c, VB_OFF // vwid)),
                  pl.BlockSpec((TILE, vwid), lambda b, c: (b * nc + c, RB_OFF // vwid)),
                  pl.BlockSpec((TILE, LANES), lambda b, c: (b * nc + c, MISC_OFF // LANES)),
                  _resident((GATE_RANK, kwid)),
                  _resident((1, kwid)),
                  _resident((1, B_DV)),
                  _resident(mats.shape)],
        out_specs=(pl.BlockSpec((TILE, vwid), lambda b, c: (b * nc + c, 0)),
                   pl.BlockSpec((1, kwid, B_DV), lambda b, c: (b, 0, 0))),
        scratch_shapes=[pltpu.VMEM((kwid, B_DV), F32)],
        compiler_params=_cparams(("parallel", "arbitrary")),
        name="gla_prompt",
    )(proj, proj, proj, proj, proj, w_gate, b_gate.reshape(1, kwid), g_out.reshape(1, B_DV), mats)
    return o, s.reshape(n_batch, B_HEADS, B_DK, B_DV)


def _gla_sample_kernel(qb_ref, kb_ref, vb_ref, rb_ref, misc_ref, wg_ref, bg_ref, go_ref, mats_ref, s0_ref,
                       o_ref, s_ref, *, seg):
    nbt = TILE // seg
    q, k, att, b_cum, rem = _gla_common(qb_ref, kb_ref, misc_ref, wg_ref, bg_ref, mats_ref, seg)
    v = vb_ref[...]
    qe = q * jnp.exp(b_cum)
    ke = k * jnp.exp(rem)
    r1 = lax.broadcasted_iota(I32, (TILE, 1), 0)
    e_last = jnp.where(r1 % seg == seg - 1, jnp.exp(b_cum), 0.0)
    wide = nbt * B_DK
    mq = (lax.broadcasted_iota(I32, (TILE, wide), 0) // seg) == (lax.broadcasted_iota(I32, (TILE, wide), 1) // B_DK)
    mk = (lax.broadcasted_iota(I32, (wide, TILE), 0) // B_DK) == (lax.broadcasted_iota(I32, (wide, TILE), 1) // seg)
    o_heads = []
    for h in range(B_HEADS):
        ks = slice(h * B_DK, (h + 1) * B_DK)
        vs = slice(h * B_DV, (h + 1) * B_DV)
        state = s0_ref[:, h].reshape(wide, B_DV)
        q_bd = jnp.where(mq, jnp.concatenate([qe[:, ks]] * nbt, axis=1), 0.0)
        o_heads.append(_bdot(q_bd, state) + _bdot(att[h], v[:, vs]))
        pair_t = jnp.concatenate([ke[:, ks], e_last[:, ks]], axis=1).T
        k_bd = jnp.where(mk, jnp.concatenate([pair_t[:B_DK]] * nbt, axis=0), 0.0)
        e_bd = jnp.where(mk, jnp.concatenate([pair_t[B_DK:]] * nbt, axis=0), 0.0)
        e_col = jnp.sum(e_bd, axis=-1, keepdims=True)
        new_state = state * e_col + _bdot(k_bd, v[:, vs])
        s_ref[:, h] = new_state.reshape(nbt, B_DK, B_DV)
    _gla_finish(o_heads, rb_ref, go_ref, o_ref)


def gla_sample(proj, w_gate, b_gate, g_out, s0, n_tok):
    T = proj.shape[0]
    DB = s0.shape[0]
    nbt = TILE // n_tok
    kwid = B_HEADS * B_DK
    vwid = B_HEADS * B_DV
    mats = _gla_sum_matrices(n_tok)
    return pl.pallas_call(
        functools.partial(_gla_sample_kernel, seg=n_tok),
        out_shape=(jax.ShapeDtypeStruct((T, vwid), BF16),
                   jax.ShapeDtypeStruct(s0.shape, F32)),
        grid=(T // TILE,),
        in_specs=[pl.BlockSpec((TILE, kwid), lambda i: (i, QB_OFF // kwid)),
                  pl.BlockSpec((TILE, kwid), lambda i: (i, KB_OFF // kwid)),
                  pl.BlockSpec((TILE, vwid), lambda i: (i, VB_OFF // vwid)),
                  pl.BlockSpec((TILE, vwid), lambda i: (i, RB_OFF // vwid)),
                  pl.BlockSpec((TILE, LANES), lambda i: (i, MISC_OFF // LANES)),
                  _resident((GATE_RANK, kwid)),
                  _resident((1, kwid)),
                  _resident((1, B_DV)),
                  _resident(mats.shape),
                  pl.BlockSpec((nbt, B_HEADS, B_DK, B_DV), lambda i: (i, 0, 0, 0))],
        out_specs=(pl.BlockSpec((TILE, vwid), lambda i: (i, 0)),
                   pl.BlockSpec((nbt, B_HEADS, B_DK, B_DV), lambda i: (i, 0, 0, 0))),
        compiler_params=_cparams(("parallel",)),
        name="gla_sample",
    )(proj, proj, proj, proj, proj, w_gate, b_gate.reshape(1, kwid), g_out.reshape(1, B_DV), mats, s0)


def _gelu(x):
    return jax.nn.gelu(x)


def _gmlp_kernel(uc_ref, vc_ref, gv_ref, ws_ref, bcol_ref, o_ref, vn_ref, *, seg):
    r, c, same_seg = _seg_masks(seg)
    keep = same_seg & (c <= r)
    u = _gelu(uc_ref[...])
    vg = _gelu(vc_ref[...])
    for g in range(C_GROUPS):
        gs = slice(g * C_GROUP_DIM, (g + 1) * C_GROUP_DIM)
        vn = _rms(vg[:, gs], gv_ref[:, gs])
        vn_ref[:, gs] = vn
        w = jnp.where(keep, ws_ref[g], 0.0).astype(BF16)
        s = jnp.dot(w, vn.astype(BF16), preferred_element_type=F32) + bcol_ref[:, g:g + 1]
        o_ref[:, gs] = (u[:, gs] * s).astype(o_ref.dtype)


def gmlp(proj, g_v, w_tiles, b_cols, seg):
    T = proj.shape[0]
    cw = C_GROUPS * C_GROUP_DIM
    return pl.pallas_call(
        functools.partial(_gmlp_kernel, seg=seg),
        out_shape=(jax.ShapeDtypeStruct((T, cw), BF16),
                   jax.ShapeDtypeStruct((T, cw), F32)),
        grid=(T // TILE,),
        in_specs=[pl.BlockSpec((TILE, cw), lambda i: (i, UC_OFF // cw)),
                  pl.BlockSpec((TILE, cw), lambda i: (i, VC_OFF // cw)),
                  _resident((1, cw)),
                  _resident((C_GROUPS, TILE, TILE)),
                  _resident((TILE, C_GROUPS))],
        out_specs=(pl.BlockSpec((TILE, cw), lambda i: (i, 0)),
                   pl.BlockSpec((TILE, cw), lambda i: (i, 0))),
        compiler_params=_cparams(("parallel",)),
        name="gmlp",
    )(proj, proj, g_v.reshape(1, cw), w_tiles, b_cols)


def _out_kernel(h_ref, oa_ref, ob_ref, oc_ref, w_ref, o_ref):
    aw = oa_ref.shape[1]
    bw = ob_ref.shape[1]
    acc = jnp.dot(oa_ref[...], w_ref[0:aw, :], preferred_element_type=F32)
    acc = acc + jnp.dot(ob_ref[...], w_ref[aw:aw + bw, :], preferred_element_type=F32)
    acc = acc + jnp.dot(oc_ref[...], w_ref[aw + bw:, :], preferred_element_type=F32)
    o_ref[...] = h_ref[...] + acc


def out_projection(h, o_a, o_b, o_c, w_out, tm):
    T, D = h.shape
    tm = min(tm, T)
    return pl.pallas_call(
        _out_kernel,
        out_shape=jax.ShapeDtypeStruct((T, D), F32),
        grid=(T // tm,),
        in_specs=[pl.BlockSpec((tm, D), lambda i: (i, 0)),
                  pl.BlockSpec((tm, o_a.shape[1]), lambda i: (i, 0)),
                  pl.BlockSpec((tm, o_b.shape[1]), lambda i: (i, 0)),
                  pl.BlockSpec((tm, o_c.shape[1]), lambda i: (i, 0)),
                  _resident(w_out.shape)],
        out_specs=pl.BlockSpec((tm, D), lambda i: (i, 0)),
        compiler_params=_cparams(("parallel",)),
        name="out_projection",
    )(h, o_a, o_b, o_c, w_out)


def _ffn_kernel(h_ref, g_ref, wg_ref, wu_ref, wd_ref, o_ref, n_ref):
    j = pl.program_id(1)

    @pl.when(j == 0)
    def _():
        h = h_ref[...]
        n_ref[...] = _rms(h, g_ref[...]).astype(BF16)
        o_ref[...] = h

    n = n_ref[...]
    a = jnp.dot(n, wg_ref[...], preferred_element_type=F32)
    u = jnp.dot(n, wu_ref[...], preferred_element_type=F32)
    act = (a * jax.nn.sigmoid(a) * u).astype(BF16)
    o_ref[...] += jnp.dot(act, wd_ref[...], preferred_element_type=F32)


def ffn(h, g, w_gate, w_up, w_down, tm, tf):
    T, D = h.shape
    tm = min(tm, T)
    FF = w_gate.shape[1]
    return pl.pallas_call(
        _ffn_kernel,
        out_shape=jax.ShapeDtypeStruct((T, D), F32),
        grid=(T // tm, FF // tf),
        in_specs=[pl.BlockSpec((tm, D), lambda i, j: (i, 0)),
                  _resident((1, D)),
                  pl.BlockSpec((D, tf), lambda i, j: (0, j)),
                  pl.BlockSpec((D, tf), lambda i, j: (0, j)),
                  pl.BlockSpec((tf, D), lambda i, j: (j, 0))],
        out_specs=pl.BlockSpec((tm, D), lambda i, j: (i, 0)),
        scratch_shapes=[pltpu.VMEM((tm, D), BF16)],
        compiler_params=_cparams(("parallel", "arbitrary")),
        name="ffn",
    )(h, g.reshape(1, D), w_gate, w_up, w_down)


def _ple_kernel(h_ref, p_ref, g_ref, wgate_ref, wproj_ref, o_ref):
    h = h_ref[...]
    n = _rms(h, g_ref[...]).astype(BF16)
    gate = jax.nn.sigmoid(jnp.dot(n, wgate_ref[...], preferred_element_type=F32))
    emb = jnp.dot(p_ref[...].astype(BF16), wproj_ref[...], preferred_element_type=F32)
    o_ref[...] = h + gate * emb


def ple(h, p, g, w_gate, w_proj, tm):
    T, D = h.shape
    tm = min(tm, T)
    P = p.shape[1]
    return pl.pallas_call(
        _ple_kernel,
        out_shape=jax.ShapeDtypeStruct((T, D), F32),
        grid=(T // tm,),
        in_specs=[pl.BlockSpec((tm, D), lambda i: (i, 0)),
                  pl.BlockSpec((tm, P), lambda i: (i, 0)),
                  _resident((1, D)),
                  _resident(w_gate.shape),
                  _resident(w_proj.shape)],
        out_specs=pl.BlockSpec((tm, D), lambda i: (i, 0)),
        compiler_params=_cparams(("parallel",)),
        name="ple",
    )(h, p, g.reshape(1, D), w_gate, w_proj)


def _pack_w_in(w):
    sizes = (1024, 256, 256, 1024, 64, 16, 256, 256, 512, 16, 512, 512, 512)
    names = ("q", "k", "v", "qi", "ki", "wi", "qb", "kb", "vb", "gb", "rb", "uc", "vc")
    seg, start = {}, 0
    for nme, sz in zip(names, sizes):
        seg[nme] = w[:, start:start + sz]
        start += sz
    pad = jnp.zeros((w.shape[0], LANES - IDX_DIM - IDX_HEADS - GATE_RANK), w.dtype)
    order = ("q", "qi", "vb", "rb", "uc", "vc", "k", "v", "qb", "kb", "ki", "wi", "gb")
    return jnp.concatenate([seg[n] for n in order] + [pad], axis=1).astype(BF16)


def _mixer_tail(h, o_a, o_b, o_c, p_l, lw, tm, tm_ffn):
    h = out_projection(h, o_a, o_b, o_c, lw["w_out"], tm)
    h = ffn(h, lw["g_ffn"], lw["w_ffn_gate"], lw["w_ffn_up"], lw["w_ffn_down"], tm_ffn, 512)
    return ple(h, p_l, lw["g_ple"], lw["w_ple_gate"], lw["w_ple_proj"], tm)


def kernel(x_prompt, x_sample, cache_k, cache_v, cache_idx_k, state_gla, page_table, p_prompt, p_sample,
           g_mix, w_in, q_norm_g, k_norm_g, rel_bias, w_gate_b, b_gate_b, g_out_b, g_v_c, w_spatial,
           b_spatial, w_out, g_ffn, w_ffn_gate, w_ffn_up, w_ffn_down, g_ple, w_ple_gate, w_ple_proj):
    n_batch, seq, d_model = x_prompt.shape
    dec_batch, dec_seq, _ = x_sample.shape
    depth = w_in.shape[0]
    n_pages = page_table.shape[1]
    past = n_pages * PAGE_SIZE
    kw = A_KV_HEADS * HEAD_DIM
    tp, ts = n_batch * seq, dec_batch * dec_seq
    rows_pad = SUBLANES
    l_sample = past + PAGE_SIZE
    k_sel_s = min(TOPK_MAX, (past + dec_seq) // 4)

    bias_p = bias_table_prompt(rel_bias)
    bias_s = bias_table_sample(rel_bias, past, dec_seq, A_KV_HEADS * l_sample)
    cache_ik_t = jnp.swapaxes(cache_idx_k, 2, 3)
    cache_k2 = cache_k.reshape(depth, cache_k.shape[1], PAGE_SIZE * A_KV_HEADS, HEAD_DIM)
    cache_v2 = cache_v.reshape(depth, cache_v.shape[1], PAGE_SIZE * A_KV_HEADS, HEAD_DIM)

    hp = x_prompt.reshape(tp, d_model)
    hs = x_sample.reshape(ts, d_model)
    outs = {k: [] for k in ("kp", "vp", "ikp", "sp", "ks", "vs", "iks", "ss", "cs")}
    eye_t = jnp.eye(rows_pad, dec_seq, dtype=F32)
    for i in range(depth):
        lw = dict(w_out=w_out[i].astype(BF16), g_ffn=g_ffn[i], w_ffn_gate=w_ffn_gate[i].astype(BF16),
                  w_ffn_up=w_ffn_up[i].astype(BF16), w_ffn_down=w_ffn_down[i].astype(BF16),
                  g_ple=g_ple[i], w_ple_gate=w_ple_gate[i].astype(BF16),
                  w_ple_proj=w_ple_proj[i].astype(BF16))
        w_packed = _pack_w_in(w_in[i])
        b_cols_p = b_spatial[i].T
        reps = TILE // dec_seq
        w_tiles_s = jnp.tile(w_spatial[i][:, :dec_seq, :dec_seq], (1, reps, reps))
        b_cols_s = jnp.tile(b_spatial[i][:, :dec_seq].T, (reps, 1))

        proj = in_projection(hp, g_mix[i], w_packed, 256)
        kn, vv, ik, vt = kv_post(proj, k_norm_g[i], 512, True)
        o_a = dsa_prompt(proj, kn, vt, bias_p, q_norm_g[i], n_batch, seq)
        o_b, s_p = gla_prompt(proj, w_gate_b[i], b_gate_b[i], g_out_b[i], n_batch, seq)
        o_c, _ = gmlp(proj, g_v_c[i], w_spatial[i], b_cols_p, TILE)
        hp = _mixer_tail(hp, o_a, o_b, o_c, p_prompt[i].reshape(tp, -1), lw, 256, 512)
        outs["kp"].append(kn.reshape(n_batch, seq, A_KV_HEADS, HEAD_DIM))
        outs["vp"].append(vv.reshape(n_batch, seq, A_KV_HEADS, HEAD_DIM))
        outs["ikp"].append(ik.reshape(n_batch, seq, IDX_DIM))
        outs["sp"].append(s_p)

        proj = in_projection(hs, g_mix[i], w_packed, 256)
        kn, vv, ik = kv_post(proj, k_norm_g[i], 512, False)
        qi_rows =proj[:, QI_OFF:QI_OFF + IDX_HEADS * IDX_DIM].reshape(dec_batch, dec_seq * IDX_HEADS, IDX_DIM)
        wi = proj[:, MISC_OFF + MISC_WI:MISC_OFF + MISC_WI + IDX_HEADS].reshape(dec_batch, dec_seq, IDX_HEADS)
        wi = wi * (IDX_HEADS ** -0.5 * IDX_DIM ** -0.5)
        wmat = (eye_t[None, :, :, None] * wi[:, None, :, :]).reshape(dec_batch, rows_pad, dec_seq * IDX_HEADS)
        ki_new_t = jnp.pad(jnp.swapaxes(ik.reshape(dec_batch, dec_seq, IDX_DIM), 1, 2),
                           ((0, 0), (0, 0), (0, PAGE_SIZE - dec_seq)))
        mask = dsa_sample_select(cache_ik_t, i, page_table, qi_rows, wmat, ki_new_t, dec_seq, k_sel_s)
        q_rows = proj[:, Q_OFF:Q_OFF + A_HEADS * HEAD_DIM].reshape(dec_batch, dec_seq * A_HEADS, HEAD_DIM)
        o_a = dsa_sample_attend(cache_k2, cache_v2, i, page_table, q_rows,
                                kn.reshape(dec_batch, dec_seq * A_KV_HEADS, HEAD_DIM),
                                vv.reshape(dec_batch, dec_seq * A_KV_HEADS, HEAD_DIM), mask, bias_s, q_norm_g[i])
        o_a = o_a.reshape(ts, A_HEADS * HEAD_DIM)
        o_b, s_s = gla_sample(proj, w_gate_b[i], b_gate_b[i], g_out_b[i], state_gla[i], dec_seq)
        o_c, vn = gmlp(proj, g_v_c[i], w_tiles_s, b_cols_s, dec_seq)
        hs = _mixer_tail(hs, o_a, o_b, o_c, p_sample[i].reshape(ts, -1), lw, 256, 512)
        outs["ks"].append(kn.reshape(dec_batch, dec_seq, A_KV_HEADS, HEAD_DIM))
        outs["vs"].append(vv.reshape(dec_batch, dec_seq, A_KV_HEADS, HEAD_DIM))
        outs["iks"].append(ik.reshape(dec_batch, dec_seq, IDX_DIM))
        outs["ss"].append(s_s)
        outs["cs"].append(vn.reshape(dec_batch, dec_seq, -1))

    st = {k: jnp.stack(v) for k, v in outs.items()}
    return (hp.reshape(n_batch, seq, d_model), hs.reshape(dec_batch, dec_seq, d_model),
            st["kp"], st["vp"], st["ikp"], st["sp"], st["ks"], st["vs"], st["iks"], st["ss"], st["cs"])
```

```python
import functools
import math

import jax
import jax.numpy as jnp
from jax import lax
from jax.experimental import pallas as pl
from jax.experimental.pallas import tpu as pltpu

F32 = jnp.float32
BF16 = jnp.bfloat16
I32 = jnp.int32
HIGHEST = lax.Precision.HIGHEST

LANES = 128
SUBLANES = 8
VMEM_LIMIT = 56 * 1024 * 1024

HEAD_DIM = 128
A_HEADS = 8
A_KV_HEADS = 2
IDX_HEADS = 16
IDX_DIM = 64
TOPK_MAX = 256
NUM_BUCKETS = 32
MAX_DISTANCE = 128
B_HEADS = 4
B_DK = 64
B_DV = 128
GATE_RANK = 16
GATE_TEMP = 16.0
C_GROUPS = 4
C_GROUP_DIM = 128
PAGE_SIZE = 128
EPS = 1e-6
NEG_BIG = -1e30
INT_MIN = -(2 ** 31)

TILE = 128
QBLK = 256

Q_OFF, QI_OFF, VB_OFF, RB_OFF, UC_OFF, VC_OFF = 0, 1024, 2048, 2560, 3072, 3584
K_OFF, V_OFF, QB_OFF, KB_OFF, MISC_OFF = 4096, 4352, 4608, 4864, 5120
PROJ_PACKED = 5248
MISC_KI, MISC_WI, MISC_GB = 0, 64, 80


def _cparams(sem):
    return pltpu.CompilerParams(dimension_semantics=sem, vmem_limit_bytes=VMEM_LIMIT)


def _rms(x, g):
    return x * lax.rsqrt(jnp.mean(x * x, axis=-1, keepdims=True) + EPS) * g


def _resident(shape):
    nd = len(shape)
    return pl.BlockSpec(shape, lambda *_: (0,) * nd, pipeline_mode=pl.Buffered(1))


def _layer_resident(shape, layer):
    nd = len(shape)
    return pl.BlockSpec((None,) + tuple(shape), lambda *_: (layer,) + (0,) * nd, pipeline_mode=pl.Buffered(1))


def _proj_kernel(x_ref, g_ref, w_ref, o_ref):
    n = _rms(x_ref[...], g_ref[...]).astype(BF16)
    ncol = o_ref.shape[1]
    step = 512
    for c0 in range(0, ncol, step):
        c1 = min(c0 + step, ncol)
        o_ref[:, c0:c1] = jnp.dot(n, w_ref[:, c0:c1], preferred_element_type=F32)


def in_projection(h, g, w_packed, layer, tm):
    T, D = h.shape
    tm = min(tm, T)
    N = w_packed.shape[2]
    return pl.pallas_call(
        _proj_kernel,
        out_shape=jax.ShapeDtypeStruct((T, N), F32),
        grid=(T // tm,),
        in_specs=[pl.BlockSpec((tm, D), lambda i: (i, 0)),
                  _resident((1, D)),
                  _layer_resident((D, N), layer)],
        out_specs=pl.BlockSpec((tm, N), lambda i: (i, 0)),
        compiler_params=_cparams(("parallel",)),
        name="in_projection",
    )(h, g.reshape(1, D), w_packed)


def _kv_kernel(k_ref, v_ref, m_ref, g_ref, ko_ref, vo_ref, io_ref, vt_ref=None):
    g = g_ref[...]
    k = k_ref[...]
    for hh in range(A_KV_HEADS):
        sl = slice(hh * HEAD_DIM, (hh + 1) * HEAD_DIM)
        ko_ref[:, sl] = _rms(k[:, sl], g)
    v = v_ref[...]
    vo_ref[...] = v
    io_ref[...] = m_ref[:, MISC_KI:MISC_KI + IDX_DIM]
    if vt_ref is not None:
        for blk in range(vt_ref.shape[0]):
            vt_ref[blk] = v[blk * QBLK:(blk + 1) * QBLK, :].T.astype(vt_ref.dtype)


def kv_post(proj, k_norm_g, tm, with_vt):
    T = proj.shape[0]
    tm = min(tm, T)
    kw = A_KV_HEADS * HEAD_DIM
    out_shape = [jax.ShapeDtypeStruct((T, kw), F32),
                 jax.ShapeDtypeStruct((T, kw), F32),
                 jax.ShapeDtypeStruct((T, IDX_DIM), F32)]
    out_specs = [pl.BlockSpec((tm, kw), lambda i: (i, 0)),
                 pl.BlockSpec((tm, kw), lambda i: (i, 0)),
                 pl.BlockSpec((tm, IDX_DIM), lambda i: (i, 0))]
    if with_vt:
        out_shape.append(jax.ShapeDtypeStruct((T // QBLK, kw, QBLK), BF16))
        out_specs.append(pl.BlockSpec((tm // QBLK, kw, QBLK), lambda i: (i, 0, 0)))
    return pl.pallas_call(
        _kv_kernel,
        out_shape=tuple(out_shape),
        grid=(T // tm,),
        in_specs=[pl.BlockSpec((tm, kw), lambda i: (i, K_OFF // kw)),
                  pl.BlockSpec((tm, kw), lambda i: (i, V_OFF // kw)),
                  pl.BlockSpec((tm, LANES), lambda i: (i, MISC_OFF // LANES)),
                  _resident((1, HEAD_DIM))],
        out_specs=tuple(out_specs),
        compiler_params=_cparams(("parallel",)),
        name="kv_post",
    )(proj, proj, proj, k_norm_g.reshape(1, HEAD_DIM))


def _bucket(dist):
    n = jnp.maximum(dist, 0)
    max_exact = NUM_BUCKETS // 2
    large = max_exact + (jnp.log(jnp.maximum(n, 1).astype(F32) / max_exact)
                         / math.log(MAX_DISTANCE / max_exact)
                         * (NUM_BUCKETS - max_exact)).astype(I32)
    large = jnp.minimum(large, NUM_BUCKETS - 1)
    return jnp.where(n < max_exact, n, large)


def _bias_prompt_kernel(rb_ref, o_ref):
    c = lax.broadcasted_iota(I32, (TILE, TILE), 0)
    t = lax.broadcasted_iota(I32, (TILE, TILE), 1)
    for z in range(3):
        bucket = _bucket(t - c + (2 - z) * TILE)
        for h in range(A_HEADS):
            acc = jnp.zeros((TILE, TILE), F32)
            for b in range(NUM_BUCKETS):
                acc = jnp.where(bucket == b, rb_ref[b, h], acc)
            o_ref[h, z] = acc


def bias_table_prompt(rel_bias):
    return pl.pallas_call(
        _bias_prompt_kernel,
        out_shape=jax.ShapeDtypeStruct((A_HEADS, 3, TILE, TILE), F32),
        in_specs=[pl.BlockSpec(memory_space=pltpu.SMEM)],
        out_specs=pl.BlockSpec(memory_space=pltpu.VMEM),
        name="bias_table_prompt",
    )(rel_bias)


def _bias_sample_kernel(rbrows_ref, o_ref, *, past, n_tok):
    rows, L = o_ref.shape
    r = lax.broadcasted_iota(I32, (rows, L), 0)
    s = lax.broadcasted_iota(I32, (rows, L), 1) // A_KV_HEADS
    bucket = _bucket(past + r // A_HEADS - s)
    rbrows = rbrows_ref[...]
    acc = jnp.zeros((rows, L), F32)
    for b in range(NUM_BUCKETS):
        acc = jnp.where(bucket == b, rbrows[:, b:b + 1], acc)
    o_ref[...] = acc


def bias_table_sample(rel_bias, past, n_tok, L):
    rows = n_tok * A_HEADS
    rbrows = jnp.tile(rel_bias.T, (n_tok, 1))
    return pl.pallas_call(
        functools.partial(_bias_sample_kernel, past=past, n_tok=n_tok),
        out_shape=jax.ShapeDtypeStruct((rows, L), F32),
        name="bias_table_sample",
    )(rbrows)


def _sortable_key(x):
    b = lax.bitcast_convert_type(x, I32)
    return b ^ ((b >> 31) & 0x7FFFFFFF)


def _topk_member(skey_ref, k_sel):
    R, L = skey_ref.shape

    def body(it, ans):
        bit = 31 - it
        cand = ans | lax.shift_left(jnp.int32(1), bit)
        cand_s = cand ^ INT_MIN
        cnt = jnp.sum(jnp.where(skey_ref[...] >= cand_s, 1.0, 0.0), axis=-1, keepdims=True)
        return jnp.where(cnt >= k_sel, cand, ans)

    ans = lax.fori_loop(0, 32, body, jnp.zeros((R, 1), I32))
    tau = ans ^ INT_MIN
    skey = skey_ref[...]
    gt = skey > tau
    eq = skey == tau
    n_gt = jnp.sum(jnp.where(gt, 1.0, 0.0), axis=-1, keepdims=True)
    room = k_sel - n_gt
    r_i = lax.broadcasted_iota(I32, (LANES, LANES), 0)
    c_i = lax.broadcasted_iota(I32, (LANES, LANES), 1)
    upper = jnp.where(r_i <= c_i, 1.0, 0.0).astype(BF16)
    off = jnp.zeros((R, 1), F32)
    parts = []
    for j in range(L // LANES):
        sl = slice(j * LANES, (j + 1) * LANES)
        eq_j = eq[:, sl]
        run = jnp.dot(jnp.where(eq_j, 1.0, 0.0).astype(BF16), upper, preferred_element_type=F32) + off
        parts.append(gt[:, sl] | (eq_j & (run <= room)))
        off = run[:, LANES - 1:LANES]
    return jnp.concatenate(parts, axis=1)


def _fold8(x, op):
    return op(x.reshape(x.shape[0] // SUBLANES, SUBLANES, x.shape[1]), axis=0)


def _dsa_prompt_kernel(q_ref, qi_ref, misc_ref, kn_ref, vt_ref, bias_ref, qg_ref, o_ref,
                       qst_ref, skey_ref, madd_ref, lg_ref, acc_ref, *, k_sel):
    i = pl.program_id(1)
    nkb = i + 1
    sub = QBLK // TILE
    rep = A_HEADS // A_KV_HEADS
    row0 = pl.multiple_of(i * QBLK, QBLK)
    s_iota = lax.broadcasted_iota(I32, (QBLK, QBLK), 0)
    t_iota = lax.broadcasted_iota(I32, (QBLK, QBLK), 1)

    def admissible(j):
        return (j * QBLK + s_iota) <= (row0 + t_iota)

    wi_t = misc_ref[pl.ds(row0, QBLK), :].T[MISC_WI:MISC_WI + IDX_HEADS, :]
    wi_t = wi_t * (IDX_HEADS ** -0.5 * IDX_DIM ** -0.5)
    for h in range(IDX_HEADS):
        qst_ref[h * QBLK:(h + 1) * QBLK, :] = qi_ref[:, h * IDX_DIM:(h + 1) * IDX_DIM].astype(BF16)

    def score_body(j, carry):
        k0 = pl.multiple_of(j * QBLK, QBLK)
        kj = misc_ref[pl.ds(k0, QBLK), MISC_KI:MISC_KI + IDX_DIM].astype(BF16)
        s = lax.dot_general(kj, qst_ref[...], (((1,), (1,)), ((), ())), preferred_element_type=F32)
        score = jnp.zeros((QBLK, QBLK), F32)
        for h in range(IDX_HEADS):
            score = score + jnp.maximum(s[:, h * QBLK:(h + 1) * QBLK], 0.0) * wi_t[h:h + 1, :]
        skey_ref[j] = _sortable_key(jnp.where(admissible(j), score, -jnp.inf))
        return carry

    lax.fori_loop(0, nkb, score_body, 0)

    def count(pred_fn):
        def body(j, acc):
            return acc + _fold8(jnp.where(pred_fn(skey_ref[j]), 1.0, 0.0), jnp.sum)
        acc = lax.fori_loop(0, nkb, body, jnp.zeros((SUBLANES, QBLK), F32))
        return jnp.sum(acc, axis=0, keepdims=True)

    def bit_body(it, ans):
        cand = ans | lax.shift_left(jnp.int32(1), 31 - it)
        cand_s = cand ^ INT_MIN
        cnt = count(lambda key: key >= cand_s)
        return jnp.where(cnt >= k_sel, cand, ans)

    ans = lax.fori_loop(0, 32, bit_body, jnp.zeros((1, QBLK), I32))
    tau = ans ^ INT_MIN
    room = k_sel - count(lambda key: key > tau)
    lower = jnp.where(t_iota <= s_iota, 1.0, 0.0).astype(BF16)

    def mask_body(j, off):
        key = skey_ref[j]
        eq = key == tau
        run = jnp.dot(lower, jnp.where(eq, 1.0, 0.0).astype(BF16), preferred_element_type=F32) + off
        sel = ((key > tau) | (eq & (run <= room))) & admissible(j)
        madd_ref[j] = jnp.where(sel, 0.0, NEG_BIG)
        return run[QBLK - 1:QBLK, :]

    lax.fori_loop(0, nkb, mask_body, jnp.zeros((1, QBLK), F32))

    qg = qg_ref[...]
    wide = rep * QBLK
    for g in range(A_KV_HEADS):
        gs = slice(g * HEAD_DIM, (g + 1) * HEAD_DIM)
        heads = list(range(g * rep, (g + 1) * rep))
        q_stack = jnp.concatenate(
            [(_rms(q_ref[:, h * HEAD_DIM:(h + 1) * HEAD_DIM], qg) * HEAD_DIM ** -0.5).astype(BF16) for h in heads],
            axis=0)

        def logit_body(j, mx):
            k0 = pl.multiple_of(j * QBLK, QBLK)
            kj = kn_ref[pl.ds(k0, QBLK), gs].astype(BF16)
            lg = lax.dot_general(kj, q_stack, (((1,), (1,)), ((), ())), preferred_element_type=F32)
            madd = madd_ref[j]
            parts = []
            for r, h in enumerate(heads):
                quads = []
                for c in range(sub):
                    quads.append(jnp.concatenate(
                        [bias_ref[h, jnp.clip(2 - ((i - j) * sub + u - c), 0, 2)] for u in range(sub)], axis=1))
                parts.append(lg[:, r * QBLK:(r + 1) * QBLK] + jnp.concatenate(quads, axis=0) + madd)
            lg = jnp.concatenate(parts, axis=1)
            lg_ref[j] = lg
            return jnp.maximum(mx, _fold8(lg, jnp.max))

        mx = lax.fori_loop(0, nkb, logit_body, jnp.full((SUBLANES, wide), NEG_BIG, F32))
        m = jnp.max(mx, axis=0, keepdims=True)
        acc_ref[...] = jnp.zeros(acc_ref.shape, F32)

        def pv_body(j, sm):
            p = jnp.exp(lg_ref[j] - m)
            acc_ref[...] += jnp.dot(vt_ref[j, gs, :], p.astype(BF16), preferred_element_type=F32)
            return sm + _fold8(p, jnp.sum)

        sm = lax.fori_loop(0, nkb, pv_body, jnp.zeros((SUBLANES, wide), F32))
        den = jnp.sum(sm, axis=0, keepdims=True)
        o = (acc_ref[...] / den).T
        for r, h in enumerate(heads):
            o_ref[:, h * HEAD_DIM:(h + 1) * HEAD_DIM] = o[r * QBLK:(r + 1) * QBLK, :].astype(o_ref.dtype)


def dsa_prompt(proj, kn, vt, bias_tab, q_norm_g, n_batch, seq):
    T = proj.shape[0]
    nb = seq // QBLK
    k_sel = min(TOPK_MAX, seq // 4)
    aw = A_HEADS * HEAD_DIM
    iw = IDX_HEADS * IDX_DIM
    kw = A_KV_HEADS * HEAD_DIM
    rep = A_HEADS // A_KV_HEADS
    return pl.pallas_call(
        functools.partial(_dsa_prompt_kernel, k_sel=k_sel),
        out_shape=jax.ShapeDtypeStruct((T, aw), BF16),
        grid=(n_batch, nb),
        in_specs=[pl.BlockSpec((QBLK, aw), lambda b, i: (b * nb + i, Q_OFF // aw)),
                  pl.BlockSpec((QBLK, iw), lambda b, i: (b * nb + i, QI_OFF // iw)),
                  pl.BlockSpec((seq, LANES), lambda b, i: (b, MISC_OFF // LANES)),
                  pl.BlockSpec((seq, kw), lambda b, i: (b, 0)),
                  pl.BlockSpec((nb, kw, QBLK), lambda b, i: (b, 0, 0)),
                  _resident((A_HEADS, 3, TILE, TILE)),
                  _resident((1, HEAD_DIM))],
        out_specs=pl.BlockSpec((QBLK, aw), lambda b, i: (b * nb + i, 0)),
        scratch_shapes=[pltpu.VMEM((IDX_HEADS * QBLK, IDX_DIM), BF16),
                        pltpu.VMEM((nb, QBLK, QBLK), I32),
                        pltpu.VMEM((nb, QBLK, QBLK), F32),
                        pltpu.VMEM((nb, QBLK, rep * QBLK), F32),
                        pltpu.VMEM((HEAD_DIM, rep * QBLK), F32)],
        compiler_params=_cparams(("parallel", "arbitrary")),
        name="dsa_prompt",
    )(proj, proj, proj, kn, vt, bias_tab, q_norm_g.reshape(1, HEAD_DIM))


def _dsa_sample_select_kernel(pt_ref, *refs, n_pages, n_tok, k_sel, rows_pad):
    del pt_ref
    page_refs = refs[:n_pages]
    qi_ref, wm_ref, kin_ref, mask_ref, sc_ref, skey_ref = refs[n_pages:]
    b = pl.program_id(0)
    nb = pl.num_programs(0)
    L = mask_ref.shape[1]
    past = n_pages * PAGE_SIZE
    per = rows_pad // n_tok

    kt_all = jnp.concatenate([r[...].astype(BF16) for r in page_refs] + [kin_ref[0].astype(BF16)], axis=1)
    relu_s = jnp.maximum(jnp.dot(qi_ref[0].astype(BF16), kt_all, preferred_element_type=F32), 0.0)
    score = jnp.dot(wm_ref[0], relu_s, precision=HIGHEST, preferred_element_type=F32)
    r0 = pl.multiple_of((b // per) * rows_pad, rows_pad)

    @pl.when(b % per == 0)
    def _():
        sc_ref[pl.ds(r0, rows_pad), :] = score

    @pl.when(b % per != 0)
    def _():
        sc_ref[pl.ds(r0, rows_pad), :] += score

    @pl.when(b == nb - 1)
    def _():
        n_blocks = sc_ref.shape[0] // TILE
        tp = past + lax.broadcasted_iota(I32, (TILE, L), 0) % n_tok
        sp = lax.broadcasted_iota(I32, (TILE, L), 1)
        adm_blk = sp <= tp
        for rb in range(n_blocks):
            rows = slice(rb * TILE, (rb + 1) * TILE)
            skey_ref[...] = _sortable_key(jnp.where(adm_blk, sc_ref[rows, :], -jnp.inf))
            sel = _topk_member(skey_ref, k_sel) & adm_blk
            mask_ref[rows, :] = jnp.where(sel, 1.0, 0.0)


def dsa_sample_select(cache_ik_t, layer, page_table, qi_rows, wmat, ki_new_t, n_tok, k_sel):
    DB, n_pages = page_table.shape
    rows_pad = wmat.shape[1]
    n_rows = DB // (rows_pad // n_tok) * rows_pad
    L = (n_pages + 1) * PAGE_SIZE
    page_specs = [pl.BlockSpec((None, None, IDX_DIM, PAGE_SIZE), functools.partial(
        lambda b, pt, p: (layer, pt[b, p], 0, 0), p=p)) for p in range(n_pages)]
    grid_spec = pltpu.PrefetchScalarGridSpec(
        num_scalar_prefetch=1,
        grid=(DB,),
        in_specs=page_specs + [
            pl.BlockSpec((1,) + qi_rows.shape[1:], lambda b, pt: (b, 0, 0)),
            pl.BlockSpec((1,) + wmat.shape[1:], lambda b, pt: (b, 0, 0)),
            pl.BlockSpec((1, IDX_DIM, PAGE_SIZE), lambda b, pt: (b, 0, 0))],
        out_specs=pl.BlockSpec((n_rows, L), lambda b, pt: (0, 0)),
        scratch_shapes=[pltpu.VMEM((n_rows, L), F32),
                        pltpu.VMEM((TILE, L), I32)],
    )
    return pl.pallas_call(
        functools.partial(_dsa_sample_select_kernel, n_pages=n_pages, n_tok=n_tok, k_sel=k_sel,
                          rows_pad=rows_pad),
        out_shape=jax.ShapeDtypeStruct((n_rows, L), F32),
        grid_spec=grid_spec,
        compiler_params=_cparams(("arbitrary",)),
        name="dsa_sample_select",
    )(page_table, *([cache_ik_t] * n_pages), qi_rows, wmat, ki_new_t)


def _dsa_sample_attend_kernel(pt_ref, *refs, n_pages, n_tok, rows_pad):
    del pt_ref
    k_refs = refs[:n_pages]
    v_refs = refs[n_pages:2 * n_pages]
    q_ref, kn_ref, vn_ref, mask_ref, bias_ref, qg_ref, o_ref = refs[2 * n_pages:]
    rows = n_tok * A_HEADS
    page_rows = PAGE_SIZE * A_KV_HEADS
    n_tiles = mask_ref.shape[1] // PAGE_SIZE
    pad = jnp.zeros((page_rows - n_tok * A_KV_HEADS, HEAD_DIM), BF16)

    def gather(page_refs, new_ref):
        return jnp.concatenate([r[...].astype(BF16) for r in page_refs]
                               + [new_ref[0].astype(BF16), pad], axis=0)

    q = (_rms(q_ref[0], qg_ref[...]) * HEAD_DIM ** -0.5).astype(BF16)
    rep = A_HEADS // A_KV_HEADS
    grp = (lax.broadcasted_iota(I32, (rows, 1), 0) % A_HEADS) // rep
    first = (pl.program_id(0) % (rows_pad // n_tok)) * n_tok
    e_r = lax.broadcasted_iota(I32, (rows, rows_pad), 0) // A_HEADS
    e_c = lax.broadcasted_iota(I32, (rows, rows_pad), 1)
    expand = jnp.where(e_r + first == e_c, 1.0, 0.0).astype(BF16)
    sel = jnp.dot(expand, mask_ref[...].astype(BF16), preferred_element_type=F32)
    d_r = lax.broadcasted_iota(I32, (PAGE_SIZE, page_rows), 0)
    d_c = lax.broadcasted_iota(I32, (PAGE_SIZE, page_rows), 1)
    dup = jnp.where(d_c // A_KV_HEADS == d_r, 1.0, 0.0).astype(BF16)
    stacked = jnp.concatenate([sel[:, j * PAGE_SIZE:(j + 1) * PAGE_SIZE] for j in range(n_tiles)], axis=0)
    stacked = jnp.dot(stacked.astype(BF16), dup, preferred_element_type=F32)
    sel = jnp.concatenate([stacked[j * rows:(j + 1) * rows, :] for j in range(n_tiles)], axis=1)
    col_grp = lax.broadcasted_iota(I32, (rows, n_tiles * page_rows), 1) % A_KV_HEADS
    valid = (sel > 0.5) & (col_grp == grp)
    logits = lax.dot_general(q, gather(k_refs, kn_ref), (((1,), (1,)), ((), ())), preferred_element_type=F32)
    logits = jnp.where(valid, logits + bias_ref[...], NEG_BIG)
    m = jnp.max(logits, axis=-1, keepdims=True)
    p = jnp.exp(logits - m)
    den = jnp.sum(p, axis=-1, keepdims=True)
    o = jnp.dot(p.astype(BF16), gather(v_refs, vn_ref), preferred_element_type=F32)
    o_ref[0] = (o / den).astype(o_ref.dtype)


def dsa_sample_attend(cache_k, cache_v, layer, page_table, q_rows, k_new, v_new, mask, bias_tab, q_norm_g):
    DB, n_pages = page_table.shape
    n_tok = k_new.shape[1] // A_KV_HEADS
    rows = n_tok * A_HEADS
    rows_pad = SUBLANES
    per = rows_pad // n_tok
    L = mask.shape[1]
    page_rows = PAGE_SIZE * A_KV_HEADS
    page_specs = [pl.BlockSpec((None, None, page_rows, HEAD_DIM), functools.partial(
        lambda b, pt, p: (layer, pt[b, p], 0, 0), p=p)) for p in range(n_pages)]
    grid_spec = pltpu.PrefetchScalarGridSpec(
        num_scalar_prefetch=1,
        grid=(DB,),
        in_specs=page_specs + page_specs + [
            pl.BlockSpec((1, rows, HEAD_DIM), lambda b, pt: (b, 0, 0)),
            pl.BlockSpec((1, n_tok * A_KV_HEADS, HEAD_DIM), lambda b, pt: (b, 0, 0)),
            pl.BlockSpec((1, n_tok * A_KV_HEADS, HEAD_DIM), lambda b, pt: (b, 0, 0)),
            pl.BlockSpec((rows_pad, L), lambda b, pt: (b // per, 0)),
            pl.BlockSpec((rows, A_KV_HEADS * L), lambda b, pt: (0, 0), pipeline_mode=pl.Buffered(1)),
            pl.BlockSpec((1, HEAD_DIM), lambda b, pt: (0, 0), pipeline_mode=pl.Buffered(1))],
        out_specs=pl.BlockSpec((1, rows, HEAD_DIM), lambda b, pt: (b, 0, 0)),
    )
    return pl.pallas_call(
        functools.partial(_dsa_sample_attend_kernel, n_pages=n_pages, n_tok=n_tok, rows_pad=rows_pad),
        out_shape=jax.ShapeDtypeStruct((DB, rows, HEAD_DIM), BF16),
        grid_spec=grid_spec,
        compiler_params=_cparams(("parallel",)),
        name="dsa_sample_attend",
    )(page_table, *([cache_k] * n_pages), *([cache_v] * n_pages), q_rows, k_new, v_new, mask,
      bias_tab, q_norm_g.reshape(1, HEAD_DIM))


def _log_sigmoid(z):
    return jnp.minimum(z, 0.0) - jnp.log(1.0 + jnp.exp(-jnp.abs(z)))


def _seg_masks(seg):
    r = lax.broadcasted_iota(I32, (TILE, TILE), 0)
    c = lax.broadcasted_iota(I32, (TILE, TILE), 1)
    return r, c, (r // seg) == (c // seg)


def _gla_levels(seg):
    w, out = seg // 2, []
    while w >= 1:
        out.append(w)
        w //= 2
    return out


def _gla_sum_matrices(seg):
    r = jnp.arange(TILE)[:, None]
    c = jnp.arange(TILE)[None, :]
    mats = []
    for w in _gla_levels(seg):
        same = (r // (2 * w)) == (c // (2 * w))
        r_right = (r % (2 * w)) >= w
        c_right = (c % (2 * w)) >= w
        mats.append(same & r_right & c_right & (c <= r))
    for w in _gla_levels(seg):
        same = (r // (2 * w)) == (c // (2 * w))
        r_right = (r % (2 * w)) >= w
        c_right = (c % (2 * w)) >= w
        mats.append(same & (~r_right) & (~c_right) & (c > r))
    same_seg = (r // seg) == (c // seg)
    mats.append(same_seg & (c <= r))
    mats.append(same_seg & (c > r))
    return jnp.concatenate(mats, axis=0).astype(BF16)


def _bdot(a, b):
    return jnp.dot(a.astype(BF16), b.astype(BF16), preferred_element_type=F32)


def _bdot_nt(a, b):
    return lax.dot_general(a.astype(BF16), b.astype(BF16), (((1,), (1,)), ((), ())), preferred_element_type=F32)


def _gla_common(qb_ref, kb_ref, misc_ref, wg_ref, bg_ref, mats_ref, seg):
    gb = misc_ref[:, MISC_GB:MISC_GB + GATE_RANK]
    z = jnp.dot(gb, wg_ref[...], precision=HIGHEST, preferred_element_type=F32) + bg_ref[...]
    la = _log_sigmoid(z) / GATE_TEMP
    la_hi = la.astype(BF16)
    la_lo = (la - la_hi.astype(F32)).astype(BF16)
    mats = mats_ref[...]
    sums = (jnp.dot(mats, la_hi, preferred_element_type=F32) + jnp.dot(mats, la_lo, preferred_element_type=F32))
    levels = _gla_levels(seg)
    nl = len(levels)
    q = qb_ref[...] * B_DK ** -0.5
    k = kb_ref[...]
    r, c, _ = _seg_masks(seg)
    att = [jnp.where(r == c, _bdot_nt(q[:, h * B_DK:(h + 1) * B_DK], k[:, h * B_DK:(h + 1) * B_DK]), 0.0)
           for h in range(B_HEADS)]
    for li, w in enumerate(levels):
        pair = ((r // (2 * w)) == (c // (2 * w))) & ((r % (2 * w)) >= w) & ((c % (2 * w)) < w)
        qd = (q * jnp.exp(sums[li * TILE:(li + 1) * TILE])).astype(BF16)
        kd = (k * jnp.exp(sums[(nl + li) * TILE:(nl + li + 1) * TILE])).astype(BF16)
        for h in range(B_HEADS):
            hs = slice(h * B_DK, (h + 1) * B_DK)
            att[h] = att[h] + jnp.where(pair, _bdot_nt(qd[:, hs], kd[:, hs]), 0.0)
    b_cum = sums[2 * nl * TILE:(2 * nl + 1) * TILE]
    rem = sums[(2 * nl + 1) * TILE:(2 * nl + 2) * TILE]
    return q, k, att, b_cum, rem


def _gla_finish(o_heads, rb_ref, go_ref, o_ref):
    go = go_ref[...]
    for h in range(B_HEADS):
        vs = slice(h * B_DV, (h + 1) * B_DV)
        rb = rb_ref[:, vs]
        o_ref[:, vs] = (_rms(o_heads[h], go) * (rb * jax.nn.sigmoid(rb))).astype(o_ref.dtype)


def _gla_prompt_kernel(qb_ref, kb_ref, vb_ref, rb_ref, misc_ref, wg_ref, bg_ref, go_ref, mats_ref,
                       o_ref, s_ref, state_ref):
    ci = pl.program_id(1)

    @pl.when(ci == 0)
    def _():
        state_ref[...] = jnp.zeros_like(state_ref)

    q, k, att, b_cum, rem = _gla_common(qb_ref, kb_ref, misc_ref, wg_ref, bg_ref, mats_ref, TILE)
    v = vb_ref[...]
    state = state_ref[...]
    qe = q * jnp.exp(b_cum)
    o_heads = []
    for h in range(B_HEADS):
        ks = slice(h * B_DK, (h + 1) * B_DK)
        vs = slice(h * B_DV, (h + 1) * B_DV)
        o_heads.append(_bdot(qe[:, ks], state[ks, :]) + _bdot(att[h], v[:, vs]))
    _gla_finish(o_heads, rb_ref, go_ref, o_ref)

    ke_t = (k * jnp.exp(rem)).T
    e_last = jnp.exp(b_cum[TILE - 1:TILE, :])
    e_col = jnp.broadcast_to(e_last, (TILE, B_HEADS * B_DK)).T[:, 0:1]
    upd = jnp.concatenate(
        [_bdot(ke_t[h * B_DK:(h + 1) * B_DK, :], v[:, h * B_DV:(h + 1) * B_DV]) for h in range(B_HEADS)], axis=0)
    new_state = state * e_col + upd
    state_ref[...] = new_state
    s_ref[0] = new_state


def gla_prompt(proj, w_gate, b_gate, g_out, n_batch, seq):
    T = proj.shape[0]
    nc = seq // TILE
    kwid = B_HEADS * B_DK
    vwid = B_HEADS * B_DV
    mats = _gla_sum_matrices(TILE)
    o, s = pl.pallas_call(
        _gla_prompt_kernel,
        out_shape=(jax.ShapeDtypeStruct((T, vwid), BF16),
                   jax.ShapeDtypeStruct((n_batch, kwid, B_DV), F32)),
        grid=(n_batch, nc),
        in_specs=[pl.BlockSpec((TILE, kwid), lambda b, c: (b * nc + c, QB_OFF // kwid)),
                  pl.BlockSpec((TILE, kwid), lambda b, c: (b * nc + c, KB_OFF // kwid)),
                  pl.BlockSpec((TILE, vwid), lambda b, c: (b * nc + c, VB_OFF // vwid)),
                  pl.BlockSpec((TILE, vwid), lambda b, c: (b * nc + c, RB_OFF // vwid)),
                  pl.BlockSpec((TILE, LANES), lambda b, c: (b * nc + c, MISC_OFF // LANES)),
                  _resident((GATE_RANK, kwid)),
                  _resident((1, kwid)),
                  _resident((1, B_DV)),
                  _resident(mats.shape)],
        out_specs=(pl.BlockSpec((TILE, vwid), lambda b, c: (b * nc + c, 0)),
                   pl.BlockSpec((1, kwid, B_DV), lambda b, c: (b, 0, 0))),
        scratch_shapes=[pltpu.VMEM((kwid, B_DV), F32)],
        compiler_params=_cparams(("parallel", "arbitrary")),
        name="gla_prompt",
    )(proj, proj, proj, proj, proj, w_gate, b_gate.reshape(1, kwid), g_out.reshape(1, B_DV), mats)
    return o, s.reshape(n_batch, B_HEADS, B_DK, B_DV)


def _gla_sample_kernel(qb_ref, kb_ref, vb_ref, rb_ref, misc_ref, wg_ref, bg_ref, go_ref, mats_ref, s0_ref,
                       o_ref, s_ref, *, seg):
    nbt = TILE // seg
    q, k, att, b_cum, rem = _gla_common(qb_ref, kb_ref, misc_ref, wg_ref, bg_ref, mats_ref, seg)
    v = vb_ref[...]
    qe = q * jnp.exp(b_cum)
    ke = k * jnp.exp(rem)
    r1 = lax.broadcasted_iota(I32, (TILE, 1), 0)
    e_last = jnp.where(r1 % seg == seg - 1, jnp.exp(b_cum), 0.0)
    wide = nbt * B_DK
    mq = (lax.broadcasted_iota(I32, (TILE, wide), 0) // seg) == (lax.broadcasted_iota(I32, (TILE, wide), 1) // B_DK)
    mk = (lax.broadcasted_iota(I32, (wide, TILE), 0) // B_DK) == (lax.broadcasted_iota(I32, (wide, TILE), 1) // seg)
    o_heads = []
    for h in range(B_HEADS):
        ks = slice(h * B_DK, (h + 1) * B_DK)
        vs = slice(h * B_DV, (h + 1) * B_DV)
        state = s0_ref[:, h].reshape(wide, B_DV)
        q_bd = jnp.where(mq, jnp.concatenate([qe[:, ks]] * nbt, axis=1), 0.0)
        o_heads.append(_bdot(q_bd, state) + _bdot(att[h], v[:, vs]))
        pair_t = jnp.concatenate([ke[:, ks], e_last[:, ks]], axis=1).T
        k_bd = jnp.where(mk, jnp.concatenate([pair_t[:B_DK]] * nbt, axis=0), 0.0)
        e_bd = jnp.where(mk, jnp.concatenate([pair_t[B_DK:]] * nbt, axis=0), 0.0)
        e_col = jnp.sum(e_bd, axis=-1, keepdims=True)
        new_state = state * e_col + _bdot(k_bd, v[:, vs])
        s_ref[:, h] = new_state.reshape(nbt, B_DK, B_DV)
    _gla_finish(o_heads, rb_ref, go_ref, o_ref)


def gla_sample(proj, w_gate, b_gate, g_out, s0, layer, n_tok):
    T = proj.shape[0]
    nbt = TILE // n_tok
    kwid = B_HEADS * B_DK
    vwid = B_HEADS * B_DV
    mats = _gla_sum_matrices(n_tok)
    return pl.pallas_call(
        functools.partial(_gla_sample_kernel, seg=n_tok),
        out_shape=(jax.ShapeDtypeStruct((T, vwid), BF16),
                   jax.ShapeDtypeStruct(s0.shape[1:], F32)),
        grid=(T // TILE,),
        in_specs=[pl.BlockSpec((TILE, kwid), lambda i: (i, QB_OFF // kwid)),
                  pl.BlockSpec((TILE, kwid), lambda i: (i, KB_OFF // kwid)),
                  pl.BlockSpec((TILE, vwid), lambda i: (i, VB_OFF // vwid)),
                  pl.BlockSpec((TILE, vwid), lambda i: (i, RB_OFF // vwid)),
                  pl.BlockSpec((TILE, LANES), lambda i: (i, MISC_OFF // LANES)),
                  _resident((GATE_RANK, kwid)),
                  _resident((1, kwid)),
                  _resident((1, B_DV)),
                  _resident(mats.shape),
                  pl.BlockSpec((None, nbt, B_HEADS, B_DK, B_DV), lambda i: (layer, i, 0, 0, 0))],
        out_specs=(pl.BlockSpec((TILE, vwid), lambda i: (i, 0)),
                   pl.BlockSpec((nbt, B_HEADS, B_DK, B_DV), lambda i: (i, 0, 0, 0))),
        compiler_params=_cparams(("parallel",)),
        name="gla_sample",
    )(proj, proj, proj, proj, proj, w_gate, b_gate.reshape(1, kwid), g_out.reshape(1, B_DV), mats, s0)


def _gelu(x):
    return jax.nn.gelu(x)


def _gmlp_kernel(uc_ref, vc_ref, gv_ref, ws_ref, bcol_ref, o_ref, vn_ref, *, seg):
    r, c, same_seg = _seg_masks(seg)
    keep = same_seg & (c <= r)
    u = _gelu(uc_ref[...])
    vg = _gelu(vc_ref[...])
    for g in range(C_GROUPS):
        gs = slice(g * C_GROUP_DIM, (g + 1) * C_GROUP_DIM)
        vn = _rms(vg[:, gs], gv_ref[:, gs])
        vn_ref[:, gs] = vn
        w = jnp.where(keep, ws_ref[g], 0.0).astype(BF16)
        s = jnp.dot(w, vn.astype(BF16), preferred_element_type=F32) + bcol_ref[:, g:g + 1]
        o_ref[:, gs] = (u[:, gs] * s).astype(o_ref.dtype)


def gmlp(proj, g_v, w_tiles, b_cols, seg):
    T = proj.shape[0]
    cw = C_GROUPS * C_GROUP_DIM
    return pl.pallas_call(
        functools.partial(_gmlp_kernel, seg=seg),
        out_shape=(jax.ShapeDtypeStruct((T, cw), BF16),
                   jax.ShapeDtypeStruct((T, cw), F32)),
        grid=(T // TILE,),
        in_specs=[pl.BlockSpec((TILE, cw), lambda i: (i, UC_OFF // cw)),
                  pl.BlockSpec((TILE, cw), lambda i: (i, VC_OFF // cw)),
                  _resident((1, cw)),
                  _resident((C_GROUPS, TILE, TILE)),
                  _resident((TILE, C_GROUPS))],
        out_specs=(pl.BlockSpec((TILE, cw), lambda i: (i, 0)),
                   pl.BlockSpec((TILE, cw), lambda i: (i, 0))),
        compiler_params=_cparams(("parallel",)),
        name="gmlp",
    )(proj, proj, g_v.reshape(1, cw), w_tiles, b_cols)


def _out_kernel(h_ref, oa_ref, ob_ref, oc_ref, w_ref, o_ref):
    aw = oa_ref.shape[1]
    bw = ob_ref.shape[1]
    acc = jnp.dot(oa_ref[...], w_ref[0:aw, :], preferred_element_type=F32)
    acc = acc + jnp.dot(ob_ref[...], w_ref[aw:aw + bw, :], preferred_element_type=F32)
    acc = acc + jnp.dot(oc_ref[...], w_ref[aw + bw:, :], preferred_element_type=F32)
    o_ref[...] = h_ref[...] + acc


def out_projection(h, o_a, o_b, o_c, w_out, layer, tm):
    T, D = h.shape
    tm = min(tm, T)
    return pl.pallas_call(
        _out_kernel,
        out_shape=jax.ShapeDtypeStruct((T, D), F32),
        grid=(T // tm,),
        in_specs=[pl.BlockSpec((tm, D), lambda i: (i, 0)),
                  pl.BlockSpec((tm, o_a.shape[1]), lambda i: (i, 0)),
                  pl.BlockSpec((tm, o_b.shape[1]), lambda i: (i, 0)),
                  pl.BlockSpec((tm, o_c.shape[1]), lambda i: (i, 0)),
                  _layer_resident(w_out.shape[1:], layer)],
        out_specs=pl.BlockSpec((tm, D), lambda i: (i, 0)),
        compiler_params=_cparams(("parallel",)),
        name="out_projection",
    )(h, o_a, o_b, o_c, w_out)


def _ffn_kernel(h_ref, g_ref, wg_ref, wu_ref, wd_ref, o_ref, n_ref):
    j = pl.program_id(1)

    @pl.when(j == 0)
    def _():
        h = h_ref[...]
        n_ref[...] = _rms(h, g_ref[...]).astype(BF16)
        o_ref[...] = h

    n = n_ref[...]
    a = jnp.dot(n, wg_ref[...], preferred_element_type=F32)
    u = jnp.dot(n, wu_ref[...], preferred_element_type=F32)
    act = (a * jax.nn.sigmoid(a) * u).astype(BF16)
    o_ref[...] += jnp.dot(act, wd_ref[...], preferred_element_type=F32)


def ffn(h, g, w_gate, w_up, w_down, layer, tm, tf):
    T, D = h.shape
    tm = min(tm, T)
    FF = w_gate.shape[2]
    return pl.pallas_call(
        _ffn_kernel,
        out_shape=jax.ShapeDtypeStruct((T, D), F32),
        grid=(T // tm, FF // tf),
        in_specs=[pl.BlockSpec((tm, D), lambda i, j: (i, 0), pipeline_mode=pl.Buffered(1)),
                  _resident((1, D)),
                  pl.BlockSpec((None, D, tf), lambda i, j: (layer, 0, j)),
                  pl.BlockSpec((None, D, tf), lambda i, j: (layer, 0, j)),
                  pl.BlockSpec((None, tf, D), lambda i, j: (layer, j, 0))],
        out_specs=pl.BlockSpec((tm, D), lambda i, j: (i, 0)),
        scratch_shapes=[pltpu.VMEM((tm, D), BF16)],
        compiler_params=_cparams(("parallel", "arbitrary")),
        name="ffn",
    )(h, g.reshape(1, D), w_gate, w_up, w_down)


def _ple_kernel(h_ref, p_ref, g_ref, wgate_ref, wproj_ref, o_ref):
    h = h_ref[...]
    n = _rms(h, g_ref[...]).astype(BF16)
    gate = jax.nn.sigmoid(jnp.dot(n, wgate_ref[...], preferred_element_type=F32))
    emb = jnp.dot(p_ref[...].astype(BF16), wproj_ref[...], preferred_element_type=F32)
    o_ref[...] = h + gate * emb


def ple(h, p, g, w_gate, w_proj, layer, tm):
    T, D = h.shape
    tm = min(tm, T)
    P = p.shape[2]
    return pl.pallas_call(
        _ple_kernel,
        out_shape=jax.ShapeDtypeStruct((T, D), F32),
        grid=(T // tm,),
        in_specs=[pl.BlockSpec((tm, D), lambda i: (i, 0)),
                  pl.BlockSpec((None, tm, P), lambda i: (layer, i, 0)),
                  _resident((1, D)),
                  _layer_resident(w_gate.shape[1:], layer),
                  _layer_resident(w_proj.shape[1:], layer)],
        out_specs=pl.BlockSpec((tm, D), lambda i: (i, 0)),
        compiler_params=_cparams(("parallel",)),
        name="ple",
    )(h, p, g.reshape(1, D), w_gate, w_proj)


def _pack_w_in(w):
    sizes = (1024, 256, 256, 1024, 64, 16, 256, 256, 512, 16, 512, 512, 512)
    names = ("q", "k", "v", "qi", "ki", "wi", "qb", "kb", "vb", "gb", "rb", "uc", "vc")
    seg, start = {}, 0
    for nme, sz in zip(names, sizes):
        seg[nme] = w[..., start:start + sz]
        start += sz
    pad = jnp.zeros(w.shape[:-1] + (LANES - IDX_DIM - IDX_HEADS - GATE_RANK,), w.dtype)
    order = ("q", "qi", "vb", "rb", "uc", "vc", "k", "v", "qb", "kb", "ki", "wi", "gb")
    return jnp.concatenate([seg[n] for n in order] + [pad], axis=-1).astype(BF16)


def _mixer_tail(h, o_a, o_b, o_c, p_all, wts, layer, tm, tm_ffn):
    h = out_projection(h, o_a, o_b, o_c, wts["w_out"], layer, tm)
    h = ffn(h, wts["g_ffn"][layer], wts["w_ffn_gate"], wts["w_ffn_up"], wts["w_ffn_down"], layer, tm_ffn, 512)
    return ple(h, p_all, wts["g_ple"][layer], wts["w_ple_gate"], wts["w_ple_proj"], layer, tm)


def kernel(x_prompt, x_sample, cache_k, cache_v, cache_idx_k, state_gla, page_table, p_prompt, p_sample,
           g_mix, w_in, q_norm_g, k_norm_g, rel_bias, w_gate_b, b_gate_b, g_out_b, g_v_c, w_spatial,
           b_spatial, w_out, g_ffn, w_ffn_gate, w_ffn_up, w_ffn_down, g_ple, w_ple_gate, w_ple_proj):
    n_batch, seq, d_model = x_prompt.shape
    dec_batch, dec_seq, _ = x_sample.shape
    depth = w_in.shape[0]
    n_pages = page_table.shape[1]
    past = n_pages * PAGE_SIZE
    kw = A_KV_HEADS * HEAD_DIM
    tp, ts = n_batch * seq, dec_batch * dec_seq
    rows_pad = SUBLANES
    l_sample = past + PAGE_SIZE
    k_sel_s = min(TOPK_MAX, (past + dec_seq) // 4)

    bias_p = bias_table_prompt(rel_bias)
    bias_s = bias_table_sample(rel_bias, past, dec_seq, A_KV_HEADS * l_sample)
    cache_ik_t = jnp.swapaxes(cache_idx_k, 2, 3)
    cache_k2 = cache_k.reshape(depth, cache_k.shape[1], PAGE_SIZE * A_KV_HEADS, HEAD_DIM)
    cache_v2 = cache_v.reshape(depth, cache_v.shape[1], PAGE_SIZE * A_KV_HEADS, HEAD_DIM)

    hp = x_prompt.reshape(tp, d_model)
    hs = x_sample.reshape(ts, d_model)
    outs = {k: [] for k in ("kp", "vp", "ikp", "sp", "ks", "vs", "iks", "ss", "cs")}
    slot = (jnp.arange(dec_batch) % (rows_pad // dec_seq)) * dec_seq
    place_t = (jnp.arange(rows_pad)[None, :, None]
               == slot[:, None, None] + jnp.arange(dec_seq)[None, None, :]).astype(F32)
    wts = dict(w_out=w_out.astype(BF16), g_ffn=g_ffn, w_ffn_gate=w_ffn_gate.astype(BF16),
               w_ffn_up=w_ffn_up.astype(BF16), w_ffn_down=w_ffn_down.astype(BF16),
               g_ple=g_ple, w_ple_gate=w_ple_gate.astype(BF16), w_ple_proj=w_ple_proj.astype(BF16))
    w_packed = _pack_w_in(w_in)
    pp_all = p_prompt.reshape(depth, tp, -1)
    ps_all = p_sample.reshape(depth, ts, -1)
    for i in range(depth):
        b_cols_p = b_spatial[i].T
        reps = TILE // dec_seq
        w_tiles_s = jnp.tile(w_spatial[i][:, :dec_seq, :dec_seq], (1, reps, reps))
        b_cols_s = jnp.tile(b_spatial[i][:, :dec_seq].T, (reps, 1))

        proj = in_projection(hp, g_mix[i], w_packed, i, 256)
        kn, vv, ik, vt = kv_post(proj, k_norm_g[i], 512, True)
        o_a = dsa_prompt(proj, kn, vt, bias_p, q_norm_g[i], n_batch, seq)
        o_b, s_p = gla_prompt(proj, w_gate_b[i], b_gate_b[i], g_out_b[i], n_batch, seq)
        o_c, _ = gmlp(proj, g_v_c[i], w_spatial[i], b_cols_p, TILE)
        hp = _mixer_tail(hp, o_a, o_b, o_c, pp_all, wts, i, 256, 1024)
        outs["kp"].append(kn.reshape(n_batch, seq, A_KV_HEADS, HEAD_DIM))
        outs["vp"].append(vv.reshape(n_batch, seq, A_KV_HEADS, HEAD_DIM))
        outs["ikp"].append(ik.reshape(n_batch, seq, IDX_DIM))
        outs["sp"].append(s_p)

        proj = in_projection(hs, g_mix[i], w_packed, i, 256)
        kn, vv, ik = kv_post(proj, k_norm_g[i], 512, False)
        qi_rows =proj[:, QI_OFF:QI_OFF + IDX_HEADS * IDX_DIM].reshape(dec_batch, dec_seq * IDX_HEADS, IDX_DIM)
        wi = proj[:, MISC_OFF + MISC_WI:MISC_OFF + MISC_WI + IDX_HEADS].reshape(dec_batch, dec_seq, IDX_HEADS)
        wi = wi * (IDX_HEADS ** -0.5 * IDX_DIM ** -0.5)
        wmat = (place_t[:, :, :, None] * wi[:, None, :, :]).reshape(dec_batch, rows_pad, dec_seq * IDX_HEADS)
        ki_new_t = jnp.pad(jnp.swapaxes(ik.reshape(dec_batch, dec_seq, IDX_DIM), 1, 2),
                           ((0, 0), (0, 0), (0, PAGE_SIZE - dec_seq)))
        mask = dsa_sample_select(cache_ik_t, i, page_table, qi_rows, wmat, ki_new_t, dec_seq, k_sel_s)
        q_rows = proj[:, Q_OFF:Q_OFF + A_HEADS * HEAD_DIM].reshape(dec_batch, dec_seq * A_HEADS, HEAD_DIM)
        o_a = dsa_sample_attend(cache_k2, cache_v2, i, page_table, q_rows,
                                kn.reshape(dec_batch, dec_seq * A_KV_HEADS, HEAD_DIM),
                                vv.reshape(dec_batch, dec_seq * A_KV_HEADS, HEAD_DIM), mask, bias_s, q_norm_g[i])
        o_a = o_a.reshape(ts, A_HEADS * HEAD_DIM)
        o_b, s_s = gla_sample(proj, w_gate_b[i], b_gate_b[i], g_out_b[i], state_gla, i, dec_seq)
        o_c, vn = gmlp(proj, g_v_c[i], w_tiles_s, b_cols_s, dec_seq)
        hs = _mixer_tail(hs, o_a, o_b, o_c, ps_all, wts, i, 256, 512)
        outs["ks"].append(kn.reshape(dec_batch, dec_seq, A_KV_HEADS, HEAD_DIM))
        outs["vs"].append(vv.reshape(dec_batch, dec_seq, A_KV_HEADS, HEAD_DIM))
        outs["iks"].append(ik.reshape(dec_batch, dec_seq, IDX_DIM))
        outs["ss"].append(s_s)
        outs["cs"].append(vn.reshape(dec_batch, dec_seq, -1))

    st = {k: jnp.stack(v) for k, v in outs.items()}
    return (hp.reshape(n_batch, seq, d_model), hs.reshape(dec_batch, dec_seq, d_model),
            st["kp"], st["vp"], st["ikp"], st["sp"], st["ks"], st["vs"], st["iks"], st["ss"], st["cs"])
```

```python
import functools
import math

import jax
import jax.numpy as jnp
from jax import lax
from jax.experimental import pallas as pl
from jax.experimental.pallas import tpu as pltpu

F32 = jnp.float32
BF16 = jnp.bfloat16
I32 = jnp.int32
HIGHEST = lax.Precision.HIGHEST

LANES = 128
SUBLANES = 8
VMEM_LIMIT = 56 * 1024 * 1024

HEAD_DIM = 128
A_HEADS = 8
A_KV_HEADS = 2
IDX_HEADS = 16
IDX_DIM = 64
TOPK_MAX = 256
NUM_BUCKETS = 32
MAX_DISTANCE = 128
B_HEADS = 4
B_DK = 64
B_DV = 128
GATE_RANK = 16
GATE_TEMP = 16.0
C_GROUPS = 4
C_GROUP_DIM = 128
PAGE_SIZE = 128
EPS = 1e-6
NEG_BIG = -1e30
INT_MIN = -(2 ** 31)

TILE = 128
QBLK = 256

Q_OFF, QI_OFF, VB_OFF, RB_OFF, UC_OFF, VC_OFF = 0, 1024, 2048, 2560, 3072, 3584
K_OFF, V_OFF, QB_OFF, KB_OFF, MISC_OFF = 4096, 4352, 4608, 4864, 5120
PROJ_PACKED = 5248
MISC_KI, MISC_WI, MISC_GB = 0, 64, 80


def _cparams(sem):
    return pltpu.CompilerParams(dimension_semantics=sem, vmem_limit_bytes=VMEM_LIMIT)


def _rms(x, g):
    return x * lax.rsqrt(jnp.mean(x * x, axis=-1, keepdims=True) + EPS) * g


def _resident(shape):
    nd = len(shape)
    return pl.BlockSpec(shape, lambda *_: (0,) * nd, pipeline_mode=pl.Buffered(1))


def _layer_resident(shape, layer):
    nd = len(shape)
    return pl.BlockSpec((None,) + tuple(shape), lambda *_: (layer,) + (0,) * nd, pipeline_mode=pl.Buffered(1))


def _proj_kernel(x_ref, g_ref, w_ref, o_ref):
    n = _rms(x_ref[...], g_ref[...]).astype(BF16)
    ncol = o_ref.shape[1]
    step = 512
    for c0 in range(0, ncol, step):
        c1 = min(c0 + step, ncol)
        o_ref[:, c0:c1] = jnp.dot(n, w_ref[:, c0:c1], preferred_element_type=F32)


def in_projection(h, g, w_packed, layer, tm):
    T, D = h.shape
    tm = min(tm, T)
    N = w_packed.shape[2]
    return pl.pallas_call(
        _proj_kernel,
        out_shape=jax.ShapeDtypeStruct((T, N), F32),
        grid=(T // tm,),
        in_specs=[pl.BlockSpec((tm, D), lambda i: (i, 0)),
                  _resident((1, D)),
                  _layer_resident((D, N), layer)],
        out_specs=pl.BlockSpec((tm, N), lambda i: (i, 0)),
        compiler_params=_cparams(("parallel",)),
        name="in_projection",
    )(h, g.reshape(1, D), w_packed)


def _kv_kernel(k_ref, v_ref, m_ref, g_ref, ko_ref, vo_ref, io_ref, vt_ref=None):
    g = g_ref[...]
    k = k_ref[...]
    for hh in range(A_KV_HEADS):
        sl = slice(hh * HEAD_DIM, (hh + 1) * HEAD_DIM)
        ko_ref[:, sl] = _rms(k[:, sl], g)
    v = v_ref[...]
    vo_ref[...] = v
    io_ref[...] = m_ref[:, MISC_KI:MISC_KI + IDX_DIM]
    if vt_ref is not None:
        for blk in range(vt_ref.shape[0]):
            vt_ref[blk] = v[blk * QBLK:(blk + 1) * QBLK, :].T.astype(vt_ref.dtype)


def kv_post(proj, k_norm_g, tm, with_vt):
    T = proj.shape[0]
    tm = min(tm, T)
    kw = A_KV_HEADS * HEAD_DIM
    out_shape = [jax.ShapeDtypeStruct((T, kw), F32),
                 jax.ShapeDtypeStruct((T, kw), F32),
                 jax.ShapeDtypeStruct((T, IDX_DIM), F32)]
    out_specs = [pl.BlockSpec((tm, kw), lambda i: (i, 0)),
                 pl.BlockSpec((tm, kw), lambda i: (i, 0)),
                 pl.BlockSpec((tm, IDX_DIM), lambda i: (i, 0))]
    if with_vt:
        out_shape.append(jax.ShapeDtypeStruct((T // QBLK, kw, QBLK), BF16))
        out_specs.append(pl.BlockSpec((tm // QBLK, kw, QBLK), lambda i: (i, 0, 0)))
    return pl.pallas_call(
        _kv_kernel,
        out_shape=tuple(out_shape),
        grid=(T // tm,),
        in_specs=[pl.BlockSpec((tm, kw), lambda i: (i, K_OFF // kw)),
                  pl.BlockSpec((tm, kw), lambda i: (i, V_OFF // kw)),
                  pl.BlockSpec((tm, LANES), lambda i: (i, MISC_OFF // LANES)),
                  _resident((1, HEAD_DIM))],
        out_specs=tuple(out_specs),
        compiler_params=_cparams(("parallel",)),
        name="kv_post",
    )(proj, proj, proj, k_norm_g.reshape(1, HEAD_DIM))


def _bucket(dist):
    n = jnp.maximum(dist, 0)
    max_exact = NUM_BUCKETS // 2
    large = max_exact + (jnp.log(jnp.maximum(n, 1).astype(F32) / max_exact)
                         / math.log(MAX_DISTANCE / max_exact)
                         * (NUM_BUCKETS - max_exact)).astype(I32)
    large = jnp.minimum(large, NUM_BUCKETS - 1)
    return jnp.where(n < max_exact, n, large)


def _bias_prompt_kernel(rb_ref, o_ref):
    c = lax.broadcasted_iota(I32, (TILE, TILE), 0)
    t = lax.broadcasted_iota(I32, (TILE, TILE), 1)
    for z in range(3):
        bucket = _bucket(t - c + (2 - z) * TILE)
        for h in range(A_HEADS):
            acc = jnp.zeros((TILE, TILE), F32)
            for b in range(NUM_BUCKETS):
                acc = jnp.where(bucket == b, rb_ref[b, h], acc)
            o_ref[h, z] = acc


def bias_table_prompt(rel_bias):
    return pl.pallas_call(
        _bias_prompt_kernel,
        out_shape=jax.ShapeDtypeStruct((A_HEADS, 3, TILE, TILE), F32),
        in_specs=[pl.BlockSpec(memory_space=pltpu.SMEM)],
        out_specs=pl.BlockSpec(memory_space=pltpu.VMEM),
        name="bias_table_prompt",
    )(rel_bias)


def _bias_sample_kernel(rbrows_ref, o_ref, *, past, n_tok):
    rows, L = o_ref.shape
    r = lax.broadcasted_iota(I32, (rows, L), 0)
    s = lax.broadcasted_iota(I32, (rows, L), 1) // A_KV_HEADS
    bucket = _bucket(past + r // A_HEADS - s)
    rbrows = rbrows_ref[...]
    acc = jnp.zeros((rows, L), F32)
    for b in range(NUM_BUCKETS):
        acc = jnp.where(bucket == b, rbrows[:, b:b + 1], acc)
    o_ref[...] = acc


def bias_table_sample(rel_bias, past, n_tok, L):
    rows = n_tok * A_HEADS
    rbrows = jnp.tile(rel_bias.T, (n_tok, 1))
    return pl.pallas_call(
        functools.partial(_bias_sample_kernel, past=past, n_tok=n_tok),
        out_shape=jax.ShapeDtypeStruct((rows, L), F32),
        name="bias_table_sample",
    )(rbrows)


def _sortable_key(x):
    b = lax.bitcast_convert_type(x, I32)
    return b ^ ((b >> 31) & 0x7FFFFFFF)


def _topk_member(skey_ref, k_sel):
    R, L = skey_ref.shape

    def body(it, ans):
        bit = 31 - it
        cand = ans | lax.shift_left(jnp.int32(1), bit)
        cand_s = cand ^ INT_MIN
        cnt = jnp.sum(jnp.where(skey_ref[...] >= cand_s, 1.0, 0.0), axis=-1, keepdims=True)
        return jnp.where(cnt >= k_sel, cand, ans)

    ans = lax.fori_loop(0, 32, body, jnp.zeros((R, 1), I32))
    tau = ans ^ INT_MIN
    skey = skey_ref[...]
    gt = skey > tau
    eq = skey == tau
    n_gt = jnp.sum(jnp.where(gt, 1.0, 0.0), axis=-1, keepdims=True)
    room = k_sel - n_gt
    r_i = lax.broadcasted_iota(I32, (LANES, LANES), 0)
    c_i = lax.broadcasted_iota(I32, (LANES, LANES), 1)
    upper = jnp.where(r_i <= c_i, 1.0, 0.0).astype(BF16)
    off = jnp.zeros((R, 1), F32)
    parts = []
    for j in range(L // LANES):
        sl = slice(j * LANES, (j + 1) * LANES)
        eq_j = eq[:, sl]
        run = jnp.dot(jnp.where(eq_j, 1.0, 0.0).astype(BF16), upper, preferred_element_type=F32) + off
        parts.append(gt[:, sl] | (eq_j & (run <= room)))
        off = run[:, LANES - 1:LANES]
    return jnp.concatenate(parts, axis=1)


def _fold8(x, op):
    return op(x.reshape(x.shape[0] // SUBLANES, SUBLANES, x.shape[1]), axis=0)


def _dsa_prompt_kernel(q_ref, qi_ref, misc_ref, kn_ref, vt_ref, bias_ref, qg_ref, o_ref,
                       qst_ref, skey_ref, madd_ref, lg_ref, acc_ref, *, k_sel):
    i = pl.program_id(1)
    nkb = i + 1
    sub = QBLK // TILE
    rep = A_HEADS // A_KV_HEADS
    row0 = pl.multiple_of(i * QBLK, QBLK)
    s_iota = lax.broadcasted_iota(I32, (QBLK, QBLK), 0)
    t_iota = lax.broadcasted_iota(I32, (QBLK, QBLK), 1)

    def admissible(j):
        return (j * QBLK + s_iota) <= (row0 + t_iota)

    wi_t = misc_ref[pl.ds(row0, QBLK), :].T[MISC_WI:MISC_WI + IDX_HEADS, :]
    wi_t = wi_t * (IDX_HEADS ** -0.5 * IDX_DIM ** -0.5)
    for h in range(IDX_HEADS):
        qst_ref[h * QBLK:(h + 1) * QBLK, :] = qi_ref[:, h * IDX_DIM:(h + 1) * IDX_DIM].astype(BF16)

    def score_body(j, carry):
        k0 = pl.multiple_of(j * QBLK, QBLK)
        kj = misc_ref[pl.ds(k0, QBLK), MISC_KI:MISC_KI + IDX_DIM].astype(BF16)
        s = lax.dot_general(kj, qst_ref[...], (((1,), (1,)), ((), ())), preferred_element_type=F32)
        score = jnp.zeros((QBLK, QBLK), F32)
        for h in range(IDX_HEADS):
            score = score + jnp.maximum(s[:, h * QBLK:(h + 1) * QBLK], 0.0) * wi_t[h:h + 1, :]
        skey_ref[j] = _sortable_key(jnp.where(admissible(j), score, -jnp.inf))
        return carry

    lax.fori_loop(0, nkb, score_body, 0)

    def count(pred_fn):
        def body(j, acc):
            return acc + _fold8(jnp.where(pred_fn(skey_ref[j]), 1.0, 0.0), jnp.sum)
        acc = lax.fori_loop(0, nkb, body, jnp.zeros((SUBLANES, QBLK), F32))
        return jnp.sum(acc, axis=0, keepdims=True)

    def bit_body(it, ans):
        cand = ans | lax.shift_left(jnp.int32(1), 31 - it)
        cand_s = cand ^ INT_MIN
        cnt = count(lambda key: key >= cand_s)
        return jnp.where(cnt >= k_sel, cand, ans)

    ans = lax.fori_loop(0, 32, bit_body, jnp.zeros((1, QBLK), I32))
    tau = ans ^ INT_MIN
    room = k_sel - count(lambda key: key > tau)
    lower = jnp.where(t_iota <= s_iota, 1.0, 0.0).astype(BF16)

    def mask_body(j, off):
        key = skey_ref[j]
        eq = key == tau
        run = jnp.dot(lower, jnp.where(eq, 1.0, 0.0).astype(BF16), preferred_element_type=F32) + off
        sel = ((key > tau) | (eq & (run <= room))) & admissible(j)
        madd_ref[j] = jnp.where(sel, 0.0, NEG_BIG)
        return run[QBLK - 1:QBLK, :]

    lax.fori_loop(0, nkb, mask_body, jnp.zeros((1, QBLK), F32))

    qg = qg_ref[...]
    wide = rep * QBLK
    for g in range(A_KV_HEADS):
        gs = slice(g * HEAD_DIM, (g + 1) * HEAD_DIM)
        heads = list(range(g * rep, (g + 1) * rep))
        q_stack = jnp.concatenate(
            [(_rms(q_ref[:, h * HEAD_DIM:(h + 1) * HEAD_DIM], qg) * HEAD_DIM ** -0.5).astype(BF16) for h in heads],
            axis=0)

        def logit_body(j, mx):
            k0 = pl.multiple_of(j * QBLK, QBLK)
            kj = kn_ref[pl.ds(k0, QBLK), gs].astype(BF16)
            lg = lax.dot_general(kj, q_stack, (((1,), (1,)), ((), ())), preferred_element_type=F32)
            madd = madd_ref[j]
            parts = []
            for r, h in enumerate(heads):
                quads = []
                for c in range(sub):
                    quads.append(jnp.concatenate(
                        [bias_ref[h, jnp.clip(2 - ((i - j) * sub + u - c), 0, 2)] for u in range(sub)], axis=1))
                parts.append(lg[:, r * QBLK:(r + 1) * QBLK] + jnp.concatenate(quads, axis=0) + madd)
            lg = jnp.concatenate(parts, axis=1)
            lg_ref[j] = lg
            return jnp.maximum(mx, _fold8(lg, jnp.max))

        mx = lax.fori_loop(0, nkb, logit_body, jnp.full((SUBLANES, wide), NEG_BIG, F32))
        m = jnp.max(mx, axis=0, keepdims=True)
        acc_ref[...] = jnp.zeros(acc_ref.shape, F32)

        def pv_body(j, sm):
            p = jnp.exp(lg_ref[j] - m)
            acc_ref[...] += jnp.dot(vt_ref[j, gs, :], p.astype(BF16), preferred_element_type=F32)
            return sm + _fold8(p, jnp.sum)

        sm = lax.fori_loop(0, nkb, pv_body, jnp.zeros((SUBLANES, wide), F32))
        den = jnp.sum(sm, axis=0, keepdims=True)
        o = (acc_ref[...] / den).T
        for r, h in enumerate(heads):
            o_ref[:, h * HEAD_DIM:(h + 1) * HEAD_DIM] = o[r * QBLK:(r + 1) * QBLK, :].astype(o_ref.dtype)


def dsa_prompt(proj, kn, vt, bias_tab, q_norm_g, n_batch, seq):
    T = proj.shape[0]
    nb = seq // QBLK
    k_sel = min(TOPK_MAX, seq // 4)
    aw = A_HEADS * HEAD_DIM
    iw = IDX_HEADS * IDX_DIM
    kw = A_KV_HEADS * HEAD_DIM
    rep = A_HEADS // A_KV_HEADS
    return pl.pallas_call(
        functools.partial(_dsa_prompt_kernel, k_sel=k_sel),
        out_shape=jax.ShapeDtypeStruct((T, aw), BF16),
        grid=(n_batch, nb),
        in_specs=[pl.BlockSpec((QBLK, aw), lambda b, i: (b * nb + i, Q_OFF // aw)),
                  pl.BlockSpec((QBLK, iw), lambda b, i: (b * nb + i, QI_OFF // iw)),
                  pl.BlockSpec((seq, LANES), lambda b, i: (b, MISC_OFF // LANES)),
                  pl.BlockSpec((seq, kw), lambda b, i: (b, 0)),
                  pl.BlockSpec((nb, kw, QBLK), lambda b, i: (b, 0, 0)),
                  _resident((A_HEADS, 3, TILE, TILE)),
                  _resident((1, HEAD_DIM))],
        out_specs=pl.BlockSpec((QBLK, aw), lambda b, i: (b * nb + i, 0)),
        scratch_shapes=[pltpu.VMEM((IDX_HEADS * QBLK, IDX_DIM), BF16),
                        pltpu.VMEM((nb, QBLK, QBLK), I32),
                        pltpu.VMEM((nb, QBLK, QBLK), F32),
                        pltpu.VMEM((nb, QBLK, rep * QBLK), F32),
                        pltpu.VMEM((HEAD_DIM, rep * QBLK), F32)],
        compiler_params=_cparams(("parallel", "arbitrary")),
        name="dsa_prompt",
    )(proj, proj, proj, kn, vt, bias_tab, q_norm_g.reshape(1, HEAD_DIM))


def _dsa_sample_select_kernel(pt_ref, *refs, n_pages, n_tok, k_sel, rows_pad):
    del pt_ref
    page_refs = refs[:n_pages]
    qi_ref, wm_ref, kin_ref, mask_ref, sc_ref, skey_ref = refs[n_pages:]
    b = pl.program_id(0)
    nb = pl.num_programs(0)
    L = mask_ref.shape[1]
    past = n_pages * PAGE_SIZE
    per = rows_pad // n_tok

    kt_all = jnp.concatenate([r[...].astype(BF16) for r in page_refs] + [kin_ref[0].astype(BF16)], axis=1)
    relu_s = jnp.maximum(jnp.dot(qi_ref[0].astype(BF16), kt_all, preferred_element_type=F32), 0.0)
    score = jnp.dot(wm_ref[0], relu_s, precision=HIGHEST, preferred_element_type=F32)
    r0 = pl.multiple_of((b // per) * rows_pad, rows_pad)

    @pl.when(b % per == 0)
    def _():
        sc_ref[pl.ds(r0, rows_pad), :] = score

    @pl.when(b % per != 0)
    def _():
        sc_ref[pl.ds(r0, rows_pad), :] += score

    @pl.when(b == nb - 1)
    def _():
        n_blocks = sc_ref.shape[0] // TILE
        tp = past + lax.broadcasted_iota(I32, (TILE, L), 0) % n_tok
        sp = lax.broadcasted_iota(I32, (TILE, L), 1)
        adm_blk = sp <= tp
        for rb in range(n_blocks):
            rows = slice(rb * TILE, (rb + 1) * TILE)
            skey_ref[...] = _sortable_key(jnp.where(adm_blk, sc_ref[rows, :], -jnp.inf))
            sel = _topk_member(skey_ref, k_sel) & adm_blk
            mask_ref[rows, :] = jnp.where(sel, 1.0, 0.0)


def dsa_sample_select(cache_ik_t, layer, page_table, qi_rows, wmat, ki_new_t, n_tok, k_sel):
    DB, n_pages = page_table.shape
    rows_pad = wmat.shape[1]
    n_rows = DB // (rows_pad // n_tok) * rows_pad
    L = (n_pages + 1) * PAGE_SIZE
    page_specs = [pl.BlockSpec((None, None, IDX_DIM, PAGE_SIZE), functools.partial(
        lambda b, pt, p: (layer, pt[b, p], 0, 0), p=p)) for p in range(n_pages)]
    grid_spec = pltpu.PrefetchScalarGridSpec(
        num_scalar_prefetch=1,
        grid=(DB,),
        in_specs=page_specs + [
            pl.BlockSpec((1,) + qi_rows.shape[1:], lambda b, pt: (b, 0, 0)),
            pl.BlockSpec((1,) + wmat.shape[1:], lambda b, pt: (b, 0, 0)),
            pl.BlockSpec((1, IDX_DIM, PAGE_SIZE), lambda b, pt: (b, 0, 0))],
        out_specs=pl.BlockSpec((n_rows, L), lambda b, pt: (0, 0)),
        scratch_shapes=[pltpu.VMEM((n_rows, L), F32),
                        pltpu.VMEM((TILE, L), I32)],
    )
    return pl.pallas_call(
        functools.partial(_dsa_sample_select_kernel, n_pages=n_pages, n_tok=n_tok, k_sel=k_sel,
                          rows_pad=rows_pad),
        out_shape=jax.ShapeDtypeStruct((n_rows, L), F32),
        grid_spec=grid_spec,
        compiler_params=_cparams(("arbitrary",)),
        name="dsa_sample_select",
    )(page_table, *([cache_ik_t] * n_pages), qi_rows, wmat, ki_new_t)


def _dsa_sample_attend_kernel(pt_ref, *refs, n_pages, n_tok, rows_pad):
    del pt_ref
    k_refs = refs[:n_pages]
    v_refs = refs[n_pages:2 * n_pages]
    q_ref, kn_ref, vn_ref, mask_ref, bias_ref, qg_ref, o_ref = refs[2 * n_pages:]
    rows = n_tok * A_HEADS
    page_rows = PAGE_SIZE * A_KV_HEADS
    n_tiles = mask_ref.shape[1] // PAGE_SIZE
    pad = jnp.zeros((page_rows - n_tok * A_KV_HEADS, HEAD_DIM), BF16)

    def gather(page_refs, new_ref):
        return jnp.concatenate([r[...].astype(BF16) for r in page_refs]
                               + [new_ref[0].astype(BF16), pad], axis=0)

    q = (_rms(q_ref[0], qg_ref[...]) * HEAD_DIM ** -0.5).astype(BF16)
    rep = A_HEADS // A_KV_HEADS
    grp = (lax.broadcasted_iota(I32, (rows, 1), 0) % A_HEADS) // rep
    first = (pl.program_id(0) % (rows_pad // n_tok)) * n_tok
    e_r = lax.broadcasted_iota(I32, (rows, rows_pad), 0) // A_HEADS
    e_c = lax.broadcasted_iota(I32, (rows, rows_pad), 1)
    expand = jnp.where(e_r + first == e_c, 1.0, 0.0).astype(BF16)
    sel = jnp.dot(expand, mask_ref[...].astype(BF16), preferred_element_type=F32)
    d_r = lax.broadcasted_iota(I32, (PAGE_SIZE, page_rows), 0)
    d_c = lax.broadcasted_iota(I32, (PAGE_SIZE, page_rows), 1)
    dup = jnp.where(d_c // A_KV_HEADS == d_r, 1.0, 0.0).astype(BF16)
    stacked = jnp.concatenate([sel[:, j * PAGE_SIZE:(j + 1) * PAGE_SIZE] for j in range(n_tiles)], axis=0)
    stacked = jnp.dot(stacked.astype(BF16), dup, preferred_element_type=F32)
    sel = jnp.concatenate([stacked[j * rows:(j + 1) * rows, :] for j in range(n_tiles)], axis=1)
    col_grp = lax.broadcasted_iota(I32, (rows, n_tiles * page_rows), 1) % A_KV_HEADS
    valid = (sel > 0.5) & (col_grp == grp)
    logits = lax.dot_general(q, gather(k_refs, kn_ref), (((1,), (1,)), ((), ())), preferred_element_type=F32)
    logits = jnp.where(valid, logits + bias_ref[...], NEG_BIG)
    m = jnp.max(logits, axis=-1, keepdims=True)
    p = jnp.exp(logits - m)
    den = jnp.sum(p, axis=-1, keepdims=True)
    o = jnp.dot(p.astype(BF16), gather(v_refs, vn_ref), preferred_element_type=F32)
    o_ref[0] = (o / den).astype(o_ref.dtype)


def dsa_sample_attend(cache_k, cache_v, layer, page_table, q_rows, k_new, v_new, mask, bias_tab, q_norm_g):
    DB, n_pages = page_table.shape
    n_tok = k_new.shape[1] // A_KV_HEADS
    rows = n_tok * A_HEADS
    rows_pad = SUBLANES
    per = rows_pad // n_tok
    L = mask.shape[1]
    page_rows = PAGE_SIZE * A_KV_HEADS
    page_specs = [pl.BlockSpec((None, None, page_rows, HEAD_DIM), functools.partial(
        lambda b, pt, p: (layer, pt[b, p], 0, 0), p=p)) for p in range(n_pages)]
    grid_spec = pltpu.PrefetchScalarGridSpec(
        num_scalar_prefetch=1,
        grid=(DB,),
        in_specs=page_specs + page_specs + [
            pl.BlockSpec((1, rows, HEAD_DIM), lambda b, pt: (b, 0, 0)),
            pl.BlockSpec((1, n_tok * A_KV_HEADS, HEAD_DIM), lambda b, pt: (b, 0, 0)),
            pl.BlockSpec((1, n_tok * A_KV_HEADS, HEAD_DIM), lambda b, pt: (b, 0, 0)),
            pl.BlockSpec((rows_pad, L), lambda b, pt: (b // per, 0)),
            pl.BlockSpec((rows, A_KV_HEADS * L), lambda b, pt: (0, 0), pipeline_mode=pl.Buffered(1)),
            pl.BlockSpec((1, HEAD_DIM), lambda b, pt: (0, 0), pipeline_mode=pl.Buffered(1))],
        out_specs=pl.BlockSpec((1, rows, HEAD_DIM), lambda b, pt: (b, 0, 0)),
    )
    return pl.pallas_call(
        functools.partial(_dsa_sample_attend_kernel, n_pages=n_pages, n_tok=n_tok, rows_pad=rows_pad),
        out_shape=jax.ShapeDtypeStruct((DB, rows, HEAD_DIM), BF16),
        grid_spec=grid_spec,
        compiler_params=_cparams(("parallel",)),
        name="dsa_sample_attend",
    )(page_table, *([cache_k] * n_pages), *([cache_v] * n_pages), q_rows, k_new, v_new, mask,
      bias_tab, q_norm_g.reshape(1, HEAD_DIM))


def _log_sigmoid(z):
    return jnp.minimum(z, 0.0) - jnp.log(1.0 + jnp.exp(-jnp.abs(z)))


def _seg_masks(seg):
    r = lax.broadcasted_iota(I32, (TILE, TILE), 0)
    c = lax.broadcasted_iota(I32, (TILE, TILE), 1)
    return r, c, (r // seg) == (c // seg)


def _gla_levels(seg):
    w, out = seg // 2, []
    while w >= 1:
        out.append(w)
        w //= 2
    return out


def _gla_sum_matrices(seg):
    r = jnp.arange(TILE)[:, None]
    c = jnp.arange(TILE)[None, :]
    mats = []
    for w in _gla_levels(seg):
        same = (r // (2 * w)) == (c // (2 * w))
        r_right = (r % (2 * w)) >= w
        c_right = (c % (2 * w)) >= w
        mats.append(same & r_right & c_right & (c <= r))
    for w in _gla_levels(seg):
        same = (r // (2 * w)) == (c // (2 * w))
        r_right = (r % (2 * w)) >= w
        c_right = (c % (2 * w)) >= w
        mats.append(same & (~r_right) & (~c_right) & (c > r))
    same_seg = (r // seg) == (c // seg)
    mats.append(same_seg & (c <= r))
    mats.append(same_seg & (c > r))
    return jnp.concatenate(mats, axis=0).astype(BF16)


def _bdot(a, b):
    return jnp.dot(a.astype(BF16), b.astype(BF16), preferred_element_type=F32)


def _bdot_nt(a, b):
    return lax.dot_general(a.astype(BF16), b.astype(BF16), (((1,), (1,)), ((), ())), preferred_element_type=F32)


def _gla_common(qb_ref, kb_ref, misc_ref, wg_ref, bg_ref, mats_ref, seg):
    gb = misc_ref[:, MISC_GB:MISC_GB + GATE_RANK]
    z = jnp.dot(gb, wg_ref[...], precision=HIGHEST, preferred_element_type=F32) + bg_ref[...]
    la = _log_sigmoid(z) / GATE_TEMP
    la_hi = la.astype(BF16)
    la_lo = (la - la_hi.astype(F32)).astype(BF16)
    mats = mats_ref[...]
    sums = (jnp.dot(mats, la_hi, preferred_element_type=F32) + jnp.dot(mats, la_lo, preferred_element_type=F32))
    levels = _gla_levels(seg)
    nl = len(levels)
    q = qb_ref[...] * B_DK ** -0.5
    k = kb_ref[...]
    r, c, _ = _seg_masks(seg)
    att = [jnp.where(r == c, _bdot_nt(q[:, h * B_DK:(h + 1) * B_DK], k[:, h * B_DK:(h + 1) * B_DK]), 0.0)
           for h in range(B_HEADS)]
    for li, w in enumerate(levels):
        pair = ((r // (2 * w)) == (c // (2 * w))) & ((r % (2 * w)) >= w) & ((c % (2 * w)) < w)
        qd = (q * jnp.exp(sums[li * TILE:(li + 1) * TILE])).astype(BF16)
        kd = (k * jnp.exp(sums[(nl + li) * TILE:(nl + li + 1) * TILE])).astype(BF16)
        for h in range(B_HEADS):
            hs = slice(h * B_DK, (h + 1) * B_DK)
            att[h] = att[h] + jnp.where(pair, _bdot_nt(qd[:, hs], kd[:, hs]), 0.0)
    b_cum = sums[2 * nl * TILE:(2 * nl + 1) * TILE]
    rem = sums[(2 * nl + 1) * TILE:(2 * nl + 2) * TILE]
    return q, k, att, b_cum, rem


def _gla_finish(o_heads, rb_ref, go_ref, o_ref):
    go = go_ref[...]
    for h in range(B_HEADS):
        vs = slice(h * B_DV, (h + 1) * B_DV)
        rb = rb_ref[:, vs]
        o_ref[:, vs] = (_rms(o_heads[h], go) * (rb * jax.nn.sigmoid(rb))).astype(o_ref.dtype)


def _gla_prompt_kernel(qb_ref, kb_ref, vb_ref, rb_ref, misc_ref, wg_ref, bg_ref, go_ref, mats_ref,
                       o_ref, s_ref, state_ref):
    ci = pl.program_id(0)

    @pl.when(ci == 0)
    def _():
        state_ref[...] = jnp.zeros_like(state_ref)

    for b in range(qb_ref.shape[0]):
        q, k, att, b_cum, rem = _gla_common(qb_ref.at[b], kb_ref.at[b], misc_ref.at[b], wg_ref, bg_ref,
                                            mats_ref, TILE)
        v = vb_ref[b]
        state = state_ref[b]
        qe = q * jnp.exp(b_cum)
        o_heads = []
        for h in range(B_HEADS):
            ks = slice(h * B_DK, (h + 1) * B_DK)
            vs = slice(h * B_DV, (h + 1) * B_DV)
            o_heads.append(_bdot(qe[:, ks], state[ks, :]) + _bdot(att[h], v[:, vs]))
        _gla_finish(o_heads, rb_ref.at[b], go_ref, o_ref.at[b])

        ke_t = (k * jnp.exp(rem)).T
        e_last = jnp.exp(b_cum[TILE - 1:TILE, :])
        e_col = jnp.broadcast_to(e_last, (TILE, B_HEADS * B_DK)).T[:, 0:1]
        upd = jnp.concatenate(
            [_bdot(ke_t[h * B_DK:(h + 1) * B_DK, :], v[:, h * B_DV:(h + 1) * B_DV]) for h in range(B_HEADS)],
            axis=0)
        new_state = state * e_col + upd
        state_ref[b] = new_state
        s_ref[b] = new_state


def gla_prompt(proj, w_gate, b_gate, g_out, n_batch, seq):
    nc = seq // TILE
    kwid = B_HEADS * B_DK
    vwid = B_HEADS * B_DV
    mats = _gla_sum_matrices(TILE)
    proj3 = proj.reshape(n_batch, seq, proj.shape[1])
    o, s = pl.pallas_call(
        _gla_prompt_kernel,
        out_shape=(jax.ShapeDtypeStruct((n_batch, seq, vwid), BF16),
                   jax.ShapeDtypeStruct((n_batch, kwid, B_DV), F32)),
        grid=(nc,),
        in_specs=[pl.BlockSpec((n_batch, TILE, kwid), lambda c: (0, c, QB_OFF // kwid)),
                  pl.BlockSpec((n_batch, TILE, kwid), lambda c: (0, c, KB_OFF // kwid)),
                  pl.BlockSpec((n_batch, TILE, vwid), lambda c: (0, c, VB_OFF // vwid)),
                  pl.BlockSpec((n_batch, TILE, vwid), lambda c: (0, c, RB_OFF // vwid)),
                  pl.BlockSpec((n_batch, TILE, LANES), lambda c: (0, c, MISC_OFF // LANES)),
                  _resident((GATE_RANK, kwid)),
                  _resident((1, kwid)),
                  _resident((1, B_DV)),
                  _resident(mats.shape)],
        out_specs=(pl.BlockSpec((n_batch, TILE, vwid), lambda c: (0, c, 0)),
                   pl.BlockSpec((n_batch, kwid, B_DV), lambda c: (0, 0, 0))),
        scratch_shapes=[pltpu.VMEM((n_batch, kwid, B_DV), F32)],
        compiler_params=_cparams(("arbitrary",)),
        name="gla_prompt",
    )(proj3, proj3, proj3, proj3, proj3, w_gate, b_gate.reshape(1, kwid), g_out.reshape(1, B_DV), mats)
    return o.reshape(n_batch * seq, vwid), s.reshape(n_batch, B_HEADS, B_DK, B_DV)


def _gla_sample_kernel(qb_ref, kb_ref, vb_ref, rb_ref, misc_ref, wg_ref, bg_ref, go_ref, mats_ref, s0_ref,
                       o_ref, s_ref, *, seg):
    nbt = TILE // seg
    q, k, att, b_cum, rem = _gla_common(qb_ref, kb_ref, misc_ref, wg_ref, bg_ref, mats_ref, seg)
    v = vb_ref[...]
    qe = q * jnp.exp(b_cum)
    ke = k * jnp.exp(rem)
    r1 = lax.broadcasted_iota(I32, (TILE, 1), 0)
    e_last = jnp.where(r1 % seg == seg - 1, jnp.exp(b_cum), 0.0)
    wide = nbt * B_DK
    mq = (lax.broadcasted_iota(I32, (TILE, wide), 0) // seg) == (lax.broadcasted_iota(I32, (TILE, wide), 1) // B_DK)
    mk = (lax.broadcasted_iota(I32, (wide, TILE), 0) // B_DK) == (lax.broadcasted_iota(I32, (wide, TILE), 1) // seg)
    o_heads = []
    for h in range(B_HEADS):
        ks = slice(h * B_DK, (h + 1) * B_DK)
        vs = slice(h * B_DV, (h + 1) * B_DV)
        state = s0_ref[:, h].reshape(wide, B_DV)
        q_bd = jnp.where(mq, jnp.concatenate([qe[:, ks]] * nbt, axis=1), 0.0)
        o_heads.append(_bdot(q_bd, state) + _bdot(att[h], v[:, vs]))
        pair_t = jnp.concatenate([ke[:, ks], e_last[:, ks]], axis=1).T
        k_bd = jnp.where(mk, jnp.concatenate([pair_t[:B_DK]] * nbt, axis=0), 0.0)
        e_bd = jnp.where(mk, jnp.concatenate([pair_t[B_DK:]] * nbt, axis=0), 0.0)
        e_col = jnp.sum(e_bd, axis=-1, keepdims=True)
        new_state = state * e_col + _bdot(k_bd, v[:, vs])
        s_ref[:, h] = new_state.reshape(nbt, B_DK, B_DV)
    _gla_finish(o_heads, rb_ref, go_ref, o_ref)


def gla_sample(proj, w_gate, b_gate, g_out, s0, layer, n_tok):
    T = proj.shape[0]
    nbt = TILE // n_tok
    kwid = B_HEADS * B_DK
    vwid = B_HEADS * B_DV
    mats = _gla_sum_matrices(n_tok)
    return pl.pallas_call(
        functools.partial(_gla_sample_kernel, seg=n_tok),
        out_shape=(jax.ShapeDtypeStruct((T, vwid), BF16),
                   jax.ShapeDtypeStruct(s0.shape[1:], F32)),
        grid=(T // TILE,),
        in_specs=[pl.BlockSpec((TILE, kwid), lambda i: (i, QB_OFF // kwid)),
                  pl.BlockSpec((TILE, kwid), lambda i: (i, KB_OFF // kwid)),
                  pl.BlockSpec((TILE, vwid), lambda i: (i, VB_OFF // vwid)),
                  pl.BlockSpec((TILE, vwid), lambda i: (i, RB_OFF // vwid)),
                  pl.BlockSpec((TILE, LANES), lambda i: (i, MISC_OFF // LANES)),
                  _resident((GATE_RANK, kwid)),
                  _resident((1, kwid)),
                  _resident((1, B_DV)),
                  _resident(mats.shape),
                  pl.BlockSpec((None, nbt, B_HEADS, B_DK, B_DV), lambda i: (layer, i, 0, 0, 0))],
        out_specs=(pl.BlockSpec((TILE, vwid), lambda i: (i, 0)),
                   pl.BlockSpec((nbt, B_HEADS, B_DK, B_DV), lambda i: (i, 0, 0, 0))),
        compiler_params=_cparams(("parallel",)),
        name="gla_sample",
    )(proj, proj, proj, proj, proj, w_gate, b_gate.reshape(1, kwid), g_out.reshape(1, B_DV), mats, s0)


def _gelu(x):
    return jax.nn.gelu(x)


def _gmlp_kernel(uc_ref, vc_ref, gv_ref, ws_ref, bcol_ref, o_ref, vn_ref, *, seg):
    r, c, same_seg = _seg_masks(seg)
    keep = same_seg & (c <= r)
    u = _gelu(uc_ref[...])
    vg = _gelu(vc_ref[...])
    for g in range(C_GROUPS):
        gs = slice(g * C_GROUP_DIM, (g + 1) * C_GROUP_DIM)
        vn = _rms(vg[:, gs], gv_ref[:, gs])
        vn_ref[:, gs] = vn
        w = jnp.where(keep, ws_ref[g], 0.0).astype(BF16)
        s = jnp.dot(w, vn.astype(BF16), preferred_element_type=F32) + bcol_ref[:, g:g + 1]
        o_ref[:, gs] = (u[:, gs] * s).astype(o_ref.dtype)


def gmlp(proj, g_v, w_tiles, b_cols, seg):
    T = proj.shape[0]
    cw = C_GROUPS * C_GROUP_DIM
    return pl.pallas_call(
        functools.partial(_gmlp_kernel, seg=seg),
        out_shape=(jax.ShapeDtypeStruct((T, cw), BF16),
                   jax.ShapeDtypeStruct((T, cw), F32)),
        grid=(T // TILE,),
        in_specs=[pl.BlockSpec((TILE, cw), lambda i: (i, UC_OFF // cw)),
                  pl.BlockSpec((TILE, cw), lambda i: (i, VC_OFF // cw)),
                  _resident((1, cw)),
                  _resident((C_GROUPS, TILE, TILE)),
                  _resident((TILE, C_GROUPS))],
        out_specs=(pl.BlockSpec((TILE, cw), lambda i: (i, 0)),
                   pl.BlockSpec((TILE, cw), lambda i: (i, 0))),
        compiler_params=_cparams(("parallel",)),
        name="gmlp",
    )(proj, proj, g_v.reshape(1, cw), w_tiles, b_cols)


def _ffn_kernel(h_ref, oa_ref, ob_ref, oc_ref, wo_ref, g_ref, wg_ref, wu_ref, wd_ref, o_ref, n_ref):
    j = pl.program_id(1)

    @pl.when(j == 0)
    def _():
        aw = oa_ref.shape[1]
        bw = ob_ref.shape[1]
        h = h_ref[...] + jnp.dot(oa_ref[...], wo_ref[0:aw, :], preferred_element_type=F32)
        h = h + jnp.dot(ob_ref[...], wo_ref[aw:aw + bw, :], preferred_element_type=F32)
        h = h + jnp.dot(oc_ref[...], wo_ref[aw + bw:, :], preferred_element_type=F32)
        n_ref[...] = _rms(h, g_ref[...]).astype(BF16)
        o_ref[...] = h

    n = n_ref[...]
    a = jnp.dot(n, wg_ref[...], preferred_element_type=F32)
    u = jnp.dot(n, wu_ref[...], preferred_element_type=F32)
    act = (a * jax.nn.sigmoid(a) * u).astype(BF16)
    o_ref[...] += jnp.dot(act, wd_ref[...], preferred_element_type=F32)


def out_proj_ffn(h, o_a, o_b, o_c, w_out, g, w_gate, w_up, w_down, layer, tm, tf):
    T, D = h.shape
    tm = min(tm, T)
    FF = w_gate.shape[2]
    return pl.pallas_call(
        _ffn_kernel,
        out_shape=jax.ShapeDtypeStruct((T, D), F32),
        grid=(T // tm, FF // tf),
        in_specs=[pl.BlockSpec((tm, D), lambda i, j: (i, 0)),
                  pl.BlockSpec((tm, o_a.shape[1]), lambda i, j: (i, 0)),
                  pl.BlockSpec((tm, o_b.shape[1]), lambda i, j: (i, 0)),
                  pl.BlockSpec((tm, o_c.shape[1]), lambda i, j: (i, 0)),
                  _layer_resident(w_out.shape[1:], layer),
                  _resident((1, D)),
                  pl.BlockSpec((None, D, tf), lambda i, j: (layer, 0, j)),
                  pl.BlockSpec((None, D, tf), lambda i, j: (layer, 0, j)),
                  pl.BlockSpec((None, tf, D), lambda i, j: (layer, j, 0))],
        out_specs=pl.BlockSpec((tm, D), lambda i, j: (i, 0)),
        scratch_shapes=[pltpu.VMEM((tm, D), BF16)],
        compiler_params=_cparams(("parallel", "arbitrary")),
        name="out_proj_ffn",
    )(h, o_a, o_b, o_c, w_out, g.reshape(1, D), w_gate, w_up, w_down)


def _ple_kernel(h_ref, p_ref, g_ref, wgate_ref, wproj_ref, o_ref):
    h = h_ref[...]
    n = _rms(h, g_ref[...]).astype(BF16)
    gate = jax.nn.sigmoid(jnp.dot(n, wgate_ref[...], preferred_element_type=F32))
    emb = jnp.dot(p_ref[...].astype(BF16), wproj_ref[...], preferred_element_type=F32)
    o_ref[...] = h + gate * emb


def ple(h, p, g, w_gate, w_proj, layer, tm):
    T, D = h.shape
    tm = min(tm, T)
    P = p.shape[2]
    return pl.pallas_call(
        _ple_kernel,
        out_shape=jax.ShapeDtypeStruct((T, D), F32),
        grid=(T // tm,),
        in_specs=[pl.BlockSpec((tm, D), lambda i: (i, 0)),
                  pl.BlockSpec((None, tm, P), lambda i: (layer, i, 0)),
                  _resident((1, D)),
                  _layer_resident(w_gate.shape[1:], layer),
                  _layer_resident(w_proj.shape[1:], layer)],
        out_specs=pl.BlockSpec((tm, D), lambda i: (i, 0)),
        compiler_params=_cparams(("parallel",)),
        name="ple",
    )(h, p, g.reshape(1, D), w_gate, w_proj)


_W_IN_SEGMENTS = (("q", 1024), ("k", 256), ("v", 256), ("qi", 1024), ("ki", 64), ("wi", 16), ("qb", 256),
                  ("kb", 256), ("vb", 512), ("gb", 16), ("rb", 512), ("uc", 512), ("vc", 512))
_W_IN_PACKED_ORDER = ("q", "qi", "vb", "rb", "uc", "vc", "k", "v", "qb", "kb", "ki", "wi", "gb")


def _pack_kernel(w_ref, o_ref):
    src, start = {}, 0
    for name, size in _W_IN_SEGMENTS:
        src[name] = (start, size)
        start += size
    dst = 0
    for name in _W_IN_PACKED_ORDER:
        s0, size = src[name]
        o_ref[:, dst:dst + size] = w_ref[:, s0:s0 + size].astype(BF16)
        dst += size
    o_ref[:, dst:] = jnp.zeros((o_ref.shape[0], o_ref.shape[1] - dst), BF16)


def _pack_w_in(w, tr=256):
    depth, D, N = w.shape
    return pl.pallas_call(
        _pack_kernel,
        out_shape=jax.ShapeDtypeStruct((depth, D, PROJ_PACKED), BF16),
        grid=(depth, D // tr),
        in_specs=[pl.BlockSpec((None, tr, N), lambda l, i: (l, i, 0))],
        out_specs=pl.BlockSpec((None, tr, PROJ_PACKED), lambda l, i: (l, i, 0)),
        compiler_params=_cparams(("parallel", "parallel")),
        name="pack_w_in",
    )(w)


def _mixer_tail(h, o_a, o_b, o_c, p_all, wts, layer, tm, tm_ffn):
    h = out_proj_ffn(h, o_a, o_b, o_c, wts["w_out"], wts["g_ffn"][layer], wts["w_ffn_gate"], wts["w_ffn_up"],
                     wts["w_ffn_down"], layer, tm_ffn, 512)
    return ple(h, p_all, wts["g_ple"][layer], wts["w_ple_gate"], wts["w_ple_proj"], layer, tm)


def kernel(x_prompt, x_sample, cache_k, cache_v, cache_idx_k, state_gla, page_table, p_prompt, p_sample,
           g_mix, w_in, q_norm_g, k_norm_g, rel_bias, w_gate_b, b_gate_b, g_out_b, g_v_c, w_spatial,
           b_spatial, w_out, g_ffn, w_ffn_gate, w_ffn_up, w_ffn_down, g_ple, w_ple_gate, w_ple_proj):
    n_batch, seq, d_model = x_prompt.shape
    dec_batch, dec_seq, _ = x_sample.shape
    depth = w_in.shape[0]
    n_pages = page_table.shape[1]
    past = n_pages * PAGE_SIZE
    kw = A_KV_HEADS * HEAD_DIM
    tp, ts = n_batch * seq, dec_batch * dec_seq
    rows_pad = SUBLANES
    l_sample = past + PAGE_SIZE
    k_sel_s = min(TOPK_MAX, (past + dec_seq) // 4)

    bias_p = bias_table_prompt(rel_bias)
    bias_s = bias_table_sample(rel_bias, past, dec_seq, A_KV_HEADS * l_sample)
    cache_ik_t = jnp.swapaxes(cache_idx_k, 2, 3)
    cache_k2 = cache_k.reshape(depth, cache_k.shape[1], PAGE_SIZE * A_KV_HEADS, HEAD_DIM)
    cache_v2 = cache_v.reshape(depth, cache_v.shape[1], PAGE_SIZE * A_KV_HEADS, HEAD_DIM)

    hp = x_prompt.reshape(tp, d_model)
    hs = x_sample.reshape(ts, d_model)
    outs = {k: [] for k in ("kp", "vp", "ikp", "sp", "ks", "vs", "iks", "ss", "cs")}
    slot = (jnp.arange(dec_batch) % (rows_pad // dec_seq)) * dec_seq
    place_t = (jnp.arange(rows_pad)[None, :, None]
               == slot[:, None, None] + jnp.arange(dec_seq)[None, None, :]).astype(F32)
    wts = dict(w_out=w_out.astype(BF16), g_ffn=g_ffn, w_ffn_gate=w_ffn_gate.astype(BF16),
               w_ffn_up=w_ffn_up.astype(BF16), w_ffn_down=w_ffn_down.astype(BF16),
               g_ple=g_ple, w_ple_gate=w_ple_gate.astype(BF16), w_ple_proj=w_ple_proj.astype(BF16))
    w_packed = _pack_w_in(w_in)
    pp_all = p_prompt.reshape(depth, tp, -1)
    ps_all = p_sample.reshape(depth, ts, -1)
    for i in range(depth):
        b_cols_p = b_spatial[i].T
        reps = TILE // dec_seq
        w_tiles_s = jnp.tile(w_spatial[i][:, :dec_seq, :dec_seq], (1, reps, reps))
        b_cols_s = jnp.tile(b_spatial[i][:, :dec_seq].T, (reps, 1))

        proj = in_projection(hp, g_mix[i], w_packed, i, 256)
        kn, vv, ik, vt = kv_post(proj, k_norm_g[i], 512, True)
        o_a = dsa_prompt(proj, kn, vt, bias_p, q_norm_g[i], n_batch, seq)
        o_b, s_p = gla_prompt(proj, w_gate_b[i], b_gate_b[i], g_out_b[i], n_batch, seq)
        o_c, _ = gmlp(proj, g_v_c[i], w_spatial[i], b_cols_p, TILE)
        hp = _mixer_tail(hp, o_a, o_b, o_c, pp_all, wts, i, 256, 512)
        outs["kp"].append(kn.reshape(n_batch, seq, A_KV_HEADS, HEAD_DIM))
        outs["vp"].append(vv.reshape(n_batch, seq, A_KV_HEADS, HEAD_DIM))
        outs["ikp"].append(ik.reshape(n_batch, seq, IDX_DIM))
        outs["sp"].append(s_p)

        proj = in_projection(hs, g_mix[i], w_packed, i, 256)
        kn, vv, ik = kv_post(proj, k_norm_g[i], 512, False)
        qi_rows =proj[:, QI_OFF:QI_OFF + IDX_HEADS * IDX_DIM].reshape(dec_batch, dec_seq * IDX_HEADS, IDX_DIM)
        wi = proj[:, MISC_OFF + MISC_WI:MISC_OFF + MISC_WI + IDX_HEADS].reshape(dec_batch, dec_seq, IDX_HEADS)
        wi = wi * (IDX_HEADS ** -0.5 * IDX_DIM ** -0.5)
        wmat = (place_t[:, :, :, None] * wi[:, None, :, :]).reshape(dec_batch, rows_pad, dec_seq * IDX_HEADS)
        ki_new_t = jnp.pad(jnp.swapaxes(ik.reshape(dec_batch, dec_seq, IDX_DIM), 1, 2),
                           ((0, 0), (0, 0), (0, PAGE_SIZE - dec_seq)))
        mask = dsa_sample_select(cache_ik_t, i, page_table, qi_rows, wmat, ki_new_t, dec_seq, k_sel_s)
        q_rows = proj[:, Q_OFF:Q_OFF + A_HEADS * HEAD_DIM].reshape(dec_batch, dec_seq * A_HEADS, HEAD_DIM)
        o_a = dsa_sample_attend(cache_k2, cache_v2, i, page_table, q_rows,
                                kn.reshape(dec_batch, dec_seq * A_KV_HEADS, HEAD_DIM),
                                vv.reshape(dec_batch, dec_seq * A_KV_HEADS, HEAD_DIM), mask, bias_s, q_norm_g[i])
        o_a = o_a.reshape(ts, A_HEADS * HEAD_DIM)
        o_b, s_s = gla_sample(proj, w_gate_b[i], b_gate_b[i], g_out_b[i], state_gla, i, dec_seq)
        o_c, vn = gmlp(proj, g_v_c[i], w_tiles_s, b_cols_s, dec_seq)
        hs = _mixer_tail(hs, o_a, o_b, o_c, ps_all, wts, i, 256, 512)
        outs["ks"].append(kn.reshape(dec_batch, dec_seq, A_KV_HEADS, HEAD_DIM))
        outs["vs"].append(vv.reshape(dec_batch, dec_seq, A_KV_HEADS, HEAD_DIM))
        outs["iks"].append(ik.reshape(dec_batch, dec_seq, IDX_DIM))
        outs["ss"].append(s_s)
        outs["cs"].append(vn.reshape(dec_batch, dec_seq, -1))

    st = {k: jnp.stack(v) for k, v in outs.items()}
    return (hp.reshape(n_batch, seq, d_model), hs.reshape(dec_batch, dec_seq, d_model),
            st["kp"], st["vp"], st["ikp"], st["sp"], st["ks"], st["vs"], st["iks"], st["ss"], st["cs"])
```

```python
import functools
import math

import jax
import jax.numpy as jnp
from jax import lax
from jax.experimental import pallas as pl
from jax.experimental.pallas import tpu as pltpu

F32 = jnp.float32
BF16 = jnp.bfloat16
I32 = jnp.int32
HIGHEST = lax.Precision.HIGHEST

LANES = 128
SUBLANES = 8
VMEM_LIMIT = 56 * 1024 * 1024

HEAD_DIM = 128
A_HEADS = 8
A_KV_HEADS = 2
IDX_HEADS = 16
IDX_DIM = 64
TOPK_MAX = 256
NUM_BUCKETS = 32
MAX_DISTANCE = 128
B_HEADS = 4
B_DK = 64
B_DV = 128
GATE_RANK = 16
GATE_TEMP = 16.0
C_GROUPS = 4
C_GROUP_DIM = 128
PAGE_SIZE = 128
EPS = 1e-6
NEG_BIG = -1e30
INT_MIN = -(2 ** 31)
NEG_INF_KEY = -2139095041

TILE = 128
QBLK = 256

Q_OFF, QI_OFF, VB_OFF, RB_OFF, UC_OFF, VC_OFF = 0, 1024, 2048, 2560, 3072, 3584
K_OFF, V_OFF, QB_OFF, KB_OFF, MISC_OFF = 4096, 4352, 4608, 4864, 5120
PROJ_PACKED = 5248
MISC_KI, MISC_WI, MISC_GB = 0, 64, 80


def _cparams(sem):
    return pltpu.CompilerParams(dimension_semantics=sem, vmem_limit_bytes=VMEM_LIMIT)


def _rms(x, g):
    return x * lax.rsqrt(jnp.mean(x * x, axis=-1, keepdims=True) + EPS) * g


def _resident(shape):
    nd = len(shape)
    return pl.BlockSpec(shape, lambda *_: (0,) * nd, pipeline_mode=pl.Buffered(1))


def _layer_resident(shape, layer):
    nd = len(shape)
    return pl.BlockSpec((None,) + tuple(shape), lambda *_: (layer,) + (0,) * nd, pipeline_mode=pl.Buffered(1))


def _proj_kernel(x_ref, g_ref, w_ref, o_ref):
    n = _rms(x_ref[...], g_ref[...]).astype(BF16)
    ncol = o_ref.shape[1]
    step = 512
    for c0 in range(0, ncol, step):
        c1 = min(c0 + step, ncol)
        o_ref[:, c0:c1] = jnp.dot(n, w_ref[:, c0:c1], preferred_element_type=F32)


def in_projection(h, g, w_packed, layer, tm):
    T, D = h.shape
    tm = min(tm, T)
    N = w_packed.shape[2]
    return pl.pallas_call(
        _proj_kernel,
        out_shape=jax.ShapeDtypeStruct((T, N), F32),
        grid=(T // tm,),
        in_specs=[pl.BlockSpec((tm, D), lambda i: (i, 0)),
                  _resident((1, D)),
                  _layer_resident((D, N), layer)],
        out_specs=pl.BlockSpec((tm, N), lambda i: (i, 0)),
        compiler_params=_cparams(("parallel",)),
        name="in_projection",
    )(h, g.reshape(1, D), w_packed)


def _kv_kernel(k_ref, v_ref, m_ref, g_ref, ko_ref, vo_ref, io_ref, vt_ref=None):
    g = g_ref[...]
    k = k_ref[...]
    for hh in range(A_KV_HEADS):
        sl = slice(hh * HEAD_DIM, (hh + 1) * HEAD_DIM)
        ko_ref[:, sl] = _rms(k[:, sl], g)
    v = v_ref[...]
    vo_ref[...] = v
    io_ref[...] = m_ref[:, MISC_KI:MISC_KI + IDX_DIM]
    if vt_ref is not None:
        for blk in range(vt_ref.shape[0]):
            vt_ref[blk] = v[blk * QBLK:(blk + 1) * QBLK, :].T.astype(vt_ref.dtype)


def kv_post(proj, k_norm_g, tm, with_vt):
    T = proj.shape[0]
    tm = min(tm, T)
    kw = A_KV_HEADS * HEAD_DIM
    out_shape = [jax.ShapeDtypeStruct((T, kw), F32),
                 jax.ShapeDtypeStruct((T, kw), F32),
                 jax.ShapeDtypeStruct((T, IDX_DIM), F32)]
    out_specs = [pl.BlockSpec((tm, kw), lambda i: (i, 0)),
                 pl.BlockSpec((tm, kw), lambda i: (i, 0)),
                 pl.BlockSpec((tm, IDX_DIM), lambda i: (i, 0))]
    if with_vt:
        out_shape.append(jax.ShapeDtypeStruct((T // QBLK, kw, QBLK), BF16))
        out_specs.append(pl.BlockSpec((tm // QBLK, kw, QBLK), lambda i: (i, 0, 0)))
    return pl.pallas_call(
        _kv_kernel,
        out_shape=tuple(out_shape),
        grid=(T // tm,),
        in_specs=[pl.BlockSpec((tm, kw), lambda i: (i, K_OFF // kw)),
                  pl.BlockSpec((tm, kw), lambda i: (i, V_OFF // kw)),
                  pl.BlockSpec((tm, LANES), lambda i: (i, MISC_OFF // LANES)),
                  _resident((1, HEAD_DIM))],
        out_specs=tuple(out_specs),
        compiler_params=_cparams(("parallel",)),
        name="kv_post",
    )(proj, proj, proj, k_norm_g.reshape(1, HEAD_DIM))


def _bucket(dist):
    n = jnp.maximum(dist, 0)
    max_exact = NUM_BUCKETS // 2
    large = max_exact + (jnp.log(jnp.maximum(n, 1).astype(F32) / max_exact)
                         / math.log(MAX_DISTANCE / max_exact)
                         * (NUM_BUCKETS - max_exact)).astype(I32)
    large = jnp.minimum(large, NUM_BUCKETS - 1)
    return jnp.where(n < max_exact, n, large)


def _bias_prompt_kernel(rb_ref, o_ref):
    c = lax.broadcasted_iota(I32, (TILE, TILE), 0)
    t = lax.broadcasted_iota(I32, (TILE, TILE), 1)
    for z in range(3):
        bucket = _bucket(t - c + (2 - z) * TILE)
        for h in range(A_HEADS):
            acc = jnp.zeros((TILE, TILE), F32)
            for b in range(NUM_BUCKETS):
                acc = jnp.where(bucket == b, rb_ref[b, h], acc)
            o_ref[h, z] = acc


def bias_table_prompt(rel_bias):
    return pl.pallas_call(
        _bias_prompt_kernel,
        out_shape=jax.ShapeDtypeStruct((A_HEADS, 3, TILE, TILE), F32),
        in_specs=[pl.BlockSpec(memory_space=pltpu.SMEM)],
        out_specs=pl.BlockSpec(memory_space=pltpu.VMEM),
        name="bias_table_prompt",
    )(rel_bias)


def _bias_sample_kernel(rbrows_ref, o_ref, *, past, n_tok):
    rows, L = o_ref.shape
    r = lax.broadcasted_iota(I32, (rows, L), 0)
    s = lax.broadcasted_iota(I32, (rows, L), 1) // A_KV_HEADS
    bucket = _bucket(past + r // A_HEADS - s)
    rbrows = rbrows_ref[...]
    acc = jnp.zeros((rows, L), F32)
    for b in range(NUM_BUCKETS):
        acc = jnp.where(bucket == b, rbrows[:, b:b + 1], acc)
    o_ref[...] = acc


def bias_table_sample(rel_bias, past, n_tok, L):
    rows = n_tok * A_HEADS
    rbrows = jnp.tile(rel_bias.T, (n_tok, 1))
    return pl.pallas_call(
        functools.partial(_bias_sample_kernel, past=past, n_tok=n_tok),
        out_shape=jax.ShapeDtypeStruct((rows, L), F32),
        name="bias_table_sample",
    )(rbrows)


def _sortable_key(x):
    b = lax.bitcast_convert_type(x, I32)
    return b ^ ((b >> 31) & 0x7FFFFFFF)


def _topk_member(skey_ref, k_sel):
    R, L = skey_ref.shape

    def body(it, ans):
        bit = 31 - it
        cand = ans | lax.shift_left(jnp.int32(1), bit)
        cand_s = cand ^ INT_MIN
        cnt = jnp.sum(jnp.where(skey_ref[...] >= cand_s, 1.0, 0.0), axis=-1, keepdims=True)
        return jnp.where(cnt >= k_sel, cand, ans)

    ans = lax.fori_loop(0, 32, body, jnp.zeros((R, 1), I32))
    tau = ans ^ INT_MIN
    skey = skey_ref[...]
    gt = skey > tau
    eq = skey == tau
    n_gt = jnp.sum(jnp.where(gt, 1.0, 0.0), axis=-1, keepdims=True)
    room = k_sel - n_gt
    r_i = lax.broadcasted_iota(I32, (LANES, LANES), 0)
    c_i = lax.broadcasted_iota(I32, (LANES, LANES), 1)
    upper = jnp.where(r_i <= c_i, 1.0, 0.0).astype(BF16)
    off = jnp.zeros((R, 1), F32)
    parts = []
    for j in range(L // LANES):
        sl = slice(j * LANES, (j + 1) * LANES)
        eq_j = eq[:, sl]
        run = jnp.dot(jnp.where(eq_j, 1.0, 0.0).astype(BF16), upper, preferred_element_type=F32) + off
        parts.append(gt[:, sl] | (eq_j & (run <= room)))
        off = run[:, LANES - 1:LANES]
    return jnp.concatenate(parts, axis=1)


def _fold8(x, op):
    return op(x.reshape(x.shape[0] // SUBLANES, SUBLANES, x.shape[1]), axis=0)


def _dsa_prompt_kernel(q_ref, qi_ref, misc_ref, kn_ref, vt_ref, bias_ref, qg_ref, o_ref,
                       qst_ref, skey_ref, madd_ref, lg_ref, acc_ref, *, k_sel):
    i = pl.program_id(1)
    nkb = i + 1
    sub = QBLK // TILE
    rep = A_HEADS // A_KV_HEADS
    row0 = pl.multiple_of(i * QBLK, QBLK)
    s_iota = lax.broadcasted_iota(I32, (QBLK, QBLK), 0)
    t_iota = lax.broadcasted_iota(I32, (QBLK, QBLK), 1)

    def admissible(j):
        return (j * QBLK + s_iota) <= (row0 + t_iota)

    wi_t = misc_ref[pl.ds(row0, QBLK), :].T[MISC_WI:MISC_WI + IDX_HEADS, :]
    wi_t = wi_t * (IDX_HEADS ** -0.5 * IDX_DIM ** -0.5)
    for h in range(IDX_HEADS):
        qst_ref[h * QBLK:(h + 1) * QBLK, :] = qi_ref[:, h * IDX_DIM:(h + 1) * IDX_DIM].astype(BF16)

    def score_body(j, carry):
        k0 = pl.multiple_of(j * QBLK, QBLK)
        kj = misc_ref[pl.ds(k0, QBLK), MISC_KI:MISC_KI + IDX_DIM].astype(BF16)
        s = lax.dot_general(kj, qst_ref[...], (((1,), (1,)), ((), ())), preferred_element_type=F32)
        score = jnp.zeros((QBLK, QBLK), F32)
        for h in range(IDX_HEADS):
            score = score + jnp.maximum(s[:, h * QBLK:(h + 1) * QBLK], 0.0) * wi_t[h:h + 1, :]
        skey_ref[j] = _sortable_key(jnp.where(admissible(j), score, -jnp.inf))
        return carry

    lax.fori_loop(0, nkb, score_body, 0)

    def count(pred_fn):
        def body(j, acc):
            return acc + _fold8(jnp.where(pred_fn(skey_ref[j]), 1.0, 0.0), jnp.sum)
        acc = lax.fori_loop(0, nkb, body, jnp.zeros((SUBLANES, QBLK), F32))
        return jnp.sum(acc, axis=0, keepdims=True)

    def bit_body(it, ans):
        cand = ans | lax.shift_left(jnp.int32(1), 31 - it)
        cand_s = cand ^ INT_MIN
        cnt = count(lambda key: key >= cand_s)
        return jnp.where(cnt >= k_sel, cand, ans)

    ans = lax.fori_loop(0, 32, bit_body, jnp.zeros((1, QBLK), I32))
    tau = ans ^ INT_MIN
    n_ge = count(lambda key: key >= tau)
    excess = jnp.max(jnp.where((n_ge > k_sel) & (tau != NEG_INF_KEY), 1.0, 0.0))

    @pl.when(excess == 0.0)
    def _():
        def mask_body(j, carry):
            madd_ref[j] = jnp.where((skey_ref[j] >= tau) & admissible(j), 0.0, NEG_BIG)
            return carry

        lax.fori_loop(0, nkb, mask_body, 0)

    @pl.when(excess > 0.0)
    def _():
        room = k_sel - count(lambda key: key > tau)
        lower = jnp.where(t_iota <= s_iota, 1.0, 0.0).astype(BF16)

        def mask_body(j, off):
            key = skey_ref[j]
            eq = key == tau
            run = jnp.dot(lower, jnp.where(eq, 1.0, 0.0).astype(BF16), preferred_element_type=F32) + off
            sel = ((key > tau) | (eq & (run <= room))) & admissible(j)
            madd_ref[j] = jnp.where(sel, 0.0, NEG_BIG)
            return run[QBLK - 1:QBLK, :]

        lax.fori_loop(0, nkb, mask_body, jnp.zeros((1, QBLK), F32))

    qg = qg_ref[...]
    wide = rep * QBLK
    for g in range(A_KV_HEADS):
        gs = slice(g * HEAD_DIM, (g + 1) * HEAD_DIM)
        heads = list(range(g * rep, (g + 1) * rep))
        q_stack = jnp.concatenate(
            [(_rms(q_ref[:, h * HEAD_DIM:(h + 1) * HEAD_DIM], qg) * HEAD_DIM ** -0.5).astype(BF16) for h in heads],
            axis=0)

        def logit_body(j, mx):
            k0 = pl.multiple_of(j * QBLK, QBLK)
            kj = kn_ref[pl.ds(k0, QBLK), gs].astype(BF16)
            lg = lax.dot_general(kj, q_stack, (((1,), (1,)), ((), ())), preferred_element_type=F32)
            madd = madd_ref[j]
            parts = []
            for r, h in enumerate(heads):
                quads = []
                for c in range(sub):
                    quads.append(jnp.concatenate(
                        [bias_ref[h, jnp.clip(2 - ((i - j) * sub + u - c), 0, 2)] for u in range(sub)], axis=1))
                parts.append(lg[:, r * QBLK:(r + 1) * QBLK] + jnp.concatenate(quads, axis=0) + madd)
            lg = jnp.concatenate(parts, axis=1)
            lg_ref[j] = lg
            return jnp.maximum(mx, _fold8(lg, jnp.max))

        mx = lax.fori_loop(0, nkb, logit_body, jnp.full((SUBLANES, wide), NEG_BIG, F32))
        m = jnp.max(mx, axis=0, keepdims=True)
        acc_ref[...] = jnp.zeros(acc_ref.shape, F32)

        def pv_body(j, sm):
            p = jnp.exp(lg_ref[j] - m)
            acc_ref[...] += jnp.dot(vt_ref[j, gs, :], p.astype(BF16), preferred_element_type=F32)
            return sm + _fold8(p, jnp.sum)

        sm = lax.fori_loop(0, nkb, pv_body, jnp.zeros((SUBLANES, wide), F32))
        den = jnp.sum(sm, axis=0, keepdims=True)
        o = (acc_ref[...] / den).T
        for r, h in enumerate(heads):
            o_ref[:, h * HEAD_DIM:(h + 1) * HEAD_DIM] = o[r * QBLK:(r + 1) * QBLK, :].astype(o_ref.dtype)


def dsa_prompt(proj, kn, vt, bias_tab, q_norm_g, n_batch, seq):
    T = proj.shape[0]
    nb = seq // QBLK
    k_sel = min(TOPK_MAX, seq // 4)
    aw = A_HEADS * HEAD_DIM
    iw = IDX_HEADS * IDX_DIM
    kw = A_KV_HEADS * HEAD_DIM
    rep = A_HEADS // A_KV_HEADS
    return pl.pallas_call(
        functools.partial(_dsa_prompt_kernel, k_sel=k_sel),
        out_shape=jax.ShapeDtypeStruct((T, aw), BF16),
        grid=(n_batch, nb),
        in_specs=[pl.BlockSpec((QBLK, aw), lambda b, i: (b * nb + i, Q_OFF // aw)),
                  pl.BlockSpec((QBLK, iw), lambda b, i: (b * nb + i, QI_OFF // iw)),
                  pl.BlockSpec((seq, LANES), lambda b, i: (b, MISC_OFF // LANES)),
                  pl.BlockSpec((seq, kw), lambda b, i: (b, 0)),
                  pl.BlockSpec((nb, kw, QBLK), lambda b, i: (b, 0, 0)),
                  _resident((A_HEADS, 3, TILE, TILE)),
                  _resident((1, HEAD_DIM))],
        out_specs=pl.BlockSpec((QBLK, aw), lambda b, i: (b * nb + i, 0)),
        scratch_shapes=[pltpu.VMEM((IDX_HEADS * QBLK, IDX_DIM), BF16),
                        pltpu.VMEM((nb, QBLK, QBLK), I32),
                        pltpu.VMEM((nb, QBLK, QBLK), F32),
                        pltpu.VMEM((nb, QBLK, rep * QBLK), F32),
                        pltpu.VMEM((HEAD_DIM, rep * QBLK), F32)],
        compiler_params=_cparams(("parallel", "arbitrary")),
        name="dsa_prompt",
    )(proj, proj, proj, kn, vt, bias_tab, q_norm_g.reshape(1, HEAD_DIM))


def _dsa_sample_select_kernel(pt_ref, *refs, n_pages, n_tok, k_sel, rows_pad):
    del pt_ref
    page_refs = refs[:n_pages]
    qi_ref, wm_ref, kin_ref, mask_ref, sc_ref, skey_ref = refs[n_pages:]
    b = pl.program_id(0)
    nb = pl.num_programs(0)
    L = mask_ref.shape[1]
    past = n_pages * PAGE_SIZE
    per = rows_pad // n_tok

    kt_all = jnp.concatenate([r[...].astype(BF16) for r in page_refs] + [kin_ref[0].astype(BF16)], axis=1)
    relu_s = jnp.maximum(jnp.dot(qi_ref[0].astype(BF16), kt_all, preferred_element_type=F32), 0.0)
    score = jnp.dot(wm_ref[0], relu_s, precision=HIGHEST, preferred_element_type=F32)
    r0 = pl.multiple_of((b // per) * rows_pad, rows_pad)

    @pl.when(b % per == 0)
    def _():
        sc_ref[pl.ds(r0, rows_pad), :] = score

    @pl.when(b % per != 0)
    def _():
        sc_ref[pl.ds(r0, rows_pad), :] += score

    @pl.when(b == nb - 1)
    def _():
        n_blocks = sc_ref.shape[0] // TILE
        tp = past + lax.broadcasted_iota(I32, (TILE, L), 0) % n_tok
        sp = lax.broadcasted_iota(I32, (TILE, L), 1)
        adm_blk = sp <= tp
        for rb in range(n_blocks):
            rows = slice(rb * TILE, (rb + 1) * TILE)
            skey_ref[...] = _sortable_key(jnp.where(adm_blk, sc_ref[rows, :], -jnp.inf))
            sel = _topk_member(skey_ref, k_sel) & adm_blk
            mask_ref[rows, :] = jnp.where(sel, 1.0, 0.0)


def dsa_sample_select(cache_ik_t, layer, page_table, qi_rows, wmat, ki_new_t, n_tok, k_sel):
    DB, n_pages = page_table.shape
    rows_pad = wmat.shape[1]
    n_rows = DB // (rows_pad // n_tok) * rows_pad
    L = (n_pages + 1) * PAGE_SIZE
    page_specs = [pl.BlockSpec((None, None, IDX_DIM, PAGE_SIZE), functools.partial(
        lambda b, pt, p: (layer, pt[b, p], 0, 0), p=p)) for p in range(n_pages)]
    grid_spec = pltpu.PrefetchScalarGridSpec(
        num_scalar_prefetch=1,
        grid=(DB,),
        in_specs=page_specs + [
            pl.BlockSpec((1,) + qi_rows.shape[1:], lambda b, pt: (b, 0, 0)),
            pl.BlockSpec((1,) + wmat.shape[1:], lambda b, pt: (b, 0, 0)),
            pl.BlockSpec((1, IDX_DIM, PAGE_SIZE), lambda b, pt: (b, 0, 0))],
        out_specs=pl.BlockSpec((n_rows, L), lambda b, pt: (0, 0)),
        scratch_shapes=[pltpu.VMEM((n_rows, L), F32),
                        pltpu.VMEM((TILE, L), I32)],
    )
    return pl.pallas_call(
        functools.partial(_dsa_sample_select_kernel, n_pages=n_pages, n_tok=n_tok, k_sel=k_sel,
                          rows_pad=rows_pad),
        out_shape=jax.ShapeDtypeStruct((n_rows, L), F32),
        grid_spec=grid_spec,
        compiler_params=_cparams(("arbitrary",)),
        name="dsa_sample_select",
    )(page_table, *([cache_ik_t] * n_pages), qi_rows, wmat, ki_new_t)


def _dsa_sample_attend_kernel(pt_ref, *refs, n_pages, n_tok, rows_pad):
    del pt_ref
    per = rows_pad // n_tok
    k_refs = refs[:per * n_pages]
    v_refs = refs[per * n_pages:2 * per * n_pages]
    q_ref, kn_ref, vn_ref, mask_ref, bias_ref, qg_ref, o_ref = refs[2 * per * n_pages:]
    rows = n_tok * A_HEADS
    page_rows = PAGE_SIZE * A_KV_HEADS
    n_tiles = mask_ref.shape[1] // PAGE_SIZE
    pad = jnp.zeros((page_rows - n_tok * A_KV_HEADS, HEAD_DIM), BF16)
    rep = A_HEADS // A_KV_HEADS
    grp = (lax.broadcasted_iota(I32, (rows, 1), 0) % A_HEADS) // rep
    e_r = lax.broadcasted_iota(I32, (rows, rows_pad), 0) // A_HEADS
    e_c = lax.broadcasted_iota(I32, (rows, rows_pad), 1)
    d_r = lax.broadcasted_iota(I32, (PAGE_SIZE, page_rows), 0)
    d_c = lax.broadcasted_iota(I32, (PAGE_SIZE, page_rows), 1)
    dup = jnp.where(d_c // A_KV_HEADS == d_r, 1.0, 0.0).astype(BF16)
    col_grp = lax.broadcasted_iota(I32, (rows, n_tiles * page_rows), 1) % A_KV_HEADS
    mask = mask_ref[...].astype(BF16)

    for e in range(per):
        def gather(page_refs, new_ref):
            return jnp.concatenate([r[...].astype(BF16) for r in page_refs[e * n_pages:(e + 1) * n_pages]]
                                   + [new_ref[e].astype(BF16), pad], axis=0)

        q = (_rms(q_ref[e], qg_ref[...]) * HEAD_DIM ** -0.5).astype(BF16)
        expand = jnp.where(e_r + e * n_tok == e_c, 1.0, 0.0).astype(BF16)
        sel = jnp.dot(expand, mask, preferred_element_type=F32)
        stacked = jnp.concatenate([sel[:, j * PAGE_SIZE:(j + 1) * PAGE_SIZE] for j in range(n_tiles)], axis=0)
        stacked = jnp.dot(stacked.astype(BF16), dup, preferred_element_type=F32)
        sel = jnp.concatenate([stacked[j * rows:(j + 1) * rows, :] for j in range(n_tiles)], axis=1)
        valid = (sel > 0.5) & (col_grp == grp)
        logits = lax.dot_general(q, gather(k_refs, kn_ref), (((1,), (1,)), ((), ())), preferred_element_type=F32)
        logits = jnp.where(valid, logits + bias_ref[...], NEG_BIG)
        m = jnp.max(logits, axis=-1, keepdims=True)
        p = jnp.exp(logits - m)
        den = jnp.sum(p, axis=-1, keepdims=True)
        o = jnp.dot(p.astype(BF16), gather(v_refs, vn_ref), preferred_element_type=F32)
        o_ref[e] = (o / den).astype(o_ref.dtype)


def dsa_sample_attend(cache_k, cache_v, layer, page_table, q_rows, k_new, v_new, mask, bias_tab, q_norm_g):
    DB, n_pages = page_table.shape
    n_tok = k_new.shape[1] // A_KV_HEADS
    rows = n_tok * A_HEADS
    rows_pad = SUBLANES
    per = rows_pad // n_tok
    L = mask.shape[1]
    page_rows = PAGE_SIZE * A_KV_HEADS
    page_specs = [pl.BlockSpec((None, None, page_rows, HEAD_DIM), functools.partial(
        lambda b, pt, e, p: (layer, pt[b * per + e, p], 0, 0), e=e, p=p))
        for e in range(per) for p in range(n_pages)]
    grid_spec = pltpu.PrefetchScalarGridSpec(
        num_scalar_prefetch=1,
        grid=(DB // per,),
        in_specs=page_specs + page_specs + [
            pl.BlockSpec((per, rows, HEAD_DIM), lambda b, pt: (b, 0, 0)),
            pl.BlockSpec((per, n_tok * A_KV_HEADS, HEAD_DIM), lambda b, pt: (b, 0, 0)),
            pl.BlockSpec((per, n_tok * A_KV_HEADS, HEAD_DIM), lambda b, pt: (b, 0, 0)),
            pl.BlockSpec((rows_pad, L), lambda b, pt: (b, 0)),
            pl.BlockSpec((rows, A_KV_HEADS * L), lambda b, pt: (0, 0), pipeline_mode=pl.Buffered(1)),
            pl.BlockSpec((1, HEAD_DIM), lambda b, pt: (0, 0), pipeline_mode=pl.Buffered(1))],
        out_specs=pl.BlockSpec((per, rows, HEAD_DIM), lambda b, pt: (b, 0, 0)),
    )
    return pl.pallas_call(
        functools.partial(_dsa_sample_attend_kernel, n_pages=n_pages, n_tok=n_tok, rows_pad=rows_pad),
        out_shape=jax.ShapeDtypeStruct((DB, rows, HEAD_DIM), BF16),
        grid_spec=grid_spec,
        compiler_params=_cparams(("parallel",)),
        name="dsa_sample_attend",
    )(page_table, *([cache_k] * (per * n_pages)), *([cache_v] * (per * n_pages)), q_rows, k_new, v_new, mask,
      bias_tab, q_norm_g.reshape(1, HEAD_DIM))


def _log_sigmoid(z):
    return jnp.minimum(z, 0.0) - jnp.log(1.0 + jnp.exp(-jnp.abs(z)))


def _seg_masks(seg):
    r = lax.broadcasted_iota(I32, (TILE, TILE), 0)
    c = lax.broadcasted_iota(I32, (TILE, TILE), 1)
    return r, c, (r // seg) == (c // seg)


def _gla_levels(seg):
    w, out = seg // 2, []
    while w >= 1:
        out.append(w)
        w //= 2
    return out


def _gla_sum_matrices(seg):
    r = jnp.arange(TILE)[:, None]
    c = jnp.arange(TILE)[None, :]
    mats = []
    for w in _gla_levels(seg):
        same = (r // (2 * w)) == (c // (2 * w))
        r_right = (r % (2 * w)) >= w
        c_right = (c % (2 * w)) >= w
        mats.append(same & r_right & c_right & (c <= r))
    for w in _gla_levels(seg):
        same = (r // (2 * w)) == (c // (2 * w))
        r_right = (r % (2 * w)) >= w
        c_right = (c % (2 * w)) >= w
        mats.append(same & (~r_right) & (~c_right) & (c > r))
    same_seg = (r // seg) == (c // seg)
    mats.append(same_seg & (c <= r))
    mats.append(same_seg & (c > r))
    return jnp.concatenate(mats, axis=0).astype(BF16)


def _bdot(a, b):
    return jnp.dot(a.astype(BF16), b.astype(BF16), preferred_element_type=F32)


def _bdot_nt(a, b):
    return lax.dot_general(a.astype(BF16), b.astype(BF16), (((1,), (1,)), ((), ())), preferred_element_type=F32)


def _gla_common(qb_ref, kb_ref, misc_ref, wg_ref, bg_ref, mats_ref, seg):
    gb = misc_ref[:, MISC_GB:MISC_GB + GATE_RANK]
    z = jnp.dot(gb, wg_ref[...], precision=HIGHEST, preferred_element_type=F32) + bg_ref[...]
    la = _log_sigmoid(z) / GATE_TEMP
    la_hi = la.astype(BF16)
    la_lo = (la - la_hi.astype(F32)).astype(BF16)
    mats = mats_ref[...]
    sums = (jnp.dot(mats, la_hi, preferred_element_type=F32) + jnp.dot(mats, la_lo, preferred_element_type=F32))
    levels = _gla_levels(seg)
    nl = len(levels)
    q = qb_ref[...] * B_DK ** -0.5
    k = kb_ref[...]
    r, c, _ = _seg_masks(seg)
    att = [jnp.where(r == c, _bdot_nt(q[:, h * B_DK:(h + 1) * B_DK], k[:, h * B_DK:(h + 1) * B_DK]), 0.0)
           for h in range(B_HEADS)]
    for li, w in enumerate(levels):
        pair = ((r // (2 * w)) == (c // (2 * w))) & ((r % (2 * w)) >= w) & ((c % (2 * w)) < w)
        qd = (q * jnp.exp(sums[li * TILE:(li + 1) * TILE])).astype(BF16)
        kd = (k * jnp.exp(sums[(nl + li) * TILE:(nl + li + 1) * TILE])).astype(BF16)
        for h in range(B_HEADS):
            hs = slice(h * B_DK, (h + 1) * B_DK)
            att[h] = att[h] + jnp.where(pair, _bdot_nt(qd[:, hs], kd[:, hs]), 0.0)
    b_cum = sums[2 * nl * TILE:(2 * nl + 1) * TILE]
    rem = sums[(2 * nl + 1) * TILE:(2 * nl + 2) * TILE]
    return q, k, att, b_cum, rem


def _gla_finish(o_heads, rb_ref, go_ref, o_ref):
    go = go_ref[...]
    for h in range(B_HEADS):
        vs = slice(h * B_DV, (h + 1) * B_DV)
        rb = rb_ref[:, vs]
        o_ref[:, vs] = (_rms(o_heads[h], go) * (rb * jax.nn.sigmoid(rb))).astype(o_ref.dtype)


def _gla_prompt_kernel(qb_ref, kb_ref, vb_ref, rb_ref, misc_ref, wg_ref, bg_ref, go_ref, mats_ref,
                       o_ref, s_ref, state_ref):
    ci = pl.program_id(0)

    @pl.when(ci == 0)
    def _():
        state_ref[...] = jnp.zeros_like(state_ref)

    for b in range(qb_ref.shape[0]):
        q, k, att, b_cum, rem = _gla_common(qb_ref.at[b], kb_ref.at[b], misc_ref.at[b], wg_ref, bg_ref,
                                            mats_ref, TILE)
        v = vb_ref[b]
        state = state_ref[b]
        qe = q * jnp.exp(b_cum)
        o_heads = []
        for h in range(B_HEADS):
            ks = slice(h * B_DK, (h + 1) * B_DK)
            vs = slice(h * B_DV, (h + 1) * B_DV)
            o_heads.append(_bdot(qe[:, ks], state[ks, :]) + _bdot(att[h], v[:, vs]))
        _gla_finish(o_heads, rb_ref.at[b], go_ref, o_ref.at[b])

        ke_t = (k * jnp.exp(rem)).T
        e_last = jnp.exp(b_cum[TILE - 1:TILE, :])
        e_col = jnp.broadcast_to(e_last, (TILE, B_HEADS * B_DK)).T[:, 0:1]
        upd = jnp.concatenate(
            [_bdot(ke_t[h * B_DK:(h + 1) * B_DK, :], v[:, h * B_DV:(h + 1) * B_DV]) for h in range(B_HEADS)],
            axis=0)
        new_state = state * e_col + upd
        state_ref[b] = new_state
        s_ref[b] = new_state


def gla_prompt(proj, w_gate, b_gate, g_out, n_batch, seq):
    nc = seq // TILE
    kwid = B_HEADS * B_DK
    vwid = B_HEADS * B_DV
    mats = _gla_sum_matrices(TILE)
    proj3 = proj.reshape(n_batch, seq, proj.shape[1])
    o, s = pl.pallas_call(
        _gla_prompt_kernel,
        out_shape=(jax.ShapeDtypeStruct((n_batch, seq, vwid), BF16),
                   jax.ShapeDtypeStruct((n_batch, kwid, B_DV), F32)),
        grid=(nc,),
        in_specs=[pl.BlockSpec((n_batch, TILE, kwid), lambda c: (0, c, QB_OFF // kwid)),
                  pl.BlockSpec((n_batch, TILE, kwid), lambda c: (0, c, KB_OFF // kwid)),
                  pl.BlockSpec((n_batch, TILE, vwid), lambda c: (0, c, VB_OFF // vwid)),
                  pl.BlockSpec((n_batch, TILE, vwid), lambda c: (0, c, RB_OFF // vwid)),
                  pl.BlockSpec((n_batch, TILE, LANES), lambda c: (0, c, MISC_OFF // LANES)),
                  _resident((GATE_RANK, kwid)),
                  _resident((1, kwid)),
                  _resident((1, B_DV)),
                  _resident(mats.shape)],
        out_specs=(pl.BlockSpec((n_batch, TILE, vwid), lambda c: (0, c, 0)),
                   pl.BlockSpec((n_batch, kwid, B_DV), lambda c: (0, 0, 0))),
        scratch_shapes=[pltpu.VMEM((n_batch, kwid, B_DV), F32)],
        compiler_params=_cparams(("arbitrary",)),
        name="gla_prompt",
    )(proj3, proj3, proj3, proj3, proj3, w_gate, b_gate.reshape(1, kwid), g_out.reshape(1, B_DV), mats)
    return o.reshape(n_batch * seq, vwid), s.reshape(n_batch, B_HEADS, B_DK, B_DV)


def _gla_sample_kernel(qb_ref, kb_ref, vb_ref, rb_ref, misc_ref, wg_ref, bg_ref, go_ref, mats_ref, s0_ref,
                       o_ref, s_ref, *, seg):
    nbt = TILE // seg
    q, k, att, b_cum, rem = _gla_common(qb_ref, kb_ref, misc_ref, wg_ref, bg_ref, mats_ref, seg)
    v = vb_ref[...]
    qe = q * jnp.exp(b_cum)
    ke = k * jnp.exp(rem)
    r1 = lax.broadcasted_iota(I32, (TILE, 1), 0)
    e_last = jnp.where(r1 % seg == seg - 1, jnp.exp(b_cum), 0.0)
    wide = nbt * B_DK
    mq = (lax.broadcasted_iota(I32, (TILE, wide), 0) // seg) == (lax.broadcasted_iota(I32, (TILE, wide), 1) // B_DK)
    mk = (lax.broadcasted_iota(I32, (wide, TILE), 0) // B_DK) == (lax.broadcasted_iota(I32, (wide, TILE), 1) // seg)
    o_heads = []
    for h in range(B_HEADS):
        ks = slice(h * B_DK, (h + 1) * B_DK)
        vs = slice(h * B_DV, (h + 1) * B_DV)
        state = s0_ref[:, h].reshape(wide, B_DV)
        q_bd = jnp.where(mq, jnp.concatenate([qe[:, ks]] * nbt, axis=1), 0.0)
        o_heads.append(_bdot(q_bd, state) + _bdot(att[h], v[:, vs]))
        pair_t = jnp.concatenate([ke[:, ks], e_last[:, ks]], axis=1).T
        k_bd = jnp.where(mk, jnp.concatenate([pair_t[:B_DK]] * nbt, axis=0), 0.0)
        e_bd = jnp.where(mk, jnp.concatenate([pair_t[B_DK:]] * nbt, axis=0), 0.0)
        e_col = jnp.sum(e_bd, axis=-1, keepdims=True)
        new_state = state * e_col + _bdot(k_bd, v[:, vs])
        s_ref[:, h] = new_state.reshape(nbt, B_DK, B_DV)
    _gla_finish(o_heads, rb_ref, go_ref, o_ref)


def gla_sample(proj, w_gate, b_gate, g_out, s0, layer, n_tok):
    T = proj.shape[0]
    nbt = TILE // n_tok
    kwid = B_HEADS * B_DK
    vwid = B_HEADS * B_DV
    mats = _gla_sum_matrices(n_tok)
    return pl.pallas_call(
        functools.partial(_gla_sample_kernel, seg=n_tok),
        out_shape=(jax.ShapeDtypeStruct((T, vwid), BF16),
                   jax.ShapeDtypeStruct(s0.shape[1:], F32)),
        grid=(T // TILE,),
        in_specs=[pl.BlockSpec((TILE, kwid), lambda i: (i, QB_OFF // kwid)),
                  pl.BlockSpec((TILE, kwid), lambda i: (i, KB_OFF // kwid)),
                  pl.BlockSpec((TILE, vwid), lambda i: (i, VB_OFF // vwid)),
                  pl.BlockSpec((TILE, vwid), lambda i: (i, RB_OFF // vwid)),
                  pl.BlockSpec((TILE, LANES), lambda i: (i, MISC_OFF // LANES)),
                  _resident((GATE_RANK, kwid)),
                  _resident((1, kwid)),
                  _resident((1, B_DV)),
                  _resident(mats.shape),
                  pl.BlockSpec((None, nbt, B_HEADS, B_DK, B_DV), lambda i: (layer, i, 0, 0, 0))],
        out_specs=(pl.BlockSpec((TILE, vwid), lambda i: (i, 0)),
                   pl.BlockSpec((nbt, B_HEADS, B_DK, B_DV), lambda i: (i, 0, 0, 0))),
        compiler_params=_cparams(("parallel",)),
        name="gla_sample",
    )(proj, proj, proj, proj, proj, w_gate, b_gate.reshape(1, kwid), g_out.reshape(1, B_DV), mats, s0)


def _gelu(x):
    return jax.nn.gelu(x)


def _gmlp_kernel(uc_ref, vc_ref, gv_ref, ws_ref, bcol_ref, o_ref, vn_ref, *, seg):
    r, c, same_seg = _seg_masks(seg)
    keep = same_seg & (c <= r)
    u = _gelu(uc_ref[...])
    vg = _gelu(vc_ref[...])
    for g in range(C_GROUPS):
        gs = slice(g * C_GROUP_DIM, (g + 1) * C_GROUP_DIM)
        vn = _rms(vg[:, gs], gv_ref[:, gs])
        vn_ref[:, gs] = vn
        w = jnp.where(keep, ws_ref[g], 0.0).astype(BF16)
        s = jnp.dot(w, vn.astype(BF16), preferred_element_type=F32) + bcol_ref[:, g:g + 1]
        o_ref[:, gs] = (u[:, gs] * s).astype(o_ref.dtype)


def gmlp(proj, g_v, w_tiles, b_cols, seg):
    T = proj.shape[0]
    cw = C_GROUPS * C_GROUP_DIM
    return pl.pallas_call(
        functools.partial(_gmlp_kernel, seg=seg),
        out_shape=(jax.ShapeDtypeStruct((T, cw), BF16),
                   jax.ShapeDtypeStruct((T, cw), F32)),
        grid=(T // TILE,),
        in_specs=[pl.BlockSpec((TILE, cw), lambda i: (i, UC_OFF // cw)),
                  pl.BlockSpec((TILE, cw), lambda i: (i, VC_OFF // cw)),
                  _resident((1, cw)),
                  _resident((C_GROUPS, TILE, TILE)),
                  _resident((TILE, C_GROUPS))],
        out_specs=(pl.BlockSpec((TILE, cw), lambda i: (i, 0)),
                   pl.BlockSpec((TILE, cw), lambda i: (i, 0))),
        compiler_params=_cparams(("parallel",)),
        name="gmlp",
    )(proj, proj, g_v.reshape(1, cw), w_tiles, b_cols)


def _ffn_kernel(h_ref, oa_ref, ob_ref, oc_ref, wo_ref, g_ref, wg_ref, wu_ref, wd_ref, o_ref, n_ref):
    j = pl.program_id(1)

    @pl.when(j == 0)
    def _():
        aw = oa_ref.shape[1]
        bw = ob_ref.shape[1]
        h = h_ref[...] + jnp.dot(oa_ref[...], wo_ref[0:aw, :], preferred_element_type=F32)
        h = h + jnp.dot(ob_ref[...], wo_ref[aw:aw + bw, :], preferred_element_type=F32)
        h = h + jnp.dot(oc_ref[...], wo_ref[aw + bw:, :], preferred_element_type=F32)
        n_ref[...] = _rms(h, g_ref[...]).astype(BF16)
        o_ref[...] = h

    n = n_ref[...]
    a = jnp.dot(n, wg_ref[...], preferred_element_type=F32)
    u = jnp.dot(n, wu_ref[...], preferred_element_type=F32)
    act = (a * jax.nn.sigmoid(a) * u).astype(BF16)
    o_ref[...] += jnp.dot(act, wd_ref[...], preferred_element_type=F32)


def out_proj_ffn(h, o_a, o_b, o_c, w_out, g, w_gate, w_up, w_down, layer, tm, tf):
    T, D = h.shape
    tm = min(tm, T)
    FF = w_gate.shape[2]
    return pl.pallas_call(
        _ffn_kernel,
        out_shape=jax.ShapeDtypeStruct((T, D), F32),
        grid=(T // tm, FF // tf),
        in_specs=[pl.BlockSpec((tm, D), lambda i, j: (i, 0)),
                  pl.BlockSpec((tm, o_a.shape[1]), lambda i, j: (i, 0)),
                  pl.BlockSpec((tm, o_b.shape[1]), lambda i, j: (i, 0)),
                  pl.BlockSpec((tm, o_c.shape[1]), lambda i, j: (i, 0)),
                  _layer_resident(w_out.shape[1:], layer),
                  _resident((1, D)),
                  pl.BlockSpec((None, D, tf), lambda i, j: (layer, 0, j)),
                  pl.BlockSpec((None, D, tf), lambda i, j: (layer, 0, j)),
                  pl.BlockSpec((None, tf, D), lambda i, j: (layer, j, 0))],
        out_specs=pl.BlockSpec((tm, D), lambda i, j: (i, 0)),
        scratch_shapes=[pltpu.VMEM((tm, D), BF16)],
        compiler_params=_cparams(("parallel", "arbitrary")),
        name="out_proj_ffn",
    )(h, o_a, o_b, o_c, w_out, g.reshape(1, D), w_gate, w_up, w_down)


def _ple_kernel(h_ref, p_ref, g_ref, wgate_ref, wproj_ref, o_ref):
    h = h_ref[...]
    n = _rms(h, g_ref[...]).astype(BF16)
    gate = jax.nn.sigmoid(jnp.dot(n, wgate_ref[...], preferred_element_type=F32))
    emb = jnp.dot(p_ref[...].astype(BF16), wproj_ref[...], preferred_element_type=F32)
    o_ref[...] = h + gate * emb


def ple(h, p, g, w_gate, w_proj, layer, tm):
    T, D = h.shape
    tm = min(tm, T)
    P = p.shape[2]
    return pl.pallas_call(
        _ple_kernel,
        out_shape=jax.ShapeDtypeStruct((T, D), F32),
        grid=(T // tm,),
        in_specs=[pl.BlockSpec((tm, D), lambda i: (i, 0)),
                  pl.BlockSpec((None, tm, P), lambda i: (layer, i, 0)),
                  _resident((1, D)),
                  _layer_resident(w_gate.shape[1:], layer),
                  _layer_resident(w_proj.shape[1:], layer)],
        out_specs=pl.BlockSpec((tm, D), lambda i: (i, 0)),
        compiler_params=_cparams(("parallel",)),
        name="ple",
    )(h, p, g.reshape(1, D), w_gate, w_proj)


_W_IN_SEGMENTS = (("q", 1024), ("k", 256), ("v", 256), ("qi", 1024), ("ki", 64), ("wi", 16), ("qb", 256),
                  ("kb", 256), ("vb", 512), ("gb", 16), ("rb", 512), ("uc", 512), ("vc", 512))
_W_IN_PACKED_ORDER = ("q", "qi", "vb", "rb", "uc", "vc", "k", "v", "qb", "kb", "ki", "wi", "gb")


def _pack_kernel(wt_ref, o_ref):
    src, start = {}, 0
    for name, size in _W_IN_SEGMENTS:
        src[name] = (start, size)
        start += size
    dst = 0
    small = []
    for name in _W_IN_PACKED_ORDER:
        s0, size = src[name]
        if size < LANES:
            small.append(wt_ref[s0:s0 + size, :])
            continue
        o_ref[:, dst:dst + size] = wt_ref[s0:s0 + size, :].T.astype(BF16)
        dst += size
    used = sum(x.shape[0] for x in small)
    small.append(jnp.zeros((LANES - used, wt_ref.shape[1]), F32))
    o_ref[:, dst:dst + LANES] = jnp.concatenate(small, axis=0).T.astype(BF16)


def _pack_w_in(w, tr=256):
    depth, D, N = w.shape
    return pl.pallas_call(
        _pack_kernel,
        out_shape=jax.ShapeDtypeStruct((depth, D, PROJ_PACKED), BF16),
        grid=(depth, D // tr),
        in_specs=[pl.BlockSpec((None, N, tr), lambda l, i: (l, 0, i))],
        out_specs=pl.BlockSpec((None, tr, PROJ_PACKED), lambda l, i: (l, i, 0)),
        compiler_params=_cparams(("parallel", "parallel")),
        name="pack_w_in",
    )(jnp.swapaxes(w, 1, 2))


def _mixer_tail(h, o_a, o_b, o_c, p_all, wts, layer, tm, tm_ffn):
    h = out_proj_ffn(h, o_a, o_b, o_c, wts["w_out"], wts["g_ffn"][layer], wts["w_ffn_gate"], wts["w_ffn_up"],
                     wts["w_ffn_down"], layer, tm_ffn, 512)
    return ple(h, p_all, wts["g_ple"][layer], wts["w_ple_gate"], wts["w_ple_proj"], layer, tm)


def kernel(x_prompt, x_sample, cache_k, cache_v, cache_idx_k, state_gla, page_table, p_prompt, p_sample,
           g_mix, w_in, q_norm_g, k_norm_g, rel_bias, w_gate_b, b_gate_b, g_out_b, g_v_c, w_spatial,
           b_spatial, w_out, g_ffn, w_ffn_gate, w_ffn_up, w_ffn_down, g_ple, w_ple_gate, w_ple_proj):
    n_batch, seq, d_model = x_prompt.shape
    dec_batch, dec_seq, _ = x_sample.shape
    depth = w_in.shape[0]
    n_pages = page_table.shape[1]
    past = n_pages * PAGE_SIZE
    kw = A_KV_HEADS * HEAD_DIM
    tp, ts = n_batch * seq, dec_batch * dec_seq
    rows_pad = SUBLANES
    l_sample = past + PAGE_SIZE
    k_sel_s = min(TOPK_MAX, (past + dec_seq) // 4)

    bias_p = bias_table_prompt(rel_bias)
    bias_s = bias_table_sample(rel_bias, past, dec_seq, A_KV_HEADS * l_sample)
    cache_ik_t = jnp.swapaxes(cache_idx_k, 2, 3)
    cache_k2 = cache_k.reshape(depth, cache_k.shape[1], PAGE_SIZE * A_KV_HEADS, HEAD_DIM)
    cache_v2 = cache_v.reshape(depth, cache_v.shape[1], PAGE_SIZE * A_KV_HEADS, HEAD_DIM)

    hp = x_prompt.reshape(tp, d_model)
    hs = x_sample.reshape(ts, d_model)
    outs = {k: [] for k in ("kp", "vp", "ikp", "sp", "ks", "vs", "iks", "ss", "cs")}
    slot = (jnp.arange(dec_batch) % (rows_pad // dec_seq)) * dec_seq
    place_t = (jnp.arange(rows_pad)[None, :, None]
               == slot[:, None, None] + jnp.arange(dec_seq)[None, None, :]).astype(F32)
    wts = dict(w_out=w_out.astype(BF16), g_ffn=g_ffn, w_ffn_gate=w_ffn_gate.astype(BF16),
               w_ffn_up=w_ffn_up.astype(BF16), w_ffn_down=w_ffn_down.astype(BF16),
               g_ple=g_ple, w_ple_gate=w_ple_gate.astype(BF16), w_ple_proj=w_ple_proj.astype(BF16))
    w_packed = _pack_w_in(w_in)
    pp_all = p_prompt.reshape(depth, tp, -1)
    ps_all = p_sample.reshape(depth, ts, -1)
    for i in range(depth):
        b_cols_p = b_spatial[i].T
        reps = TILE // dec_seq
        w_tiles_s = jnp.tile(w_spatial[i][:, :dec_seq, :dec_seq], (1, reps, reps))
        b_cols_s = jnp.tile(b_spatial[i][:, :dec_seq].T, (reps, 1))

        proj = in_projection(hp, g_mix[i], w_packed, i, 256)
        kn, vv, ik, vt = kv_post(proj, k_norm_g[i], 512, True)
        o_a = dsa_prompt(proj, kn, vt, bias_p, q_norm_g[i], n_batch, seq)
        o_b, s_p = gla_prompt(proj, w_gate_b[i], b_gate_b[i], g_out_b[i], n_batch, seq)
        o_c, _ = gmlp(proj, g_v_c[i], w_spatial[i], b_cols_p, TILE)
        hp = _mixer_tail(hp, o_a, o_b, o_c, pp_all, wts, i, 256, 512)
        outs["kp"].append(kn.reshape(n_batch, seq, A_KV_HEADS, HEAD_DIM))
        outs["vp"].append(vv.reshape(n_batch, seq, A_KV_HEADS, HEAD_DIM))
        outs["ikp"].append(ik.reshape(n_batch, seq, IDX_DIM))
        outs["sp"].append(s_p)

        proj = in_projection(hs, g_mix[i], w_packed, i, 256)
        kn, vv, ik = kv_post(proj, k_norm_g[i], 512, False)
        qi_rows =proj[:, QI_OFF:QI_OFF + IDX_HEADS * IDX_DIM].reshape(dec_batch, dec_seq * IDX_HEADS, IDX_DIM)
        wi = proj[:, MISC_OFF + MISC_WI:MISC_OFF + MISC_WI + IDX_HEADS].reshape(dec_batch, dec_seq, IDX_HEADS)
        wi = wi * (IDX_HEADS ** -0.5 * IDX_DIM ** -0.5)
        wmat = (place_t[:, :, :, None] * wi[:, None, :, :]).reshape(dec_batch, rows_pad, dec_seq * IDX_HEADS)
        ki_new_t = jnp.pad(jnp.swapaxes(ik.reshape(dec_batch, dec_seq, IDX_DIM), 1, 2),
                           ((0, 0), (0, 0), (0, PAGE_SIZE - dec_seq)))
        mask = dsa_sample_select(cache_ik_t, i, page_table, qi_rows, wmat, ki_new_t, dec_seq, k_sel_s)
        q_rows = proj[:, Q_OFF:Q_OFF + A_HEADS * HEAD_DIM].reshape(dec_batch, dec_seq * A_HEADS, HEAD_DIM)
        o_a = dsa_sample_attend(cache_k2, cache_v2, i, page_table, q_rows,
                                kn.reshape(dec_batch, dec_seq * A_KV_HEADS, HEAD_DIM),
                                vv.reshape(dec_batch, dec_seq * A_KV_HEADS, HEAD_DIM), mask, bias_s, q_norm_g[i])
        o_a = o_a.reshape(ts, A_HEADS * HEAD_DIM)
        o_b, s_s = gla_sample(proj, w_gate_b[i], b_gate_b[i], g_out_b[i], state_gla, i, dec_seq)
        o_c, vn = gmlp(proj, g_v_c[i], w_tiles_s, b_cols_s, dec_seq)
        hs = _mixer_tail(hs, o_a, o_b, o_c, ps_all, wts, i, 256, 512)
        outs["ks"].append(kn.reshape(dec_batch, dec_seq, A_KV_HEADS, HEAD_DIM))
        outs["vs"].append(vv.reshape(dec_batch, dec_seq, A_KV_HEADS, HEAD_DIM))
        outs["iks"].append(ik.reshape(dec_batch, dec_seq, IDX_DIM))
        outs["ss"].append(s_s)
        outs["cs"].append(vn.reshape(dec_batch, dec_seq, -1))

    st = {k: jnp.stack(v) for k, v in outs.items()}
    return (hp.reshape(n_batch, seq, d_model), hs.reshape(dec_batch, dec_seq, d_model),
            st["kp"], st["vp"], st["ikp"], st["sp"], st["ks"], st["vs"], st["iks"], st["ss"], st["cs"])
```

```python
import functools
import math

import jax
import jax.numpy as jnp
from jax import lax
from jax.experimental import pallas as pl
from jax.experimental.pallas import tpu as pltpu

F32 = jnp.float32
BF16 = jnp.bfloat16
I32 = jnp.int32
HIGHEST = lax.Precision.HIGHEST

LANES = 128
SUBLANES = 8
VMEM_LIMIT = 56 * 1024 * 1024

HEAD_DIM = 128
A_HEADS = 8
A_KV_HEADS = 2
IDX_HEADS = 16
IDX_DIM = 64
TOPK_MAX = 256
NUM_BUCKETS = 32
MAX_DISTANCE = 128
B_HEADS = 4
B_DK = 64
B_DV = 128
GATE_RANK = 16
GATE_TEMP = 16.0
C_GROUPS = 4
C_GROUP_DIM = 128
PAGE_SIZE = 128
EPS = 1e-6
NEG_BIG = -1e30
INT_MIN = -(2 ** 31)
NEG_INF_KEY = -2139095041

TILE = 128
QBLK = 256

Q_OFF, QI_OFF, VB_OFF, RB_OFF, UC_OFF, VC_OFF = 0, 1024, 2048, 2560, 3072, 3584
K_OFF, V_OFF, QB_OFF, KB_OFF, MISC_OFF = 4096, 4352, 4608, 4864, 5120
PROJ_PACKED = 5248
MISC_KI, MISC_WI, MISC_GB = 0, 64, 80


def _cparams(sem):
    return pltpu.CompilerParams(dimension_semantics=sem, vmem_limit_bytes=VMEM_LIMIT)


def _rms(x, g):
    return x * lax.rsqrt(jnp.mean(x * x, axis=-1, keepdims=True) + EPS) * g


def _resident(shape):
    nd = len(shape)
    return pl.BlockSpec(shape, lambda *_: (0,) * nd, pipeline_mode=pl.Buffered(1))


def _layer_resident(shape, layer):
    nd = len(shape)
    return pl.BlockSpec((None,) + tuple(shape), lambda *_: (layer,) + (0,) * nd, pipeline_mode=pl.Buffered(1))


def _proj_kernel(x_ref, g_ref, w_ref, o_ref):
    n = _rms(x_ref[...], g_ref[...]).astype(BF16)
    ncol = o_ref.shape[1]
    step = 512
    for c0 in range(0, ncol, step):
        c1 = min(c0 + step, ncol)
        o_ref[:, c0:c1] = jnp.dot(n, w_ref[:, c0:c1], preferred_element_type=F32)


def in_projection(h, g, w_packed, layer, tm):
    T, D = h.shape
    tm = min(tm, T)
    N = w_packed.shape[2]
    return pl.pallas_call(
        _proj_kernel,
        out_shape=jax.ShapeDtypeStruct((T, N), F32),
        grid=(T // tm,),
        in_specs=[pl.BlockSpec((tm, D), lambda i: (i, 0)),
                  _resident((1, D)),
                  _layer_resident((D, N), layer)],
        out_specs=pl.BlockSpec((tm, N), lambda i: (i, 0)),
        compiler_params=_cparams(("parallel",)),
        name="in_projection",
    )(h, g.reshape(1, D), w_packed)


def _kv_kernel(k_ref, v_ref, m_ref, g_ref, ko_ref, vo_ref, io_ref, vt_ref=None):
    g = g_ref[...]
    k = k_ref[...]
    for hh in range(A_KV_HEADS):
        sl = slice(hh * HEAD_DIM, (hh + 1) * HEAD_DIM)
        ko_ref[:, sl] = _rms(k[:, sl], g)
    v = v_ref[...]
    vo_ref[...] = v
    io_ref[...] = m_ref[:, MISC_KI:MISC_KI + IDX_DIM]
    if vt_ref is not None:
        for blk in range(vt_ref.shape[0]):
            vt_ref[blk] = v[blk * QBLK:(blk + 1) * QBLK, :].T.astype(vt_ref.dtype)


def kv_post(proj, k_norm_g, tm, with_vt):
    T = proj.shape[0]
    tm = min(tm, T)
    kw = A_KV_HEADS * HEAD_DIM
    out_shape = [jax.ShapeDtypeStruct((T, kw), F32),
                 jax.ShapeDtypeStruct((T, kw), F32),
                 jax.ShapeDtypeStruct((T, IDX_DIM), F32)]
    out_specs = [pl.BlockSpec((tm, kw), lambda i: (i, 0)),
                 pl.BlockSpec((tm, kw), lambda i: (i, 0)),
                 pl.BlockSpec((tm, IDX_DIM), lambda i: (i, 0))]
    if with_vt:
        out_shape.append(jax.ShapeDtypeStruct((T // QBLK, kw, QBLK), BF16))
        out_specs.append(pl.BlockSpec((tm // QBLK, kw, QBLK), lambda i: (i, 0, 0)))
    return pl.pallas_call(
        _kv_kernel,
        out_shape=tuple(out_shape),
        grid=(T // tm,),
        in_specs=[pl.BlockSpec((tm, kw), lambda i: (i, K_OFF // kw)),
                  pl.BlockSpec((tm, kw), lambda i: (i, V_OFF // kw)),
                  pl.BlockSpec((tm, LANES), lambda i: (i, MISC_OFF // LANES)),
                  _resident((1, HEAD_DIM))],
        out_specs=tuple(out_specs),
        compiler_params=_cparams(("parallel",)),
        name="kv_post",
    )(proj, proj, proj, k_norm_g.reshape(1, HEAD_DIM))


def _bucket(dist):
    n = jnp.maximum(dist, 0)
    max_exact = NUM_BUCKETS // 2
    large = max_exact + (jnp.log(jnp.maximum(n, 1).astype(F32) / max_exact)
                         / math.log(MAX_DISTANCE / max_exact)
                         * (NUM_BUCKETS - max_exact)).astype(I32)
    large = jnp.minimum(large, NUM_BUCKETS - 1)
    return jnp.where(n < max_exact, n, large)


def _bias_prompt_kernel(rb_ref, o_ref):
    c = lax.broadcasted_iota(I32, (TILE, TILE), 0)
    t = lax.broadcasted_iota(I32, (TILE, TILE), 1)
    for z in range(3):
        bucket = _bucket(t - c + (2 - z) * TILE)
        for h in range(A_HEADS):
            acc = jnp.zeros((TILE, TILE), F32)
            for b in range(NUM_BUCKETS):
                acc = jnp.where(bucket == b, rb_ref[b, h], acc)
            o_ref[h, z] = acc


def bias_table_prompt(rel_bias):
    return pl.pallas_call(
        _bias_prompt_kernel,
        out_shape=jax.ShapeDtypeStruct((A_HEADS, 3, TILE, TILE), F32),
        in_specs=[pl.BlockSpec(memory_space=pltpu.SMEM)],
        out_specs=pl.BlockSpec(memory_space=pltpu.VMEM),
        name="bias_table_prompt",
    )(rel_bias)


def _bias_sample_kernel(rbrows_ref, o_ref, *, past, n_tok):
    rows, L = o_ref.shape
    r = lax.broadcasted_iota(I32, (rows, L), 0)
    s = lax.broadcasted_iota(I32, (rows, L), 1) // A_KV_HEADS
    bucket = _bucket(past + r // A_HEADS - s)
    rbrows = rbrows_ref[...]
    acc = jnp.zeros((rows, L), F32)
    for b in range(NUM_BUCKETS):
        acc = jnp.where(bucket == b, rbrows[:, b:b + 1], acc)
    o_ref[...] = acc


def bias_table_sample(rel_bias, past, n_tok, L):
    rows = n_tok * A_HEADS
    rbrows = jnp.tile(rel_bias.T, (n_tok, 1))
    return pl.pallas_call(
        functools.partial(_bias_sample_kernel, past=past, n_tok=n_tok),
        out_shape=jax.ShapeDtypeStruct((rows, L), F32),
        name="bias_table_sample",
    )(rbrows)


def _sortable_key(x):
    b = lax.bitcast_convert_type(x, I32)
    return b ^ ((b >> 31) & 0x7FFFFFFF)


def _topk_member(skey_ref, k_sel):
    R, L = skey_ref.shape

    def body(it, ans):
        bit = 31 - it
        cand = ans | lax.shift_left(jnp.int32(1), bit)
        cand_s = cand ^ INT_MIN
        cnt = jnp.sum(jnp.where(skey_ref[...] >= cand_s, 1.0, 0.0), axis=-1, keepdims=True)
        return jnp.where(cnt >= k_sel, cand, ans)

    ans = lax.fori_loop(0, 32, body, jnp.zeros((R, 1), I32))
    tau = ans ^ INT_MIN
    skey = skey_ref[...]
    gt = skey > tau
    eq = skey == tau
    n_gt = jnp.sum(jnp.where(gt, 1.0, 0.0), axis=-1, keepdims=True)
    room = k_sel - n_gt
    r_i = lax.broadcasted_iota(I32, (LANES, LANES), 0)
    c_i = lax.broadcasted_iota(I32, (LANES, LANES), 1)
    upper = jnp.where(r_i <= c_i, 1.0, 0.0).astype(BF16)
    off = jnp.zeros((R, 1), F32)
    parts = []
    for j in range(L // LANES):
        sl = slice(j * LANES, (j + 1) * LANES)
        eq_j = eq[:, sl]
        run = jnp.dot(jnp.where(eq_j, 1.0, 0.0).astype(BF16), upper, preferred_element_type=F32) + off
        parts.append(gt[:, sl] | (eq_j & (run <= room)))
        off = run[:, LANES - 1:LANES]
    return jnp.concatenate(parts, axis=1)


def _fold8(x, op):
    return op(x.reshape(x.shape[0] // SUBLANES, SUBLANES, x.shape[1]), axis=0)


def _dsa_prompt_kernel(q_ref, qi_ref, misc_ref, kn_ref, vt_ref, bias_ref, qg_ref, o_ref,
                       qst_ref, skey_ref, madd_ref, lg_ref, acc_ref, *, k_sel):
    i = pl.program_id(1)
    nkb = i + 1
    sub = QBLK // TILE
    rep = A_HEADS // A_KV_HEADS
    row0 = pl.multiple_of(i * QBLK, QBLK)
    s_iota = lax.broadcasted_iota(I32, (QBLK, QBLK), 0)
    t_iota = lax.broadcasted_iota(I32, (QBLK, QBLK), 1)

    def admissible(j):
        return (j * QBLK + s_iota) <= (row0 + t_iota)

    wi_t = misc_ref[pl.ds(row0, QBLK), :].T[MISC_WI:MISC_WI + IDX_HEADS, :]
    wi_t = wi_t * (IDX_HEADS ** -0.5 * IDX_DIM ** -0.5)
    for h in range(IDX_HEADS):
        qst_ref[h * QBLK:(h + 1) * QBLK, :] = qi_ref[:, h * IDX_DIM:(h + 1) * IDX_DIM].astype(BF16)

    def score_body(j, carry):
        k0 = pl.multiple_of(j * QBLK, QBLK)
        kj = misc_ref[pl.ds(k0, QBLK), MISC_KI:MISC_KI + IDX_DIM].astype(BF16)
        s = lax.dot_general(kj, qst_ref[...], (((1,), (1,)), ((), ())), preferred_element_type=F32)
        score = jnp.zeros((QBLK, QBLK), F32)
        for h in range(IDX_HEADS):
            score = score + jnp.maximum(s[:, h * QBLK:(h + 1) * QBLK], 0.0) * wi_t[h:h + 1, :]
        skey_ref[j] = _sortable_key(jnp.where(admissible(j), score, -jnp.inf))
        return carry

    lax.fori_loop(0, nkb, score_body, 0)

    def count(pred_fn):
        def body(j, acc):
            return acc + _fold8(jnp.where(pred_fn(skey_ref[j]), 1.0, 0.0), jnp.sum)
        acc = lax.fori_loop(0, nkb, body, jnp.zeros((SUBLANES, QBLK), F32))
        return jnp.sum(acc, axis=0, keepdims=True)

    def bit_body(it, ans):
        cand = ans | lax.shift_left(jnp.int32(1), 31 - it)
        cand_s = cand ^ INT_MIN
        cnt = count(lambda key: key >= cand_s)
        return jnp.where(cnt >= k_sel, cand, ans)

    ans = lax.fori_loop(0, 32, bit_body, jnp.zeros((1, QBLK), I32))
    tau = ans ^ INT_MIN
    n_ge = count(lambda key: key >= tau)
    excess = jnp.max(jnp.where((n_ge > k_sel) & (tau != NEG_INF_KEY), 1.0, 0.0))

    @pl.when(excess == 0.0)
    def _():
        def mask_body(j, carry):
            madd_ref[j] = jnp.where((skey_ref[j] >= tau) & admissible(j), 0.0, NEG_BIG)
            return carry

        lax.fori_loop(0, nkb, mask_body, 0)

    @pl.when(excess > 0.0)
    def _():
        room = k_sel - count(lambda key: key > tau)
        lower = jnp.where(t_iota <= s_iota, 1.0, 0.0).astype(BF16)

        def mask_body(j, off):
            key = skey_ref[j]
            eq = key == tau
            run = jnp.dot(lower, jnp.where(eq, 1.0, 0.0).astype(BF16), preferred_element_type=F32) + off
            sel = ((key > tau) | (eq & (run <= room))) & admissible(j)
            madd_ref[j] = jnp.where(sel, 0.0, NEG_BIG)
            return run[QBLK - 1:QBLK, :]

        lax.fori_loop(0, nkb, mask_body, jnp.zeros((1, QBLK), F32))

    qg = qg_ref[...]
    wide = rep * QBLK
    for g in range(A_KV_HEADS):
        gs = slice(g * HEAD_DIM, (g + 1) * HEAD_DIM)
        heads = list(range(g * rep, (g + 1) * rep))
        q_stack = jnp.concatenate(
            [(_rms(q_ref[:, h * HEAD_DIM:(h + 1) * HEAD_DIM], qg) * HEAD_DIM ** -0.5).astype(BF16) for h in heads],
            axis=0)

        def logit_body(j, mx):
            k0 = pl.multiple_of(j * QBLK, QBLK)
            kj = kn_ref[pl.ds(k0, QBLK), gs].astype(BF16)
            lg = lax.dot_general(kj, q_stack, (((1,), (1,)), ((), ())), preferred_element_type=F32)
            madd = madd_ref[j]
            parts = []
            for r, h in enumerate(heads):
                quads = []
                for c in range(sub):
                    quads.append(jnp.concatenate(
                        [bias_ref[h, jnp.clip(2 - ((i - j) * sub + u - c), 0, 2)] for u in range(sub)], axis=1))
                parts.append(lg[:, r * QBLK:(r + 1) * QBLK] + jnp.concatenate(quads, axis=0) + madd)
            lg = jnp.concatenate(parts, axis=1)
            lg_ref[j] = lg
            return jnp.maximum(mx, _fold8(lg, jnp.max))

        mx = lax.fori_loop(0, nkb, logit_body, jnp.full((SUBLANES, wide), NEG_BIG, F32))
        m = jnp.max(mx, axis=0, keepdims=True)
        acc_ref[...] = jnp.zeros(acc_ref.shape, F32)

        def pv_body(j, sm):
            p = jnp.exp(lg_ref[j] - m)
            acc_ref[...] += jnp.dot(vt_ref[j, gs, :], p.astype(BF16), preferred_element_type=F32)
            return sm + _fold8(p, jnp.sum)

        sm = lax.fori_loop(0, nkb, pv_body, jnp.zeros((SUBLANES, wide), F32))
        den = jnp.sum(sm, axis=0, keepdims=True)
        o = (acc_ref[...] / den).T
        for r, h in enumerate(heads):
            o_ref[:, h * HEAD_DIM:(h + 1) * HEAD_DIM] = o[r * QBLK:(r + 1) * QBLK, :].astype(o_ref.dtype)


def dsa_prompt(proj, kn, vt, bias_tab, q_norm_g, n_batch, seq):
    T = proj.shape[0]
    nb = seq // QBLK
    k_sel = min(TOPK_MAX, seq // 4)
    aw = A_HEADS * HEAD_DIM
    iw = IDX_HEADS * IDX_DIM
    kw = A_KV_HEADS * HEAD_DIM
    rep = A_HEADS // A_KV_HEADS
    return pl.pallas_call(
        functools.partial(_dsa_prompt_kernel, k_sel=k_sel),
        out_shape=jax.ShapeDtypeStruct((T, aw), BF16),
        grid=(n_batch, nb),
        in_specs=[pl.BlockSpec((QBLK, aw), lambda b, i: (b * nb + i, Q_OFF // aw)),
                  pl.BlockSpec((QBLK, iw), lambda b, i: (b * nb + i, QI_OFF // iw)),
                  pl.BlockSpec((seq, LANES), lambda b, i: (b, MISC_OFF // LANES)),
                  pl.BlockSpec((seq, kw), lambda b, i: (b, 0)),
                  pl.BlockSpec((nb, kw, QBLK), lambda b, i: (b, 0, 0)),
                  _resident((A_HEADS, 3, TILE, TILE)),
                  _resident((1, HEAD_DIM))],
        out_specs=pl.BlockSpec((QBLK, aw), lambda b, i: (b * nb + i, 0)),
        scratch_shapes=[pltpu.VMEM((IDX_HEADS * QBLK, IDX_DIM), BF16),
                        pltpu.VMEM((nb, QBLK, QBLK), I32),
                        pltpu.VMEM((nb, QBLK, QBLK), F32),
                        pltpu.VMEM((nb, QBLK, rep * QBLK), F32),
                        pltpu.VMEM((HEAD_DIM, rep * QBLK), F32)],
        compiler_params=_cparams(("parallel", "arbitrary")),
        name="dsa_prompt",
    )(proj, proj, proj, kn, vt, bias_tab, q_norm_g.reshape(1, HEAD_DIM))


def _dsa_sample_select_kernel(pt_ref, *refs, n_pages, n_tok, k_sel, rows_pad):
    del pt_ref
    per = rows_pad // n_tok
    page_refs = refs[:per * n_pages]
    qi_ref, wm_ref, kin_ref, mask_ref, sc_ref, skey_ref = refs[per * n_pages:]
    b = pl.program_id(0)
    nb = pl.num_programs(0)
    L = sc_ref.shape[1]
    past = n_pages * PAGE_SIZE

    relu_s = []
    for e in range(per):
        kt_all = jnp.concatenate([r[...].astype(BF16) for r in page_refs[e * n_pages:(e + 1) * n_pages]]
                                 + [kin_ref[e].astype(BF16)], axis=1)
        relu_s.append(jnp.maximum(jnp.dot(qi_ref[e].astype(BF16), kt_all, preferred_element_type=F32), 0.0))
    score = jnp.dot(wm_ref[0], jnp.concatenate(relu_s, axis=0), precision=HIGHEST,
                    preferred_element_type=F32)
    r0 = pl.multiple_of(b * rows_pad, rows_pad)
    sc_ref[pl.ds(r0, rows_pad), :] = score

    @pl.when(b == nb - 1)
    def _():
        n_blocks = sc_ref.shape[0] // TILE
        n_tiles = L // PAGE_SIZE
        tp = past + lax.broadcasted_iota(I32, (TILE, L), 0) % n_tok
        sp = lax.broadcasted_iota(I32, (TILE, L), 1)
        adm_blk = sp <= tp
        d_r = lax.broadcasted_iota(I32, (PAGE_SIZE, PAGE_SIZE * A_KV_HEADS), 0)
        d_c = lax.broadcasted_iota(I32, (PAGE_SIZE, PAGE_SIZE * A_KV_HEADS), 1)
        dup = jnp.where(d_c // A_KV_HEADS == d_r, 1.0, 0.0).astype(BF16)
        for rb in range(n_blocks):
            rows = slice(rb * TILE, (rb + 1) * TILE)
            skey_ref[...] = _sortable_key(jnp.where(adm_blk, sc_ref[rows, :], -jnp.inf))
            sel = jnp.where(_topk_member(skey_ref, k_sel) & adm_blk, 1.0, 0.0).astype(BF16)
            stacked = jnp.concatenate([sel[:, j * PAGE_SIZE:(j + 1) * PAGE_SIZE] for j in range(n_tiles)], axis=0)
            stacked = jnp.dot(stacked, dup, preferred_element_type=F32)
            mask_ref[rows, :] = jnp.concatenate(
                [stacked[j * TILE:(j + 1) * TILE, :] for j in range(n_tiles)], axis=1)


def dsa_sample_select(cache_ik_t, layer, page_table, qi_rows, wmat, ki_new_t, n_tok, k_sel):
    DB, n_pages = page_table.shape
    rows_pad = wmat.shape[1]
    per = rows_pad // n_tok
    n_rows = DB // per * rows_pad
    L = (n_pages + 1) * PAGE_SIZE
    page_specs = [pl.BlockSpec((None, None, IDX_DIM, PAGE_SIZE), functools.partial(
        lambda b, pt, e, p: (layer, pt[b * per + e, p], 0, 0), e=e, p=p))
        for e in range(per) for p in range(n_pages)]
    grid_spec = pltpu.PrefetchScalarGridSpec(
        num_scalar_prefetch=1,
        grid=(DB // per,),
        in_specs=page_specs + [
            pl.BlockSpec((per,) + qi_rows.shape[1:], lambda b, pt: (b, 0, 0)),
            pl.BlockSpec((1,) + wmat.shape[1:], lambda b, pt: (b, 0, 0)),
            pl.BlockSpec((per, IDX_DIM, PAGE_SIZE), lambda b, pt: (b, 0, 0))],
        out_specs=pl.BlockSpec((n_rows, A_KV_HEADS * L), lambda b, pt: (0, 0)),
        scratch_shapes=[pltpu.VMEM((n_rows, L), F32),
                        pltpu.VMEM((TILE, L), I32)],
    )
    return pl.pallas_call(
        functools.partial(_dsa_sample_select_kernel, n_pages=n_pages, n_tok=n_tok, k_sel=k_sel,
                          rows_pad=rows_pad),
        out_shape=jax.ShapeDtypeStruct((n_rows, A_KV_HEADS * L), F32),
        grid_spec=grid_spec,
        compiler_params=_cparams(("arbitrary",)),
        name="dsa_sample_select",
    )(page_table, *([cache_ik_t] * (per * n_pages)), qi_rows, wmat, ki_new_t)


def _dsa_sample_attend_kernel(pt_ref, *refs, n_pages, n_tok, rows_pad):
    del pt_ref
    per = rows_pad // n_tok
    k_refs = refs[:per * n_pages]
    v_refs = refs[per * n_pages:2 * per * n_pages]
    q_ref, kn_ref, vn_ref, mask_ref, bias_ref, qg_ref, o_ref = refs[2 * per * n_pages:]
    rows = n_tok * A_HEADS
    page_rows = PAGE_SIZE * A_KV_HEADS
    n_cols = mask_ref.shape[1]
    pad = jnp.zeros((page_rows - n_tok * A_KV_HEADS, HEAD_DIM), BF16)
    rep = A_HEADS // A_KV_HEADS
    grp = (lax.broadcasted_iota(I32, (rows, 1), 0) % A_HEADS) // rep
    own_group = (lax.broadcasted_iota(I32, (rows, n_cols), 1) % A_KV_HEADS) == grp
    member = mask_ref[...]

    for e in range(per):
        def gather(page_refs, new_ref):
            return jnp.concatenate([r[...].astype(BF16) for r in page_refs[e * n_pages:(e + 1) * n_pages]]
                                   + [new_ref[e].astype(BF16), pad], axis=0)

        q = (_rms(q_ref[e], qg_ref[...]) * HEAD_DIM ** -0.5).astype(BF16)
        sel = jnp.concatenate(
            [jnp.broadcast_to(member[e * n_tok + t:e * n_tok + t + 1, :], (A_HEADS, n_cols)) for t in range(n_tok)],
            axis=0)
        valid = (sel > 0.5) & own_group
        logits = lax.dot_general(q, gather(k_refs, kn_ref), (((1,), (1,)), ((), ())), preferred_element_type=F32)
        logits = jnp.where(valid, logits + bias_ref[...], NEG_BIG)
        m = jnp.max(logits, axis=-1, keepdims=True)
        p = jnp.exp(logits - m)
        den = jnp.sum(p, axis=-1, keepdims=True)
        o = jnp.dot(p.astype(BF16), gather(v_refs, vn_ref), preferred_element_type=F32)
        o_ref[e] = (o / den).astype(o_ref.dtype)


def dsa_sample_attend(cache_k, cache_v, layer, page_table, q_rows, k_new, v_new, mask, bias_tab, q_norm_g):
    DB, n_pages = page_table.shape
    n_tok = k_new.shape[1] // A_KV_HEADS
    rows = n_tok * A_HEADS
    rows_pad = SUBLANES
    per = rows_pad // n_tok
    n_cols = mask.shape[1]
    page_rows = PAGE_SIZE * A_KV_HEADS
    page_specs = [pl.BlockSpec((None, None, page_rows, HEAD_DIM), functools.partial(
        lambda b, pt, e, p: (layer, pt[b * per + e, p], 0, 0), e=e, p=p))
        for e in range(per) for p in range(n_pages)]
    grid_spec = pltpu.PrefetchScalarGridSpec(
        num_scalar_prefetch=1,
        grid=(DB // per,),
        in_specs=page_specs + page_specs + [
            pl.BlockSpec((per, rows, HEAD_DIM), lambda b, pt: (b, 0, 0)),
            pl.BlockSpec((per, n_tok * A_KV_HEADS, HEAD_DIM), lambda b, pt: (b, 0, 0)),
            pl.BlockSpec((per, n_tok * A_KV_HEADS, HEAD_DIM), lambda b, pt: (b, 0, 0)),
            pl.BlockSpec((rows_pad, n_cols), lambda b, pt: (b, 0)),
            pl.BlockSpec((rows, n_cols), lambda b, pt: (0, 0), pipeline_mode=pl.Buffered(1)),
            pl.BlockSpec((1, HEAD_DIM), lambda b, pt: (0, 0), pipeline_mode=pl.Buffered(1))],
        out_specs=pl.BlockSpec((per, rows, HEAD_DIM), lambda b, pt: (b, 0, 0)),
    )
    return pl.pallas_call(
        functools.partial(_dsa_sample_attend_kernel, n_pages=n_pages, n_tok=n_tok, rows_pad=rows_pad),
        out_shape=jax.ShapeDtypeStruct((DB, rows, HEAD_DIM), BF16),
        grid_spec=grid_spec,
        compiler_params=_cparams(("parallel",)),
        name="dsa_sample_attend",
    )(page_table, *([cache_k] * (per * n_pages)), *([cache_v] * (per * n_pages)), q_rows, k_new, v_new, mask,
      bias_tab, q_norm_g.reshape(1, HEAD_DIM))


def _log_sigmoid(z):
    return jnp.minimum(z, 0.0) - jnp.log(1.0 + jnp.exp(-jnp.abs(z)))


def _seg_masks(seg):
    r = lax.broadcasted_iota(I32, (TILE, TILE), 0)
    c = lax.broadcasted_iota(I32, (TILE, TILE), 1)
    return r, c, (r // seg) == (c // seg)


def _gla_levels(seg):
    w, out = seg // 2, []
    while w >= 1:
        out.append(w)
        w //= 2
    return out


def _gla_sum_matrices(seg):
    r = jnp.arange(TILE)[:, None]
    c = jnp.arange(TILE)[None, :]
    mats = []
    for w in _gla_levels(seg):
        same = (r // (2 * w)) == (c // (2 * w))
        r_right = (r % (2 * w)) >= w
        c_right = (c % (2 * w)) >= w
        mats.append(same & r_right & c_right & (c <= r))
    for w in _gla_levels(seg):
        same = (r // (2 * w)) == (c // (2 * w))
        r_right = (r % (2 * w)) >= w
        c_right = (c % (2 * w)) >= w
        mats.append(same & (~r_right) & (~c_right) & (c > r))
    same_seg = (r // seg) == (c // seg)
    mats.append(same_seg & (c <= r))
    mats.append(same_seg & (c > r))
    return jnp.concatenate(mats, axis=0).astype(BF16)


def _bdot(a, b):
    return jnp.dot(a.astype(BF16), b.astype(BF16), preferred_element_type=F32)


def _bdot_nt(a, b):
    return lax.dot_general(a.astype(BF16), b.astype(BF16), (((1,), (1,)), ((), ())), preferred_element_type=F32)


def _gla_common(tiles, wg_ref, bg_ref, mats_ref, seg):
    n = len(tiles)
    kw = B_HEADS * B_DK
    las = []
    for _, _, misc_ref in tiles:
        gb = misc_ref[:, MISC_GB:MISC_GB + GATE_RANK]
        z = jnp.dot(gb, wg_ref[...], precision=HIGHEST, preferred_element_type=F32) + bg_ref[...]
        las.append(_log_sigmoid(z) / GATE_TEMP)
    la = jnp.concatenate(las, axis=1)
    la_hi = la.astype(BF16)
    la_lo = (la - la_hi.astype(F32)).astype(BF16)
    mats = mats_ref[...]
    sums = (jnp.dot(mats, la_hi, preferred_element_type=F32) + jnp.dot(mats, la_lo, preferred_element_type=F32))
    levels = _gla_levels(seg)
    nl = len(levels)
    qs = [qb_ref[...] * B_DK ** -0.5 for qb_ref, _, _ in tiles]
    ks = [kb_ref[...] for _, kb_ref, _ in tiles]
    r, c, _ = _seg_masks(seg)
    atts = [[jnp.where(r == c, _bdot_nt(qs[i][:, h * B_DK:(h + 1) * B_DK], ks[i][:, h * B_DK:(h + 1) * B_DK]), 0.0)
             for h in range(B_HEADS)] for i in range(n)]
    for li, w in enumerate(levels):
        pair = ((r // (2 * w)) == (c // (2 * w))) & ((r % (2 * w)) >= w) & ((c % (2 * w)) < w)
        qd = [(qs[i] * jnp.exp(sums[li * TILE:(li + 1) * TILE, i * kw:(i + 1) * kw])).astype(BF16) for i in range(n)]
        kd = [(ks[i] * jnp.exp(sums[(nl + li) * TILE:(nl + li + 1) * TILE, i * kw:(i + 1) * kw])).astype(BF16)
              for i in range(n)]
        for h in range(B_HEADS):
            hs = slice(h * B_DK, (h + 1) * B_DK)
            for i in range(n):
                atts[i][h] = atts[i][h] + jnp.where(pair, _bdot_nt(qd[i][:, hs], kd[i][:, hs]), 0.0)
    out = []
    for i in range(n):
        cs = slice(i * kw, (i + 1) * kw)
        b_cum = sums[2 * nl * TILE:(2 * nl + 1) * TILE, cs]
        rem = sums[(2 * nl + 1) * TILE:(2 * nl + 2) * TILE, cs]
        out.append((qs[i], ks[i], atts[i], b_cum, rem))
    return out


def _gla_finish(o_heads, rb_ref, go_ref, o_ref):
    go = go_ref[...]
    for h in range(B_HEADS):
        vs = slice(h * B_DV, (h + 1) * B_DV)
        rb = rb_ref[:, vs]
        o_ref[:, vs] = (_rms(o_heads[h], go) * (rb * jax.nn.sigmoid(rb))).astype(o_ref.dtype)


def _gla_prompt_kernel(qb_ref, kb_ref, vb_ref, rb_ref, misc_ref, wg_ref, bg_ref, go_ref, mats_ref,
                       o_ref, s_ref, state_ref):
    ci = pl.program_id(0)

    @pl.when(ci == 0)
    def _():
        state_ref[...] = jnp.zeros_like(state_ref)

    nb = qb_ref.shape[0]
    common = _gla_common([(qb_ref.at[b], kb_ref.at[b], misc_ref.at[b]) for b in range(nb)],
                         wg_ref, bg_ref, mats_ref, TILE)
    vals = [vb_ref[b] for b in range(nb)]
    states = [state_ref[b] for b in range(nb)]
    qes = [common[b][0] * jnp.exp(common[b][3]) for b in range(nb)]
    o_heads = [[] for _ in range(nb)]
    for h in range(B_HEADS):
        ks = slice(h * B_DK, (h + 1) * B_DK)
        vs = slice(h * B_DV, (h + 1) * B_DV)
        for b in range(nb):
            o_heads[b].append(_bdot(qes[b][:, ks], states[b][ks, :]) + _bdot(common[b][2][h], vals[b][:, vs]))
    for b in range(nb):
        _gla_finish(o_heads[b], rb_ref.at[b], go_ref, o_ref.at[b])

    ke_ts = [(common[b][1] * jnp.exp(common[b][4])).T for b in range(nb)]
    e_cols = [jnp.broadcast_to(jnp.exp(common[b][3][TILE - 1:TILE, :]), (TILE, B_HEADS * B_DK)).T[:, 0:1]
              for b in range(nb)]
    for b in range(nb):
        upd = jnp.concatenate(
            [_bdot(ke_ts[b][h * B_DK:(h + 1) * B_DK, :], vals[b][:, h * B_DV:(h + 1) * B_DV])
             for h in range(B_HEADS)], axis=0)
        new_state = states[b] * e_cols[b] + upd
        state_ref[b] = new_state
        s_ref[b] = new_state


def gla_prompt(proj, w_gate, b_gate, g_out, n_batch, seq):
    nc = seq // TILE
    kwid = B_HEADS * B_DK
    vwid = B_HEADS * B_DV
    mats = _gla_sum_matrices(TILE)
    proj3 = proj.reshape(n_batch, seq, proj.shape[1])
    o, s = pl.pallas_call(
        _gla_prompt_kernel,
        out_shape=(jax.ShapeDtypeStruct((n_batch, seq, vwid), BF16),
                   jax.ShapeDtypeStruct((n_batch, kwid, B_DV), F32)),
        grid=(nc,),
        in_specs=[pl.BlockSpec((n_batch, TILE, kwid), lambda c: (0, c, QB_OFF // kwid)),
                  pl.BlockSpec((n_batch, TILE, kwid), lambda c: (0, c, KB_OFF // kwid)),
                  pl.BlockSpec((n_batch, TILE, vwid), lambda c: (0, c, VB_OFF // vwid)),
                  pl.BlockSpec((n_batch, TILE, vwid), lambda c: (0, c, RB_OFF // vwid)),
                  pl.BlockSpec((n_batch, TILE, LANES), lambda c: (0, c, MISC_OFF // LANES)),
                  _resident((GATE_RANK, kwid)),
                  _resident((1, kwid)),
                  _resident((1, B_DV)),
                  _resident(mats.shape)],
        out_specs=(pl.BlockSpec((n_batch, TILE, vwid), lambda c: (0, c, 0)),
                   pl.BlockSpec((n_batch, kwid, B_DV), lambda c: (0, 0, 0))),
        scratch_shapes=[pltpu.VMEM((n_batch, kwid, B_DV), F32)],
        compiler_params=_cparams(("arbitrary",)),
        name="gla_prompt",
    )(proj3, proj3, proj3, proj3, proj3, w_gate, b_gate.reshape(1, kwid), g_out.reshape(1, B_DV), mats)
    return o.reshape(n_batch * seq, vwid), s.reshape(n_batch, B_HEADS, B_DK, B_DV)


def _gla_sample_kernel(qb_ref, kb_ref, vb_ref, rb_ref, misc_ref, wg_ref, bg_ref, go_ref, mats_ref, s0_ref,
                       o_ref, s_ref, *, seg):
    nbt = TILE // seg
    (q, k, att, b_cum, rem), = _gla_common([(qb_ref, kb_ref, misc_ref)], wg_ref, bg_ref, mats_ref, seg)
    v = vb_ref[...]
    qe = q * jnp.exp(b_cum)
    ke = k * jnp.exp(rem)
    r1 = lax.broadcasted_iota(I32, (TILE, 1), 0)
    e_last = jnp.where(r1 % seg == seg - 1, jnp.exp(b_cum), 0.0)
    wide = nbt * B_DK
    mq = (lax.broadcasted_iota(I32, (TILE, wide), 0) // seg) == (lax.broadcasted_iota(I32, (TILE, wide), 1) // B_DK)
    mk = (lax.broadcasted_iota(I32, (wide, TILE), 0) // B_DK) == (lax.broadcasted_iota(I32, (wide, TILE), 1) // seg)
    o_heads = []
    for h in range(B_HEADS):
        ks = slice(h * B_DK, (h + 1) * B_DK)
        vs = slice(h * B_DV, (h + 1) * B_DV)
        state = s0_ref[:, h].reshape(wide, B_DV)
        q_bd = jnp.where(mq, jnp.concatenate([qe[:, ks]] * nbt, axis=1), 0.0)
        o_heads.append(_bdot(q_bd, state) + _bdot(att[h], v[:, vs]))
        pair_t = jnp.concatenate([ke[:, ks], e_last[:, ks]], axis=1).T
        k_bd = jnp.where(mk, jnp.concatenate([pair_t[:B_DK]] * nbt, axis=0), 0.0)
        e_bd = jnp.where(mk, jnp.concatenate([pair_t[B_DK:]] * nbt, axis=0), 0.0)
        e_col = jnp.sum(e_bd, axis=-1, keepdims=True)
        new_state = state * e_col + _bdot(k_bd, v[:, vs])
        s_ref[:, h] = new_state.reshape(nbt, B_DK, B_DV)
    _gla_finish(o_heads, rb_ref, go_ref, o_ref)


def gla_sample(proj, w_gate, b_gate, g_out, s0, layer, n_tok):
    T = proj.shape[0]
    nbt = TILE // n_tok
    kwid = B_HEADS * B_DK
    vwid = B_HEADS * B_DV
    mats = _gla_sum_matrices(n_tok)
    return pl.pallas_call(
        functools.partial(_gla_sample_kernel, seg=n_tok),
        out_shape=(jax.ShapeDtypeStruct((T, vwid), BF16),
                   jax.ShapeDtypeStruct(s0.shape[1:], F32)),
        grid=(T // TILE,),
        in_specs=[pl.BlockSpec((TILE, kwid), lambda i: (i, QB_OFF // kwid)),
                  pl.BlockSpec((TILE, kwid), lambda i: (i, KB_OFF // kwid)),
                  pl.BlockSpec((TILE, vwid), lambda i: (i, VB_OFF // vwid)),
                  pl.BlockSpec((TILE, vwid), lambda i: (i, RB_OFF // vwid)),
                  pl.BlockSpec((TILE, LANES), lambda i: (i, MISC_OFF // LANES)),
                  _resident((GATE_RANK, kwid)),
                  _resident((1, kwid)),
                  _resident((1, B_DV)),
                  _resident(mats.shape),
                  pl.BlockSpec((None, nbt, B_HEADS, B_DK, B_DV), lambda i: (layer, i, 0, 0, 0))],
        out_specs=(pl.BlockSpec((TILE, vwid), lambda i: (i, 0)),
                   pl.BlockSpec((nbt, B_HEADS, B_DK, B_DV), lambda i: (i, 0, 0, 0))),
        compiler_params=_cparams(("parallel",)),
        name="gla_sample",
    )(proj, proj, proj, proj, proj, w_gate, b_gate.reshape(1, kwid), g_out.reshape(1, B_DV), mats, s0)


def _gelu(x):
    return jax.nn.gelu(x)


def _gmlp_kernel(uc_ref, vc_ref, gv_ref, ws_ref, bcol_ref, o_ref, vn_ref, *, seg):
    r, c, same_seg = _seg_masks(seg)
    keep = same_seg & (c <= r)
    u = _gelu(uc_ref[...])
    vg = _gelu(vc_ref[...])
    for g in range(C_GROUPS):
        gs = slice(g * C_GROUP_DIM, (g + 1) * C_GROUP_DIM)
        vn = _rms(vg[:, gs], gv_ref[:, gs])
        vn_ref[:, gs] = vn
        w = jnp.where(keep, ws_ref[g], 0.0).astype(BF16)
        s = jnp.dot(w, vn.astype(BF16), preferred_element_type=F32) + bcol_ref[:, g:g + 1]
        o_ref[:, gs] = (u[:, gs] * s).astype(o_ref.dtype)


def gmlp(proj, g_v, w_tiles, b_cols, seg):
    T = proj.shape[0]
    cw = C_GROUPS * C_GROUP_DIM
    return pl.pallas_call(
        functools.partial(_gmlp_kernel, seg=seg),
        out_shape=(jax.ShapeDtypeStruct((T, cw), BF16),
                   jax.ShapeDtypeStruct((T, cw), F32)),
        grid=(T // TILE,),
        in_specs=[pl.BlockSpec((TILE, cw), lambda i: (i, UC_OFF // cw)),
                  pl.BlockSpec((TILE, cw), lambda i: (i, VC_OFF // cw)),
                  _resident((1, cw)),
                  _resident((C_GROUPS, TILE, TILE)),
                  _resident((TILE, C_GROUPS))],
        out_specs=(pl.BlockSpec((TILE, cw), lambda i: (i, 0)),
                   pl.BlockSpec((TILE, cw), lambda i: (i, 0))),
        compiler_params=_cparams(("parallel",)),
        name="gmlp",
    )(proj, proj, g_v.reshape(1, cw), w_tiles, b_cols)


def _ffn_kernel(h_ref, oa_ref, ob_ref, oc_ref, wo_ref, g_ref, wg_ref, wu_ref, wd_ref, o_ref, n_ref):
    j = pl.program_id(1)

    @pl.when(j == 0)
    def _():
        aw = oa_ref.shape[1]
        bw = ob_ref.shape[1]
        h = h_ref[...] + jnp.dot(oa_ref[...], wo_ref[0:aw, :], preferred_element_type=F32)
        h = h + jnp.dot(ob_ref[...], wo_ref[aw:aw + bw, :], preferred_element_type=F32)
        h = h + jnp.dot(oc_ref[...], wo_ref[aw + bw:, :], preferred_element_type=F32)
        n_ref[...] = _rms(h, g_ref[...]).astype(BF16)
        o_ref[...] = h

    n = n_ref[...]
    a = jnp.dot(n, wg_ref[...], preferred_element_type=F32)
    u = jnp.dot(n, wu_ref[...], preferred_element_type=F32)
    act = (a * jax.nn.sigmoid(a) * u).astype(BF16)
    o_ref[...] += jnp.dot(act, wd_ref[...], preferred_element_type=F32)


def out_proj_ffn(h, o_a, o_b, o_c, w_out, g, w_gate, w_up, w_down, layer, tm, tf):
    T, D = h.shape
    tm = min(tm, T)
    FF = w_gate.shape[2]
    return pl.pallas_call(
        _ffn_kernel,
        out_shape=jax.ShapeDtypeStruct((T, D), F32),
        grid=(T // tm, FF // tf),
        in_specs=[pl.BlockSpec((tm, D), lambda i, j: (i, 0)),
                  pl.BlockSpec((tm, o_a.shape[1]), lambda i, j: (i, 0)),
                  pl.BlockSpec((tm, o_b.shape[1]), lambda i, j: (i, 0)),
                  pl.BlockSpec((tm, o_c.shape[1]), lambda i, j: (i, 0)),
                  _layer_resident(w_out.shape[1:], layer),
                  _resident((1, D)),
                  pl.BlockSpec((None, D, tf), lambda i, j: (layer, 0, j)),
                  pl.BlockSpec((None, D, tf), lambda i, j: (layer, 0, j)),
                  pl.BlockSpec((None, tf, D), lambda i, j: (layer, j, 0))],
        out_specs=pl.BlockSpec((tm, D), lambda i, j: (i, 0)),
        scratch_shapes=[pltpu.VMEM((tm, D), BF16)],
        compiler_params=_cparams(("parallel", "arbitrary")),
        name="out_proj_ffn",
    )(h, o_a, o_b, o_c, w_out, g.reshape(1, D), w_gate, w_up, w_down)


def _ple_kernel(h_ref, p_ref, g_ref, wgate_ref, wproj_ref, o_ref):
    h = h_ref[...]
    n = _rms(h, g_ref[...]).astype(BF16)
    gate = jax.nn.sigmoid(jnp.dot(n, wgate_ref[...], preferred_element_type=F32))
    emb = jnp.dot(p_ref[...].astype(BF16), wproj_ref[...], preferred_element_type=F32)
    o_ref[...] = h + gate * emb


def ple(h, p, g, w_gate, w_proj, layer, tm):
    T, D = h.shape
    tm = min(tm, T)
    P = p.shape[2]
    return pl.pallas_call(
        _ple_kernel,
        out_shape=jax.ShapeDtypeStruct((T, D), F32),
        grid=(T // tm,),
        in_specs=[pl.BlockSpec((tm, D), lambda i: (i, 0)),
                  pl.BlockSpec((None, tm, P), lambda i: (layer, i, 0)),
                  _resident((1, D)),
                  _layer_resident(w_gate.shape[1:], layer),
                  _layer_resident(w_proj.shape[1:], layer)],
        out_specs=pl.BlockSpec((tm, D), lambda i: (i, 0)),
        compiler_params=_cparams(("parallel",)),
        name="ple",
    )(h, p, g.reshape(1, D), w_gate, w_proj)


_W_IN_SEGMENTS = (("q", 1024), ("k", 256), ("v", 256), ("qi", 1024), ("ki", 64), ("wi", 16), ("qb", 256),
                  ("kb", 256), ("vb", 512), ("gb", 16), ("rb", 512), ("uc", 512), ("vc", 512))
_W_IN_PACKED_ORDER = ("q", "qi", "vb", "rb", "uc", "vc", "k", "v", "qb", "kb", "ki", "wi", "gb")


def _pack_kernel(wt_ref, o_ref):
    src, start = {}, 0
    for name, size in _W_IN_SEGMENTS:
        src[name] = (start, size)
        start += size
    dst = 0
    small = []
    for name in _W_IN_PACKED_ORDER:
        s0, size = src[name]
        if size < LANES:
            small.append(wt_ref[s0:s0 + size, :])
            continue
        o_ref[:, dst:dst + size] = wt_ref[s0:s0 + size, :].T.astype(BF16)
        dst += size
    used = sum(x.shape[0] for x in small)
    small.append(jnp.zeros((LANES - used, wt_ref.shape[1]), F32))
    o_ref[:, dst:dst + LANES] = jnp.concatenate(small, axis=0).T.astype(BF16)


def _pack_w_in(w, tr=256):
    depth, D, N = w.shape
    return pl.pallas_call(
        _pack_kernel,
        out_shape=jax.ShapeDtypeStruct((depth, D, PROJ_PACKED), BF16),
        grid=(depth, D // tr),
        in_specs=[pl.BlockSpec((None, N, tr), lambda l, i: (l, 0, i))],
        out_specs=pl.BlockSpec((None, tr, PROJ_PACKED), lambda l, i: (l, i, 0)),
        compiler_params=_cparams(("parallel", "parallel")),
        name="pack_w_in",
    )(jnp.swapaxes(w, 1, 2))


def _mixer_tail(h, o_a, o_b, o_c, p_all, wts, layer, tm, tm_ffn):
    h = out_proj_ffn(h, o_a, o_b, o_c, wts["w_out"], wts["g_ffn"][layer], wts["w_ffn_gate"], wts["w_ffn_up"],
                     wts["w_ffn_down"], layer, tm_ffn, 512)
    return ple(h, p_all, wts["g_ple"][layer], wts["w_ple_gate"], wts["w_ple_proj"], layer, tm)


def kernel(x_prompt, x_sample, cache_k, cache_v, cache_idx_k, state_gla, page_table, p_prompt, p_sample,
           g_mix, w_in, q_norm_g, k_norm_g, rel_bias, w_gate_b, b_gate_b, g_out_b, g_v_c, w_spatial,
           b_spatial, w_out, g_ffn, w_ffn_gate, w_ffn_up, w_ffn_down, g_ple, w_ple_gate, w_ple_proj):
    n_batch, seq, d_model = x_prompt.shape
    dec_batch, dec_seq, _ = x_sample.shape
    depth = w_in.shape[0]
    n_pages = page_table.shape[1]
    past = n_pages * PAGE_SIZE
    kw = A_KV_HEADS * HEAD_DIM
    tp, ts = n_batch * seq, dec_batch * dec_seq
    rows_pad = SUBLANES
    l_sample = past + PAGE_SIZE
    k_sel_s = min(TOPK_MAX, (past + dec_seq) // 4)

    bias_p = bias_table_prompt(rel_bias)
    bias_s = bias_table_sample(rel_bias, past, dec_seq, A_KV_HEADS * l_sample)
    cache_ik_t = jnp.swapaxes(cache_idx_k, 2, 3)
    cache_k2 = cache_k.reshape(depth, cache_k.shape[1], PAGE_SIZE * A_KV_HEADS, HEAD_DIM)
    cache_v2 = cache_v.reshape(depth, cache_v.shape[1], PAGE_SIZE * A_KV_HEADS, HEAD_DIM)

    hp = x_prompt.reshape(tp, d_model)
    hs = x_sample.reshape(ts, d_model)
    outs = {k: [] for k in ("kp", "vp", "ikp", "sp", "ks", "vs", "iks", "ss", "cs")}
    per_s = rows_pad // dec_seq
    place_t = (jnp.arange(rows_pad)[:, None, None]
               == jnp.arange(per_s)[None, :, None] * dec_seq + jnp.arange(dec_seq)[None, None, :]
               ).astype(F32)
    wts = dict(w_out=w_out.astype(BF16), g_ffn=g_ffn, w_ffn_gate=w_ffn_gate.astype(BF16),
               w_ffn_up=w_ffn_up.astype(BF16), w_ffn_down=w_ffn_down.astype(BF16),
               g_ple=g_ple, w_ple_gate=w_ple_gate.astype(BF16), w_ple_proj=w_ple_proj.astype(BF16))
    w_packed = _pack_w_in(w_in)
    pp_all = p_prompt.reshape(depth, tp, -1)
    ps_all = p_sample.reshape(depth, ts, -1)
    for i in range(depth):
        b_cols_p = b_spatial[i].T
        reps = TILE // dec_seq
        w_tiles_s = jnp.tile(w_spatial[i][:, :dec_seq, :dec_seq], (1, reps, reps))
        b_cols_s = jnp.tile(b_spatial[i][:, :dec_seq].T, (reps, 1))

        proj = in_projection(hp, g_mix[i], w_packed, i, 256)
        kn, vv, ik, vt = kv_post(proj, k_norm_g[i], 512, True)
        o_a = dsa_prompt(proj, kn, vt, bias_p, q_norm_g[i], n_batch, seq)
        o_b, s_p = gla_prompt(proj, w_gate_b[i], b_gate_b[i], g_out_b[i], n_batch, seq)
        o_c, _ = gmlp(proj, g_v_c[i], w_spatial[i], b_cols_p, TILE)
        hp = _mixer_tail(hp, o_a, o_b, o_c, pp_all, wts, i, 256, 512)
        outs["kp"].append(kn.reshape(n_batch, seq, A_KV_HEADS, HEAD_DIM))
        outs["vp"].append(vv.reshape(n_batch, seq, A_KV_HEADS, HEAD_DIM))
        outs["ikp"].append(ik.reshape(n_batch, seq, IDX_DIM))
        outs["sp"].append(s_p)

        proj = in_projection(hs, g_mix[i], w_packed, i, 256)
        kn, vv, ik = kv_post(proj, k_norm_g[i], 512, False)
        qi_rows =proj[:, QI_OFF:QI_OFF + IDX_HEADS * IDX_DIM].reshape(dec_batch, dec_seq * IDX_HEADS, IDX_DIM)
        wi = proj[:, MISC_OFF + MISC_WI:MISC_OFF + MISC_WI + IDX_HEADS].reshape(dec_batch, dec_seq, IDX_HEADS)
        wi = wi * (IDX_HEADS ** -0.5 * IDX_DIM ** -0.5)
        wmat = (place_t[None, :, :, :, None] * wi.reshape(dec_batch // per_s, 1, per_s, dec_seq, IDX_HEADS)
                ).reshape(dec_batch // per_s, rows_pad, per_s * dec_seq * IDX_HEADS)
        ki_new_t = jnp.pad(jnp.swapaxes(ik.reshape(dec_batch, dec_seq, IDX_DIM), 1, 2),
                           ((0, 0), (0, 0), (0, PAGE_SIZE - dec_seq)))
        mask = dsa_sample_select(cache_ik_t, i, page_table, qi_rows, wmat, ki_new_t, dec_seq, k_sel_s)
        q_rows = proj[:, Q_OFF:Q_OFF + A_HEADS * HEAD_DIM].reshape(dec_batch, dec_seq * A_HEADS, HEAD_DIM)
        o_a = dsa_sample_attend(cache_k2, cache_v2, i, page_table, q_rows,
                                kn.reshape(dec_batch, dec_seq * A_KV_HEADS, HEAD_DIM),
                                vv.reshape(dec_batch, dec_seq * A_KV_HEADS, HEAD_DIM), mask, bias_s, q_norm_g[i])
        o_a = o_a.reshape(ts, A_HEADS * HEAD_DIM)
        o_b, s_s = gla_sample(proj, w_gate_b[i], b_gate_b[i], g_out_b[i], state_gla, i, dec_seq)
        o_c, vn = gmlp(proj, g_v_c[i], w_tiles_s, b_cols_s, dec_seq)
        hs = _mixer_tail(hs, o_a, o_b, o_c, ps_all, wts, i, 256, 512)
        outs["ks"].append(kn.reshape(dec_batch, dec_seq, A_KV_HEADS, HEAD_DIM))
        outs["vs"].append(vv.reshape(dec_batch, dec_seq, A_KV_HEADS, HEAD_DIM))
        outs["iks"].append(ik.reshape(dec_batch, dec_seq, IDX_DIM))
        outs["ss"].append(s_s)
        outs["cs"].append(vn.reshape(dec_batch, dec_seq, -1))

    st = {k: jnp.stack(v) for k, v in outs.items()}
    return (hp.reshape(n_batch, seq, d_model), hs.reshape(dec_batch, dec_seq, d_model),
            st["kp"], st["vp"], st["ikp"], st["sp"], st["ks"], st["vs"], st["iks"], st["ss"], st["cs"])
```

```python
import functools
import math

import jax
import jax.numpy as jnp
from jax import lax
from jax.experimental import pallas as pl
from jax.experimental.pallas import tpu as pltpu

F32 = jnp.float32
BF16 = jnp.bfloat16
I32 = jnp.int32
HIGHEST = lax.Precision.HIGHEST

LANES = 128
SUBLANES = 8
VMEM_LIMIT = 56 * 1024 * 1024

HEAD_DIM = 128
A_HEADS = 8
A_KV_HEADS = 2
IDX_HEADS = 16
IDX_DIM = 64
TOPK_MAX = 256
NUM_BUCKETS = 32
MAX_DISTANCE = 128
B_HEADS = 4
B_DK = 64
B_DV = 128
GATE_RANK = 16
GATE_TEMP = 16.0
C_GROUPS = 4
C_GROUP_DIM = 128
PAGE_SIZE = 128
EPS = 1e-6
NEG_BIG = -1e30
INT_MIN = -(2 ** 31)
NEG_INF_KEY = -2139095041

TILE = 128
QBLK = 256

Q_OFF, QI_OFF, VB_OFF, RB_OFF, UC_OFF, VC_OFF = 0, 1024, 2048, 2560, 3072, 3584
K_OFF, V_OFF, QB_OFF, KB_OFF, MISC_OFF = 4096, 4352, 4608, 4864, 5120
PROJ_PACKED = 5248
MISC_KI, MISC_WI, MISC_GB = 0, 64, 80


def _cparams(sem):
    return pltpu.CompilerParams(dimension_semantics=sem, vmem_limit_bytes=VMEM_LIMIT)


def _rms(x, g):
    return x * lax.rsqrt(jnp.mean(x * x, axis=-1, keepdims=True) + EPS) * g


def _resident(shape):
    nd = len(shape)
    return pl.BlockSpec(shape, lambda *_: (0,) * nd, pipeline_mode=pl.Buffered(1))


def _layer_resident(shape, layer):
    nd = len(shape)
    return pl.BlockSpec((None,) + tuple(shape), lambda *_: (layer,) + (0,) * nd, pipeline_mode=pl.Buffered(1))


def _proj_kernel(x_ref, g_ref, w_ref, *refs, n_cast):
    cast_in, o_ref, cast_out = refs[:n_cast], refs[n_cast], refs[n_cast + 1:]
    n = _rms(x_ref[...], g_ref[...]).astype(BF16)
    ncol = o_ref.shape[1]
    step = 512
    for c0 in range(0, ncol, step):
        c1 = min(c0 + step, ncol)
        o_ref[:, c0:c1] = jnp.dot(n, w_ref[:, c0:c1], preferred_element_type=F32)
    for src, dst in zip(cast_in, cast_out):
        dst[...] = src[...].astype(dst.dtype)


def in_projection(h, g, w_packed, layer, tm, cast=()):
    T, D = h.shape
    tm = min(tm, T)
    N = w_packed.shape[2]
    steps = T // tm
    cast_specs_in = [pl.BlockSpec((None, w.shape[1] // steps, w.shape[2]), lambda i: (layer, i, 0)) for w in cast]
    cast_specs_out = [pl.BlockSpec((w.shape[1] // steps, w.shape[2]), lambda i: (i, 0)) for w in cast]
    outs = pl.pallas_call(
        functools.partial(_proj_kernel, n_cast=len(cast)),
        out_shape=(jax.ShapeDtypeStruct((T, N), F32),) + tuple(
            jax.ShapeDtypeStruct(w.shape[1:], BF16) for w in cast),
        grid=(steps,),
        in_specs=[pl.BlockSpec((tm, D), lambda i: (i, 0)),
                  _resident((1, D)),
                  _layer_resident((D, N), layer)] + cast_specs_in,
        out_specs=(pl.BlockSpec((tm, N), lambda i: (i, 0)),) + tuple(cast_specs_out),
        compiler_params=_cparams(("parallel",)),
        name="in_projection",
    )(h, g.reshape(1, D), w_packed, *cast)
    return outs[0], outs[1:]


def _kv_kernel(k_ref, v_ref, m_ref, g_ref, ko_ref, vo_ref, io_ref, vt_ref=None):
    g = g_ref[...]
    k = k_ref[...]
    for hh in range(A_KV_HEADS):
        sl = slice(hh * HEAD_DIM, (hh + 1) * HEAD_DIM)
        ko_ref[:, sl] = _rms(k[:, sl], g)
    v = v_ref[...]
    vo_ref[...] = v
    io_ref[...] = m_ref[:, MISC_KI:MISC_KI + IDX_DIM]
    if vt_ref is not None:
        for blk in range(vt_ref.shape[0]):
            vt_ref[blk] = v[blk * QBLK:(blk + 1) * QBLK, :].T.astype(vt_ref.dtype)


def kv_post(proj, k_norm_g, tm, with_vt):
    T = proj.shape[0]
    tm = min(tm, T)
    kw = A_KV_HEADS * HEAD_DIM
    out_shape = [jax.ShapeDtypeStruct((T, kw), F32),
                 jax.ShapeDtypeStruct((T, kw), F32),
                 jax.ShapeDtypeStruct((T, IDX_DIM), F32)]
    out_specs = [pl.BlockSpec((tm, kw), lambda i: (i, 0)),
                 pl.BlockSpec((tm, kw), lambda i: (i, 0)),
                 pl.BlockSpec((tm, IDX_DIM), lambda i: (i, 0))]
    if with_vt:
        out_shape.append(jax.ShapeDtypeStruct((T // QBLK, kw, QBLK), BF16))
        out_specs.append(pl.BlockSpec((tm // QBLK, kw, QBLK), lambda i: (i, 0, 0)))
    return pl.pallas_call(
        _kv_kernel,
        out_shape=tuple(out_shape),
        grid=(T // tm,),
        in_specs=[pl.BlockSpec((tm, kw), lambda i: (i, K_OFF // kw)),
                  pl.BlockSpec((tm, kw), lambda i: (i, V_OFF // kw)),
                  pl.BlockSpec((tm, LANES), lambda i: (i, MISC_OFF // LANES)),
                  _resident((1, HEAD_DIM))],
        out_specs=tuple(out_specs),
        compiler_params=_cparams(("parallel",)),
        name="kv_post",
    )(proj, proj, proj, k_norm_g.reshape(1, HEAD_DIM))


def _bucket(dist):
    n = jnp.maximum(dist, 0)
    max_exact = NUM_BUCKETS // 2
    large = max_exact + (jnp.log(jnp.maximum(n, 1).astype(F32) / max_exact)
                         / math.log(MAX_DISTANCE / max_exact)
                         * (NUM_BUCKETS - max_exact)).astype(I32)
    large = jnp.minimum(large, NUM_BUCKETS - 1)
    return jnp.where(n < max_exact, n, large)


def _bias_prompt_kernel(rb_ref, o_ref):
    c = lax.broadcasted_iota(I32, (TILE, TILE), 0)
    t = lax.broadcasted_iota(I32, (TILE, TILE), 1)
    for z in range(3):
        bucket = _bucket(t - c + (2 - z) * TILE)
        for h in range(A_HEADS):
            acc = jnp.zeros((TILE, TILE), F32)
            for b in range(NUM_BUCKETS):
                acc = jnp.where(bucket == b, rb_ref[b, h], acc)
            o_ref[h, z] = acc


def bias_table_prompt(rel_bias):
    return pl.pallas_call(
        _bias_prompt_kernel,
        out_shape=jax.ShapeDtypeStruct((A_HEADS, 3, TILE, TILE), F32),
        in_specs=[pl.BlockSpec(memory_space=pltpu.SMEM)],
        out_specs=pl.BlockSpec(memory_space=pltpu.VMEM),
        name="bias_table_prompt",
    )(rel_bias)


def _bias_sample_kernel(rbrows_ref, o_ref, *, past, n_tok):
    rows, L = o_ref.shape
    r = lax.broadcasted_iota(I32, (rows, L), 0)
    s = lax.broadcasted_iota(I32, (rows, L), 1) // A_KV_HEADS
    bucket = _bucket(past + r // A_HEADS - s)
    rbrows = rbrows_ref[...]
    acc = jnp.zeros((rows, L), F32)
    for b in range(NUM_BUCKETS):
        acc = jnp.where(bucket == b, rbrows[:, b:b + 1], acc)
    o_ref[...] = acc


def bias_table_sample(rel_bias, past, n_tok, L):
    rows = n_tok * A_HEADS
    rbrows = jnp.tile(rel_bias.T, (n_tok, 1))
    return pl.pallas_call(
        functools.partial(_bias_sample_kernel, past=past, n_tok=n_tok),
        out_shape=jax.ShapeDtypeStruct((rows, L), F32),
        name="bias_table_sample",
    )(rbrows)


def _sortable_key(x):
    b = lax.bitcast_convert_type(x, I32)
    return b ^ ((b >> 31) & 0x7FFFFFFF)


def _topk_member(skey_ref, k_sel):
    R, L = skey_ref.shape

    def body(it, ans):
        bit = 31 - it
        cand = ans | lax.shift_left(jnp.int32(1), bit)
        cand_s = cand ^ INT_MIN
        cnt = jnp.sum(jnp.where(skey_ref[...] >= cand_s, 1.0, 0.0), axis=-1, keepdims=True)
        return jnp.where(cnt >= k_sel, cand, ans)

    ans = lax.fori_loop(0, 32, body, jnp.zeros((R, 1), I32))
    tau = ans ^ INT_MIN
    skey = skey_ref[...]
    gt = skey > tau
    eq = skey == tau
    n_gt = jnp.sum(jnp.where(gt, 1.0, 0.0), axis=-1, keepdims=True)
    room = k_sel - n_gt
    r_i = lax.broadcasted_iota(I32, (LANES, LANES), 0)
    c_i = lax.broadcasted_iota(I32, (LANES, LANES), 1)
    upper = jnp.where(r_i <= c_i, 1.0, 0.0).astype(BF16)
    off = jnp.zeros((R, 1), F32)
    parts = []
    for j in range(L // LANES):
        sl = slice(j * LANES, (j + 1) * LANES)
        eq_j = eq[:, sl]
        run = jnp.dot(jnp.where(eq_j, 1.0, 0.0).astype(BF16), upper, preferred_element_type=F32) + off
        parts.append(gt[:, sl] | (eq_j & (run <= room)))
        off = run[:, LANES - 1:LANES]
    return jnp.concatenate(parts, axis=1)


def _fold8(x, op):
    return op(x.reshape(x.shape[0] // SUBLANES, SUBLANES, x.shape[1]), axis=0)


def _dsa_prompt_kernel(q_ref, qi_ref, misc_ref, kn_ref, vt_ref, bias_ref, qg_ref, o_ref,
                       qst_ref, skey_ref, madd_ref, lg_ref, acc_ref, *, k_sel):
    i = pl.program_id(1)
    nkb = i + 1
    sub = QBLK // TILE
    rep = A_HEADS // A_KV_HEADS
    row0 = pl.multiple_of(i * QBLK, QBLK)
    s_iota = lax.broadcasted_iota(I32, (QBLK, QBLK), 0)
    t_iota = lax.broadcasted_iota(I32, (QBLK, QBLK), 1)

    def admissible(j):
        return (j * QBLK + s_iota) <= (row0 + t_iota)

    wi_t = misc_ref[pl.ds(row0, QBLK), :].T[MISC_WI:MISC_WI + IDX_HEADS, :]
    wi_t = wi_t * (IDX_HEADS ** -0.5 * IDX_DIM ** -0.5)
    for h in range(IDX_HEADS):
        qst_ref[h * QBLK:(h + 1) * QBLK, :] = qi_ref[:, h * IDX_DIM:(h + 1) * IDX_DIM].astype(BF16)

    def score_body(j, carry):
        k0 = pl.multiple_of(j * QBLK, QBLK)
        kj = misc_ref[pl.ds(k0, QBLK), MISC_KI:MISC_KI + IDX_DIM].astype(BF16)
        s = lax.dot_general(kj, qst_ref[...], (((1,), (1,)), ((), ())), preferred_element_type=F32)
        score = jnp.zeros((QBLK, QBLK), F32)
        for h in range(IDX_HEADS):
            score = score + jnp.maximum(s[:, h * QBLK:(h + 1) * QBLK], 0.0) * wi_t[h:h + 1, :]
        skey_ref[j] = _sortable_key(jnp.where(admissible(j), score, -jnp.inf))
        return carry

    lax.fori_loop(0, nkb, score_body, 0)

    def count(pred_fn):
        def body(j, acc):
            return acc + _fold8(jnp.where(pred_fn(skey_ref[j]), 1.0, 0.0), jnp.sum)
        acc = lax.fori_loop(0, nkb, body, jnp.zeros((SUBLANES, QBLK), F32))
        return jnp.sum(acc, axis=0, keepdims=True)

    def bit_body(it, ans):
        cand = ans | lax.shift_left(jnp.int32(1), 31 - it)
        cand_s = cand ^ INT_MIN
        cnt = count(lambda key: key >= cand_s)
        return jnp.where(cnt >= k_sel, cand, ans)

    ans = lax.fori_loop(0, 32, bit_body, jnp.zeros((1, QBLK), I32))
    tau = ans ^ INT_MIN
    n_ge = count(lambda key: key >= tau)
    excess = jnp.max(jnp.where((n_ge > k_sel) & (tau != NEG_INF_KEY), 1.0, 0.0))

    @pl.when(excess == 0.0)
    def _():
        def mask_body(j, carry):
            madd_ref[j] = jnp.where((skey_ref[j] >= tau) & admissible(j), 0.0, NEG_BIG)
            return carry

        lax.fori_loop(0, nkb, mask_body, 0)

    @pl.when(excess > 0.0)
    def _():
        room = k_sel - count(lambda key: key > tau)
        lower = jnp.where(t_iota <= s_iota, 1.0, 0.0).astype(BF16)

        def mask_body(j, off):
            key = skey_ref[j]
            eq = key == tau
            run = jnp.dot(lower, jnp.where(eq, 1.0, 0.0).astype(BF16), preferred_element_type=F32) + off
            sel = ((key > tau) | (eq & (run <= room))) & admissible(j)
            madd_ref[j] = jnp.where(sel, 0.0, NEG_BIG)
            return run[QBLK - 1:QBLK, :]

        lax.fori_loop(0, nkb, mask_body, jnp.zeros((1, QBLK), F32))

    qg = qg_ref[...]
    wide = rep * QBLK
    for g in range(A_KV_HEADS):
        gs = slice(g * HEAD_DIM, (g + 1) * HEAD_DIM)
        heads = list(range(g * rep, (g + 1) * rep))
        q_stack = jnp.concatenate(
            [(_rms(q_ref[:, h * HEAD_DIM:(h + 1) * HEAD_DIM], qg) * HEAD_DIM ** -0.5).astype(BF16) for h in heads],
            axis=0)

        def logit_body(j, mx):
            k0 = pl.multiple_of(j * QBLK, QBLK)
            kj = kn_ref[pl.ds(k0, QBLK), gs].astype(BF16)
            lg = lax.dot_general(kj, q_stack, (((1,), (1,)), ((), ())), preferred_element_type=F32)
            madd = madd_ref[j]
            parts = []
            for r, h in enumerate(heads):
                quads = []
                for c in range(sub):
                    quads.append(jnp.concatenate(
                        [bias_ref[h, jnp.clip(2 - ((i - j) * sub + u - c), 0, 2)] for u in range(sub)], axis=1))
                parts.append(lg[:, r * QBLK:(r + 1) * QBLK] + jnp.concatenate(quads, axis=0) + madd)
            lg = jnp.concatenate(parts, axis=1)
            lg_ref[j] = lg
            return jnp.maximum(mx, _fold8(lg, jnp.max))

        mx = lax.fori_loop(0, nkb, logit_body, jnp.full((SUBLANES, wide), NEG_BIG, F32))
        m = jnp.max(mx, axis=0, keepdims=True)
        acc_ref[...] = jnp.zeros(acc_ref.shape, F32)

        def pv_body(j, sm):
            p = jnp.exp(lg_ref[j] - m)
            acc_ref[...] += jnp.dot(vt_ref[j, gs, :], p.astype(BF16), preferred_element_type=F32)
            return sm + _fold8(p, jnp.sum)

        sm = lax.fori_loop(0, nkb, pv_body, jnp.zeros((SUBLANES, wide), F32))
        den = jnp.sum(sm, axis=0, keepdims=True)
        o = (acc_ref[...] / den).T
        for r, h in enumerate(heads):
            o_ref[:, h * HEAD_DIM:(h + 1) * HEAD_DIM] = o[r * QBLK:(r + 1) * QBLK, :].astype(o_ref.dtype)


def dsa_prompt(proj, kn, vt, bias_tab, q_norm_g, n_batch, seq):
    T = proj.shape[0]
    nb = seq // QBLK
    k_sel = min(TOPK_MAX, seq // 4)
    aw = A_HEADS * HEAD_DIM
    iw = IDX_HEADS * IDX_DIM
    kw = A_KV_HEADS * HEAD_DIM
    rep = A_HEADS // A_KV_HEADS
    return pl.pallas_call(
        functools.partial(_dsa_prompt_kernel, k_sel=k_sel),
        out_shape=jax.ShapeDtypeStruct((T, aw), BF16),
        grid=(n_batch, nb),
        in_specs=[pl.BlockSpec((QBLK, aw), lambda b, i: (b * nb + i, Q_OFF // aw)),
                  pl.BlockSpec((QBLK, iw), lambda b, i: (b * nb + i, QI_OFF // iw)),
                  pl.BlockSpec((seq, LANES), lambda b, i: (b, MISC_OFF // LANES)),
                  pl.BlockSpec((seq, kw), lambda b, i: (b, 0)),
                  pl.BlockSpec((nb, kw, QBLK), lambda b, i: (b, 0, 0)),
                  _resident((A_HEADS, 3, TILE, TILE)),
                  _resident((1, HEAD_DIM))],
        out_specs=pl.BlockSpec((QBLK, aw), lambda b, i: (b * nb + i, 0)),
        scratch_shapes=[pltpu.VMEM((IDX_HEADS * QBLK, IDX_DIM), BF16),
                        pltpu.VMEM((nb, QBLK, QBLK), I32),
                        pltpu.VMEM((nb, QBLK, QBLK), F32),
                        pltpu.VMEM((nb, QBLK, rep * QBLK), F32),
                        pltpu.VMEM((HEAD_DIM, rep * QBLK), F32)],
        compiler_params=_cparams(("parallel", "arbitrary")),
        name="dsa_prompt",
    )(proj, proj, proj, kn, vt, bias_tab, q_norm_g.reshape(1, HEAD_DIM))


def _dsa_sample_select_kernel(pt_ref, *refs, n_pages, n_tok, k_sel, rows_pad):
    del pt_ref
    per = rows_pad // n_tok
    page_refs = refs[:per * n_pages]
    qi_ref, wm_ref, kin_ref, mask_ref, sc_ref, skey_ref = refs[per * n_pages:]
    b = pl.program_id(0)
    nb = pl.num_programs(0)
    L = sc_ref.shape[1]
    past = n_pages * PAGE_SIZE

    relu_s = []
    for e in range(per):
        kt_all = jnp.concatenate([r[...].astype(BF16) for r in page_refs[e * n_pages:(e + 1) * n_pages]]
                                 + [kin_ref[e].astype(BF16)], axis=1)
        relu_s.append(jnp.maximum(jnp.dot(qi_ref[e].astype(BF16), kt_all, preferred_element_type=F32), 0.0))
    score = jnp.dot(wm_ref[0], jnp.concatenate(relu_s, axis=0), precision=HIGHEST,
                    preferred_element_type=F32)
    r0 = pl.multiple_of(b * rows_pad, rows_pad)
    sc_ref[pl.ds(r0, rows_pad), :] = score

    @pl.when(b == nb - 1)
    def _():
        n_blocks = sc_ref.shape[0] // TILE
        n_tiles = L // PAGE_SIZE
        tp = past + lax.broadcasted_iota(I32, (TILE, L), 0) % n_tok
        sp = lax.broadcasted_iota(I32, (TILE, L), 1)
        adm_blk = sp <= tp
        d_r = lax.broadcasted_iota(I32, (PAGE_SIZE, PAGE_SIZE * A_KV_HEADS), 0)
        d_c = lax.broadcasted_iota(I32, (PAGE_SIZE, PAGE_SIZE * A_KV_HEADS), 1)
        dup = jnp.where(d_c // A_KV_HEADS == d_r, 1.0, 0.0).astype(BF16)
        for rb in range(n_blocks):
            rows = slice(rb * TILE, (rb + 1) * TILE)
            skey_ref[...] = _sortable_key(jnp.where(adm_blk, sc_ref[rows, :], -jnp.inf))
            sel = jnp.where(_topk_member(skey_ref, k_sel) & adm_blk, 1.0, 0.0).astype(BF16)
            stacked = jnp.concatenate([sel[:, j * PAGE_SIZE:(j + 1) * PAGE_SIZE] for j in range(n_tiles)], axis=0)
            stacked = jnp.dot(stacked, dup, preferred_element_type=F32)
            mask_ref[rows, :] = jnp.concatenate(
                [stacked[j * TILE:(j + 1) * TILE, :] for j in range(n_tiles)], axis=1)


def dsa_sample_select(cache_ik_t, layer, page_table, qi_rows, wmat, ki_new_t, n_tok, k_sel):
    DB, n_pages = page_table.shape
    rows_pad = wmat.shape[1]
    per = rows_pad // n_tok
    n_rows = DB // per * rows_pad
    L = (n_pages + 1) * PAGE_SIZE
    page_specs = [pl.BlockSpec((None, None, IDX_DIM, PAGE_SIZE), functools.partial(
        lambda b, pt, e, p: (layer, pt[b * per + e, p], 0, 0), e=e, p=p))
        for e in range(per) for p in range(n_pages)]
    grid_spec = pltpu.PrefetchScalarGridSpec(
        num_scalar_prefetch=1,
        grid=(DB // per,),
        in_specs=page_specs + [
            pl.BlockSpec((per,) + qi_rows.shape[1:], lambda b, pt: (b, 0, 0)),
            pl.BlockSpec((1,) + wmat.shape[1:], lambda b, pt: (b, 0, 0)),
            pl.BlockSpec((per, IDX_DIM, PAGE_SIZE), lambda b, pt: (b, 0, 0))],
        out_specs=pl.BlockSpec((n_rows, A_KV_HEADS * L), lambda b, pt: (0, 0)),
        scratch_shapes=[pltpu.VMEM((n_rows, L), F32),
                        pltpu.VMEM((TILE, L), I32)],
    )
    return pl.pallas_call(
        functools.partial(_dsa_sample_select_kernel, n_pages=n_pages, n_tok=n_tok, k_sel=k_sel,
                          rows_pad=rows_pad),
        out_shape=jax.ShapeDtypeStruct((n_rows, A_KV_HEADS * L), F32),
        grid_spec=grid_spec,
        compiler_params=_cparams(("arbitrary",)),
        name="dsa_sample_select",
    )(page_table, *([cache_ik_t] * (per * n_pages)), qi_rows, wmat, ki_new_t)


def _dsa_sample_attend_kernel(pt_ref, *refs, n_pages, n_tok, rows_pad):
    del pt_ref
    per = rows_pad // n_tok
    k_refs = refs[:per * n_pages]
    v_refs = refs[per * n_pages:2 * per * n_pages]
    q_ref, kn_ref, vn_ref, mask_ref, bias_ref, qg_ref, o_ref = refs[2 * per * n_pages:]
    rows = n_tok * A_HEADS
    page_rows = PAGE_SIZE * A_KV_HEADS
    n_cols = mask_ref.shape[1]
    pad = jnp.zeros((page_rows - n_tok * A_KV_HEADS, HEAD_DIM), BF16)
    rep = A_HEADS // A_KV_HEADS
    grp = (lax.broadcasted_iota(I32, (rows, 1), 0) % A_HEADS) // rep
    own_group = (lax.broadcasted_iota(I32, (rows, n_cols), 1) % A_KV_HEADS) == grp
    member = mask_ref[...]

    for e in range(per):
        def gather(page_refs, new_ref):
            return jnp.concatenate([r[...].astype(BF16) for r in page_refs[e * n_pages:(e + 1) * n_pages]]
                                   + [new_ref[e].astype(BF16), pad], axis=0)

        q = (_rms(q_ref[e], qg_ref[...]) * HEAD_DIM ** -0.5).astype(BF16)
        sel = jnp.concatenate(
            [jnp.broadcast_to(member[e * n_tok + t:e * n_tok + t + 1, :], (A_HEADS, n_cols)) for t in range(n_tok)],
            axis=0)
        valid = (sel > 0.5) & own_group
        logits = lax.dot_general(q, gather(k_refs, kn_ref), (((1,), (1,)), ((), ())), preferred_element_type=F32)
        logits = jnp.where(valid, logits + bias_ref[...], NEG_BIG)
        m = jnp.max(logits, axis=-1, keepdims=True)
        p = jnp.exp(logits - m)
        den = jnp.sum(p, axis=-1, keepdims=True)
        o = jnp.dot(p.astype(BF16), gather(v_refs, vn_ref), preferred_element_type=F32)
        o_ref[e] = (o / den).astype(o_ref.dtype)


def dsa_sample_attend(cache_k, cache_v, layer, page_table, q_rows, k_new, v_new, mask, bias_tab, q_norm_g):
    DB, n_pages = page_table.shape
    n_tok = k_new.shape[1] // A_KV_HEADS
    rows = n_tok * A_HEADS
    rows_pad = SUBLANES
    per = rows_pad // n_tok
    n_cols = mask.shape[1]
    page_rows = PAGE_SIZE * A_KV_HEADS
    page_specs = [pl.BlockSpec((None, None, page_rows, HEAD_DIM), functools.partial(
        lambda b, pt, e, p: (layer, pt[b * per + e, p], 0, 0), e=e, p=p))
        for e in range(per) for p in range(n_pages)]
    grid_spec = pltpu.PrefetchScalarGridSpec(
        num_scalar_prefetch=1,
        grid=(DB // per,),
        in_specs=page_specs + page_specs + [
            pl.BlockSpec((per, rows, HEAD_DIM), lambda b, pt: (b, 0, 0)),
            pl.BlockSpec((per, n_tok * A_KV_HEADS, HEAD_DIM), lambda b, pt: (b, 0, 0)),
            pl.BlockSpec((per, n_tok * A_KV_HEADS, HEAD_DIM), lambda b, pt: (b, 0, 0)),
            pl.BlockSpec((rows_pad, n_cols), lambda b, pt: (b, 0)),
            pl.BlockSpec((rows, n_cols), lambda b, pt: (0, 0), pipeline_mode=pl.Buffered(1)),
            pl.BlockSpec((1, HEAD_DIM), lambda b, pt: (0, 0), pipeline_mode=pl.Buffered(1))],
        out_specs=pl.BlockSpec((per, rows, HEAD_DIM), lambda b, pt: (b, 0, 0)),
    )
    return pl.pallas_call(
        functools.partial(_dsa_sample_attend_kernel, n_pages=n_pages, n_tok=n_tok, rows_pad=rows_pad),
        out_shape=jax.ShapeDtypeStruct((DB, rows, HEAD_DIM), BF16),
        grid_spec=grid_spec,
        compiler_params=_cparams(("parallel",)),
        name="dsa_sample_attend",
    )(page_table, *([cache_k] * (per * n_pages)), *([cache_v] * (per * n_pages)), q_rows, k_new, v_new, mask,
      bias_tab, q_norm_g.reshape(1, HEAD_DIM))


def _log_sigmoid(z):
    return jnp.minimum(z, 0.0) - jnp.log(1.0 + jnp.exp(-jnp.abs(z)))


def _seg_masks(seg):
    r = lax.broadcasted_iota(I32, (TILE, TILE), 0)
    c = lax.broadcasted_iota(I32, (TILE, TILE), 1)
    return r, c, (r // seg) == (c // seg)


def _gla_levels(seg):
    w, out = seg // 2, []
    while w >= 1:
        out.append(w)
        w //= 2
    return out


def _gla_sum_matrices(seg):
    r = jnp.arange(TILE)[:, None]
    c = jnp.arange(TILE)[None, :]
    mats = []
    for w in _gla_levels(seg):
        same = (r // (2 * w)) == (c // (2 * w))
        r_right = (r % (2 * w)) >= w
        c_right = (c % (2 * w)) >= w
        mats.append(same & r_right & c_right & (c <= r))
    for w in _gla_levels(seg):
        same = (r // (2 * w)) == (c // (2 * w))
        r_right = (r % (2 * w)) >= w
        c_right = (c % (2 * w)) >= w
        mats.append(same & (~r_right) & (~c_right) & (c > r))
    same_seg = (r // seg) == (c // seg)
    mats.append(same_seg & (c <= r))
    mats.append(same_seg & (c > r))
    return jnp.concatenate(mats, axis=0).astype(BF16)


def _bdot(a, b):
    return jnp.dot(a.astype(BF16), b.astype(BF16), preferred_element_type=F32)


def _bdot_nt(a, b):
    return lax.dot_general(a.astype(BF16), b.astype(BF16), (((1,), (1,)), ((), ())), preferred_element_type=F32)


def _gla_common(tiles, wg_ref, bg_ref, mats_ref, seg):
    n = len(tiles)
    kw = B_HEADS * B_DK
    las = []
    for _, _, misc_ref in tiles:
        gb = misc_ref[:, MISC_GB:MISC_GB + GATE_RANK]
        z = jnp.dot(gb, wg_ref[...], precision=HIGHEST, preferred_element_type=F32) + bg_ref[...]
        las.append(_log_sigmoid(z) / GATE_TEMP)
    la = jnp.concatenate(las, axis=1)
    la_hi = la.astype(BF16)
    la_lo = (la - la_hi.astype(F32)).astype(BF16)
    mats = mats_ref[...]
    sums = (jnp.dot(mats, la_hi, preferred_element_type=F32) + jnp.dot(mats, la_lo, preferred_element_type=F32))
    levels = _gla_levels(seg)
    nl = len(levels)
    qs = [qb_ref[...] * B_DK ** -0.5 for qb_ref, _, _ in tiles]
    ks = [kb_ref[...] for _, kb_ref, _ in tiles]
    r, c, _ = _seg_masks(seg)
    atts = [[jnp.where(r == c, _bdot_nt(qs[i][:, h * B_DK:(h + 1) * B_DK], ks[i][:, h * B_DK:(h + 1) * B_DK]), 0.0)
             for h in range(B_HEADS)] for i in range(n)]
    for li, w in enumerate(levels):
        pair = ((r // (2 * w)) == (c // (2 * w))) & ((r % (2 * w)) >= w) & ((c % (2 * w)) < w)
        qd = [(qs[i] * jnp.exp(sums[li * TILE:(li + 1) * TILE, i * kw:(i + 1) * kw])).astype(BF16) for i in range(n)]
        kd = [(ks[i] * jnp.exp(sums[(nl + li) * TILE:(nl + li + 1) * TILE, i * kw:(i + 1) * kw])).astype(BF16)
              for i in range(n)]
        for h in range(B_HEADS):
            hs = slice(h * B_DK, (h + 1) * B_DK)
            for i in range(n):
                atts[i][h] = atts[i][h] + jnp.where(pair, _bdot_nt(qd[i][:, hs], kd[i][:, hs]), 0.0)
    out = []
    for i in range(n):
        cs = slice(i * kw, (i + 1) * kw)
        b_cum = sums[2 * nl * TILE:(2 * nl + 1) * TILE, cs]
        rem = sums[(2 * nl + 1) * TILE:(2 * nl + 2) * TILE, cs]
        out.append((qs[i], ks[i], atts[i], b_cum, rem))
    return out


def _gla_finish(o_heads, rb_ref, go_ref, o_ref):
    go = go_ref[...]
    for h in range(B_HEADS):
        vs = slice(h * B_DV, (h + 1) * B_DV)
        rb = rb_ref[:, vs]
        o_ref[:, vs] = (_rms(o_heads[h], go) * (rb * jax.nn.sigmoid(rb))).astype(o_ref.dtype)


def _gla_prompt_kernel(qb_ref, kb_ref, vb_ref, rb_ref, misc_ref, wg_ref, bg_ref, go_ref, mats_ref,
                       o_ref, s_ref, state_ref):
    ci = pl.program_id(0)

    @pl.when(ci == 0)
    def _():
        state_ref[...] = jnp.zeros_like(state_ref)

    nb = qb_ref.shape[0]
    common = _gla_common([(qb_ref.at[b], kb_ref.at[b], misc_ref.at[b]) for b in range(nb)],
                         wg_ref, bg_ref, mats_ref, TILE)
    vals = [vb_ref[b] for b in range(nb)]
    states = [state_ref[b] for b in range(nb)]
    qes = [common[b][0] * jnp.exp(common[b][3]) for b in range(nb)]
    o_heads = [[] for _ in range(nb)]
    for h in range(B_HEADS):
        ks = slice(h * B_DK, (h + 1) * B_DK)
        vs = slice(h * B_DV, (h + 1) * B_DV)
        for b in range(nb):
            o_heads[b].append(_bdot(qes[b][:, ks], states[b][ks, :]) + _bdot(common[b][2][h], vals[b][:, vs]))
    for b in range(nb):
        _gla_finish(o_heads[b], rb_ref.at[b], go_ref, o_ref.at[b])

    ke_ts = [(common[b][1] * jnp.exp(common[b][4])).T for b in range(nb)]
    e_cols = [jnp.broadcast_to(jnp.exp(common[b][3][TILE - 1:TILE, :]), (TILE, B_HEADS * B_DK)).T[:, 0:1]
              for b in range(nb)]
    for b in range(nb):
        upd = jnp.concatenate(
            [_bdot(ke_ts[b][h * B_DK:(h + 1) * B_DK, :], vals[b][:, h * B_DV:(h + 1) * B_DV])
             for h in range(B_HEADS)], axis=0)
        new_state = states[b] * e_cols[b] + upd
        state_ref[b] = new_state
        s_ref[b] = new_state


def gla_prompt(proj, w_gate, b_gate, g_out, n_batch, seq):
    nc = seq // TILE
    kwid = B_HEADS * B_DK
    vwid = B_HEADS * B_DV
    mats = _gla_sum_matrices(TILE)
    proj3 = proj.reshape(n_batch, seq, proj.shape[1])
    o, s = pl.pallas_call(
        _gla_prompt_kernel,
        out_shape=(jax.ShapeDtypeStruct((n_batch, seq, vwid), BF16),
                   jax.ShapeDtypeStruct((n_batch, kwid, B_DV), F32)),
        grid=(nc,),
        in_specs=[pl.BlockSpec((n_batch, TILE, kwid), lambda c: (0, c, QB_OFF // kwid)),
                  pl.BlockSpec((n_batch, TILE, kwid), lambda c: (0, c, KB_OFF // kwid)),
                  pl.BlockSpec((n_batch, TILE, vwid), lambda c: (0, c, VB_OFF // vwid)),
                  pl.BlockSpec((n_batch, TILE, vwid), lambda c: (0, c, RB_OFF // vwid)),
                  pl.BlockSpec((n_batch, TILE, LANES), lambda c: (0, c, MISC_OFF // LANES)),
                  _resident((GATE_RANK, kwid)),
                  _resident((1, kwid)),
                  _resident((1, B_DV)),
                  _resident(mats.shape)],
        out_specs=(pl.BlockSpec((n_batch, TILE, vwid), lambda c: (0, c, 0)),
                   pl.BlockSpec((n_batch, kwid, B_DV), lambda c: (0, 0, 0))),
        scratch_shapes=[pltpu.VMEM((n_batch, kwid, B_DV), F32)],
        compiler_params=_cparams(("arbitrary",)),
        name="gla_prompt",
    )(proj3, proj3, proj3, proj3, proj3, w_gate, b_gate.reshape(1, kwid), g_out.reshape(1, B_DV), mats)
    return o.reshape(n_batch * seq, vwid), s.reshape(n_batch, B_HEADS, B_DK, B_DV)


def _gla_sample_kernel(qb_ref, kb_ref, vb_ref, rb_ref, misc_ref, wg_ref, bg_ref, go_ref, mats_ref, s0_ref,
                       o_ref, s_ref, *, seg):
    nbt = TILE // seg
    (q, k, att, b_cum, rem), = _gla_common([(qb_ref, kb_ref, misc_ref)], wg_ref, bg_ref, mats_ref, seg)
    v = vb_ref[...]
    qe = q * jnp.exp(b_cum)
    ke = k * jnp.exp(rem)
    r1 = lax.broadcasted_iota(I32, (TILE, 1), 0)
    e_last = jnp.where(r1 % seg == seg - 1, jnp.exp(b_cum), 0.0)
    wide = nbt * B_DK
    mq = (lax.broadcasted_iota(I32, (TILE, wide), 0) // seg) == (lax.broadcasted_iota(I32, (TILE, wide), 1) // B_DK)
    mk = (lax.broadcasted_iota(I32, (wide, TILE), 0) // B_DK) == (lax.broadcasted_iota(I32, (wide, TILE), 1) // seg)
    o_heads = []
    for h in range(B_HEADS):
        ks = slice(h * B_DK, (h + 1) * B_DK)
        vs = slice(h * B_DV, (h + 1) * B_DV)
        state = s0_ref[:, h].reshape(wide, B_DV)
        q_bd = jnp.where(mq, jnp.concatenate([qe[:, ks]] * nbt, axis=1), 0.0)
        o_heads.append(_bdot(q_bd, state) + _bdot(att[h], v[:, vs]))
        pair_t = jnp.concatenate([ke[:, ks], e_last[:, ks]], axis=1).T
        k_bd = jnp.where(mk, jnp.concatenate([pair_t[:B_DK]] * nbt, axis=0), 0.0)
        e_bd = jnp.where(mk, jnp.concatenate([pair_t[B_DK:]] * nbt, axis=0), 0.0)
        e_col = jnp.sum(e_bd, axis=-1, keepdims=True)
        new_state = state * e_col + _bdot(k_bd, v[:, vs])
        s_ref[:, h] = new_state.reshape(nbt, B_DK, B_DV)
    _gla_finish(o_heads, rb_ref, go_ref, o_ref)


def gla_sample(proj, w_gate, b_gate, g_out, s0, layer, n_tok):
    T = proj.shape[0]
    nbt = TILE // n_tok
    kwid = B_HEADS * B_DK
    vwid = B_HEADS * B_DV
    mats = _gla_sum_matrices(n_tok)
    return pl.pallas_call(
        functools.partial(_gla_sample_kernel, seg=n_tok),
        out_shape=(jax.ShapeDtypeStruct((T, vwid), BF16),
                   jax.ShapeDtypeStruct(s0.shape[1:], F32)),
        grid=(T // TILE,),
        in_specs=[pl.BlockSpec((TILE, kwid), lambda i: (i, QB_OFF // kwid)),
                  pl.BlockSpec((TILE, kwid), lambda i: (i, KB_OFF // kwid)),
                  pl.BlockSpec((TILE, vwid), lambda i: (i, VB_OFF // vwid)),
                  pl.BlockSpec((TILE, vwid), lambda i: (i, RB_OFF // vwid)),
                  pl.BlockSpec((TILE, LANES), lambda i: (i, MISC_OFF // LANES)),
                  _resident((GATE_RANK, kwid)),
                  _resident((1, kwid)),
                  _resident((1, B_DV)),
                  _resident(mats.shape),
                  pl.BlockSpec((None, nbt, B_HEADS, B_DK, B_DV), lambda i: (layer, i, 0, 0, 0))],
        out_specs=(pl.BlockSpec((TILE, vwid), lambda i: (i, 0)),
                   pl.BlockSpec((nbt, B_HEADS, B_DK, B_DV), lambda i: (i, 0, 0, 0))),
        compiler_params=_cparams(("parallel",)),
        name="gla_sample",
    )(proj, proj, proj, proj, proj, w_gate, b_gate.reshape(1, kwid), g_out.reshape(1, B_DV), mats, s0)


def _gelu(x):
    return jax.nn.gelu(x)


def _gmlp_kernel(uc_ref, vc_ref, gv_ref, ws_ref, bcol_ref, o_ref, vn_ref, *, seg):
    r, c, same_seg = _seg_masks(seg)
    keep = same_seg & (c <= r)
    u = _gelu(uc_ref[...])
    vg = _gelu(vc_ref[...])
    for g in range(C_GROUPS):
        gs = slice(g * C_GROUP_DIM, (g + 1) * C_GROUP_DIM)
        vn = _rms(vg[:, gs], gv_ref[:, gs])
        vn_ref[:, gs] = vn
        w = jnp.where(keep, ws_ref[g], 0.0).astype(BF16)
        s = jnp.dot(w, vn.astype(BF16), preferred_element_type=F32) + bcol_ref[:, g:g + 1]
        o_ref[:, gs] = (u[:, gs] * s).astype(o_ref.dtype)


def gmlp(proj, g_v, w_tiles, b_cols, seg):
    T = proj.shape[0]
    cw = C_GROUPS * C_GROUP_DIM
    return pl.pallas_call(
        functools.partial(_gmlp_kernel, seg=seg),
        out_shape=(jax.ShapeDtypeStruct((T, cw), BF16),
                   jax.ShapeDtypeStruct((T, cw), F32)),
        grid=(T // TILE,),
        in_specs=[pl.BlockSpec((TILE, cw), lambda i: (i, UC_OFF // cw)),
                  pl.BlockSpec((TILE, cw), lambda i: (i, VC_OFF // cw)),
                  _resident((1, cw)),
                  _resident((C_GROUPS, TILE, TILE)),
                  _resident((TILE, C_GROUPS))],
        out_specs=(pl.BlockSpec((TILE, cw), lambda i: (i, 0)),
                   pl.BlockSpec((TILE, cw), lambda i: (i, 0))),
        compiler_params=_cparams(("parallel",)),
        name="gmlp",
    )(proj, proj, g_v.reshape(1, cw), w_tiles, b_cols)


def _ffn_kernel(h_ref, oa_ref, ob_ref, oc_ref, wo_ref, g_ref, wg_ref, wu_ref, wd_ref, o_ref, n_ref):
    j = pl.program_id(1)

    @pl.when(j == 0)
    def _():
        aw = oa_ref.shape[1]
        bw = ob_ref.shape[1]
        h = h_ref[...] + jnp.dot(oa_ref[...], wo_ref[0:aw, :], preferred_element_type=F32)
        h = h + jnp.dot(ob_ref[...], wo_ref[aw:aw + bw, :], preferred_element_type=F32)
        h = h + jnp.dot(oc_ref[...], wo_ref[aw + bw:, :], preferred_element_type=F32)
        n_ref[...] = _rms(h, g_ref[...]).astype(BF16)
        o_ref[...] = h

    n = n_ref[...]
    a = jnp.dot(n, wg_ref[...], preferred_element_type=F32)
    u = jnp.dot(n, wu_ref[...], preferred_element_type=F32)
    act = (a * jax.nn.sigmoid(a) * u).astype(BF16)
    o_ref[...] += jnp.dot(act, wd_ref[...], preferred_element_type=F32)


def out_proj_ffn(h, o_a, o_b, o_c, w_out, g, w_gate, w_up, w_down, tm, tf):
    T, D = h.shape
    tm = min(tm, T)
    FF = w_gate.shape[1]
    return pl.pallas_call(
        _ffn_kernel,
        out_shape=jax.ShapeDtypeStruct((T, D), F32),
        grid=(T // tm, FF // tf),
        in_specs=[pl.BlockSpec((tm, D), lambda i, j: (i, 0)),
                  pl.BlockSpec((tm, o_a.shape[1]), lambda i, j: (i, 0)),
                  pl.BlockSpec((tm, o_b.shape[1]), lambda i, j: (i, 0)),
                  pl.BlockSpec((tm, o_c.shape[1]), lambda i, j: (i, 0)),
                  _resident(w_out.shape),
                  _resident((1, D)),
                  pl.BlockSpec((D, tf), lambda i, j: (0, j)),
                  pl.BlockSpec((D, tf), lambda i, j: (0, j)),
                  pl.BlockSpec((tf, D), lambda i, j: (j, 0))],
        out_specs=pl.BlockSpec((tm, D), lambda i, j: (i, 0)),
        scratch_shapes=[pltpu.VMEM((tm, D), BF16)],
        compiler_params=_cparams(("parallel", "arbitrary")),
        name="out_proj_ffn",
    )(h, o_a, o_b, o_c, w_out, g.reshape(1, D), w_gate, w_up, w_down)


def _ple_kernel(h_ref, p_ref, g_ref, wgate_ref, wproj_ref, o_ref):
    h = h_ref[...]
    n = _rms(h, g_ref[...]).astype(BF16)
    gate = jax.nn.sigmoid(jnp.dot(n, wgate_ref[...], preferred_element_type=F32))
    emb = jnp.dot(p_ref[...].astype(BF16), wproj_ref[...], preferred_element_type=F32)
    o_ref[...] = h + gate * emb


def ple(h, p, g, w_gate, w_proj, layer, tm):
    T, D = h.shape
    tm = min(tm, T)
    P = p.shape[2]
    return pl.pallas_call(
        _ple_kernel,
        out_shape=jax.ShapeDtypeStruct((T, D), F32),
        grid=(T // tm,),
        in_specs=[pl.BlockSpec((tm, D), lambda i: (i, 0)),
                  pl.BlockSpec((None, tm, P), lambda i: (layer, i, 0)),
                  _resident((1, D)),
                  _resident(w_gate.shape),
                  _resident(w_proj.shape)],
        out_specs=pl.BlockSpec((tm, D), lambda i: (i, 0)),
        compiler_params=_cparams(("parallel",)),
        name="ple",
    )(h, p, g.reshape(1, D), w_gate, w_proj)


_W_IN_SEGMENTS = (("q", 1024), ("k", 256), ("v", 256), ("qi", 1024), ("ki", 64), ("wi", 16), ("qb", 256),
                  ("kb", 256), ("vb", 512), ("gb", 16), ("rb", 512), ("uc", 512), ("vc", 512))
_W_IN_PACKED_ORDER = ("q", "qi", "vb", "rb", "uc", "vc", "k", "v", "qb", "kb", "ki", "wi", "gb")


def _pack_kernel(wt_ref, o_ref):
    src, start = {}, 0
    for name, size in _W_IN_SEGMENTS:
        src[name] = (start, size)
        start += size
    dst = 0
    small = []
    for name in _W_IN_PACKED_ORDER:
        s0, size = src[name]
        if size < LANES:
            small.append(wt_ref[s0:s0 + size, :])
            continue
        o_ref[:, dst:dst + size] = wt_ref[s0:s0 + size, :].T.astype(BF16)
        dst += size
    used = sum(x.shape[0] for x in small)
    small.append(jnp.zeros((LANES - used, wt_ref.shape[1]), F32))
    o_ref[:, dst:dst + LANES] = jnp.concatenate(small, axis=0).T.astype(BF16)


def _pack_w_in(w, tr=256):
    depth, D, N = w.shape
    return pl.pallas_call(
        _pack_kernel,
        out_shape=jax.ShapeDtypeStruct((depth, D, PROJ_PACKED), BF16),
        grid=(depth, D // tr),
        in_specs=[pl.BlockSpec((None, N, tr), lambda l, i: (l, 0, i))],
        out_specs=pl.BlockSpec((None, tr, PROJ_PACKED), lambda l, i: (l, i, 0)),
        compiler_params=_cparams(("parallel", "parallel")),
        name="pack_w_in",
    )(jnp.swapaxes(w, 1, 2))


def _mixer_tail(h, o_a, o_b, o_c, p_all, lw, layer, tm, tm_ffn):
    h = out_proj_ffn(h, o_a, o_b, o_c, lw["w_out"], lw["g_ffn"], lw["w_ffn_gate"], lw["w_ffn_up"],
                     lw["w_ffn_down"], tm_ffn, 512)
    return ple(h, p_all, lw["g_ple"], lw["w_ple_gate"], lw["w_ple_proj"], layer, tm)


def kernel(x_prompt, x_sample, cache_k, cache_v, cache_idx_k, state_gla, page_table, p_prompt, p_sample,
           g_mix, w_in, q_norm_g, k_norm_g, rel_bias, w_gate_b, b_gate_b, g_out_b, g_v_c, w_spatial,
           b_spatial, w_out, g_ffn, w_ffn_gate, w_ffn_up, w_ffn_down, g_ple, w_ple_gate, w_ple_proj):
    n_batch, seq, d_model = x_prompt.shape
    dec_batch, dec_seq, _ = x_sample.shape
    depth = w_in.shape[0]
    n_pages = page_table.shape[1]
    past = n_pages * PAGE_SIZE
    kw = A_KV_HEADS * HEAD_DIM
    tp, ts = n_batch * seq, dec_batch * dec_seq
    rows_pad = SUBLANES
    l_sample = past + PAGE_SIZE
    k_sel_s = min(TOPK_MAX, (past + dec_seq) // 4)

    bias_p = bias_table_prompt(rel_bias)
    bias_s = bias_table_sample(rel_bias, past, dec_seq, A_KV_HEADS * l_sample)
    cache_ik_t = jnp.swapaxes(cache_idx_k, 2, 3)
    cache_k2 = cache_k.reshape(depth, cache_k.shape[1], PAGE_SIZE * A_KV_HEADS, HEAD_DIM)
    cache_v2 = cache_v.reshape(depth, cache_v.shape[1], PAGE_SIZE * A_KV_HEADS, HEAD_DIM)

    hp = x_prompt.reshape(tp, d_model)
    hs = x_sample.reshape(ts, d_model)
    outs = {k: [] for k in ("kp", "vp", "ikp", "sp", "ks", "vs", "iks", "ss", "cs")}
    per_s = rows_pad // dec_seq
    place_t = (jnp.arange(rows_pad)[:, None, None]
               == jnp.arange(per_s)[None, :, None] * dec_seq + jnp.arange(dec_seq)[None, None, :]
               ).astype(F32)
    w_ple_proj_b = w_ple_proj.astype(BF16)
    w_packed = _pack_w_in(w_in)
    pp_all = p_prompt.reshape(depth, tp, -1)
    ps_all = p_sample.reshape(depth, ts, -1)
    for i in range(depth):
        b_cols_p = b_spatial[i].T
        reps = TILE // dec_seq
        w_tiles_s = jnp.tile(w_spatial[i][:, :dec_seq, :dec_seq], (1, reps, reps))
        b_cols_s = jnp.tile(b_spatial[i][:, :dec_seq].T, (reps, 1))

        proj, (wo_b, wg_b, wu_b, wd_b, wpg_b) = in_projection(
            hp, g_mix[i], w_packed, i, 256, cast=(w_out, w_ffn_gate, w_ffn_up, w_ffn_down, w_ple_gate))
        lw = dict(w_out=wo_b, g_ffn=g_ffn[i], w_ffn_gate=wg_b, w_ffn_up=wu_b, w_ffn_down=wd_b,
                  g_ple=g_ple[i], w_ple_gate=wpg_b, w_ple_proj=w_ple_proj_b[i])
        kn, vv, ik, vt = kv_post(proj, k_norm_g[i], 512, True)
        o_a = dsa_prompt(proj, kn, vt, bias_p, q_norm_g[i], n_batch, seq)
        o_b, s_p = gla_prompt(proj, w_gate_b[i], b_gate_b[i], g_out_b[i], n_batch, seq)
        o_c, _ = gmlp(proj, g_v_c[i], w_spatial[i], b_cols_p, TILE)
        hp = _mixer_tail(hp, o_a, o_b, o_c, pp_all, lw, i, 256, 512)
        outs["kp"].append(kn.reshape(n_batch, seq, A_KV_HEADS, HEAD_DIM))
        outs["vp"].append(vv.reshape(n_batch, seq, A_KV_HEADS, HEAD_DIM))
        outs["ikp"].append(ik.reshape(n_batch, seq, IDX_DIM))
        outs["sp"].append(s_p)

        proj, _ = in_projection(hs, g_mix[i], w_packed, i, 256)
        kn, vv, ik = kv_post(proj, k_norm_g[i], 512, False)
        qi_rows =proj[:, QI_OFF:QI_OFF + IDX_HEADS * IDX_DIM].reshape(dec_batch, dec_seq * IDX_HEADS, IDX_DIM)
        wi = proj[:, MISC_OFF + MISC_WI:MISC_OFF + MISC_WI + IDX_HEADS].reshape(dec_batch, dec_seq, IDX_HEADS)
        wi = wi * (IDX_HEADS ** -0.5 * IDX_DIM ** -0.5)
        wmat = (place_t[None, :, :, :, None] * wi.reshape(dec_batch // per_s, 1, per_s, dec_seq, IDX_HEADS)
                ).reshape(dec_batch // per_s, rows_pad, per_s * dec_seq * IDX_HEADS)
        ki_new_t = jnp.pad(jnp.swapaxes(ik.reshape(dec_batch, dec_seq, IDX_DIM), 1, 2),
                           ((0, 0), (0, 0), (0, PAGE_SIZE - dec_seq)))
        mask = dsa_sample_select(cache_ik_t, i, page_table, qi_rows, wmat, ki_new_t, dec_seq, k_sel_s)
        q_rows = proj[:, Q_OFF:Q_OFF + A_HEADS * HEAD_DIM].reshape(dec_batch, dec_seq * A_HEADS, HEAD_DIM)
        o_a = dsa_sample_attend(cache_k2, cache_v2, i, page_table, q_rows,
                                kn.reshape(dec_batch, dec_seq * A_KV_HEADS, HEAD_DIM),
                                vv.reshape(dec_batch, dec_seq * A_KV_HEADS, HEAD_DIM), mask, bias_s, q_norm_g[i])
        o_a = o_a.reshape(ts, A_HEADS * HEAD_DIM)
        o_b, s_s = gla_sample(proj, w_gate_b[i], b_gate_b[i], g_out_b[i], state_gla, i, dec_seq)
        o_c, vn = gmlp(proj, g_v_c[i], w_tiles_s, b_cols_s, dec_seq)
        hs = _mixer_tail(hs, o_a, o_b, o_c, ps_all, lw, i, 256, 512)
        outs["ks"].append(kn.reshape(dec_batch, dec_seq, A_KV_HEADS, HEAD_DIM))
        outs["vs"].append(vv.reshape(dec_batch, dec_seq, A_KV_HEADS, HEAD_DIM))
        outs["iks"].append(ik.reshape(dec_batch, dec_seq, IDX_DIM))
        outs["ss"].append(s_s)
        outs["cs"].append(vn.reshape(dec_batch, dec_seq, -1))

    st = {k: jnp.stack(v) for k, v in outs.items()}
    return (hp.reshape(n_batch, seq, d_model), hs.reshape(dec_batch, dec_seq, d_model),
            st["kp"], st["vp"], st["ikp"], st["sp"], st["ks"], st["vs"], st["iks"], st["ss"], st["cs"])
```

```python
import functools
import math

import jax
import jax.numpy as jnp
from jax import lax
from jax.experimental import pallas as pl
from jax.experimental.pallas import tpu as pltpu

F32 = jnp.float32
BF16 = jnp.bfloat16
I32 = jnp.int32
HIGHEST = lax.Precision.HIGHEST

LANES = 128
SUBLANES = 8
VMEM_LIMIT = 56 * 1024 * 1024

HEAD_DIM = 128
A_HEADS = 8
A_KV_HEADS = 2
IDX_HEADS = 16
IDX_DIM = 64
TOPK_MAX = 256
NUM_BUCKETS = 32
MAX_DISTANCE = 128
B_HEADS = 4
B_DK = 64
B_DV = 128
GATE_RANK = 16
GATE_TEMP = 16.0
C_GROUPS = 4
C_GROUP_DIM = 128
PAGE_SIZE = 128
EPS = 1e-6
NEG_BIG = -1e30
INT_MIN = -(2 ** 31)
NEG_INF_KEY = -2139095041

TILE = 128
QBLK = 256

Q_OFF, QI_OFF, VB_OFF, RB_OFF, UC_OFF, VC_OFF = 0, 1024, 2048, 2560, 3072, 3584
K_OFF, V_OFF, QB_OFF, KB_OFF, MISC_OFF = 4096, 4352, 4608, 4864, 5120
PROJ_PACKED = 5248
MISC_KI, MISC_WI, MISC_GB = 0, 64, 80


def _cparams(sem):
    return pltpu.CompilerParams(dimension_semantics=sem, vmem_limit_bytes=VMEM_LIMIT)


def _rms(x, g):
    return x * lax.rsqrt(jnp.mean(x * x, axis=-1, keepdims=True) + EPS) * g


def _resident(shape):
    nd = len(shape)
    return pl.BlockSpec(shape, lambda *_: (0,) * nd, pipeline_mode=pl.Buffered(1))


def _layer_resident(shape, layer):
    nd = len(shape)
    return pl.BlockSpec((None,) + tuple(shape), lambda *_: (layer,) + (0,) * nd, pipeline_mode=pl.Buffered(1))


def _proj_kernel(x_ref, g_ref, w_ref, *refs, n_cast):
    cast_in, o_ref, cast_out = refs[:n_cast], refs[n_cast], refs[n_cast + 1:]
    n = _rms(x_ref[...], g_ref[...]).astype(BF16)
    ncol = o_ref.shape[1]
    step = 512
    for c0 in range(0, ncol, step):
        c1 = min(c0 + step, ncol)
        o_ref[:, c0:c1] = jnp.dot(n, w_ref[:, c0:c1], preferred_element_type=F32)
    for src, dst in zip(cast_in, cast_out):
        dst[...] = src[...].astype(dst.dtype)


def in_projection(h, g, w_packed, layer, tm, cast=()):
    T, D = h.shape
    tm = min(tm, T)
    N = w_packed.shape[2]
    steps = T // tm
    cast_specs_in = [pl.BlockSpec((None, w.shape[1] // steps, w.shape[2]), lambda i: (layer, i, 0)) for w in cast]
    cast_specs_out = [pl.BlockSpec((w.shape[1] // steps, w.shape[2]), lambda i: (i, 0)) for w in cast]
    outs = pl.pallas_call(
        functools.partial(_proj_kernel, n_cast=len(cast)),
        out_shape=(jax.ShapeDtypeStruct((T, N), F32),) + tuple(
            jax.ShapeDtypeStruct(w.shape[1:], BF16) for w in cast),
        grid=(steps,),
        in_specs=[pl.BlockSpec((tm, D), lambda i: (i, 0)),
                  _resident((1, D)),
                  _layer_resident((D, N), layer)] + cast_specs_in,
        out_specs=(pl.BlockSpec((tm, N), lambda i: (i, 0)),) + tuple(cast_specs_out),
        compiler_params=_cparams(("parallel",)),
        name="in_projection",
    )(h, g.reshape(1, D), w_packed, *cast)
    return outs[0], outs[1:]


def _kv_kernel(k_ref, v_ref, m_ref, g_ref, ko_ref, vo_ref, io_ref, vt_ref=None):
    g = g_ref[...]
    k = k_ref[...]
    for hh in range(A_KV_HEADS):
        sl = slice(hh * HEAD_DIM, (hh + 1) * HEAD_DIM)
        ko_ref[:, sl] = _rms(k[:, sl], g)
    v = v_ref[...]
    vo_ref[...] = v
    io_ref[...] = m_ref[:, MISC_KI:MISC_KI + IDX_DIM]
    if vt_ref is not None:
        for blk in range(vt_ref.shape[0]):
            vt_ref[blk] = v[blk * QBLK:(blk + 1) * QBLK, :].T.astype(vt_ref.dtype)


def kv_post(proj, k_norm_g, tm, with_vt):
    T = proj.shape[0]
    tm = min(tm, T)
    kw = A_KV_HEADS * HEAD_DIM
    out_shape = [jax.ShapeDtypeStruct((T, kw), F32),
                 jax.ShapeDtypeStruct((T, kw), F32),
                 jax.ShapeDtypeStruct((T, IDX_DIM), F32)]
    out_specs = [pl.BlockSpec((tm, kw), lambda i: (i, 0)),
                 pl.BlockSpec((tm, kw), lambda i: (i, 0)),
                 pl.BlockSpec((tm, IDX_DIM), lambda i: (i, 0))]
    if with_vt:
        out_shape.append(jax.ShapeDtypeStruct((T // QBLK, kw, QBLK), BF16))
        out_specs.append(pl.BlockSpec((tm // QBLK, kw, QBLK), lambda i: (i, 0, 0)))
    return pl.pallas_call(
        _kv_kernel,
        out_shape=tuple(out_shape),
        grid=(T // tm,),
        in_specs=[pl.BlockSpec((tm, kw), lambda i: (i, K_OFF // kw)),
                  pl.BlockSpec((tm, kw), lambda i: (i, V_OFF // kw)),
                  pl.BlockSpec((tm, LANES), lambda i: (i, MISC_OFF // LANES)),
                  _resident((1, HEAD_DIM))],
        out_specs=tuple(out_specs),
        compiler_params=_cparams(("parallel",)),
        name="kv_post",
    )(proj, proj, proj, k_norm_g.reshape(1, HEAD_DIM))


def _bucket(dist):
    n = jnp.maximum(dist, 0)
    max_exact = NUM_BUCKETS // 2
    large = max_exact + (jnp.log(jnp.maximum(n, 1).astype(F32) / max_exact)
                         / math.log(MAX_DISTANCE / max_exact)
                         * (NUM_BUCKETS - max_exact)).astype(I32)
    large = jnp.minimum(large, NUM_BUCKETS - 1)
    return jnp.where(n < max_exact, n, large)


def _bias_prompt_kernel(rb_ref, o_ref):
    c = lax.broadcasted_iota(I32, (TILE, TILE), 0)
    t = lax.broadcasted_iota(I32, (TILE, TILE), 1)
    for z in range(3):
        bucket = _bucket(t - c + (2 - z) * TILE)
        for h in range(A_HEADS):
            acc = jnp.zeros((TILE, TILE), F32)
            for b in range(NUM_BUCKETS):
                acc = jnp.where(bucket == b, rb_ref[b, h], acc)
            o_ref[h, z] = acc


def bias_table_prompt(rel_bias):
    return pl.pallas_call(
        _bias_prompt_kernel,
        out_shape=jax.ShapeDtypeStruct((A_HEADS, 3, TILE, TILE), F32),
        in_specs=[pl.BlockSpec(memory_space=pltpu.SMEM)],
        out_specs=pl.BlockSpec(memory_space=pltpu.VMEM),
        name="bias_table_prompt",
    )(rel_bias)


def _bias_sample_kernel(rbrows_ref, o_ref, *, past, n_tok):
    rows, L = o_ref.shape
    r = lax.broadcasted_iota(I32, (rows, L), 0)
    s = lax.broadcasted_iota(I32, (rows, L), 1) // A_KV_HEADS
    bucket = _bucket(past + r // A_HEADS - s)
    rbrows = rbrows_ref[...]
    acc = jnp.zeros((rows, L), F32)
    for b in range(NUM_BUCKETS):
        acc = jnp.where(bucket == b, rbrows[:, b:b + 1], acc)
    o_ref[...] = acc


def bias_table_sample(rel_bias, past, n_tok, L):
    rows = n_tok * A_HEADS
    rbrows = jnp.tile(rel_bias.T, (n_tok, 1))
    return pl.pallas_call(
        functools.partial(_bias_sample_kernel, past=past, n_tok=n_tok),
        out_shape=jax.ShapeDtypeStruct((rows, L), F32),
        name="bias_table_sample",
    )(rbrows)


def _sortable_key(x):
    b = lax.bitcast_convert_type(x, I32)
    return b ^ ((b >> 31) & 0x7FFFFFFF)


def _topk_member(skey_ref, k_sel):
    R, L = skey_ref.shape

    def body(it, ans):
        bit = 31 - it
        cand = ans | lax.shift_left(jnp.int32(1), bit)
        cand_s = cand ^ INT_MIN
        cnt = jnp.sum(jnp.where(skey_ref[...] >= cand_s, 1.0, 0.0), axis=-1, keepdims=True)
        return jnp.where(cnt >= k_sel, cand, ans)

    ans = lax.fori_loop(0, 32, body, jnp.zeros((R, 1), I32))
    tau = ans ^ INT_MIN
    skey = skey_ref[...]
    gt = skey > tau
    eq = skey == tau
    n_gt = jnp.sum(jnp.where(gt, 1.0, 0.0), axis=-1, keepdims=True)
    room = k_sel - n_gt
    r_i = lax.broadcasted_iota(I32, (LANES, LANES), 0)
    c_i = lax.broadcasted_iota(I32, (LANES, LANES), 1)
    upper = jnp.where(r_i <= c_i, 1.0, 0.0).astype(BF16)
    off = jnp.zeros((R, 1), F32)
    parts = []
    for j in range(L // LANES):
        sl = slice(j * LANES, (j + 1) * LANES)
        eq_j = eq[:, sl]
        run = jnp.dot(jnp.where(eq_j, 1.0, 0.0).astype(BF16), upper, preferred_element_type=F32) + off
        parts.append(gt[:, sl] | (eq_j & (run <= room)))
        off = run[:, LANES - 1:LANES]
    return jnp.concatenate(parts, axis=1)


def _fold8(x, op):
    return op(x.reshape(x.shape[0] // SUBLANES, SUBLANES, x.shape[1]), axis=0)


def _dsa_prompt_kernel(q_ref, qi_ref, misc_ref, kn_ref, vt_ref, bias_ref, qg_ref, o_ref,
                       qst_ref, skey_ref, madd_ref, lg_ref, acc_ref, *, k_sel):
    i = pl.program_id(1)
    nkb = i + 1
    sub = QBLK // TILE
    rep = A_HEADS // A_KV_HEADS
    row0 = pl.multiple_of(i * QBLK, QBLK)
    s_iota = lax.broadcasted_iota(I32, (QBLK, QBLK), 0)
    t_iota = lax.broadcasted_iota(I32, (QBLK, QBLK), 1)

    def admissible(j):
        return (j * QBLK + s_iota) <= (row0 + t_iota)

    wi_t = misc_ref[pl.ds(row0, QBLK), :].T[MISC_WI:MISC_WI + IDX_HEADS, :]
    wi_t = wi_t * (IDX_HEADS ** -0.5 * IDX_DIM ** -0.5)
    for h in range(IDX_HEADS):
        qst_ref[h * QBLK:(h + 1) * QBLK, :] = qi_ref[:, h * IDX_DIM:(h + 1) * IDX_DIM].astype(BF16)

    def score_body(j, carry):
        k0 = pl.multiple_of(j * QBLK, QBLK)
        kj = misc_ref[pl.ds(k0, QBLK), MISC_KI:MISC_KI + IDX_DIM].astype(BF16)
        s = lax.dot_general(kj, qst_ref[...], (((1,), (1,)), ((), ())), preferred_element_type=F32)
        score = jnp.zeros((QBLK, QBLK), F32)
        for h in range(IDX_HEADS):
            score = score + jnp.maximum(s[:, h * QBLK:(h + 1) * QBLK], 0.0) * wi_t[h:h + 1, :]
        skey_ref[j] = _sortable_key(jnp.where(admissible(j), score, -jnp.inf))
        return carry

    lax.fori_loop(0, nkb, score_body, 0)

    def count(pred_fn):
        def body(j, acc):
            return acc + _fold8(jnp.where(pred_fn(skey_ref[j]), 1.0, 0.0), jnp.sum)
        acc = lax.fori_loop(0, nkb, body, jnp.zeros((SUBLANES, QBLK), F32))
        return jnp.sum(acc, axis=0, keepdims=True)

    def bit_body(it, ans):
        cand = ans | lax.shift_left(jnp.int32(1), 31 - it)
        cand_s = cand ^ INT_MIN
        cnt = count(lambda key: key >= cand_s)
        return jnp.where(cnt >= k_sel, cand, ans)

    ans = lax.fori_loop(0, 32, bit_body, jnp.zeros((1, QBLK), I32))
    tau = ans ^ INT_MIN
    n_ge = count(lambda key: key >= tau)
    excess = jnp.max(jnp.where((n_ge > k_sel) & (tau != NEG_INF_KEY), 1.0, 0.0))

    @pl.when(excess == 0.0)
    def _():
        def mask_body(j, carry):
            madd_ref[j] = jnp.where((skey_ref[j] >= tau) & admissible(j), 0.0, NEG_BIG)
            return carry

        lax.fori_loop(0, nkb, mask_body, 0)

    @pl.when(excess > 0.0)
    def _():
        room = k_sel - count(lambda key: key > tau)
        lower = jnp.where(t_iota <= s_iota, 1.0, 0.0).astype(BF16)

        def mask_body(j, off):
            key = skey_ref[j]
            eq = key == tau
            run = jnp.dot(lower, jnp.where(eq, 1.0, 0.0).astype(BF16), preferred_element_type=F32) + off
            sel = ((key > tau) | (eq & (run <= room))) & admissible(j)
            madd_ref[j] = jnp.where(sel, 0.0, NEG_BIG)
            return run[QBLK - 1:QBLK, :]

        lax.fori_loop(0, nkb, mask_body, jnp.zeros((1, QBLK), F32))

    qg = qg_ref[...]
    wide = rep * QBLK
    for g in range(A_KV_HEADS):
        gs = slice(g * HEAD_DIM, (g + 1) * HEAD_DIM)
        heads = list(range(g * rep, (g + 1) * rep))
        q_stack = jnp.concatenate(
            [(_rms(q_ref[:, h * HEAD_DIM:(h + 1) * HEAD_DIM], qg) * HEAD_DIM ** -0.5).astype(BF16) for h in heads],
            axis=0)

        def logit_body(j, mx):
            k0 = pl.multiple_of(j * QBLK, QBLK)
            kj = kn_ref[pl.ds(k0, QBLK), gs].astype(BF16)
            lg = lax.dot_general(kj, q_stack, (((1,), (1,)), ((), ())), preferred_element_type=F32)
            madd = madd_ref[j]
            parts = []
            for r, h in enumerate(heads):
                quads = []
                for c in range(sub):
                    quads.append(jnp.concatenate(
                        [bias_ref[h, jnp.clip(2 - ((i - j) * sub + u - c), 0, 2)] for u in range(sub)], axis=1))
                parts.append(lg[:, r * QBLK:(r + 1) * QBLK] + jnp.concatenate(quads, axis=0) + madd)
            lg = jnp.concatenate(parts, axis=1)
            lg_ref[j] = lg
            return jnp.maximum(mx, _fold8(lg, jnp.max))

        mx = lax.fori_loop(0, nkb, logit_body, jnp.full((SUBLANES, wide), NEG_BIG, F32))
        m = jnp.max(mx, axis=0, keepdims=True)
        acc_ref[...] = jnp.zeros(acc_ref.shape, F32)

        def pv_body(j, sm):
            p = jnp.exp(lg_ref[j] - m)
            acc_ref[...] += jnp.dot(vt_ref[j, gs, :], p.astype(BF16), preferred_element_type=F32)
            return sm + _fold8(p, jnp.sum)

        sm = lax.fori_loop(0, nkb, pv_body, jnp.zeros((SUBLANES, wide), F32))
        den = jnp.sum(sm, axis=0, keepdims=True)
        o = (acc_ref[...] / den).T
        for r, h in enumerate(heads):
            o_ref[:, h * HEAD_DIM:(h + 1) * HEAD_DIM] = o[r * QBLK:(r + 1) * QBLK, :].astype(o_ref.dtype)


def dsa_prompt(proj, kn, vt, bias_tab, q_norm_g, n_batch, seq):
    T = proj.shape[0]
    nb = seq // QBLK
    k_sel = min(TOPK_MAX, seq // 4)
    aw = A_HEADS * HEAD_DIM
    iw = IDX_HEADS * IDX_DIM
    kw = A_KV_HEADS * HEAD_DIM
    rep = A_HEADS // A_KV_HEADS
    return pl.pallas_call(
        functools.partial(_dsa_prompt_kernel, k_sel=k_sel),
        out_shape=jax.ShapeDtypeStruct((T, aw), BF16),
        grid=(n_batch, nb),
        in_specs=[pl.BlockSpec((QBLK, aw), lambda b, i: (b * nb + i, Q_OFF // aw)),
                  pl.BlockSpec((QBLK, iw), lambda b, i: (b * nb + i, QI_OFF // iw)),
                  pl.BlockSpec((seq, LANES), lambda b, i: (b, MISC_OFF // LANES)),
                  pl.BlockSpec((seq, kw), lambda b, i: (b, 0)),
                  pl.BlockSpec((nb, kw, QBLK), lambda b, i: (b, 0, 0)),
                  _resident((A_HEADS, 3, TILE, TILE)),
                  _resident((1, HEAD_DIM))],
        out_specs=pl.BlockSpec((QBLK, aw), lambda b, i: (b * nb + i, 0)),
        scratch_shapes=[pltpu.VMEM((IDX_HEADS * QBLK, IDX_DIM), BF16),
                        pltpu.VMEM((nb, QBLK, QBLK), I32),
                        pltpu.VMEM((nb, QBLK, QBLK), F32),
                        pltpu.VMEM((nb, QBLK, rep * QBLK), F32),
                        pltpu.VMEM((HEAD_DIM, rep * QBLK), F32)],
        compiler_params=_cparams(("parallel", "arbitrary")),
        name="dsa_prompt",
    )(proj, proj, proj, kn, vt, bias_tab, q_norm_g.reshape(1, HEAD_DIM))


def _dsa_sample_select_kernel(pt_ref, *refs, n_pages, n_tok, k_sel, rows_pad):
    del pt_ref
    per = rows_pad // n_tok
    page_refs = refs[:per * n_pages]
    qi_ref, wm_ref, kin_ref, mask_ref, sc_ref, skey_ref = refs[per * n_pages:]
    b = pl.program_id(0)
    nb = pl.num_programs(0)
    L = sc_ref.shape[1]
    past = n_pages * PAGE_SIZE

    relu_s = []
    for e in range(per):
        kt_all = jnp.concatenate([r[...].astype(BF16) for r in page_refs[e * n_pages:(e + 1) * n_pages]]
                                 + [kin_ref[e].astype(BF16)], axis=1)
        relu_s.append(jnp.maximum(jnp.dot(qi_ref[e].astype(BF16), kt_all, preferred_element_type=F32), 0.0))
    score = jnp.dot(wm_ref[0], jnp.concatenate(relu_s, axis=0), precision=HIGHEST,
                    preferred_element_type=F32)
    r0 = pl.multiple_of(b * rows_pad, rows_pad)
    sc_ref[pl.ds(r0, rows_pad), :] = score

    @pl.when(b == nb - 1)
    def _():
        n_blocks = sc_ref.shape[0] // TILE
        n_tiles = L // PAGE_SIZE
        tp = past + lax.broadcasted_iota(I32, (TILE, L), 0) % n_tok
        sp = lax.broadcasted_iota(I32, (TILE, L), 1)
        adm_blk = sp <= tp
        d_r = lax.broadcasted_iota(I32, (PAGE_SIZE, PAGE_SIZE * A_KV_HEADS), 0)
        d_c = lax.broadcasted_iota(I32, (PAGE_SIZE, PAGE_SIZE * A_KV_HEADS), 1)
        dup = jnp.where(d_c // A_KV_HEADS == d_r, 1.0, 0.0).astype(BF16)
        for rb in range(n_blocks):
            rows = slice(rb * TILE, (rb + 1) * TILE)
            skey_ref[...] = _sortable_key(jnp.where(adm_blk, sc_ref[rows, :], -jnp.inf))
            sel = jnp.where(_topk_member(skey_ref, k_sel) & adm_blk, 1.0, 0.0).astype(BF16)
            stacked = jnp.concatenate([sel[:, j * PAGE_SIZE:(j + 1) * PAGE_SIZE] for j in range(n_tiles)], axis=0)
            stacked = jnp.dot(stacked, dup, preferred_element_type=F32)
            mask_ref[rows, :] = jnp.concatenate(
                [stacked[j * TILE:(j + 1) * TILE, :] for j in range(n_tiles)], axis=1)


def dsa_sample_select(cache_ik_t, layer, page_table, qi_rows, wmat, ki_new_t, n_tok, k_sel):
    DB, n_pages = page_table.shape
    rows_pad = wmat.shape[1]
    per = rows_pad // n_tok
    n_rows = DB // per * rows_pad
    L = (n_pages + 1) * PAGE_SIZE
    page_specs = [pl.BlockSpec((None, None, IDX_DIM, PAGE_SIZE), functools.partial(
        lambda b, pt, e, p: (layer, pt[b * per + e, p], 0, 0), e=e, p=p))
        for e in range(per) for p in range(n_pages)]
    grid_spec = pltpu.PrefetchScalarGridSpec(
        num_scalar_prefetch=1,
        grid=(DB // per,),
        in_specs=page_specs + [
            pl.BlockSpec((per,) + qi_rows.shape[1:], lambda b, pt: (b, 0, 0)),
            pl.BlockSpec((1,) + wmat.shape[1:], lambda b, pt: (b, 0, 0)),
            pl.BlockSpec((per, IDX_DIM, PAGE_SIZE), lambda b, pt: (b, 0, 0))],
        out_specs=pl.BlockSpec((n_rows, A_KV_HEADS * L), lambda b, pt: (0, 0)),
        scratch_shapes=[pltpu.VMEM((n_rows, L), F32),
                        pltpu.VMEM((TILE, L), I32)],
    )
    return pl.pallas_call(
        functools.partial(_dsa_sample_select_kernel, n_pages=n_pages, n_tok=n_tok, k_sel=k_sel,
                          rows_pad=rows_pad),
        out_shape=jax.ShapeDtypeStruct((n_rows, A_KV_HEADS * L), F32),
        grid_spec=grid_spec,
        compiler_params=_cparams(("arbitrary",)),
        name="dsa_sample_select",
    )(page_table, *([cache_ik_t] * (per * n_pages)), qi_rows, wmat, ki_new_t)


def _dsa_sample_attend_kernel(pt_ref, *refs, n_pages, n_tok, rows_pad):
    del pt_ref
    per = rows_pad // n_tok
    k_refs = refs[:per * n_pages]
    v_refs = refs[per * n_pages:2 * per * n_pages]
    q_ref, kn_ref, vn_ref, mask_ref, bias_ref, qg_ref, o_ref = refs[2 * per * n_pages:]
    rows = n_tok * A_HEADS
    page_rows = PAGE_SIZE * A_KV_HEADS
    n_cols = mask_ref.shape[1]
    pad = jnp.zeros((page_rows - n_tok * A_KV_HEADS, HEAD_DIM), BF16)
    rep = A_HEADS // A_KV_HEADS
    grp = (lax.broadcasted_iota(I32, (rows, 1), 0) % A_HEADS) // rep
    own_group = (lax.broadcasted_iota(I32, (rows, n_cols), 1) % A_KV_HEADS) == grp
    member = mask_ref[...]

    for e in range(per):
        def tiles(page_refs, new_ref):
            new = jnp.concatenate([new_ref[e].astype(BF16), pad], axis=0)
            return [r[...].astype(BF16) for r in page_refs[e * n_pages:(e + 1) * n_pages]] + [new]

        q = (_rms(q_ref[e], qg_ref[...]) * HEAD_DIM ** -0.5).astype(BF16)
        sel = jnp.concatenate(
            [jnp.broadcast_to(member[e * n_tok + t:e * n_tok + t + 1, :], (A_HEADS, n_cols)) for t in range(n_tok)],
            axis=0)
        valid = (sel > 0.5) & own_group
        logits = jnp.concatenate(
            [lax.dot_general(q, kt, (((1,), (1,)), ((), ())), preferred_element_type=F32)
             for kt in tiles(k_refs, kn_ref)], axis=1)
        logits = jnp.where(valid, logits + bias_ref[...], NEG_BIG)
        m = jnp.max(logits, axis=-1, keepdims=True)
        p = jnp.exp(logits - m)
        den = jnp.sum(p, axis=-1, keepdims=True)
        pb = p.astype(BF16)
        o = jnp.zeros((rows, HEAD_DIM), F32)
        for j, vt in enumerate(tiles(v_refs, vn_ref)):
            o = o + jnp.dot(pb[:, j * page_rows:(j + 1) * page_rows], vt, preferred_element_type=F32)
        o_ref[e] = (o / den).astype(o_ref.dtype)


def dsa_sample_attend(cache_k, cache_v, layer, page_table, q_rows, k_new, v_new, mask, bias_tab, q_norm_g):
    DB, n_pages = page_table.shape
    n_tok = k_new.shape[1] // A_KV_HEADS
    rows = n_tok * A_HEADS
    rows_pad = SUBLANES
    per = rows_pad // n_tok
    n_cols = mask.shape[1]
    page_rows = PAGE_SIZE * A_KV_HEADS
    page_specs = [pl.BlockSpec((None, None, page_rows, HEAD_DIM), functools.partial(
        lambda b, pt, e, p: (layer, pt[b * per + e, p], 0, 0), e=e, p=p))
        for e in range(per) for p in range(n_pages)]
    grid_spec = pltpu.PrefetchScalarGridSpec(
        num_scalar_prefetch=1,
        grid=(DB // per,),
        in_specs=page_specs + page_specs + [
            pl.BlockSpec((per, rows, HEAD_DIM), lambda b, pt: (b, 0, 0)),
            pl.BlockSpec((per, n_tok * A_KV_HEADS, HEAD_DIM), lambda b, pt: (b, 0, 0)),
            pl.BlockSpec((per, n_tok * A_KV_HEADS, HEAD_DIM), lambda b, pt: (b, 0, 0)),
            pl.BlockSpec((rows_pad, n_cols), lambda b, pt: (b, 0)),
            pl.BlockSpec((rows, n_cols), lambda b, pt: (0, 0), pipeline_mode=pl.Buffered(1)),
            pl.BlockSpec((1, HEAD_DIM), lambda b, pt: (0, 0), pipeline_mode=pl.Buffered(1))],
        out_specs=pl.BlockSpec((per, rows, HEAD_DIM), lambda b, pt: (b, 0, 0)),
    )
    return pl.pallas_call(
        functools.partial(_dsa_sample_attend_kernel, n_pages=n_pages, n_tok=n_tok, rows_pad=rows_pad),
        out_shape=jax.ShapeDtypeStruct((DB, rows, HEAD_DIM), BF16),
        grid_spec=grid_spec,
        compiler_params=_cparams(("parallel",)),
        name="dsa_sample_attend",
    )(page_table, *([cache_k] * (per * n_pages)), *([cache_v] * (per * n_pages)), q_rows, k_new, v_new, mask,
      bias_tab, q_norm_g.reshape(1, HEAD_DIM))


def _log_sigmoid(z):
    return jnp.minimum(z, 0.0) - jnp.log(1.0 + jnp.exp(-jnp.abs(z)))


def _seg_masks(seg):
    r = lax.broadcasted_iota(I32, (TILE, TILE), 0)
    c = lax.broadcasted_iota(I32, (TILE, TILE), 1)
    return r, c, (r // seg) == (c // seg)


def _gla_levels(seg):
    w, out = seg // 2, []
    while w >= 1:
        out.append(w)
        w //= 2
    return out


def _gla_sum_matrices(seg):
    r = jnp.arange(TILE)[:, None]
    c = jnp.arange(TILE)[None, :]
    mats = []
    for w in _gla_levels(seg):
        same = (r // (2 * w)) == (c // (2 * w))
        r_right = (r % (2 * w)) >= w
        c_right = (c % (2 * w)) >= w
        mats.append(same & r_right & c_right & (c <= r))
    for w in _gla_levels(seg):
        same = (r // (2 * w)) == (c // (2 * w))
        r_right = (r % (2 * w)) >= w
        c_right = (c % (2 * w)) >= w
        mats.append(same & (~r_right) & (~c_right) & (c > r))
    same_seg = (r // seg) == (c // seg)
    mats.append(same_seg & (c <= r))
    mats.append(same_seg & (c > r))
    return jnp.concatenate(mats, axis=0).astype(BF16)


def _bdot(a, b):
    return jnp.dot(a.astype(BF16), b.astype(BF16), preferred_element_type=F32)


def _bdot_nt(a, b):
    return lax.dot_general(a.astype(BF16), b.astype(BF16), (((1,), (1,)), ((), ())), preferred_element_type=F32)


def _gla_common(tiles, wg_ref, bg_ref, mats_ref, seg):
    n = len(tiles)
    kw = B_HEADS * B_DK
    las = []
    for _, _, misc_ref in tiles:
        gb = misc_ref[:, MISC_GB:MISC_GB + GATE_RANK]
        z = jnp.dot(gb, wg_ref[...], precision=HIGHEST, preferred_element_type=F32) + bg_ref[...]
        las.append(_log_sigmoid(z) / GATE_TEMP)
    la = jnp.concatenate(las, axis=1)
    la_hi = la.astype(BF16)
    la_lo = (la - la_hi.astype(F32)).astype(BF16)
    mats = mats_ref[...]
    sums = (jnp.dot(mats, la_hi, preferred_element_type=F32) + jnp.dot(mats, la_lo, preferred_element_type=F32))
    levels = _gla_levels(seg)
    nl = len(levels)
    qs = [qb_ref[...] * B_DK ** -0.5 for qb_ref, _, _ in tiles]
    ks = [kb_ref[...] for _, kb_ref, _ in tiles]
    r, c, _ = _seg_masks(seg)
    atts = [[jnp.where(r == c, _bdot_nt(qs[i][:, h * B_DK:(h + 1) * B_DK], ks[i][:, h * B_DK:(h + 1) * B_DK]), 0.0)
             for h in range(B_HEADS)] for i in range(n)]
    for li, w in enumerate(levels):
        pair = ((r // (2 * w)) == (c // (2 * w))) & ((r % (2 * w)) >= w) & ((c % (2 * w)) < w)
        qd = [(qs[i] * jnp.exp(sums[li * TILE:(li + 1) * TILE, i * kw:(i + 1) * kw])).astype(BF16) for i in range(n)]
        kd = [(ks[i] * jnp.exp(sums[(nl + li) * TILE:(nl + li + 1) * TILE, i * kw:(i + 1) * kw])).astype(BF16)
              for i in range(n)]
        for h in range(B_HEADS):
            hs = slice(h * B_DK, (h + 1) * B_DK)
            for i in range(n):
                atts[i][h] = atts[i][h] + jnp.where(pair, _bdot_nt(qd[i][:, hs], kd[i][:, hs]), 0.0)
    out = []
    for i in range(n):
        cs = slice(i * kw, (i + 1) * kw)
        b_cum = sums[2 * nl * TILE:(2 * nl + 1) * TILE, cs]
        rem = sums[(2 * nl + 1) * TILE:(2 * nl + 2) * TILE, cs]
        out.append((qs[i], ks[i], atts[i], b_cum, rem))
    return out


def _gla_finish(o_heads, rb_ref, go_ref, o_ref):
    go = go_ref[...]
    for h in range(B_HEADS):
        vs = slice(h * B_DV, (h + 1) * B_DV)
        rb = rb_ref[:, vs]
        o_ref[:, vs] = (_rms(o_heads[h], go) * (rb * jax.nn.sigmoid(rb))).astype(o_ref.dtype)


def _gla_prompt_kernel(qb_ref, kb_ref, vb_ref, rb_ref, misc_ref, wg_ref, bg_ref, go_ref, mats_ref,
                       o_ref, s_ref, state_ref):
    ci = pl.program_id(0)

    @pl.when(ci == 0)
    def _():
        state_ref[...] = jnp.zeros_like(state_ref)

    nb = qb_ref.shape[0]
    common = _gla_common([(qb_ref.at[b], kb_ref.at[b], misc_ref.at[b]) for b in range(nb)],
                         wg_ref, bg_ref, mats_ref, TILE)
    vals = [vb_ref[b] for b in range(nb)]
    states = [state_ref[b] for b in range(nb)]
    qes = [common[b][0] * jnp.exp(common[b][3]) for b in range(nb)]
    o_heads = [[] for _ in range(nb)]
    for h in range(B_HEADS):
        ks = slice(h * B_DK, (h + 1) * B_DK)
        vs = slice(h * B_DV, (h + 1) * B_DV)
        for b in range(nb):
            o_heads[b].append(_bdot(qes[b][:, ks], states[b][ks, :]) + _bdot(common[b][2][h], vals[b][:, vs]))
    for b in range(nb):
        _gla_finish(o_heads[b], rb_ref.at[b], go_ref, o_ref.at[b])

    ke_ts = [(common[b][1] * jnp.exp(common[b][4])).T for b in range(nb)]
    e_cols = [jnp.broadcast_to(jnp.exp(common[b][3][TILE - 1:TILE, :]), (TILE, B_HEADS * B_DK)).T[:, 0:1]
              for b in range(nb)]
    for b in range(nb):
        upd = jnp.concatenate(
            [_bdot(ke_ts[b][h * B_DK:(h + 1) * B_DK, :], vals[b][:, h * B_DV:(h + 1) * B_DV])
             for h in range(B_HEADS)], axis=0)
        new_state = states[b] * e_cols[b] + upd
        state_ref[b] = new_state
        s_ref[b] = new_state


def gla_prompt(proj, w_gate, b_gate, g_out, n_batch, seq):
    nc = seq // TILE
    kwid = B_HEADS * B_DK
    vwid = B_HEADS * B_DV
    mats = _gla_sum_matrices(TILE)
    proj3 = proj.reshape(n_batch, seq, proj.shape[1])
    o, s = pl.pallas_call(
        _gla_prompt_kernel,
        out_shape=(jax.ShapeDtypeStruct((n_batch, seq, vwid), BF16),
                   jax.ShapeDtypeStruct((n_batch, kwid, B_DV), F32)),
        grid=(nc,),
        in_specs=[pl.BlockSpec((n_batch, TILE, kwid), lambda c: (0, c, QB_OFF // kwid)),
                  pl.BlockSpec((n_batch, TILE, kwid), lambda c: (0, c, KB_OFF // kwid)),
                  pl.BlockSpec((n_batch, TILE, vwid), lambda c: (0, c, VB_OFF // vwid)),
                  pl.BlockSpec((n_batch, TILE, vwid), lambda c: (0, c, RB_OFF // vwid)),
                  pl.BlockSpec((n_batch, TILE, LANES), lambda c: (0, c, MISC_OFF // LANES)),
                  _resident((GATE_RANK, kwid)),
                  _resident((1, kwid)),
                  _resident((1, B_DV)),
                  _resident(mats.shape)],
        out_specs=(pl.BlockSpec((n_batch, TILE, vwid), lambda c: (0, c, 0)),
                   pl.BlockSpec((n_batch, kwid, B_DV), lambda c: (0, 0, 0))),
        scratch_shapes=[pltpu.VMEM((n_batch, kwid, B_DV), F32)],
        compiler_params=_cparams(("arbitrary",)),
        name="gla_prompt",
    )(proj3, proj3, proj3, proj3, proj3, w_gate, b_gate.reshape(1, kwid), g_out.reshape(1, B_DV), mats)
    return o.reshape(n_batch * seq, vwid), s.reshape(n_batch, B_HEADS, B_DK, B_DV)


def _gla_sample_kernel(qb_ref, kb_ref, vb_ref, rb_ref, misc_ref, wg_ref, bg_ref, go_ref, mats_ref, s0_ref,
                       o_ref, s_ref, *, seg):
    nbt = TILE // seg
    (q, k, att, b_cum, rem), = _gla_common([(qb_ref, kb_ref, misc_ref)], wg_ref, bg_ref, mats_ref, seg)
    v = vb_ref[...]
    qe = q * jnp.exp(b_cum)
    ke = k * jnp.exp(rem)
    r1 = lax.broadcasted_iota(I32, (TILE, 1), 0)
    e_last = jnp.where(r1 % seg == seg - 1, jnp.exp(b_cum), 0.0)
    wide = nbt * B_DK
    mq = (lax.broadcasted_iota(I32, (TILE, wide), 0) // seg) == (lax.broadcasted_iota(I32, (TILE, wide), 1) // B_DK)
    mk = (lax.broadcasted_iota(I32, (wide, TILE), 0) // B_DK) == (lax.broadcasted_iota(I32, (wide, TILE), 1) // seg)
    o_heads = []
    for h in range(B_HEADS):
        ks = slice(h * B_DK, (h + 1) * B_DK)
        vs = slice(h * B_DV, (h + 1) * B_DV)
        state = s0_ref[:, h].reshape(wide, B_DV)
        q_bd = jnp.where(mq, jnp.concatenate([qe[:, ks]] * nbt, axis=1), 0.0)
        o_heads.append(_bdot(q_bd, state) + _bdot(att[h], v[:, vs]))
        pair_t = jnp.concatenate([ke[:, ks], e_last[:, ks]], axis=1).T
        k_bd = jnp.where(mk, jnp.concatenate([pair_t[:B_DK]] * nbt, axis=0), 0.0)
        e_bd = jnp.where(mk, jnp.concatenate([pair_t[B_DK:]] * nbt, axis=0), 0.0)
        e_col = jnp.sum(e_bd, axis=-1, keepdims=True)
        new_state = state * e_col + _bdot(k_bd, v[:, vs])
        s_ref[:, h] = new_state.reshape(nbt, B_DK, B_DV)
    _gla_finish(o_heads, rb_ref, go_ref, o_ref)


def gla_sample(proj, w_gate, b_gate, g_out, s0, layer, n_tok):
    T = proj.shape[0]
    nbt = TILE // n_tok
    kwid = B_HEADS * B_DK
    vwid = B_HEADS * B_DV
    mats = _gla_sum_matrices(n_tok)
    return pl.pallas_call(
        functools.partial(_gla_sample_kernel, seg=n_tok),
        out_shape=(jax.ShapeDtypeStruct((T, vwid), BF16),
                   jax.ShapeDtypeStruct(s0.shape[1:], F32)),
        grid=(T // TILE,),
        in_specs=[pl.BlockSpec((TILE, kwid), lambda i: (i, QB_OFF // kwid)),
                  pl.BlockSpec((TILE, kwid), lambda i: (i, KB_OFF // kwid)),
                  pl.BlockSpec((TILE, vwid), lambda i: (i, VB_OFF // vwid)),
                  pl.BlockSpec((TILE, vwid), lambda i: (i, RB_OFF // vwid)),
                  pl.BlockSpec((TILE, LANES), lambda i: (i, MISC_OFF // LANES)),
                  _resident((GATE_RANK, kwid)),
                  _resident((1, kwid)),
                  _resident((1, B_DV)),
                  _resident(mats.shape),
                  pl.BlockSpec((None, nbt, B_HEADS, B_DK, B_DV), lambda i: (layer, i, 0, 0, 0))],
        out_specs=(pl.BlockSpec((TILE, vwid), lambda i: (i, 0)),
                   pl.BlockSpec((nbt, B_HEADS, B_DK, B_DV), lambda i: (i, 0, 0, 0))),
        compiler_params=_cparams(("parallel",)),
        name="gla_sample",
    )(proj, proj, proj, proj, proj, w_gate, b_gate.reshape(1, kwid), g_out.reshape(1, B_DV), mats, s0)


def _gelu(x):
    return jax.nn.gelu(x)


def _gmlp_kernel(uc_ref, vc_ref, gv_ref, ws_ref, bcol_ref, o_ref, vn_ref=None, *, seg):
    r, c, same_seg = _seg_masks(seg)
    keep = same_seg & (c <= r)
    for t in range(uc_ref.shape[0] // TILE):
        rows = slice(t * TILE, (t + 1) * TILE)
        u = _gelu(uc_ref[rows, :])
        vg = _gelu(vc_ref[rows, :])
        for g in range(C_GROUPS):
            gs = slice(g * C_GROUP_DIM, (g + 1) * C_GROUP_DIM)
            vn = _rms(vg[:, gs], gv_ref[:, gs])
            if vn_ref is not None:
                vn_ref[rows, gs] = vn
            w = jnp.where(keep, ws_ref[g], 0.0).astype(BF16)
            s = jnp.dot(w, vn.astype(BF16), preferred_element_type=F32) + bcol_ref[:, g:g + 1]
            o_ref[rows, gs] = (u[:, gs] * s).astype(o_ref.dtype)


def gmlp(proj, g_v, w_tiles, b_cols, seg, tm, with_vn):
    T = proj.shape[0]
    tm = min(tm, T)
    cw = C_GROUPS * C_GROUP_DIM
    out_shape = [jax.ShapeDtypeStruct((T, cw), BF16)]
    out_specs = [pl.BlockSpec((tm, cw), lambda i: (i, 0))]
    if with_vn:
        out_shape.append(jax.ShapeDtypeStruct((T, cw), F32))
        out_specs.append(pl.BlockSpec((tm, cw), lambda i: (i, 0)))
    return pl.pallas_call(
        functools.partial(_gmlp_kernel, seg=seg),
        out_shape=tuple(out_shape),
        grid=(T // tm,),
        in_specs=[pl.BlockSpec((tm, cw), lambda i: (i, UC_OFF // cw)),
                  pl.BlockSpec((tm, cw), lambda i: (i, VC_OFF // cw)),
                  _resident((1, cw)),
                  _resident((C_GROUPS, TILE, TILE)),
                  _resident((TILE, C_GROUPS))],
        out_specs=tuple(out_specs),
        compiler_params=_cparams(("parallel",)),
        name="gmlp",
    )(proj, proj, g_v.reshape(1, cw), w_tiles, b_cols)


def _ffn_kernel(h_ref, oa_ref, ob_ref, oc_ref, wo_ref, g_ref, wg_ref, wu_ref, wd_ref, o_ref, n_ref):
    j = pl.program_id(1)

    @pl.when(j == 0)
    def _():
        aw = oa_ref.shape[1]
        bw = ob_ref.shape[1]
        h = h_ref[...] + jnp.dot(oa_ref[...], wo_ref[0:aw, :], preferred_element_type=F32)
        h = h + jnp.dot(ob_ref[...], wo_ref[aw:aw + bw, :], preferred_element_type=F32)
        h = h + jnp.dot(oc_ref[...], wo_ref[aw + bw:, :], preferred_element_type=F32)
        n_ref[...] = _rms(h, g_ref[...]).astype(BF16)
        o_ref[...] = h

    n = n_ref[...]
    a = jnp.dot(n, wg_ref[...], preferred_element_type=F32)
    u = jnp.dot(n, wu_ref[...], preferred_element_type=F32)
    act = (a * jax.nn.sigmoid(a) * u).astype(BF16)
    o_ref[...] += jnp.dot(act, wd_ref[...], preferred_element_type=F32)


def out_proj_ffn(h, o_a, o_b, o_c, w_out, g, w_gate, w_up, w_down, tm, tf):
    T, D = h.shape
    tm = min(tm, T)
    FF = w_gate.shape[1]
    return pl.pallas_call(
        _ffn_kernel,
        out_shape=jax.ShapeDtypeStruct((T, D), F32),
        grid=(T // tm, FF // tf),
        in_specs=[pl.BlockSpec((tm, D), lambda i, j: (i, 0)),
                  pl.BlockSpec((tm, o_a.shape[1]), lambda i, j: (i, 0)),
                  pl.BlockSpec((tm, o_b.shape[1]), lambda i, j: (i, 0)),
                  pl.BlockSpec((tm, o_c.shape[1]), lambda i, j: (i, 0)),
                  _resident(w_out.shape),
                  _resident((1, D)),
                  pl.BlockSpec((D, tf), lambda i, j: (0, j)),
                  pl.BlockSpec((D, tf), lambda i, j: (0, j)),
                  pl.BlockSpec((tf, D), lambda i, j: (j, 0))],
        out_specs=pl.BlockSpec((tm, D), lambda i, j: (i, 0)),
        scratch_shapes=[pltpu.VMEM((tm, D), BF16)],
        compiler_params=_cparams(("parallel", "arbitrary")),
        name="out_proj_ffn",
    )(h, o_a, o_b, o_c, w_out, g.reshape(1, D), w_gate, w_up, w_down)


def _ple_kernel(h_ref, p_ref, g_ref, wgate_ref, wproj_ref, o_ref):
    h = h_ref[...]
    n = _rms(h, g_ref[...]).astype(BF16)
    gate = jax.nn.sigmoid(jnp.dot(n, wgate_ref[...], preferred_element_type=F32))
    emb = jnp.dot(p_ref[...].astype(BF16), wproj_ref[...], preferred_element_type=F32)
    o_ref[...] = h + gate * emb


def ple(h, p, g, w_gate, w_proj, layer, tm):
    T, D = h.shape
    tm = min(tm, T)
    P = p.shape[2]
    return pl.pallas_call(
        _ple_kernel,
        out_shape=jax.ShapeDtypeStruct((T, D), F32),
        grid=(T // tm,),
        in_specs=[pl.BlockSpec((tm, D), lambda i: (i, 0)),
                  pl.BlockSpec((None, tm, P), lambda i: (layer, i, 0)),
                  _resident((1, D)),
                  _resident(w_gate.shape),
                  _resident(w_proj.shape)],
        out_specs=pl.BlockSpec((tm, D), lambda i: (i, 0)),
        compiler_params=_cparams(("parallel",)),
        name="ple",
    )(h, p, g.reshape(1, D), w_gate, w_proj)


_W_IN_SEGMENTS = (("q", 1024), ("k", 256), ("v", 256), ("qi", 1024), ("ki", 64), ("wi", 16), ("qb", 256),
                  ("kb", 256), ("vb", 512), ("gb", 16), ("rb", 512), ("uc", 512), ("vc", 512))
_W_IN_PACKED_ORDER = ("q", "qi", "vb", "rb", "uc", "vc", "k", "v", "qb", "kb", "ki", "wi", "gb")


def _pack_kernel(wt_ref, o_ref):
    src, start = {}, 0
    for name, size in _W_IN_SEGMENTS:
        src[name] = (start, size)
        start += size
    dst = 0
    small = []
    for name in _W_IN_PACKED_ORDER:
        s0, size = src[name]
        if size < LANES:
            small.append(wt_ref[s0:s0 + size, :])
            continue
        o_ref[:, dst:dst + size] = wt_ref[s0:s0 + size, :].T.astype(BF16)
        dst += size
    used = sum(x.shape[0] for x in small)
    small.append(jnp.zeros((LANES - used, wt_ref.shape[1]), F32))
    o_ref[:, dst:dst + LANES] = jnp.concatenate(small, axis=0).T.astype(BF16)


def _pack_w_in(w, tr=256):
    depth, D, N = w.shape
    return pl.pallas_call(
        _pack_kernel,
        out_shape=jax.ShapeDtypeStruct((depth, D, PROJ_PACKED), BF16),
        grid=(depth, D // tr),
        in_specs=[pl.BlockSpec((None, N, tr), lambda l, i: (l, 0, i))],
        out_specs=pl.BlockSpec((None, tr, PROJ_PACKED), lambda l, i: (l, i, 0)),
        compiler_params=_cparams(("parallel", "parallel")),
        name="pack_w_in",
    )(jnp.swapaxes(w, 1, 2))


def _mixer_tail(h, o_a, o_b, o_c, p_all, lw, layer, tm, tm_ffn):
    h = out_proj_ffn(h, o_a, o_b, o_c, lw["w_out"], lw["g_ffn"], lw["w_ffn_gate"], lw["w_ffn_up"],
                     lw["w_ffn_down"], tm_ffn, 512)
    return ple(h, p_all, lw["g_ple"], lw["w_ple_gate"], lw["w_ple_proj"], layer, tm)


def kernel(x_prompt, x_sample, cache_k, cache_v, cache_idx_k, state_gla, page_table, p_prompt, p_sample,
           g_mix, w_in, q_norm_g, k_norm_g, rel_bias, w_gate_b, b_gate_b, g_out_b, g_v_c, w_spatial,
           b_spatial, w_out, g_ffn, w_ffn_gate, w_ffn_up, w_ffn_down, g_ple, w_ple_gate, w_ple_proj):
    n_batch, seq, d_model = x_prompt.shape
    dec_batch, dec_seq, _ = x_sample.shape
    depth = w_in.shape[0]
    n_pages = page_table.shape[1]
    past = n_pages * PAGE_SIZE
    kw = A_KV_HEADS * HEAD_DIM
    tp, ts = n_batch * seq, dec_batch * dec_seq
    rows_pad = SUBLANES
    l_sample = past + PAGE_SIZE
    k_sel_s = min(TOPK_MAX, (past + dec_seq) // 4)

    bias_p = bias_table_prompt(rel_bias)
    bias_s = bias_table_sample(rel_bias, past, dec_seq, A_KV_HEADS * l_sample)
    cache_ik_t = jnp.swapaxes(cache_idx_k, 2, 3)
    cache_k2 = cache_k.reshape(depth, cache_k.shape[1], PAGE_SIZE * A_KV_HEADS, HEAD_DIM)
    cache_v2 = cache_v.reshape(depth, cache_v.shape[1], PAGE_SIZE * A_KV_HEADS, HEAD_DIM)

    hp = x_prompt.reshape(tp, d_model)
    hs = x_sample.reshape(ts, d_model)
    outs = {k: [] for k in ("kp", "vp", "ikp", "sp", "ks", "vs", "iks", "ss", "cs")}
    per_s = rows_pad // dec_seq
    place_t = (jnp.arange(rows_pad)[:, None, None]
               == jnp.arange(per_s)[None, :, None] * dec_seq + jnp.arange(dec_seq)[None, None, :]
               ).astype(F32)
    w_ple_proj_b = w_ple_proj.astype(BF16)
    w_packed = _pack_w_in(w_in)
    pp_all = p_prompt.reshape(depth, tp, -1)
    ps_all = p_sample.reshape(depth, ts, -1)
    for i in range(depth):
        b_cols_p = b_spatial[i].T
        reps = TILE // dec_seq
        w_tiles_s = jnp.tile(w_spatial[i][:, :dec_seq, :dec_seq], (1, reps, reps))
        b_cols_s = jnp.tile(b_spatial[i][:, :dec_seq].T, (reps, 1))

        proj, (wo_b, wg_b, wu_b, wd_b, wpg_b) = in_projection(
            hp, g_mix[i], w_packed, i, 256, cast=(w_out, w_ffn_gate, w_ffn_up, w_ffn_down, w_ple_gate))
        lw = dict(w_out=wo_b, g_ffn=g_ffn[i], w_ffn_gate=wg_b, w_ffn_up=wu_b, w_ffn_down=wd_b,
                  g_ple=g_ple[i], w_ple_gate=wpg_b, w_ple_proj=w_ple_proj_b[i])
        kn, vv, ik, vt = kv_post(proj, k_norm_g[i], 512, True)
        o_a = dsa_prompt(proj, kn, vt, bias_p, q_norm_g[i], n_batch, seq)
        o_b, s_p = gla_prompt(proj, w_gate_b[i], b_gate_b[i], g_out_b[i], n_batch, seq)
        o_c, = gmlp(proj, g_v_c[i], w_spatial[i], b_cols_p, TILE, 512, False)
        hp = _mixer_tail(hp, o_a, o_b, o_c, pp_all, lw, i, 256, 512)
        outs["kp"].append(kn.reshape(n_batch, seq, A_KV_HEADS, HEAD_DIM))
        outs["vp"].append(vv.reshape(n_batch, seq, A_KV_HEADS, HEAD_DIM))
        outs["ikp"].append(ik.reshape(n_batch, seq, IDX_DIM))
        outs["sp"].append(s_p)

        proj, _ = in_projection(hs, g_mix[i], w_packed, i, 256)
        kn, vv, ik = kv_post(proj, k_norm_g[i], 512, False)
        qi_rows =proj[:, QI_OFF:QI_OFF + IDX_HEADS * IDX_DIM].reshape(dec_batch, dec_seq * IDX_HEADS, IDX_DIM)
        wi = proj[:, MISC_OFF + MISC_WI:MISC_OFF + MISC_WI + IDX_HEADS].reshape(dec_batch, dec_seq, IDX_HEADS)
        wi = wi * (IDX_HEADS ** -0.5 * IDX_DIM ** -0.5)
        wmat = (place_t[None, :, :, :, None] * wi.reshape(dec_batch // per_s, 1, per_s, dec_seq, IDX_HEADS)
                ).reshape(dec_batch // per_s, rows_pad, per_s * dec_seq * IDX_HEADS)
        ki_new_t = jnp.pad(jnp.swapaxes(ik.reshape(dec_batch, dec_seq, IDX_DIM), 1, 2),
                           ((0, 0), (0, 0), (0, PAGE_SIZE - dec_seq)))
        mask = dsa_sample_select(cache_ik_t, i, page_table, qi_rows, wmat, ki_new_t, dec_seq, k_sel_s)
        q_rows = proj[:, Q_OFF:Q_OFF + A_HEADS * HEAD_DIM].reshape(dec_batch, dec_seq * A_HEADS, HEAD_DIM)
        o_a = dsa_sample_attend(cache_k2, cache_v2, i, page_table, q_rows,
                                kn.reshape(dec_batch, dec_seq * A_KV_HEADS, HEAD_DIM),
                                vv.reshape(dec_batch, dec_seq * A_KV_HEADS, HEAD_DIM), mask, bias_s, q_norm_g[i])
        o_a = o_a.reshape(ts, A_HEADS * HEAD_DIM)
        o_b, s_s = gla_sample(proj, w_gate_b[i], b_gate_b[i], g_out_b[i], state_gla, i, dec_seq)
        o_c, vn = gmlp(proj, g_v_c[i], w_tiles_s, b_cols_s, dec_seq, 256, True)
        hs = _mixer_tail(hs, o_a, o_b, o_c, ps_all, lw, i, 256, 512)
        outs["ks"].append(kn.reshape(dec_batch, dec_seq, A_KV_HEADS, HEAD_DIM))
        outs["vs"].append(vv.reshape(dec_batch, dec_seq, A_KV_HEADS, HEAD_DIM))
        outs["iks"].append(ik.reshape(dec_batch, dec_seq, IDX_DIM))
        outs["ss"].append(s_s)
        outs["cs"].append(vn.reshape(dec_batch, dec_seq, -1))

    st = {k: jnp.stack(v) for k, v in outs.items()}
    return (hp.reshape(n_batch, seq, d_model), hs.reshape(dec_batch, dec_seq, d_model),
            st["kp"], st["vp"], st["ikp"], st["sp"], st["ks"], st["vs"], st["iks"], st["ss"], st["cs"])
```

```python
import functools
import math

import jax
import jax.numpy as jnp
from jax import lax
from jax.experimental import pallas as pl
from jax.experimental.pallas import tpu as pltpu

F32 = jnp.float32
BF16 = jnp.bfloat16
I32 = jnp.int32
HIGHEST = lax.Precision.HIGHEST

LANES = 128
SUBLANES = 8
VMEM_LIMIT = 56 * 1024 * 1024

HEAD_DIM = 128
A_HEADS = 8
A_KV_HEADS = 2
IDX_HEADS = 16
IDX_DIM = 64
TOPK_MAX = 256
NUM_BUCKETS = 32
MAX_DISTANCE = 128
B_HEADS = 4
B_DK = 64
B_DV = 128
GATE_RANK = 16
GATE_TEMP = 16.0
C_GROUPS = 4
C_GROUP_DIM = 128
PAGE_SIZE = 128
EPS = 1e-6
NEG_BIG = -1e30
INT_MIN = -(2 ** 31)
NEG_INF_KEY = -2139095041

TILE = 128
QBLK = 256
TM_PROJ = 256
TM_KV = 1024
TM_GMLP = 512
TM_FFN = 512
TF_FFN = 512
TM_PLE = 512

Q_OFF, QI_OFF, VB_OFF, RB_OFF, UC_OFF, VC_OFF = 0, 1024, 2048, 2560, 3072, 3584
K_OFF, V_OFF, QB_OFF, KB_OFF, MISC_OFF = 4096, 4352, 4608, 4864, 5120
PROJ_PACKED = 5248
MISC_KI, MISC_WI, MISC_GB = 0, 64, 80


def _cparams(sem):
    return pltpu.CompilerParams(dimension_semantics=sem, vmem_limit_bytes=VMEM_LIMIT)


def _rms(x, g):
    return x * lax.rsqrt(jnp.mean(x * x, axis=-1, keepdims=True) + EPS) * g


def _resident(shape):
    nd = len(shape)
    return pl.BlockSpec(shape, lambda *_: (0,) * nd, pipeline_mode=pl.Buffered(1))


def _layer_resident(shape, layer):
    nd = len(shape)
    return pl.BlockSpec((None,) + tuple(shape), lambda *_: (layer,) + (0,) * nd, pipeline_mode=pl.Buffered(1))


def _proj_kernel(x_ref, g_ref, w_ref, *refs, n_cast):
    cast_in, o_ref, cast_out = refs[:n_cast], refs[n_cast], refs[n_cast + 1:]
    n = _rms(x_ref[...], g_ref[...]).astype(BF16)
    ncol = o_ref.shape[1]
    step = 512
    for c0 in range(0, ncol, step):
        c1 = min(c0 + step, ncol)
        o_ref[:, c0:c1] = jnp.dot(n, w_ref[:, c0:c1], preferred_element_type=F32)
    for src, dst in zip(cast_in, cast_out):
        dst[...] = src[...].astype(dst.dtype)


def in_projection(h, g, w_packed, layer, tm, cast=()):
    T, D = h.shape
    tm = min(tm, T)
    N = w_packed.shape[2]
    steps = T // tm
    cast_specs_in = [pl.BlockSpec((None, w.shape[1] // steps, w.shape[2]), lambda i: (layer, i, 0)) for w in cast]
    cast_specs_out = [pl.BlockSpec((w.shape[1] // steps, w.shape[2]), lambda i: (i, 0)) for w in cast]
    outs = pl.pallas_call(
        functools.partial(_proj_kernel, n_cast=len(cast)),
        out_shape=(jax.ShapeDtypeStruct((T, N), F32),) + tuple(
            jax.ShapeDtypeStruct(w.shape[1:], BF16) for w in cast),
        grid=(steps,),
        in_specs=[pl.BlockSpec((tm, D), lambda i: (i, 0)),
                  _resident((1, D)),
                  _layer_resident((D, N), layer)] + cast_specs_in,
        out_specs=(pl.BlockSpec((tm, N), lambda i: (i, 0)),) + tuple(cast_specs_out),
        compiler_params=_cparams(("parallel",)),
        name="in_projection",
    )(h, g.reshape(1, D), w_packed, *cast)
    return outs[0], outs[1:]


def _kv_kernel(k_ref, v_ref, m_ref, g_ref, ko_ref, vo_ref, io_ref, vt_ref=None):
    g = g_ref[...]
    k = k_ref[...]
    for hh in range(A_KV_HEADS):
        sl = slice(hh * HEAD_DIM, (hh + 1) * HEAD_DIM)
        ko_ref[:, sl] = _rms(k[:, sl], g)
    v = v_ref[...]
    vo_ref[...] = v
    io_ref[...] = m_ref[:, MISC_KI:MISC_KI + IDX_DIM]
    if vt_ref is not None:
        for blk in range(vt_ref.shape[0]):
            vt_ref[blk] = v[blk * QBLK:(blk + 1) * QBLK, :].T.astype(vt_ref.dtype)


def kv_post(proj, k_norm_g, tm, with_vt):
    T = proj.shape[0]
    tm = min(tm, T)
    kw = A_KV_HEADS * HEAD_DIM
    out_shape = [jax.ShapeDtypeStruct((T, kw), F32),
                 jax.ShapeDtypeStruct((T, kw), F32),
                 jax.ShapeDtypeStruct((T, IDX_DIM), F32)]
    out_specs = [pl.BlockSpec((tm, kw), lambda i: (i, 0)),
                 pl.BlockSpec((tm, kw), lambda i: (i, 0)),
                 pl.BlockSpec((tm, IDX_DIM), lambda i: (i, 0))]
    if with_vt:
        out_shape.append(jax.ShapeDtypeStruct((T // QBLK, kw, QBLK), BF16))
        out_specs.append(pl.BlockSpec((tm // QBLK, kw, QBLK), lambda i: (i, 0, 0)))
    return pl.pallas_call(
        _kv_kernel,
        out_shape=tuple(out_shape),
        grid=(T // tm,),
        in_specs=[pl.BlockSpec((tm, kw), lambda i: (i, K_OFF // kw)),
                  pl.BlockSpec((tm, kw), lambda i: (i, V_OFF // kw)),
                  pl.BlockSpec((tm, LANES), lambda i: (i, MISC_OFF // LANES)),
                  _resident((1, HEAD_DIM))],
        out_specs=tuple(out_specs),
        compiler_params=_cparams(("parallel",)),
        name="kv_post",
    )(proj, proj, proj, k_norm_g.reshape(1, HEAD_DIM))


def _bucket(dist):
    n = jnp.maximum(dist, 0)
    max_exact = NUM_BUCKETS // 2
    large = max_exact + (jnp.log(jnp.maximum(n, 1).astype(F32) / max_exact)
                         / math.log(MAX_DISTANCE / max_exact)
                         * (NUM_BUCKETS - max_exact)).astype(I32)
    large = jnp.minimum(large, NUM_BUCKETS - 1)
    return jnp.where(n < max_exact, n, large)


def _bias_prompt_kernel(rb_ref, o_ref):
    c = lax.broadcasted_iota(I32, (TILE, TILE), 0)
    t = lax.broadcasted_iota(I32, (TILE, TILE), 1)
    for z in range(3):
        bucket = _bucket(t - c + (2 - z) * TILE)
        for h in range(A_HEADS):
            acc = jnp.zeros((TILE, TILE), F32)
            for b in range(NUM_BUCKETS):
                acc = jnp.where(bucket == b, rb_ref[b, h], acc)
            o_ref[h, z] = acc


def bias_table_prompt(rel_bias):
    return pl.pallas_call(
        _bias_prompt_kernel,
        out_shape=jax.ShapeDtypeStruct((A_HEADS, 3, TILE, TILE), F32),
        in_specs=[pl.BlockSpec(memory_space=pltpu.SMEM)],
        out_specs=pl.BlockSpec(memory_space=pltpu.VMEM),
        name="bias_table_prompt",
    )(rel_bias)


def _bias_sample_kernel(rbrows_ref, o_ref, *, past, n_tok):
    rows, L = o_ref.shape
    r = lax.broadcasted_iota(I32, (rows, L), 0)
    s = lax.broadcasted_iota(I32, (rows, L), 1) // A_KV_HEADS
    bucket = _bucket(past + r // A_HEADS - s)
    rbrows = rbrows_ref[...]
    acc = jnp.zeros((rows, L), F32)
    for b in range(NUM_BUCKETS):
        acc = jnp.where(bucket == b, rbrows[:, b:b + 1], acc)
    o_ref[...] = acc


def bias_table_sample(rel_bias, past, n_tok, L):
    rows = n_tok * A_HEADS
    rbrows = jnp.tile(rel_bias.T, (n_tok, 1))
    return pl.pallas_call(
        functools.partial(_bias_sample_kernel, past=past, n_tok=n_tok),
        out_shape=jax.ShapeDtypeStruct((rows, L), F32),
        name="bias_table_sample",
    )(rbrows)


def _sortable_key(x):
    b = lax.bitcast_convert_type(x, I32)
    return b ^ ((b >> 31) & 0x7FFFFFFF)


def _topk_member(skey_ref, k_sel):
    R, L = skey_ref.shape

    def body(it, ans):
        bit = 31 - it
        cand = ans | lax.shift_left(jnp.int32(1), bit)
        cand_s = cand ^ INT_MIN
        cnt = jnp.sum(jnp.where(skey_ref[...] >= cand_s, 1.0, 0.0), axis=-1, keepdims=True)
        return jnp.where(cnt >= k_sel, cand, ans)

    ans = lax.fori_loop(0, 32, body, jnp.zeros((R, 1), I32))
    tau = ans ^ INT_MIN
    skey = skey_ref[...]
    gt = skey > tau
    eq = skey == tau
    n_gt = jnp.sum(jnp.where(gt, 1.0, 0.0), axis=-1, keepdims=True)
    room = k_sel - n_gt
    r_i = lax.broadcasted_iota(I32, (LANES, LANES), 0)
    c_i = lax.broadcasted_iota(I32, (LANES, LANES), 1)
    upper = jnp.where(r_i <= c_i, 1.0, 0.0).astype(BF16)
    off = jnp.zeros((R, 1), F32)
    parts = []
    for j in range(L // LANES):
        sl = slice(j * LANES, (j + 1) * LANES)
        eq_j = eq[:, sl]
        run = jnp.dot(jnp.where(eq_j, 1.0, 0.0).astype(BF16), upper, preferred_element_type=F32) + off
        parts.append(gt[:, sl] | (eq_j & (run <= room)))
        off = run[:, LANES - 1:LANES]
    return jnp.concatenate(parts, axis=1)


def _fold8(x, op):
    return op(x.reshape(x.shape[0] // SUBLANES, SUBLANES, x.shape[1]), axis=0)


def _dsa_prompt_kernel(q_ref, qi_ref, misc_ref, kn_ref, vt_ref, bias_ref, qg_ref, o_ref,
                       qst_ref, skey_ref, madd_ref, lg_ref, acc_ref, *, k_sel):
    i = pl.program_id(1)
    nkb = i + 1
    sub = QBLK // TILE
    rep = A_HEADS // A_KV_HEADS
    row0 = pl.multiple_of(i * QBLK, QBLK)
    s_iota = lax.broadcasted_iota(I32, (QBLK, QBLK), 0)
    t_iota = lax.broadcasted_iota(I32, (QBLK, QBLK), 1)

    def admissible(j):
        return (j * QBLK + s_iota) <= (row0 + t_iota)

    wi_t = misc_ref[pl.ds(row0, QBLK), :].T[MISC_WI:MISC_WI + IDX_HEADS, :]
    wi_t = wi_t * (IDX_HEADS ** -0.5 * IDX_DIM ** -0.5)
    for h in range(IDX_HEADS):
        qst_ref[h * QBLK:(h + 1) * QBLK, :] = qi_ref[:, h * IDX_DIM:(h + 1) * IDX_DIM].astype(BF16)

    def score_body(j, carry):
        k0 = pl.multiple_of(j * QBLK, QBLK)
        kj = misc_ref[pl.ds(k0, QBLK), MISC_KI:MISC_KI + IDX_DIM].astype(BF16)
        s = lax.dot_general(kj, qst_ref[...], (((1,), (1,)), ((), ())), preferred_element_type=F32)
        score = jnp.zeros((QBLK, QBLK), F32)
        for h in range(IDX_HEADS):
            score = score + jnp.maximum(s[:, h * QBLK:(h + 1) * QBLK], 0.0) * wi_t[h:h + 1, :]
        skey_ref[j] = _sortable_key(jnp.where(admissible(j), score, -jnp.inf))
        return carry

    lax.fori_loop(0, nkb, score_body, 0)

    def count(pred_fn):
        def hits(j):
            return _fold8(jnp.where(pred_fn(skey_ref[j]), 1.0, 0.0), jnp.sum)

        def body(jj, accs):
            return accs[0] + hits(2 * jj), accs[1] + hits(2 * jj + 1)

        zero = jnp.zeros((SUBLANES, QBLK), F32)
        acc0, acc1 = lax.fori_loop(0, nkb // 2, body, (zero, zero))
        acc = lax.cond(nkb % 2 == 1, lambda: acc0 + acc1 + hits(nkb - 1), lambda: acc0 + acc1)
        return jnp.sum(acc, axis=0, keepdims=True)

    def bit_body(it, ans):
        cand = ans | lax.shift_left(jnp.int32(1), 31 - it)
        cand_s = cand ^ INT_MIN
        cnt = count(lambda key: key >= cand_s)
        return jnp.where(cnt >= k_sel, cand, ans)

    ans = lax.fori_loop(0, 32, bit_body, jnp.zeros((1, QBLK), I32))
    tau = ans ^ INT_MIN
    n_ge = count(lambda key: key >= tau)
    excess = jnp.max(jnp.where((n_ge > k_sel) & (tau != NEG_INF_KEY), 1.0, 0.0))

    @pl.when(excess == 0.0)
    def _():
        def mask_body(j, carry):
            madd_ref[j] = jnp.where((skey_ref[j] >= tau) & admissible(j), 0.0, NEG_BIG)
            return carry

        lax.fori_loop(0, nkb, mask_body, 0)

    @pl.when(excess > 0.0)
    def _():
        room = k_sel - count(lambda key: key > tau)
        lower = jnp.where(t_iota <= s_iota, 1.0, 0.0).astype(BF16)

        def mask_body(j, off):
            key = skey_ref[j]
            eq = key == tau
            run = jnp.dot(lower, jnp.where(eq, 1.0, 0.0).astype(BF16), preferred_element_type=F32) + off
            sel = ((key > tau) | (eq & (run <= room))) & admissible(j)
            madd_ref[j] = jnp.where(sel, 0.0, NEG_BIG)
            return run[QBLK - 1:QBLK, :]

        lax.fori_loop(0, nkb, mask_body, jnp.zeros((1, QBLK), F32))

    qg = qg_ref[...]
    wide = rep * QBLK
    for g in range(A_KV_HEADS):
        gs = slice(g * HEAD_DIM, (g + 1) * HEAD_DIM)
        heads = list(range(g * rep, (g + 1) * rep))
        q_stack = jnp.concatenate(
            [(_rms(q_ref[:, h * HEAD_DIM:(h + 1) * HEAD_DIM], qg) * HEAD_DIM ** -0.5).astype(BF16) for h in heads],
            axis=0)

        def logit_body(j, mx):
            k0 = pl.multiple_of(j * QBLK, QBLK)
            kj = kn_ref[pl.ds(k0, QBLK), gs].astype(BF16)
            lg = lax.dot_general(kj, q_stack, (((1,), (1,)), ((), ())), preferred_element_type=F32)
            madd = madd_ref[j]
            parts = []
            for r, h in enumerate(heads):
                quads = []
                for c in range(sub):
                    quads.append(jnp.concatenate(
                        [bias_ref[h, jnp.clip(2 - ((i - j) * sub + u - c), 0, 2)] for u in range(sub)], axis=1))
                parts.append(lg[:, r * QBLK:(r + 1) * QBLK] + jnp.concatenate(quads, axis=0) + madd)
            lg = jnp.concatenate(parts, axis=1)
            lg_ref[j] = lg
            return jnp.maximum(mx, _fold8(lg, jnp.max))

        mx = lax.fori_loop(0, nkb, logit_body, jnp.full((SUBLANES, wide), NEG_BIG, F32))
        m = jnp.max(mx, axis=0, keepdims=True)
        acc_ref[...] = jnp.zeros(acc_ref.shape, F32)

        def pv_body(j, sm):
            p = jnp.exp(lg_ref[j] - m)
            acc_ref[...] += jnp.dot(vt_ref[j, gs, :], p.astype(BF16), preferred_element_type=F32)
            return sm + _fold8(p, jnp.sum)

        sm = lax.fori_loop(0, nkb, pv_body, jnp.zeros((SUBLANES, wide), F32))
        den = jnp.sum(sm, axis=0, keepdims=True)
        o = (acc_ref[...] / den).T
        for r, h in enumerate(heads):
            o_ref[:, h * HEAD_DIM:(h + 1) * HEAD_DIM] = o[r * QBLK:(r + 1) * QBLK, :].astype(o_ref.dtype)


def dsa_prompt(proj, kn, vt, bias_tab, q_norm_g, n_batch, seq):
    T = proj.shape[0]
    nb = seq // QBLK
    k_sel = min(TOPK_MAX, seq // 4)
    aw = A_HEADS * HEAD_DIM
    iw = IDX_HEADS * IDX_DIM
    kw = A_KV_HEADS * HEAD_DIM
    rep = A_HEADS // A_KV_HEADS
    return pl.pallas_call(
        functools.partial(_dsa_prompt_kernel, k_sel=k_sel),
        out_shape=jax.ShapeDtypeStruct((T, aw), BF16),
        grid=(n_batch, nb),
        in_specs=[pl.BlockSpec((QBLK, aw), lambda b, i: (b * nb + i, Q_OFF // aw)),
                  pl.BlockSpec((QBLK, iw), lambda b, i: (b * nb + i, QI_OFF // iw)),
                  pl.BlockSpec((seq, LANES), lambda b, i: (b, MISC_OFF // LANES)),
                  pl.BlockSpec((seq, kw), lambda b, i: (b, 0)),
                  pl.BlockSpec((nb, kw, QBLK), lambda b, i: (b, 0, 0)),
                  _resident((A_HEADS, 3, TILE, TILE)),
                  _resident((1, HEAD_DIM))],
        out_specs=pl.BlockSpec((QBLK, aw), lambda b, i: (b * nb + i, 0)),
        scratch_shapes=[pltpu.VMEM((IDX_HEADS * QBLK, IDX_DIM), BF16),
                        pltpu.VMEM((nb, QBLK, QBLK), I32),
                        pltpu.VMEM((nb, QBLK, QBLK), F32),
                        pltpu.VMEM((nb, QBLK, rep * QBLK), F32),
                        pltpu.VMEM((HEAD_DIM, rep * QBLK), F32)],
        compiler_params=_cparams(("parallel", "arbitrary")),
        name="dsa_prompt",
    )(proj, proj, proj, kn, vt, bias_tab, q_norm_g.reshape(1, HEAD_DIM))


def _dsa_sample_select_kernel(pt_ref, *refs, n_pages, n_tok, k_sel, rows_pad):
    del pt_ref
    per = rows_pad // n_tok
    page_refs = refs[:per * n_pages]
    qi_ref, wm_ref, kin_ref, mask_ref, sc_ref, skey_ref = refs[per * n_pages:]
    b = pl.program_id(0)
    nb = pl.num_programs(0)
    L = sc_ref.shape[1]
    past = n_pages * PAGE_SIZE

    relu_s = []
    for e in range(per):
        kt_all = jnp.concatenate([r[...].astype(BF16) for r in page_refs[e * n_pages:(e + 1) * n_pages]]
                                 + [kin_ref[e].astype(BF16)], axis=1)
        relu_s.append(jnp.maximum(jnp.dot(qi_ref[e].astype(BF16), kt_all, preferred_element_type=F32), 0.0))
    score = jnp.dot(wm_ref[0], jnp.concatenate(relu_s, axis=0), precision=HIGHEST,
                    preferred_element_type=F32)
    r0 = pl.multiple_of(b * rows_pad, rows_pad)
    sc_ref[pl.ds(r0, rows_pad), :] = score

    @pl.when(b == nb - 1)
    def _():
        n_blocks = sc_ref.shape[0] // TILE
        n_tiles = L // PAGE_SIZE
        tp = past + lax.broadcasted_iota(I32, (TILE, L), 0) % n_tok
        sp = lax.broadcasted_iota(I32, (TILE, L), 1)
        adm_blk = sp <= tp
        d_r = lax.broadcasted_iota(I32, (PAGE_SIZE, PAGE_SIZE * A_KV_HEADS), 0)
        d_c = lax.broadcasted_iota(I32, (PAGE_SIZE, PAGE_SIZE * A_KV_HEADS), 1)
        dup = jnp.where(d_c // A_KV_HEADS == d_r, 1.0, 0.0).astype(BF16)
        for rb in range(n_blocks):
            rows = slice(rb * TILE, (rb + 1) * TILE)
            skey_ref[...] = _sortable_key(jnp.where(adm_blk, sc_ref[rows, :], -jnp.inf))
            sel = jnp.where(_topk_member(skey_ref, k_sel) & adm_blk, 1.0, 0.0).astype(BF16)
            stacked = jnp.concatenate([sel[:, j * PAGE_SIZE:(j + 1) * PAGE_SIZE] for j in range(n_tiles)], axis=0)
            stacked = jnp.dot(stacked, dup, preferred_element_type=F32)
            mask_ref[rows, :] = jnp.concatenate(
                [stacked[j * TILE:(j + 1) * TILE, :] for j in range(n_tiles)], axis=1)


def dsa_sample_select(cache_ik_t, layer, page_table, qi_rows, wmat, ki_new_t, n_tok, k_sel):
    DB, n_pages = page_table.shape
    rows_pad = wmat.shape[1]
    per = rows_pad // n_tok
    n_rows = DB // per * rows_pad
    L = (n_pages + 1) * PAGE_SIZE
    page_specs = [pl.BlockSpec((None, None, IDX_DIM, PAGE_SIZE), functools.partial(
        lambda b, pt, e, p: (layer, pt[b * per + e, p], 0, 0), e=e, p=p))
        for e in range(per) for p in range(n_pages)]
    grid_spec = pltpu.PrefetchScalarGridSpec(
        num_scalar_prefetch=1,
        grid=(DB // per,),
        in_specs=page_specs + [
            pl.BlockSpec((per,) + qi_rows.shape[1:], lambda b, pt: (b, 0, 0)),
            pl.BlockSpec((1,) + wmat.shape[1:], lambda b, pt: (b, 0, 0)),
            pl.BlockSpec((per, IDX_DIM, PAGE_SIZE), lambda b, pt: (b, 0, 0))],
        out_specs=pl.BlockSpec((n_rows, A_KV_HEADS * L), lambda b, pt: (0, 0)),
        scratch_shapes=[pltpu.VMEM((n_rows, L), F32),
                        pltpu.VMEM((TILE, L), I32)],
    )
    return pl.pallas_call(
        functools.partial(_dsa_sample_select_kernel, n_pages=n_pages, n_tok=n_tok, k_sel=k_sel,
                          rows_pad=rows_pad),
        out_shape=jax.ShapeDtypeStruct((n_rows, A_KV_HEADS * L), F32),
        grid_spec=grid_spec,
        compiler_params=_cparams(("arbitrary",)),
        name="dsa_sample_select",
    )(page_table, *([cache_ik_t] * (per * n_pages)), qi_rows, wmat, ki_new_t)


def _dsa_sample_attend_kernel(pt_ref, *refs, n_pages, n_tok, rows_pad):
    del pt_ref
    per = rows_pad // n_tok
    k_refs = refs[:per * n_pages]
    v_refs = refs[per * n_pages:2 * per * n_pages]
    q_ref, kn_ref, vn_ref, mask_ref, bias_ref, qg_ref, o_ref = refs[2 * per * n_pages:]
    rows = n_tok * A_HEADS
    page_rows = PAGE_SIZE * A_KV_HEADS
    n_cols = mask_ref.shape[1]
    pad = jnp.zeros((page_rows - n_tok * A_KV_HEADS, HEAD_DIM), BF16)
    rep = A_HEADS // A_KV_HEADS
    grp = (lax.broadcasted_iota(I32, (rows, 1), 0) % A_HEADS) // rep
    own_group = (lax.broadcasted_iota(I32, (rows, n_cols), 1) % A_KV_HEADS) == grp
    member = mask_ref[...]

    for e in range(per):
        def tiles(page_refs, new_ref):
            new = jnp.concatenate([new_ref[e].astype(BF16), pad], axis=0)
            return [r[...].astype(BF16) for r in page_refs[e * n_pages:(e + 1) * n_pages]] + [new]

        q = (_rms(q_ref[e], qg_ref[...]) * HEAD_DIM ** -0.5).astype(BF16)
        sel = jnp.concatenate(
            [jnp.broadcast_to(member[e * n_tok + t:e * n_tok + t + 1, :], (A_HEADS, n_cols)) for t in range(n_tok)],
            axis=0)
        valid = (sel > 0.5) & own_group
        logits = jnp.concatenate(
            [lax.dot_general(q, kt, (((1,), (1,)), ((), ())), preferred_element_type=F32)
             for kt in tiles(k_refs, kn_ref)], axis=1)
        logits = jnp.where(valid, logits + bias_ref[...], NEG_BIG)
        m = jnp.max(logits, axis=-1, keepdims=True)
        p = jnp.exp(logits - m)
        den = jnp.sum(p, axis=-1, keepdims=True)
        pb = p.astype(BF16)
        o = jnp.zeros((rows, HEAD_DIM), F32)
        for j, vt in enumerate(tiles(v_refs, vn_ref)):
            o = o + jnp.dot(pb[:, j * page_rows:(j + 1) * page_rows], vt, preferred_element_type=F32)
        o_ref[e] = (o / den).astype(o_ref.dtype)


def dsa_sample_attend(cache_k, cache_v, layer, page_table, q_rows, k_new, v_new, mask, bias_tab, q_norm_g):
    DB, n_pages = page_table.shape
    n_tok = k_new.shape[1] // A_KV_HEADS
    rows = n_tok * A_HEADS
    rows_pad = SUBLANES
    per = rows_pad // n_tok
    n_cols = mask.shape[1]
    page_rows = PAGE_SIZE * A_KV_HEADS
    page_specs = [pl.BlockSpec((None, None, page_rows, HEAD_DIM), functools.partial(
        lambda b, pt, e, p: (layer, pt[b * per + e, p], 0, 0), e=e, p=p))
        for e in range(per) for p in range(n_pages)]
    grid_spec = pltpu.PrefetchScalarGridSpec(
        num_scalar_prefetch=1,
        grid=(DB // per,),
        in_specs=page_specs + page_specs + [
            pl.BlockSpec((per, rows, HEAD_DIM), lambda b, pt: (b, 0, 0)),
            pl.BlockSpec((per, n_tok * A_KV_HEADS, HEAD_DIM), lambda b, pt: (b, 0, 0)),
            pl.BlockSpec((per, n_tok * A_KV_HEADS, HEAD_DIM), lambda b, pt: (b, 0, 0)),
            pl.BlockSpec((rows_pad, n_cols), lambda b, pt: (b, 0)),
            pl.BlockSpec((rows, n_cols), lambda b, pt: (0, 0), pipeline_mode=pl.Buffered(1)),
            pl.BlockSpec((1, HEAD_DIM), lambda b, pt: (0, 0), pipeline_mode=pl.Buffered(1))],
        out_specs=pl.BlockSpec((per, rows, HEAD_DIM), lambda b, pt: (b, 0, 0)),
    )
    return pl.pallas_call(
        functools.partial(_dsa_sample_attend_kernel, n_pages=n_pages, n_tok=n_tok, rows_pad=rows_pad),
        out_shape=jax.ShapeDtypeStruct((DB, rows, HEAD_DIM), BF16),
        grid_spec=grid_spec,
        compiler_params=_cparams(("parallel",)),
        name="dsa_sample_attend",
    )(page_table, *([cache_k] * (per * n_pages)), *([cache_v] * (per * n_pages)), q_rows, k_new, v_new, mask,
      bias_tab, q_norm_g.reshape(1, HEAD_DIM))


def _log_sigmoid(z):
    return jnp.minimum(z, 0.0) - jnp.log(1.0 + jnp.exp(-jnp.abs(z)))


def _seg_masks(seg):
    r = lax.broadcasted_iota(I32, (TILE, TILE), 0)
    c = lax.broadcasted_iota(I32, (TILE, TILE), 1)
    return r, c, (r // seg) == (c // seg)


def _gla_levels(seg):
    w, out = seg // 2, []
    while w >= 1:
        out.append(w)
        w //= 2
    return out


def _gla_sum_matrices(seg):
    r = jnp.arange(TILE)[:, None]
    c = jnp.arange(TILE)[None, :]
    mats = []
    for w in _gla_levels(seg):
        same = (r // (2 * w)) == (c // (2 * w))
        r_right = (r % (2 * w)) >= w
        c_right = (c % (2 * w)) >= w
        mats.append(same & r_right & c_right & (c <= r))
    for w in _gla_levels(seg):
        same = (r // (2 * w)) == (c // (2 * w))
        r_right = (r % (2 * w)) >= w
        c_right = (c % (2 * w)) >= w
        mats.append(same & (~r_right) & (~c_right) & (c > r))
    same_seg = (r // seg) == (c // seg)
    mats.append(same_seg & (c <= r))
    mats.append(same_seg & (c > r))
    return jnp.concatenate(mats, axis=0).astype(BF16)


def _bdot(a, b):
    return jnp.dot(a.astype(BF16), b.astype(BF16), preferred_element_type=F32)


def _bdot_nt(a, b):
    return lax.dot_general(a.astype(BF16), b.astype(BF16), (((1,), (1,)), ((), ())), preferred_element_type=F32)


def _gla_common(tiles, wg_ref, bg_ref, mats_ref, seg):
    n = len(tiles)
    kw = B_HEADS * B_DK
    las = []
    for _, _, misc_ref in tiles:
        gb = misc_ref[:, MISC_GB:MISC_GB + GATE_RANK]
        z = jnp.dot(gb, wg_ref[...], precision=HIGHEST, preferred_element_type=F32) + bg_ref[...]
        las.append(_log_sigmoid(z) / GATE_TEMP)
    la = jnp.concatenate(las, axis=1)
    la_hi = la.astype(BF16)
    la_lo = (la - la_hi.astype(F32)).astype(BF16)
    mats = mats_ref[...]
    sums = (jnp.dot(mats, la_hi, preferred_element_type=F32) + jnp.dot(mats, la_lo, preferred_element_type=F32))
    levels = _gla_levels(seg)
    nl = len(levels)
    qs = [qb_ref[...] * B_DK ** -0.5 for qb_ref, _, _ in tiles]
    ks = [kb_ref[...] for _, kb_ref, _ in tiles]
    r, c, _ = _seg_masks(seg)
    atts = [[jnp.where(r == c, _bdot_nt(qs[i][:, h * B_DK:(h + 1) * B_DK], ks[i][:, h * B_DK:(h + 1) * B_DK]), 0.0)
             for h in range(B_HEADS)] for i in range(n)]
    for li, w in enumerate(levels):
        pair = ((r // (2 * w)) == (c // (2 * w))) & ((r % (2 * w)) >= w) & ((c % (2 * w)) < w)
        qd = [(qs[i] * jnp.exp(sums[li * TILE:(li + 1) * TILE, i * kw:(i + 1) * kw])).astype(BF16) for i in range(n)]
        kd = [(ks[i] * jnp.exp(sums[(nl + li) * TILE:(nl + li + 1) * TILE, i * kw:(i + 1) * kw])).astype(BF16)
              for i in range(n)]
        for h in range(B_HEADS):
            hs = slice(h * B_DK, (h + 1) * B_DK)
            for i in range(n):
                atts[i][h] = atts[i][h] + jnp.where(pair, _bdot_nt(qd[i][:, hs], kd[i][:, hs]), 0.0)
    out = []
    for i in range(n):
        cs = slice(i * kw, (i + 1) * kw)
        b_cum = sums[2 * nl * TILE:(2 * nl + 1) * TILE, cs]
        rem = sums[(2 * nl + 1) * TILE:(2 * nl + 2) * TILE, cs]
        out.append((qs[i], ks[i], atts[i], b_cum, rem))
    return out


def _gla_finish(o_heads, rb_ref, go_ref, o_ref):
    go = go_ref[...]
    for h in range(B_HEADS):
        vs = slice(h * B_DV, (h + 1) * B_DV)
        rb = rb_ref[:, vs]
        o_ref[:, vs] = (_rms(o_heads[h], go) * (rb * jax.nn.sigmoid(rb))).astype(o_ref.dtype)


def _gla_prompt_kernel(qb_ref, kb_ref, vb_ref, rb_ref, misc_ref, wg_ref, bg_ref, go_ref, mats_ref,
                       o_ref, s_ref, state_ref):
    ci = pl.program_id(0)

    @pl.when(ci == 0)
    def _():
        state_ref[...] = jnp.zeros_like(state_ref)

    nb = qb_ref.shape[0]
    common = _gla_common([(qb_ref.at[b], kb_ref.at[b], misc_ref.at[b]) for b in range(nb)],
                         wg_ref, bg_ref, mats_ref, TILE)
    vals = [vb_ref[b] for b in range(nb)]
    states = [state_ref[b] for b in range(nb)]
    qes = [common[b][0] * jnp.exp(common[b][3]) for b in range(nb)]
    o_heads = [[] for _ in range(nb)]
    for h in range(B_HEADS):
        ks = slice(h * B_DK, (h + 1) * B_DK)
        vs = slice(h * B_DV, (h + 1) * B_DV)
        for b in range(nb):
            o_heads[b].append(_bdot(qes[b][:, ks], states[b][ks, :]) + _bdot(common[b][2][h], vals[b][:, vs]))
    for b in range(nb):
        _gla_finish(o_heads[b], rb_ref.at[b], go_ref, o_ref.at[b])

    ke_ts = [(common[b][1] * jnp.exp(common[b][4])).T for b in range(nb)]
    e_cols = [jnp.broadcast_to(jnp.exp(common[b][3][TILE - 1:TILE, :]), (TILE, B_HEADS * B_DK)).T[:, 0:1]
              for b in range(nb)]
    for b in range(nb):
        upd = jnp.concatenate(
            [_bdot(ke_ts[b][h * B_DK:(h + 1) * B_DK, :], vals[b][:, h * B_DV:(h + 1) * B_DV])
             for h in range(B_HEADS)], axis=0)
        new_state = states[b] * e_cols[b] + upd
        state_ref[b] = new_state
        s_ref[b] = new_state


def gla_prompt(proj, w_gate, b_gate, g_out, n_batch, seq):
    nc = seq // TILE
    kwid = B_HEADS * B_DK
    vwid = B_HEADS * B_DV
    mats = _gla_sum_matrices(TILE)
    proj3 = proj.reshape(n_batch, seq, proj.shape[1])
    o, s = pl.pallas_call(
        _gla_prompt_kernel,
        out_shape=(jax.ShapeDtypeStruct((n_batch, seq, vwid), BF16),
                   jax.ShapeDtypeStruct((n_batch, kwid, B_DV), F32)),
        grid=(nc,),
        in_specs=[pl.BlockSpec((n_batch, TILE, kwid), lambda c: (0, c, QB_OFF // kwid)),
                  pl.BlockSpec((n_batch, TILE, kwid), lambda c: (0, c, KB_OFF // kwid)),
                  pl.BlockSpec((n_batch, TILE, vwid), lambda c: (0, c, VB_OFF // vwid)),
                  pl.BlockSpec((n_batch, TILE, vwid), lambda c: (0, c, RB_OFF // vwid)),
                  pl.BlockSpec((n_batch, TILE, LANES), lambda c: (0, c, MISC_OFF // LANES)),
                  _resident((GATE_RANK, kwid)),
                  _resident((1, kwid)),
                  _resident((1, B_DV)),
                  _resident(mats.shape)],
        out_specs=(pl.BlockSpec((n_batch, TILE, vwid), lambda c: (0, c, 0)),
                   pl.BlockSpec((n_batch, kwid, B_DV), lambda c: (0, 0, 0))),
        scratch_shapes=[pltpu.VMEM((n_batch, kwid, B_DV), F32)],
        compiler_params=_cparams(("arbitrary",)),
        name="gla_prompt",
    )(proj3, proj3, proj3, proj3, proj3, w_gate, b_gate.reshape(1, kwid), g_out.reshape(1, B_DV), mats)
    return o.reshape(n_batch * seq, vwid), s.reshape(n_batch, B_HEADS, B_DK, B_DV)


def _gla_sample_kernel(qb_ref, kb_ref, vb_ref, rb_ref, misc_ref, wg_ref, bg_ref, go_ref, mats_ref, s0_ref,
                       o_ref, s_ref, *, seg):
    nbt = TILE // seg
    (q, k, att, b_cum, rem), = _gla_common([(qb_ref, kb_ref, misc_ref)], wg_ref, bg_ref, mats_ref, seg)
    v = vb_ref[...]
    qe = q * jnp.exp(b_cum)
    ke = k * jnp.exp(rem)
    r1 = lax.broadcasted_iota(I32, (TILE, 1), 0)
    e_last = jnp.where(r1 % seg == seg - 1, jnp.exp(b_cum), 0.0)
    wide = nbt * B_DK
    mq = (lax.broadcasted_iota(I32, (TILE, wide), 0) // seg) == (lax.broadcasted_iota(I32, (TILE, wide), 1) // B_DK)
    mk = (lax.broadcasted_iota(I32, (wide, TILE), 0) // B_DK) == (lax.broadcasted_iota(I32, (wide, TILE), 1) // seg)
    o_heads = []
    for h in range(B_HEADS):
        ks = slice(h * B_DK, (h + 1) * B_DK)
        vs = slice(h * B_DV, (h + 1) * B_DV)
        state = s0_ref[:, h].reshape(wide, B_DV)
        q_bd = jnp.where(mq, jnp.concatenate([qe[:, ks]] * nbt, axis=1), 0.0)
        o_heads.append(_bdot(q_bd, state) + _bdot(att[h], v[:, vs]))
        pair_t = jnp.concatenate([ke[:, ks], e_last[:, ks]], axis=1).T
        k_bd = jnp.where(mk, jnp.concatenate([pair_t[:B_DK]] * nbt, axis=0), 0.0)
        e_bd = jnp.where(mk, jnp.concatenate([pair_t[B_DK:]] * nbt, axis=0), 0.0)
        e_col = jnp.sum(e_bd, axis=-1, keepdims=True)
        new_state = state * e_col + _bdot(k_bd, v[:, vs])
        s_ref[:, h] = new_state.reshape(nbt, B_DK, B_DV)
    _gla_finish(o_heads, rb_ref, go_ref, o_ref)


def gla_sample(proj, w_gate, b_gate, g_out, s0, layer, n_tok):
    T = proj.shape[0]
    nbt = TILE // n_tok
    kwid = B_HEADS * B_DK
    vwid = B_HEADS * B_DV
    mats = _gla_sum_matrices(n_tok)
    return pl.pallas_call(
        functools.partial(_gla_sample_kernel, seg=n_tok),
        out_shape=(jax.ShapeDtypeStruct((T, vwid), BF16),
                   jax.ShapeDtypeStruct(s0.shape[1:], F32)),
        grid=(T // TILE,),
        in_specs=[pl.BlockSpec((TILE, kwid), lambda i: (i, QB_OFF // kwid)),
                  pl.BlockSpec((TILE, kwid), lambda i: (i, KB_OFF // kwid)),
                  pl.BlockSpec((TILE, vwid), lambda i: (i, VB_OFF // vwid)),
                  pl.BlockSpec((TILE, vwid), lambda i: (i, RB_OFF // vwid)),
                  pl.BlockSpec((TILE, LANES), lambda i: (i, MISC_OFF // LANES)),
                  _resident((GATE_RANK, kwid)),
                  _resident((1, kwid)),
                  _resident((1, B_DV)),
                  _resident(mats.shape),
                  pl.BlockSpec((None, nbt, B_HEADS, B_DK, B_DV), lambda i: (layer, i, 0, 0, 0))],
        out_specs=(pl.BlockSpec((TILE, vwid), lambda i: (i, 0)),
                   pl.BlockSpec((nbt, B_HEADS, B_DK, B_DV), lambda i: (i, 0, 0, 0))),
        compiler_params=_cparams(("parallel",)),
        name="gla_sample",
    )(proj, proj, proj, proj, proj, w_gate, b_gate.reshape(1, kwid), g_out.reshape(1, B_DV), mats, s0)


def _gelu(x):
    return jax.nn.gelu(x)


def _gmlp_kernel(uc_ref, vc_ref, gv_ref, ws_ref, bcol_ref, o_ref, vn_ref=None, *, seg):
    r, c, same_seg = _seg_masks(seg)
    keep = same_seg & (c <= r)
    for t in range(uc_ref.shape[0] // TILE):
        rows = slice(t * TILE, (t + 1) * TILE)
        u = _gelu(uc_ref[rows, :])
        vg = _gelu(vc_ref[rows, :])
        for g in range(C_GROUPS):
            gs = slice(g * C_GROUP_DIM, (g + 1) * C_GROUP_DIM)
            vn = _rms(vg[:, gs], gv_ref[:, gs])
            if vn_ref is not None:
                vn_ref[rows, gs] = vn
            w = jnp.where(keep, ws_ref[g], 0.0).astype(BF16)
            s = jnp.dot(w, vn.astype(BF16), preferred_element_type=F32) + bcol_ref[:, g:g + 1]
            o_ref[rows, gs] = (u[:, gs] * s).astype(o_ref.dtype)


def gmlp(proj, g_v, w_tiles, b_cols, seg, tm, with_vn):
    T = proj.shape[0]
    tm = min(tm, T)
    cw = C_GROUPS * C_GROUP_DIM
    out_shape = [jax.ShapeDtypeStruct((T, cw), BF16)]
    out_specs = [pl.BlockSpec((tm, cw), lambda i: (i, 0))]
    if with_vn:
        out_shape.append(jax.ShapeDtypeStruct((T, cw), F32))
        out_specs.append(pl.BlockSpec((tm, cw), lambda i: (i, 0)))
    return pl.pallas_call(
        functools.partial(_gmlp_kernel, seg=seg),
        out_shape=tuple(out_shape),
        grid=(T // tm,),
        in_specs=[pl.BlockSpec((tm, cw), lambda i: (i, UC_OFF // cw)),
                  pl.BlockSpec((tm, cw), lambda i: (i, VC_OFF // cw)),
                  _resident((1, cw)),
                  _resident((C_GROUPS, TILE, TILE)),
                  _resident((TILE, C_GROUPS))],
        out_specs=tuple(out_specs),
        compiler_params=_cparams(("parallel",)),
        name="gmlp",
    )(proj, proj, g_v.reshape(1, cw), w_tiles, b_cols)


def _ffn_kernel(h_ref, oa_ref, ob_ref, oc_ref, wo_ref, g_ref, wg_ref, wu_ref, wd_ref, o_ref, n_ref):
    j = pl.program_id(1)

    @pl.when(j == 0)
    def _():
        aw = oa_ref.shape[1]
        bw = ob_ref.shape[1]
        h = h_ref[...] + jnp.dot(oa_ref[...], wo_ref[0:aw, :], preferred_element_type=F32)
        h = h + jnp.dot(ob_ref[...], wo_ref[aw:aw + bw, :], preferred_element_type=F32)
        h = h + jnp.dot(oc_ref[...], wo_ref[aw + bw:, :], preferred_element_type=F32)
        n_ref[...] = _rms(h, g_ref[...]).astype(BF16)
        o_ref[...] = h

    n = n_ref[...]
    a = jnp.dot(n, wg_ref[...], preferred_element_type=F32)
    u = jnp.dot(n, wu_ref[...], preferred_element_type=F32)
    act = (a * jax.nn.sigmoid(a) * u).astype(BF16)
    o_ref[...] += jnp.dot(act, wd_ref[...], preferred_element_type=F32)


def out_proj_ffn(h, o_a, o_b, o_c, w_out, g, w_gate, w_up, w_down, tm, tf):
    T, D = h.shape
    tm = min(tm, T)
    FF = w_gate.shape[1]
    return pl.pallas_call(
        _ffn_kernel,
        out_shape=jax.ShapeDtypeStruct((T, D), F32),
        grid=(T // tm, FF // tf),
        in_specs=[pl.BlockSpec((tm, D), lambda i, j: (i, 0)),
                  pl.BlockSpec((tm, o_a.shape[1]), lambda i, j: (i, 0)),
                  pl.BlockSpec((tm, o_b.shape[1]), lambda i, j: (i, 0)),
                  pl.BlockSpec((tm, o_c.shape[1]), lambda i, j: (i, 0)),
                  _resident(w_out.shape),
                  _resident((1, D)),
                  pl.BlockSpec((D, tf), lambda i, j: (0, j)),
                  pl.BlockSpec((D, tf), lambda i, j: (0, j)),
                  pl.BlockSpec((tf, D), lambda i, j: (j, 0))],
        out_specs=pl.BlockSpec((tm, D), lambda i, j: (i, 0)),
        scratch_shapes=[pltpu.VMEM((tm, D), BF16)],
        compiler_params=_cparams(("parallel", "arbitrary")),
        name="out_proj_ffn",
    )(h, o_a, o_b, o_c, w_out, g.reshape(1, D), w_gate, w_up, w_down)


def _ple_kernel(h_ref, p_ref, g_ref, wgate_ref, wproj_ref, o_ref):
    h = h_ref[...]
    n = _rms(h, g_ref[...]).astype(BF16)
    gate = jax.nn.sigmoid(jnp.dot(n, wgate_ref[...], preferred_element_type=F32))
    emb = jnp.dot(p_ref[...].astype(BF16), wproj_ref[...], preferred_element_type=F32)
    o_ref[...] = h + gate * emb


def ple(h, p, g, w_gate, w_proj, layer, tm):
    T, D = h.shape
    tm = min(tm, T)
    P = p.shape[2]
    return pl.pallas_call(
        _ple_kernel,
        out_shape=jax.ShapeDtypeStruct((T, D), F32),
        grid=(T // tm,),
        in_specs=[pl.BlockSpec((tm, D), lambda i: (i, 0)),
                  pl.BlockSpec((None, tm, P), lambda i: (layer, i, 0)),
                  _resident((1, D)),
                  _resident(w_gate.shape),
                  _resident(w_proj.shape)],
        out_specs=pl.BlockSpec((tm, D), lambda i: (i, 0)),
        compiler_params=_cparams(("parallel",)),
        name="ple",
    )(h, p, g.reshape(1, D), w_gate, w_proj)


_W_IN_SEGMENTS = (("q", 1024), ("k", 256), ("v", 256), ("qi", 1024), ("ki", 64), ("wi", 16), ("qb", 256),
                  ("kb", 256), ("vb", 512), ("gb", 16), ("rb", 512), ("uc", 512), ("vc", 512))
_W_IN_PACKED_ORDER = ("q", "qi", "vb", "rb", "uc", "vc", "k", "v", "qb", "kb", "ki", "wi", "gb")


def _pack_kernel(wt_ref, o_ref):
    src, start = {}, 0
    for name, size in _W_IN_SEGMENTS:
        src[name] = (start, size)
        start += size
    dst = 0
    small = []
    for name in _W_IN_PACKED_ORDER:
        s0, size = src[name]
        if size < LANES:
            small.append(wt_ref[s0:s0 + size, :])
            continue
        o_ref[:, dst:dst + size] = wt_ref[s0:s0 + size, :].T.astype(BF16)
        dst += size
    used = sum(x.shape[0] for x in small)
    small.append(jnp.zeros((LANES - used, wt_ref.shape[1]), F32))
    o_ref[:, dst:dst + LANES] = jnp.concatenate(small, axis=0).T.astype(BF16)


def _pack_w_in(w, tr=256):
    depth, D, N = w.shape
    return pl.pallas_call(
        _pack_kernel,
        out_shape=jax.ShapeDtypeStruct((depth, D, PROJ_PACKED), BF16),
        grid=(depth, D // tr),
        in_specs=[pl.BlockSpec((None, N, tr), lambda l, i: (l, 0, i))],
        out_specs=pl.BlockSpec((None, tr, PROJ_PACKED), lambda l, i: (l, i, 0)),
        compiler_params=_cparams(("parallel", "parallel")),
        name="pack_w_in",
    )(jnp.swapaxes(w, 1, 2))


def _mixer_tail(h, o_a, o_b, o_c, p_all, lw, layer):
    h = out_proj_ffn(h, o_a, o_b, o_c, lw["w_out"], lw["g_ffn"], lw["w_ffn_gate"], lw["w_ffn_up"],
                     lw["w_ffn_down"], TM_FFN, TF_FFN)
    return ple(h, p_all, lw["g_ple"], lw["w_ple_gate"], lw["w_ple_proj"], layer, TM_PLE)


def kernel(x_prompt, x_sample, cache_k, cache_v, cache_idx_k, state_gla, page_table, p_prompt, p_sample,
           g_mix, w_in, q_norm_g, k_norm_g, rel_bias, w_gate_b, b_gate_b, g_out_b, g_v_c, w_spatial,
           b_spatial, w_out, g_ffn, w_ffn_gate, w_ffn_up, w_ffn_down, g_ple, w_ple_gate, w_ple_proj):
    n_batch, seq, d_model = x_prompt.shape
    dec_batch, dec_seq, _ = x_sample.shape
    depth = w_in.shape[0]
    n_pages = page_table.shape[1]
    past = n_pages * PAGE_SIZE
    kw = A_KV_HEADS * HEAD_DIM
    tp, ts = n_batch * seq, dec_batch * dec_seq
    rows_pad = SUBLANES
    l_sample = past + PAGE_SIZE
    k_sel_s = min(TOPK_MAX, (past + dec_seq) // 4)

    bias_p = bias_table_prompt(rel_bias)
    bias_s = bias_table_sample(rel_bias, past, dec_seq, A_KV_HEADS * l_sample)
    cache_ik_t = jnp.swapaxes(cache_idx_k, 2, 3)
    cache_k2 = cache_k.reshape(depth, cache_k.shape[1], PAGE_SIZE * A_KV_HEADS, HEAD_DIM)
    cache_v2 = cache_v.reshape(depth, cache_v.shape[1], PAGE_SIZE * A_KV_HEADS, HEAD_DIM)

    hp = x_prompt.reshape(tp, d_model)
    hs = x_sample.reshape(ts, d_model)
    outs = {k: [] for k in ("kp", "vp", "ikp", "sp", "ks", "vs", "iks", "ss", "cs")}
    per_s = rows_pad // dec_seq
    place_t = (jnp.arange(rows_pad)[:, None, None]
               == jnp.arange(per_s)[None, :, None] * dec_seq + jnp.arange(dec_seq)[None, None, :]
               ).astype(F32)
    w_ple_proj_b = w_ple_proj.astype(BF16)
    w_packed = _pack_w_in(w_in)
    pp_all = p_prompt.reshape(depth, tp, -1)
    ps_all = p_sample.reshape(depth, ts, -1)
    for i in range(depth):
        b_cols_p = b_spatial[i].T
        reps = TILE // dec_seq
        w_tiles_s = jnp.tile(w_spatial[i][:, :dec_seq, :dec_seq], (1, reps, reps))
        b_cols_s = jnp.tile(b_spatial[i][:, :dec_seq].T, (reps, 1))

        proj, (wo_b, wg_b, wu_b, wd_b, wpg_b) = in_projection(
            hp, g_mix[i], w_packed, i, TM_PROJ, cast=(w_out, w_ffn_gate, w_ffn_up, w_ffn_down, w_ple_gate))
        lw = dict(w_out=wo_b, g_ffn=g_ffn[i], w_ffn_gate=wg_b, w_ffn_up=wu_b, w_ffn_down=wd_b,
                  g_ple=g_ple[i], w_ple_gate=wpg_b, w_ple_proj=w_ple_proj_b[i])
        kn, vv, ik, vt = kv_post(proj, k_norm_g[i], TM_KV, True)
        o_a = dsa_prompt(proj, kn, vt, bias_p, q_norm_g[i], n_batch, seq)
        o_b, s_p = gla_prompt(proj, w_gate_b[i], b_gate_b[i], g_out_b[i], n_batch, seq)
        o_c, = gmlp(proj, g_v_c[i], w_spatial[i], b_cols_p, TILE, TM_GMLP, False)
        hp = _mixer_tail(hp, o_a, o_b, o_c, pp_all, lw, i)
        outs["kp"].append(kn.reshape(n_batch, seq, A_KV_HEADS, HEAD_DIM))
        outs["vp"].append(vv.reshape(n_batch, seq, A_KV_HEADS, HEAD_DIM))
        outs["ikp"].append(ik.reshape(n_batch, seq, IDX_DIM))
        outs["sp"].append(s_p)

        proj, _ = in_projection(hs, g_mix[i], w_packed, i, TM_PROJ)
        kn, vv, ik = kv_post(proj, k_norm_g[i], TM_KV, False)
        qi_rows = proj[:, QI_OFF:QI_OFF + IDX_HEADS * IDX_DIM].reshape(dec_batch, dec_seq * IDX_HEADS, IDX_DIM)
        wi = proj[:, MISC_OFF + MISC_WI:MISC_OFF + MISC_WI + IDX_HEADS].reshape(dec_batch, dec_seq, IDX_HEADS)
        wi = wi * (IDX_HEADS ** -0.5 * IDX_DIM ** -0.5)
        wmat = (place_t[None, :, :, :, None] * wi.reshape(dec_batch // per_s, 1, per_s, dec_seq, IDX_HEADS)
                ).reshape(dec_batch // per_s, rows_pad, per_s * dec_seq * IDX_HEADS)
        ki_new_t = jnp.pad(jnp.swapaxes(ik.reshape(dec_batch, dec_seq, IDX_DIM), 1, 2),
                           ((0, 0), (0, 0), (0, PAGE_SIZE - dec_seq)))
        mask = dsa_sample_select(cache_ik_t, i, page_table, qi_rows, wmat, ki_new_t, dec_seq, k_sel_s)
        q_rows = proj[:, Q_OFF:Q_OFF + A_HEADS * HEAD_DIM].reshape(dec_batch, dec_seq * A_HEADS, HEAD_DIM)
        o_a = dsa_sample_attend(cache_k2, cache_v2, i, page_table, q_rows,
                                kn.reshape(dec_batch, dec_seq * A_KV_HEADS, HEAD_DIM),
                                vv.reshape(dec_batch, dec_seq * A_KV_HEADS, HEAD_DIM), mask, bias_s, q_norm_g[i])
        o_a = o_a.reshape(ts, A_HEADS * HEAD_DIM)
        o_b, s_s = gla_sample(proj, w_gate_b[i], b_gate_b[i], g_out_b[i], state_gla, i, dec_seq)
        o_c, vn = gmlp(proj, g_v_c[i], w_tiles_s, b_cols_s, dec_seq, TM_GMLP, True)
        hs = _mixer_tail(hs, o_a, o_b, o_c, ps_all, lw, i)
        outs["ks"].append(kn.reshape(dec_batch, dec_seq, A_KV_HEADS, HEAD_DIM))
        outs["vs"].append(vv.reshape(dec_batch, dec_seq, A_KV_HEADS, HEAD_DIM))
        outs["iks"].append(ik.reshape(dec_batch, dec_seq, IDX_DIM))
        outs["ss"].append(s_s)
        outs["cs"].append(vn.reshape(dec_batch, dec_seq, -1))

    st = {k: jnp.stack(v) for k, v in outs.items()}
    return (hp.reshape(n_batch, seq, d_model), hs.reshape(dec_batch, dec_seq, d_model),
            st["kp"], st["vp"], st["ikp"], st["sp"], st["ks"], st["vs"], st["iks"], st["ss"], st["cs"])
```

```python
import functools
import math

import jax
import jax.numpy as jnp
from jax import lax
from jax.experimental import pallas as pl
from jax.experimental.pallas import tpu as pltpu

F32 = jnp.float32
BF16 = jnp.bfloat16
I32 = jnp.int32
HIGHEST = lax.Precision.HIGHEST

LANES = 128
SUBLANES = 8
VMEM_LIMIT = 56 * 1024 * 1024

HEAD_DIM = 128
A_HEADS = 8
A_KV_HEADS = 2
IDX_HEADS = 16
IDX_DIM = 64
TOPK_MAX = 256
NUM_BUCKETS = 32
MAX_DISTANCE = 128
B_HEADS = 4
B_DK = 64
B_DV = 128
GATE_RANK = 16
GATE_TEMP = 16.0
C_GROUPS = 4
C_GROUP_DIM = 128
PAGE_SIZE = 128
EPS = 1e-6
NEG_BIG = -1e30
INT_MIN = -(2 ** 31)
NEG_INF_KEY = -2139095041

TILE = 128
QBLK = 256
TM_PROJ = 256
TM_KV = 1024
TM_GMLP = 512
TM_FFN = 512
TF_FFN = 512
TM_PLE = 512

Q_OFF, QI_OFF, VB_OFF, RB_OFF, UC_OFF, VC_OFF = 0, 1024, 2048, 2560, 3072, 3584
K_OFF, V_OFF, QB_OFF, KB_OFF, MISC_OFF = 4096, 4352, 4608, 4864, 5120
PROJ_PACKED = 5248
MISC_KI, MISC_WI, MISC_GB = 0, 64, 80


def _cparams(sem):
    return pltpu.CompilerParams(dimension_semantics=sem, vmem_limit_bytes=VMEM_LIMIT)


def _rms(x, g):
    return x * lax.rsqrt(jnp.mean(x * x, axis=-1, keepdims=True) + EPS) * g


def _resident(shape):
    nd = len(shape)
    return pl.BlockSpec(shape, lambda *_: (0,) * nd, pipeline_mode=pl.Buffered(1))


def _layer_resident(shape, layer):
    nd = len(shape)
    return pl.BlockSpec((None,) + tuple(shape), lambda *_: (layer,) + (0,) * nd, pipeline_mode=pl.Buffered(1))


def _proj_kernel(x_ref, g_ref, w_ref, *refs, n_cast):
    cast_in, o_ref, cast_out = refs[:n_cast], refs[n_cast], refs[n_cast + 1:]
    n = _rms(x_ref[...], g_ref[...]).astype(BF16)
    ncol = o_ref.shape[1]
    step = 512
    for c0 in range(0, ncol, step):
        c1 = min(c0 + step, ncol)
        o_ref[:, c0:c1] = jnp.dot(n, w_ref[:, c0:c1], preferred_element_type=F32)
    for src, dst in zip(cast_in, cast_out):
        dst[...] = src[...].astype(dst.dtype)


def in_projection(h, g, w_packed, layer, tm, cast=()):
    T, D = h.shape
    tm = min(tm, T)
    N = w_packed.shape[2]
    steps = T // tm
    cast_specs_in = [pl.BlockSpec((None, w.shape[1] // steps, w.shape[2]), lambda i: (layer, i, 0)) for w in cast]
    cast_specs_out = [pl.BlockSpec((w.shape[1] // steps, w.shape[2]), lambda i: (i, 0)) for w in cast]
    outs = pl.pallas_call(
        functools.partial(_proj_kernel, n_cast=len(cast)),
        out_shape=(jax.ShapeDtypeStruct((T, N), F32),) + tuple(
            jax.ShapeDtypeStruct(w.shape[1:], BF16) for w in cast),
        grid=(steps,),
        in_specs=[pl.BlockSpec((tm, D), lambda i: (i, 0)),
                  _resident((1, D)),
                  _layer_resident((D, N), layer)] + cast_specs_in,
        out_specs=(pl.BlockSpec((tm, N), lambda i: (i, 0)),) + tuple(cast_specs_out),
        compiler_params=_cparams(("parallel",)),
        name="in_projection",
    )(h, g.reshape(1, D), w_packed, *cast)
    return outs[0], outs[1:]


def _kv_kernel(k_ref, v_ref, m_ref, g_ref, ko_ref, vo_ref, io_ref, vt_ref=None):
    g = g_ref[...]
    k = k_ref[...]
    for hh in range(A_KV_HEADS):
        sl = slice(hh * HEAD_DIM, (hh + 1) * HEAD_DIM)
        ko_ref[:, sl] = _rms(k[:, sl], g)
    v = v_ref[...]
    vo_ref[...] = v
    io_ref[...] = m_ref[:, MISC_KI:MISC_KI + IDX_DIM]
    if vt_ref is not None:
        for blk in range(vt_ref.shape[0]):
            vt_ref[blk] = v[blk * QBLK:(blk + 1) * QBLK, :].T.astype(vt_ref.dtype)


def kv_post(proj, k_norm_g, tm, with_vt):
    T = proj.shape[0]
    tm = min(tm, T)
    kw = A_KV_HEADS * HEAD_DIM
    out_shape = [jax.ShapeDtypeStruct((T, kw), F32),
                 jax.ShapeDtypeStruct((T, kw), F32),
                 jax.ShapeDtypeStruct((T, IDX_DIM), F32)]
    out_specs = [pl.BlockSpec((tm, kw), lambda i: (i, 0)),
                 pl.BlockSpec((tm, kw), lambda i: (i, 0)),
                 pl.BlockSpec((tm, IDX_DIM), lambda i: (i, 0))]
    if with_vt:
        out_shape.append(jax.ShapeDtypeStruct((T // QBLK, kw, QBLK), BF16))
        out_specs.append(pl.BlockSpec((tm // QBLK, kw, QBLK), lambda i: (i, 0, 0)))
    return pl.pallas_call(
        _kv_kernel,
        out_shape=tuple(out_shape),
        grid=(T // tm,),
        in_specs=[pl.BlockSpec((tm, kw), lambda i: (i, K_OFF // kw)),
                  pl.BlockSpec((tm, kw), lambda i: (i, V_OFF // kw)),
                  pl.BlockSpec((tm, LANES), lambda i: (i, MISC_OFF // LANES)),
                  _resident((1, HEAD_DIM))],
        out_specs=tuple(out_specs),
        compiler_params=_cparams(("parallel",)),
        name="kv_post",
    )(proj, proj, proj, k_norm_g.reshape(1, HEAD_DIM))


def _bucket(dist):
    n = jnp.maximum(dist, 0)
    max_exact = NUM_BUCKETS // 2
    large = max_exact + (jnp.log(jnp.maximum(n, 1).astype(F32) / max_exact)
                         / math.log(MAX_DISTANCE / max_exact)
                         * (NUM_BUCKETS - max_exact)).astype(I32)
    large = jnp.minimum(large, NUM_BUCKETS - 1)
    return jnp.where(n < max_exact, n, large)


def _bias_prompt_kernel(rb_ref, o_ref):
    c = lax.broadcasted_iota(I32, (TILE, TILE), 0)
    t = lax.broadcasted_iota(I32, (TILE, TILE), 1)
    for z in range(3):
        bucket = _bucket(t - c + (2 - z) * TILE)
        for h in range(A_HEADS):
            acc = jnp.zeros((TILE, TILE), F32)
            for b in range(NUM_BUCKETS):
                acc = jnp.where(bucket == b, rb_ref[b, h], acc)
            o_ref[h, z] = acc


def bias_table_prompt(rel_bias):
    return pl.pallas_call(
        _bias_prompt_kernel,
        out_shape=jax.ShapeDtypeStruct((A_HEADS, 3, TILE, TILE), F32),
        in_specs=[pl.BlockSpec(memory_space=pltpu.SMEM)],
        out_specs=pl.BlockSpec(memory_space=pltpu.VMEM),
        name="bias_table_prompt",
    )(rel_bias)


def _bias_sample_kernel(rbrows_ref, o_ref, *, past, n_tok):
    rows, L = o_ref.shape
    r = lax.broadcasted_iota(I32, (rows, L), 0)
    s = lax.broadcasted_iota(I32, (rows, L), 1) // A_KV_HEADS
    bucket = _bucket(past + r // A_HEADS - s)
    rbrows = rbrows_ref[...]
    acc = jnp.zeros((rows, L), F32)
    for b in range(NUM_BUCKETS):
        acc = jnp.where(bucket == b, rbrows[:, b:b + 1], acc)
    o_ref[...] = acc


def bias_table_sample(rel_bias, past, n_tok, L):
    rows = n_tok * A_HEADS
    rbrows = jnp.tile(rel_bias.T, (n_tok, 1))
    return pl.pallas_call(
        functools.partial(_bias_sample_kernel, past=past, n_tok=n_tok),
        out_shape=jax.ShapeDtypeStruct((rows, L), F32),
        name="bias_table_sample",
    )(rbrows)


def _sortable_key(x):
    b = lax.bitcast_convert_type(x, I32)
    return b ^ ((b >> 31) & 0x7FFFFFFF)


def _topk_member(skey_ref, k_sel):
    R, L = skey_ref.shape

    def body(it, ans):
        bit = 31 - it
        cand = ans | lax.shift_left(jnp.int32(1), bit)
        cand_s = cand ^ INT_MIN
        cnt = jnp.sum(jnp.where(skey_ref[...] >= cand_s, 1.0, 0.0), axis=-1, keepdims=True)
        return jnp.where(cnt >= k_sel, cand, ans)

    ans = lax.fori_loop(0, 32, body, jnp.zeros((R, 1), I32))
    tau = ans ^ INT_MIN
    skey = skey_ref[...]
    gt = skey > tau
    eq = skey == tau
    n_gt = jnp.sum(jnp.where(gt, 1.0, 0.0), axis=-1, keepdims=True)
    room = k_sel - n_gt
    r_i = lax.broadcasted_iota(I32, (LANES, LANES), 0)
    c_i = lax.broadcasted_iota(I32, (LANES, LANES), 1)
    upper = jnp.where(r_i <= c_i, 1.0, 0.0).astype(BF16)
    off = jnp.zeros((R, 1), F32)
    parts = []
    for j in range(L // LANES):
        sl = slice(j * LANES, (j + 1) * LANES)
        eq_j = eq[:, sl]
        run = jnp.dot(jnp.where(eq_j, 1.0, 0.0).astype(BF16), upper, preferred_element_type=F32) + off
        parts.append(gt[:, sl] | (eq_j & (run <= room)))
        off = run[:, LANES - 1:LANES]
    return jnp.concatenate(parts, axis=1)


def _fold8(x, op):
    return op(x.reshape(x.shape[0] // SUBLANES, SUBLANES, x.shape[1]), axis=0)


def _dsa_prompt_kernel(q_ref, qi_ref, misc_ref, kn_ref, vt_ref, bias_ref, qg_ref, o_ref,
                       qst_ref, skey_ref, madd_ref, lg_ref, acc_ref, *, k_sel):
    i = pl.program_id(1)
    nkb = i + 1
    sub = QBLK // TILE
    rep = A_HEADS // A_KV_HEADS
    row0 = pl.multiple_of(i * QBLK, QBLK)
    s_iota = lax.broadcasted_iota(I32, (QBLK, QBLK), 0)
    t_iota = lax.broadcasted_iota(I32, (QBLK, QBLK), 1)

    def admissible(j):
        return (j * QBLK + s_iota) <= (row0 + t_iota)

    wi_t = misc_ref[pl.ds(row0, QBLK), :].T[MISC_WI:MISC_WI + IDX_HEADS, :]
    wi_t = wi_t * (IDX_HEADS ** -0.5 * IDX_DIM ** -0.5)
    for h in range(IDX_HEADS):
        qst_ref[h * QBLK:(h + 1) * QBLK, :] = qi_ref[:, h * IDX_DIM:(h + 1) * IDX_DIM].astype(BF16)

    def score_body(j, carry):
        k0 = pl.multiple_of(j * QBLK, QBLK)
        kj = misc_ref[pl.ds(k0, QBLK), MISC_KI:MISC_KI + IDX_DIM].astype(BF16)
        s = lax.dot_general(kj, qst_ref[...], (((1,), (1,)), ((), ())), preferred_element_type=F32)
        score = jnp.zeros((QBLK, QBLK), F32)
        for h in range(IDX_HEADS):
            score = score + jnp.maximum(s[:, h * QBLK:(h + 1) * QBLK], 0.0) * wi_t[h:h + 1, :]
        skey_ref[j] = _sortable_key(jnp.where(admissible(j), score, -jnp.inf))
        return carry

    lax.fori_loop(0, nkb, score_body, 0)

    def count(pred_fn):
        def hits(j):
            return _fold8(jnp.where(pred_fn(skey_ref[j]), 1.0, 0.0), jnp.sum)

        def body(jj, accs):
            return accs[0] + hits(2 * jj), accs[1] + hits(2 * jj + 1)

        zero = jnp.zeros((SUBLANES, QBLK), F32)
        acc0, acc1 = lax.fori_loop(0, nkb // 2, body, (zero, zero))
        acc = lax.cond(nkb % 2 == 1, lambda: acc0 + acc1 + hits(nkb - 1), lambda: acc0 + acc1)
        return jnp.sum(acc, axis=0, keepdims=True)

    def bit_body(it, ans):
        cand = ans | lax.shift_left(jnp.int32(1), 31 - it)
        cand_s = cand ^ INT_MIN
        cnt = count(lambda key: key >= cand_s)
        return jnp.where(cnt >= k_sel, cand, ans)

    ans = lax.fori_loop(0, 32, bit_body, jnp.zeros((1, QBLK), I32))
    tau = ans ^ INT_MIN
    n_ge = count(lambda key: key >= tau)
    excess = jnp.max(jnp.where((n_ge > k_sel) & (tau != NEG_INF_KEY), 1.0, 0.0))

    @pl.when(excess == 0.0)
    def _():
        def mask_body(j, carry):
            madd_ref[j] = jnp.where((skey_ref[j] >= tau) & admissible(j), 0.0, NEG_BIG)
            return carry

        lax.fori_loop(0, nkb, mask_body, 0)

    @pl.when(excess > 0.0)
    def _():
        room = k_sel - count(lambda key: key > tau)
        lower = jnp.where(t_iota <= s_iota, 1.0, 0.0).astype(BF16)

        def mask_body(j, off):
            key = skey_ref[j]
            eq = key == tau
            run = jnp.dot(lower, jnp.where(eq, 1.0, 0.0).astype(BF16), preferred_element_type=F32) + off
            sel = ((key > tau) | (eq & (run <= room))) & admissible(j)
            madd_ref[j] = jnp.where(sel, 0.0, NEG_BIG)
            return run[QBLK - 1:QBLK, :]

        lax.fori_loop(0, nkb, mask_body, jnp.zeros((1, QBLK), F32))

    qg = qg_ref[...]
    wide = rep * QBLK
    for g in range(A_KV_HEADS):
        gs = slice(g * HEAD_DIM, (g + 1) * HEAD_DIM)
        heads = list(range(g * rep, (g + 1) * rep))
        q_stack = jnp.concatenate(
            [(_rms(q_ref[:, h * HEAD_DIM:(h + 1) * HEAD_DIM], qg) * HEAD_DIM ** -0.5).astype(BF16) for h in heads],
            axis=0)

        def logit_body(j, mx):
            k0 = pl.multiple_of(j * QBLK, QBLK)
            kj = kn_ref[pl.ds(k0, QBLK), gs].astype(BF16)
            lg = lax.dot_general(kj, q_stack, (((1,), (1,)), ((), ())), preferred_element_type=F32)
            madd = madd_ref[j]
            parts = []
            for r, h in enumerate(heads):
                quads = []
                for c in range(sub):
                    quads.append(jnp.concatenate(
                        [bias_ref[h, jnp.clip(2 - ((i - j) * sub + u - c), 0, 2)] for u in range(sub)], axis=1))
                parts.append(lg[:, r * QBLK:(r + 1) * QBLK] + jnp.concatenate(quads, axis=0) + madd)
            lg = jnp.concatenate(parts, axis=1)
            lg_ref[j] = lg
            return jnp.maximum(mx, _fold8(lg, jnp.max))

        mx = lax.fori_loop(0, nkb, logit_body, jnp.full((SUBLANES, wide), NEG_BIG, F32))
        m = jnp.max(mx, axis=0, keepdims=True)
        acc_ref[...] = jnp.zeros(acc_ref.shape, F32)

        def pv_body(j, sm):
            p = jnp.exp(lg_ref[j] - m)
            acc_ref[...] += jnp.dot(vt_ref[j, gs, :], p.astype(BF16), preferred_element_type=F32)
            return sm + _fold8(p, jnp.sum)

        sm = lax.fori_loop(0, nkb, pv_body, jnp.zeros((SUBLANES, wide), F32))
        den = jnp.sum(sm, axis=0, keepdims=True)
        o = (acc_ref[...] / den).T
        for r, h in enumerate(heads):
            o_ref[:, h * HEAD_DIM:(h + 1) * HEAD_DIM] = o[r * QBLK:(r + 1) * QBLK, :].astype(o_ref.dtype)


def dsa_prompt(proj, kn, vt, bias_tab, q_norm_g, n_batch, seq):
    T = proj.shape[0]
    nb = seq // QBLK
    k_sel = min(TOPK_MAX, seq // 4)
    aw = A_HEADS * HEAD_DIM
    iw = IDX_HEADS * IDX_DIM
    kw = A_KV_HEADS * HEAD_DIM
    rep = A_HEADS // A_KV_HEADS
    return pl.pallas_call(
        functools.partial(_dsa_prompt_kernel, k_sel=k_sel),
        out_shape=jax.ShapeDtypeStruct((T, aw), BF16),
        grid=(n_batch, nb),
        in_specs=[pl.BlockSpec((QBLK, aw), lambda b, i: (b * nb + i, Q_OFF // aw)),
                  pl.BlockSpec((QBLK, iw), lambda b, i: (b * nb + i, QI_OFF // iw)),
                  pl.BlockSpec((seq, LANES), lambda b, i: (b, MISC_OFF // LANES)),
                  pl.BlockSpec((seq, kw), lambda b, i: (b, 0)),
                  pl.BlockSpec((nb, kw, QBLK), lambda b, i: (b, 0, 0)),
                  _resident((A_HEADS, 3, TILE, TILE)),
                  _resident((1, HEAD_DIM))],
        out_specs=pl.BlockSpec((QBLK, aw), lambda b, i: (b * nb + i, 0)),
        scratch_shapes=[pltpu.VMEM((IDX_HEADS * QBLK, IDX_DIM), BF16),
                        pltpu.VMEM((nb, QBLK, QBLK), I32),
                        pltpu.VMEM((nb, QBLK, QBLK), F32),
                        pltpu.VMEM((nb, QBLK, rep * QBLK), F32),
                        pltpu.VMEM((HEAD_DIM, rep * QBLK), F32)],
        compiler_params=_cparams(("parallel", "arbitrary")),
        name="dsa_prompt",
    )(proj, proj, proj, kn, vt, bias_tab, q_norm_g.reshape(1, HEAD_DIM))


def _dsa_sample_select_kernel(pt_ref, *refs, n_pages, n_tok, k_sel, rows_pad):
    del pt_ref
    per = rows_pad // n_tok
    page_refs = refs[:per * n_pages]
    qi_ref, wm_ref, kin_ref, mask_ref, sc_ref, skey_ref = refs[per * n_pages:]
    b = pl.program_id(0)
    nb = pl.num_programs(0)
    L = sc_ref.shape[1]
    past = n_pages * PAGE_SIZE

    relu_s = []
    for e in range(per):
        kt_all = jnp.concatenate([r[...].astype(BF16) for r in page_refs[e * n_pages:(e + 1) * n_pages]]
                                 + [kin_ref[e].astype(BF16)], axis=1)
        relu_s.append(jnp.maximum(jnp.dot(qi_ref[e].astype(BF16), kt_all, preferred_element_type=F32), 0.0))
    relu_cat = jnp.concatenate(relu_s, axis=0)
    wm = wm_ref[0]
    r_hi, w_hi = relu_cat.astype(BF16), wm.astype(BF16)
    r_lo, w_lo = (relu_cat - r_hi.astype(F32)).astype(BF16), (wm - w_hi.astype(F32)).astype(BF16)
    score = (jnp.dot(w_hi, r_hi, preferred_element_type=F32) + jnp.dot(w_hi, r_lo, preferred_element_type=F32)
             + jnp.dot(w_lo, r_hi, preferred_element_type=F32))
    r0 = pl.multiple_of(b * rows_pad, rows_pad)
    sc_ref[pl.ds(r0, rows_pad), :] = score

    @pl.when(b == nb - 1)
    def _():
        n_blocks = sc_ref.shape[0] // TILE
        n_tiles = L // PAGE_SIZE
        tp = past + lax.broadcasted_iota(I32, (TILE, L), 0) % n_tok
        sp = lax.broadcasted_iota(I32, (TILE, L), 1)
        adm_blk = sp <= tp
        d_r = lax.broadcasted_iota(I32, (PAGE_SIZE, PAGE_SIZE * A_KV_HEADS), 0)
        d_c = lax.broadcasted_iota(I32, (PAGE_SIZE, PAGE_SIZE * A_KV_HEADS), 1)
        dup = jnp.where(d_c // A_KV_HEADS == d_r, 1.0, 0.0).astype(BF16)
        for rb in range(n_blocks):
            rows = slice(rb * TILE, (rb + 1) * TILE)
            skey_ref[...] = _sortable_key(jnp.where(adm_blk, sc_ref[rows, :], -jnp.inf))
            sel = jnp.where(_topk_member(skey_ref, k_sel) & adm_blk, 1.0, 0.0).astype(BF16)
            stacked = jnp.concatenate([sel[:, j * PAGE_SIZE:(j + 1) * PAGE_SIZE] for j in range(n_tiles)], axis=0)
            stacked = jnp.dot(stacked, dup, preferred_element_type=F32)
            mask_ref[rows, :] = jnp.concatenate(
                [stacked[j * TILE:(j + 1) * TILE, :] for j in range(n_tiles)], axis=1)


def dsa_sample_select(cache_ik_t, layer, page_table, qi_rows, wmat, ki_new_t, n_tok, k_sel):
    DB, n_pages = page_table.shape
    rows_pad = wmat.shape[1]
    per = rows_pad // n_tok
    n_rows = DB // per * rows_pad
    L = (n_pages + 1) * PAGE_SIZE
    page_specs = [pl.BlockSpec((None, None, IDX_DIM, PAGE_SIZE), functools.partial(
        lambda b, pt, e, p: (layer, pt[b * per + e, p], 0, 0), e=e, p=p))
        for e in range(per) for p in range(n_pages)]
    grid_spec = pltpu.PrefetchScalarGridSpec(
        num_scalar_prefetch=1,
        grid=(DB // per,),
        in_specs=page_specs + [
            pl.BlockSpec((per,) + qi_rows.shape[1:], lambda b, pt: (b, 0, 0)),
            pl.BlockSpec((1,) + wmat.shape[1:], lambda b, pt: (b, 0, 0)),
            pl.BlockSpec((per, IDX_DIM, PAGE_SIZE), lambda b, pt: (b, 0, 0))],
        out_specs=pl.BlockSpec((n_rows, A_KV_HEADS * L), lambda b, pt: (0, 0)),
        scratch_shapes=[pltpu.VMEM((n_rows, L), F32),
                        pltpu.VMEM((TILE, L), I32)],
    )
    return pl.pallas_call(
        functools.partial(_dsa_sample_select_kernel, n_pages=n_pages, n_tok=n_tok, k_sel=k_sel,
                          rows_pad=rows_pad),
        out_shape=jax.ShapeDtypeStruct((n_rows, A_KV_HEADS * L), F32),
        grid_spec=grid_spec,
        compiler_params=_cparams(("arbitrary",)),
        name="dsa_sample_select",
    )(page_table, *([cache_ik_t] * (per * n_pages)), qi_rows, wmat, ki_new_t)


def _dsa_sample_attend_kernel(pt_ref, *refs, n_pages, n_tok, rows_pad):
    del pt_ref
    per = rows_pad // n_tok
    k_refs = refs[:per * n_pages]
    v_refs = refs[per * n_pages:2 * per * n_pages]
    q_ref, kn_ref, vn_ref, mask_ref, bias_ref, qg_ref, o_ref = refs[2 * per * n_pages:]
    rows = n_tok * A_HEADS
    page_rows = PAGE_SIZE * A_KV_HEADS
    n_cols = mask_ref.shape[1]
    pad = jnp.zeros((page_rows - n_tok * A_KV_HEADS, HEAD_DIM), BF16)
    rep = A_HEADS // A_KV_HEADS
    grp = (lax.broadcasted_iota(I32, (rows, 1), 0) % A_HEADS) // rep
    own_group = (lax.broadcasted_iota(I32, (rows, n_cols), 1) % A_KV_HEADS) == grp
    member = mask_ref[...]

    for e in range(per):
        def tiles(page_refs, new_ref):
            new = jnp.concatenate([new_ref[e].astype(BF16), pad], axis=0)
            return [r[...].astype(BF16) for r in page_refs[e * n_pages:(e + 1) * n_pages]] + [new]

        q = (_rms(q_ref[e], qg_ref[...]) * HEAD_DIM ** -0.5).astype(BF16)
        sel = jnp.concatenate(
            [jnp.broadcast_to(member[e * n_tok + t:e * n_tok + t + 1, :], (A_HEADS, n_cols)) for t in range(n_tok)],
            axis=0)
        valid = (sel > 0.5) & own_group
        logits = jnp.concatenate(
            [lax.dot_general(q, kt, (((1,), (1,)), ((), ())), preferred_element_type=F32)
             for kt in tiles(k_refs, kn_ref)], axis=1)
        logits = jnp.where(valid, logits + bias_ref[...], NEG_BIG)
        m = jnp.max(logits, axis=-1, keepdims=True)
        p = jnp.exp(logits - m)
        den = jnp.sum(p, axis=-1, keepdims=True)
        pb = p.astype(BF16)
        o = jnp.zeros((rows, HEAD_DIM), F32)
        for j, vt in enumerate(tiles(v_refs, vn_ref)):
            o = o + jnp.dot(pb[:, j * page_rows:(j + 1) * page_rows], vt, preferred_element_type=F32)
        o_ref[e] = (o / den).astype(o_ref.dtype)


def dsa_sample_attend(cache_k, cache_v, layer, page_table, q_rows, k_new, v_new, mask, bias_tab, q_norm_g):
    DB, n_pages = page_table.shape
    n_tok = k_new.shape[1] // A_KV_HEADS
    rows = n_tok * A_HEADS
    rows_pad = SUBLANES
    per = rows_pad // n_tok
    n_cols = mask.shape[1]
    page_rows = PAGE_SIZE * A_KV_HEADS
    page_specs = [pl.BlockSpec((None, None, page_rows, HEAD_DIM), functools.partial(
        lambda b, pt, e, p: (layer, pt[b * per + e, p], 0, 0), e=e, p=p))
        for e in range(per) for p in range(n_pages)]
    grid_spec = pltpu.PrefetchScalarGridSpec(
        num_scalar_prefetch=1,
        grid=(DB // per,),
        in_specs=page_specs + page_specs + [
            pl.BlockSpec((per, rows, HEAD_DIM), lambda b, pt: (b, 0, 0)),
            pl.BlockSpec((per, n_tok * A_KV_HEADS, HEAD_DIM), lambda b, pt: (b, 0, 0)),
            pl.BlockSpec((per, n_tok * A_KV_HEADS, HEAD_DIM), lambda b, pt: (b, 0, 0)),
            pl.BlockSpec((rows_pad, n_cols), lambda b, pt: (b, 0)),
            pl.BlockSpec((rows, n_cols), lambda b, pt: (0, 0), pipeline_mode=pl.Buffered(1)),
            pl.BlockSpec((1, HEAD_DIM), lambda b, pt: (0, 0), pipeline_mode=pl.Buffered(1))],
        out_specs=pl.BlockSpec((per, rows, HEAD_DIM), lambda b, pt: (b, 0, 0)),
    )
    return pl.pallas_call(
        functools.partial(_dsa_sample_attend_kernel, n_pages=n_pages, n_tok=n_tok, rows_pad=rows_pad),
        out_shape=jax.ShapeDtypeStruct((DB, rows, HEAD_DIM), BF16),
        grid_spec=grid_spec,
        compiler_params=_cparams(("parallel",)),
        name="dsa_sample_attend",
    )(page_table, *([cache_k] * (per * n_pages)), *([cache_v] * (per * n_pages)), q_rows, k_new, v_new, mask,
      bias_tab, q_norm_g.reshape(1, HEAD_DIM))


def _log_sigmoid(z):
    return jnp.minimum(z, 0.0) - jnp.log(1.0 + jnp.exp(-jnp.abs(z)))


def _seg_masks(seg):
    r = lax.broadcasted_iota(I32, (TILE, TILE), 0)
    c = lax.broadcasted_iota(I32, (TILE, TILE), 1)
    return r, c, (r // seg) == (c // seg)


def _gla_levels(seg):
    w, out = seg // 2, []
    while w >= 1:
        out.append(w)
        w //= 2
    return out


def _gla_sum_matrices(seg):
    r = jnp.arange(TILE)[:, None]
    c = jnp.arange(TILE)[None, :]
    mats = []
    for w in _gla_levels(seg):
        same = (r // (2 * w)) == (c // (2 * w))
        r_right = (r % (2 * w)) >= w
        c_right = (c % (2 * w)) >= w
        mats.append(same & r_right & c_right & (c <= r))
    for w in _gla_levels(seg):
        same = (r // (2 * w)) == (c // (2 * w))
        r_right = (r % (2 * w)) >= w
        c_right = (c % (2 * w)) >= w
        mats.append(same & (~r_right) & (~c_right) & (c > r))
    same_seg = (r // seg) == (c // seg)
    mats.append(same_seg & (c <= r))
    mats.append(same_seg & (c > r))
    return jnp.concatenate(mats, axis=0).astype(BF16)


def _bdot(a, b):
    return jnp.dot(a.astype(BF16), b.astype(BF16), preferred_element_type=F32)


def _bdot_nt(a, b):
    return lax.dot_general(a.astype(BF16), b.astype(BF16), (((1,), (1,)), ((), ())), preferred_element_type=F32)


def _gla_common(tiles, wg_ref, bg_ref, mats_ref, seg):
    n = len(tiles)
    kw = B_HEADS * B_DK
    las = []
    for _, _, misc_ref in tiles:
        gb = misc_ref[:, MISC_GB:MISC_GB + GATE_RANK]
        z = jnp.dot(gb, wg_ref[...], precision=HIGHEST, preferred_element_type=F32) + bg_ref[...]
        las.append(_log_sigmoid(z) / GATE_TEMP)
    la = jnp.concatenate(las, axis=1)
    la_hi = la.astype(BF16)
    la_lo = (la - la_hi.astype(F32)).astype(BF16)
    mats = mats_ref[...]
    sums = (jnp.dot(mats, la_hi, preferred_element_type=F32) + jnp.dot(mats, la_lo, preferred_element_type=F32))
    levels = _gla_levels(seg)
    nl = len(levels)
    qs = [qb_ref[...] * B_DK ** -0.5 for qb_ref, _, _ in tiles]
    ks = [kb_ref[...] for _, kb_ref, _ in tiles]
    r, c, _ = _seg_masks(seg)
    atts = [[jnp.where(r == c, _bdot_nt(qs[i][:, h * B_DK:(h + 1) * B_DK], ks[i][:, h * B_DK:(h + 1) * B_DK]), 0.0)
             for h in range(B_HEADS)] for i in range(n)]
    for li, w in enumerate(levels):
        pair = ((r // (2 * w)) == (c // (2 * w))) & ((r % (2 * w)) >= w) & ((c % (2 * w)) < w)
        qd = [(qs[i] * jnp.exp(sums[li * TILE:(li + 1) * TILE, i * kw:(i + 1) * kw])).astype(BF16) for i in range(n)]
        kd = [(ks[i] * jnp.exp(sums[(nl + li) * TILE:(nl + li + 1) * TILE, i * kw:(i + 1) * kw])).astype(BF16)
              for i in range(n)]
        for h in range(B_HEADS):
            hs = slice(h * B_DK, (h + 1) * B_DK)
            for i in range(n):
                atts[i][h] = atts[i][h] + jnp.where(pair, _bdot_nt(qd[i][:, hs], kd[i][:, hs]), 0.0)
    out = []
    for i in range(n):
        cs = slice(i * kw, (i + 1) * kw)
        b_cum = sums[2 * nl * TILE:(2 * nl + 1) * TILE, cs]
        rem = sums[(2 * nl + 1) * TILE:(2 * nl + 2) * TILE, cs]
        out.append((qs[i], ks[i], atts[i], b_cum, rem))
    return out


def _gla_finish(o_heads, rb_ref, go_ref, o_ref):
    go = go_ref[...]
    for h in range(B_HEADS):
        vs = slice(h * B_DV, (h + 1) * B_DV)
        rb = rb_ref[:, vs]
        o_ref[:, vs] = (_rms(o_heads[h], go) * (rb * jax.nn.sigmoid(rb))).astype(o_ref.dtype)


def _gla_prompt_kernel(qb_ref, kb_ref, vb_ref, rb_ref, misc_ref, wg_ref, bg_ref, go_ref, mats_ref,
                       o_ref, s_ref, state_ref):
    ci = pl.program_id(0)

    @pl.when(ci == 0)
    def _():
        state_ref[...] = jnp.zeros_like(state_ref)

    nb = qb_ref.shape[0]
    common = _gla_common([(qb_ref.at[b], kb_ref.at[b], misc_ref.at[b]) for b in range(nb)],
                         wg_ref, bg_ref, mats_ref, TILE)
    vals = [vb_ref[b] for b in range(nb)]
    states = [state_ref[b] for b in range(nb)]
    qes = [common[b][0] * jnp.exp(common[b][3]) for b in range(nb)]
    o_heads = [[] for _ in range(nb)]
    for h in range(B_HEADS):
        ks = slice(h * B_DK, (h + 1) * B_DK)
        vs = slice(h * B_DV, (h + 1) * B_DV)
        for b in range(nb):
            o_heads[b].append(_bdot(qes[b][:, ks], states[b][ks, :]) + _bdot(common[b][2][h], vals[b][:, vs]))
    for b in range(nb):
        _gla_finish(o_heads[b], rb_ref.at[b], go_ref, o_ref.at[b])

    ke_ts = [(common[b][1] * jnp.exp(common[b][4])).T for b in range(nb)]
    e_cols = [jnp.broadcast_to(jnp.exp(common[b][3][TILE - 1:TILE, :]), (TILE, B_HEADS * B_DK)).T[:, 0:1]
              for b in range(nb)]
    for b in range(nb):
        upd = jnp.concatenate(
            [_bdot(ke_ts[b][h * B_DK:(h + 1) * B_DK, :], vals[b][:, h * B_DV:(h + 1) * B_DV])
             for h in range(B_HEADS)], axis=0)
        new_state = states[b] * e_cols[b] + upd
        state_ref[b] = new_state
        s_ref[b] = new_state


def gla_prompt(proj, w_gate, b_gate, g_out, n_batch, seq):
    nc = seq // TILE
    kwid = B_HEADS * B_DK
    vwid = B_HEADS * B_DV
    mats = _gla_sum_matrices(TILE)
    proj3 = proj.reshape(n_batch, seq, proj.shape[1])
    o, s = pl.pallas_call(
        _gla_prompt_kernel,
        out_shape=(jax.ShapeDtypeStruct((n_batch, seq, vwid), BF16),
                   jax.ShapeDtypeStruct((n_batch, kwid, B_DV), F32)),
        grid=(nc,),
        in_specs=[pl.BlockSpec((n_batch, TILE, kwid), lambda c: (0, c, QB_OFF // kwid)),
                  pl.BlockSpec((n_batch, TILE, kwid), lambda c: (0, c, KB_OFF // kwid)),
                  pl.BlockSpec((n_batch, TILE, vwid), lambda c: (0, c, VB_OFF // vwid)),
                  pl.BlockSpec((n_batch, TILE, vwid), lambda c: (0, c, RB_OFF // vwid)),
                  pl.BlockSpec((n_batch, TILE, LANES), lambda c: (0, c, MISC_OFF // LANES)),
                  _resident((GATE_RANK, kwid)),
                  _resident((1, kwid)),
                  _resident((1, B_DV)),
                  _resident(mats.shape)],
        out_specs=(pl.BlockSpec((n_batch, TILE, vwid), lambda c: (0, c, 0)),
                   pl.BlockSpec((n_batch, kwid, B_DV), lambda c: (0, 0, 0))),
        scratch_shapes=[pltpu.VMEM((n_batch, kwid, B_DV), F32)],
        compiler_params=_cparams(("arbitrary",)),
        name="gla_prompt",
    )(proj3, proj3, proj3, proj3, proj3, w_gate, b_gate.reshape(1, kwid), g_out.reshape(1, B_DV), mats)
    return o.reshape(n_batch * seq, vwid), s.reshape(n_batch, B_HEADS, B_DK, B_DV)


def _gla_sample_kernel(qb_ref, kb_ref, vb_ref, rb_ref, misc_ref, wg_ref, bg_ref, go_ref, mats_ref, s0_ref,
                       o_ref, s_ref, *, seg):
    nbt = TILE // seg
    (q, k, att, b_cum, rem), = _gla_common([(qb_ref, kb_ref, misc_ref)], wg_ref, bg_ref, mats_ref, seg)
    v = vb_ref[...]
    qe = q * jnp.exp(b_cum)
    ke = k * jnp.exp(rem)
    r1 = lax.broadcasted_iota(I32, (TILE, 1), 0)
    e_last = jnp.where(r1 % seg == seg - 1, jnp.exp(b_cum), 0.0)
    wide = nbt * B_DK
    mq = (lax.broadcasted_iota(I32, (TILE, wide), 0) // seg) == (lax.broadcasted_iota(I32, (TILE, wide), 1) // B_DK)
    mk = (lax.broadcasted_iota(I32, (wide, TILE), 0) // B_DK) == (lax.broadcasted_iota(I32, (wide, TILE), 1) // seg)
    o_heads = []
    for h in range(B_HEADS):
        ks = slice(h * B_DK, (h + 1) * B_DK)
        vs = slice(h * B_DV, (h + 1) * B_DV)
        state = s0_ref[:, h].reshape(wide, B_DV)
        q_bd = jnp.where(mq, jnp.concatenate([qe[:, ks]] * nbt, axis=1), 0.0)
        o_heads.append(_bdot(q_bd, state) + _bdot(att[h], v[:, vs]))
        pair_t = jnp.concatenate([ke[:, ks], e_last[:, ks]], axis=1).T
        k_bd = jnp.where(mk, jnp.concatenate([pair_t[:B_DK]] * nbt, axis=0), 0.0)
        e_bd = jnp.where(mk, jnp.concatenate([pair_t[B_DK:]] * nbt, axis=0), 0.0)
        e_col = jnp.sum(e_bd, axis=-1, keepdims=True)
        new_state = state * e_col + _bdot(k_bd, v[:, vs])
        s_ref[:, h] = new_state.reshape(nbt, B_DK, B_DV)
    _gla_finish(o_heads, rb_ref, go_ref, o_ref)


def gla_sample(proj, w_gate, b_gate, g_out, s0, layer, n_tok):
    T = proj.shape[0]
    nbt = TILE // n_tok
    kwid = B_HEADS * B_DK
    vwid = B_HEADS * B_DV
    mats = _gla_sum_matrices(n_tok)
    return pl.pallas_call(
        functools.partial(_gla_sample_kernel, seg=n_tok),
        out_shape=(jax.ShapeDtypeStruct((T, vwid), BF16),
                   jax.ShapeDtypeStruct(s0.shape[1:], F32)),
        grid=(T // TILE,),
        in_specs=[pl.BlockSpec((TILE, kwid), lambda i: (i, QB_OFF // kwid)),
                  pl.BlockSpec((TILE, kwid), lambda i: (i, KB_OFF // kwid)),
                  pl.BlockSpec((TILE, vwid), lambda i: (i, VB_OFF // vwid)),
                  pl.BlockSpec((TILE, vwid), lambda i: (i, RB_OFF // vwid)),
                  pl.BlockSpec((TILE, LANES), lambda i: (i, MISC_OFF // LANES)),
                  _resident((GATE_RANK, kwid)),
                  _resident((1, kwid)),
                  _resident((1, B_DV)),
                  _resident(mats.shape),
                  pl.BlockSpec((None, nbt, B_HEADS, B_DK, B_DV), lambda i: (layer, i, 0, 0, 0))],
        out_specs=(pl.BlockSpec((TILE, vwid), lambda i: (i, 0)),
                   pl.BlockSpec((nbt, B_HEADS, B_DK, B_DV), lambda i: (i, 0, 0, 0))),
        compiler_params=_cparams(("parallel",)),
        name="gla_sample",
    )(proj, proj, proj, proj, proj, w_gate, b_gate.reshape(1, kwid), g_out.reshape(1, B_DV), mats, s0)


def _gelu(x):
    return jax.nn.gelu(x)


def _gmlp_kernel(uc_ref, vc_ref, gv_ref, ws_ref, bcol_ref, o_ref, vn_ref=None, *, seg):
    r, c, same_seg = _seg_masks(seg)
    keep = same_seg & (c <= r)
    for t in range(uc_ref.shape[0] // TILE):
        rows = slice(t * TILE, (t + 1) * TILE)
        u = _gelu(uc_ref[rows, :])
        vg = _gelu(vc_ref[rows, :])
        for g in range(C_GROUPS):
            gs = slice(g * C_GROUP_DIM, (g + 1) * C_GROUP_DIM)
            vn = _rms(vg[:, gs], gv_ref[:, gs])
            if vn_ref is not None:
                vn_ref[rows, gs] = vn
            w = jnp.where(keep, ws_ref[g], 0.0).astype(BF16)
            s = jnp.dot(w, vn.astype(BF16), preferred_element_type=F32) + bcol_ref[:, g:g + 1]
            o_ref[rows, gs] = (u[:, gs] * s).astype(o_ref.dtype)


def gmlp(proj, g_v, w_tiles, b_cols, layer, seg, tm, with_vn):
    T = proj.shape[0]
    tm = min(tm, T)
    cw = C_GROUPS * C_GROUP_DIM
    out_shape = [jax.ShapeDtypeStruct((T, cw), BF16)]
    out_specs = [pl.BlockSpec((tm, cw), lambda i: (i, 0))]
    if with_vn:
        out_shape.append(jax.ShapeDtypeStruct((T, cw), F32))
        out_specs.append(pl.BlockSpec((tm, cw), lambda i: (i, 0)))
    return pl.pallas_call(
        functools.partial(_gmlp_kernel, seg=seg),
        out_shape=tuple(out_shape),
        grid=(T // tm,),
        in_specs=[pl.BlockSpec((tm, cw), lambda i: (i, UC_OFF // cw)),
                  pl.BlockSpec((tm, cw), lambda i: (i, VC_OFF // cw)),
                  _resident((1, cw)),
                  _layer_resident((C_GROUPS, TILE, TILE), layer),
                  _layer_resident((TILE, C_GROUPS), layer)],
        out_specs=tuple(out_specs),
        compiler_params=_cparams(("parallel",)),
        name="gmlp",
    )(proj, proj, g_v.reshape(1, cw), w_tiles, b_cols)


def _ffn_kernel(h_ref, oa_ref, ob_ref, oc_ref, wo_ref, g_ref, wg_ref, wu_ref, wd_ref, o_ref, n_ref):
    j = pl.program_id(1)

    @pl.when(j == 0)
    def _():
        aw = oa_ref.shape[1]
        bw = ob_ref.shape[1]
        h = h_ref[...] + jnp.dot(oa_ref[...], wo_ref[0:aw, :], preferred_element_type=F32)
        h = h + jnp.dot(ob_ref[...], wo_ref[aw:aw + bw, :], preferred_element_type=F32)
        h = h + jnp.dot(oc_ref[...], wo_ref[aw + bw:, :], preferred_element_type=F32)
        n_ref[...] = _rms(h, g_ref[...]).astype(BF16)
        o_ref[...] = h

    n = n_ref[...]
    a = jnp.dot(n, wg_ref[...], preferred_element_type=F32)
    u = jnp.dot(n, wu_ref[...], preferred_element_type=F32)
    act = (a * jax.nn.sigmoid(a) * u).astype(BF16)
    o_ref[...] += jnp.dot(act, wd_ref[...], preferred_element_type=F32)


def out_proj_ffn(h, o_a, o_b, o_c, w_out, g, w_gate, w_up, w_down, tm, tf):
    T, D = h.shape
    tm = min(tm, T)
    FF = w_gate.shape[1]
    return pl.pallas_call(
        _ffn_kernel,
        out_shape=jax.ShapeDtypeStruct((T, D), F32),
        grid=(T // tm, FF // tf),
        in_specs=[pl.BlockSpec((tm, D), lambda i, j: (i, 0)),
                  pl.BlockSpec((tm, o_a.shape[1]), lambda i, j: (i, 0)),
                  pl.BlockSpec((tm, o_b.shape[1]), lambda i, j: (i, 0)),
                  pl.BlockSpec((tm, o_c.shape[1]), lambda i, j: (i, 0)),
                  _resident(w_out.shape),
                  _resident((1, D)),
                  pl.BlockSpec((D, tf), lambda i, j: (0, j)),
                  pl.BlockSpec((D, tf), lambda i, j: (0, j)),
                  pl.BlockSpec((tf, D), lambda i, j: (j, 0))],
        out_specs=pl.BlockSpec((tm, D), lambda i, j: (i, 0)),
        scratch_shapes=[pltpu.VMEM((tm, D), BF16)],
        compiler_params=_cparams(("parallel", "arbitrary")),
        name="out_proj_ffn",
    )(h, o_a, o_b, o_c, w_out, g.reshape(1, D), w_gate, w_up, w_down)


def _ple_kernel(h_ref, p_ref, g_ref, wgate_ref, wproj_ref, o_ref):
    h = h_ref[...]
    n = _rms(h, g_ref[...]).astype(BF16)
    gate = jax.nn.sigmoid(jnp.dot(n, wgate_ref[...], preferred_element_type=F32))
    emb = jnp.dot(p_ref[...].astype(BF16), wproj_ref[...], preferred_element_type=F32)
    o_ref[...] = h + gate * emb


def ple(h, p, g, w_gate, w_proj, layer, tm):
    T, D = h.shape
    tm = min(tm, T)
    P = p.shape[2]
    return pl.pallas_call(
        _ple_kernel,
        out_shape=jax.ShapeDtypeStruct((T, D), F32),
        grid=(T // tm,),
        in_specs=[pl.BlockSpec((tm, D), lambda i: (i, 0)),
                  pl.BlockSpec((None, tm, P), lambda i: (layer, i, 0)),
                  _resident((1, D)),
                  _resident(w_gate.shape),
                  _resident(w_proj.shape)],
        out_specs=pl.BlockSpec((tm, D), lambda i: (i, 0)),
        compiler_params=_cparams(("parallel",)),
        name="ple",
    )(h, p, g.reshape(1, D), w_gate, w_proj)


_W_IN_SEGMENTS = (("q", 1024), ("k", 256), ("v", 256), ("qi", 1024), ("ki", 64), ("wi", 16), ("qb", 256),
                  ("kb", 256), ("vb", 512), ("gb", 16), ("rb", 512), ("uc", 512), ("vc", 512))
_W_IN_PACKED_ORDER = ("q", "qi", "vb", "rb", "uc", "vc", "k", "v", "qb", "kb", "ki", "wi", "gb")


def _pack_kernel(wt_ref, o_ref):
    src, start = {}, 0
    for name, size in _W_IN_SEGMENTS:
        src[name] = (start, size)
        start += size
    dst = 0
    small = []
    for name in _W_IN_PACKED_ORDER:
        s0, size = src[name]
        if size < LANES:
            small.append(wt_ref[s0:s0 + size, :])
            continue
        o_ref[:, dst:dst + size] = wt_ref[s0:s0 + size, :].T.astype(BF16)
        dst += size
    used = sum(x.shape[0] for x in small)
    small.append(jnp.zeros((LANES - used, wt_ref.shape[1]), F32))
    o_ref[:, dst:dst + LANES] = jnp.concatenate(small, axis=0).T.astype(BF16)


def _pack_w_in(w, tr=256):
    depth, D, N = w.shape
    return pl.pallas_call(
        _pack_kernel,
        out_shape=jax.ShapeDtypeStruct((depth, D, PROJ_PACKED), BF16),
        grid=(depth, D // tr),
        in_specs=[pl.BlockSpec((None, N, tr), lambda l, i: (l, 0, i))],
        out_specs=pl.BlockSpec((None, tr, PROJ_PACKED), lambda l, i: (l, i, 0)),
        compiler_params=_cparams(("parallel", "parallel")),
        name="pack_w_in",
    )(jnp.swapaxes(w, 1, 2))


def _mixer_tail(h, o_a, o_b, o_c, p_all, lw, layer):
    h = out_proj_ffn(h, o_a, o_b, o_c, lw["w_out"], lw["g_ffn"], lw["w_ffn_gate"], lw["w_ffn_up"],
                     lw["w_ffn_down"], TM_FFN, TF_FFN)
    return ple(h, p_all, lw["g_ple"], lw["w_ple_gate"], lw["w_ple_proj"], layer, TM_PLE)


def kernel(x_prompt, x_sample, cache_k, cache_v, cache_idx_k, state_gla, page_table, p_prompt, p_sample,
           g_mix, w_in, q_norm_g, k_norm_g, rel_bias, w_gate_b, b_gate_b, g_out_b, g_v_c, w_spatial,
           b_spatial, w_out, g_ffn, w_ffn_gate, w_ffn_up, w_ffn_down, g_ple, w_ple_gate, w_ple_proj):
    n_batch, seq, d_model = x_prompt.shape
    dec_batch, dec_seq, _ = x_sample.shape
    depth = w_in.shape[0]
    n_pages = page_table.shape[1]
    past = n_pages * PAGE_SIZE
    kw = A_KV_HEADS * HEAD_DIM
    tp, ts = n_batch * seq, dec_batch * dec_seq
    rows_pad = SUBLANES
    l_sample = past + PAGE_SIZE
    k_sel_s = min(TOPK_MAX, (past + dec_seq) // 4)

    bias_p = bias_table_prompt(rel_bias)
    bias_s = bias_table_sample(rel_bias, past, dec_seq, A_KV_HEADS * l_sample)
    cache_ik_t = jnp.swapaxes(cache_idx_k, 2, 3)
    cache_k2 = cache_k.reshape(depth, cache_k.shape[1], PAGE_SIZE * A_KV_HEADS, HEAD_DIM)
    cache_v2 = cache_v.reshape(depth, cache_v.shape[1], PAGE_SIZE * A_KV_HEADS, HEAD_DIM)

    hp = x_prompt.reshape(tp, d_model)
    hs = x_sample.reshape(ts, d_model)
    outs = {k: [] for k in ("kp", "vp", "ikp", "sp", "ks", "vs", "iks", "ss", "cs")}
    per_s = rows_pad // dec_seq
    place_t = (jnp.arange(rows_pad)[:, None, None]
               == jnp.arange(per_s)[None, :, None] * dec_seq + jnp.arange(dec_seq)[None, None, :]
               ).astype(F32)
    w_ple_proj_b = w_ple_proj.astype(BF16)
    w_packed = _pack_w_in(w_in)
    pp_all = p_prompt.reshape(depth, tp, -1)
    ps_all = p_sample.reshape(depth, ts, -1)
    b_cols_p = jnp.swapaxes(b_spatial, 1, 2)
    reps = TILE // dec_seq
    w_tiles_s = jnp.tile(w_spatial[:, :, :dec_seq, :dec_seq], (1, 1, reps, reps))
    b_cols_s = jnp.tile(jnp.swapaxes(b_spatial[:, :, :dec_seq], 1, 2), (1, reps, 1))
    for i in range(depth):
        proj, (wo_b, wg_b, wu_b, wd_b, wpg_b) = in_projection(
            hp, g_mix[i], w_packed, i, TM_PROJ, cast=(w_out, w_ffn_gate, w_ffn_up, w_ffn_down, w_ple_gate))
        lw = dict(w_out=wo_b, g_ffn=g_ffn[i], w_ffn_gate=wg_b, w_ffn_up=wu_b, w_ffn_down=wd_b,
                  g_ple=g_ple[i], w_ple_gate=wpg_b, w_ple_proj=w_ple_proj_b[i])
        kn, vv, ik, vt = kv_post(proj, k_norm_g[i], TM_KV, True)
        o_a = dsa_prompt(proj, kn, vt, bias_p, q_norm_g[i], n_batch, seq)
        o_b, s_p = gla_prompt(proj, w_gate_b[i], b_gate_b[i], g_out_b[i], n_batch, seq)
        o_c, = gmlp(proj, g_v_c[i], w_spatial, b_cols_p, i, TILE, TM_GMLP, False)
        hp = _mixer_tail(hp, o_a, o_b, o_c, pp_all, lw, i)
        outs["kp"].append(kn.reshape(n_batch, seq, A_KV_HEADS, HEAD_DIM))
        outs["vp"].append(vv.reshape(n_batch, seq, A_KV_HEADS, HEAD_DIM))
        outs["ikp"].append(ik.reshape(n_batch, seq, IDX_DIM))
        outs["sp"].append(s_p)

        proj, _ = in_projection(hs, g_mix[i], w_packed, i, TM_PROJ)
        kn, vv, ik = kv_post(proj, k_norm_g[i], TM_KV, False)
        qi_rows = proj[:, QI_OFF:QI_OFF + IDX_HEADS * IDX_DIM].reshape(dec_batch, dec_seq * IDX_HEADS, IDX_DIM)
        wi = proj[:, MISC_OFF + MISC_WI:MISC_OFF + MISC_WI + IDX_HEADS].reshape(dec_batch, dec_seq, IDX_HEADS)
        wi = wi * (IDX_HEADS ** -0.5 * IDX_DIM ** -0.5)
        wmat = (place_t[None, :, :, :, None] * wi.reshape(dec_batch // per_s, 1, per_s, dec_seq, IDX_HEADS)
                ).reshape(dec_batch // per_s, rows_pad, per_s * dec_seq * IDX_HEADS)
        ki_new_t = jnp.pad(jnp.swapaxes(ik.reshape(dec_batch, dec_seq, IDX_DIM), 1, 2),
                           ((0, 0), (0, 0), (0, PAGE_SIZE - dec_seq)))
        mask = dsa_sample_select(cache_ik_t, i, page_table, qi_rows, wmat, ki_new_t, dec_seq, k_sel_s)
        q_rows = proj[:, Q_OFF:Q_OFF + A_HEADS * HEAD_DIM].reshape(dec_batch, dec_seq * A_HEADS, HEAD_DIM)
        o_a = dsa_sample_attend(cache_k2, cache_v2, i, page_table, q_rows,
                                kn.reshape(dec_batch, dec_seq * A_KV_HEADS, HEAD_DIM),
                                vv.reshape(dec_batch, dec_seq * A_KV_HEADS, HEAD_DIM), mask, bias_s, q_norm_g[i])
        o_a = o_a.reshape(ts, A_HEADS * HEAD_DIM)
        o_b, s_s = gla_sample(proj, w_gate_b[i], b_gate_b[i], g_out_b[i], state_gla, i, dec_seq)
        o_c, vn = gmlp(proj, g_v_c[i], w_tiles_s, b_cols_s, i, dec_seq, TM_GMLP, True)
        hs = _mixer_tail(hs, o_a, o_b, o_c, ps_all, lw, i)
        outs["ks"].append(kn.reshape(dec_batch, dec_seq, A_KV_HEADS, HEAD_DIM))
        outs["vs"].append(vv.reshape(dec_batch, dec_seq, A_KV_HEADS, HEAD_DIM))
        outs["iks"].append(ik.reshape(dec_batch, dec_seq, IDX_DIM))
        outs["ss"].append(s_s)
        outs["cs"].append(vn.reshape(dec_batch, dec_seq, -1))

    st = {k: jnp.stack(v) for k, v in outs.items()}
    return (hp.reshape(n_batch, seq, d_model), hs.reshape(dec_batch, dec_seq, d_model),
            st["kp"], st["vp"], st["ikp"], st["sp"], st["ks"], st["vs"], st["iks"], st["ss"], st["cs"])
```

```python
import functools
import math

import jax
import jax.numpy as jnp
from jax import lax
from jax.experimental import pallas as pl
from jax.experimental.pallas import tpu as pltpu

F32 = jnp.float32
BF16 = jnp.bfloat16
I32 = jnp.int32
HIGHEST = lax.Precision.HIGHEST

LANES = 128
SUBLANES = 8
VMEM_LIMIT = 56 * 1024 * 1024

HEAD_DIM = 128
A_HEADS = 8
A_KV_HEADS = 2
IDX_HEADS = 16
IDX_DIM = 64
TOPK_MAX = 256
NUM_BUCKETS = 32
MAX_DISTANCE = 128
B_HEADS = 4
B_DK = 64
B_DV = 128
GATE_RANK = 16
GATE_TEMP = 16.0
C_GROUPS = 4
C_GROUP_DIM = 128
PAGE_SIZE = 128
EPS = 1e-6
NEG_BIG = -1e30
INT_MIN = -(2 ** 31)
NEG_INF_KEY = -2139095041

TILE = 128
QBLK = 256
TM_PROJ = 256
TM_GMLP = 512
TM_FFN = 512
TF_FFN = 512
TM_PLE = 512

Q_OFF, QI_OFF, VB_OFF, RB_OFF, UC_OFF, VC_OFF = 0, 1024, 2048, 2560, 3072, 3584
K_OFF, V_OFF, QB_OFF, KB_OFF, MISC_OFF = 4096, 4352, 4608, 4864, 5120
PROJ_PACKED = 5248
MISC_KI, MISC_WI, MISC_GB = 0, 64, 80


def _cparams(sem):
    return pltpu.CompilerParams(dimension_semantics=sem, vmem_limit_bytes=VMEM_LIMIT)


def _rms(x, g):
    return x * lax.rsqrt(jnp.mean(x * x, axis=-1, keepdims=True) + EPS) * g


def _resident(shape):
    nd = len(shape)
    return pl.BlockSpec(shape, lambda *_: (0,) * nd, pipeline_mode=pl.Buffered(1))


def _layer_resident(shape, layer):
    nd = len(shape)
    return pl.BlockSpec((None,) + tuple(shape), lambda *_: (layer,) + (0,) * nd, pipeline_mode=pl.Buffered(1))


def _proj_kernel(x_ref, g_ref, w_ref, kg_ref, *refs, n_cast, with_vt):
    cast_in, refs = refs[:n_cast], refs[n_cast:]
    o_ref, ko_ref, vo_ref, io_ref = refs[:4]
    vt_ref = refs[4] if with_vt else None
    cast_out = refs[5 if with_vt else 4:]
    n = _rms(x_ref[...], g_ref[...]).astype(BF16)
    ncol = o_ref.shape[1]
    step = 512
    for c0 in range(0, ncol, step):
        c1 = min(c0 + step, ncol)
        o_ref[:, c0:c1] = jnp.dot(n, w_ref[:, c0:c1], preferred_element_type=F32)
    kg = kg_ref[...]
    for hh in range(A_KV_HEADS):
        ko_ref[:, hh * HEAD_DIM:(hh + 1) * HEAD_DIM] = _rms(
            o_ref[:, K_OFF + hh * HEAD_DIM:K_OFF + (hh + 1) * HEAD_DIM], kg)
    v = o_ref[:, V_OFF:V_OFF + A_KV_HEADS * HEAD_DIM]
    vo_ref[...] = v
    io_ref[...] = o_ref[:, MISC_OFF + MISC_KI:MISC_OFF + MISC_KI + IDX_DIM]
    if with_vt:
        for blk in range(vt_ref.shape[0]):
            vt_ref[blk] = v[blk * QBLK:(blk + 1) * QBLK, :].T.astype(vt_ref.dtype)
    for src, dst in zip(cast_in, cast_out):
        dst[...] = src[...].astype(dst.dtype)


def in_projection(h, g, w_packed, k_norm_g, layer, tm, with_vt, cast=()):
    T, D = h.shape
    tm = min(tm, T)
    N = w_packed.shape[2]
    kw = A_KV_HEADS * HEAD_DIM
    steps = T // tm
    cast_specs_in = [pl.BlockSpec((None, w.shape[1] // steps, w.shape[2]), lambda i: (layer, i, 0)) for w in cast]
    cast_specs_out = [pl.BlockSpec((w.shape[1] // steps, w.shape[2]), lambda i: (i, 0)) for w in cast]
    kv_shape = [jax.ShapeDtypeStruct((T, kw), F32),
                jax.ShapeDtypeStruct((T, kw), F32),
                jax.ShapeDtypeStruct((T, IDX_DIM), F32)]
    kv_specs = [pl.BlockSpec((tm, kw), lambda i: (i, 0)),
                pl.BlockSpec((tm, kw), lambda i: (i, 0)),
                pl.BlockSpec((tm, IDX_DIM), lambda i: (i, 0))]
    if with_vt:
        kv_shape.append(jax.ShapeDtypeStruct((T // QBLK, kw, QBLK), BF16))
        kv_specs.append(pl.BlockSpec((tm // QBLK, kw, QBLK), lambda i: (i, 0, 0)))
    outs = pl.pallas_call(
        functools.partial(_proj_kernel, n_cast=len(cast), with_vt=with_vt),
        out_shape=(jax.ShapeDtypeStruct((T, N), F32),) + tuple(kv_shape) + tuple(
            jax.ShapeDtypeStruct(w.shape[1:], BF16) for w in cast),
        grid=(steps,),
        in_specs=[pl.BlockSpec((tm, D), lambda i: (i, 0)),
                  _resident((1, D)),
                  _layer_resident((D, N), layer),
                  _resident((1, HEAD_DIM))] + cast_specs_in,
        out_specs=(pl.BlockSpec((tm, N), lambda i: (i, 0)),) + tuple(kv_specs) + tuple(cast_specs_out),
        compiler_params=_cparams(("parallel",)),
        name="in_projection",
    )(h, g.reshape(1, D), w_packed, k_norm_g.reshape(1, HEAD_DIM), *cast)
    n_kv = len(kv_shape)
    return outs[0], outs[1:1 + n_kv], outs[1 + n_kv:]


def _bucket(dist):
    n = jnp.maximum(dist, 0)
    max_exact = NUM_BUCKETS // 2
    large = max_exact + (jnp.log(jnp.maximum(n, 1).astype(F32) / max_exact)
                         / math.log(MAX_DISTANCE / max_exact)
                         * (NUM_BUCKETS - max_exact)).astype(I32)
    large = jnp.minimum(large, NUM_BUCKETS - 1)
    return jnp.where(n < max_exact, n, large)


def _bias_prompt_kernel(rb_ref, o_ref):
    c = lax.broadcasted_iota(I32, (TILE, TILE), 0)
    t = lax.broadcasted_iota(I32, (TILE, TILE), 1)
    for z in range(3):
        bucket = _bucket(t - c + (2 - z) * TILE)
        for h in range(A_HEADS):
            acc = jnp.zeros((TILE, TILE), F32)
            for b in range(NUM_BUCKETS):
                acc = jnp.where(bucket == b, rb_ref[b, h], acc)
            o_ref[h, z] = acc


def bias_table_prompt(rel_bias):
    return pl.pallas_call(
        _bias_prompt_kernel,
        out_shape=jax.ShapeDtypeStruct((A_HEADS, 3, TILE, TILE), F32),
        in_specs=[pl.BlockSpec(memory_space=pltpu.SMEM)],
        out_specs=pl.BlockSpec(memory_space=pltpu.VMEM),
        name="bias_table_prompt",
    )(rel_bias)


def _bias_sample_kernel(rbrows_ref, o_ref, *, past, n_tok):
    rows, L = o_ref.shape
    r = lax.broadcasted_iota(I32, (rows, L), 0)
    s = lax.broadcasted_iota(I32, (rows, L), 1) // A_KV_HEADS
    bucket = _bucket(past + r // A_HEADS - s)
    rbrows = rbrows_ref[...]
    acc = jnp.zeros((rows, L), F32)
    for b in range(NUM_BUCKETS):
        acc = jnp.where(bucket == b, rbrows[:, b:b + 1], acc)
    o_ref[...] = acc


def bias_table_sample(rel_bias, past, n_tok, L):
    rows = n_tok * A_HEADS
    rbrows = jnp.tile(rel_bias.T, (n_tok, 1))
    return pl.pallas_call(
        functools.partial(_bias_sample_kernel, past=past, n_tok=n_tok),
        out_shape=jax.ShapeDtypeStruct((rows, L), F32),
        name="bias_table_sample",
    )(rbrows)


def _sortable_key(x):
    b = lax.bitcast_convert_type(x, I32)
    return b ^ ((b >> 31) & 0x7FFFFFFF)


def _topk_member(skey_ref, k_sel):
    R, L = skey_ref.shape

    def body(it, ans):
        bit = 31 - it
        cand = ans | lax.shift_left(jnp.int32(1), bit)
        cand_s = cand ^ INT_MIN
        cnt = jnp.sum(jnp.where(skey_ref[...] >= cand_s, 1.0, 0.0), axis=-1, keepdims=True)
        return jnp.where(cnt >= k_sel, cand, ans)

    ans = lax.fori_loop(0, 32, body, jnp.zeros((R, 1), I32))
    tau = ans ^ INT_MIN
    skey = skey_ref[...]
    gt = skey > tau
    eq = skey == tau
    n_gt = jnp.sum(jnp.where(gt, 1.0, 0.0), axis=-1, keepdims=True)
    room = k_sel - n_gt
    r_i = lax.broadcasted_iota(I32, (LANES, LANES), 0)
    c_i = lax.broadcasted_iota(I32, (LANES, LANES), 1)
    upper = jnp.where(r_i <= c_i, 1.0, 0.0).astype(BF16)
    off = jnp.zeros((R, 1), F32)
    parts = []
    for j in range(L // LANES):
        sl = slice(j * LANES, (j + 1) * LANES)
        eq_j = eq[:, sl]
        run = jnp.dot(jnp.where(eq_j, 1.0, 0.0).astype(BF16), upper, preferred_element_type=F32) + off
        parts.append(gt[:, sl] | (eq_j & (run <= room)))
        off = run[:, LANES - 1:LANES]
    return jnp.concatenate(parts, axis=1)


def _fold8(x, op):
    return op(x.reshape(x.shape[0] // SUBLANES, SUBLANES, x.shape[1]), axis=0)


def _dsa_prompt_kernel(q_ref, qi_ref, misc_ref, kn_ref, vt_ref, bias_ref, qg_ref, o_ref,
                       qst_ref, skey_ref, madd_ref, lg_ref, acc_ref, *, k_sel):
    i = pl.program_id(1)
    nkb = i + 1
    sub = QBLK // TILE
    rep = A_HEADS // A_KV_HEADS
    row0 = pl.multiple_of(i * QBLK, QBLK)
    s_iota = lax.broadcasted_iota(I32, (QBLK, QBLK), 0)
    t_iota = lax.broadcasted_iota(I32, (QBLK, QBLK), 1)

    def admissible(j):
        return (j * QBLK + s_iota) <= (row0 + t_iota)

    wi_t = misc_ref[pl.ds(row0, QBLK), :].T[MISC_WI:MISC_WI + IDX_HEADS, :]
    wi_t = wi_t * (IDX_HEADS ** -0.5 * IDX_DIM ** -0.5)
    for h in range(IDX_HEADS):
        qst_ref[h * QBLK:(h + 1) * QBLK, :] = qi_ref[:, h * IDX_DIM:(h + 1) * IDX_DIM].astype(BF16)

    def score_body(j, carry):
        k0 = pl.multiple_of(j * QBLK, QBLK)
        kj = misc_ref[pl.ds(k0, QBLK), MISC_KI:MISC_KI + IDX_DIM].astype(BF16)
        s = lax.dot_general(kj, qst_ref[...], (((1,), (1,)), ((), ())), preferred_element_type=F32)
        score = jnp.zeros((QBLK, QBLK), F32)
        for h in range(IDX_HEADS):
            score = score + jnp.maximum(s[:, h * QBLK:(h + 1) * QBLK], 0.0) * wi_t[h:h + 1, :]
        skey_ref[j] = _sortable_key(jnp.where(admissible(j), score, -jnp.inf))
        return carry

    lax.fori_loop(0, nkb, score_body, 0)

    def count(pred_fn):
        def hits(j):
            return _fold8(jnp.where(pred_fn(skey_ref[j]), 1.0, 0.0), jnp.sum)

        def body(jj, accs):
            return accs[0] + hits(2 * jj), accs[1] + hits(2 * jj + 1)

        zero = jnp.zeros((SUBLANES, QBLK), F32)
        acc0, acc1 = lax.fori_loop(0, nkb // 2, body, (zero, zero))
        acc = lax.cond(nkb % 2 == 1, lambda: acc0 + acc1 + hits(nkb - 1), lambda: acc0 + acc1)
        return jnp.sum(acc, axis=0, keepdims=True)

    def bit_body(it, ans):
        cand = ans | lax.shift_left(jnp.int32(1), 31 - it)
        cand_s = cand ^ INT_MIN
        cnt = count(lambda key: key >= cand_s)
        return jnp.where(cnt >= k_sel, cand, ans)

    ans = lax.fori_loop(0, 32, bit_body, jnp.zeros((1, QBLK), I32))
    tau = ans ^ INT_MIN
    n_ge = count(lambda key: key >= tau)
    excess = jnp.max(jnp.where((n_ge > k_sel) & (tau != NEG_INF_KEY), 1.0, 0.0))

    @pl.when(excess == 0.0)
    def _():
        def mask_body(j, carry):
            madd_ref[j] = jnp.where((skey_ref[j] >= tau) & admissible(j), 0.0, NEG_BIG)
            return carry

        lax.fori_loop(0, nkb, mask_body, 0)

    @pl.when(excess > 0.0)
    def _():
        room = k_sel - count(lambda key: key > tau)
        lower = jnp.where(t_iota <= s_iota, 1.0, 0.0).astype(BF16)

        def mask_body(j, off):
            key = skey_ref[j]
            eq = key == tau
            run = jnp.dot(lower, jnp.where(eq, 1.0, 0.0).astype(BF16), preferred_element_type=F32) + off
            sel = ((key > tau) | (eq & (run <= room))) & admissible(j)
            madd_ref[j] = jnp.where(sel, 0.0, NEG_BIG)
            return run[QBLK - 1:QBLK, :]

        lax.fori_loop(0, nkb, mask_body, jnp.zeros((1, QBLK), F32))

    qg = qg_ref[...]
    wide = rep * QBLK
    for g in range(A_KV_HEADS):
        gs = slice(g * HEAD_DIM, (g + 1) * HEAD_DIM)
        heads = list(range(g * rep, (g + 1) * rep))
        q_stack = jnp.concatenate(
            [(_rms(q_ref[:, h * HEAD_DIM:(h + 1) * HEAD_DIM], qg) * HEAD_DIM ** -0.5).astype(BF16) for h in heads],
            axis=0)

        def logit_body(j, mx):
            k0 = pl.multiple_of(j * QBLK, QBLK)
            kj = kn_ref[pl.ds(k0, QBLK), gs].astype(BF16)
            lg = lax.dot_general(kj, q_stack, (((1,), (1,)), ((), ())), preferred_element_type=F32)
            madd = madd_ref[j]
            parts = []
            for r, h in enumerate(heads):
                quads = []
                for c in range(sub):
                    quads.append(jnp.concatenate(
                        [bias_ref[h, jnp.clip(2 - ((i - j) * sub + u - c), 0, 2)] for u in range(sub)], axis=1))
                parts.append(lg[:, r * QBLK:(r + 1) * QBLK] + jnp.concatenate(quads, axis=0) + madd)
            lg = jnp.concatenate(parts, axis=1)
            lg_ref[j] = lg
            return jnp.maximum(mx, _fold8(lg, jnp.max))

        mx = lax.fori_loop(0, nkb, logit_body, jnp.full((SUBLANES, wide), NEG_BIG, F32))
        m = jnp.max(mx, axis=0, keepdims=True)
        acc_ref[...] = jnp.zeros(acc_ref.shape, F32)

        def pv_body(j, sm):
            p = jnp.exp(lg_ref[j] - m)
            acc_ref[...] += jnp.dot(vt_ref[j, gs, :], p.astype(BF16), preferred_element_type=F32)
            return sm + _fold8(p, jnp.sum)

        sm = lax.fori_loop(0, nkb, pv_body, jnp.zeros((SUBLANES, wide), F32))
        den = jnp.sum(sm, axis=0, keepdims=True)
        o = (acc_ref[...] / den).T
        for r, h in enumerate(heads):
            o_ref[:, h * HEAD_DIM:(h + 1) * HEAD_DIM] = o[r * QBLK:(r + 1) * QBLK, :].astype(o_ref.dtype)


def dsa_prompt(proj, kn, vt, bias_tab, q_norm_g, n_batch, seq):
    T = proj.shape[0]
    nb = seq // QBLK
    k_sel = min(TOPK_MAX, seq // 4)
    aw = A_HEADS * HEAD_DIM
    iw = IDX_HEADS * IDX_DIM
    kw = A_KV_HEADS * HEAD_DIM
    rep = A_HEADS // A_KV_HEADS
    return pl.pallas_call(
        functools.partial(_dsa_prompt_kernel, k_sel=k_sel),
        out_shape=jax.ShapeDtypeStruct((T, aw), BF16),
        grid=(n_batch, nb),
        in_specs=[pl.BlockSpec((QBLK, aw), lambda b, i: (b * nb + i, Q_OFF // aw)),
                  pl.BlockSpec((QBLK, iw), lambda b, i: (b * nb + i, QI_OFF // iw)),
                  pl.BlockSpec((seq, LANES), lambda b, i: (b, MISC_OFF // LANES)),
                  pl.BlockSpec((seq, kw), lambda b, i: (b, 0)),
                  pl.BlockSpec((nb, kw, QBLK), lambda b, i: (b, 0, 0)),
                  _resident((A_HEADS, 3, TILE, TILE)),
                  _resident((1, HEAD_DIM))],
        out_specs=pl.BlockSpec((QBLK, aw), lambda b, i: (b * nb + i, 0)),
        scratch_shapes=[pltpu.VMEM((IDX_HEADS * QBLK, IDX_DIM), BF16),
                        pltpu.VMEM((nb, QBLK, QBLK), I32),
                        pltpu.VMEM((nb, QBLK, QBLK), F32),
                        pltpu.VMEM((nb, QBLK, rep * QBLK), F32),
                        pltpu.VMEM((HEAD_DIM, rep * QBLK), F32)],
        compiler_params=_cparams(("parallel", "arbitrary")),
        name="dsa_prompt",
    )(proj, proj, proj, kn, vt, bias_tab, q_norm_g.reshape(1, HEAD_DIM))


def _dsa_sample_select_kernel(pt_ref, *refs, n_pages, n_tok, k_sel, rows_pad):
    del pt_ref
    per = rows_pad // n_tok
    page_refs = refs[:per * n_pages]
    qi_ref, wm_ref, kin_ref, mask_ref, sc_ref, skey_ref = refs[per * n_pages:]
    b = pl.program_id(0)
    nb = pl.num_programs(0)
    L = sc_ref.shape[1]
    past = n_pages * PAGE_SIZE

    relu_s = []
    for e in range(per):
        kt_all = jnp.concatenate([r[...].astype(BF16) for r in page_refs[e * n_pages:(e + 1) * n_pages]]
                                 + [kin_ref[e].astype(BF16)], axis=1)
        relu_s.append(jnp.maximum(jnp.dot(qi_ref[e].astype(BF16), kt_all, preferred_element_type=F32), 0.0))
    relu_cat = jnp.concatenate(relu_s, axis=0)
    wm = wm_ref[0]
    r_hi, w_hi = relu_cat.astype(BF16), wm.astype(BF16)
    r_lo, w_lo = (relu_cat - r_hi.astype(F32)).astype(BF16), (wm - w_hi.astype(F32)).astype(BF16)
    score = (jnp.dot(w_hi, r_hi, preferred_element_type=F32) + jnp.dot(w_hi, r_lo, preferred_element_type=F32)
             + jnp.dot(w_lo, r_hi, preferred_element_type=F32))
    r0 = pl.multiple_of(b * rows_pad, rows_pad)
    sc_ref[pl.ds(r0, rows_pad), :] = score

    @pl.when(b == nb - 1)
    def _():
        n_blocks = sc_ref.shape[0] // TILE
        n_tiles = L // PAGE_SIZE
        tp = past + lax.broadcasted_iota(I32, (TILE, L), 0) % n_tok
        sp = lax.broadcasted_iota(I32, (TILE, L), 1)
        adm_blk = sp <= tp
        d_r = lax.broadcasted_iota(I32, (PAGE_SIZE, PAGE_SIZE * A_KV_HEADS), 0)
        d_c = lax.broadcasted_iota(I32, (PAGE_SIZE, PAGE_SIZE * A_KV_HEADS), 1)
        dup = jnp.where(d_c // A_KV_HEADS == d_r, 1.0, 0.0).astype(BF16)
        for rb in range(n_blocks):
            rows = slice(rb * TILE, (rb + 1) * TILE)
            skey_ref[...] = _sortable_key(jnp.where(adm_blk, sc_ref[rows, :], -jnp.inf))
            sel = jnp.where(_topk_member(skey_ref, k_sel) & adm_blk, 1.0, 0.0).astype(BF16)
            stacked = jnp.concatenate([sel[:, j * PAGE_SIZE:(j + 1) * PAGE_SIZE] for j in range(n_tiles)], axis=0)
            stacked = jnp.dot(stacked, dup, preferred_element_type=F32)
            mask_ref[rows, :] = jnp.concatenate(
                [stacked[j * TILE:(j + 1) * TILE, :] for j in range(n_tiles)], axis=1)


def dsa_sample_select(cache_ik_t, layer, page_table, qi_rows, wmat, ki_new_t, n_tok, k_sel):
    DB, n_pages = page_table.shape
    rows_pad = wmat.shape[1]
    per = rows_pad // n_tok
    n_rows = DB // per * rows_pad
    L = (n_pages + 1) * PAGE_SIZE
    page_specs = [pl.BlockSpec((None, None, IDX_DIM, PAGE_SIZE), functools.partial(
        lambda b, pt, e, p: (layer, pt[b * per + e, p], 0, 0), e=e, p=p))
        for e in range(per) for p in range(n_pages)]
    grid_spec = pltpu.PrefetchScalarGridSpec(
        num_scalar_prefetch=1,
        grid=(DB // per,),
        in_specs=page_specs + [
            pl.BlockSpec((per,) + qi_rows.shape[1:], lambda b, pt: (b, 0, 0)),
            pl.BlockSpec((1,) + wmat.shape[1:], lambda b, pt: (b, 0, 0)),
            pl.BlockSpec((per, IDX_DIM, PAGE_SIZE), lambda b, pt: (b, 0, 0))],
        out_specs=pl.BlockSpec((n_rows, A_KV_HEADS * L), lambda b, pt: (0, 0)),
        scratch_shapes=[pltpu.VMEM((n_rows, L), F32),
                        pltpu.VMEM((TILE, L), I32)],
    )
    return pl.pallas_call(
        functools.partial(_dsa_sample_select_kernel, n_pages=n_pages, n_tok=n_tok, k_sel=k_sel,
                          rows_pad=rows_pad),
        out_shape=jax.ShapeDtypeStruct((n_rows, A_KV_HEADS * L), F32),
        grid_spec=grid_spec,
        compiler_params=_cparams(("arbitrary",)),
        name="dsa_sample_select",
    )(page_table, *([cache_ik_t] * (per * n_pages)), qi_rows, wmat, ki_new_t)


def _dsa_sample_attend_kernel(pt_ref, *refs, n_pages, n_tok, rows_pad):
    del pt_ref
    per = rows_pad // n_tok
    k_refs = refs[:per * n_pages]
    v_refs = refs[per * n_pages:2 * per * n_pages]
    q_ref, kn_ref, vn_ref, mask_ref, bias_ref, qg_ref, o_ref = refs[2 * per * n_pages:]
    rows = n_tok * A_HEADS
    page_rows = PAGE_SIZE * A_KV_HEADS
    n_cols = mask_ref.shape[1]
    pad = jnp.zeros((page_rows - n_tok * A_KV_HEADS, HEAD_DIM), BF16)
    rep = A_HEADS // A_KV_HEADS
    grp = (lax.broadcasted_iota(I32, (rows, 1), 0) % A_HEADS) // rep
    own_group = (lax.broadcasted_iota(I32, (rows, n_cols), 1) % A_KV_HEADS) == grp
    member = mask_ref[...]

    for e in range(per):
        def tiles(page_refs, new_ref):
            new = jnp.concatenate([new_ref[e].astype(BF16), pad], axis=0)
            return [r[...].astype(BF16) for r in page_refs[e * n_pages:(e + 1) * n_pages]] + [new]

        q = (_rms(q_ref[e], qg_ref[...]) * HEAD_DIM ** -0.5).astype(BF16)
        sel = jnp.concatenate(
            [jnp.broadcast_to(member[e * n_tok + t:e * n_tok + t + 1, :], (A_HEADS, n_cols)) for t in range(n_tok)],
            axis=0)
        valid = (sel > 0.5) & own_group
        logits = jnp.concatenate(
            [lax.dot_general(q, kt, (((1,), (1,)), ((), ())), preferred_element_type=F32)
             for kt in tiles(k_refs, kn_ref)], axis=1)
        logits = jnp.where(valid, logits + bias_ref[...], NEG_BIG)
        m = jnp.max(logits, axis=-1, keepdims=True)
        p = jnp.exp(logits - m)
        den = jnp.sum(p, axis=-1, keepdims=True)
        pb = p.astype(BF16)
        o = jnp.zeros((rows, HEAD_DIM), F32)
        for j, vt in enumerate(tiles(v_refs, vn_ref)):
            o = o + jnp.dot(pb[:, j * page_rows:(j + 1) * page_rows], vt, preferred_element_type=F32)
        o_ref[e] = (o / den).astype(o_ref.dtype)


def dsa_sample_attend(cache_k, cache_v, layer, page_table, q_rows, k_new, v_new, mask, bias_tab, q_norm_g):
    DB, n_pages = page_table.shape
    n_tok = k_new.shape[1] // A_KV_HEADS
    rows = n_tok * A_HEADS
    rows_pad = SUBLANES
    per = rows_pad // n_tok
    n_cols = mask.shape[1]
    page_rows = PAGE_SIZE * A_KV_HEADS
    page_specs = [pl.BlockSpec((None, None, page_rows, HEAD_DIM), functools.partial(
        lambda b, pt, e, p: (layer, pt[b * per + e, p], 0, 0), e=e, p=p))
        for e in range(per) for p in range(n_pages)]
    grid_spec = pltpu.PrefetchScalarGridSpec(
        num_scalar_prefetch=1,
        grid=(DB // per,),
        in_specs=page_specs + page_specs + [
            pl.BlockSpec((per, rows, HEAD_DIM), lambda b, pt: (b, 0, 0)),
            pl.BlockSpec((per, n_tok * A_KV_HEADS, HEAD_DIM), lambda b, pt: (b, 0, 0)),
            pl.BlockSpec((per, n_tok * A_KV_HEADS, HEAD_DIM), lambda b, pt: (b, 0, 0)),
            pl.BlockSpec((rows_pad, n_cols), lambda b, pt: (b, 0)),
            pl.BlockSpec((rows, n_cols), lambda b, pt: (0, 0), pipeline_mode=pl.Buffered(1)),
            pl.BlockSpec((1, HEAD_DIM), lambda b, pt: (0, 0), pipeline_mode=pl.Buffered(1))],
        out_specs=pl.BlockSpec((per, rows, HEAD_DIM), lambda b, pt: (b, 0, 0)),
    )
    return pl.pallas_call(
        functools.partial(_dsa_sample_attend_kernel, n_pages=n_pages, n_tok=n_tok, rows_pad=rows_pad),
        out_shape=jax.ShapeDtypeStruct((DB, rows, HEAD_DIM), BF16),
        grid_spec=grid_spec,
        compiler_params=_cparams(("parallel",)),
        name="dsa_sample_attend",
    )(page_table, *([cache_k] * (per * n_pages)), *([cache_v] * (per * n_pages)), q_rows, k_new, v_new, mask,
      bias_tab, q_norm_g.reshape(1, HEAD_DIM))


def _log_sigmoid(z):
    return jnp.minimum(z, 0.0) - jnp.log(1.0 + jnp.exp(-jnp.abs(z)))


def _seg_masks(seg):
    r = lax.broadcasted_iota(I32, (TILE, TILE), 0)
    c = lax.broadcasted_iota(I32, (TILE, TILE), 1)
    return r, c, (r // seg) == (c // seg)


def _gla_levels(seg):
    w, out = seg // 2, []
    while w >= 1:
        out.append(w)
        w //= 2
    return out


def _gla_sum_matrices(seg):
    r = jnp.arange(TILE)[:, None]
    c = jnp.arange(TILE)[None, :]
    mats = []
    for w in _gla_levels(seg):
        same = (r // (2 * w)) == (c // (2 * w))
        r_right = (r % (2 * w)) >= w
        c_right = (c % (2 * w)) >= w
        mats.append(same & r_right & c_right & (c <= r))
    for w in _gla_levels(seg):
        same = (r // (2 * w)) == (c // (2 * w))
        r_right = (r % (2 * w)) >= w
        c_right = (c % (2 * w)) >= w
        mats.append(same & (~r_right) & (~c_right) & (c > r))
    same_seg = (r // seg) == (c // seg)
    mats.append(same_seg & (c <= r))
    mats.append(same_seg & (c > r))
    return jnp.concatenate(mats, axis=0).astype(BF16)


def _bdot(a, b):
    return jnp.dot(a.astype(BF16), b.astype(BF16), preferred_element_type=F32)


def _bdot_nt(a, b):
    return lax.dot_general(a.astype(BF16), b.astype(BF16), (((1,), (1,)), ((), ())), preferred_element_type=F32)


def _gla_common(tiles, wg_ref, bg_ref, mats_ref, seg):
    n = len(tiles)
    kw = B_HEADS * B_DK
    las = []
    for _, _, misc_ref in tiles:
        gb = misc_ref[:, MISC_GB:MISC_GB + GATE_RANK]
        z = jnp.dot(gb, wg_ref[...], precision=HIGHEST, preferred_element_type=F32) + bg_ref[...]
        las.append(_log_sigmoid(z) / GATE_TEMP)
    la = jnp.concatenate(las, axis=1)
    la_hi = la.astype(BF16)
    la_lo = (la - la_hi.astype(F32)).astype(BF16)
    mats = mats_ref[...]
    sums = (jnp.dot(mats, la_hi, preferred_element_type=F32) + jnp.dot(mats, la_lo, preferred_element_type=F32))
    levels = _gla_levels(seg)
    nl = len(levels)
    qs = [qb_ref[...] * B_DK ** -0.5 for qb_ref, _, _ in tiles]
    ks = [kb_ref[...] for _, kb_ref, _ in tiles]
    r, c, _ = _seg_masks(seg)
    atts = [[jnp.where(r == c, _bdot_nt(qs[i][:, h * B_DK:(h + 1) * B_DK], ks[i][:, h * B_DK:(h + 1) * B_DK]), 0.0)
             for h in range(B_HEADS)] for i in range(n)]
    for li, w in enumerate(levels):
        pair = ((r // (2 * w)) == (c // (2 * w))) & ((r % (2 * w)) >= w) & ((c % (2 * w)) < w)
        qd = [(qs[i] * jnp.exp(sums[li * TILE:(li + 1) * TILE, i * kw:(i + 1) * kw])).astype(BF16) for i in range(n)]
        kd = [(ks[i] * jnp.exp(sums[(nl + li) * TILE:(nl + li + 1) * TILE, i * kw:(i + 1) * kw])).astype(BF16)
              for i in range(n)]
        for h in range(B_HEADS):
            hs = slice(h * B_DK, (h + 1) * B_DK)
            for i in range(n):
                atts[i][h] = atts[i][h] + jnp.where(pair, _bdot_nt(qd[i][:, hs], kd[i][:, hs]), 0.0)
    out = []
    for i in range(n):
        cs = slice(i * kw, (i + 1) * kw)
        b_cum = sums[2 * nl * TILE:(2 * nl + 1) * TILE, cs]
        rem = sums[(2 * nl + 1) * TILE:(2 * nl + 2) * TILE, cs]
        out.append((qs[i], ks[i], atts[i], b_cum, rem))
    return out


def _gla_finish(o_heads, rb_ref, go_ref, o_ref):
    go = go_ref[...]
    for h in range(B_HEADS):
        vs = slice(h * B_DV, (h + 1) * B_DV)
        rb = rb_ref[:, vs]
        o_ref[:, vs] = (_rms(o_heads[h], go) * (rb * jax.nn.sigmoid(rb))).astype(o_ref.dtype)


def _gla_prompt_kernel(qb_ref, kb_ref, vb_ref, rb_ref, misc_ref, wg_ref, bg_ref, go_ref, mats_ref,
                       o_ref, s_ref, state_ref):
    ci = pl.program_id(0)

    @pl.when(ci == 0)
    def _():
        state_ref[...] = jnp.zeros_like(state_ref)

    nb = qb_ref.shape[0]
    common = _gla_common([(qb_ref.at[b], kb_ref.at[b], misc_ref.at[b]) for b in range(nb)],
                         wg_ref, bg_ref, mats_ref, TILE)
    vals = [vb_ref[b] for b in range(nb)]
    states = [state_ref[b] for b in range(nb)]
    qes = [common[b][0] * jnp.exp(common[b][3]) for b in range(nb)]
    o_heads = [[] for _ in range(nb)]
    for h in range(B_HEADS):
        ks = slice(h * B_DK, (h + 1) * B_DK)
        vs = slice(h * B_DV, (h + 1) * B_DV)
        for b in range(nb):
            o_heads[b].append(_bdot(qes[b][:, ks], states[b][ks, :]) + _bdot(common[b][2][h], vals[b][:, vs]))
    for b in range(nb):
        _gla_finish(o_heads[b], rb_ref.at[b], go_ref, o_ref.at[b])

    ke_ts = [(common[b][1] * jnp.exp(common[b][4])).T for b in range(nb)]
    e_cols = [jnp.broadcast_to(jnp.exp(common[b][3][TILE - 1:TILE, :]), (TILE, B_HEADS * B_DK)).T[:, 0:1]
              for b in range(nb)]
    for b in range(nb):
        upd = jnp.concatenate(
            [_bdot(ke_ts[b][h * B_DK:(h + 1) * B_DK, :], vals[b][:, h * B_DV:(h + 1) * B_DV])
             for h in range(B_HEADS)], axis=0)
        new_state = states[b] * e_cols[b] + upd
        state_ref[b] = new_state
        s_ref[b] = new_state


def gla_prompt(proj, w_gate, b_gate, g_out, n_batch, seq):
    nc = seq // TILE
    kwid = B_HEADS * B_DK
    vwid = B_HEADS * B_DV
    mats = _gla_sum_matrices(TILE)
    proj3 = proj.reshape(n_batch, seq, proj.shape[1])
    o, s = pl.pallas_call(
        _gla_prompt_kernel,
        out_shape=(jax.ShapeDtypeStruct((n_batch, seq, vwid), BF16),
                   jax.ShapeDtypeStruct((n_batch, kwid, B_DV), F32)),
        grid=(nc,),
        in_specs=[pl.BlockSpec((n_batch, TILE, kwid), lambda c: (0, c, QB_OFF // kwid)),
                  pl.BlockSpec((n_batch, TILE, kwid), lambda c: (0, c, KB_OFF // kwid)),
                  pl.BlockSpec((n_batch, TILE, vwid), lambda c: (0, c, VB_OFF // vwid)),
                  pl.BlockSpec((n_batch, TILE, vwid), lambda c: (0, c, RB_OFF // vwid)),
                  pl.BlockSpec((n_batch, TILE, LANES), lambda c: (0, c, MISC_OFF // LANES)),
                  _resident((GATE_RANK, kwid)),
                  _resident((1, kwid)),
                  _resident((1, B_DV)),
                  _resident(mats.shape)],
        out_specs=(pl.BlockSpec((n_batch, TILE, vwid), lambda c: (0, c, 0)),
                   pl.BlockSpec((n_batch, kwid, B_DV), lambda c: (0, 0, 0))),
        scratch_shapes=[pltpu.VMEM((n_batch, kwid, B_DV), F32)],
        compiler_params=_cparams(("arbitrary",)),
        name="gla_prompt",
    )(proj3, proj3, proj3, proj3, proj3, w_gate, b_gate.reshape(1, kwid), g_out.reshape(1, B_DV), mats)
    return o.reshape(n_batch * seq, vwid), s.reshape(n_batch, B_HEADS, B_DK, B_DV)


def _gla_sample_kernel(qb_ref, kb_ref, vb_ref, rb_ref, misc_ref, wg_ref, bg_ref, go_ref, mats_ref, s0_ref,
                       o_ref, s_ref, *, seg):
    nbt = TILE // seg
    (q, k, att, b_cum, rem), = _gla_common([(qb_ref, kb_ref, misc_ref)], wg_ref, bg_ref, mats_ref, seg)
    v = vb_ref[...]
    qe = q * jnp.exp(b_cum)
    ke = k * jnp.exp(rem)
    r1 = lax.broadcasted_iota(I32, (TILE, 1), 0)
    e_last = jnp.where(r1 % seg == seg - 1, jnp.exp(b_cum), 0.0)
    wide = nbt * B_DK
    mq = (lax.broadcasted_iota(I32, (TILE, wide), 0) // seg) == (lax.broadcasted_iota(I32, (TILE, wide), 1) // B_DK)
    mk = (lax.broadcasted_iota(I32, (wide, TILE), 0) // B_DK) == (lax.broadcasted_iota(I32, (wide, TILE), 1) // seg)
    o_heads = []
    for h in range(B_HEADS):
        ks = slice(h * B_DK, (h + 1) * B_DK)
        vs = slice(h * B_DV, (h + 1) * B_DV)
        state = s0_ref[:, h].reshape(wide, B_DV)
        q_bd = jnp.where(mq, jnp.concatenate([qe[:, ks]] * nbt, axis=1), 0.0)
        o_heads.append(_bdot(q_bd, state) + _bdot(att[h], v[:, vs]))
        pair_t = jnp.concatenate([ke[:, ks], e_last[:, ks]], axis=1).T
        k_bd = jnp.where(mk, jnp.concatenate([pair_t[:B_DK]] * nbt, axis=0), 0.0)
        e_bd = jnp.where(mk, jnp.concatenate([pair_t[B_DK:]] * nbt, axis=0), 0.0)
        e_col = jnp.sum(e_bd, axis=-1, keepdims=True)
        new_state = state * e_col + _bdot(k_bd, v[:, vs])
        s_ref[:, h] = new_state.reshape(nbt, B_DK, B_DV)
    _gla_finish(o_heads, rb_ref, go_ref, o_ref)


def gla_sample(proj, w_gate, b_gate, g_out, s0, layer, n_tok):
    T = proj.shape[0]
    nbt = TILE // n_tok
    kwid = B_HEADS * B_DK
    vwid = B_HEADS * B_DV
    mats = _gla_sum_matrices(n_tok)
    return pl.pallas_call(
        functools.partial(_gla_sample_kernel, seg=n_tok),
        out_shape=(jax.ShapeDtypeStruct((T, vwid), BF16),
                   jax.ShapeDtypeStruct(s0.shape[1:], F32)),
        grid=(T // TILE,),
        in_specs=[pl.BlockSpec((TILE, kwid), lambda i: (i, QB_OFF // kwid)),
                  pl.BlockSpec((TILE, kwid), lambda i: (i, KB_OFF // kwid)),
                  pl.BlockSpec((TILE, vwid), lambda i: (i, VB_OFF // vwid)),
                  pl.BlockSpec((TILE, vwid), lambda i: (i, RB_OFF // vwid)),
                  pl.BlockSpec((TILE, LANES), lambda i: (i, MISC_OFF // LANES)),
                  _resident((GATE_RANK, kwid)),
                  _resident((1, kwid)),
                  _resident((1, B_DV)),
                  _resident(mats.shape),
                  pl.BlockSpec((None, nbt, B_HEADS, B_DK, B_DV), lambda i: (layer, i, 0, 0, 0))],
        out_specs=(pl.BlockSpec((TILE, vwid), lambda i: (i, 0)),
                   pl.BlockSpec((nbt, B_HEADS, B_DK, B_DV), lambda i: (i, 0, 0, 0))),
        compiler_params=_cparams(("parallel",)),
        name="gla_sample",
    )(proj, proj, proj, proj, proj, w_gate, b_gate.reshape(1, kwid), g_out.reshape(1, B_DV), mats, s0)


def _gelu(x):
    return jax.nn.gelu(x)


def _gmlp_kernel(uc_ref, vc_ref, gv_ref, ws_ref, bcol_ref, o_ref, vn_ref=None, *, seg):
    r, c, same_seg = _seg_masks(seg)
    keep = same_seg & (c <= r)
    for t in range(uc_ref.shape[0] // TILE):
        rows = slice(t * TILE, (t + 1) * TILE)
        u = _gelu(uc_ref[rows, :])
        vg = _gelu(vc_ref[rows, :])
        for g in range(C_GROUPS):
            gs = slice(g * C_GROUP_DIM, (g + 1) * C_GROUP_DIM)
            vn = _rms(vg[:, gs], gv_ref[:, gs])
            if vn_ref is not None:
                vn_ref[rows, gs] = vn
            w = jnp.where(keep, ws_ref[g], 0.0).astype(BF16)
            s = jnp.dot(w, vn.astype(BF16), preferred_element_type=F32) + bcol_ref[:, g:g + 1]
            o_ref[rows, gs] = (u[:, gs] * s).astype(o_ref.dtype)


def gmlp(proj, g_v, w_tiles, b_cols, layer, seg, tm, with_vn):
    T = proj.shape[0]
    tm = min(tm, T)
    cw = C_GROUPS * C_GROUP_DIM
    out_shape = [jax.ShapeDtypeStruct((T, cw), BF16)]
    out_specs = [pl.BlockSpec((tm, cw), lambda i: (i, 0))]
    if with_vn:
        out_shape.append(jax.ShapeDtypeStruct((T, cw), F32))
        out_specs.append(pl.BlockSpec((tm, cw), lambda i: (i, 0)))
    return pl.pallas_call(
        functools.partial(_gmlp_kernel, seg=seg),
        out_shape=tuple(out_shape),
        grid=(T // tm,),
        in_specs=[pl.BlockSpec((tm, cw), lambda i: (i, UC_OFF // cw)),
                  pl.BlockSpec((tm, cw), lambda i: (i, VC_OFF // cw)),
                  _resident((1, cw)),
                  _layer_resident((C_GROUPS, TILE, TILE), layer),
                  _layer_resident((TILE, C_GROUPS), layer)],
        out_specs=tuple(out_specs),
        compiler_params=_cparams(("parallel",)),
        name="gmlp",
    )(proj, proj, g_v.reshape(1, cw), w_tiles, b_cols)


def _ffn_kernel(h_ref, oa_ref, ob_ref, oc_ref, wo_ref, g_ref, wg_ref, wu_ref, wd_ref, o_ref, n_ref):
    j = pl.program_id(1)

    @pl.when(j == 0)
    def _():
        aw = oa_ref.shape[1]
        bw = ob_ref.shape[1]
        h = h_ref[...] + jnp.dot(oa_ref[...], wo_ref[0:aw, :], preferred_element_type=F32)
        h = h + jnp.dot(ob_ref[...], wo_ref[aw:aw + bw, :], preferred_element_type=F32)
        h = h + jnp.dot(oc_ref[...], wo_ref[aw + bw:, :], preferred_element_type=F32)
        n_ref[...] = _rms(h, g_ref[...]).astype(BF16)
        o_ref[...] = h

    n = n_ref[...]
    a = jnp.dot(n, wg_ref[...], preferred_element_type=F32)
    u = jnp.dot(n, wu_ref[...], preferred_element_type=F32)
    act = (a * jax.nn.sigmoid(a) * u).astype(BF16)
    o_ref[...] += jnp.dot(act, wd_ref[...], preferred_element_type=F32)


def out_proj_ffn(h, o_a, o_b, o_c, w_out, g, w_gate, w_up, w_down, tm, tf):
    T, D = h.shape
    tm = min(tm, T)
    FF = w_gate.shape[1]
    return pl.pallas_call(
        _ffn_kernel,
        out_shape=jax.ShapeDtypeStruct((T, D), F32),
        grid=(T // tm, FF // tf),
        in_specs=[pl.BlockSpec((tm, D), lambda i, j: (i, 0)),
                  pl.BlockSpec((tm, o_a.shape[1]), lambda i, j: (i, 0)),
                  pl.BlockSpec((tm, o_b.shape[1]), lambda i, j: (i, 0)),
                  pl.BlockSpec((tm, o_c.shape[1]), lambda i, j: (i, 0)),
                  _resident(w_out.shape),
                  _resident((1, D)),
                  pl.BlockSpec((D, tf), lambda i, j: (0, j)),
                  pl.BlockSpec((D, tf), lambda i, j: (0, j)),
                  pl.BlockSpec((tf, D), lambda i, j: (j, 0))],
        out_specs=pl.BlockSpec((tm, D), lambda i, j: (i, 0)),
        scratch_shapes=[pltpu.VMEM((tm, D), BF16)],
        compiler_params=_cparams(("parallel", "arbitrary")),
        name="out_proj_ffn",
    )(h, o_a, o_b, o_c, w_out, g.reshape(1, D), w_gate, w_up, w_down)


def _ple_kernel(h_ref, p_ref, g_ref, wgate_ref, wproj_ref, o_ref):
    h = h_ref[...]
    n = _rms(h, g_ref[...]).astype(BF16)
    gate = jax.nn.sigmoid(jnp.dot(n, wgate_ref[...], preferred_element_type=F32))
    emb = jnp.dot(p_ref[...].astype(BF16), wproj_ref[...], preferred_element_type=F32)
    o_ref[...] = h + gate * emb


def ple(h, p, g, w_gate, w_proj, layer, tm):
    T, D = h.shape
    tm = min(tm, T)
    P = p.shape[2]
    return pl.pallas_call(
        _ple_kernel,
        out_shape=jax.ShapeDtypeStruct((T, D), F32),
        grid=(T // tm,),
        in_specs=[pl.BlockSpec((tm, D), lambda i: (i, 0)),
                  pl.BlockSpec((None, tm, P), lambda i: (layer, i, 0)),
                  _resident((1, D)),
                  _resident(w_gate.shape),
                  _resident(w_proj.shape)],
        out_specs=pl.BlockSpec((tm, D), lambda i: (i, 0)),
        compiler_params=_cparams(("parallel",)),
        name="ple",
    )(h, p, g.reshape(1, D), w_gate, w_proj)


_W_IN_SEGMENTS = (("q", 1024), ("k", 256), ("v", 256), ("qi", 1024), ("ki", 64), ("wi", 16), ("qb", 256),
                  ("kb", 256), ("vb", 512), ("gb", 16), ("rb", 512), ("uc", 512), ("vc", 512))
_W_IN_PACKED_ORDER = ("q", "qi", "vb", "rb", "uc", "vc", "k", "v", "qb", "kb", "ki", "wi", "gb")


def _pack_kernel(wt_ref, o_ref):
    src, start = {}, 0
    for name, size in _W_IN_SEGMENTS:
        src[name] = (start, size)
        start += size
    dst = 0
    small = []
    for name in _W_IN_PACKED_ORDER:
        s0, size = src[name]
        if size < LANES:
            small.append(wt_ref[s0:s0 + size, :])
            continue
        o_ref[:, dst:dst + size] = wt_ref[s0:s0 + size, :].T.astype(BF16)
        dst += size
    used = sum(x.shape[0] for x in small)
    small.append(jnp.zeros((LANES - used, wt_ref.shape[1]), F32))
    o_ref[:, dst:dst + LANES] = jnp.concatenate(small, axis=0).T.astype(BF16)


def _pack_w_in(w, tr=256):
    depth, D, N = w.shape
    return pl.pallas_call(
        _pack_kernel,
        out_shape=jax.ShapeDtypeStruct((depth, D, PROJ_PACKED), BF16),
        grid=(depth, D // tr),
        in_specs=[pl.BlockSpec((None, N, tr), lambda l, i: (l, 0, i))],
        out_specs=pl.BlockSpec((None, tr, PROJ_PACKED), lambda l, i: (l, i, 0)),
        compiler_params=_cparams(("parallel", "parallel")),
        name="pack_w_in",
    )(jnp.swapaxes(w, 1, 2))


def _mixer_tail(h, o_a, o_b, o_c, p_all, lw, layer):
    h = out_proj_ffn(h, o_a, o_b, o_c, lw["w_out"], lw["g_ffn"], lw["w_ffn_gate"], lw["w_ffn_up"],
                     lw["w_ffn_down"], TM_FFN, TF_FFN)
    return ple(h, p_all, lw["g_ple"], lw["w_ple_gate"], lw["w_ple_proj"], layer, TM_PLE)


def kernel(x_prompt, x_sample, cache_k, cache_v, cache_idx_k, state_gla, page_table, p_prompt, p_sample,
           g_mix, w_in, q_norm_g, k_norm_g, rel_bias, w_gate_b, b_gate_b, g_out_b, g_v_c, w_spatial,
           b_spatial, w_out, g_ffn, w_ffn_gate, w_ffn_up, w_ffn_down, g_ple, w_ple_gate, w_ple_proj):
    n_batch, seq, d_model = x_prompt.shape
    dec_batch, dec_seq, _ = x_sample.shape
    depth = w_in.shape[0]
    n_pages = page_table.shape[1]
    past = n_pages * PAGE_SIZE
    kw = A_KV_HEADS * HEAD_DIM
    tp, ts = n_batch * seq, dec_batch * dec_seq
    rows_pad = SUBLANES
    l_sample = past + PAGE_SIZE
    k_sel_s = min(TOPK_MAX, (past + dec_seq) // 4)

    bias_p = bias_table_prompt(rel_bias)
    bias_s = bias_table_sample(rel_bias, past, dec_seq, A_KV_HEADS * l_sample)
    cache_ik_t = jnp.swapaxes(cache_idx_k, 2, 3)
    cache_k2 = cache_k.reshape(depth, cache_k.shape[1], PAGE_SIZE * A_KV_HEADS, HEAD_DIM)
    cache_v2 = cache_v.reshape(depth, cache_v.shape[1], PAGE_SIZE * A_KV_HEADS, HEAD_DIM)

    hp = x_prompt.reshape(tp, d_model)
    hs = x_sample.reshape(ts, d_model)
    outs = {k: [] for k in ("kp", "vp", "ikp", "sp", "ks", "vs", "iks", "ss", "cs")}
    per_s = rows_pad // dec_seq
    place_t = (jnp.arange(rows_pad)[:, None, None]
               == jnp.arange(per_s)[None, :, None] * dec_seq + jnp.arange(dec_seq)[None, None, :]
               ).astype(F32)
    w_ple_proj_b = w_ple_proj.astype(BF16)
    w_packed = _pack_w_in(w_in)
    pp_all = p_prompt.reshape(depth, tp, -1)
    ps_all = p_sample.reshape(depth, ts, -1)
    b_cols_p = jnp.swapaxes(b_spatial, 1, 2)
    reps = TILE // dec_seq
    w_tiles_s = jnp.tile(w_spatial[:, :, :dec_seq, :dec_seq], (1, 1, reps, reps))
    b_cols_s = jnp.tile(jnp.swapaxes(b_spatial[:, :, :dec_seq], 1, 2), (1, reps, 1))
    for i in range(depth):
        proj, (kn, vv, ik, vt), (wo_b, wg_b, wu_b, wd_b, wpg_b) = in_projection(
            hp, g_mix[i], w_packed, k_norm_g[i], i, TM_PROJ, True,
            cast=(w_out, w_ffn_gate, w_ffn_up, w_ffn_down, w_ple_gate))
        lw = dict(w_out=wo_b, g_ffn=g_ffn[i], w_ffn_gate=wg_b, w_ffn_up=wu_b, w_ffn_down=wd_b,
                  g_ple=g_ple[i], w_ple_gate=wpg_b, w_ple_proj=w_ple_proj_b[i])
        o_a = dsa_prompt(proj, kn, vt, bias_p, q_norm_g[i], n_batch, seq)
        o_b, s_p = gla_prompt(proj, w_gate_b[i], b_gate_b[i], g_out_b[i], n_batch, seq)
        o_c, = gmlp(proj, g_v_c[i], w_spatial, b_cols_p, i, TILE, TM_GMLP, False)
        hp = _mixer_tail(hp, o_a, o_b, o_c, pp_all, lw, i)
        outs["kp"].append(kn.reshape(n_batch, seq, A_KV_HEADS, HEAD_DIM))
        outs["vp"].append(vv.reshape(n_batch, seq, A_KV_HEADS, HEAD_DIM))
        outs["ikp"].append(ik.reshape(n_batch, seq, IDX_DIM))
        outs["sp"].append(s_p)

        proj, (kn, vv, ik), _ = in_projection(hs, g_mix[i], w_packed, k_norm_g[i], i, TM_PROJ, False)
        qi_rows = proj[:, QI_OFF:QI_OFF + IDX_HEADS * IDX_DIM].reshape(dec_batch, dec_seq * IDX_HEADS, IDX_DIM)
        wi = proj[:, MISC_OFF + MISC_WI:MISC_OFF + MISC_WI + IDX_HEADS].reshape(dec_batch, dec_seq, IDX_HEADS)
        wi = wi * (IDX_HEADS ** -0.5 * IDX_DIM ** -0.5)
        wmat = (place_t[None, :, :, :, None] * wi.reshape(dec_batch // per_s, 1, per_s, dec_seq, IDX_HEADS)
                ).reshape(dec_batch // per_s, rows_pad, per_s * dec_seq * IDX_HEADS)
        ki_new_t = jnp.pad(jnp.swapaxes(ik.reshape(dec_batch, dec_seq, IDX_DIM), 1, 2),
                           ((0, 0), (0, 0), (0, PAGE_SIZE - dec_seq)))
        mask = dsa_sample_select(cache_ik_t, i, page_table, qi_rows, wmat, ki_new_t, dec_seq, k_sel_s)
        q_rows = proj[:, Q_OFF:Q_OFF + A_HEADS * HEAD_DIM].reshape(dec_batch, dec_seq * A_HEADS, HEAD_DIM)
        o_a = dsa_sample_attend(cache_k2, cache_v2, i, page_table, q_rows,
                                kn.reshape(dec_batch, dec_seq * A_KV_HEADS, HEAD_DIM),
                                vv.reshape(dec_batch, dec_seq * A_KV_HEADS, HEAD_DIM), mask, bias_s, q_norm_g[i])
        o_a = o_a.reshape(ts, A_HEADS * HEAD_DIM)
        o_b, s_s = gla_sample(proj, w_gate_b[i], b_gate_b[i], g_out_b[i], state_gla, i, dec_seq)
        o_c, vn = gmlp(proj, g_v_c[i], w_tiles_s, b_cols_s, i, dec_seq, TM_GMLP, True)
        hs = _mixer_tail(hs, o_a, o_b, o_c, ps_all, lw, i)
        outs["ks"].append(kn.reshape(dec_batch, dec_seq, A_KV_HEADS, HEAD_DIM))
        outs["vs"].append(vv.reshape(dec_batch, dec_seq, A_KV_HEADS, HEAD_DIM))
        outs["iks"].append(ik.reshape(dec_batch, dec_seq, IDX_DIM))
        outs["ss"].append(s_s)
        outs["cs"].append(vn.reshape(dec_batch, dec_seq, -1))

    st = {k: jnp.stack(v) for k, v in outs.items()}
    return (hp.reshape(n_batch, seq, d_model), hs.reshape(dec_batch, dec_seq, d_model),
            st["kp"], st["vp"], st["ikp"], st["sp"], st["ks"], st["vs"], st["iks"], st["ss"], st["cs"])
```

```python
import functools
import math

import jax
import jax.numpy as jnp
from jax import lax
from jax.experimental import pallas as pl
from jax.experimental.pallas import tpu as pltpu

F32 = jnp.float32
BF16 = jnp.bfloat16
I32 = jnp.int32
HIGHEST = lax.Precision.HIGHEST

LANES = 128
SUBLANES = 8
VMEM_LIMIT = 56 * 1024 * 1024

HEAD_DIM = 128
A_HEADS = 8
A_KV_HEADS = 2
IDX_HEADS = 16
IDX_DIM = 64
TOPK_MAX = 256
NUM_BUCKETS = 32
MAX_DISTANCE = 128
B_HEADS = 4
B_DK = 64
B_DV = 128
GATE_RANK = 16
GATE_TEMP = 16.0
C_GROUPS = 4
C_GROUP_DIM = 128
PAGE_SIZE = 128
EPS = 1e-6
NEG_BIG = -1e30
INT_MIN = -(2 ** 31)
NEG_INF_KEY = -2139095041

TILE = 128
QBLK = 256
TM_PROJ = 256
TM_FFN = 512
TF_FFN = 512
TM_PLE = 512

Q_OFF, QI_OFF, VB_OFF, RB_OFF, UC_OFF, VC_OFF = 0, 1024, 2048, 2560, 3072, 3584
K_OFF, V_OFF, QB_OFF, KB_OFF, MISC_OFF = 4096, 4352, 4608, 4864, 5120
PROJ_PACKED = 5248
MISC_KI, MISC_WI, MISC_GB = 0, 64, 80


def _cparams(sem):
    return pltpu.CompilerParams(dimension_semantics=sem, vmem_limit_bytes=VMEM_LIMIT)


def _rms(x, g):
    return x * lax.rsqrt(jnp.mean(x * x, axis=-1, keepdims=True) + EPS) * g


def _resident(shape):
    nd = len(shape)
    return pl.BlockSpec(shape, lambda *_: (0,) * nd, pipeline_mode=pl.Buffered(1))


def _layer_resident(shape, layer):
    nd = len(shape)
    return pl.BlockSpec((None,) + tuple(shape), lambda *_: (layer,) + (0,) * nd, pipeline_mode=pl.Buffered(1))


def _proj_kernel(x_ref, g_ref, w_ref, kg_ref, gv_ref, ws_ref, bcol_ref, *refs, n_cast, with_vt, with_vn, seg):
    cast_in, refs = list(refs[:n_cast]), list(refs[n_cast:])
    o_ref, ko_ref, vo_ref, io_ref, oc_ref = refs[:5]
    refs = refs[5:]
    vt_ref = refs.pop(0) if with_vt else None
    vn_ref = refs.pop(0) if with_vn else None
    cast_out = refs
    n = _rms(x_ref[...], g_ref[...]).astype(BF16)
    ncol = o_ref.shape[1]
    step = 512
    for c0 in range(0, ncol, step):
        c1 = min(c0 + step, ncol)
        o_ref[:, c0:c1] = jnp.dot(n, w_ref[:, c0:c1], preferred_element_type=F32)
    kg = kg_ref[...]
    for hh in range(A_KV_HEADS):
        ko_ref[:, hh * HEAD_DIM:(hh + 1) * HEAD_DIM] = _rms(
            o_ref[:, K_OFF + hh * HEAD_DIM:K_OFF + (hh + 1) * HEAD_DIM], kg)
    v = o_ref[:, V_OFF:V_OFF + A_KV_HEADS * HEAD_DIM]
    vo_ref[...] = v
    io_ref[...] = o_ref[:, MISC_OFF + MISC_KI:MISC_OFF + MISC_KI + IDX_DIM]
    if with_vt:
        for blk in range(vt_ref.shape[0]):
            vt_ref[blk] = v[blk * QBLK:(blk + 1) * QBLK, :].T.astype(vt_ref.dtype)
    cw = C_GROUPS * C_GROUP_DIM
    _gmlp_rows(o_ref.at[:, UC_OFF:UC_OFF + cw], o_ref.at[:, VC_OFF:VC_OFF + cw], gv_ref, ws_ref, bcol_ref,
               oc_ref, vn_ref, seg)
    for src, dst in zip(cast_in, cast_out):
        dst[...] = src[...].astype(dst.dtype)


def in_projection(h, g, w_packed, k_norm_g, gmlp_params, layer, tm, with_vt, with_vn, cast=()):
    T, D = h.shape
    tm = min(tm, T)
    N = w_packed.shape[2]
    kw = A_KV_HEADS * HEAD_DIM
    cw = C_GROUPS * C_GROUP_DIM
    g_v, w_tiles, b_cols, seg = gmlp_params
    steps = T // tm
    cast_specs_in = [pl.BlockSpec((None, w.shape[1] // steps, w.shape[2]), lambda i: (layer, i, 0)) for w in cast]
    cast_specs_out = [pl.BlockSpec((w.shape[1] // steps, w.shape[2]), lambda i: (i, 0)) for w in cast]
    kv_shape = [jax.ShapeDtypeStruct((T, kw), F32),
                jax.ShapeDtypeStruct((T, kw), F32),
                jax.ShapeDtypeStruct((T, IDX_DIM), F32),
                jax.ShapeDtypeStruct((T, cw), BF16)]
    kv_specs = [pl.BlockSpec((tm, kw), lambda i: (i, 0)),
                pl.BlockSpec((tm, kw), lambda i: (i, 0)),
                pl.BlockSpec((tm, IDX_DIM), lambda i: (i, 0)),
                pl.BlockSpec((tm, cw), lambda i: (i, 0))]
    if with_vt:
        kv_shape.append(jax.ShapeDtypeStruct((T // QBLK, kw, QBLK), BF16))
        kv_specs.append(pl.BlockSpec((tm // QBLK, kw, QBLK), lambda i: (i, 0, 0)))
    if with_vn:
        kv_shape.append(jax.ShapeDtypeStruct((T, cw), F32))
        kv_specs.append(pl.BlockSpec((tm, cw), lambda i: (i, 0)))
    outs = pl.pallas_call(
        functools.partial(_proj_kernel, n_cast=len(cast), with_vt=with_vt, with_vn=with_vn, seg=seg),
        out_shape=(jax.ShapeDtypeStruct((T, N), F32),) + tuple(kv_shape) + tuple(
            jax.ShapeDtypeStruct(w.shape[1:], BF16) for w in cast),
        grid=(steps,),
        in_specs=[pl.BlockSpec((tm, D), lambda i: (i, 0)),
                  _resident((1, D)),
                  _layer_resident((D, N), layer),
                  _resident((1, HEAD_DIM)),
                  _resident((1, cw)),
                  _layer_resident((C_GROUPS, TILE, TILE), layer),
                  _layer_resident((TILE, C_GROUPS), layer)] + cast_specs_in,
        out_specs=(pl.BlockSpec((tm, N), lambda i: (i, 0)),) + tuple(kv_specs) + tuple(cast_specs_out),
        compiler_params=_cparams(("parallel",)),
        name="in_projection",
    )(h, g.reshape(1, D), w_packed, k_norm_g.reshape(1, HEAD_DIM), g_v.reshape(1, cw), w_tiles, b_cols, *cast)
    n_kv = len(kv_shape)
    return outs[0], outs[1:1 + n_kv], outs[1 + n_kv:]


def _bucket(dist):
    n = jnp.maximum(dist, 0)
    max_exact = NUM_BUCKETS // 2
    large = max_exact + (jnp.log(jnp.maximum(n, 1).astype(F32) / max_exact)
                         / math.log(MAX_DISTANCE / max_exact)
                         * (NUM_BUCKETS - max_exact)).astype(I32)
    large = jnp.minimum(large, NUM_BUCKETS - 1)
    return jnp.where(n < max_exact, n, large)


def _bias_prompt_kernel(rb_ref, o_ref):
    c = lax.broadcasted_iota(I32, (TILE, TILE), 0)
    t = lax.broadcasted_iota(I32, (TILE, TILE), 1)
    for z in range(3):
        bucket = _bucket(t - c + (2 - z) * TILE)
        for h in range(A_HEADS):
            acc = jnp.zeros((TILE, TILE), F32)
            for b in range(NUM_BUCKETS):
                acc = jnp.where(bucket == b, rb_ref[b, h], acc)
            o_ref[h, z] = acc


def bias_table_prompt(rel_bias):
    return pl.pallas_call(
        _bias_prompt_kernel,
        out_shape=jax.ShapeDtypeStruct((A_HEADS, 3, TILE, TILE), F32),
        in_specs=[pl.BlockSpec(memory_space=pltpu.SMEM)],
        out_specs=pl.BlockSpec(memory_space=pltpu.VMEM),
        name="bias_table_prompt",
    )(rel_bias)


def _bias_sample_kernel(rbrows_ref, o_ref, *, past, n_tok):
    rows, L = o_ref.shape
    r = lax.broadcasted_iota(I32, (rows, L), 0)
    s = lax.broadcasted_iota(I32, (rows, L), 1) // A_KV_HEADS
    bucket = _bucket(past + r // A_HEADS - s)
    rbrows = rbrows_ref[...]
    acc = jnp.zeros((rows, L), F32)
    for b in range(NUM_BUCKETS):
        acc = jnp.where(bucket == b, rbrows[:, b:b + 1], acc)
    o_ref[...] = acc


def bias_table_sample(rel_bias, past, n_tok, L):
    rows = n_tok * A_HEADS
    rbrows = jnp.tile(rel_bias.T, (n_tok, 1))
    return pl.pallas_call(
        functools.partial(_bias_sample_kernel, past=past, n_tok=n_tok),
        out_shape=jax.ShapeDtypeStruct((rows, L), F32),
        name="bias_table_sample",
    )(rbrows)


def _sortable_key(x):
    b = lax.bitcast_convert_type(x, I32)
    return b ^ ((b >> 31) & 0x7FFFFFFF)


def _topk_member(skey_ref, k_sel):
    R, L = skey_ref.shape

    def body(it, ans):
        bit = 31 - it
        cand = ans | lax.shift_left(jnp.int32(1), bit)
        cand_s = cand ^ INT_MIN
        cnt = jnp.sum(jnp.where(skey_ref[...] >= cand_s, 1.0, 0.0), axis=-1, keepdims=True)
        return jnp.where(cnt >= k_sel, cand, ans)

    ans = lax.fori_loop(0, 32, body, jnp.zeros((R, 1), I32))
    tau = ans ^ INT_MIN
    skey = skey_ref[...]
    gt = skey > tau
    eq = skey == tau
    n_gt = jnp.sum(jnp.where(gt, 1.0, 0.0), axis=-1, keepdims=True)
    room = k_sel - n_gt
    r_i = lax.broadcasted_iota(I32, (LANES, LANES), 0)
    c_i = lax.broadcasted_iota(I32, (LANES, LANES), 1)
    upper = jnp.where(r_i <= c_i, 1.0, 0.0).astype(BF16)
    off = jnp.zeros((R, 1), F32)
    parts = []
    for j in range(L // LANES):
        sl = slice(j * LANES, (j + 1) * LANES)
        eq_j = eq[:, sl]
        run = jnp.dot(jnp.where(eq_j, 1.0, 0.0).astype(BF16), upper, preferred_element_type=F32) + off
        parts.append(gt[:, sl] | (eq_j & (run <= room)))
        off = run[:, LANES - 1:LANES]
    return jnp.concatenate(parts, axis=1)


def _fold8(x, op):
    return op(x.reshape(x.shape[0] // SUBLANES, SUBLANES, x.shape[1]), axis=0)


def _dsa_prompt_kernel(q_ref, qi_ref, misc_ref, kn_ref, vt_ref, bias_ref, qg_ref, o_ref,
                       qst_ref, skey_ref, madd_ref, lg_ref, acc_ref, *, k_sel):
    i = pl.program_id(1)
    nkb = i + 1
    sub = QBLK // TILE
    rep = A_HEADS // A_KV_HEADS
    row0 = pl.multiple_of(i * QBLK, QBLK)
    s_iota = lax.broadcasted_iota(I32, (QBLK, QBLK), 0)
    t_iota = lax.broadcasted_iota(I32, (QBLK, QBLK), 1)

    def admissible(j):
        return (j * QBLK + s_iota) <= (row0 + t_iota)

    wi_t = misc_ref[pl.ds(row0, QBLK), :].T[MISC_WI:MISC_WI + IDX_HEADS, :]
    wi_t = wi_t * (IDX_HEADS ** -0.5 * IDX_DIM ** -0.5)
    for h in range(IDX_HEADS):
        qst_ref[h * QBLK:(h + 1) * QBLK, :] = qi_ref[:, h * IDX_DIM:(h + 1) * IDX_DIM].astype(BF16)

    def score_body(j, carry):
        k0 = pl.multiple_of(j * QBLK, QBLK)
        kj = misc_ref[pl.ds(k0, QBLK), MISC_KI:MISC_KI + IDX_DIM].astype(BF16)
        s = lax.dot_general(kj, qst_ref[...], (((1,), (1,)), ((), ())), preferred_element_type=F32)
        score = jnp.zeros((QBLK, QBLK), F32)
        for h in range(IDX_HEADS):
            score = score + jnp.maximum(s[:, h * QBLK:(h + 1) * QBLK], 0.0) * wi_t[h:h + 1, :]
        skey_ref[j] = _sortable_key(jnp.where(admissible(j), score, -jnp.inf))
        return carry

    lax.fori_loop(0, nkb, score_body, 0)

    def count(pred_fn):
        def hits(j):
            return _fold8(jnp.where(pred_fn(skey_ref[j]), 1.0, 0.0), jnp.sum)

        def body(jj, accs):
            return accs[0] + hits(2 * jj), accs[1] + hits(2 * jj + 1)

        zero = jnp.zeros((SUBLANES, QBLK), F32)
        acc0, acc1 = lax.fori_loop(0, nkb // 2, body, (zero, zero))
        acc = lax.cond(nkb % 2 == 1, lambda: acc0 + acc1 + hits(nkb - 1), lambda: acc0 + acc1)
        return jnp.sum(acc, axis=0, keepdims=True)

    def bit_body(it, ans):
        cand = ans | lax.shift_left(jnp.int32(1), 31 - it)
        cand_s = cand ^ INT_MIN
        cnt = count(lambda key: key >= cand_s)
        return jnp.where(cnt >= k_sel, cand, ans)

    ans = lax.fori_loop(0, 32, bit_body, jnp.zeros((1, QBLK), I32))
    tau = ans ^ INT_MIN
    n_ge = count(lambda key: key >= tau)
    excess = jnp.max(jnp.where((n_ge > k_sel) & (tau != NEG_INF_KEY), 1.0, 0.0))

    @pl.when(excess == 0.0)
    def _():
        def mask_body(j, carry):
            madd_ref[j] = jnp.where((skey_ref[j] >= tau) & admissible(j), 0.0, NEG_BIG)
            return carry

        lax.fori_loop(0, nkb, mask_body, 0)

    @pl.when(excess > 0.0)
    def _():
        room = k_sel - count(lambda key: key > tau)
        lower = jnp.where(t_iota <= s_iota, 1.0, 0.0).astype(BF16)

        def mask_body(j, off):
            key = skey_ref[j]
            eq = key == tau
            run = jnp.dot(lower, jnp.where(eq, 1.0, 0.0).astype(BF16), preferred_element_type=F32) + off
            sel = ((key > tau) | (eq & (run <= room))) & admissible(j)
            madd_ref[j] = jnp.where(sel, 0.0, NEG_BIG)
            return run[QBLK - 1:QBLK, :]

        lax.fori_loop(0, nkb, mask_body, jnp.zeros((1, QBLK), F32))

    qg = qg_ref[...]
    wide = rep * QBLK
    for g in range(A_KV_HEADS):
        gs = slice(g * HEAD_DIM, (g + 1) * HEAD_DIM)
        heads = list(range(g * rep, (g + 1) * rep))
        q_stack = jnp.concatenate(
            [(_rms(q_ref[:, h * HEAD_DIM:(h + 1) * HEAD_DIM], qg) * HEAD_DIM ** -0.5).astype(BF16) for h in heads],
            axis=0)

        def logit_body(j, mx):
            k0 = pl.multiple_of(j * QBLK, QBLK)
            kj = kn_ref[pl.ds(k0, QBLK), gs].astype(BF16)
            lg = lax.dot_general(kj, q_stack, (((1,), (1,)), ((), ())), preferred_element_type=F32)
            madd = madd_ref[j]
            parts = []
            for r, h in enumerate(heads):
                quads = []
                for c in range(sub):
                    quads.append(jnp.concatenate(
                        [bias_ref[h, jnp.clip(2 - ((i - j) * sub + u - c), 0, 2)] for u in range(sub)], axis=1))
                parts.append(lg[:, r * QBLK:(r + 1) * QBLK] + jnp.concatenate(quads, axis=0) + madd)
            lg = jnp.concatenate(parts, axis=1)
            lg_ref[j] = lg
            return jnp.maximum(mx, _fold8(lg, jnp.max))

        mx = lax.fori_loop(0, nkb, logit_body, jnp.full((SUBLANES, wide), NEG_BIG, F32))
        m = jnp.max(mx, axis=0, keepdims=True)
        acc_ref[...] = jnp.zeros(acc_ref.shape, F32)

        def pv_body(j, sm):
            p = jnp.exp(lg_ref[j] - m)
            acc_ref[...] += jnp.dot(vt_ref[j, gs, :], p.astype(BF16), preferred_element_type=F32)
            return sm + _fold8(p, jnp.sum)

        sm = lax.fori_loop(0, nkb, pv_body, jnp.zeros((SUBLANES, wide), F32))
        den = jnp.sum(sm, axis=0, keepdims=True)
        o = (acc_ref[...] / den).T
        for r, h in enumerate(heads):
            o_ref[:, h * HEAD_DIM:(h + 1) * HEAD_DIM] = o[r * QBLK:(r + 1) * QBLK, :].astype(o_ref.dtype)


def dsa_prompt(proj, kn, vt, bias_tab, q_norm_g, n_batch, seq):
    T = proj.shape[0]
    nb = seq // QBLK
    k_sel = min(TOPK_MAX, seq // 4)
    aw = A_HEADS * HEAD_DIM
    iw = IDX_HEADS * IDX_DIM
    kw = A_KV_HEADS * HEAD_DIM
    rep = A_HEADS // A_KV_HEADS
    return pl.pallas_call(
        functools.partial(_dsa_prompt_kernel, k_sel=k_sel),
        out_shape=jax.ShapeDtypeStruct((T, aw), BF16),
        grid=(n_batch, nb),
        in_specs=[pl.BlockSpec((QBLK, aw), lambda b, i: (b * nb + i, Q_OFF // aw)),
                  pl.BlockSpec((QBLK, iw), lambda b, i: (b * nb + i, QI_OFF // iw)),
                  pl.BlockSpec((seq, LANES), lambda b, i: (b, MISC_OFF // LANES)),
                  pl.BlockSpec((seq, kw), lambda b, i: (b, 0)),
                  pl.BlockSpec((nb, kw, QBLK), lambda b, i: (b, 0, 0)),
                  _resident((A_HEADS, 3, TILE, TILE)),
                  _resident((1, HEAD_DIM))],
        out_specs=pl.BlockSpec((QBLK, aw), lambda b, i: (b * nb + i, 0)),
        scratch_shapes=[pltpu.VMEM((IDX_HEADS * QBLK, IDX_DIM), BF16),
                        pltpu.VMEM((nb, QBLK, QBLK), I32),
                        pltpu.VMEM((nb, QBLK, QBLK), F32),
                        pltpu.VMEM((nb, QBLK, rep * QBLK), F32),
                        pltpu.VMEM((HEAD_DIM, rep * QBLK), F32)],
        compiler_params=_cparams(("parallel", "arbitrary")),
        name="dsa_prompt",
    )(proj, proj, proj, kn, vt, bias_tab, q_norm_g.reshape(1, HEAD_DIM))


def _dsa_sample_select_kernel(pt_ref, *refs, n_pages, n_tok, k_sel, rows_pad):
    del pt_ref
    per = rows_pad // n_tok
    page_refs = refs[:per * n_pages]
    qi_ref, wm_ref, kin_ref, mask_ref, sc_ref, skey_ref = refs[per * n_pages:]
    b = pl.program_id(0)
    nb = pl.num_programs(0)
    L = sc_ref.shape[1]
    past = n_pages * PAGE_SIZE

    relu_s = []
    for e in range(per):
        kt_all = jnp.concatenate([r[...].astype(BF16) for r in page_refs[e * n_pages:(e + 1) * n_pages]]
                                 + [kin_ref[e].astype(BF16)], axis=1)
        relu_s.append(jnp.maximum(jnp.dot(qi_ref[e].astype(BF16), kt_all, preferred_element_type=F32), 0.0))
    relu_cat = jnp.concatenate(relu_s, axis=0)
    wm = wm_ref[0]
    r_hi, w_hi = relu_cat.astype(BF16), wm.astype(BF16)
    r_lo, w_lo = (relu_cat - r_hi.astype(F32)).astype(BF16), (wm - w_hi.astype(F32)).astype(BF16)
    score = (jnp.dot(w_hi, r_hi, preferred_element_type=F32) + jnp.dot(w_hi, r_lo, preferred_element_type=F32)
             + jnp.dot(w_lo, r_hi, preferred_element_type=F32))
    r0 = pl.multiple_of(b * rows_pad, rows_pad)
    sc_ref[pl.ds(r0, rows_pad), :] = score

    @pl.when(b == nb - 1)
    def _():
        n_blocks = sc_ref.shape[0] // TILE
        n_tiles = L // PAGE_SIZE
        tp = past + lax.broadcasted_iota(I32, (TILE, L), 0) % n_tok
        sp = lax.broadcasted_iota(I32, (TILE, L), 1)
        adm_blk = sp <= tp
        d_r = lax.broadcasted_iota(I32, (PAGE_SIZE, PAGE_SIZE * A_KV_HEADS), 0)
        d_c = lax.broadcasted_iota(I32, (PAGE_SIZE, PAGE_SIZE * A_KV_HEADS), 1)
        dup = jnp.where(d_c // A_KV_HEADS == d_r, 1.0, 0.0).astype(BF16)
        for rb in range(n_blocks):
            rows = slice(rb * TILE, (rb + 1) * TILE)
            skey_ref[...] = _sortable_key(jnp.where(adm_blk, sc_ref[rows, :], -jnp.inf))
            sel = jnp.where(_topk_member(skey_ref, k_sel) & adm_blk, 1.0, 0.0).astype(BF16)
            stacked = jnp.concatenate([sel[:, j * PAGE_SIZE:(j + 1) * PAGE_SIZE] for j in range(n_tiles)], axis=0)
            stacked = jnp.dot(stacked, dup, preferred_element_type=F32)
            mask_ref[rows, :] = jnp.concatenate(
                [stacked[j * TILE:(j + 1) * TILE, :] for j in range(n_tiles)], axis=1)


def dsa_sample_select(cache_ik_t, layer, page_table, qi_rows, wmat, ki_new_t, n_tok, k_sel):
    DB, n_pages = page_table.shape
    rows_pad = wmat.shape[1]
    per = rows_pad // n_tok
    n_rows = DB // per * rows_pad
    L = (n_pages + 1) * PAGE_SIZE
    page_specs = [pl.BlockSpec((None, None, IDX_DIM, PAGE_SIZE), functools.partial(
        lambda b, pt, e, p: (layer, pt[b * per + e, p], 0, 0), e=e, p=p))
        for e in range(per) for p in range(n_pages)]
    grid_spec = pltpu.PrefetchScalarGridSpec(
        num_scalar_prefetch=1,
        grid=(DB // per,),
        in_specs=page_specs + [
            pl.BlockSpec((per,) + qi_rows.shape[1:], lambda b, pt: (b, 0, 0)),
            pl.BlockSpec((1,) + wmat.shape[1:], lambda b, pt: (b, 0, 0)),
            pl.BlockSpec((per, IDX_DIM, PAGE_SIZE), lambda b, pt: (b, 0, 0))],
        out_specs=pl.BlockSpec((n_rows, A_KV_HEADS * L), lambda b, pt: (0, 0)),
        scratch_shapes=[pltpu.VMEM((n_rows, L), F32),
                        pltpu.VMEM((TILE, L), I32)],
    )
    return pl.pallas_call(
        functools.partial(_dsa_sample_select_kernel, n_pages=n_pages, n_tok=n_tok, k_sel=k_sel,
                          rows_pad=rows_pad),
        out_shape=jax.ShapeDtypeStruct((n_rows, A_KV_HEADS * L), F32),
        grid_spec=grid_spec,
        compiler_params=_cparams(("arbitrary",)),
        name="dsa_sample_select",
    )(page_table, *([cache_ik_t] * (per * n_pages)), qi_rows, wmat, ki_new_t)


def _dsa_sample_attend_kernel(pt_ref, *refs, n_pages, n_tok, rows_pad):
    del pt_ref
    per = rows_pad // n_tok
    k_refs = refs[:per * n_pages]
    v_refs = refs[per * n_pages:2 * per * n_pages]
    q_ref, kn_ref, vn_ref, mask_ref, bias_ref, qg_ref, o_ref = refs[2 * per * n_pages:]
    rows = n_tok * A_HEADS
    page_rows = PAGE_SIZE * A_KV_HEADS
    n_cols = mask_ref.shape[1]
    pad = jnp.zeros((page_rows - n_tok * A_KV_HEADS, HEAD_DIM), BF16)
    rep = A_HEADS // A_KV_HEADS
    grp = (lax.broadcasted_iota(I32, (rows, 1), 0) % A_HEADS) // rep
    own_group = (lax.broadcasted_iota(I32, (rows, n_cols), 1) % A_KV_HEADS) == grp
    member = mask_ref[...]

    for e in range(per):
        def tiles(page_refs, new_ref):
            new = jnp.concatenate([new_ref[e].astype(BF16), pad], axis=0)
            return [r[...].astype(BF16) for r in page_refs[e * n_pages:(e + 1) * n_pages]] + [new]

        q = (_rms(q_ref[e], qg_ref[...]) * HEAD_DIM ** -0.5).astype(BF16)
        sel = jnp.concatenate(
            [jnp.broadcast_to(member[e * n_tok + t:e * n_tok + t + 1, :], (A_HEADS, n_cols)) for t in range(n_tok)],
            axis=0)
        valid = (sel > 0.5) & own_group
        logits = jnp.concatenate(
            [lax.dot_general(q, kt, (((1,), (1,)), ((), ())), preferred_element_type=F32)
             for kt in tiles(k_refs, kn_ref)], axis=1)
        logits = jnp.where(valid, logits + bias_ref[...], NEG_BIG)
        m = jnp.max(logits, axis=-1, keepdims=True)
        p = jnp.exp(logits - m)
        den = jnp.sum(p, axis=-1, keepdims=True)
        pb = p.astype(BF16)
        o = jnp.zeros((rows, HEAD_DIM), F32)
        for j, vt in enumerate(tiles(v_refs, vn_ref)):
            o = o + jnp.dot(pb[:, j * page_rows:(j + 1) * page_rows], vt, preferred_element_type=F32)
        o_ref[e] = (o / den).astype(o_ref.dtype)


def dsa_sample_attend(cache_k, cache_v, layer, page_table, q_rows, k_new, v_new, mask, bias_tab, q_norm_g):
    DB, n_pages = page_table.shape
    n_tok = k_new.shape[1] // A_KV_HEADS
    rows = n_tok * A_HEADS
    rows_pad = SUBLANES
    per = rows_pad // n_tok
    n_cols = mask.shape[1]
    page_rows = PAGE_SIZE * A_KV_HEADS
    page_specs = [pl.BlockSpec((None, None, page_rows, HEAD_DIM), functools.partial(
        lambda b, pt, e, p: (layer, pt[b * per + e, p], 0, 0), e=e, p=p))
        for e in range(per) for p in range(n_pages)]
    grid_spec = pltpu.PrefetchScalarGridSpec(
        num_scalar_prefetch=1,
        grid=(DB // per,),
        in_specs=page_specs + page_specs + [
            pl.BlockSpec((per, rows, HEAD_DIM), lambda b, pt: (b, 0, 0)),
            pl.BlockSpec((per, n_tok * A_KV_HEADS, HEAD_DIM), lambda b, pt: (b, 0, 0)),
            pl.BlockSpec((per, n_tok * A_KV_HEADS, HEAD_DIM), lambda b, pt: (b, 0, 0)),
            pl.BlockSpec((rows_pad, n_cols), lambda b, pt: (b, 0)),
            pl.BlockSpec((rows, n_cols), lambda b, pt: (0, 0), pipeline_mode=pl.Buffered(1)),
            pl.BlockSpec((1, HEAD_DIM), lambda b, pt: (0, 0), pipeline_mode=pl.Buffered(1))],
        out_specs=pl.BlockSpec((per, rows, HEAD_DIM), lambda b, pt: (b, 0, 0)),
    )
    return pl.pallas_call(
        functools.partial(_dsa_sample_attend_kernel, n_pages=n_pages, n_tok=n_tok, rows_pad=rows_pad),
        out_shape=jax.ShapeDtypeStruct((DB, rows, HEAD_DIM), BF16),
        grid_spec=grid_spec,
        compiler_params=_cparams(("parallel",)),
        name="dsa_sample_attend",
    )(page_table, *([cache_k] * (per * n_pages)), *([cache_v] * (per * n_pages)), q_rows, k_new, v_new, mask,
      bias_tab, q_norm_g.reshape(1, HEAD_DIM))


def _log_sigmoid(z):
    return jnp.minimum(z, 0.0) - jnp.log(1.0 + jnp.exp(-jnp.abs(z)))


def _seg_masks(seg):
    r = lax.broadcasted_iota(I32, (TILE, TILE), 0)
    c = lax.broadcasted_iota(I32, (TILE, TILE), 1)
    return r, c, (r // seg) == (c // seg)


def _gla_levels(seg):
    w, out = seg // 2, []
    while w >= 1:
        out.append(w)
        w //= 2
    return out


def _gla_sum_matrices(seg):
    r = jnp.arange(TILE)[:, None]
    c = jnp.arange(TILE)[None, :]
    mats = []
    for w in _gla_levels(seg):
        same = (r // (2 * w)) == (c // (2 * w))
        r_right = (r % (2 * w)) >= w
        c_right = (c % (2 * w)) >= w
        mats.append(same & r_right & c_right & (c <= r))
    for w in _gla_levels(seg):
        same = (r // (2 * w)) == (c // (2 * w))
        r_right = (r % (2 * w)) >= w
        c_right = (c % (2 * w)) >= w
        mats.append(same & (~r_right) & (~c_right) & (c > r))
    same_seg = (r // seg) == (c // seg)
    mats.append(same_seg & (c <= r))
    mats.append(same_seg & (c > r))
    return jnp.concatenate(mats, axis=0).astype(BF16)


def _bdot(a, b):
    return jnp.dot(a.astype(BF16), b.astype(BF16), preferred_element_type=F32)


def _bdot_nt(a, b):
    return lax.dot_general(a.astype(BF16), b.astype(BF16), (((1,), (1,)), ((), ())), preferred_element_type=F32)


def _gla_common(tiles, wg_ref, bg_ref, mats_ref, seg):
    n = len(tiles)
    kw = B_HEADS * B_DK
    las = []
    for _, _, misc_ref in tiles:
        gb = misc_ref[:, MISC_GB:MISC_GB + GATE_RANK]
        z = jnp.dot(gb, wg_ref[...], precision=HIGHEST, preferred_element_type=F32) + bg_ref[...]
        las.append(_log_sigmoid(z) / GATE_TEMP)
    la = jnp.concatenate(las, axis=1)
    la_hi = la.astype(BF16)
    la_lo = (la - la_hi.astype(F32)).astype(BF16)
    mats = mats_ref[...]
    sums = (jnp.dot(mats, la_hi, preferred_element_type=F32) + jnp.dot(mats, la_lo, preferred_element_type=F32))
    levels = _gla_levels(seg)
    nl = len(levels)
    qs = [qb_ref[...] * B_DK ** -0.5 for qb_ref, _, _ in tiles]
    ks = [kb_ref[...] for _, kb_ref, _ in tiles]
    r, c, _ = _seg_masks(seg)
    atts = [[jnp.where(r == c, _bdot_nt(qs[i][:, h * B_DK:(h + 1) * B_DK], ks[i][:, h * B_DK:(h + 1) * B_DK]), 0.0)
             for h in range(B_HEADS)] for i in range(n)]
    for li, w in enumerate(levels):
        pair = ((r // (2 * w)) == (c // (2 * w))) & ((r % (2 * w)) >= w) & ((c % (2 * w)) < w)
        qd = [(qs[i] * jnp.exp(sums[li * TILE:(li + 1) * TILE, i * kw:(i + 1) * kw])).astype(BF16) for i in range(n)]
        kd = [(ks[i] * jnp.exp(sums[(nl + li) * TILE:(nl + li + 1) * TILE, i * kw:(i + 1) * kw])).astype(BF16)
              for i in range(n)]
        for h in range(B_HEADS):
            hs = slice(h * B_DK, (h + 1) * B_DK)
            for i in range(n):
                atts[i][h] = atts[i][h] + jnp.where(pair, _bdot_nt(qd[i][:, hs], kd[i][:, hs]), 0.0)
    out = []
    for i in range(n):
        cs = slice(i * kw, (i + 1) * kw)
        b_cum = sums[2 * nl * TILE:(2 * nl + 1) * TILE, cs]
        rem = sums[(2 * nl + 1) * TILE:(2 * nl + 2) * TILE, cs]
        out.append((qs[i], ks[i], atts[i], b_cum, rem))
    return out


def _gla_finish(o_heads, rb_ref, go_ref, o_ref):
    go = go_ref[...]
    for h in range(B_HEADS):
        vs = slice(h * B_DV, (h + 1) * B_DV)
        rb = rb_ref[:, vs]
        o_ref[:, vs] = (_rms(o_heads[h], go) * (rb * jax.nn.sigmoid(rb))).astype(o_ref.dtype)


def _gla_prompt_kernel(qb_ref, kb_ref, vb_ref, rb_ref, misc_ref, wg_ref, bg_ref, go_ref, mats_ref,
                       o_ref, s_ref, state_ref):
    ci = pl.program_id(0)

    @pl.when(ci == 0)
    def _():
        state_ref[...] = jnp.zeros_like(state_ref)

    nb = qb_ref.shape[0]
    common = _gla_common([(qb_ref.at[b], kb_ref.at[b], misc_ref.at[b]) for b in range(nb)],
                         wg_ref, bg_ref, mats_ref, TILE)
    vals = [vb_ref[b] for b in range(nb)]
    states = [state_ref[b] for b in range(nb)]
    qes = [common[b][0] * jnp.exp(common[b][3]) for b in range(nb)]
    o_heads = [[] for _ in range(nb)]
    for h in range(B_HEADS):
        ks = slice(h * B_DK, (h + 1) * B_DK)
        vs = slice(h * B_DV, (h + 1) * B_DV)
        for b in range(nb):
            o_heads[b].append(_bdot(qes[b][:, ks], states[b][ks, :]) + _bdot(common[b][2][h], vals[b][:, vs]))
    for b in range(nb):
        _gla_finish(o_heads[b], rb_ref.at[b], go_ref, o_ref.at[b])

    ke_ts = [(common[b][1] * jnp.exp(common[b][4])).T for b in range(nb)]
    e_cols = [jnp.broadcast_to(jnp.exp(common[b][3][TILE - 1:TILE, :]), (TILE, B_HEADS * B_DK)).T[:, 0:1]
              for b in range(nb)]
    for b in range(nb):
        upd = jnp.concatenate(
            [_bdot(ke_ts[b][h * B_DK:(h + 1) * B_DK, :], vals[b][:, h * B_DV:(h + 1) * B_DV])
             for h in range(B_HEADS)], axis=0)
        new_state = states[b] * e_cols[b] + upd
        state_ref[b] = new_state
        s_ref[b] = new_state


def gla_prompt(proj, w_gate, b_gate, g_out, n_batch, seq):
    nc = seq // TILE
    kwid = B_HEADS * B_DK
    vwid = B_HEADS * B_DV
    mats = _gla_sum_matrices(TILE)
    proj3 = proj.reshape(n_batch, seq, proj.shape[1])
    o, s = pl.pallas_call(
        _gla_prompt_kernel,
        out_shape=(jax.ShapeDtypeStruct((n_batch, seq, vwid), BF16),
                   jax.ShapeDtypeStruct((n_batch, kwid, B_DV), F32)),
        grid=(nc,),
        in_specs=[pl.BlockSpec((n_batch, TILE, kwid), lambda c: (0, c, QB_OFF // kwid)),
                  pl.BlockSpec((n_batch, TILE, kwid), lambda c: (0, c, KB_OFF // kwid)),
                  pl.BlockSpec((n_batch, TILE, vwid), lambda c: (0, c, VB_OFF // vwid)),
                  pl.BlockSpec((n_batch, TILE, vwid), lambda c: (0, c, RB_OFF // vwid)),
                  pl.BlockSpec((n_batch, TILE, LANES), lambda c: (0, c, MISC_OFF // LANES)),
                  _resident((GATE_RANK, kwid)),
                  _resident((1, kwid)),
                  _resident((1, B_DV)),
                  _resident(mats.shape)],
        out_specs=(pl.BlockSpec((n_batch, TILE, vwid), lambda c: (0, c, 0)),
                   pl.BlockSpec((n_batch, kwid, B_DV), lambda c: (0, 0, 0))),
        scratch_shapes=[pltpu.VMEM((n_batch, kwid, B_DV), F32)],
        compiler_params=_cparams(("arbitrary",)),
        name="gla_prompt",
    )(proj3, proj3, proj3, proj3, proj3, w_gate, b_gate.reshape(1, kwid), g_out.reshape(1, B_DV), mats)
    return o.reshape(n_batch * seq, vwid), s.reshape(n_batch, B_HEADS, B_DK, B_DV)


def _gla_sample_kernel(qb_ref, kb_ref, vb_ref, rb_ref, misc_ref, wg_ref, bg_ref, go_ref, mats_ref, s0_ref,
                       o_ref, s_ref, *, seg):
    nbt = TILE // seg
    (q, k, att, b_cum, rem), = _gla_common([(qb_ref, kb_ref, misc_ref)], wg_ref, bg_ref, mats_ref, seg)
    v = vb_ref[...]
    qe = q * jnp.exp(b_cum)
    ke = k * jnp.exp(rem)
    r1 = lax.broadcasted_iota(I32, (TILE, 1), 0)
    e_last = jnp.where(r1 % seg == seg - 1, jnp.exp(b_cum), 0.0)
    wide = nbt * B_DK
    mq = (lax.broadcasted_iota(I32, (TILE, wide), 0) // seg) == (lax.broadcasted_iota(I32, (TILE, wide), 1) // B_DK)
    mk = (lax.broadcasted_iota(I32, (wide, TILE), 0) // B_DK) == (lax.broadcasted_iota(I32, (wide, TILE), 1) // seg)
    o_heads = []
    for h in range(B_HEADS):
        ks = slice(h * B_DK, (h + 1) * B_DK)
        vs = slice(h * B_DV, (h + 1) * B_DV)
        state = s0_ref[:, h].reshape(wide, B_DV)
        q_bd = jnp.where(mq, jnp.concatenate([qe[:, ks]] * nbt, axis=1), 0.0)
        o_heads.append(_bdot(q_bd, state) + _bdot(att[h], v[:, vs]))
        pair_t = jnp.concatenate([ke[:, ks], e_last[:, ks]], axis=1).T
        k_bd = jnp.where(mk, jnp.concatenate([pair_t[:B_DK]] * nbt, axis=0), 0.0)
        e_bd = jnp.where(mk, jnp.concatenate([pair_t[B_DK:]] * nbt, axis=0), 0.0)
        e_col = jnp.sum(e_bd, axis=-1, keepdims=True)
        new_state = state * e_col + _bdot(k_bd, v[:, vs])
        s_ref[:, h] = new_state.reshape(nbt, B_DK, B_DV)
    _gla_finish(o_heads, rb_ref, go_ref, o_ref)


def gla_sample(proj, w_gate, b_gate, g_out, s0, layer, n_tok):
    T = proj.shape[0]
    nbt = TILE // n_tok
    kwid = B_HEADS * B_DK
    vwid = B_HEADS * B_DV
    mats = _gla_sum_matrices(n_tok)
    return pl.pallas_call(
        functools.partial(_gla_sample_kernel, seg=n_tok),
        out_shape=(jax.ShapeDtypeStruct((T, vwid), BF16),
                   jax.ShapeDtypeStruct(s0.shape[1:], F32)),
        grid=(T // TILE,),
        in_specs=[pl.BlockSpec((TILE, kwid), lambda i: (i, QB_OFF // kwid)),
                  pl.BlockSpec((TILE, kwid), lambda i: (i, KB_OFF // kwid)),
                  pl.BlockSpec((TILE, vwid), lambda i: (i, VB_OFF // vwid)),
                  pl.BlockSpec((TILE, vwid), lambda i: (i, RB_OFF // vwid)),
                  pl.BlockSpec((TILE, LANES), lambda i: (i, MISC_OFF // LANES)),
                  _resident((GATE_RANK, kwid)),
                  _resident((1, kwid)),
                  _resident((1, B_DV)),
                  _resident(mats.shape),
                  pl.BlockSpec((None, nbt, B_HEADS, B_DK, B_DV), lambda i: (layer, i, 0, 0, 0))],
        out_specs=(pl.BlockSpec((TILE, vwid), lambda i: (i, 0)),
                   pl.BlockSpec((nbt, B_HEADS, B_DK, B_DV), lambda i: (i, 0, 0, 0))),
        compiler_params=_cparams(("parallel",)),
        name="gla_sample",
    )(proj, proj, proj, proj, proj, w_gate, b_gate.reshape(1, kwid), g_out.reshape(1, B_DV), mats, s0)


def _gelu(x):
    return jax.nn.gelu(x)


def _gmlp_rows(uc_ref, vc_ref, gv_ref, ws_ref, bcol_ref, o_ref, vn_ref, seg):
    r, c, same_seg = _seg_masks(seg)
    keep = same_seg & (c <= r)
    for t in range(uc_ref.shape[0] // TILE):
        rows = slice(t * TILE, (t + 1) * TILE)
        u = _gelu(uc_ref[rows, :])
        vg = _gelu(vc_ref[rows, :])
        for g in range(C_GROUPS):
            gs = slice(g * C_GROUP_DIM, (g + 1) * C_GROUP_DIM)
            vn = _rms(vg[:, gs], gv_ref[:, gs])
            if vn_ref is not None:
                vn_ref[rows, gs] = vn
            w = jnp.where(keep, ws_ref[g], 0.0).astype(BF16)
            s = jnp.dot(w, vn.astype(BF16), preferred_element_type=F32) + bcol_ref[:, g:g + 1]
            o_ref[rows, gs] = (u[:, gs] * s).astype(o_ref.dtype)


def _ffn_kernel(h_ref, oa_ref, ob_ref, oc_ref, wo_ref, g_ref, wg_ref, wu_ref, wd_ref, o_ref, n_ref):
    j = pl.program_id(1)

    @pl.when(j == 0)
    def _():
        aw = oa_ref.shape[1]
        bw = ob_ref.shape[1]
        h = h_ref[...] + jnp.dot(oa_ref[...], wo_ref[0:aw, :], preferred_element_type=F32)
        h = h + jnp.dot(ob_ref[...], wo_ref[aw:aw + bw, :], preferred_element_type=F32)
        h = h + jnp.dot(oc_ref[...], wo_ref[aw + bw:, :], preferred_element_type=F32)
        n_ref[...] = _rms(h, g_ref[...]).astype(BF16)
        o_ref[...] = h

    n = n_ref[...]
    a = jnp.dot(n, wg_ref[...], preferred_element_type=F32)
    u = jnp.dot(n, wu_ref[...], preferred_element_type=F32)
    act = (a * jax.nn.sigmoid(a) * u).astype(BF16)
    o_ref[...] += jnp.dot(act, wd_ref[...], preferred_element_type=F32)


def out_proj_ffn(h, o_a, o_b, o_c, w_out, g, w_gate, w_up, w_down, tm, tf):
    T, D = h.shape
    tm = min(tm, T)
    FF = w_gate.shape[1]
    return pl.pallas_call(
        _ffn_kernel,
        out_shape=jax.ShapeDtypeStruct((T, D), F32),
        grid=(T // tm, FF // tf),
        in_specs=[pl.BlockSpec((tm, D), lambda i, j: (i, 0)),
                  pl.BlockSpec((tm, o_a.shape[1]), lambda i, j: (i, 0)),
                  pl.BlockSpec((tm, o_b.shape[1]), lambda i, j: (i, 0)),
                  pl.BlockSpec((tm, o_c.shape[1]), lambda i, j: (i, 0)),
                  _resident(w_out.shape),
                  _resident((1, D)),
                  pl.BlockSpec((D, tf), lambda i, j: (0, j)),
                  pl.BlockSpec((D, tf), lambda i, j: (0, j)),
                  pl.BlockSpec((tf, D), lambda i, j: (j, 0))],
        out_specs=pl.BlockSpec((tm, D), lambda i, j: (i, 0)),
        scratch_shapes=[pltpu.VMEM((tm, D), BF16)],
        compiler_params=_cparams(("parallel", "arbitrary")),
        name="out_proj_ffn",
    )(h, o_a, o_b, o_c, w_out, g.reshape(1, D), w_gate, w_up, w_down)


def _ple_kernel(h_ref, p_ref, g_ref, wgate_ref, wproj_ref, o_ref):
    h = h_ref[...]
    n = _rms(h, g_ref[...]).astype(BF16)
    gate = jax.nn.sigmoid(jnp.dot(n, wgate_ref[...], preferred_element_type=F32))
    emb = jnp.dot(p_ref[...].astype(BF16), wproj_ref[...], preferred_element_type=F32)
    o_ref[...] = h + gate * emb


def ple(h, p, g, w_gate, w_proj, layer, tm):
    T, D = h.shape
    tm = min(tm, T)
    P = p.shape[2]
    return pl.pallas_call(
        _ple_kernel,
        out_shape=jax.ShapeDtypeStruct((T, D), F32),
        grid=(T // tm,),
        in_specs=[pl.BlockSpec((tm, D), lambda i: (i, 0)),
                  pl.BlockSpec((None, tm, P), lambda i: (layer, i, 0)),
                  _resident((1, D)),
                  _resident(w_gate.shape),
                  _resident(w_proj.shape)],
        out_specs=pl.BlockSpec((tm, D), lambda i: (i, 0)),
        compiler_params=_cparams(("parallel",)),
        name="ple",
    )(h, p, g.reshape(1, D), w_gate, w_proj)


_W_IN_SEGMENTS = (("q", 1024), ("k", 256), ("v", 256), ("qi", 1024), ("ki", 64), ("wi", 16), ("qb", 256),
                  ("kb", 256), ("vb", 512), ("gb", 16), ("rb", 512), ("uc", 512), ("vc", 512))
_W_IN_PACKED_ORDER = ("q", "qi", "vb", "rb", "uc", "vc", "k", "v", "qb", "kb", "ki", "wi", "gb")


def _pack_kernel(wt_ref, o_ref):
    src, start = {}, 0
    for name, size in _W_IN_SEGMENTS:
        src[name] = (start, size)
        start += size
    dst = 0
    small = []
    for name in _W_IN_PACKED_ORDER:
        s0, size = src[name]
        if size < LANES:
            small.append(wt_ref[s0:s0 + size, :])
            continue
        o_ref[:, dst:dst + size] = wt_ref[s0:s0 + size, :].T.astype(BF16)
        dst += size
    used = sum(x.shape[0] for x in small)
    small.append(jnp.zeros((LANES - used, wt_ref.shape[1]), F32))
    o_ref[:, dst:dst + LANES] = jnp.concatenate(small, axis=0).T.astype(BF16)


def _pack_w_in(w, tr=256):
    depth, D, N = w.shape
    return pl.pallas_call(
        _pack_kernel,
        out_shape=jax.ShapeDtypeStruct((depth, D, PROJ_PACKED), BF16),
        grid=(depth, D // tr),
        in_specs=[pl.BlockSpec((None, N, tr), lambda l, i: (l, 0, i))],
        out_specs=pl.BlockSpec((None, tr, PROJ_PACKED), lambda l, i: (l, i, 0)),
        compiler_params=_cparams(("parallel", "parallel")),
        name="pack_w_in",
    )(jnp.swapaxes(w, 1, 2))


def _mixer_tail(h, o_a, o_b, o_c, p_all, lw, layer):
    h = out_proj_ffn(h, o_a, o_b, o_c, lw["w_out"], lw["g_ffn"], lw["w_ffn_gate"], lw["w_ffn_up"],
                     lw["w_ffn_down"], TM_FFN, TF_FFN)
    return ple(h, p_all, lw["g_ple"], lw["w_ple_gate"], lw["w_ple_proj"], layer, TM_PLE)


def kernel(x_prompt, x_sample, cache_k, cache_v, cache_idx_k, state_gla, page_table, p_prompt, p_sample,
           g_mix, w_in, q_norm_g, k_norm_g, rel_bias, w_gate_b, b_gate_b, g_out_b, g_v_c, w_spatial,
           b_spatial, w_out, g_ffn, w_ffn_gate, w_ffn_up, w_ffn_down, g_ple, w_ple_gate, w_ple_proj):
    n_batch, seq, d_model = x_prompt.shape
    dec_batch, dec_seq, _ = x_sample.shape
    depth = w_in.shape[0]
    n_pages = page_table.shape[1]
    past = n_pages * PAGE_SIZE
    kw = A_KV_HEADS * HEAD_DIM
    tp, ts = n_batch * seq, dec_batch * dec_seq
    rows_pad = SUBLANES
    l_sample = past + PAGE_SIZE
    k_sel_s = min(TOPK_MAX, (past + dec_seq) // 4)

    bias_p = bias_table_prompt(rel_bias)
    bias_s = bias_table_sample(rel_bias, past, dec_seq, A_KV_HEADS * l_sample)
    cache_ik_t = jnp.swapaxes(cache_idx_k, 2, 3)
    cache_k2 = cache_k.reshape(depth, cache_k.shape[1], PAGE_SIZE * A_KV_HEADS, HEAD_DIM)
    cache_v2 = cache_v.reshape(depth, cache_v.shape[1], PAGE_SIZE * A_KV_HEADS, HEAD_DIM)

    hp = x_prompt.reshape(tp, d_model)
    hs = x_sample.reshape(ts, d_model)
    outs = {k: [] for k in ("kp", "vp", "ikp", "sp", "ks", "vs", "iks", "ss", "cs")}
    per_s = rows_pad // dec_seq
    place_t = (jnp.arange(rows_pad)[:, None, None]
               == jnp.arange(per_s)[None, :, None] * dec_seq + jnp.arange(dec_seq)[None, None, :]
               ).astype(F32)
    w_ple_proj_b = w_ple_proj.astype(BF16)
    w_packed = _pack_w_in(w_in)
    pp_all = p_prompt.reshape(depth, tp, -1)
    ps_all = p_sample.reshape(depth, ts, -1)
    b_cols_p = jnp.swapaxes(b_spatial, 1, 2)
    reps = TILE // dec_seq
    w_tiles_s = jnp.tile(w_spatial[:, :, :dec_seq, :dec_seq], (1, 1, reps, reps))
    b_cols_s = jnp.tile(jnp.swapaxes(b_spatial[:, :, :dec_seq], 1, 2), (1, reps, 1))
    for i in range(depth):
        proj, (kn, vv, ik, o_c, vt), (wo_b, wg_b, wu_b, wd_b, wpg_b) = in_projection(
            hp, g_mix[i], w_packed, k_norm_g[i], (g_v_c[i], w_spatial, b_cols_p, TILE), i, TM_PROJ, True, False,
            cast=(w_out, w_ffn_gate, w_ffn_up, w_ffn_down, w_ple_gate))
        lw = dict(w_out=wo_b, g_ffn=g_ffn[i], w_ffn_gate=wg_b, w_ffn_up=wu_b, w_ffn_down=wd_b,
                  g_ple=g_ple[i], w_ple_gate=wpg_b, w_ple_proj=w_ple_proj_b[i])
        o_a = dsa_prompt(proj, kn, vt, bias_p, q_norm_g[i], n_batch, seq)
        o_b, s_p = gla_prompt(proj, w_gate_b[i], b_gate_b[i], g_out_b[i], n_batch, seq)
        hp = _mixer_tail(hp, o_a, o_b, o_c, pp_all, lw, i)
        outs["kp"].append(kn.reshape(n_batch, seq, A_KV_HEADS, HEAD_DIM))
        outs["vp"].append(vv.reshape(n_batch, seq, A_KV_HEADS, HEAD_DIM))
        outs["ikp"].append(ik.reshape(n_batch, seq, IDX_DIM))
        outs["sp"].append(s_p)

        proj, (kn, vv, ik, o_c, vn), _ = in_projection(
            hs, g_mix[i], w_packed, k_norm_g[i], (g_v_c[i], w_tiles_s, b_cols_s, dec_seq), i, TM_PROJ, False, True)
        qi_rows = proj[:, QI_OFF:QI_OFF + IDX_HEADS * IDX_DIM].reshape(dec_batch, dec_seq * IDX_HEADS, IDX_DIM)
        wi = proj[:, MISC_OFF + MISC_WI:MISC_OFF + MISC_WI + IDX_HEADS].reshape(dec_batch, dec_seq, IDX_HEADS)
        wi = wi * (IDX_HEADS ** -0.5 * IDX_DIM ** -0.5)
        wmat = (place_t[None, :, :, :, None] * wi.reshape(dec_batch // per_s, 1, per_s, dec_seq, IDX_HEADS)
                ).reshape(dec_batch // per_s, rows_pad, per_s * dec_seq * IDX_HEADS)
        ki_new_t = jnp.pad(jnp.swapaxes(ik.reshape(dec_batch, dec_seq, IDX_DIM), 1, 2),
                           ((0, 0), (0, 0), (0, PAGE_SIZE - dec_seq)))
        mask = dsa_sample_select(cache_ik_t, i, page_table, qi_rows, wmat, ki_new_t, dec_seq, k_sel_s)
        q_rows = proj[:, Q_OFF:Q_OFF + A_HEADS * HEAD_DIM].reshape(dec_batch, dec_seq * A_HEADS, HEAD_DIM)
        o_a = dsa_sample_attend(cache_k2, cache_v2, i, page_table, q_rows,
                                kn.reshape(dec_batch, dec_seq * A_KV_HEADS, HEAD_DIM),
                                vv.reshape(dec_batch, dec_seq * A_KV_HEADS, HEAD_DIM), mask, bias_s, q_norm_g[i])
        o_a = o_a.reshape(ts, A_HEADS * HEAD_DIM)
        o_b, s_s = gla_sample(proj, w_gate_b[i], b_gate_b[i], g_out_b[i], state_gla, i, dec_seq)
        hs = _mixer_tail(hs, o_a, o_b, o_c, ps_all, lw, i)
        outs["ks"].append(kn.reshape(dec_batch, dec_seq, A_KV_HEADS, HEAD_DIM))
        outs["vs"].append(vv.reshape(dec_batch, dec_seq, A_KV_HEADS, HEAD_DIM))
        outs["iks"].append(ik.reshape(dec_batch, dec_seq, IDX_DIM))
        outs["ss"].append(s_s)
        outs["cs"].append(vn.reshape(dec_batch, dec_seq, -1))

    st = {k: jnp.stack(v) for k, v in outs.items()}
    return (hp.reshape(n_batch, seq, d_model), hs.reshape(dec_batch, dec_seq, d_model),
            st["kp"], st["vp"], st["ikp"], st["sp"], st["ks"], st["vs"], st["iks"], st["ss"], st["cs"])
```

```python
import functools
import math

import jax
import jax.numpy as jnp
from jax import lax
from jax.experimental import pallas as pl
from jax.experimental.pallas import tpu as pltpu

F32 = jnp.float32
BF16 = jnp.bfloat16
I32 = jnp.int32
HIGHEST = lax.Precision.HIGHEST

LANES = 128
SUBLANES = 8
VMEM_LIMIT = 60 * 1024 * 1024

HEAD_DIM = 128
A_HEADS = 8
A_KV_HEADS = 2
IDX_HEADS = 16
IDX_DIM = 64
TOPK_MAX = 256
NUM_BUCKETS = 32
MAX_DISTANCE = 128
B_HEADS = 4
B_DK = 64
B_DV = 128
GATE_RANK = 16
GATE_TEMP = 16.0
C_GROUPS = 4
C_GROUP_DIM = 128
PAGE_SIZE = 128
EPS = 1e-6
NEG_BIG = -1e30
INT_MIN = -(2 ** 31)
NEG_INF_KEY = -2139095041

TILE = 128
QBLK = 256
TM_PROJ = 256
TM_FFN = 512
TF_FFN = 512
TM_PLE = 512

Q_OFF, QI_OFF, VB_OFF, RB_OFF, UC_OFF, VC_OFF = 0, 1024, 2048, 2560, 3072, 3584
K_OFF, V_OFF, QB_OFF, KB_OFF, MISC_OFF = 4096, 4352, 4608, 4864, 5120
PROJ_PACKED = 5248
MISC_KI, MISC_WI, MISC_GB = 0, 64, 80


def _cparams(sem):
    return pltpu.CompilerParams(dimension_semantics=sem, vmem_limit_bytes=VMEM_LIMIT)


def _rms(x, g):
    return x * lax.rsqrt(jnp.mean(x * x, axis=-1, keepdims=True) + EPS) * g


def _resident(shape):
    nd = len(shape)
    return pl.BlockSpec(shape, lambda *_: (0,) * nd, pipeline_mode=pl.Buffered(1))


def _layer_resident(shape, layer):
    nd = len(shape)
    return pl.BlockSpec((None,) + tuple(shape), lambda *_: (layer,) + (0,) * nd, pipeline_mode=pl.Buffered(1))


def _proj_kernel(x_ref, g_ref, w_ref, kg_ref, qg_ref, gv_ref, ws_ref, bcol_ref, *refs, n_cast, with_vt, with_vn,
                 seg):
    cast_in, refs = list(refs[:n_cast]), list(refs[n_cast:])
    o_ref, ko_ref, vo_ref, io_ref, oc_ref = refs[:5]
    refs = refs[5:]
    vt_ref, k3_ref, v3_ref, qn_ref, qst_ref = [refs.pop(0) for _ in range(5)] if with_vt else [None] * 5
    vn_ref = refs.pop(0) if with_vn else None
    cast_out = refs
    n = _rms(x_ref[...], g_ref[...]).astype(BF16)
    ncol = o_ref.shape[1]
    step = 512
    for c0 in range(0, ncol, step):
        c1 = min(c0 + step, ncol)
        o_ref[:, c0:c1] = jnp.dot(n, w_ref[:, c0:c1], preferred_element_type=F32)
    kg = kg_ref[...]
    v = o_ref[:, V_OFF:V_OFF + A_KV_HEADS * HEAD_DIM]
    for hh in range(A_KV_HEADS):
        hs = slice(hh * HEAD_DIM, (hh + 1) * HEAD_DIM)
        kn = _rms(o_ref[:, K_OFF + hh * HEAD_DIM:K_OFF + (hh + 1) * HEAD_DIM], kg)
        ko_ref[:, hs] = kn
        if with_vt:
            k3_ref[:, hh, :] = kn
            v3_ref[:, hh, :] = v[:, hs]
    vo_ref[...] = v
    io_ref[...] = o_ref[:, MISC_OFF + MISC_KI:MISC_OFF + MISC_KI + IDX_DIM]
    if with_vt:
        for blk in range(vt_ref.shape[0]):
            vt_ref[blk] = v[blk * QBLK:(blk + 1) * QBLK, :].T.astype(vt_ref.dtype)
        tm = o_ref.shape[0]
        qg = qg_ref[...]
        for h in range(A_HEADS):
            q = o_ref[:, Q_OFF + h * HEAD_DIM:Q_OFF + (h + 1) * HEAD_DIM]
            qn_ref[h * tm:(h + 1) * tm, :] = (_rms(q, qg) * HEAD_DIM ** -0.5).astype(qn_ref.dtype)
        for h in range(IDX_HEADS):
            qst_ref[h * tm:(h + 1) * tm, :] = o_ref[:, QI_OFF + h * IDX_DIM:QI_OFF + (h + 1) * IDX_DIM].astype(
                qst_ref.dtype)
    cw = C_GROUPS * C_GROUP_DIM
    _gmlp_rows(o_ref.at[:, UC_OFF:UC_OFF + cw], o_ref.at[:, VC_OFF:VC_OFF + cw], gv_ref, ws_ref, bcol_ref,
               oc_ref, vn_ref, seg)
    for src, dst in zip(cast_in, cast_out):
        dst[...] = src[...].astype(dst.dtype)


def in_projection(h, g, w_packed, k_norm_g, q_norm_g, gmlp_params, layer, tm, with_vt, with_vn, cast=()):
    T, D = h.shape
    tm = min(tm, T)
    N = w_packed.shape[2]
    kw = A_KV_HEADS * HEAD_DIM
    cw = C_GROUPS * C_GROUP_DIM
    g_v, w_tiles, b_cols, seg = gmlp_params
    steps = T // tm
    cast_specs_in = [pl.BlockSpec((None, w.shape[1] // steps, w.shape[2]), lambda i: (layer, i, 0)) for w in cast]
    cast_specs_out = [pl.BlockSpec((w.shape[1] // steps, w.shape[2]), lambda i: (i, 0)) for w in cast]
    kv_shape = [jax.ShapeDtypeStruct((T, kw), F32),
                jax.ShapeDtypeStruct((T, kw), F32),
                jax.ShapeDtypeStruct((T, IDX_DIM), F32),
                jax.ShapeDtypeStruct((T, cw), BF16)]
    kv_specs = [pl.BlockSpec((tm, kw), lambda i: (i, 0)),
                pl.BlockSpec((tm, kw), lambda i: (i, 0)),
                pl.BlockSpec((tm, IDX_DIM), lambda i: (i, 0)),
                pl.BlockSpec((tm, cw), lambda i: (i, 0))]
    if with_vt:
        kv_shape.append(jax.ShapeDtypeStruct((T // QBLK, kw, QBLK), BF16))
        kv_specs.append(pl.BlockSpec((tm // QBLK, kw, QBLK), lambda i: (i, 0, 0)))
        for _ in range(2):
            kv_shape.append(jax.ShapeDtypeStruct((T, A_KV_HEADS, HEAD_DIM), F32))
            kv_specs.append(pl.BlockSpec((tm, A_KV_HEADS, HEAD_DIM), lambda i: (i, 0, 0)))
        assert tm == QBLK, "the prompt attention takes one query block per projection step"
        for rows, width in ((A_HEADS * tm, HEAD_DIM), (IDX_HEADS * tm, IDX_DIM)):
            kv_shape.append(jax.ShapeDtypeStruct((steps, rows, width), BF16))
            kv_specs.append(pl.BlockSpec((None, rows, width), lambda i: (i, 0, 0)))
    if with_vn:
        kv_shape.append(jax.ShapeDtypeStruct((T, cw), F32))
        kv_specs.append(pl.BlockSpec((tm, cw), lambda i: (i, 0)))
    outs = pl.pallas_call(
        functools.partial(_proj_kernel, n_cast=len(cast), with_vt=with_vt, with_vn=with_vn, seg=seg),
        out_shape=(jax.ShapeDtypeStruct((T, N), F32),) + tuple(kv_shape) + tuple(
            jax.ShapeDtypeStruct(w.shape[1:], BF16) for w in cast),
        grid=(steps,),
        in_specs=[pl.BlockSpec((tm, D), lambda i: (i, 0)),
                  _resident((1, D)),
                  _layer_resident((D, N), layer),
                  _resident((1, HEAD_DIM)),
                  _resident((1, HEAD_DIM)),
                  _resident((1, cw)),
                  _layer_resident((C_GROUPS, TILE, TILE), layer),
                  _layer_resident((TILE, C_GROUPS), layer)] + cast_specs_in,
        out_specs=(pl.BlockSpec((tm, N), lambda i: (i, 0)),) + tuple(kv_specs) + tuple(cast_specs_out),
        compiler_params=_cparams(("parallel",)),
        name="in_projection",
    )(h, g.reshape(1, D), w_packed, k_norm_g.reshape(1, HEAD_DIM), q_norm_g.reshape(1, HEAD_DIM),
      g_v.reshape(1, cw), w_tiles, b_cols, *cast)
    n_kv = len(kv_shape)
    return outs[0], outs[1:1 + n_kv], outs[1 + n_kv:]


def _bucket(dist):
    n = jnp.maximum(dist, 0)
    max_exact = NUM_BUCKETS // 2
    large = max_exact + (jnp.log(jnp.maximum(n, 1).astype(F32) / max_exact)
                         / math.log(MAX_DISTANCE / max_exact)
                         * (NUM_BUCKETS - max_exact)).astype(I32)
    large = jnp.minimum(large, NUM_BUCKETS - 1)
    return jnp.where(n < max_exact, n, large)


def _bias_prompt_kernel(rb_ref, o_ref):
    c = lax.broadcasted_iota(I32, (TILE, TILE), 0)
    t = lax.broadcasted_iota(I32, (TILE, TILE), 1)
    for z in range(3):
        bucket = _bucket(t - c + (2 - z) * TILE)
        for h in range(A_HEADS):
            acc = jnp.zeros((TILE, TILE), F32)
            for b in range(NUM_BUCKETS):
                acc = jnp.where(bucket == b, rb_ref[b, h], acc)
            o_ref[h, z] = acc


def bias_table_prompt(rel_bias):
    return pl.pallas_call(
        _bias_prompt_kernel,
        out_shape=jax.ShapeDtypeStruct((A_HEADS, 3, TILE, TILE), F32),
        in_specs=[pl.BlockSpec(memory_space=pltpu.SMEM)],
        out_specs=pl.BlockSpec(memory_space=pltpu.VMEM),
        name="bias_table_prompt",
    )(rel_bias)


def _bias_sample_kernel(rbrows_ref, o_ref, *, past, n_tok):
    rows, L = o_ref.shape
    r = lax.broadcasted_iota(I32, (rows, L), 0)
    s = lax.broadcasted_iota(I32, (rows, L), 1) // A_KV_HEADS
    bucket = _bucket(past + r // A_HEADS - s)
    rbrows = rbrows_ref[...]
    acc = jnp.zeros((rows, L), F32)
    for b in range(NUM_BUCKETS):
        acc = jnp.where(bucket == b, rbrows[:, b:b + 1], acc)
    o_ref[...] = acc


def bias_table_sample(rel_bias, past, n_tok, L):
    rows = n_tok * A_HEADS
    rbrows = jnp.tile(rel_bias.T, (n_tok, 1))
    return pl.pallas_call(
        functools.partial(_bias_sample_kernel, past=past, n_tok=n_tok),
        out_shape=jax.ShapeDtypeStruct((rows, L), F32),
        name="bias_table_sample",
    )(rbrows)


def _sortable_key(x):
    b = lax.bitcast_convert_type(x, I32)
    return b ^ ((b >> 31) & 0x7FFFFFFF)


def _topk_member(skey_ref, k_sel):
    R, L = skey_ref.shape

    def body(it, ans):
        bit = 31 - it
        cand = ans | lax.shift_left(jnp.int32(1), bit)
        cand_s = cand ^ INT_MIN
        cnt = jnp.sum(jnp.where(skey_ref[...] >= cand_s, 1.0, 0.0), axis=-1, keepdims=True)
        return jnp.where(cnt >= k_sel, cand, ans)

    ans = lax.fori_loop(0, 32, body, jnp.zeros((R, 1), I32))
    tau = ans ^ INT_MIN
    skey = skey_ref[...]
    gt = skey > tau
    eq = skey == tau
    n_gt = jnp.sum(jnp.where(gt, 1.0, 0.0), axis=-1, keepdims=True)
    room = k_sel - n_gt
    r_i = lax.broadcasted_iota(I32, (LANES, LANES), 0)
    c_i = lax.broadcasted_iota(I32, (LANES, LANES), 1)
    upper = jnp.where(r_i <= c_i, 1.0, 0.0).astype(BF16)
    off = jnp.zeros((R, 1), F32)
    parts = []
    for j in range(L // LANES):
        sl = slice(j * LANES, (j + 1) * LANES)
        eq_j = eq[:, sl]
        run = jnp.dot(jnp.where(eq_j, 1.0, 0.0).astype(BF16), upper, preferred_element_type=F32) + off
        parts.append(gt[:, sl] | (eq_j & (run <= room)))
        off = run[:, LANES - 1:LANES]
    return jnp.concatenate(parts, axis=1)


def _fold8(x, op):
    return op(x.reshape(x.shape[0] // SUBLANES, SUBLANES, x.shape[1]), axis=0)


def _dsa_prompt_kernel(qn_ref, qst_ref, misc_ref, kn_ref, vt_ref, bias_ref, o_ref,
                       skey_ref, madd_ref, lg_ref, acc_ref, *, k_sel):
    i = pl.program_id(1)
    nkb = i + 1
    sub = QBLK // TILE
    rep = A_HEADS // A_KV_HEADS
    row0 = pl.multiple_of(i * QBLK, QBLK)
    s_iota = lax.broadcasted_iota(I32, (QBLK, QBLK), 0)
    t_iota = lax.broadcasted_iota(I32, (QBLK, QBLK), 1)

    def admissible(j):
        return (j * QBLK + s_iota) <= (row0 + t_iota)

    wi_t = misc_ref[pl.ds(row0, QBLK), :].T[MISC_WI:MISC_WI + IDX_HEADS, :]
    wi_t = wi_t * (IDX_HEADS ** -0.5 * IDX_DIM ** -0.5)

    def score_body(j, carry):
        k0 = pl.multiple_of(j * QBLK, QBLK)
        kj = misc_ref[pl.ds(k0, QBLK), MISC_KI:MISC_KI + IDX_DIM].astype(BF16)
        s = lax.dot_general(kj, qst_ref[...], (((1,), (1,)), ((), ())), preferred_element_type=F32)
        score = jnp.zeros((QBLK, QBLK), F32)
        for h in range(IDX_HEADS):
            score = score + jnp.maximum(s[:, h * QBLK:(h + 1) * QBLK], 0.0) * wi_t[h:h + 1, :]
        skey_ref[j] = _sortable_key(jnp.where(admissible(j), score, -jnp.inf))
        return carry

    lax.fori_loop(0, nkb, score_body, 0)

    def count(pred_fn):
        def hits(j):
            return _fold8(jnp.where(pred_fn(skey_ref[j]), 1.0, 0.0), jnp.sum)

        def body(jj, accs):
            return accs[0] + hits(2 * jj), accs[1] + hits(2 * jj + 1)

        zero = jnp.zeros((SUBLANES, QBLK), F32)
        acc0, acc1 = lax.fori_loop(0, nkb // 2, body, (zero, zero))
        acc = lax.cond(nkb % 2 == 1, lambda: acc0 + acc1 + hits(nkb - 1), lambda: acc0 + acc1)
        return jnp.sum(acc, axis=0, keepdims=True)

    def bit_body(it, ans):
        cand = ans | lax.shift_left(jnp.int32(1), 31 - it)
        cand_s = cand ^ INT_MIN
        cnt = count(lambda key: key >= cand_s)
        return jnp.where(cnt >= k_sel, cand, ans)

    ans = lax.fori_loop(0, 32, bit_body, jnp.zeros((1, QBLK), I32))
    tau = ans ^ INT_MIN
    n_ge = count(lambda key: key >= tau)
    excess = jnp.max(jnp.where((n_ge > k_sel) & (tau != NEG_INF_KEY), 1.0, 0.0))

    @pl.when(excess == 0.0)
    def _():
        def mask_body(j, carry):
            madd_ref[j] = jnp.where((skey_ref[j] >= tau) & admissible(j), 0.0, NEG_BIG)
            return carry

        lax.fori_loop(0, nkb, mask_body, 0)

    @pl.when(excess > 0.0)
    def _():
        room = k_sel - count(lambda key: key > tau)
        lower = jnp.where(t_iota <= s_iota, 1.0, 0.0).astype(BF16)

        def mask_body(j, off):
            key = skey_ref[j]
            eq = key == tau
            run = jnp.dot(lower, jnp.where(eq, 1.0, 0.0).astype(BF16), preferred_element_type=F32) + off
            sel = ((key > tau) | (eq & (run <= room))) & admissible(j)
            madd_ref[j] = jnp.where(sel, 0.0, NEG_BIG)
            return run[QBLK - 1:QBLK, :]

        lax.fori_loop(0, nkb, mask_body, jnp.zeros((1, QBLK), F32))

    wide = rep * QBLK
    for g in range(A_KV_HEADS):
        gs = slice(g * HEAD_DIM, (g + 1) * HEAD_DIM)
        heads = list(range(g * rep, (g + 1) * rep))
        q_stack = qn_ref[g * wide:(g + 1) * wide, :]

        def logit_body(j, mx):
            k0 = pl.multiple_of(j * QBLK, QBLK)
            kj = kn_ref[pl.ds(k0, QBLK), gs].astype(BF16)
            lg = lax.dot_general(kj, q_stack, (((1,), (1,)), ((), ())), preferred_element_type=F32)
            madd = madd_ref[j]
            parts = []
            for r, h in enumerate(heads):
                quads = []
                for c in range(sub):
                    quads.append(jnp.concatenate(
                        [bias_ref[h, jnp.clip(2 - ((i - j) * sub + u - c), 0, 2)] for u in range(sub)], axis=1))
                parts.append(lg[:, r * QBLK:(r + 1) * QBLK] + jnp.concatenate(quads, axis=0) + madd)
            lg = jnp.concatenate(parts, axis=1)
            lg_ref[j] = lg
            return jnp.maximum(mx, _fold8(lg, jnp.max))

        mx = lax.fori_loop(0, nkb, logit_body, jnp.full((SUBLANES, wide), NEG_BIG, F32))
        m = jnp.max(mx, axis=0, keepdims=True)
        acc_ref[...] = jnp.zeros(acc_ref.shape, F32)

        def pv_body(j, sm):
            p = jnp.exp(lg_ref[j] - m)
            acc_ref[...] += jnp.dot(vt_ref[j, gs, :], p.astype(BF16), preferred_element_type=F32)
            return sm + _fold8(p, jnp.sum)

        sm = lax.fori_loop(0, nkb, pv_body, jnp.zeros((SUBLANES, wide), F32))
        den = jnp.sum(sm, axis=0, keepdims=True)
        o = (acc_ref[...] / den).T
        for r, h in enumerate(heads):
            o_ref[:, h * HEAD_DIM:(h + 1) * HEAD_DIM] = o[r * QBLK:(r + 1) * QBLK, :].astype(o_ref.dtype)


def dsa_prompt(proj, qn, qst, kn, vt, bias_tab, n_batch, seq):
    T = proj.shape[0]
    nb = seq // QBLK
    k_sel = min(TOPK_MAX, seq // 4)
    aw = A_HEADS * HEAD_DIM
    kw = A_KV_HEADS * HEAD_DIM
    rep = A_HEADS // A_KV_HEADS
    return pl.pallas_call(
        functools.partial(_dsa_prompt_kernel, k_sel=k_sel),
        out_shape=jax.ShapeDtypeStruct((T, aw), BF16),
        grid=(n_batch, nb),
        in_specs=[pl.BlockSpec((None,) + qn.shape[1:], lambda b, i: (b * nb + i, 0, 0)),
                  pl.BlockSpec((None,) + qst.shape[1:], lambda b, i: (b * nb + i, 0, 0)),
                  pl.BlockSpec((seq, LANES), lambda b, i: (b, MISC_OFF // LANES)),
                  pl.BlockSpec((seq, kw), lambda b, i: (b, 0)),
                  pl.BlockSpec((nb, kw, QBLK), lambda b, i: (b, 0, 0)),
                  _resident((A_HEADS, 3, TILE, TILE))],
        out_specs=pl.BlockSpec((QBLK, aw), lambda b, i: (b * nb + i, 0)),
        scratch_shapes=[pltpu.VMEM((nb, QBLK, QBLK), I32),
                        pltpu.VMEM((nb, QBLK, QBLK), F32),
                        pltpu.VMEM((nb, QBLK, rep * QBLK), F32),
                        pltpu.VMEM((HEAD_DIM, rep * QBLK), F32)],
        compiler_params=_cparams(("parallel", "arbitrary")),
        name="dsa_prompt",
    )(qn, qst, proj, kn, vt, bias_tab)


def _dsa_sample_select_kernel(pt_ref, *refs, n_pages, n_tok, k_sel, rows_pad):
    del pt_ref
    per = rows_pad // n_tok
    page_refs = refs[:per * n_pages]
    qi_ref, wm_ref, kin_ref, mask_ref, sc_ref, skey_ref = refs[per * n_pages:]
    b = pl.program_id(0)
    nb = pl.num_programs(0)
    L = sc_ref.shape[1]
    past = n_pages * PAGE_SIZE

    relu_s = []
    for e in range(per):
        kt_all = jnp.concatenate([r[...].astype(BF16) for r in page_refs[e * n_pages:(e + 1) * n_pages]]
                                 + [kin_ref[e].astype(BF16)], axis=1)
        relu_s.append(jnp.maximum(jnp.dot(qi_ref[e].astype(BF16), kt_all, preferred_element_type=F32), 0.0))
    relu_cat = jnp.concatenate(relu_s, axis=0)
    wm = wm_ref[0]
    r_hi, w_hi = relu_cat.astype(BF16), wm.astype(BF16)
    r_lo, w_lo = (relu_cat - r_hi.astype(F32)).astype(BF16), (wm - w_hi.astype(F32)).astype(BF16)
    score = (jnp.dot(w_hi, r_hi, preferred_element_type=F32) + jnp.dot(w_hi, r_lo, preferred_element_type=F32)
             + jnp.dot(w_lo, r_hi, preferred_element_type=F32))
    r0 = pl.multiple_of(b * rows_pad, rows_pad)
    sc_ref[pl.ds(r0, rows_pad), :] = score

    @pl.when(b == nb - 1)
    def _():
        n_blocks = sc_ref.shape[0] // TILE
        n_tiles = L // PAGE_SIZE
        tp = past + lax.broadcasted_iota(I32, (TILE, L), 0) % n_tok
        sp = lax.broadcasted_iota(I32, (TILE, L), 1)
        adm_blk = sp <= tp
        d_r = lax.broadcasted_iota(I32, (PAGE_SIZE, PAGE_SIZE * A_KV_HEADS), 0)
        d_c = lax.broadcasted_iota(I32, (PAGE_SIZE, PAGE_SIZE * A_KV_HEADS), 1)
        dup = jnp.where(d_c // A_KV_HEADS == d_r, 1.0, 0.0).astype(BF16)
        for rb in range(n_blocks):
            rows = slice(rb * TILE, (rb + 1) * TILE)
            skey_ref[...] = _sortable_key(jnp.where(adm_blk, sc_ref[rows, :], -jnp.inf))
            sel = jnp.where(_topk_member(skey_ref, k_sel) & adm_blk, 1.0, 0.0).astype(BF16)
            stacked = jnp.concatenate([sel[:, j * PAGE_SIZE:(j + 1) * PAGE_SIZE] for j in range(n_tiles)], axis=0)
            stacked = jnp.dot(stacked, dup, preferred_element_type=F32)
            mask_ref[rows, :] = jnp.concatenate(
                [stacked[j * TILE:(j + 1) * TILE, :] for j in range(n_tiles)], axis=1)


def dsa_sample_select(cache_ik_t, layer, page_table, qi_rows, wmat, ki_new_t, n_tok, k_sel):
    DB, n_pages = page_table.shape
    rows_pad = wmat.shape[1]
    per = rows_pad // n_tok
    n_rows = DB // per * rows_pad
    L = (n_pages + 1) * PAGE_SIZE
    page_specs = [pl.BlockSpec((None, None, IDX_DIM, PAGE_SIZE), functools.partial(
        lambda b, pt, e, p: (layer, pt[b * per + e, p], 0, 0), e=e, p=p))
        for e in range(per) for p in range(n_pages)]
    grid_spec = pltpu.PrefetchScalarGridSpec(
        num_scalar_prefetch=1,
        grid=(DB // per,),
        in_specs=page_specs + [
            pl.BlockSpec((per,) + qi_rows.shape[1:], lambda b, pt: (b, 0, 0)),
            pl.BlockSpec((1,) + wmat.shape[1:], lambda b, pt: (b, 0, 0)),
            pl.BlockSpec((per, IDX_DIM, PAGE_SIZE), lambda b, pt: (b, 0, 0))],
        out_specs=pl.BlockSpec((n_rows, A_KV_HEADS * L), lambda b, pt: (0, 0)),
        scratch_shapes=[pltpu.VMEM((n_rows, L), F32),
                        pltpu.VMEM((TILE, L), I32)],
    )
    return pl.pallas_call(
        functools.partial(_dsa_sample_select_kernel, n_pages=n_pages, n_tok=n_tok, k_sel=k_sel,
                          rows_pad=rows_pad),
        out_shape=jax.ShapeDtypeStruct((n_rows, A_KV_HEADS * L), F32),
        grid_spec=grid_spec,
        compiler_params=_cparams(("arbitrary",)),
        name="dsa_sample_select",
    )(page_table, *([cache_ik_t] * (per * n_pages)), qi_rows, wmat, ki_new_t)


def _dsa_sample_attend_kernel(pt_ref, *refs, n_pages, n_tok, rows_pad):
    del pt_ref
    per = rows_pad // n_tok
    k_refs = refs[:per * n_pages]
    v_refs = refs[per * n_pages:2 * per * n_pages]
    q_ref, kn_ref, vn_ref, mask_ref, bias_ref, qg_ref, o_ref = refs[2 * per * n_pages:]
    rows = n_tok * A_HEADS
    page_rows = PAGE_SIZE * A_KV_HEADS
    n_cols = mask_ref.shape[1]
    pad = jnp.zeros((page_rows - n_tok * A_KV_HEADS, HEAD_DIM), BF16)
    rep = A_HEADS // A_KV_HEADS
    grp = (lax.broadcasted_iota(I32, (rows, 1), 0) % A_HEADS) // rep
    own_group = (lax.broadcasted_iota(I32, (rows, n_cols), 1) % A_KV_HEADS) == grp
    member = mask_ref[...]

    for e in range(per):
        def tiles(page_refs, new_ref):
            new = jnp.concatenate([new_ref[e].astype(BF16), pad], axis=0)
            return [r[...].astype(BF16) for r in page_refs[e * n_pages:(e + 1) * n_pages]] + [new]

        q = (_rms(q_ref[e], qg_ref[...]) * HEAD_DIM ** -0.5).astype(BF16)
        sel = jnp.concatenate(
            [jnp.broadcast_to(member[e * n_tok + t:e * n_tok + t + 1, :], (A_HEADS, n_cols)) for t in range(n_tok)],
            axis=0)
        valid = (sel > 0.5) & own_group
        logits = jnp.concatenate(
            [lax.dot_general(q, kt, (((1,), (1,)), ((), ())), preferred_element_type=F32)
             for kt in tiles(k_refs, kn_ref)], axis=1)
        logits = jnp.where(valid, logits + bias_ref[...], NEG_BIG)
        m = jnp.max(logits, axis=-1, keepdims=True)
        p = jnp.exp(logits - m)
        den = jnp.sum(p, axis=-1, keepdims=True)
        pb = p.astype(BF16)
        o = jnp.zeros((rows, HEAD_DIM), F32)
        for j, vt in enumerate(tiles(v_refs, vn_ref)):
            o = o + jnp.dot(pb[:, j * page_rows:(j + 1) * page_rows], vt, preferred_element_type=F32)
        o_ref[e] = (o / den).astype(o_ref.dtype)


def dsa_sample_attend(cache_k, cache_v, layer, page_table, q_rows, k_new, v_new, mask, bias_tab, q_norm_g):
    DB, n_pages = page_table.shape
    n_tok = k_new.shape[1] // A_KV_HEADS
    rows = n_tok * A_HEADS
    rows_pad = SUBLANES
    per = rows_pad // n_tok
    n_cols = mask.shape[1]
    page_rows = PAGE_SIZE * A_KV_HEADS
    page_specs = [pl.BlockSpec((None, None, page_rows, HEAD_DIM), functools.partial(
        lambda b, pt, e, p: (layer, pt[b * per + e, p], 0, 0), e=e, p=p))
        for e in range(per) for p in range(n_pages)]
    grid_spec = pltpu.PrefetchScalarGridSpec(
        num_scalar_prefetch=1,
        grid=(DB // per,),
        in_specs=page_specs + page_specs + [
            pl.BlockSpec((per, rows, HEAD_DIM), lambda b, pt: (b, 0, 0)),
            pl.BlockSpec((per, n_tok * A_KV_HEADS, HEAD_DIM), lambda b, pt: (b, 0, 0)),
            pl.BlockSpec((per, n_tok * A_KV_HEADS, HEAD_DIM), lambda b, pt: (b, 0, 0)),
            pl.BlockSpec((rows_pad, n_cols), lambda b, pt: (b, 0)),
            pl.BlockSpec((rows, n_cols), lambda b, pt: (0, 0), pipeline_mode=pl.Buffered(1)),
            pl.BlockSpec((1, HEAD_DIM), lambda b, pt: (0, 0), pipeline_mode=pl.Buffered(1))],
        out_specs=pl.BlockSpec((per, rows, HEAD_DIM), lambda b, pt: (b, 0, 0)),
    )
    return pl.pallas_call(
        functools.partial(_dsa_sample_attend_kernel, n_pages=n_pages, n_tok=n_tok, rows_pad=rows_pad),
        out_shape=jax.ShapeDtypeStruct((DB, rows, HEAD_DIM), BF16),
        grid_spec=grid_spec,
        compiler_params=_cparams(("parallel",)),
        name="dsa_sample_attend",
    )(page_table, *([cache_k] * (per * n_pages)), *([cache_v] * (per * n_pages)), q_rows, k_new, v_new, mask,
      bias_tab, q_norm_g.reshape(1, HEAD_DIM))


def _log_sigmoid(z):
    return jnp.minimum(z, 0.0) - jnp.log(1.0 + jnp.exp(-jnp.abs(z)))


def _seg_masks(seg):
    r = lax.broadcasted_iota(I32, (TILE, TILE), 0)
    c = lax.broadcasted_iota(I32, (TILE, TILE), 1)
    return r, c, (r // seg) == (c // seg)


def _gla_levels(seg):
    w, out = seg // 2, []
    while w >= 1:
        out.append(w)
        w //= 2
    return out


def _gla_sum_matrices(seg):
    r = jnp.arange(TILE)[:, None]
    c = jnp.arange(TILE)[None, :]
    mats = []
    for w in _gla_levels(seg):
        same = (r // (2 * w)) == (c // (2 * w))
        r_right = (r % (2 * w)) >= w
        c_right = (c % (2 * w)) >= w
        mats.append(same & r_right & c_right & (c <= r))
    for w in _gla_levels(seg):
        same = (r // (2 * w)) == (c // (2 * w))
        r_right = (r % (2 * w)) >= w
        c_right = (c % (2 * w)) >= w
        mats.append(same & (~r_right) & (~c_right) & (c > r))
    same_seg = (r // seg) == (c // seg)
    mats.append(same_seg & (c <= r))
    mats.append(same_seg & (c > r))
    return jnp.concatenate(mats, axis=0).astype(BF16)


def _bdot(a, b):
    return jnp.dot(a.astype(BF16), b.astype(BF16), preferred_element_type=F32)


def _bdot_nt(a, b):
    return lax.dot_general(a.astype(BF16), b.astype(BF16), (((1,), (1,)), ((), ())), preferred_element_type=F32)


def _gla_common(tiles, wg_ref, bg_ref, mats_ref, seg):
    n = len(tiles)
    kw = B_HEADS * B_DK
    las = []
    for _, _, misc_ref in tiles:
        gb = misc_ref[:, MISC_GB:MISC_GB + GATE_RANK]
        z = jnp.dot(gb, wg_ref[...], precision=HIGHEST, preferred_element_type=F32) + bg_ref[...]
        las.append(_log_sigmoid(z) / GATE_TEMP)
    la = jnp.concatenate(las, axis=1)
    la_hi = la.astype(BF16)
    la_lo = (la - la_hi.astype(F32)).astype(BF16)
    mats = mats_ref[...]
    sums = (jnp.dot(mats, la_hi, preferred_element_type=F32) + jnp.dot(mats, la_lo, preferred_element_type=F32))
    levels = _gla_levels(seg)
    nl = len(levels)
    qs = [qb_ref[...] * B_DK ** -0.5 for qb_ref, _, _ in tiles]
    ks = [kb_ref[...] for _, kb_ref, _ in tiles]
    r, c, _ = _seg_masks(seg)
    atts = [[jnp.where(r == c, _bdot_nt(qs[i][:, h * B_DK:(h + 1) * B_DK], ks[i][:, h * B_DK:(h + 1) * B_DK]), 0.0)
             for h in range(B_HEADS)] for i in range(n)]
    for li, w in enumerate(levels):
        pair = ((r // (2 * w)) == (c // (2 * w))) & ((r % (2 * w)) >= w) & ((c % (2 * w)) < w)
        qd = [(qs[i] * jnp.exp(sums[li * TILE:(li + 1) * TILE, i * kw:(i + 1) * kw])).astype(BF16) for i in range(n)]
        kd = [(ks[i] * jnp.exp(sums[(nl + li) * TILE:(nl + li + 1) * TILE, i * kw:(i + 1) * kw])).astype(BF16)
              for i in range(n)]
        for h in range(B_HEADS):
            hs = slice(h * B_DK, (h + 1) * B_DK)
            for i in range(n):
                atts[i][h] = atts[i][h] + jnp.where(pair, _bdot_nt(qd[i][:, hs], kd[i][:, hs]), 0.0)
    out = []
    for i in range(n):
        cs = slice(i * kw, (i + 1) * kw)
        b_cum = sums[2 * nl * TILE:(2 * nl + 1) * TILE, cs]
        rem = sums[(2 * nl + 1) * TILE:(2 * nl + 2) * TILE, cs]
        out.append((qs[i], ks[i], atts[i], b_cum, rem))
    return out


def _gla_finish(o_heads, rb_ref, go_ref, o_ref):
    go = go_ref[...]
    for h in range(B_HEADS):
        vs = slice(h * B_DV, (h + 1) * B_DV)
        rb = rb_ref[:, vs]
        o_ref[:, vs] = (_rms(o_heads[h], go) * (rb * jax.nn.sigmoid(rb))).astype(o_ref.dtype)


def _gla_prompt_kernel(qb_ref, kb_ref, vb_ref, rb_ref, misc_ref, wg_ref, bg_ref, go_ref, mats_ref,
                       o_ref, s_ref, state_ref):
    ci = pl.program_id(0)

    @pl.when(ci == 0)
    def _():
        state_ref[...] = jnp.zeros_like(state_ref)

    nb = qb_ref.shape[0]
    common = _gla_common([(qb_ref.at[b], kb_ref.at[b], misc_ref.at[b]) for b in range(nb)],
                         wg_ref, bg_ref, mats_ref, TILE)
    vals = [vb_ref[b] for b in range(nb)]
    states = [state_ref[b] for b in range(nb)]
    qes = [common[b][0] * jnp.exp(common[b][3]) for b in range(nb)]
    o_heads = [[] for _ in range(nb)]
    for h in range(B_HEADS):
        ks = slice(h * B_DK, (h + 1) * B_DK)
        vs = slice(h * B_DV, (h + 1) * B_DV)
        for b in range(nb):
            o_heads[b].append(_bdot(qes[b][:, ks], states[b][ks, :]) + _bdot(common[b][2][h], vals[b][:, vs]))
    for b in range(nb):
        _gla_finish(o_heads[b], rb_ref.at[b], go_ref, o_ref.at[b])

    ke_ts = [(common[b][1] * jnp.exp(common[b][4])).T for b in range(nb)]
    e_cols = [jnp.broadcast_to(jnp.exp(common[b][3][TILE - 1:TILE, :]), (TILE, B_HEADS * B_DK)).T[:, 0:1]
              for b in range(nb)]
    for b in range(nb):
        upd = jnp.concatenate(
            [_bdot(ke_ts[b][h * B_DK:(h + 1) * B_DK, :], vals[b][:, h * B_DV:(h + 1) * B_DV])
             for h in range(B_HEADS)], axis=0)
        new_state = states[b] * e_cols[b] + upd
        state_ref[b] = new_state
        s_ref[b] = new_state


def gla_prompt(proj, w_gate, b_gate, g_out, n_batch, seq):
    nc = seq // TILE
    kwid = B_HEADS * B_DK
    vwid = B_HEADS * B_DV
    mats = _gla_sum_matrices(TILE)
    proj3 = proj.reshape(n_batch, seq, proj.shape[1])
    o, s = pl.pallas_call(
        _gla_prompt_kernel,
        out_shape=(jax.ShapeDtypeStruct((n_batch, seq, vwid), BF16),
                   jax.ShapeDtypeStruct((n_batch, kwid, B_DV), F32)),
        grid=(nc,),
        in_specs=[pl.BlockSpec((n_batch, TILE, kwid), lambda c: (0, c, QB_OFF // kwid)),
                  pl.BlockSpec((n_batch, TILE, kwid), lambda c: (0, c, KB_OFF // kwid)),
                  pl.BlockSpec((n_batch, TILE, vwid), lambda c: (0, c, VB_OFF // vwid)),
                  pl.BlockSpec((n_batch, TILE, vwid), lambda c: (0, c, RB_OFF // vwid)),
                  pl.BlockSpec((n_batch, TILE, LANES), lambda c: (0, c, MISC_OFF // LANES)),
                  _resident((GATE_RANK, kwid)),
                  _resident((1, kwid)),
                  _resident((1, B_DV)),
                  _resident(mats.shape)],
        out_specs=(pl.BlockSpec((n_batch, TILE, vwid), lambda c: (0, c, 0)),
                   pl.BlockSpec((n_batch, kwid, B_DV), lambda c: (0, 0, 0))),
        scratch_shapes=[pltpu.VMEM((n_batch, kwid, B_DV), F32)],
        compiler_params=_cparams(("arbitrary",)),
        name="gla_prompt",
    )(proj3, proj3, proj3, proj3, proj3, w_gate, b_gate.reshape(1, kwid), g_out.reshape(1, B_DV), mats)
    return o.reshape(n_batch * seq, vwid), s.reshape(n_batch, B_HEADS, B_DK, B_DV)


def _gla_sample_kernel(qb_ref, kb_ref, vb_ref, rb_ref, misc_ref, wg_ref, bg_ref, go_ref, mats_ref, s0_ref,
                       o_ref, s_ref, *, seg):
    nbt = TILE // seg
    (q, k, att, b_cum, rem), = _gla_common([(qb_ref, kb_ref, misc_ref)], wg_ref, bg_ref, mats_ref, seg)
    v = vb_ref[...]
    qe = q * jnp.exp(b_cum)
    ke = k * jnp.exp(rem)
    r1 = lax.broadcasted_iota(I32, (TILE, 1), 0)
    e_last = jnp.where(r1 % seg == seg - 1, jnp.exp(b_cum), 0.0)
    wide = nbt * B_DK
    mq = (lax.broadcasted_iota(I32, (TILE, wide), 0) // seg) == (lax.broadcasted_iota(I32, (TILE, wide), 1) // B_DK)
    mk = (lax.broadcasted_iota(I32, (wide, TILE), 0) // B_DK) == (lax.broadcasted_iota(I32, (wide, TILE), 1) // seg)
    o_heads = []
    for h in range(B_HEADS):
        ks = slice(h * B_DK, (h + 1) * B_DK)
        vs = slice(h * B_DV, (h + 1) * B_DV)
        state = s0_ref[:, h].reshape(wide, B_DV)
        q_bd = jnp.where(mq, jnp.concatenate([qe[:, ks]] * nbt, axis=1), 0.0)
        o_heads.append(_bdot(q_bd, state) + _bdot(att[h], v[:, vs]))
        pair_t = jnp.concatenate([ke[:, ks], e_last[:, ks]], axis=1).T
        k_bd = jnp.where(mk, jnp.concatenate([pair_t[:B_DK]] * nbt, axis=0), 0.0)
        e_bd = jnp.where(mk, jnp.concatenate([pair_t[B_DK:]] * nbt, axis=0), 0.0)
        e_col = jnp.sum(e_bd, axis=-1, keepdims=True)
        new_state = state * e_col + _bdot(k_bd, v[:, vs])
        s_ref[:, h] = new_state.reshape(nbt, B_DK, B_DV)
    _gla_finish(o_heads, rb_ref, go_ref, o_ref)


def gla_sample(proj, w_gate, b_gate, g_out, s0, layer, n_tok):
    T = proj.shape[0]
    nbt = TILE // n_tok
    kwid = B_HEADS * B_DK
    vwid = B_HEADS * B_DV
    mats = _gla_sum_matrices(n_tok)
    return pl.pallas_call(
        functools.partial(_gla_sample_kernel, seg=n_tok),
        out_shape=(jax.ShapeDtypeStruct((T, vwid), BF16),
                   jax.ShapeDtypeStruct(s0.shape[1:], F32)),
        grid=(T // TILE,),
        in_specs=[pl.BlockSpec((TILE, kwid), lambda i: (i, QB_OFF // kwid)),
                  pl.BlockSpec((TILE, kwid), lambda i: (i, KB_OFF // kwid)),
                  pl.BlockSpec((TILE, vwid), lambda i: (i, VB_OFF // vwid)),
                  pl.BlockSpec((TILE, vwid), lambda i: (i, RB_OFF // vwid)),
                  pl.BlockSpec((TILE, LANES), lambda i: (i, MISC_OFF // LANES)),
                  _resident((GATE_RANK, kwid)),
                  _resident((1, kwid)),
                  _resident((1, B_DV)),
                  _resident(mats.shape),
                  pl.BlockSpec((None, nbt, B_HEADS, B_DK, B_DV), lambda i: (layer, i, 0, 0, 0))],
        out_specs=(pl.BlockSpec((TILE, vwid), lambda i: (i, 0)),
                   pl.BlockSpec((nbt, B_HEADS, B_DK, B_DV), lambda i: (i, 0, 0, 0))),
        compiler_params=_cparams(("parallel",)),
        name="gla_sample",
    )(proj, proj, proj, proj, proj, w_gate, b_gate.reshape(1, kwid), g_out.reshape(1, B_DV), mats, s0)


def _gelu(x):
    return jax.nn.gelu(x)


def _gmlp_rows(uc_ref, vc_ref, gv_ref, ws_ref, bcol_ref, o_ref, vn_ref, seg):
    r, c, same_seg = _seg_masks(seg)
    keep = same_seg & (c <= r)
    for t in range(uc_ref.shape[0] // TILE):
        rows = slice(t * TILE, (t + 1) * TILE)
        u = _gelu(uc_ref[rows, :])
        vg = _gelu(vc_ref[rows, :])
        for g in range(C_GROUPS):
            gs = slice(g * C_GROUP_DIM, (g + 1) * C_GROUP_DIM)
            vn = _rms(vg[:, gs], gv_ref[:, gs])
            if vn_ref is not None:
                vn_ref[rows, gs] = vn
            w = jnp.where(keep, ws_ref[g], 0.0).astype(BF16)
            s = jnp.dot(w, vn.astype(BF16), preferred_element_type=F32) + bcol_ref[:, g:g + 1]
            o_ref[rows, gs] = (u[:, gs] * s).astype(o_ref.dtype)


def _ffn_kernel(h_ref, oa_ref, ob_ref, oc_ref, wo_ref, g_ref, wg_ref, wu_ref, wd_ref, o_ref, n_ref):
    j = pl.program_id(1)

    @pl.when(j == 0)
    def _():
        aw = oa_ref.shape[1]
        bw = ob_ref.shape[1]
        h = h_ref[...] + jnp.dot(oa_ref[...], wo_ref[0:aw, :], preferred_element_type=F32)
        h = h + jnp.dot(ob_ref[...], wo_ref[aw:aw + bw, :], preferred_element_type=F32)
        h = h + jnp.dot(oc_ref[...], wo_ref[aw + bw:, :], preferred_element_type=F32)
        n_ref[...] = _rms(h, g_ref[...]).astype(BF16)
        o_ref[...] = h

    n = n_ref[...]
    a = jnp.dot(n, wg_ref[...], preferred_element_type=F32)
    u = jnp.dot(n, wu_ref[...], preferred_element_type=F32)
    act = (a * jax.nn.sigmoid(a) * u).astype(BF16)
    o_ref[...] += jnp.dot(act, wd_ref[...], preferred_element_type=F32)


def out_proj_ffn(h, o_a, o_b, o_c, w_out, g, w_gate, w_up, w_down, tm, tf):
    T, D = h.shape
    tm = min(tm, T)
    FF = w_gate.shape[1]
    return pl.pallas_call(
        _ffn_kernel,
        out_shape=jax.ShapeDtypeStruct((T, D), F32),
        grid=(T // tm, FF // tf),
        in_specs=[pl.BlockSpec((tm, D), lambda i, j: (i, 0)),
                  pl.BlockSpec((tm, o_a.shape[1]), lambda i, j: (i, 0)),
                  pl.BlockSpec((tm, o_b.shape[1]), lambda i, j: (i, 0)),
                  pl.BlockSpec((tm, o_c.shape[1]), lambda i, j: (i, 0)),
                  _resident(w_out.shape),
                  _resident((1, D)),
                  pl.BlockSpec((D, tf), lambda i, j: (0, j)),
                  pl.BlockSpec((D, tf), lambda i, j: (0, j)),
                  pl.BlockSpec((tf, D), lambda i, j: (j, 0))],
        out_specs=pl.BlockSpec((tm, D), lambda i, j: (i, 0)),
        scratch_shapes=[pltpu.VMEM((tm, D), BF16)],
        compiler_params=_cparams(("parallel", "arbitrary")),
        name="out_proj_ffn",
    )(h, o_a, o_b, o_c, w_out, g.reshape(1, D), w_gate, w_up, w_down)


def _ple_kernel(h_ref, p_ref, g_ref, wgate_ref, wproj_ref, o_ref):
    h = h_ref[...]
    n = _rms(h, g_ref[...]).astype(BF16)
    gate = jax.nn.sigmoid(jnp.dot(n, wgate_ref[...], preferred_element_type=F32))
    emb = jnp.dot(p_ref[...].astype(BF16), wproj_ref[...], preferred_element_type=F32)
    o_ref[...] = h + gate * emb


def ple(h, p, g, w_gate, w_proj, layer, tm):
    T, D = h.shape
    tm = min(tm, T)
    P = p.shape[2]
    return pl.pallas_call(
        _ple_kernel,
        out_shape=jax.ShapeDtypeStruct((T, D), F32),
        grid=(T // tm,),
        in_specs=[pl.BlockSpec((tm, D), lambda i: (i, 0)),
                  pl.BlockSpec((None, tm, P), lambda i: (layer, i, 0)),
                  _resident((1, D)),
                  _resident(w_gate.shape),
                  _resident(w_proj.shape)],
        out_specs=pl.BlockSpec((tm, D), lambda i: (i, 0)),
        compiler_params=_cparams(("parallel",)),
        name="ple",
    )(h, p, g.reshape(1, D), w_gate, w_proj)


_W_IN_SEGMENTS = (("q", 1024), ("k", 256), ("v", 256), ("qi", 1024), ("ki", 64), ("wi", 16), ("qb", 256),
                  ("kb", 256), ("vb", 512), ("gb", 16), ("rb", 512), ("uc", 512), ("vc", 512))
_W_IN_PACKED_ORDER = ("q", "qi", "vb", "rb", "uc", "vc", "k", "v", "qb", "kb", "ki", "wi", "gb")


def _pack_kernel(wt_ref, o_ref):
    src, start = {}, 0
    for name, size in _W_IN_SEGMENTS:
        src[name] = (start, size)
        start += size
    dst = 0
    small = []
    for name in _W_IN_PACKED_ORDER:
        s0, size = src[name]
        if size < LANES:
            small.append(wt_ref[s0:s0 + size, :])
            continue
        o_ref[:, dst:dst + size] = wt_ref[s0:s0 + size, :].T.astype(BF16)
        dst += size
    used = sum(x.shape[0] for x in small)
    small.append(jnp.zeros((LANES - used, wt_ref.shape[1]), F32))
    o_ref[:, dst:dst + LANES] = jnp.concatenate(small, axis=0).T.astype(BF16)


def _pack_w_in(w, tr=256):
    depth, D, N = w.shape
    return pl.pallas_call(
        _pack_kernel,
        out_shape=jax.ShapeDtypeStruct((depth, D, PROJ_PACKED), BF16),
        grid=(depth, D // tr),
        in_specs=[pl.BlockSpec((None, N, tr), lambda l, i: (l, 0, i))],
        out_specs=pl.BlockSpec((None, tr, PROJ_PACKED), lambda l, i: (l, i, 0)),
        compiler_params=_cparams(("parallel", "parallel")),
        name="pack_w_in",
    )(jnp.swapaxes(w, 1, 2))


def _mixer_tail(h, o_a, o_b, o_c, p_all, lw, layer):
    h = out_proj_ffn(h, o_a, o_b, o_c, lw["w_out"], lw["g_ffn"], lw["w_ffn_gate"], lw["w_ffn_up"],
                     lw["w_ffn_down"], TM_FFN, TF_FFN)
    return ple(h, p_all, lw["g_ple"], lw["w_ple_gate"], lw["w_ple_proj"], layer, TM_PLE)


def kernel(x_prompt, x_sample, cache_k, cache_v, cache_idx_k, state_gla, page_table, p_prompt, p_sample,
           g_mix, w_in, q_norm_g, k_norm_g, rel_bias, w_gate_b, b_gate_b, g_out_b, g_v_c, w_spatial,
           b_spatial, w_out, g_ffn, w_ffn_gate, w_ffn_up, w_ffn_down, g_ple, w_ple_gate, w_ple_proj):
    n_batch, seq, d_model = x_prompt.shape
    dec_batch, dec_seq, _ = x_sample.shape
    depth = w_in.shape[0]
    n_pages = page_table.shape[1]
    past = n_pages * PAGE_SIZE
    kw = A_KV_HEADS * HEAD_DIM
    tp, ts = n_batch * seq, dec_batch * dec_seq
    rows_pad = SUBLANES
    l_sample = past + PAGE_SIZE
    k_sel_s = min(TOPK_MAX, (past + dec_seq) // 4)

    bias_p = bias_table_prompt(rel_bias)
    bias_s = bias_table_sample(rel_bias, past, dec_seq, A_KV_HEADS * l_sample)
    cache_ik_t = jnp.swapaxes(cache_idx_k, 2, 3)
    cache_k2 = cache_k.reshape(depth, cache_k.shape[1], PAGE_SIZE * A_KV_HEADS, HEAD_DIM)
    cache_v2 = cache_v.reshape(depth, cache_v.shape[1], PAGE_SIZE * A_KV_HEADS, HEAD_DIM)

    hp = x_prompt.reshape(tp, d_model)
    hs = x_sample.reshape(ts, d_model)
    outs = {k: [] for k in ("kp", "vp", "ikp", "sp", "ks", "vs", "iks", "ss", "cs")}
    per_s = rows_pad // dec_seq
    place_t = (jnp.arange(rows_pad)[:, None, None]
               == jnp.arange(per_s)[None, :, None] * dec_seq + jnp.arange(dec_seq)[None, None, :]
               ).astype(F32)
    w_ple_proj_b = w_ple_proj.astype(BF16)
    w_packed = _pack_w_in(w_in)
    pp_all = p_prompt.reshape(depth, tp, -1)
    ps_all = p_sample.reshape(depth, ts, -1)
    b_cols_p = jnp.swapaxes(b_spatial, 1, 2)
    reps = TILE // dec_seq
    w_tiles_s = jnp.tile(w_spatial[:, :, :dec_seq, :dec_seq], (1, 1, reps, reps))
    b_cols_s = jnp.tile(jnp.swapaxes(b_spatial[:, :, :dec_seq], 1, 2), (1, reps, 1))
    for i in range(depth):
        proj, (kn, vv, ik, o_c, vt, k3, v3, qn, qst), (wo_b, wg_b, wu_b, wd_b, wpg_b) = in_projection(
            hp, g_mix[i], w_packed, k_norm_g[i], q_norm_g[i], (g_v_c[i], w_spatial, b_cols_p, TILE), i, TM_PROJ,
            True, False, cast=(w_out, w_ffn_gate, w_ffn_up, w_ffn_down, w_ple_gate))
        lw = dict(w_out=wo_b, g_ffn=g_ffn[i], w_ffn_gate=wg_b, w_ffn_up=wu_b, w_ffn_down=wd_b,
                  g_ple=g_ple[i], w_ple_gate=wpg_b, w_ple_proj=w_ple_proj_b[i])
        o_a = dsa_prompt(proj, qn, qst, kn, vt, bias_p, n_batch, seq)
        o_b, s_p = gla_prompt(proj, w_gate_b[i], b_gate_b[i], g_out_b[i], n_batch, seq)
        hp = _mixer_tail(hp, o_a, o_b, o_c, pp_all, lw, i)
        outs["kp"].append(k3.reshape(n_batch, seq, A_KV_HEADS, HEAD_DIM))
        outs["vp"].append(v3.reshape(n_batch, seq, A_KV_HEADS, HEAD_DIM))
        outs["ikp"].append(ik.reshape(n_batch, seq, IDX_DIM))
        outs["sp"].append(s_p)

        proj, (kn, vv, ik, o_c, vn), _ = in_projection(
            hs, g_mix[i], w_packed, k_norm_g[i], q_norm_g[i], (g_v_c[i], w_tiles_s, b_cols_s, dec_seq), i, TM_PROJ,
            False, True)
        qi_rows = proj[:, QI_OFF:QI_OFF + IDX_HEADS * IDX_DIM].reshape(dec_batch, dec_seq * IDX_HEADS, IDX_DIM)
        wi = proj[:, MISC_OFF + MISC_WI:MISC_OFF + MISC_WI + IDX_HEADS].reshape(dec_batch, dec_seq, IDX_HEADS)
        wi = wi * (IDX_HEADS ** -0.5 * IDX_DIM ** -0.5)
        wmat = (place_t[None, :, :, :, None] * wi.reshape(dec_batch // per_s, 1, per_s, dec_seq, IDX_HEADS)
                ).reshape(dec_batch // per_s, rows_pad, per_s * dec_seq * IDX_HEADS)
        ki_new_t = jnp.pad(jnp.swapaxes(ik.reshape(dec_batch, dec_seq, IDX_DIM), 1, 2),
                           ((0, 0), (0, 0), (0, PAGE_SIZE - dec_seq)))
        mask = dsa_sample_select(cache_ik_t, i, page_table, qi_rows, wmat, ki_new_t, dec_seq, k_sel_s)
        q_rows = proj[:, Q_OFF:Q_OFF + A_HEADS * HEAD_DIM].reshape(dec_batch, dec_seq * A_HEADS, HEAD_DIM)
        o_a = dsa_sample_attend(cache_k2, cache_v2, i, page_table, q_rows,
                                kn.reshape(dec_batch, dec_seq * A_KV_HEADS, HEAD_DIM),
                                vv.reshape(dec_batch, dec_seq * A_KV_HEADS, HEAD_DIM), mask, bias_s, q_norm_g[i])
        o_a = o_a.reshape(ts, A_HEADS * HEAD_DIM)
        o_b, s_s = gla_sample(proj, w_gate_b[i], b_gate_b[i], g_out_b[i], state_gla, i, dec_seq)
        hs = _mixer_tail(hs, o_a, o_b, o_c, ps_all, lw, i)
        outs["ks"].append(kn.reshape(dec_batch, dec_seq, A_KV_HEADS, HEAD_DIM))
        outs["vs"].append(vv.reshape(dec_batch, dec_seq, A_KV_HEADS, HEAD_DIM))
        outs["iks"].append(ik.reshape(dec_batch, dec_seq, IDX_DIM))
        outs["ss"].append(s_s)
        outs["cs"].append(vn.reshape(dec_batch, dec_seq, -1))

    st = {k: jnp.stack(v) for k, v in outs.items()}
    return (hp.reshape(n_batch, seq, d_model), hs.reshape(dec_batch, dec_seq, d_model),
            st["kp"], st["vp"], st["ikp"], st["sp"], st["ks"], st["vs"], st["iks"], st["ss"], st["cs"])
```

```python
import functools
import math

import jax
import jax.numpy as jnp
from jax import lax
from jax.experimental import pallas as pl
from jax.experimental.pallas import tpu as pltpu

F32 = jnp.float32
BF16 = jnp.bfloat16
I32 = jnp.int32
HIGHEST = lax.Precision.HIGHEST

LANES = 128
SUBLANES = 8
VMEM_LIMIT = 60 * 1024 * 1024

HEAD_DIM = 128
A_HEADS = 8
A_KV_HEADS = 2
IDX_HEADS = 16
IDX_DIM = 64
TOPK_MAX = 256
NUM_BUCKETS = 32
MAX_DISTANCE = 128
B_HEADS = 4
B_DK = 64
B_DV = 128
GATE_RANK = 16
GATE_TEMP = 16.0
C_GROUPS = 4
C_GROUP_DIM = 128
PAGE_SIZE = 128
EPS = 1e-6
NEG_BIG = -1e30
INT_MIN = -(2 ** 31)
NEG_INF_KEY = -2139095041

TILE = 128
QBLK = 256
TM_PROJ = 256
TM_FFN = 512
TF_FFN = 512
TM_PLE = 512

Q_OFF, QI_OFF, VB_OFF, RB_OFF, UC_OFF, VC_OFF = 0, 1024, 2048, 2560, 3072, 3584
K_OFF, V_OFF, QB_OFF, KB_OFF, MISC_OFF = 4096, 4352, 4608, 4864, 5120
PROJ_PACKED = 5248
MISC_KI, MISC_WI, MISC_GB = 0, 64, 80


def _cparams(sem):
    return pltpu.CompilerParams(dimension_semantics=sem, vmem_limit_bytes=VMEM_LIMIT)


def _rms(x, g):
    return x * lax.rsqrt(jnp.mean(x * x, axis=-1, keepdims=True) + EPS) * g


def _resident(shape):
    nd = len(shape)
    return pl.BlockSpec(shape, lambda *_: (0,) * nd, pipeline_mode=pl.Buffered(1))


def _layer_resident(shape, layer):
    nd = len(shape)
    return pl.BlockSpec((None,) + tuple(shape), lambda *_: (layer,) + (0,) * nd, pipeline_mode=pl.Buffered(1))


def _proj_kernel(x_ref, g_ref, w_ref, kg_ref, qg_ref, gv_ref, ws_ref, bcol_ref, *refs, n_cast, with_vt, with_vn,
                 seg):
    cast_in, refs = list(refs[:n_cast]), list(refs[n_cast:])
    o_ref, ko_ref, vo_ref, io_ref, oc_ref = refs[:5]
    refs = refs[5:]
    vt_ref, k3_ref, v3_ref, qn_ref, qst_ref = [refs.pop(0) for _ in range(5)] if with_vt else [None] * 5
    vn_ref = refs.pop(0) if with_vn else None
    cast_out = refs
    n = _rms(x_ref[...], g_ref[...]).astype(BF16)
    ncol = o_ref.shape[1]
    step = 512
    for c0 in range(0, ncol, step):
        c1 = min(c0 + step, ncol)
        o_ref[:, c0:c1] = jnp.dot(n, w_ref[:, c0:c1], preferred_element_type=F32)
    kg = kg_ref[...]
    v = o_ref[:, V_OFF:V_OFF + A_KV_HEADS * HEAD_DIM]
    for hh in range(A_KV_HEADS):
        hs = slice(hh * HEAD_DIM, (hh + 1) * HEAD_DIM)
        kn = _rms(o_ref[:, K_OFF + hh * HEAD_DIM:K_OFF + (hh + 1) * HEAD_DIM], kg)
        ko_ref[:, hs] = kn
        if with_vt:
            k3_ref[:, hh, :] = kn
            v3_ref[:, hh, :] = v[:, hs]
    vo_ref[...] = v
    io_ref[...] = o_ref[:, MISC_OFF + MISC_KI:MISC_OFF + MISC_KI + IDX_DIM]
    if with_vt:
        for blk in range(vt_ref.shape[0]):
            vt_ref[blk] = v[blk * QBLK:(blk + 1) * QBLK, :].T.astype(vt_ref.dtype)
        tm = o_ref.shape[0]
        qg = qg_ref[...]
        for h in range(A_HEADS):
            q = o_ref[:, Q_OFF + h * HEAD_DIM:Q_OFF + (h + 1) * HEAD_DIM]
            qn_ref[h * tm:(h + 1) * tm, :] = (_rms(q, qg) * HEAD_DIM ** -0.5).astype(qn_ref.dtype)
        for h in range(IDX_HEADS):
            qst_ref[h * tm:(h + 1) * tm, :] = o_ref[:, QI_OFF + h * IDX_DIM:QI_OFF + (h + 1) * IDX_DIM].astype(
                qst_ref.dtype)
    cw = C_GROUPS * C_GROUP_DIM
    _gmlp_rows(o_ref.at[:, UC_OFF:UC_OFF + cw], o_ref.at[:, VC_OFF:VC_OFF + cw], gv_ref, ws_ref, bcol_ref,
               oc_ref, vn_ref, seg)
    for src, dst in zip(cast_in, cast_out):
        dst[...] = src[...].astype(dst.dtype)


def in_projection(h, g, w_packed, k_norm_g, q_norm_g, gmlp_params, layer, tm, with_vt, with_vn, cast=()):
    T, D = h.shape
    tm = min(tm, T)
    N = w_packed.shape[2]
    kw = A_KV_HEADS * HEAD_DIM
    cw = C_GROUPS * C_GROUP_DIM
    g_v, w_tiles, b_cols, seg = gmlp_params
    steps = T // tm
    cast_specs_in = [pl.BlockSpec((None, w.shape[1] // steps, w.shape[2]), lambda i: (layer, i, 0)) for w in cast]
    cast_specs_out = [pl.BlockSpec((w.shape[1] // steps, w.shape[2]), lambda i: (i, 0)) for w in cast]
    kv_shape = [jax.ShapeDtypeStruct((T, kw), F32),
                jax.ShapeDtypeStruct((T, kw), F32),
                jax.ShapeDtypeStruct((T, IDX_DIM), F32),
                jax.ShapeDtypeStruct((T, cw), BF16)]
    kv_specs = [pl.BlockSpec((tm, kw), lambda i: (i, 0)),
                pl.BlockSpec((tm, kw), lambda i: (i, 0)),
                pl.BlockSpec((tm, IDX_DIM), lambda i: (i, 0)),
                pl.BlockSpec((tm, cw), lambda i: (i, 0))]
    if with_vt:
        kv_shape.append(jax.ShapeDtypeStruct((T // QBLK, kw, QBLK), BF16))
        kv_specs.append(pl.BlockSpec((tm // QBLK, kw, QBLK), lambda i: (i, 0, 0)))
        for _ in range(2):
            kv_shape.append(jax.ShapeDtypeStruct((T, A_KV_HEADS, HEAD_DIM), F32))
            kv_specs.append(pl.BlockSpec((tm, A_KV_HEADS, HEAD_DIM), lambda i: (i, 0, 0)))
        assert tm == QBLK, "the prompt attention takes one query block per projection step"
        for rows, width in ((A_HEADS * tm, HEAD_DIM), (IDX_HEADS * tm, IDX_DIM)):
            kv_shape.append(jax.ShapeDtypeStruct((steps, rows, width), BF16))
            kv_specs.append(pl.BlockSpec((None, rows, width), lambda i: (i, 0, 0)))
    if with_vn:
        kv_shape.append(jax.ShapeDtypeStruct((T, cw), F32))
        kv_specs.append(pl.BlockSpec((tm, cw), lambda i: (i, 0)))
    outs = pl.pallas_call(
        functools.partial(_proj_kernel, n_cast=len(cast), with_vt=with_vt, with_vn=with_vn, seg=seg),
        out_shape=(jax.ShapeDtypeStruct((T, N), F32),) + tuple(kv_shape) + tuple(
            jax.ShapeDtypeStruct(w.shape[1:], BF16) for w in cast),
        grid=(steps,),
        in_specs=[pl.BlockSpec((tm, D), lambda i: (i, 0)),
                  _resident((1, D)),
                  _layer_resident((D, N), layer),
                  _resident((1, HEAD_DIM)),
                  _resident((1, HEAD_DIM)),
                  _resident((1, cw)),
                  _layer_resident((C_GROUPS, TILE, TILE), layer),
                  _layer_resident((TILE, C_GROUPS), layer)] + cast_specs_in,
        out_specs=(pl.BlockSpec((tm, N), lambda i: (i, 0)),) + tuple(kv_specs) + tuple(cast_specs_out),
        compiler_params=_cparams(("parallel",)),
        name="in_projection",
    )(h, g.reshape(1, D), w_packed, k_norm_g.reshape(1, HEAD_DIM), q_norm_g.reshape(1, HEAD_DIM),
      g_v.reshape(1, cw), w_tiles, b_cols, *cast)
    n_kv = len(kv_shape)
    return outs[0], outs[1:1 + n_kv], outs[1 + n_kv:]


def _bucket(dist):
    n = jnp.maximum(dist, 0)
    max_exact = NUM_BUCKETS // 2
    large = max_exact + (jnp.log(jnp.maximum(n, 1).astype(F32) / max_exact)
                         / math.log(MAX_DISTANCE / max_exact)
                         * (NUM_BUCKETS - max_exact)).astype(I32)
    large = jnp.minimum(large, NUM_BUCKETS - 1)
    return jnp.where(n < max_exact, n, large)


def _bias_prompt_kernel(rb_ref, o_ref):
    c = lax.broadcasted_iota(I32, (TILE, TILE), 0)
    t = lax.broadcasted_iota(I32, (TILE, TILE), 1)
    for z in range(3):
        bucket = _bucket(t - c + (2 - z) * TILE)
        for h in range(A_HEADS):
            acc = jnp.zeros((TILE, TILE), F32)
            for b in range(NUM_BUCKETS):
                acc = jnp.where(bucket == b, rb_ref[b, h], acc)
            o_ref[h, z] = acc


def bias_table_prompt(rel_bias):
    return pl.pallas_call(
        _bias_prompt_kernel,
        out_shape=jax.ShapeDtypeStruct((A_HEADS, 3, TILE, TILE), F32),
        in_specs=[pl.BlockSpec(memory_space=pltpu.SMEM)],
        out_specs=pl.BlockSpec(memory_space=pltpu.VMEM),
        name="bias_table_prompt",
    )(rel_bias)


def _bias_sample_kernel(rbrows_ref, o_ref, *, past, n_tok):
    rows, L = o_ref.shape
    r = lax.broadcasted_iota(I32, (rows, L), 0)
    s = lax.broadcasted_iota(I32, (rows, L), 1) // A_KV_HEADS
    bucket = _bucket(past + r // A_HEADS - s)
    rbrows = rbrows_ref[...]
    acc = jnp.zeros((rows, L), F32)
    for b in range(NUM_BUCKETS):
        acc = jnp.where(bucket == b, rbrows[:, b:b + 1], acc)
    o_ref[...] = acc


def bias_table_sample(rel_bias, past, n_tok, L):
    rows = n_tok * A_HEADS
    rbrows = jnp.tile(rel_bias.T, (n_tok, 1))
    return pl.pallas_call(
        functools.partial(_bias_sample_kernel, past=past, n_tok=n_tok),
        out_shape=jax.ShapeDtypeStruct((rows, L), F32),
        name="bias_table_sample",
    )(rbrows)


def _sortable_key(x):
    b = lax.bitcast_convert_type(x, I32)
    return b ^ ((b >> 31) & 0x7FFFFFFF)


def _topk_member(skey_ref, k_sel):
    R, L = skey_ref.shape

    def body(it, ans):
        bit = 31 - it
        cand = ans | lax.shift_left(jnp.int32(1), bit)
        cand_s = cand ^ INT_MIN
        cnt = jnp.sum(jnp.where(skey_ref[...] >= cand_s, 1.0, 0.0), axis=-1, keepdims=True)
        return jnp.where(cnt >= k_sel, cand, ans)

    ans = lax.fori_loop(0, 32, body, jnp.zeros((R, 1), I32))
    tau = ans ^ INT_MIN
    skey = skey_ref[...]
    gt = skey > tau
    eq = skey == tau
    n_gt = jnp.sum(jnp.where(gt, 1.0, 0.0), axis=-1, keepdims=True)
    room = k_sel - n_gt
    r_i = lax.broadcasted_iota(I32, (LANES, LANES), 0)
    c_i = lax.broadcasted_iota(I32, (LANES, LANES), 1)
    upper = jnp.where(r_i <= c_i, 1.0, 0.0).astype(BF16)
    off = jnp.zeros((R, 1), F32)
    parts = []
    for j in range(L // LANES):
        sl = slice(j * LANES, (j + 1) * LANES)
        eq_j = eq[:, sl]
        run = jnp.dot(jnp.where(eq_j, 1.0, 0.0).astype(BF16), upper, preferred_element_type=F32) + off
        parts.append(gt[:, sl] | (eq_j & (run <= room)))
        off = run[:, LANES - 1:LANES]
    return jnp.concatenate(parts, axis=1)


def _fold8(x, op):
    return op(x.reshape(x.shape[0] // SUBLANES, SUBLANES, x.shape[1]), axis=0)


def _dsa_prompt_kernel(qn_ref, qst_ref, misc_ref, kn_ref, vt_ref, bias_ref, o_ref,
                       skey_ref, madd_ref, lg_ref, acc_ref, *, k_sel):
    i = pl.program_id(1)
    nkb = i + 1
    sub = QBLK // TILE
    rep = A_HEADS // A_KV_HEADS
    row0 = pl.multiple_of(i * QBLK, QBLK)
    s_iota = lax.broadcasted_iota(I32, (QBLK, QBLK), 0)
    t_iota = lax.broadcasted_iota(I32, (QBLK, QBLK), 1)

    def admissible(j):
        return (j * QBLK + s_iota) <= (row0 + t_iota)

    wi_t = misc_ref[pl.ds(row0, QBLK), :].T[MISC_WI:MISC_WI + IDX_HEADS, :]
    wi_t = wi_t * (IDX_HEADS ** -0.5 * IDX_DIM ** -0.5)

    def score_body(j, carry):
        k0 = pl.multiple_of(j * QBLK, QBLK)
        kj = misc_ref[pl.ds(k0, QBLK), MISC_KI:MISC_KI + IDX_DIM].astype(BF16)
        s = lax.dot_general(kj, qst_ref[...], (((1,), (1,)), ((), ())), preferred_element_type=F32)
        score = jnp.zeros((QBLK, QBLK), F32)
        for h in range(IDX_HEADS):
            score = score + jnp.maximum(s[:, h * QBLK:(h + 1) * QBLK], 0.0) * wi_t[h:h + 1, :]
        skey_ref[j] = _sortable_key(jnp.where(admissible(j), score, -jnp.inf))
        return carry

    lax.fori_loop(0, nkb, score_body, 0)

    def count(pred_fn):
        def hits(j):
            return _fold8(jnp.where(pred_fn(skey_ref[j]), 1.0, 0.0), jnp.sum)

        def body(jj, accs):
            return accs[0] + hits(2 * jj), accs[1] + hits(2 * jj + 1)

        zero = jnp.zeros((SUBLANES, QBLK), F32)
        acc0, acc1 = lax.fori_loop(0, nkb // 2, body, (zero, zero))
        acc = lax.cond(nkb % 2 == 1, lambda: acc0 + acc1 + hits(nkb - 1), lambda: acc0 + acc1)
        return jnp.sum(acc, axis=0, keepdims=True)

    def bit_body(it, ans):
        cand = ans | lax.shift_left(jnp.int32(1), 31 - it)
        cand_s = cand ^ INT_MIN
        cnt = count(lambda key: key >= cand_s)
        return jnp.where(cnt >= k_sel, cand, ans)

    ans = lax.fori_loop(0, 32, bit_body, jnp.zeros((1, QBLK), I32))
    tau = ans ^ INT_MIN
    n_ge = count(lambda key: key >= tau)
    excess = jnp.max(jnp.where((n_ge > k_sel) & (tau != NEG_INF_KEY), 1.0, 0.0))

    @pl.when(excess == 0.0)
    def _():
        def mask_body(j, carry):
            madd_ref[j] = jnp.where((skey_ref[j] >= tau) & admissible(j), 0.0, NEG_BIG)
            return carry

        lax.fori_loop(0, nkb, mask_body, 0)

    @pl.when(excess > 0.0)
    def _():
        room = k_sel - count(lambda key: key > tau)
        lower = jnp.where(t_iota <= s_iota, 1.0, 0.0).astype(BF16)

        def mask_body(j, off):
            key = skey_ref[j]
            eq = key == tau
            run = jnp.dot(lower, jnp.where(eq, 1.0, 0.0).astype(BF16), preferred_element_type=F32) + off
            sel = ((key > tau) | (eq & (run <= room))) & admissible(j)
            madd_ref[j] = jnp.where(sel, 0.0, NEG_BIG)
            return run[QBLK - 1:QBLK, :]

        lax.fori_loop(0, nkb, mask_body, jnp.zeros((1, QBLK), F32))

    wide = rep * QBLK
    for g in range(A_KV_HEADS):
        gs = slice(g * HEAD_DIM, (g + 1) * HEAD_DIM)
        heads = list(range(g * rep, (g + 1) * rep))
        q_stack = qn_ref[g * wide:(g + 1) * wide, :]

        def logit_body(j, mx):
            k0 = pl.multiple_of(j * QBLK, QBLK)
            kj = kn_ref[pl.ds(k0, QBLK), gs].astype(BF16)
            lg = lax.dot_general(kj, q_stack, (((1,), (1,)), ((), ())), preferred_element_type=F32)
            madd = madd_ref[j]
            parts = []
            for r, h in enumerate(heads):
                quads = []
                for c in range(sub):
                    quads.append(jnp.concatenate(
                        [bias_ref[h, jnp.clip(2 - ((i - j) * sub + u - c), 0, 2)] for u in range(sub)], axis=1))
                parts.append(lg[:, r * QBLK:(r + 1) * QBLK] + jnp.concatenate(quads, axis=0) + madd)
            lg = jnp.concatenate(parts, axis=1)
            lg_ref[j] = lg
            return jnp.maximum(mx, _fold8(lg, jnp.max))

        mx = lax.fori_loop(0, nkb // 2, lambda jj, mx_: logit_body(2 * jj + 1, logit_body(2 * jj, mx_)),
                           jnp.full((SUBLANES, wide), NEG_BIG, F32))
        mx = lax.cond(nkb % 2 == 1, lambda mx_: logit_body(nkb - 1, mx_), lambda mx_: mx_, mx)
        m = jnp.max(mx, axis=0, keepdims=True)
        acc_ref[...] = jnp.zeros(acc_ref.shape, F32)

        def pv_body(blocks, sm):
            ps = [jnp.exp(lg_ref[j] - m) for j in blocks]
            acc_ref[...] += sum(jnp.dot(vt_ref[j, gs, :], p.astype(BF16), preferred_element_type=F32)
                                for j, p in zip(blocks, ps))
            return sm + sum(_fold8(p, jnp.sum) for p in ps)

        sm = lax.fori_loop(0, nkb // 2, lambda jj, sm_: pv_body((2 * jj, 2 * jj + 1), sm_),
                           jnp.zeros((SUBLANES, wide), F32))
        sm = lax.cond(nkb % 2 == 1, lambda sm_: pv_body((nkb - 1,), sm_), lambda sm_: sm_, sm)
        den = jnp.sum(sm, axis=0, keepdims=True)
        o = (acc_ref[...] / den).T
        for r, h in enumerate(heads):
            o_ref[:, h * HEAD_DIM:(h + 1) * HEAD_DIM] = o[r * QBLK:(r + 1) * QBLK, :].astype(o_ref.dtype)


def dsa_prompt(proj, qn, qst, kn, vt, bias_tab, n_batch, seq):
    T = proj.shape[0]
    nb = seq // QBLK
    k_sel = min(TOPK_MAX, seq // 4)
    aw = A_HEADS * HEAD_DIM
    kw = A_KV_HEADS * HEAD_DIM
    rep = A_HEADS // A_KV_HEADS
    return pl.pallas_call(
        functools.partial(_dsa_prompt_kernel, k_sel=k_sel),
        out_shape=jax.ShapeDtypeStruct((T, aw), BF16),
        grid=(n_batch, nb),
        in_specs=[pl.BlockSpec((None,) + qn.shape[1:], lambda b, i: (b * nb + i, 0, 0)),
                  pl.BlockSpec((None,) + qst.shape[1:], lambda b, i: (b * nb + i, 0, 0)),
                  pl.BlockSpec((seq, LANES), lambda b, i: (b, MISC_OFF // LANES)),
                  pl.BlockSpec((seq, kw), lambda b, i: (b, 0)),
                  pl.BlockSpec((nb, kw, QBLK), lambda b, i: (b, 0, 0)),
                  _resident((A_HEADS, 3, TILE, TILE))],
        out_specs=pl.BlockSpec((QBLK, aw), lambda b, i: (b * nb + i, 0)),
        scratch_shapes=[pltpu.VMEM((nb, QBLK, QBLK), I32),
                        pltpu.VMEM((nb, QBLK, QBLK), F32),
                        pltpu.VMEM((nb, QBLK, rep * QBLK), F32),
                        pltpu.VMEM((HEAD_DIM, rep * QBLK), F32)],
        compiler_params=_cparams(("parallel", "arbitrary")),
        name="dsa_prompt",
    )(qn, qst, proj, kn, vt, bias_tab)


def _dsa_sample_select_kernel(pt_ref, *refs, n_pages, n_tok, k_sel, rows_pad):
    del pt_ref
    per = rows_pad // n_tok
    page_refs = refs[:per * n_pages]
    qi_ref, wm_ref, kin_ref, mask_ref, sc_ref, skey_ref = refs[per * n_pages:]
    b = pl.program_id(0)
    nb = pl.num_programs(0)
    L = sc_ref.shape[1]
    past = n_pages * PAGE_SIZE

    relu_s = []
    for e in range(per):
        kt_all = jnp.concatenate([r[...].astype(BF16) for r in page_refs[e * n_pages:(e + 1) * n_pages]]
                                 + [kin_ref[e].astype(BF16)], axis=1)
        relu_s.append(jnp.maximum(jnp.dot(qi_ref[e].astype(BF16), kt_all, preferred_element_type=F32), 0.0))
    relu_cat = jnp.concatenate(relu_s, axis=0)
    wm = wm_ref[0]
    r_hi, w_hi = relu_cat.astype(BF16), wm.astype(BF16)
    r_lo, w_lo = (relu_cat - r_hi.astype(F32)).astype(BF16), (wm - w_hi.astype(F32)).astype(BF16)
    score = (jnp.dot(w_hi, r_hi, preferred_element_type=F32) + jnp.dot(w_hi, r_lo, preferred_element_type=F32)
             + jnp.dot(w_lo, r_hi, preferred_element_type=F32))
    r0 = pl.multiple_of(b * rows_pad, rows_pad)
    sc_ref[pl.ds(r0, rows_pad), :] = score

    @pl.when(b == nb - 1)
    def _():
        n_blocks = sc_ref.shape[0] // TILE
        n_tiles = L // PAGE_SIZE
        tp = past + lax.broadcasted_iota(I32, (TILE, L), 0) % n_tok
        sp = lax.broadcasted_iota(I32, (TILE, L), 1)
        adm_blk = sp <= tp
        d_r = lax.broadcasted_iota(I32, (PAGE_SIZE, PAGE_SIZE * A_KV_HEADS), 0)
        d_c = lax.broadcasted_iota(I32, (PAGE_SIZE, PAGE_SIZE * A_KV_HEADS), 1)
        dup = jnp.where(d_c // A_KV_HEADS == d_r, 1.0, 0.0).astype(BF16)
        for rb in range(n_blocks):
            rows = slice(rb * TILE, (rb + 1) * TILE)
            skey_ref[...] = _sortable_key(jnp.where(adm_blk, sc_ref[rows, :], -jnp.inf))
            sel = jnp.where(_topk_member(skey_ref, k_sel) & adm_blk, 1.0, 0.0).astype(BF16)
            stacked = jnp.concatenate([sel[:, j * PAGE_SIZE:(j + 1) * PAGE_SIZE] for j in range(n_tiles)], axis=0)
            stacked = jnp.dot(stacked, dup, preferred_element_type=F32)
            mask_ref[rows, :] = jnp.concatenate(
                [stacked[j * TILE:(j + 1) * TILE, :] for j in range(n_tiles)], axis=1)


def dsa_sample_select(cache_ik_t, layer, page_table, qi_rows, wmat, ki_new_t, n_tok, k_sel):
    DB, n_pages = page_table.shape
    rows_pad = wmat.shape[1]
    per = rows_pad // n_tok
    n_rows = DB // per * rows_pad
    L = (n_pages + 1) * PAGE_SIZE
    page_specs = [pl.BlockSpec((None, None, IDX_DIM, PAGE_SIZE), functools.partial(
        lambda b, pt, e, p: (layer, pt[b * per + e, p], 0, 0), e=e, p=p))
        for e in range(per) for p in range(n_pages)]
    grid_spec = pltpu.PrefetchScalarGridSpec(
        num_scalar_prefetch=1,
        grid=(DB // per,),
        in_specs=page_specs + [
            pl.BlockSpec((per,) + qi_rows.shape[1:], lambda b, pt: (b, 0, 0)),
            pl.BlockSpec((1,) + wmat.shape[1:], lambda b, pt: (b, 0, 0)),
            pl.BlockSpec((per, IDX_DIM, PAGE_SIZE), lambda b, pt: (b, 0, 0))],
        out_specs=pl.BlockSpec((n_rows, A_KV_HEADS * L), lambda b, pt: (0, 0)),
        scratch_shapes=[pltpu.VMEM((n_rows, L), F32),
                        pltpu.VMEM((TILE, L), I32)],
    )
    return pl.pallas_call(
        functools.partial(_dsa_sample_select_kernel, n_pages=n_pages, n_tok=n_tok, k_sel=k_sel,
                          rows_pad=rows_pad),
        out_shape=jax.ShapeDtypeStruct((n_rows, A_KV_HEADS * L), F32),
        grid_spec=grid_spec,
        compiler_params=_cparams(("arbitrary",)),
        name="dsa_sample_select",
    )(page_table, *([cache_ik_t] * (per * n_pages)), qi_rows, wmat, ki_new_t)


def _dsa_sample_attend_kernel(pt_ref, *refs, n_pages, n_tok, rows_pad):
    del pt_ref
    per = rows_pad // n_tok
    k_refs = refs[:per * n_pages]
    v_refs = refs[per * n_pages:2 * per * n_pages]
    q_ref, kn_ref, vn_ref, mask_ref, bias_ref, qg_ref, o_ref = refs[2 * per * n_pages:]
    rows = n_tok * A_HEADS
    page_rows = PAGE_SIZE * A_KV_HEADS
    n_cols = mask_ref.shape[1]
    pad = jnp.zeros((page_rows - n_tok * A_KV_HEADS, HEAD_DIM), BF16)
    rep = A_HEADS // A_KV_HEADS
    grp = (lax.broadcasted_iota(I32, (rows, 1), 0) % A_HEADS) // rep
    own_group = (lax.broadcasted_iota(I32, (rows, n_cols), 1) % A_KV_HEADS) == grp
    member = mask_ref[...]

    for e in range(per):
        def tiles(page_refs, new_ref):
            new = jnp.concatenate([new_ref[e].astype(BF16), pad], axis=0)
            return [r[...].astype(BF16) for r in page_refs[e * n_pages:(e + 1) * n_pages]] + [new]

        q = (_rms(q_ref[e], qg_ref[...]) * HEAD_DIM ** -0.5).astype(BF16)
        sel = jnp.concatenate(
            [jnp.broadcast_to(member[e * n_tok + t:e * n_tok + t + 1, :], (A_HEADS, n_cols)) for t in range(n_tok)],
            axis=0)
        valid = (sel > 0.5) & own_group
        logits = jnp.concatenate(
            [lax.dot_general(q, kt, (((1,), (1,)), ((), ())), preferred_element_type=F32)
             for kt in tiles(k_refs, kn_ref)], axis=1)
        logits = jnp.where(valid, logits + bias_ref[...], NEG_BIG)
        m = jnp.max(logits, axis=-1, keepdims=True)
        p = jnp.exp(logits - m)
        den = jnp.sum(p, axis=-1, keepdims=True)
        pb = p.astype(BF16)
        o = jnp.zeros((rows, HEAD_DIM), F32)
        for j, vt in enumerate(tiles(v_refs, vn_ref)):
            o = o + jnp.dot(pb[:, j * page_rows:(j + 1) * page_rows], vt, preferred_element_type=F32)
        o_ref[e] = (o / den).astype(o_ref.dtype)


def dsa_sample_attend(cache_k, cache_v, layer, page_table, q_rows, k_new, v_new, mask, bias_tab, q_norm_g):
    DB, n_pages = page_table.shape
    n_tok = k_new.shape[1] // A_KV_HEADS
    rows = n_tok * A_HEADS
    rows_pad = SUBLANES
    per = rows_pad // n_tok
    n_cols = mask.shape[1]
    page_rows = PAGE_SIZE * A_KV_HEADS
    page_specs = [pl.BlockSpec((None, None, page_rows, HEAD_DIM), functools.partial(
        lambda b, pt, e, p: (layer, pt[b * per + e, p], 0, 0), e=e, p=p))
        for e in range(per) for p in range(n_pages)]
    grid_spec = pltpu.PrefetchScalarGridSpec(
        num_scalar_prefetch=1,
        grid=(DB // per,),
        in_specs=page_specs + page_specs + [
            pl.BlockSpec((per, rows, HEAD_DIM), lambda b, pt: (b, 0, 0)),
            pl.BlockSpec((per, n_tok * A_KV_HEADS, HEAD_DIM), lambda b, pt: (b, 0, 0)),
            pl.BlockSpec((per, n_tok * A_KV_HEADS, HEAD_DIM), lambda b, pt: (b, 0, 0)),
            pl.BlockSpec((rows_pad, n_cols), lambda b, pt: (b, 0)),
            pl.BlockSpec((rows, n_cols), lambda b, pt: (0, 0), pipeline_mode=pl.Buffered(1)),
            pl.BlockSpec((1, HEAD_DIM), lambda b, pt: (0, 0), pipeline_mode=pl.Buffered(1))],
        out_specs=pl.BlockSpec((per, rows, HEAD_DIM), lambda b, pt: (b, 0, 0)),
    )
    return pl.pallas_call(
        functools.partial(_dsa_sample_attend_kernel, n_pages=n_pages, n_tok=n_tok, rows_pad=rows_pad),
        out_shape=jax.ShapeDtypeStruct((DB, rows, HEAD_DIM), BF16),
        grid_spec=grid_spec,
        compiler_params=_cparams(("parallel",)),
        name="dsa_sample_attend",
    )(page_table, *([cache_k] * (per * n_pages)), *([cache_v] * (per * n_pages)), q_rows, k_new, v_new, mask,
      bias_tab, q_norm_g.reshape(1, HEAD_DIM))


def _log_sigmoid(z):
    return jnp.minimum(z, 0.0) - jnp.log(1.0 + jnp.exp(-jnp.abs(z)))


def _seg_masks(seg):
    r = lax.broadcasted_iota(I32, (TILE, TILE), 0)
    c = lax.broadcasted_iota(I32, (TILE, TILE), 1)
    return r, c, (r // seg) == (c // seg)


def _gla_levels(seg):
    w, out = seg // 2, []
    while w >= 1:
        out.append(w)
        w //= 2
    return out


def _gla_sum_matrices(seg):
    r = jnp.arange(TILE)[:, None]
    c = jnp.arange(TILE)[None, :]
    mats = []
    for w in _gla_levels(seg):
        same = (r // (2 * w)) == (c // (2 * w))
        r_right = (r % (2 * w)) >= w
        c_right = (c % (2 * w)) >= w
        mats.append(same & r_right & c_right & (c <= r))
    for w in _gla_levels(seg):
        same = (r // (2 * w)) == (c // (2 * w))
        r_right = (r % (2 * w)) >= w
        c_right = (c % (2 * w)) >= w
        mats.append(same & (~r_right) & (~c_right) & (c > r))
    same_seg = (r // seg) == (c // seg)
    mats.append(same_seg & (c <= r))
    mats.append(same_seg & (c > r))
    return jnp.concatenate(mats, axis=0).astype(BF16)


def _bdot(a, b):
    return jnp.dot(a.astype(BF16), b.astype(BF16), preferred_element_type=F32)


def _bdot_nt(a, b):
    return lax.dot_general(a.astype(BF16), b.astype(BF16), (((1,), (1,)), ((), ())), preferred_element_type=F32)


def _gla_common(tiles, wg_ref, bg_ref, mats_ref, seg):
    n = len(tiles)
    kw = B_HEADS * B_DK
    las = []
    for _, _, misc_ref in tiles:
        gb = misc_ref[:, MISC_GB:MISC_GB + GATE_RANK]
        z = jnp.dot(gb, wg_ref[...], precision=HIGHEST, preferred_element_type=F32) + bg_ref[...]
        las.append(_log_sigmoid(z) / GATE_TEMP)
    la = jnp.concatenate(las, axis=1)
    la_hi = la.astype(BF16)
    la_lo = (la - la_hi.astype(F32)).astype(BF16)
    mats = mats_ref[...]
    sums = (jnp.dot(mats, la_hi, preferred_element_type=F32) + jnp.dot(mats, la_lo, preferred_element_type=F32))
    levels = _gla_levels(seg)
    nl = len(levels)
    qs = [qb_ref[...] * B_DK ** -0.5 for qb_ref, _, _ in tiles]
    ks = [kb_ref[...] for _, kb_ref, _ in tiles]
    r, c, _ = _seg_masks(seg)
    atts = [[jnp.where(r == c, _bdot_nt(qs[i][:, h * B_DK:(h + 1) * B_DK], ks[i][:, h * B_DK:(h + 1) * B_DK]), 0.0)
             for h in range(B_HEADS)] for i in range(n)]
    for li, w in enumerate(levels):
        pair = ((r // (2 * w)) == (c // (2 * w))) & ((r % (2 * w)) >= w) & ((c % (2 * w)) < w)
        qd = [(qs[i] * jnp.exp(sums[li * TILE:(li + 1) * TILE, i * kw:(i + 1) * kw])).astype(BF16) for i in range(n)]
        kd = [(ks[i] * jnp.exp(sums[(nl + li) * TILE:(nl + li + 1) * TILE, i * kw:(i + 1) * kw])).astype(BF16)
              for i in range(n)]
        for h in range(B_HEADS):
            hs = slice(h * B_DK, (h + 1) * B_DK)
            for i in range(n):
                atts[i][h] = atts[i][h] + jnp.where(pair, _bdot_nt(qd[i][:, hs], kd[i][:, hs]), 0.0)
    out = []
    for i in range(n):
        cs = slice(i * kw, (i + 1) * kw)
        b_cum = sums[2 * nl * TILE:(2 * nl + 1) * TILE, cs]
        rem = sums[(2 * nl + 1) * TILE:(2 * nl + 2) * TILE, cs]
        out.append((qs[i], ks[i], atts[i], b_cum, rem))
    return out


def _gla_finish(o_heads, rb_ref, go_ref, o_ref):
    go = go_ref[...]
    for h in range(B_HEADS):
        vs = slice(h * B_DV, (h + 1) * B_DV)
        rb = rb_ref[:, vs]
        o_ref[:, vs] = (_rms(o_heads[h], go) * (rb * jax.nn.sigmoid(rb))).astype(o_ref.dtype)


def _gla_prompt_kernel(qb_ref, kb_ref, vb_ref, rb_ref, misc_ref, wg_ref, bg_ref, go_ref, mats_ref,
                       o_ref, s_ref, state_ref):
    ci = pl.program_id(0)

    @pl.when(ci == 0)
    def _():
        state_ref[...] = jnp.zeros_like(state_ref)

    nb = qb_ref.shape[0]
    common = _gla_common([(qb_ref.at[b], kb_ref.at[b], misc_ref.at[b]) for b in range(nb)],
                         wg_ref, bg_ref, mats_ref, TILE)
    vals = [vb_ref[b] for b in range(nb)]
    states = [state_ref[b] for b in range(nb)]
    qes = [common[b][0] * jnp.exp(common[b][3]) for b in range(nb)]
    o_heads = [[] for _ in range(nb)]
    for h in range(B_HEADS):
        ks = slice(h * B_DK, (h + 1) * B_DK)
        vs = slice(h * B_DV, (h + 1) * B_DV)
        for b in range(nb):
            o_heads[b].append(_bdot(qes[b][:, ks], states[b][ks, :]) + _bdot(common[b][2][h], vals[b][:, vs]))
    for b in range(nb):
        _gla_finish(o_heads[b], rb_ref.at[b], go_ref, o_ref.at[b])

    ke_ts = [(common[b][1] * jnp.exp(common[b][4])).T for b in range(nb)]
    e_cols = [jnp.broadcast_to(jnp.exp(common[b][3][TILE - 1:TILE, :]), (TILE, B_HEADS * B_DK)).T[:, 0:1]
              for b in range(nb)]
    for b in range(nb):
        upd = jnp.concatenate(
            [_bdot(ke_ts[b][h * B_DK:(h + 1) * B_DK, :], vals[b][:, h * B_DV:(h + 1) * B_DV])
             for h in range(B_HEADS)], axis=0)
        new_state = states[b] * e_cols[b] + upd
        state_ref[b] = new_state
        s_ref[b] = new_state


def gla_prompt(proj, w_gate, b_gate, g_out, n_batch, seq):
    nc = seq // TILE
    kwid = B_HEADS * B_DK
    vwid = B_HEADS * B_DV
    mats = _gla_sum_matrices(TILE)
    proj3 = proj.reshape(n_batch, seq, proj.shape[1])
    o, s = pl.pallas_call(
        _gla_prompt_kernel,
        out_shape=(jax.ShapeDtypeStruct((n_batch, seq, vwid), BF16),
                   jax.ShapeDtypeStruct((n_batch, kwid, B_DV), F32)),
        grid=(nc,),
        in_specs=[pl.BlockSpec((n_batch, TILE, kwid), lambda c: (0, c, QB_OFF // kwid)),
                  pl.BlockSpec((n_batch, TILE, kwid), lambda c: (0, c, KB_OFF // kwid)),
                  pl.BlockSpec((n_batch, TILE, vwid), lambda c: (0, c, VB_OFF // vwid)),
                  pl.BlockSpec((n_batch, TILE, vwid), lambda c: (0, c, RB_OFF // vwid)),
                  pl.BlockSpec((n_batch, TILE, LANES), lambda c: (0, c, MISC_OFF // LANES)),
                  _resident((GATE_RANK, kwid)),
                  _resident((1, kwid)),
                  _resident((1, B_DV)),
                  _resident(mats.shape)],
        out_specs=(pl.BlockSpec((n_batch, TILE, vwid), lambda c: (0, c, 0)),
                   pl.BlockSpec((n_batch, kwid, B_DV), lambda c: (0, 0, 0))),
        scratch_shapes=[pltpu.VMEM((n_batch, kwid, B_DV), F32)],
        compiler_params=_cparams(("arbitrary",)),
        name="gla_prompt",
    )(proj3, proj3, proj3, proj3, proj3, w_gate, b_gate.reshape(1, kwid), g_out.reshape(1, B_DV), mats)
    return o.reshape(n_batch * seq, vwid), s.reshape(n_batch, B_HEADS, B_DK, B_DV)


def _gla_sample_kernel(qb_ref, kb_ref, vb_ref, rb_ref, misc_ref, wg_ref, bg_ref, go_ref, mats_ref, s0_ref,
                       o_ref, s_ref, *, seg):
    nbt = TILE // seg
    (q, k, att, b_cum, rem), = _gla_common([(qb_ref, kb_ref, misc_ref)], wg_ref, bg_ref, mats_ref, seg)
    v = vb_ref[...]
    qe = q * jnp.exp(b_cum)
    ke = k * jnp.exp(rem)
    r1 = lax.broadcasted_iota(I32, (TILE, 1), 0)
    e_last = jnp.where(r1 % seg == seg - 1, jnp.exp(b_cum), 0.0)
    wide = nbt * B_DK
    mq = (lax.broadcasted_iota(I32, (TILE, wide), 0) // seg) == (lax.broadcasted_iota(I32, (TILE, wide), 1) // B_DK)
    mk = (lax.broadcasted_iota(I32, (wide, TILE), 0) // B_DK) == (lax.broadcasted_iota(I32, (wide, TILE), 1) // seg)
    o_heads = []
    for h in range(B_HEADS):
        ks = slice(h * B_DK, (h + 1) * B_DK)
        vs = slice(h * B_DV, (h + 1) * B_DV)
        state = s0_ref[:, h].reshape(wide, B_DV)
        q_bd = jnp.where(mq, jnp.concatenate([qe[:, ks]] * nbt, axis=1), 0.0)
        o_heads.append(_bdot(q_bd, state) + _bdot(att[h], v[:, vs]))
        pair_t = jnp.concatenate([ke[:, ks], e_last[:, ks]], axis=1).T
        k_bd = jnp.where(mk, jnp.concatenate([pair_t[:B_DK]] * nbt, axis=0), 0.0)
        e_bd = jnp.where(mk, jnp.concatenate([pair_t[B_DK:]] * nbt, axis=0), 0.0)
        e_col = jnp.sum(e_bd, axis=-1, keepdims=True)
        new_state = state * e_col + _bdot(k_bd, v[:, vs])
        s_ref[:, h] = new_state.reshape(nbt, B_DK, B_DV)
    _gla_finish(o_heads, rb_ref, go_ref, o_ref)


def gla_sample(proj, w_gate, b_gate, g_out, s0, layer, n_tok):
    T = proj.shape[0]
    nbt = TILE // n_tok
    kwid = B_HEADS * B_DK
    vwid = B_HEADS * B_DV
    mats = _gla_sum_matrices(n_tok)
    return pl.pallas_call(
        functools.partial(_gla_sample_kernel, seg=n_tok),
        out_shape=(jax.ShapeDtypeStruct((T, vwid), BF16),
                   jax.ShapeDtypeStruct(s0.shape[1:], F32)),
        grid=(T // TILE,),
        in_specs=[pl.BlockSpec((TILE, kwid), lambda i: (i, QB_OFF // kwid)),
                  pl.BlockSpec((TILE, kwid), lambda i: (i, KB_OFF // kwid)),
                  pl.BlockSpec((TILE, vwid), lambda i: (i, VB_OFF // vwid)),
                  pl.BlockSpec((TILE, vwid), lambda i: (i, RB_OFF // vwid)),
                  pl.BlockSpec((TILE, LANES), lambda i: (i, MISC_OFF // LANES)),
                  _resident((GATE_RANK, kwid)),
                  _resident((1, kwid)),
                  _resident((1, B_DV)),
                  _resident(mats.shape),
                  pl.BlockSpec((None, nbt, B_HEADS, B_DK, B_DV), lambda i: (layer, i, 0, 0, 0))],
        out_specs=(pl.BlockSpec((TILE, vwid), lambda i: (i, 0)),
                   pl.BlockSpec((nbt, B_HEADS, B_DK, B_DV), lambda i: (i, 0, 0, 0))),
        compiler_params=_cparams(("parallel",)),
        name="gla_sample",
    )(proj, proj, proj, proj, proj, w_gate, b_gate.reshape(1, kwid), g_out.reshape(1, B_DV), mats, s0)


def _gelu(x):
    return jax.nn.gelu(x)


def _gmlp_rows(uc_ref, vc_ref, gv_ref, ws_ref, bcol_ref, o_ref, vn_ref, seg):
    r, c, same_seg = _seg_masks(seg)
    keep = same_seg & (c <= r)
    for t in range(uc_ref.shape[0] // TILE):
        rows = slice(t * TILE, (t + 1) * TILE)
        u = _gelu(uc_ref[rows, :])
        vg = _gelu(vc_ref[rows, :])
        for g in range(C_GROUPS):
            gs = slice(g * C_GROUP_DIM, (g + 1) * C_GROUP_DIM)
            vn = _rms(vg[:, gs], gv_ref[:, gs])
            if vn_ref is not None:
                vn_ref[rows, gs] = vn
            w = jnp.where(keep, ws_ref[g], 0.0).astype(BF16)
            s = jnp.dot(w, vn.astype(BF16), preferred_element_type=F32) + bcol_ref[:, g:g + 1]
            o_ref[rows, gs] = (u[:, gs] * s).astype(o_ref.dtype)


def _ffn_kernel(h_ref, oa_ref, ob_ref, oc_ref, wo_ref, g_ref, wg_ref, wu_ref, wd_ref, o_ref, n_ref):
    j = pl.program_id(1)

    @pl.when(j == 0)
    def _():
        aw = oa_ref.shape[1]
        bw = ob_ref.shape[1]
        h = h_ref[...] + jnp.dot(oa_ref[...], wo_ref[0:aw, :], preferred_element_type=F32)
        h = h + jnp.dot(ob_ref[...], wo_ref[aw:aw + bw, :], preferred_element_type=F32)
        h = h + jnp.dot(oc_ref[...], wo_ref[aw + bw:, :], preferred_element_type=F32)
        n_ref[...] = _rms(h, g_ref[...]).astype(BF16)
        o_ref[...] = h

    n = n_ref[...]
    a = jnp.dot(n, wg_ref[...], preferred_element_type=F32)
    u = jnp.dot(n, wu_ref[...], preferred_element_type=F32)
    act = (a * jax.nn.sigmoid(a) * u).astype(BF16)
    o_ref[...] += jnp.dot(act, wd_ref[...], preferred_element_type=F32)


def out_proj_ffn(h, o_a, o_b, o_c, w_out, g, w_gate, w_up, w_down, tm, tf):
    T, D = h.shape
    tm = min(tm, T)
    FF = w_gate.shape[1]
    return pl.pallas_call(
        _ffn_kernel,
        out_shape=jax.ShapeDtypeStruct((T, D), F32),
        grid=(T // tm, FF // tf),
        in_specs=[pl.BlockSpec((tm, D), lambda i, j: (i, 0)),
                  pl.BlockSpec((tm, o_a.shape[1]), lambda i, j: (i, 0)),
                  pl.BlockSpec((tm, o_b.shape[1]), lambda i, j: (i, 0)),
                  pl.BlockSpec((tm, o_c.shape[1]), lambda i, j: (i, 0)),
                  _resident(w_out.shape),
                  _resident((1, D)),
                  pl.BlockSpec((D, tf), lambda i, j: (0, j)),
                  pl.BlockSpec((D, tf), lambda i, j: (0, j)),
                  pl.BlockSpec((tf, D), lambda i, j: (j, 0))],
        out_specs=pl.BlockSpec((tm, D), lambda i, j: (i, 0)),
        scratch_shapes=[pltpu.VMEM((tm, D), BF16)],
        compiler_params=_cparams(("parallel", "arbitrary")),
        name="out_proj_ffn",
    )(h, o_a, o_b, o_c, w_out, g.reshape(1, D), w_gate, w_up, w_down)


def _ple_kernel(h_ref, p_ref, g_ref, wgate_ref, wproj_ref, o_ref):
    h = h_ref[...]
    n = _rms(h, g_ref[...]).astype(BF16)
    gate = jax.nn.sigmoid(jnp.dot(n, wgate_ref[...], preferred_element_type=F32))
    emb = jnp.dot(p_ref[...].astype(BF16), wproj_ref[...], preferred_element_type=F32)
    o_ref[...] = h + gate * emb


def ple(h, p, g, w_gate, w_proj, layer, tm):
    T, D = h.shape
    tm = min(tm, T)
    P = p.shape[2]
    return pl.pallas_call(
        _ple_kernel,
        out_shape=jax.ShapeDtypeStruct((T, D), F32),
        grid=(T // tm,),
        in_specs=[pl.BlockSpec((tm, D), lambda i: (i, 0)),
                  pl.BlockSpec((None, tm, P), lambda i: (layer, i, 0)),
                  _resident((1, D)),
                  _resident(w_gate.shape),
                  _resident(w_proj.shape)],
        out_specs=pl.BlockSpec((tm, D), lambda i: (i, 0)),
        compiler_params=_cparams(("parallel",)),
        name="ple",
    )(h, p, g.reshape(1, D), w_gate, w_proj)


_W_IN_SEGMENTS = (("q", 1024), ("k", 256), ("v", 256), ("qi", 1024), ("ki", 64), ("wi", 16), ("qb", 256),
                  ("kb", 256), ("vb", 512), ("gb", 16), ("rb", 512), ("uc", 512), ("vc", 512))
_W_IN_PACKED_ORDER = ("q", "qi", "vb", "rb", "uc", "vc", "k", "v", "qb", "kb", "ki", "wi", "gb")


def _pack_kernel(wt_ref, o_ref):
    src, start = {}, 0
    for name, size in _W_IN_SEGMENTS:
        src[name] = (start, size)
        start += size
    dst = 0
    small = []
    for name in _W_IN_PACKED_ORDER:
        s0, size = src[name]
        if size < LANES:
            small.append(wt_ref[s0:s0 + size, :])
            continue
        o_ref[:, dst:dst + size] = wt_ref[s0:s0 + size, :].T.astype(BF16)
        dst += size
    used = sum(x.shape[0] for x in small)
    small.append(jnp.zeros((LANES - used, wt_ref.shape[1]), F32))
    o_ref[:, dst:dst + LANES] = jnp.concatenate(small, axis=0).T.astype(BF16)


def _pack_w_in(w, tr=256):
    depth, D, N = w.shape
    return pl.pallas_call(
        _pack_kernel,
        out_shape=jax.ShapeDtypeStruct((depth, D, PROJ_PACKED), BF16),
        grid=(depth, D // tr),
        in_specs=[pl.BlockSpec((None, N, tr), lambda l, i: (l, 0, i))],
        out_specs=pl.BlockSpec((None, tr, PROJ_PACKED), lambda l, i: (l, i, 0)),
        compiler_params=_cparams(("parallel", "parallel")),
        name="pack_w_in",
    )(jnp.swapaxes(w, 1, 2))


def _mixer_tail(h, o_a, o_b, o_c, p_all, lw, layer):
    h = out_proj_ffn(h, o_a, o_b, o_c, lw["w_out"], lw["g_ffn"], lw["w_ffn_gate"], lw["w_ffn_up"],
                     lw["w_ffn_down"], TM_FFN, TF_FFN)
    return ple(h, p_all, lw["g_ple"], lw["w_ple_gate"], lw["w_ple_proj"], layer, TM_PLE)


def kernel(x_prompt, x_sample, cache_k, cache_v, cache_idx_k, state_gla, page_table, p_prompt, p_sample,
           g_mix, w_in, q_norm_g, k_norm_g, rel_bias, w_gate_b, b_gate_b, g_out_b, g_v_c, w_spatial,
           b_spatial, w_out, g_ffn, w_ffn_gate, w_ffn_up, w_ffn_down, g_ple, w_ple_gate, w_ple_proj):
    n_batch, seq, d_model = x_prompt.shape
    dec_batch, dec_seq, _ = x_sample.shape
    depth = w_in.shape[0]
    n_pages = page_table.shape[1]
    past = n_pages * PAGE_SIZE
    kw = A_KV_HEADS * HEAD_DIM
    tp, ts = n_batch * seq, dec_batch * dec_seq
    rows_pad = SUBLANES
    l_sample = past + PAGE_SIZE
    k_sel_s = min(TOPK_MAX, (past + dec_seq) // 4)

    bias_p = bias_table_prompt(rel_bias)
    bias_s = bias_table_sample(rel_bias, past, dec_seq, A_KV_HEADS * l_sample)
    cache_ik_t = jnp.swapaxes(cache_idx_k, 2, 3)
    cache_k2 = cache_k.reshape(depth, cache_k.shape[1], PAGE_SIZE * A_KV_HEADS, HEAD_DIM)
    cache_v2 = cache_v.reshape(depth, cache_v.shape[1], PAGE_SIZE * A_KV_HEADS, HEAD_DIM)

    hp = x_prompt.reshape(tp, d_model)
    hs = x_sample.reshape(ts, d_model)
    outs = {k: [] for k in ("kp", "vp", "ikp", "sp", "ks", "vs", "iks", "ss", "cs")}
    per_s = rows_pad // dec_seq
    place_t = (jnp.arange(rows_pad)[:, None, None]
               == jnp.arange(per_s)[None, :, None] * dec_seq + jnp.arange(dec_seq)[None, None, :]
               ).astype(F32)
    w_ple_proj_b = w_ple_proj.astype(BF16)
    w_packed = _pack_w_in(w_in)
    pp_all = p_prompt.reshape(depth, tp, -1)
    ps_all = p_sample.reshape(depth, ts, -1)
    b_cols_p = jnp.swapaxes(b_spatial, 1, 2)
    reps = TILE // dec_seq
    w_tiles_s = jnp.tile(w_spatial[:, :, :dec_seq, :dec_seq], (1, 1, reps, reps))
    b_cols_s = jnp.tile(jnp.swapaxes(b_spatial[:, :, :dec_seq], 1, 2), (1, reps, 1))
    for i in range(depth):
        proj, (kn, vv, ik, o_c, vt, k3, v3, qn, qst), (wo_b, wg_b, wu_b, wd_b, wpg_b) = in_projection(
            hp, g_mix[i], w_packed, k_norm_g[i], q_norm_g[i], (g_v_c[i], w_spatial, b_cols_p, TILE), i, TM_PROJ,
            True, False, cast=(w_out, w_ffn_gate, w_ffn_up, w_ffn_down, w_ple_gate))
        lw = dict(w_out=wo_b, g_ffn=g_ffn[i], w_ffn_gate=wg_b, w_ffn_up=wu_b, w_ffn_down=wd_b,
                  g_ple=g_ple[i], w_ple_gate=wpg_b, w_ple_proj=w_ple_proj_b[i])
        o_a = dsa_prompt(proj, qn, qst, kn, vt, bias_p, n_batch, seq)
        o_b, s_p = gla_prompt(proj, w_gate_b[i], b_gate_b[i], g_out_b[i], n_batch, seq)
        hp = _mixer_tail(hp, o_a, o_b, o_c, pp_all, lw, i)
        outs["kp"].append(k3.reshape(n_batch, seq, A_KV_HEADS, HEAD_DIM))
        outs["vp"].append(v3.reshape(n_batch, seq, A_KV_HEADS, HEAD_DIM))
        outs["ikp"].append(ik.reshape(n_batch, seq, IDX_DIM))
        outs["sp"].append(s_p)

        proj, (kn, vv, ik, o_c, vn), _ = in_projection(
            hs, g_mix[i], w_packed, k_norm_g[i], q_norm_g[i], (g_v_c[i], w_tiles_s, b_cols_s, dec_seq), i, TM_PROJ,
            False, True)
        qi_rows = proj[:, QI_OFF:QI_OFF + IDX_HEADS * IDX_DIM].reshape(dec_batch, dec_seq * IDX_HEADS, IDX_DIM)
        wi = proj[:, MISC_OFF + MISC_WI:MISC_OFF + MISC_WI + IDX_HEADS].reshape(dec_batch, dec_seq, IDX_HEADS)
        wi = wi * (IDX_HEADS ** -0.5 * IDX_DIM ** -0.5)
        wmat = (place_t[None, :, :, :, None] * wi.reshape(dec_batch // per_s, 1, per_s, dec_seq, IDX_HEADS)
                ).reshape(dec_batch // per_s, rows_pad, per_s * dec_seq * IDX_HEADS)
        ki_new_t = jnp.pad(jnp.swapaxes(ik.reshape(dec_batch, dec_seq, IDX_DIM), 1, 2),
                           ((0, 0), (0, 0), (0, PAGE_SIZE - dec_seq)))
        mask = dsa_sample_select(cache_ik_t, i, page_table, qi_rows, wmat, ki_new_t, dec_seq, k_sel_s)
        q_rows = proj[:, Q_OFF:Q_OFF + A_HEADS * HEAD_DIM].reshape(dec_batch, dec_seq * A_HEADS, HEAD_DIM)
        o_a = dsa_sample_attend(cache_k2, cache_v2, i, page_table, q_rows,
                                kn.reshape(dec_batch, dec_seq * A_KV_HEADS, HEAD_DIM),
                                vv.reshape(dec_batch, dec_seq * A_KV_HEADS, HEAD_DIM), mask, bias_s, q_norm_g[i])
        o_a = o_a.reshape(ts, A_HEADS * HEAD_DIM)
        o_b, s_s = gla_sample(proj, w_gate_b[i], b_gate_b[i], g_out_b[i], state_gla, i, dec_seq)
        hs = _mixer_tail(hs, o_a, o_b, o_c, ps_all, lw, i)
        outs["ks"].append(kn.reshape(dec_batch, dec_seq, A_KV_HEADS, HEAD_DIM))
        outs["vs"].append(vv.reshape(dec_batch, dec_seq, A_KV_HEADS, HEAD_DIM))
        outs["iks"].append(ik.reshape(dec_batch, dec_seq, IDX_DIM))
        outs["ss"].append(s_s)
        outs["cs"].append(vn.reshape(dec_batch, dec_seq, -1))

    st = {k: jnp.stack(v) for k, v in outs.items()}
    return (hp.reshape(n_batch, seq, d_model), hs.reshape(dec_batch, dec_seq, d_model),
            st["kp"], st["vp"], st["ikp"], st["sp"], st["ks"], st["vs"], st["iks"], st["ss"], st["cs"])
```

```python
import functools
import math

import jax
import jax.numpy as jnp
from jax import lax
from jax.experimental import pallas as pl
from jax.experimental.pallas import tpu as pltpu

F32 = jnp.float32
BF16 = jnp.bfloat16
I32 = jnp.int32
HIGHEST = lax.Precision.HIGHEST

LANES = 128
SUBLANES = 8
VMEM_LIMIT = 60 * 1024 * 1024

HEAD_DIM = 128
A_HEADS = 8
A_KV_HEADS = 2
IDX_HEADS = 16
IDX_DIM = 64
TOPK_MAX = 256
NUM_BUCKETS = 32
MAX_DISTANCE = 128
B_HEADS = 4
B_DK = 64
B_DV = 128
GATE_RANK = 16
GATE_TEMP = 16.0
C_GROUPS = 4
C_GROUP_DIM = 128
PAGE_SIZE = 128
EPS = 1e-6
NEG_BIG = -1e30
INT_MIN = -(2 ** 31)
NEG_INF_KEY = -2139095041

TILE = 128
QBLK = 256
TM_PROJ = 256
TM_FFN = 512
TF_FFN = 512
TM_PLE = 512

Q_OFF, QI_OFF, VB_OFF, RB_OFF, UC_OFF, VC_OFF = 0, 1024, 2048, 2560, 3072, 3584
K_OFF, V_OFF, QB_OFF, KB_OFF, MISC_OFF = 4096, 4352, 4608, 4864, 5120
PROJ_PACKED = 5248
MISC_KI, MISC_WI, MISC_GB = 0, 64, 80


def _cparams(sem):
    return pltpu.CompilerParams(dimension_semantics=sem, vmem_limit_bytes=VMEM_LIMIT)


def _rms(x, g):
    return x * lax.rsqrt(jnp.mean(x * x, axis=-1, keepdims=True) + EPS) * g


def _resident(shape):
    nd = len(shape)
    return pl.BlockSpec(shape, lambda *_: (0,) * nd, pipeline_mode=pl.Buffered(1))


def _layer_resident(shape, layer):
    nd = len(shape)
    return pl.BlockSpec((None,) + tuple(shape), lambda *_: (layer,) + (0,) * nd, pipeline_mode=pl.Buffered(1))


def _proj_kernel(x_ref, g_ref, w_ref, kg_ref, qg_ref, gv_ref, ws_ref, bcol_ref, *refs, n_cast, with_vt, with_vn,
                 seg):
    cast_in, refs = list(refs[:n_cast]), list(refs[n_cast:])
    o_ref, ko_ref, vo_ref, io_ref, oc_ref = refs[:5]
    refs = refs[5:]
    vt_ref, k3_ref, v3_ref, qn_ref, qst_ref = [refs.pop(0) for _ in range(5)] if with_vt else [None] * 5
    vn_ref = refs.pop(0) if with_vn else None
    cast_out = refs
    n = _rms(x_ref[...], g_ref[...]).astype(BF16)
    ncol = o_ref.shape[1]
    step = 512
    for c0 in range(0, ncol, step):
        c1 = min(c0 + step, ncol)
        o_ref[:, c0:c1] = jnp.dot(n, w_ref[:, c0:c1], preferred_element_type=F32)
    kg = kg_ref[...]
    v = o_ref[:, V_OFF:V_OFF + A_KV_HEADS * HEAD_DIM]
    for hh in range(A_KV_HEADS):
        hs = slice(hh * HEAD_DIM, (hh + 1) * HEAD_DIM)
        kn = _rms(o_ref[:, K_OFF + hh * HEAD_DIM:K_OFF + (hh + 1) * HEAD_DIM], kg)
        ko_ref[:, hs] = kn
        if with_vt:
            k3_ref[:, hh, :] = kn
            v3_ref[:, hh, :] = v[:, hs]
    vo_ref[...] = v
    io_ref[...] = o_ref[:, MISC_OFF + MISC_KI:MISC_OFF + MISC_KI + IDX_DIM]
    if with_vt:
        for blk in range(vt_ref.shape[0]):
            vt_ref[blk] = v[blk * QBLK:(blk + 1) * QBLK, :].T.astype(vt_ref.dtype)
        tm = o_ref.shape[0]
        qg = qg_ref[...]
        for h in range(A_HEADS):
            q = o_ref[:, Q_OFF + h * HEAD_DIM:Q_OFF + (h + 1) * HEAD_DIM]
            qn_ref[h * tm:(h + 1) * tm, :] = (_rms(q, qg) * HEAD_DIM ** -0.5).astype(qn_ref.dtype)
        for h in range(IDX_HEADS):
            qst_ref[h * tm:(h + 1) * tm, :] = o_ref[:, QI_OFF + h * IDX_DIM:QI_OFF + (h + 1) * IDX_DIM].astype(
                qst_ref.dtype)
    cw = C_GROUPS * C_GROUP_DIM
    _gmlp_rows(o_ref.at[:, UC_OFF:UC_OFF + cw], o_ref.at[:, VC_OFF:VC_OFF + cw], gv_ref, ws_ref, bcol_ref,
               oc_ref, vn_ref, seg)
    for src, dst in zip(cast_in, cast_out):
        dst[...] = src[...].astype(dst.dtype)


def in_projection(h, g, w_packed, k_norm_g, q_norm_g, gmlp_params, layer, tm, with_vt, with_vn, cast=()):
    T, D = h.shape
    tm = min(tm, T)
    N = w_packed.shape[2]
    kw = A_KV_HEADS * HEAD_DIM
    cw = C_GROUPS * C_GROUP_DIM
    g_v, w_tiles, b_cols, seg = gmlp_params
    steps = T // tm
    cast_specs_in = [pl.BlockSpec((None, w.shape[1] // steps, w.shape[2]), lambda i: (layer, i, 0)) for w in cast]
    cast_specs_out = [pl.BlockSpec((w.shape[1] // steps, w.shape[2]), lambda i: (i, 0)) for w in cast]
    kv_shape = [jax.ShapeDtypeStruct((T, kw), F32),
                jax.ShapeDtypeStruct((T, kw), F32),
                jax.ShapeDtypeStruct((T, IDX_DIM), F32),
                jax.ShapeDtypeStruct((T, cw), BF16)]
    kv_specs = [pl.BlockSpec((tm, kw), lambda i: (i, 0)),
                pl.BlockSpec((tm, kw), lambda i: (i, 0)),
                pl.BlockSpec((tm, IDX_DIM), lambda i: (i, 0)),
                pl.BlockSpec((tm, cw), lambda i: (i, 0))]
    if with_vt:
        kv_shape.append(jax.ShapeDtypeStruct((T // QBLK, kw, QBLK), BF16))
        kv_specs.append(pl.BlockSpec((tm // QBLK, kw, QBLK), lambda i: (i, 0, 0)))
        for _ in range(2):
            kv_shape.append(jax.ShapeDtypeStruct((T, A_KV_HEADS, HEAD_DIM), F32))
            kv_specs.append(pl.BlockSpec((tm, A_KV_HEADS, HEAD_DIM), lambda i: (i, 0, 0)))
        assert tm == QBLK, "the prompt attention takes one query block per projection step"
        for rows, width in ((A_HEADS * tm, HEAD_DIM), (IDX_HEADS * tm, IDX_DIM)):
            kv_shape.append(jax.ShapeDtypeStruct((steps, rows, width), BF16))
            kv_specs.append(pl.BlockSpec((None, rows, width), lambda i: (i, 0, 0)))
    if with_vn:
        kv_shape.append(jax.ShapeDtypeStruct((T, cw), F32))
        kv_specs.append(pl.BlockSpec((tm, cw), lambda i: (i, 0)))
    outs = pl.pallas_call(
        functools.partial(_proj_kernel, n_cast=len(cast), with_vt=with_vt, with_vn=with_vn, seg=seg),
        out_shape=(jax.ShapeDtypeStruct((T, N), F32),) + tuple(kv_shape) + tuple(
            jax.ShapeDtypeStruct(w.shape[1:], BF16) for w in cast),
        grid=(steps,),
        in_specs=[pl.BlockSpec((tm, D), lambda i: (i, 0)),
                  _resident((1, D)),
                  _layer_resident((D, N), layer),
                  _resident((1, HEAD_DIM)),
                  _resident((1, HEAD_DIM)),
                  _resident((1, cw)),
                  _layer_resident((C_GROUPS, TILE, TILE), layer),
                  _layer_resident((TILE, C_GROUPS), layer)] + cast_specs_in,
        out_specs=(pl.BlockSpec((tm, N), lambda i: (i, 0)),) + tuple(kv_specs) + tuple(cast_specs_out),
        compiler_params=_cparams(("parallel",)),
        name="in_projection",
    )(h, g.reshape(1, D), w_packed, k_norm_g.reshape(1, HEAD_DIM), q_norm_g.reshape(1, HEAD_DIM),
      g_v.reshape(1, cw), w_tiles, b_cols, *cast)
    n_kv = len(kv_shape)
    return outs[0], outs[1:1 + n_kv], outs[1 + n_kv:]


def _bucket(dist):
    n = jnp.maximum(dist, 0)
    max_exact = NUM_BUCKETS // 2
    large = max_exact + (jnp.log(jnp.maximum(n, 1).astype(F32) / max_exact)
                         / math.log(MAX_DISTANCE / max_exact)
                         * (NUM_BUCKETS - max_exact)).astype(I32)
    large = jnp.minimum(large, NUM_BUCKETS - 1)
    return jnp.where(n < max_exact, n, large)


def _bias_prompt_kernel(rb_ref, o_ref):
    c = lax.broadcasted_iota(I32, (TILE, TILE), 0)
    t = lax.broadcasted_iota(I32, (TILE, TILE), 1)
    for z in range(3):
        bucket = _bucket(t - c + (2 - z) * TILE)
        for h in range(A_HEADS):
            acc = jnp.zeros((TILE, TILE), F32)
            for b in range(NUM_BUCKETS):
                acc = jnp.where(bucket == b, rb_ref[b, h], acc)
            o_ref[h, z] = acc


def bias_table_prompt(rel_bias):
    return pl.pallas_call(
        _bias_prompt_kernel,
        out_shape=jax.ShapeDtypeStruct((A_HEADS, 3, TILE, TILE), F32),
        in_specs=[pl.BlockSpec(memory_space=pltpu.SMEM)],
        out_specs=pl.BlockSpec(memory_space=pltpu.VMEM),
        name="bias_table_prompt",
    )(rel_bias)


def _bias_sample_kernel(rbrows_ref, o_ref, *, past, n_tok):
    rows, L = o_ref.shape
    r = lax.broadcasted_iota(I32, (rows, L), 0)
    s = lax.broadcasted_iota(I32, (rows, L), 1) // A_KV_HEADS
    bucket = _bucket(past + r // A_HEADS - s)
    rbrows = rbrows_ref[...]
    acc = jnp.zeros((rows, L), F32)
    for b in range(NUM_BUCKETS):
        acc = jnp.where(bucket == b, rbrows[:, b:b + 1], acc)
    o_ref[...] = acc


def bias_table_sample(rel_bias, past, n_tok, L):
    rows = n_tok * A_HEADS
    rbrows = jnp.tile(rel_bias.T, (n_tok, 1))
    return pl.pallas_call(
        functools.partial(_bias_sample_kernel, past=past, n_tok=n_tok),
        out_shape=jax.ShapeDtypeStruct((rows, L), F32),
        name="bias_table_sample",
    )(rbrows)


def _sortable_key(x):
    b = lax.bitcast_convert_type(x, I32)
    return b ^ ((b >> 31) & 0x7FFFFFFF)


def _topk_member(skey_ref, k_sel):
    R, L = skey_ref.shape

    def body(it, ans):
        bit = 31 - it
        cand = ans | lax.shift_left(jnp.int32(1), bit)
        cand_s = cand ^ INT_MIN
        cnt = jnp.sum(jnp.where(skey_ref[...] >= cand_s, 1.0, 0.0), axis=-1, keepdims=True)
        return jnp.where(cnt >= k_sel, cand, ans)

    ans = lax.fori_loop(0, 32, body, jnp.zeros((R, 1), I32))
    tau = ans ^ INT_MIN
    skey = skey_ref[...]
    gt = skey > tau
    eq = skey == tau
    n_gt = jnp.sum(jnp.where(gt, 1.0, 0.0), axis=-1, keepdims=True)
    room = k_sel - n_gt
    r_i = lax.broadcasted_iota(I32, (LANES, LANES), 0)
    c_i = lax.broadcasted_iota(I32, (LANES, LANES), 1)
    upper = jnp.where(r_i <= c_i, 1.0, 0.0).astype(BF16)
    off = jnp.zeros((R, 1), F32)
    parts = []
    for j in range(L // LANES):
        sl = slice(j * LANES, (j + 1) * LANES)
        eq_j = eq[:, sl]
        run = jnp.dot(jnp.where(eq_j, 1.0, 0.0).astype(BF16), upper, preferred_element_type=F32) + off
        parts.append(gt[:, sl] | (eq_j & (run <= room)))
        off = run[:, LANES - 1:LANES]
    return jnp.concatenate(parts, axis=1)


def _fold8(x, op):
    return op(x.reshape(x.shape[0] // SUBLANES, SUBLANES, x.shape[1]), axis=0)


def _dsa_prompt_kernel(qn_ref, qst_ref, misc_ref, kn_ref, vt_ref, bias_ref, o_ref,
                       skey_ref, madd_ref, lg_ref, acc_ref, *, k_sel):
    i = pl.program_id(1)
    nkb = i + 1
    sub = QBLK // TILE
    rep = A_HEADS // A_KV_HEADS
    row0 = pl.multiple_of(i * QBLK, QBLK)
    s_iota = lax.broadcasted_iota(I32, (QBLK, QBLK), 0)
    t_iota = lax.broadcasted_iota(I32, (QBLK, QBLK), 1)

    def admissible(j):
        return (j * QBLK + s_iota) <= (row0 + t_iota)

    wi_t = misc_ref[pl.ds(row0, QBLK), :].T[MISC_WI:MISC_WI + IDX_HEADS, :]
    wi_t = wi_t * (IDX_HEADS ** -0.5 * IDX_DIM ** -0.5)

    def score_body(j, carry):
        k0 = pl.multiple_of(j * QBLK, QBLK)
        kj = misc_ref[pl.ds(k0, QBLK), MISC_KI:MISC_KI + IDX_DIM].astype(BF16)
        s = lax.dot_general(kj, qst_ref[...], (((1,), (1,)), ((), ())), preferred_element_type=F32)
        score = jnp.zeros((QBLK, QBLK), F32)
        for h in range(IDX_HEADS):
            score = score + jnp.maximum(s[:, h * QBLK:(h + 1) * QBLK], 0.0) * wi_t[h:h + 1, :]
        skey_ref[j] = _sortable_key(jnp.where(admissible(j), score, -jnp.inf))
        return carry

    lax.fori_loop(0, nkb // 2, lambda jj, c: score_body(2 * jj + 1, score_body(2 * jj, c)), 0)

    @pl.when(nkb % 2 == 1)
    def _():
        score_body(nkb - 1, 0)

    def count(pred_fn):
        def hits(j):
            return _fold8(jnp.where(pred_fn(skey_ref[j]), 1.0, 0.0), jnp.sum)

        def body(jj, accs):
            return accs[0] + hits(2 * jj), accs[1] + hits(2 * jj + 1)

        zero = jnp.zeros((SUBLANES, QBLK), F32)
        acc0, acc1 = lax.fori_loop(0, nkb // 2, body, (zero, zero))
        acc = lax.cond(nkb % 2 == 1, lambda: acc0 + acc1 + hits(nkb - 1), lambda: acc0 + acc1)
        return jnp.sum(acc, axis=0, keepdims=True)

    def bit_body(it, ans):
        cand = ans | lax.shift_left(jnp.int32(1), 31 - it)
        cand_s = cand ^ INT_MIN
        cnt = count(lambda key: key >= cand_s)
        return jnp.where(cnt >= k_sel, cand, ans)

    ans = lax.fori_loop(0, 32, bit_body, jnp.zeros((1, QBLK), I32))
    tau = ans ^ INT_MIN
    n_ge = count(lambda key: key >= tau)
    excess = jnp.max(jnp.where((n_ge > k_sel) & (tau != NEG_INF_KEY), 1.0, 0.0))

    @pl.when(excess == 0.0)
    def _():
        def mask_body(j, carry):
            madd_ref[j] = jnp.where((skey_ref[j] >= tau) & admissible(j), 0.0, NEG_BIG)
            return carry

        lax.fori_loop(0, nkb, mask_body, 0)

    @pl.when(excess > 0.0)
    def _():
        room = k_sel - count(lambda key: key > tau)
        lower = jnp.where(t_iota <= s_iota, 1.0, 0.0).astype(BF16)

        def mask_body(j, off):
            key = skey_ref[j]
            eq = key == tau
            run = jnp.dot(lower, jnp.where(eq, 1.0, 0.0).astype(BF16), preferred_element_type=F32) + off
            sel = ((key > tau) | (eq & (run <= room))) & admissible(j)
            madd_ref[j] = jnp.where(sel, 0.0, NEG_BIG)
            return run[QBLK - 1:QBLK, :]

        lax.fori_loop(0, nkb, mask_body, jnp.zeros((1, QBLK), F32))

    wide = rep * QBLK
    for g in range(A_KV_HEADS):
        gs = slice(g * HEAD_DIM, (g + 1) * HEAD_DIM)
        heads = list(range(g * rep, (g + 1) * rep))
        q_stack = qn_ref[g * wide:(g + 1) * wide, :]

        def logit_body(j, mx):
            k0 = pl.multiple_of(j * QBLK, QBLK)
            kj = kn_ref[pl.ds(k0, QBLK), gs].astype(BF16)
            lg = lax.dot_general(kj, q_stack, (((1,), (1,)), ((), ())), preferred_element_type=F32)
            madd = madd_ref[j]
            parts = []
            for r, h in enumerate(heads):
                quads = []
                for c in range(sub):
                    quads.append(jnp.concatenate(
                        [bias_ref[h, jnp.clip(2 - ((i - j) * sub + u - c), 0, 2)] for u in range(sub)], axis=1))
                parts.append(lg[:, r * QBLK:(r + 1) * QBLK] + jnp.concatenate(quads, axis=0) + madd)
            lg = jnp.concatenate(parts, axis=1)
            lg_ref[j] = lg
            return jnp.maximum(mx, _fold8(lg, jnp.max))

        mx = lax.fori_loop(0, nkb // 2, lambda jj, mx_: logit_body(2 * jj + 1, logit_body(2 * jj, mx_)),
                           jnp.full((SUBLANES, wide), NEG_BIG, F32))
        mx = lax.cond(nkb % 2 == 1, lambda mx_: logit_body(nkb - 1, mx_), lambda mx_: mx_, mx)
        m = jnp.max(mx, axis=0, keepdims=True)
        acc_ref[...] = jnp.zeros(acc_ref.shape, F32)

        def pv_body(blocks, sm):
            ps = [jnp.exp(lg_ref[j] - m) for j in blocks]
            acc_ref[...] += sum(jnp.dot(vt_ref[j, gs, :], p.astype(BF16), preferred_element_type=F32)
                                for j, p in zip(blocks, ps))
            return sm + sum(_fold8(p, jnp.sum) for p in ps)

        sm = lax.fori_loop(0, nkb // 2, lambda jj, sm_: pv_body((2 * jj, 2 * jj + 1), sm_),
                           jnp.zeros((SUBLANES, wide), F32))
        sm = lax.cond(nkb % 2 == 1, lambda sm_: pv_body((nkb - 1,), sm_), lambda sm_: sm_, sm)
        den = jnp.sum(sm, axis=0, keepdims=True)
        o = (acc_ref[...] / den).T
        for r, h in enumerate(heads):
            o_ref[:, h * HEAD_DIM:(h + 1) * HEAD_DIM] = o[r * QBLK:(r + 1) * QBLK, :].astype(o_ref.dtype)


def dsa_prompt(proj, qn, qst, kn, vt, bias_tab, n_batch, seq):
    T = proj.shape[0]
    nb = seq // QBLK
    k_sel = min(TOPK_MAX, seq // 4)
    aw = A_HEADS * HEAD_DIM
    kw = A_KV_HEADS * HEAD_DIM
    rep = A_HEADS // A_KV_HEADS
    return pl.pallas_call(
        functools.partial(_dsa_prompt_kernel, k_sel=k_sel),
        out_shape=jax.ShapeDtypeStruct((T, aw), BF16),
        grid=(n_batch, nb),
        in_specs=[pl.BlockSpec((None,) + qn.shape[1:], lambda b, i: (b * nb + i, 0, 0)),
                  pl.BlockSpec((None,) + qst.shape[1:], lambda b, i: (b * nb + i, 0, 0)),
                  pl.BlockSpec((seq, LANES), lambda b, i: (b, MISC_OFF // LANES)),
                  pl.BlockSpec((seq, kw), lambda b, i: (b, 0)),
                  pl.BlockSpec((nb, kw, QBLK), lambda b, i: (b, 0, 0)),
                  _resident((A_HEADS, 3, TILE, TILE))],
        out_specs=pl.BlockSpec((QBLK, aw), lambda b, i: (b * nb + i, 0)),
        scratch_shapes=[pltpu.VMEM((nb, QBLK, QBLK), I32),
                        pltpu.VMEM((nb, QBLK, QBLK), F32),
                        pltpu.VMEM((nb, QBLK, rep * QBLK), F32),
                        pltpu.VMEM((HEAD_DIM, rep * QBLK), F32)],
        compiler_params=_cparams(("parallel", "arbitrary")),
        name="dsa_prompt",
    )(qn, qst, proj, kn, vt, bias_tab)


def _dsa_sample_select_kernel(pt_ref, *refs, n_pages, n_tok, k_sel, rows_pad):
    del pt_ref
    per = rows_pad // n_tok
    page_refs = refs[:per * n_pages]
    qi_ref, wm_ref, kin_ref, mask_ref, sc_ref, skey_ref = refs[per * n_pages:]
    b = pl.program_id(0)
    nb = pl.num_programs(0)
    L = sc_ref.shape[1]
    past = n_pages * PAGE_SIZE

    relu_s = []
    for e in range(per):
        kt_all = jnp.concatenate([r[...].astype(BF16) for r in page_refs[e * n_pages:(e + 1) * n_pages]]
                                 + [kin_ref[e].astype(BF16)], axis=1)
        relu_s.append(jnp.maximum(jnp.dot(qi_ref[e].astype(BF16), kt_all, preferred_element_type=F32), 0.0))
    relu_cat = jnp.concatenate(relu_s, axis=0)
    wm = wm_ref[0]
    r_hi, w_hi = relu_cat.astype(BF16), wm.astype(BF16)
    r_lo, w_lo = (relu_cat - r_hi.astype(F32)).astype(BF16), (wm - w_hi.astype(F32)).astype(BF16)
    score = (jnp.dot(w_hi, r_hi, preferred_element_type=F32) + jnp.dot(w_hi, r_lo, preferred_element_type=F32)
             + jnp.dot(w_lo, r_hi, preferred_element_type=F32))
    r0 = pl.multiple_of(b * rows_pad, rows_pad)
    sc_ref[pl.ds(r0, rows_pad), :] = score

    @pl.when(b == nb - 1)
    def _():
        n_blocks = sc_ref.shape[0] // TILE
        n_tiles = L // PAGE_SIZE
        tp = past + lax.broadcasted_iota(I32, (TILE, L), 0) % n_tok
        sp = lax.broadcasted_iota(I32, (TILE, L), 1)
        adm_blk = sp <= tp
        d_r = lax.broadcasted_iota(I32, (PAGE_SIZE, PAGE_SIZE * A_KV_HEADS), 0)
        d_c = lax.broadcasted_iota(I32, (PAGE_SIZE, PAGE_SIZE * A_KV_HEADS), 1)
        dup = jnp.where(d_c // A_KV_HEADS == d_r, 1.0, 0.0).astype(BF16)
        for rb in range(n_blocks):
            rows = slice(rb * TILE, (rb + 1) * TILE)
            skey_ref[...] = _sortable_key(jnp.where(adm_blk, sc_ref[rows, :], -jnp.inf))
            sel = jnp.where(_topk_member(skey_ref, k_sel) & adm_blk, 1.0, 0.0).astype(BF16)
            stacked = jnp.concatenate([sel[:, j * PAGE_SIZE:(j + 1) * PAGE_SIZE] for j in range(n_tiles)], axis=0)
            stacked = jnp.dot(stacked, dup, preferred_element_type=F32)
            mask_ref[rows, :] = jnp.concatenate(
                [stacked[j * TILE:(j + 1) * TILE, :] for j in range(n_tiles)], axis=1)


def dsa_sample_select(cache_ik_t, layer, page_table, qi_rows, wmat, ki_new_t, n_tok, k_sel):
    DB, n_pages = page_table.shape
    rows_pad = wmat.shape[1]
    per = rows_pad // n_tok
    n_rows = DB // per * rows_pad
    L = (n_pages + 1) * PAGE_SIZE
    page_specs = [pl.BlockSpec((None, None, IDX_DIM, PAGE_SIZE), functools.partial(
        lambda b, pt, e, p: (layer, pt[b * per + e, p], 0, 0), e=e, p=p))
        for e in range(per) for p in range(n_pages)]
    grid_spec = pltpu.PrefetchScalarGridSpec(
        num_scalar_prefetch=1,
        grid=(DB // per,),
        in_specs=page_specs + [
            pl.BlockSpec((per,) + qi_rows.shape[1:], lambda b, pt: (b, 0, 0)),
            pl.BlockSpec((1,) + wmat.shape[1:], lambda b, pt: (b, 0, 0)),
            pl.BlockSpec((per, IDX_DIM, PAGE_SIZE), lambda b, pt: (b, 0, 0))],
        out_specs=pl.BlockSpec((n_rows, A_KV_HEADS * L), lambda b, pt: (0, 0)),
        scratch_shapes=[pltpu.VMEM((n_rows, L), F32),
                        pltpu.VMEM((TILE, L), I32)],
    )
    return pl.pallas_call(
        functools.partial(_dsa_sample_select_kernel, n_pages=n_pages, n_tok=n_tok, k_sel=k_sel,
                          rows_pad=rows_pad),
        out_shape=jax.ShapeDtypeStruct((n_rows, A_KV_HEADS * L), F32),
        grid_spec=grid_spec,
        compiler_params=_cparams(("arbitrary",)),
        name="dsa_sample_select",
    )(page_table, *([cache_ik_t] * (per * n_pages)), qi_rows, wmat, ki_new_t)


def _dsa_sample_attend_kernel(pt_ref, *refs, n_pages, n_tok, rows_pad):
    del pt_ref
    per = rows_pad // n_tok
    k_refs = refs[:per * n_pages]
    v_refs = refs[per * n_pages:2 * per * n_pages]
    q_ref, kn_ref, vn_ref, mask_ref, bias_ref, qg_ref, o_ref = refs[2 * per * n_pages:]
    rows = n_tok * A_HEADS
    page_rows = PAGE_SIZE * A_KV_HEADS
    n_cols = mask_ref.shape[1]
    pad = jnp.zeros((page_rows - n_tok * A_KV_HEADS, HEAD_DIM), BF16)
    rep = A_HEADS // A_KV_HEADS
    grp = (lax.broadcasted_iota(I32, (rows, 1), 0) % A_HEADS) // rep
    own_group = (lax.broadcasted_iota(I32, (rows, n_cols), 1) % A_KV_HEADS) == grp
    member = mask_ref[...]

    for e in range(per):
        def tiles(page_refs, new_ref):
            new = jnp.concatenate([new_ref[e].astype(BF16), pad], axis=0)
            return [r[...].astype(BF16) for r in page_refs[e * n_pages:(e + 1) * n_pages]] + [new]

        q = (_rms(q_ref[e], qg_ref[...]) * HEAD_DIM ** -0.5).astype(BF16)
        sel = jnp.concatenate(
            [jnp.broadcast_to(member[e * n_tok + t:e * n_tok + t + 1, :], (A_HEADS, n_cols)) for t in range(n_tok)],
            axis=0)
        valid = (sel > 0.5) & own_group
        logits = jnp.concatenate(
            [lax.dot_general(q, kt, (((1,), (1,)), ((), ())), preferred_element_type=F32)
             for kt in tiles(k_refs, kn_ref)], axis=1)
        logits = jnp.where(valid, logits + bias_ref[...], NEG_BIG)
        m = jnp.max(logits, axis=-1, keepdims=True)
        p = jnp.exp(logits - m)
        den = jnp.sum(p, axis=-1, keepdims=True)
        pb = p.astype(BF16)
        o = jnp.zeros((rows, HEAD_DIM), F32)
        for j, vt in enumerate(tiles(v_refs, vn_ref)):
            o = o + jnp.dot(pb[:, j * page_rows:(j + 1) * page_rows], vt, preferred_element_type=F32)
        o_ref[e] = (o / den).astype(o_ref.dtype)


def dsa_sample_attend(cache_k, cache_v, layer, page_table, q_rows, k_new, v_new, mask, bias_tab, q_norm_g):
    DB, n_pages = page_table.shape
    n_tok = k_new.shape[1] // A_KV_HEADS
    rows = n_tok * A_HEADS
    rows_pad = SUBLANES
    per = rows_pad // n_tok
    n_cols = mask.shape[1]
    page_rows = PAGE_SIZE * A_KV_HEADS
    page_specs = [pl.BlockSpec((None, None, page_rows, HEAD_DIM), functools.partial(
        lambda b, pt, e, p: (layer, pt[b * per + e, p], 0, 0), e=e, p=p))
        for e in range(per) for p in range(n_pages)]
    grid_spec = pltpu.PrefetchScalarGridSpec(
        num_scalar_prefetch=1,
        grid=(DB // per,),
        in_specs=page_specs + page_specs + [
            pl.BlockSpec((per, rows, HEAD_DIM), lambda b, pt: (b, 0, 0)),
            pl.BlockSpec((per, n_tok * A_KV_HEADS, HEAD_DIM), lambda b, pt: (b, 0, 0)),
            pl.BlockSpec((per, n_tok * A_KV_HEADS, HEAD_DIM), lambda b, pt: (b, 0, 0)),
            pl.BlockSpec((rows_pad, n_cols), lambda b, pt: (b, 0)),
            pl.BlockSpec((rows, n_cols), lambda b, pt: (0, 0), pipeline_mode=pl.Buffered(1)),
            pl.BlockSpec((1, HEAD_DIM), lambda b, pt: (0, 0), pipeline_mode=pl.Buffered(1))],
        out_specs=pl.BlockSpec((per, rows, HEAD_DIM), lambda b, pt: (b, 0, 0)),
    )
    return pl.pallas_call(
        functools.partial(_dsa_sample_attend_kernel, n_pages=n_pages, n_tok=n_tok, rows_pad=rows_pad),
        out_shape=jax.ShapeDtypeStruct((DB, rows, HEAD_DIM), BF16),
        grid_spec=grid_spec,
        compiler_params=_cparams(("parallel",)),
        name="dsa_sample_attend",
    )(page_table, *([cache_k] * (per * n_pages)), *([cache_v] * (per * n_pages)), q_rows, k_new, v_new, mask,
      bias_tab, q_norm_g.reshape(1, HEAD_DIM))


def _log_sigmoid(z):
    return jnp.minimum(z, 0.0) - jnp.log(1.0 + jnp.exp(-jnp.abs(z)))


def _seg_masks(seg):
    r = lax.broadcasted_iota(I32, (TILE, TILE), 0)
    c = lax.broadcasted_iota(I32, (TILE, TILE), 1)
    return r, c, (r // seg) == (c // seg)


def _gla_levels(seg):
    w, out = seg // 2, []
    while w >= 1:
        out.append(w)
        w //= 2
    return out


def _gla_sum_matrices(seg):
    r = jnp.arange(TILE)[:, None]
    c = jnp.arange(TILE)[None, :]
    mats = []
    for w in _gla_levels(seg):
        same = (r // (2 * w)) == (c // (2 * w))
        r_right = (r % (2 * w)) >= w
        c_right = (c % (2 * w)) >= w
        mats.append(same & r_right & c_right & (c <= r))
    for w in _gla_levels(seg):
        same = (r // (2 * w)) == (c // (2 * w))
        r_right = (r % (2 * w)) >= w
        c_right = (c % (2 * w)) >= w
        mats.append(same & (~r_right) & (~c_right) & (c > r))
    same_seg = (r // seg) == (c // seg)
    mats.append(same_seg & (c <= r))
    mats.append(same_seg & (c > r))
    return jnp.concatenate(mats, axis=0).astype(BF16)


def _bdot(a, b):
    return jnp.dot(a.astype(BF16), b.astype(BF16), preferred_element_type=F32)


def _bdot_nt(a, b):
    return lax.dot_general(a.astype(BF16), b.astype(BF16), (((1,), (1,)), ((), ())), preferred_element_type=F32)


def _gla_common(tiles, wg_ref, bg_ref, mats_ref, seg):
    n = len(tiles)
    kw = B_HEADS * B_DK
    las = []
    for _, _, misc_ref in tiles:
        gb = misc_ref[:, MISC_GB:MISC_GB + GATE_RANK]
        z = jnp.dot(gb, wg_ref[...], precision=HIGHEST, preferred_element_type=F32) + bg_ref[...]
        las.append(_log_sigmoid(z) / GATE_TEMP)
    la = jnp.concatenate(las, axis=1)
    la_hi = la.astype(BF16)
    la_lo = (la - la_hi.astype(F32)).astype(BF16)
    mats = mats_ref[...]
    sums = (jnp.dot(mats, la_hi, preferred_element_type=F32) + jnp.dot(mats, la_lo, preferred_element_type=F32))
    levels = _gla_levels(seg)
    nl = len(levels)
    qs = [qb_ref[...] * B_DK ** -0.5 for qb_ref, _, _ in tiles]
    ks = [kb_ref[...] for _, kb_ref, _ in tiles]
    r, c, _ = _seg_masks(seg)
    atts = [[jnp.where(r == c, _bdot_nt(qs[i][:, h * B_DK:(h + 1) * B_DK], ks[i][:, h * B_DK:(h + 1) * B_DK]), 0.0)
             for h in range(B_HEADS)] for i in range(n)]
    for li, w in enumerate(levels):
        pair = ((r // (2 * w)) == (c // (2 * w))) & ((r % (2 * w)) >= w) & ((c % (2 * w)) < w)
        qd = [(qs[i] * jnp.exp(sums[li * TILE:(li + 1) * TILE, i * kw:(i + 1) * kw])).astype(BF16) for i in range(n)]
        kd = [(ks[i] * jnp.exp(sums[(nl + li) * TILE:(nl + li + 1) * TILE, i * kw:(i + 1) * kw])).astype(BF16)
              for i in range(n)]
        for h in range(B_HEADS):
            hs = slice(h * B_DK, (h + 1) * B_DK)
            for i in range(n):
                atts[i][h] = atts[i][h] + jnp.where(pair, _bdot_nt(qd[i][:, hs], kd[i][:, hs]), 0.0)
    out = []
    for i in range(n):
        cs = slice(i * kw, (i + 1) * kw)
        b_cum = sums[2 * nl * TILE:(2 * nl + 1) * TILE, cs]
        rem = sums[(2 * nl + 1) * TILE:(2 * nl + 2) * TILE, cs]
        out.append((qs[i], ks[i], atts[i], b_cum, rem))
    return out


def _gla_finish(o_heads, rb_ref, go_ref, o_ref):
    go = go_ref[...]
    for h in range(B_HEADS):
        vs = slice(h * B_DV, (h + 1) * B_DV)
        rb = rb_ref[:, vs]
        o_ref[:, vs] = (_rms(o_heads[h], go) * (rb * jax.nn.sigmoid(rb))).astype(o_ref.dtype)


def _gla_prompt_kernel(qb_ref, kb_ref, vb_ref, rb_ref, misc_ref, wg_ref, bg_ref, go_ref, mats_ref,
                       o_ref, s_ref, state_ref):
    ci = pl.program_id(0)

    @pl.when(ci == 0)
    def _():
        state_ref[...] = jnp.zeros_like(state_ref)

    nb = qb_ref.shape[0]
    common = _gla_common([(qb_ref.at[b], kb_ref.at[b], misc_ref.at[b]) for b in range(nb)],
                         wg_ref, bg_ref, mats_ref, TILE)
    vals = [vb_ref[b] for b in range(nb)]
    states = [state_ref[b] for b in range(nb)]
    qes = [common[b][0] * jnp.exp(common[b][3]) for b in range(nb)]
    o_heads = [[] for _ in range(nb)]
    for h in range(B_HEADS):
        ks = slice(h * B_DK, (h + 1) * B_DK)
        vs = slice(h * B_DV, (h + 1) * B_DV)
        for b in range(nb):
            o_heads[b].append(_bdot(qes[b][:, ks], states[b][ks, :]) + _bdot(common[b][2][h], vals[b][:, vs]))
    for b in range(nb):
        _gla_finish(o_heads[b], rb_ref.at[b], go_ref, o_ref.at[b])

    ke_ts = [(common[b][1] * jnp.exp(common[b][4])).T for b in range(nb)]
    e_cols = [jnp.broadcast_to(jnp.exp(common[b][3][TILE - 1:TILE, :]), (TILE, B_HEADS * B_DK)).T[:, 0:1]
              for b in range(nb)]
    for b in range(nb):
        upd = jnp.concatenate(
            [_bdot(ke_ts[b][h * B_DK:(h + 1) * B_DK, :], vals[b][:, h * B_DV:(h + 1) * B_DV])
             for h in range(B_HEADS)], axis=0)
        new_state = states[b] * e_cols[b] + upd
        state_ref[b] = new_state
        s_ref[b] = new_state


def gla_prompt(proj, w_gate, b_gate, g_out, n_batch, seq):
    nc = seq // TILE
    kwid = B_HEADS * B_DK
    vwid = B_HEADS * B_DV
    mats = _gla_sum_matrices(TILE)
    proj3 = proj.reshape(n_batch, seq, proj.shape[1])
    o, s = pl.pallas_call(
        _gla_prompt_kernel,
        out_shape=(jax.ShapeDtypeStruct((n_batch, seq, vwid), BF16),
                   jax.ShapeDtypeStruct((n_batch, kwid, B_DV), F32)),
        grid=(nc,),
        in_specs=[pl.BlockSpec((n_batch, TILE, kwid), lambda c: (0, c, QB_OFF // kwid)),
                  pl.BlockSpec((n_batch, TILE, kwid), lambda c: (0, c, KB_OFF // kwid)),
                  pl.BlockSpec((n_batch, TILE, vwid), lambda c: (0, c, VB_OFF // vwid)),
                  pl.BlockSpec((n_batch, TILE, vwid), lambda c: (0, c, RB_OFF // vwid)),
                  pl.BlockSpec((n_batch, TILE, LANES), lambda c: (0, c, MISC_OFF // LANES)),
                  _resident((GATE_RANK, kwid)),
                  _resident((1, kwid)),
                  _resident((1, B_DV)),
                  _resident(mats.shape)],
        out_specs=(pl.BlockSpec((n_batch, TILE, vwid), lambda c: (0, c, 0)),
                   pl.BlockSpec((n_batch, kwid, B_DV), lambda c: (0, 0, 0))),
        scratch_shapes=[pltpu.VMEM((n_batch, kwid, B_DV), F32)],
        compiler_params=_cparams(("arbitrary",)),
        name="gla_prompt",
    )(proj3, proj3, proj3, proj3, proj3, w_gate, b_gate.reshape(1, kwid), g_out.reshape(1, B_DV), mats)
    return o.reshape(n_batch * seq, vwid), s.reshape(n_batch, B_HEADS, B_DK, B_DV)


def _gla_sample_kernel(qb_ref, kb_ref, vb_ref, rb_ref, misc_ref, wg_ref, bg_ref, go_ref, mats_ref, s0_ref,
                       o_ref, s_ref, *, seg):
    nbt = TILE // seg
    (q, k, att, b_cum, rem), = _gla_common([(qb_ref, kb_ref, misc_ref)], wg_ref, bg_ref, mats_ref, seg)
    v = vb_ref[...]
    qe = q * jnp.exp(b_cum)
    ke = k * jnp.exp(rem)
    r1 = lax.broadcasted_iota(I32, (TILE, 1), 0)
    e_last = jnp.where(r1 % seg == seg - 1, jnp.exp(b_cum), 0.0)
    wide = nbt * B_DK
    mq = (lax.broadcasted_iota(I32, (TILE, wide), 0) // seg) == (lax.broadcasted_iota(I32, (TILE, wide), 1) // B_DK)
    mk = (lax.broadcasted_iota(I32, (wide, TILE), 0) // B_DK) == (lax.broadcasted_iota(I32, (wide, TILE), 1) // seg)
    o_heads = []
    for h in range(B_HEADS):
        ks = slice(h * B_DK, (h + 1) * B_DK)
        vs = slice(h * B_DV, (h + 1) * B_DV)
        state = s0_ref[:, h].reshape(wide, B_DV)
        q_bd = jnp.where(mq, jnp.concatenate([qe[:, ks]] * nbt, axis=1), 0.0)
        o_heads.append(_bdot(q_bd, state) + _bdot(att[h], v[:, vs]))
        pair_t = jnp.concatenate([ke[:, ks], e_last[:, ks]], axis=1).T
        k_bd = jnp.where(mk, jnp.concatenate([pair_t[:B_DK]] * nbt, axis=0), 0.0)
        e_bd = jnp.where(mk, jnp.concatenate([pair_t[B_DK:]] * nbt, axis=0), 0.0)
        e_col = jnp.sum(e_bd, axis=-1, keepdims=True)
        new_state = state * e_col + _bdot(k_bd, v[:, vs])
        s_ref[:, h] = new_state.reshape(nbt, B_DK, B_DV)
    _gla_finish(o_heads, rb_ref, go_ref, o_ref)


def gla_sample(proj, w_gate, b_gate, g_out, s0, layer, n_tok):
    T = proj.shape[0]
    nbt = TILE // n_tok
    kwid = B_HEADS * B_DK
    vwid = B_HEADS * B_DV
    mats = _gla_sum_matrices(n_tok)
    return pl.pallas_call(
        functools.partial(_gla_sample_kernel, seg=n_tok),
        out_shape=(jax.ShapeDtypeStruct((T, vwid), BF16),
                   jax.ShapeDtypeStruct(s0.shape[1:], F32)),
        grid=(T // TILE,),
        in_specs=[pl.BlockSpec((TILE, kwid), lambda i: (i, QB_OFF // kwid)),
                  pl.BlockSpec((TILE, kwid), lambda i: (i, KB_OFF // kwid)),
                  pl.BlockSpec((TILE, vwid), lambda i: (i, VB_OFF // vwid)),
                  pl.BlockSpec((TILE, vwid), lambda i: (i, RB_OFF // vwid)),
                  pl.BlockSpec((TILE, LANES), lambda i: (i, MISC_OFF // LANES)),
                  _resident((GATE_RANK, kwid)),
                  _resident((1, kwid)),
                  _resident((1, B_DV)),
                  _resident(mats.shape),
                  pl.BlockSpec((None, nbt, B_HEADS, B_DK, B_DV), lambda i: (layer, i, 0, 0, 0))],
        out_specs=(pl.BlockSpec((TILE, vwid), lambda i: (i, 0)),
                   pl.BlockSpec((nbt, B_HEADS, B_DK, B_DV), lambda i: (i, 0, 0, 0))),
        compiler_params=_cparams(("parallel",)),
        name="gla_sample",
    )(proj, proj, proj, proj, proj, w_gate, b_gate.reshape(1, kwid), g_out.reshape(1, B_DV), mats, s0)


def _gelu(x):
    return jax.nn.gelu(x)


def _gmlp_rows(uc_ref, vc_ref, gv_ref, ws_ref, bcol_ref, o_ref, vn_ref, seg):
    r, c, same_seg = _seg_masks(seg)
    keep = same_seg & (c <= r)
    for t in range(uc_ref.shape[0] // TILE):
        rows = slice(t * TILE, (t + 1) * TILE)
        u = _gelu(uc_ref[rows, :])
        vg = _gelu(vc_ref[rows, :])
        for g in range(C_GROUPS):
            gs = slice(g * C_GROUP_DIM, (g + 1) * C_GROUP_DIM)
            vn = _rms(vg[:, gs], gv_ref[:, gs])
            if vn_ref is not None:
                vn_ref[rows, gs] = vn
            w = jnp.where(keep, ws_ref[g], 0.0).astype(BF16)
            s = jnp.dot(w, vn.astype(BF16), preferred_element_type=F32) + bcol_ref[:, g:g + 1]
            o_ref[rows, gs] = (u[:, gs] * s).astype(o_ref.dtype)


def _ffn_kernel(h_ref, oa_ref, ob_ref, oc_ref, wo_ref, g_ref, wg_ref, wu_ref, wd_ref, o_ref, n_ref):
    j = pl.program_id(1)

    @pl.when(j == 0)
    def _():
        aw = oa_ref.shape[1]
        bw = ob_ref.shape[1]
        h = h_ref[...] + jnp.dot(oa_ref[...], wo_ref[0:aw, :], preferred_element_type=F32)
        h = h + jnp.dot(ob_ref[...], wo_ref[aw:aw + bw, :], preferred_element_type=F32)
        h = h + jnp.dot(oc_ref[...], wo_ref[aw + bw:, :], preferred_element_type=F32)
        n_ref[...] = _rms(h, g_ref[...]).astype(BF16)
        o_ref[...] = h

    n = n_ref[...]
    a = jnp.dot(n, wg_ref[...], preferred_element_type=F32)
    u = jnp.dot(n, wu_ref[...], preferred_element_type=F32)
    act = (a * jax.nn.sigmoid(a) * u).astype(BF16)
    o_ref[...] += jnp.dot(act, wd_ref[...], preferred_element_type=F32)


def out_proj_ffn(h, o_a, o_b, o_c, w_out, g, w_gate, w_up, w_down, tm, tf):
    T, D = h.shape
    tm = min(tm, T)
    FF = w_gate.shape[1]
    return pl.pallas_call(
        _ffn_kernel,
        out_shape=jax.ShapeDtypeStruct((T, D), F32),
        grid=(T // tm, FF // tf),
        in_specs=[pl.BlockSpec((tm, D), lambda i, j: (i, 0)),
                  pl.BlockSpec((tm, o_a.shape[1]), lambda i, j: (i, 0)),
                  pl.BlockSpec((tm, o_b.shape[1]), lambda i, j: (i, 0)),
                  pl.BlockSpec((tm, o_c.shape[1]), lambda i, j: (i, 0)),
                  _resident(w_out.shape),
                  _resident((1, D)),
                  pl.BlockSpec((D, tf), lambda i, j: (0, j)),
                  pl.BlockSpec((D, tf), lambda i, j: (0, j)),
                  pl.BlockSpec((tf, D), lambda i, j: (j, 0))],
        out_specs=pl.BlockSpec((tm, D), lambda i, j: (i, 0)),
        scratch_shapes=[pltpu.VMEM((tm, D), BF16)],
        compiler_params=_cparams(("parallel", "arbitrary")),
        name="out_proj_ffn",
    )(h, o_a, o_b, o_c, w_out, g.reshape(1, D), w_gate, w_up, w_down)


def _ple_kernel(h_ref, p_ref, g_ref, wgate_ref, wproj_ref, o_ref):
    h = h_ref[...]
    n = _rms(h, g_ref[...]).astype(BF16)
    gate = jax.nn.sigmoid(jnp.dot(n, wgate_ref[...], preferred_element_type=F32))
    emb = jnp.dot(p_ref[...].astype(BF16), wproj_ref[...], preferred_element_type=F32)
    o_ref[...] = h + gate * emb


def ple(h, p, g, w_gate, w_proj, layer, tm):
    T, D = h.shape
    tm = min(tm, T)
    P = p.shape[2]
    return pl.pallas_call(
        _ple_kernel,
        out_shape=jax.ShapeDtypeStruct((T, D), F32),
        grid=(T // tm,),
        in_specs=[pl.BlockSpec((tm, D), lambda i: (i, 0)),
                  pl.BlockSpec((None, tm, P), lambda i: (layer, i, 0)),
                  _resident((1, D)),
                  _resident(w_gate.shape),
                  _resident(w_proj.shape)],
        out_specs=pl.BlockSpec((tm, D), lambda i: (i, 0)),
        compiler_params=_cparams(("parallel",)),
        name="ple",
    )(h, p, g.reshape(1, D), w_gate, w_proj)


_W_IN_SEGMENTS = (("q", 1024), ("k", 256), ("v", 256), ("qi", 1024), ("ki", 64), ("wi", 16), ("qb", 256),
                  ("kb", 256), ("vb", 512), ("gb", 16), ("rb", 512), ("uc", 512), ("vc", 512))
_W_IN_PACKED_ORDER = ("q", "qi", "vb", "rb", "uc", "vc", "k", "v", "qb", "kb", "ki", "wi", "gb")


def _pack_kernel(wt_ref, o_ref):
    src, start = {}, 0
    for name, size in _W_IN_SEGMENTS:
        src[name] = (start, size)
        start += size
    dst = 0
    small = []
    for name in _W_IN_PACKED_ORDER:
        s0, size = src[name]
        if size < LANES:
            small.append(wt_ref[s0:s0 + size, :])
            continue
        o_ref[:, dst:dst + size] = wt_ref[s0:s0 + size, :].T.astype(BF16)
        dst += size
    used = sum(x.shape[0] for x in small)
    small.append(jnp.zeros((LANES - used, wt_ref.shape[1]), F32))
    o_ref[:, dst:dst + LANES] = jnp.concatenate(small, axis=0).T.astype(BF16)


def _pack_w_in(w, tr=256):
    depth, D, N = w.shape
    return pl.pallas_call(
        _pack_kernel,
        out_shape=jax.ShapeDtypeStruct((depth, D, PROJ_PACKED), BF16),
        grid=(depth, D // tr),
        in_specs=[pl.BlockSpec((None, N, tr), lambda l, i: (l, 0, i))],
        out_specs=pl.BlockSpec((None, tr, PROJ_PACKED), lambda l, i: (l, i, 0)),
        compiler_params=_cparams(("parallel", "parallel")),
        name="pack_w_in",
    )(jnp.swapaxes(w, 1, 2))


def _mixer_tail(h, o_a, o_b, o_c, p_all, lw, layer):
    h = out_proj_ffn(h, o_a, o_b, o_c, lw["w_out"], lw["g_ffn"], lw["w_ffn_gate"], lw["w_ffn_up"],
                     lw["w_ffn_down"], TM_FFN, TF_FFN)
    return ple(h, p_all, lw["g_ple"], lw["w_ple_gate"], lw["w_ple_proj"], layer, TM_PLE)


def kernel(x_prompt, x_sample, cache_k, cache_v, cache_idx_k, state_gla, page_table, p_prompt, p_sample,
           g_mix, w_in, q_norm_g, k_norm_g, rel_bias, w_gate_b, b_gate_b, g_out_b, g_v_c, w_spatial,
           b_spatial, w_out, g_ffn, w_ffn_gate, w_ffn_up, w_ffn_down, g_ple, w_ple_gate, w_ple_proj):
    n_batch, seq, d_model = x_prompt.shape
    dec_batch, dec_seq, _ = x_sample.shape
    depth = w_in.shape[0]
    n_pages = page_table.shape[1]
    past = n_pages * PAGE_SIZE
    kw = A_KV_HEADS * HEAD_DIM
    tp, ts = n_batch * seq, dec_batch * dec_seq
    rows_pad = SUBLANES
    l_sample = past + PAGE_SIZE
    k_sel_s = min(TOPK_MAX, (past + dec_seq) // 4)

    bias_p = bias_table_prompt(rel_bias)
    bias_s = bias_table_sample(rel_bias, past, dec_seq, A_KV_HEADS * l_sample)
    cache_ik_t = jnp.swapaxes(cache_idx_k, 2, 3)
    cache_k2 = cache_k.reshape(depth, cache_k.shape[1], PAGE_SIZE * A_KV_HEADS, HEAD_DIM)
    cache_v2 = cache_v.reshape(depth, cache_v.shape[1], PAGE_SIZE * A_KV_HEADS, HEAD_DIM)

    hp = x_prompt.reshape(tp, d_model)
    hs = x_sample.reshape(ts, d_model)
    outs = {k: [] for k in ("kp", "vp", "ikp", "sp", "ks", "vs", "iks", "ss", "cs")}
    per_s = rows_pad // dec_seq
    place_t = (jnp.arange(rows_pad)[:, None, None]
               == jnp.arange(per_s)[None, :, None] * dec_seq + jnp.arange(dec_seq)[None, None, :]
               ).astype(F32)
    w_ple_proj_b = w_ple_proj.astype(BF16)
    w_packed = _pack_w_in(w_in)
    pp_all = p_prompt.reshape(depth, tp, -1)
    ps_all = p_sample.reshape(depth, ts, -1)
    b_cols_p = jnp.swapaxes(b_spatial, 1, 2)
    reps = TILE // dec_seq
    w_tiles_s = jnp.tile(w_spatial[:, :, :dec_seq, :dec_seq], (1, 1, reps, reps))
    b_cols_s = jnp.tile(jnp.swapaxes(b_spatial[:, :, :dec_seq], 1, 2), (1, reps, 1))
    for i in range(depth):
        proj, (kn, vv, ik, o_c, vt, k3, v3, qn, qst), (wo_b, wg_b, wu_b, wd_b, wpg_b) = in_projection(
            hp, g_mix[i], w_packed, k_norm_g[i], q_norm_g[i], (g_v_c[i], w_spatial, b_cols_p, TILE), i, TM_PROJ,
            True, False, cast=(w_out, w_ffn_gate, w_ffn_up, w_ffn_down, w_ple_gate))
        lw = dict(w_out=wo_b, g_ffn=g_ffn[i], w_ffn_gate=wg_b, w_ffn_up=wu_b, w_ffn_down=wd_b,
                  g_ple=g_ple[i], w_ple_gate=wpg_b, w_ple_proj=w_ple_proj_b[i])
        o_a = dsa_prompt(proj, qn, qst, kn, vt, bias_p, n_batch, seq)
        o_b, s_p = gla_prompt(proj, w_gate_b[i], b_gate_b[i], g_out_b[i], n_batch, seq)
        hp = _mixer_tail(hp, o_a, o_b, o_c, pp_all, lw, i)
        outs["kp"].append(k3.reshape(n_batch, seq, A_KV_HEADS, HEAD_DIM))
        outs["vp"].append(v3.reshape(n_batch, seq, A_KV_HEADS, HEAD_DIM))
        outs["ikp"].append(ik.reshape(n_batch, seq, IDX_DIM))
        outs["sp"].append(s_p)

        proj, (kn, vv, ik, o_c, vn), _ = in_projection(
            hs, g_mix[i], w_packed, k_norm_g[i], q_norm_g[i], (g_v_c[i], w_tiles_s, b_cols_s, dec_seq), i, TM_PROJ,
            False, True)
        qi_rows = proj[:, QI_OFF:QI_OFF + IDX_HEADS * IDX_DIM].reshape(dec_batch, dec_seq * IDX_HEADS, IDX_DIM)
        wi = proj[:, MISC_OFF + MISC_WI:MISC_OFF + MISC_WI + IDX_HEADS].reshape(dec_batch, dec_seq, IDX_HEADS)
        wi = wi * (IDX_HEADS ** -0.5 * IDX_DIM ** -0.5)
        wmat = (place_t[None, :, :, :, None] * wi.reshape(dec_batch // per_s, 1, per_s, dec_seq, IDX_HEADS)
                ).reshape(dec_batch // per_s, rows_pad, per_s * dec_seq * IDX_HEADS)
        ki_new_t = jnp.pad(jnp.swapaxes(ik.reshape(dec_batch, dec_seq, IDX_DIM), 1, 2),
                           ((0, 0), (0, 0), (0, PAGE_SIZE - dec_seq)))
        mask = dsa_sample_select(cache_ik_t, i, page_table, qi_rows, wmat, ki_new_t, dec_seq, k_sel_s)
        q_rows = proj[:, Q_OFF:Q_OFF + A_HEADS * HEAD_DIM].reshape(dec_batch, dec_seq * A_HEADS, HEAD_DIM)
        o_a = dsa_sample_attend(cache_k2, cache_v2, i, page_table, q_rows,
                                kn.reshape(dec_batch, dec_seq * A_KV_HEADS, HEAD_DIM),
                                vv.reshape(dec_batch, dec_seq * A_KV_HEADS, HEAD_DIM), mask, bias_s, q_norm_g[i])
        o_a = o_a.reshape(ts, A_HEADS * HEAD_DIM)
        o_b, s_s = gla_sample(proj, w_gate_b[i], b_gate_b[i], g_out_b[i], state_gla, i, dec_seq)
        hs = _mixer_tail(hs, o_a, o_b, o_c, ps_all, lw, i)
        outs["ks"].append(kn.reshape(dec_batch, dec_seq, A_KV_HEADS, HEAD_DIM))
        outs["vs"].append(vv.reshape(dec_batch, dec_seq, A_KV_HEADS, HEAD_DIM))
        outs["iks"].append(ik.reshape(dec_batch, dec_seq, IDX_DIM))
        outs["ss"].append(s_s)
        outs["cs"].append(vn.reshape(dec_batch, dec_seq, -1))

    st = {k: jnp.stack(v) for k, v in outs.items()}
    return (hp.reshape(n_batch, seq, d_model), hs.reshape(dec_batch, dec_seq, d_model),
            st["kp"], st["vp"], st["ikp"], st["sp"], st["ks"], st["vs"], st["iks"], st["ss"], st["cs"])
```

```python
import functools
import math

import jax
import jax.numpy as jnp
from jax import lax
from jax.experimental import pallas as pl
from jax.experimental.pallas import tpu as pltpu

F32 = jnp.float32
BF16 = jnp.bfloat16
I32 = jnp.int32
HIGHEST = lax.Precision.HIGHEST

LANES = 128
SUBLANES = 8
VMEM_LIMIT = 60 * 1024 * 1024

HEAD_DIM = 128
A_HEADS = 8
A_KV_HEADS = 2
IDX_HEADS = 16
IDX_DIM = 64
TOPK_MAX = 256
NUM_BUCKETS = 32
MAX_DISTANCE = 128
B_HEADS = 4
B_DK = 64
B_DV = 128
GATE_RANK = 16
GATE_TEMP = 16.0
C_GROUPS = 4
C_GROUP_DIM = 128
PAGE_SIZE = 128
EPS = 1e-6
NEG_BIG = -1e30
INT_MIN = -(2 ** 31)
NEG_INF_KEY = -2139095041

TILE = 128
QBLK = 256
TM_PROJ = 256
TM_FFN = 512
TF_FFN = 512
TM_PLE = 512

Q_OFF, QI_OFF, VB_OFF, RB_OFF, UC_OFF, VC_OFF = 0, 1024, 2048, 2560, 3072, 3584
K_OFF, V_OFF, QB_OFF, KB_OFF, MISC_OFF = 4096, 4352, 4608, 4864, 5120
PROJ_PACKED = 5248
MISC_KI, MISC_WI, MISC_GB = 0, 64, 80


def _cparams(sem):
    return pltpu.CompilerParams(dimension_semantics=sem, vmem_limit_bytes=VMEM_LIMIT)


def _rms(x, g):
    return x * lax.rsqrt(jnp.mean(x * x, axis=-1, keepdims=True) + EPS) * g


def _resident(shape):
    nd = len(shape)
    return pl.BlockSpec(shape, lambda *_: (0,) * nd, pipeline_mode=pl.Buffered(1))


def _layer_resident(shape, layer):
    nd = len(shape)
    return pl.BlockSpec((None,) + tuple(shape), lambda *_: (layer,) + (0,) * nd, pipeline_mode=pl.Buffered(1))


def _proj_kernel(x_ref, g_ref, w_ref, kg_ref, qg_ref, gv_ref, ws_ref, bcol_ref, *refs, n_cast, with_vt, with_vn,
                 seg):
    cast_in, refs = list(refs[:n_cast]), list(refs[n_cast:])
    o_ref, ko_ref, vo_ref, io_ref, oc_ref = refs[:5]
    refs = refs[5:]
    vt_ref, k3_ref, v3_ref, qn_ref, qst_ref = [refs.pop(0) for _ in range(5)] if with_vt else [None] * 5
    vn_ref = refs.pop(0) if with_vn else None
    cast_out = refs
    n = _rms(x_ref[...], g_ref[...]).astype(BF16)
    ncol = o_ref.shape[1]
    step = 512
    for c0 in range(0, ncol, step):
        c1 = min(c0 + step, ncol)
        o_ref[:, c0:c1] = jnp.dot(n, w_ref[:, c0:c1], preferred_element_type=F32)
    kg = kg_ref[...]
    v = o_ref[:, V_OFF:V_OFF + A_KV_HEADS * HEAD_DIM]
    for hh in range(A_KV_HEADS):
        hs = slice(hh * HEAD_DIM, (hh + 1) * HEAD_DIM)
        kn = _rms(o_ref[:, K_OFF + hh * HEAD_DIM:K_OFF + (hh + 1) * HEAD_DIM], kg)
        ko_ref[:, hs] = kn
        if with_vt:
            k3_ref[:, hh, :] = kn
            v3_ref[:, hh, :] = v[:, hs]
    vo_ref[...] = v
    io_ref[...] = o_ref[:, MISC_OFF + MISC_KI:MISC_OFF + MISC_KI + IDX_DIM]
    if with_vt:
        for blk in range(vt_ref.shape[0]):
            vt_ref[blk] = v[blk * QBLK:(blk + 1) * QBLK, :].T.astype(vt_ref.dtype)
        tm = o_ref.shape[0]
        qg = qg_ref[...]
        for h in range(A_HEADS):
            q = o_ref[:, Q_OFF + h * HEAD_DIM:Q_OFF + (h + 1) * HEAD_DIM]
            qn_ref[h * tm:(h + 1) * tm, :] = (_rms(q, qg) * HEAD_DIM ** -0.5).astype(qn_ref.dtype)
        for h in range(IDX_HEADS):
            qst_ref[h * tm:(h + 1) * tm, :] = o_ref[:, QI_OFF + h * IDX_DIM:QI_OFF + (h + 1) * IDX_DIM].astype(
                qst_ref.dtype)
    cw = C_GROUPS * C_GROUP_DIM
    _gmlp_rows(o_ref.at[:, UC_OFF:UC_OFF + cw], o_ref.at[:, VC_OFF:VC_OFF + cw], gv_ref, ws_ref, bcol_ref,
               oc_ref, vn_ref, seg)
    for src, dst in zip(cast_in, cast_out):
        dst[...] = src[...].astype(dst.dtype)


def in_projection(h, g, w_packed, k_norm_g, q_norm_g, gmlp_params, layer, tm, with_vt, with_vn, cast=()):
    T, D = h.shape
    tm = min(tm, T)
    N = w_packed.shape[2]
    kw = A_KV_HEADS * HEAD_DIM
    cw = C_GROUPS * C_GROUP_DIM
    g_v, w_tiles, b_cols, seg = gmlp_params
    steps = T // tm
    cast_specs_in = [pl.BlockSpec((None, w.shape[1] // steps, w.shape[2]), lambda i: (layer, i, 0)) for w in cast]
    cast_specs_out = [pl.BlockSpec((w.shape[1] // steps, w.shape[2]), lambda i: (i, 0)) for w in cast]
    kv_shape = [jax.ShapeDtypeStruct((T, kw), F32),
                jax.ShapeDtypeStruct((T, kw), F32),
                jax.ShapeDtypeStruct((T, IDX_DIM), F32),
                jax.ShapeDtypeStruct((T, cw), BF16)]
    kv_specs = [pl.BlockSpec((tm, kw), lambda i: (i, 0)),
                pl.BlockSpec((tm, kw), lambda i: (i, 0)),
                pl.BlockSpec((tm, IDX_DIM), lambda i: (i, 0)),
                pl.BlockSpec((tm, cw), lambda i: (i, 0))]
    if with_vt:
        kv_shape.append(jax.ShapeDtypeStruct((T // QBLK, kw, QBLK), BF16))
        kv_specs.append(pl.BlockSpec((tm // QBLK, kw, QBLK), lambda i: (i, 0, 0)))
        for _ in range(2):
            kv_shape.append(jax.ShapeDtypeStruct((T, A_KV_HEADS, HEAD_DIM), F32))
            kv_specs.append(pl.BlockSpec((tm, A_KV_HEADS, HEAD_DIM), lambda i: (i, 0, 0)))
        assert tm == QBLK, "the prompt attention takes one query block per projection step"
        for rows, width in ((A_HEADS * tm, HEAD_DIM), (IDX_HEADS * tm, IDX_DIM)):
            kv_shape.append(jax.ShapeDtypeStruct((steps, rows, width), BF16))
            kv_specs.append(pl.BlockSpec((None, rows, width), lambda i: (i, 0, 0)))
    if with_vn:
        kv_shape.append(jax.ShapeDtypeStruct((T, cw), F32))
        kv_specs.append(pl.BlockSpec((tm, cw), lambda i: (i, 0)))
    outs = pl.pallas_call(
        functools.partial(_proj_kernel, n_cast=len(cast), with_vt=with_vt, with_vn=with_vn, seg=seg),
        out_shape=(jax.ShapeDtypeStruct((T, N), F32),) + tuple(kv_shape) + tuple(
            jax.ShapeDtypeStruct(w.shape[1:], BF16) for w in cast),
        grid=(steps,),
        in_specs=[pl.BlockSpec((tm, D), lambda i: (i, 0)),
                  _resident((1, D)),
                  _layer_resident((D, N), layer),
                  _resident((1, HEAD_DIM)),
                  _resident((1, HEAD_DIM)),
                  _resident((1, cw)),
                  _layer_resident((C_GROUPS, TILE, TILE), layer),
                  _layer_resident((TILE, C_GROUPS), layer)] + cast_specs_in,
        out_specs=(pl.BlockSpec((tm, N), lambda i: (i, 0)),) + tuple(kv_specs) + tuple(cast_specs_out),
        compiler_params=_cparams(("parallel",)),
        name="in_projection",
    )(h, g.reshape(1, D), w_packed, k_norm_g.reshape(1, HEAD_DIM), q_norm_g.reshape(1, HEAD_DIM),
      g_v.reshape(1, cw), w_tiles, b_cols, *cast)
    n_kv = len(kv_shape)
    return outs[0], outs[1:1 + n_kv], outs[1 + n_kv:]


def _bucket(dist):
    n = jnp.maximum(dist, 0)
    max_exact = NUM_BUCKETS // 2
    large = max_exact + (jnp.log(jnp.maximum(n, 1).astype(F32) / max_exact)
                         / math.log(MAX_DISTANCE / max_exact)
                         * (NUM_BUCKETS - max_exact)).astype(I32)
    large = jnp.minimum(large, NUM_BUCKETS - 1)
    return jnp.where(n < max_exact, n, large)


def _bias_prompt_kernel(rb_ref, o_ref):
    c = lax.broadcasted_iota(I32, (TILE, TILE), 0)
    t = lax.broadcasted_iota(I32, (TILE, TILE), 1)
    for z in range(3):
        bucket = _bucket(t - c + (2 - z) * TILE)
        for h in range(A_HEADS):
            acc = jnp.zeros((TILE, TILE), F32)
            for b in range(NUM_BUCKETS):
                acc = jnp.where(bucket == b, rb_ref[b, h], acc)
            o_ref[h, z] = acc


def bias_table_prompt(rel_bias):
    return pl.pallas_call(
        _bias_prompt_kernel,
        out_shape=jax.ShapeDtypeStruct((A_HEADS, 3, TILE, TILE), F32),
        in_specs=[pl.BlockSpec(memory_space=pltpu.SMEM)],
        out_specs=pl.BlockSpec(memory_space=pltpu.VMEM),
        name="bias_table_prompt",
    )(rel_bias)


def _bias_sample_kernel(rbrows_ref, o_ref, *, past, n_tok):
    rows, L = o_ref.shape
    r = lax.broadcasted_iota(I32, (rows, L), 0)
    s = lax.broadcasted_iota(I32, (rows, L), 1) // A_KV_HEADS
    bucket = _bucket(past + r // A_HEADS - s)
    rbrows = rbrows_ref[...]
    acc = jnp.zeros((rows, L), F32)
    for b in range(NUM_BUCKETS):
        acc = jnp.where(bucket == b, rbrows[:, b:b + 1], acc)
    o_ref[...] = acc


def bias_table_sample(rel_bias, past, n_tok, L):
    rows = n_tok * A_HEADS
    rbrows = jnp.tile(rel_bias.T, (n_tok, 1))
    return pl.pallas_call(
        functools.partial(_bias_sample_kernel, past=past, n_tok=n_tok),
        out_shape=jax.ShapeDtypeStruct((rows, L), F32),
        name="bias_table_sample",
    )(rbrows)


def _sortable_key(x):
    b = lax.bitcast_convert_type(x, I32)
    return b ^ ((b >> 31) & 0x7FFFFFFF)


def _topk_member(skey_ref, k_sel):
    R, L = skey_ref.shape

    def body(it, ans):
        bit = 31 - it
        cand = ans | lax.shift_left(jnp.int32(1), bit)
        cand_s = cand ^ INT_MIN
        cnt = jnp.sum(jnp.where(skey_ref[...] >= cand_s, 1.0, 0.0), axis=-1, keepdims=True)
        return jnp.where(cnt >= k_sel, cand, ans)

    ans = lax.fori_loop(0, 32, body, jnp.zeros((R, 1), I32))
    tau = ans ^ INT_MIN
    skey = skey_ref[...]
    gt = skey > tau
    eq = skey == tau
    n_gt = jnp.sum(jnp.where(gt, 1.0, 0.0), axis=-1, keepdims=True)
    room = k_sel - n_gt
    r_i = lax.broadcasted_iota(I32, (LANES, LANES), 0)
    c_i = lax.broadcasted_iota(I32, (LANES, LANES), 1)
    upper = jnp.where(r_i <= c_i, 1.0, 0.0).astype(BF16)
    off = jnp.zeros((R, 1), F32)
    parts = []
    for j in range(L // LANES):
        sl = slice(j * LANES, (j + 1) * LANES)
        eq_j = eq[:, sl]
        run = jnp.dot(jnp.where(eq_j, 1.0, 0.0).astype(BF16), upper, preferred_element_type=F32) + off
        parts.append(gt[:, sl] | (eq_j & (run <= room)))
        off = run[:, LANES - 1:LANES]
    return jnp.concatenate(parts, axis=1)


def _fold8(x, op):
    return op(x.reshape(x.shape[0] // SUBLANES, SUBLANES, x.shape[1]), axis=0)


def _dsa_prompt_kernel(qn_ref, qst_ref, misc_ref, kn_ref, vt_ref, bias_ref, o_ref,
                       skey_ref, madd_ref, lg_ref, acc_ref, *, k_sel):
    i = pl.program_id(1)
    nkb = i + 1
    sub = QBLK // TILE
    rep = A_HEADS // A_KV_HEADS
    row0 = pl.multiple_of(i * QBLK, QBLK)
    s_iota = lax.broadcasted_iota(I32, (QBLK, QBLK), 0)
    t_iota = lax.broadcasted_iota(I32, (QBLK, QBLK), 1)

    def admissible(j):
        return (j * QBLK + s_iota) <= (row0 + t_iota)

    wi_t = misc_ref[pl.ds(row0, QBLK), :].T[MISC_WI:MISC_WI + IDX_HEADS, :]
    wi_t = wi_t * (IDX_HEADS ** -0.5 * IDX_DIM ** -0.5)

    def score_body(j, carry):
        k0 = pl.multiple_of(j * QBLK, QBLK)
        kj = misc_ref[pl.ds(k0, QBLK), MISC_KI:MISC_KI + IDX_DIM].astype(BF16)
        s = lax.dot_general(kj, qst_ref[...], (((1,), (1,)), ((), ())), preferred_element_type=F32)
        score = jnp.zeros((QBLK, QBLK), F32)
        for h in range(IDX_HEADS):
            score = score + jnp.maximum(s[:, h * QBLK:(h + 1) * QBLK], 0.0) * wi_t[h:h + 1, :]
        skey_ref[j] = _sortable_key(jnp.where(admissible(j), score, -jnp.inf))
        return carry

    lax.fori_loop(0, nkb // 2, lambda jj, c: score_body(2 * jj + 1, score_body(2 * jj, c)), 0)

    @pl.when(nkb % 2 == 1)
    def _():
        score_body(nkb - 1, 0)

    def count(pred_fn):
        def hits(j):
            return _fold8(jnp.where(pred_fn(skey_ref[j]), 1.0, 0.0), jnp.sum)

        def body(jj, accs):
            return accs[0] + hits(2 * jj), accs[1] + hits(2 * jj + 1)

        zero = jnp.zeros((SUBLANES, QBLK), F32)
        acc0, acc1 = lax.fori_loop(0, nkb // 2, body, (zero, zero))
        acc = lax.cond(nkb % 2 == 1, lambda: acc0 + acc1 + hits(nkb - 1), lambda: acc0 + acc1)
        return jnp.sum(acc, axis=0, keepdims=True)

    def bit_body(it, ans):
        cand = ans | lax.shift_left(jnp.int32(1), 31 - it)
        cand_s = cand ^ INT_MIN
        cnt = count(lambda key: key >= cand_s)
        return jnp.where(cnt >= k_sel, cand, ans)

    ans = lax.fori_loop(0, 32, bit_body, jnp.zeros((1, QBLK), I32))
    tau = ans ^ INT_MIN
    n_ge = count(lambda key: key >= tau)
    excess = jnp.max(jnp.where((n_ge > k_sel) & (tau != NEG_INF_KEY), 1.0, 0.0))

    @pl.when(excess == 0.0)
    def _():
        def mask_body(j, carry):
            madd_ref[j] = jnp.where((skey_ref[j] >= tau) & admissible(j), 0.0, NEG_BIG)
            return carry

        lax.fori_loop(0, nkb, mask_body, 0)

    @pl.when(excess > 0.0)
    def _():
        room = k_sel - count(lambda key: key > tau)
        lower = jnp.where(t_iota <= s_iota, 1.0, 0.0).astype(BF16)

        def mask_body(j, off):
            key = skey_ref[j]
            eq = key == tau
            run = jnp.dot(lower, jnp.where(eq, 1.0, 0.0).astype(BF16), preferred_element_type=F32) + off
            sel = ((key > tau) | (eq & (run <= room))) & admissible(j)
            madd_ref[j] = jnp.where(sel, 0.0, NEG_BIG)
            return run[QBLK - 1:QBLK, :]

        lax.fori_loop(0, nkb, mask_body, jnp.zeros((1, QBLK), F32))

    wide = rep * QBLK
    for g in range(A_KV_HEADS):
        gs = slice(g * HEAD_DIM, (g + 1) * HEAD_DIM)
        heads = list(range(g * rep, (g + 1) * rep))
        q_stack = qn_ref[g * wide:(g + 1) * wide, :]

        def logit_body(j, mx):
            k0 = pl.multiple_of(j * QBLK, QBLK)
            kj = kn_ref[pl.ds(k0, QBLK), gs].astype(BF16)
            lg = lax.dot_general(kj, q_stack, (((1,), (1,)), ((), ())), preferred_element_type=F32)
            madd = madd_ref[j]
            parts = []
            for r, h in enumerate(heads):
                quads = []
                for c in range(sub):
                    quads.append(jnp.concatenate(
                        [bias_ref[h, jnp.clip(2 - ((i - j) * sub + u - c), 0, 2)] for u in range(sub)], axis=1))
                parts.append(lg[:, r * QBLK:(r + 1) * QBLK] + jnp.concatenate(quads, axis=0) + madd)
            lg = jnp.concatenate(parts, axis=1)
            lg_ref[j] = lg
            return jnp.maximum(mx, _fold8(lg, jnp.max))

        mx = lax.fori_loop(0, nkb // 2, lambda jj, mx_: logit_body(2 * jj + 1, logit_body(2 * jj, mx_)),
                           jnp.full((SUBLANES, wide), NEG_BIG, F32))
        mx = lax.cond(nkb % 2 == 1, lambda mx_: logit_body(nkb - 1, mx_), lambda mx_: mx_, mx)
        m = jnp.max(mx, axis=0, keepdims=True)
        acc_ref[...] = jnp.zeros(acc_ref.shape, F32)

        def pv_body(blocks, sm):
            ps = [jnp.exp(lg_ref[j] - m) for j in blocks]
            acc_ref[...] += sum(jnp.dot(vt_ref[j, gs, :], p.astype(BF16), preferred_element_type=F32)
                                for j, p in zip(blocks, ps))
            return sm + sum(_fold8(p, jnp.sum) for p in ps)

        sm = lax.fori_loop(0, nkb // 2, lambda jj, sm_: pv_body((2 * jj, 2 * jj + 1), sm_),
                           jnp.zeros((SUBLANES, wide), F32))
        sm = lax.cond(nkb % 2 == 1, lambda sm_: pv_body((nkb - 1,), sm_), lambda sm_: sm_, sm)
        den = jnp.sum(sm, axis=0, keepdims=True)
        o = (acc_ref[...] / den).T
        for r, h in enumerate(heads):
            o_ref[:, h * HEAD_DIM:(h + 1) * HEAD_DIM] = o[r * QBLK:(r + 1) * QBLK, :].astype(o_ref.dtype)


def dsa_prompt(proj, qn, qst, kn, vt, bias_tab, n_batch, seq):
    T = proj.shape[0]
    nb = seq // QBLK
    k_sel = min(TOPK_MAX, seq // 4)
    aw = A_HEADS * HEAD_DIM
    kw = A_KV_HEADS * HEAD_DIM
    rep = A_HEADS // A_KV_HEADS
    return pl.pallas_call(
        functools.partial(_dsa_prompt_kernel, k_sel=k_sel),
        out_shape=jax.ShapeDtypeStruct((T, aw), BF16),
        grid=(n_batch, nb),
        in_specs=[pl.BlockSpec((None,) + qn.shape[1:], lambda b, i: (b * nb + i, 0, 0)),
                  pl.BlockSpec((None,) + qst.shape[1:], lambda b, i: (b * nb + i, 0, 0)),
                  pl.BlockSpec((seq, LANES), lambda b, i: (b, MISC_OFF // LANES)),
                  pl.BlockSpec((seq, kw), lambda b, i: (b, 0)),
                  pl.BlockSpec((nb, kw, QBLK), lambda b, i: (b, 0, 0)),
                  _resident((A_HEADS, 3, TILE, TILE))],
        out_specs=pl.BlockSpec((QBLK, aw), lambda b, i: (b * nb + i, 0)),
        scratch_shapes=[pltpu.VMEM((nb, QBLK, QBLK), I32),
                        pltpu.VMEM((nb, QBLK, QBLK), F32),
                        pltpu.VMEM((nb, QBLK, rep * QBLK), F32),
                        pltpu.VMEM((HEAD_DIM, rep * QBLK), F32)],
        compiler_params=_cparams(("parallel", "arbitrary")),
        name="dsa_prompt",
    )(qn, qst, proj, kn, vt, bias_tab)


def _dsa_sample_select_kernel(pt_ref, *refs, n_pages, n_tok, k_sel, rows_pad):
    del pt_ref
    per = rows_pad // n_tok
    page_refs = refs[:per * n_pages]
    qi_ref, wm_ref, kin_ref, mask_ref, sc_ref, skey_ref = refs[per * n_pages:]
    b = pl.program_id(0)
    nb = pl.num_programs(0)
    L = sc_ref.shape[1]
    past = n_pages * PAGE_SIZE

    relu_s = []
    for e in range(per):
        kt_all = jnp.concatenate([r[...].astype(BF16) for r in page_refs[e * n_pages:(e + 1) * n_pages]]
                                 + [kin_ref[e].astype(BF16)], axis=1)
        relu_s.append(jnp.maximum(jnp.dot(qi_ref[e].astype(BF16), kt_all, preferred_element_type=F32), 0.0))
    relu_cat = jnp.concatenate(relu_s, axis=0)
    wm = wm_ref[0]
    r_hi, w_hi = relu_cat.astype(BF16), wm.astype(BF16)
    r_lo, w_lo = (relu_cat - r_hi.astype(F32)).astype(BF16), (wm - w_hi.astype(F32)).astype(BF16)
    score = (jnp.dot(w_hi, r_hi, preferred_element_type=F32) + jnp.dot(w_hi, r_lo, preferred_element_type=F32)
             + jnp.dot(w_lo, r_hi, preferred_element_type=F32))
    r0 = pl.multiple_of(b * rows_pad, rows_pad)
    sc_ref[pl.ds(r0, rows_pad), :] = score

    @pl.when(b == nb - 1)
    def _():
        n_blocks = sc_ref.shape[0] // TILE
        n_tiles = L // PAGE_SIZE
        tp = past + lax.broadcasted_iota(I32, (TILE, L), 0) % n_tok
        sp = lax.broadcasted_iota(I32, (TILE, L), 1)
        adm_blk = sp <= tp
        d_r = lax.broadcasted_iota(I32, (PAGE_SIZE, PAGE_SIZE * A_KV_HEADS), 0)
        d_c = lax.broadcasted_iota(I32, (PAGE_SIZE, PAGE_SIZE * A_KV_HEADS), 1)
        dup = jnp.where(d_c // A_KV_HEADS == d_r, 1.0, 0.0).astype(BF16)
        for rb in range(n_blocks):
            rows = slice(rb * TILE, (rb + 1) * TILE)
            skey_ref[...] = _sortable_key(jnp.where(adm_blk, sc_ref[rows, :], -jnp.inf))
            sel = jnp.where(_topk_member(skey_ref, k_sel) & adm_blk, 1.0, 0.0).astype(BF16)
            stacked = jnp.concatenate([sel[:, j * PAGE_SIZE:(j + 1) * PAGE_SIZE] for j in range(n_tiles)], axis=0)
            stacked = jnp.dot(stacked, dup, preferred_element_type=F32)
            mask_ref[rows, :] = jnp.concatenate(
                [stacked[j * TILE:(j + 1) * TILE, :] for j in range(n_tiles)], axis=1)


def dsa_sample_select(cache_ik_t, layer, page_table, qi_rows, wmat, ki_new_t, n_tok, k_sel):
    DB, n_pages = page_table.shape
    rows_pad = wmat.shape[1]
    per = rows_pad // n_tok
    n_rows = DB // per * rows_pad
    L = (n_pages + 1) * PAGE_SIZE
    page_specs = [pl.BlockSpec((None, None, IDX_DIM, PAGE_SIZE), functools.partial(
        lambda b, pt, e, p: (layer, pt[b * per + e, p], 0, 0), e=e, p=p))
        for e in range(per) for p in range(n_pages)]
    grid_spec = pltpu.PrefetchScalarGridSpec(
        num_scalar_prefetch=1,
        grid=(DB // per,),
        in_specs=page_specs + [
            pl.BlockSpec((per,) + qi_rows.shape[1:], lambda b, pt: (b, 0, 0)),
            pl.BlockSpec((1,) + wmat.shape[1:], lambda b, pt: (b, 0, 0)),
            pl.BlockSpec((per, IDX_DIM, PAGE_SIZE), lambda b, pt: (b, 0, 0))],
        out_specs=pl.BlockSpec((n_rows, A_KV_HEADS * L), lambda b, pt: (0, 0)),
        scratch_shapes=[pltpu.VMEM((n_rows, L), F32),
                        pltpu.VMEM((TILE, L), I32)],
    )
    return pl.pallas_call(
        functools.partial(_dsa_sample_select_kernel, n_pages=n_pages, n_tok=n_tok, k_sel=k_sel,
                          rows_pad=rows_pad),
        out_shape=jax.ShapeDtypeStruct((n_rows, A_KV_HEADS * L), F32),
        grid_spec=grid_spec,
        compiler_params=_cparams(("arbitrary",)),
        name="dsa_sample_select",
    )(page_table, *([cache_ik_t] * (per * n_pages)), qi_rows, wmat, ki_new_t)


def _dsa_sample_attend_kernel(pt_ref, *refs, n_pages, n_tok, rows_pad):
    del pt_ref
    per = rows_pad // n_tok
    k_refs = refs[:per * n_pages]
    v_refs = refs[per * n_pages:2 * per * n_pages]
    q_ref, kn_ref, vn_ref, mask_ref, bias_ref, qg_ref, o_ref = refs[2 * per * n_pages:]
    rows = n_tok * A_HEADS
    page_rows = PAGE_SIZE * A_KV_HEADS
    n_cols = mask_ref.shape[1]
    pad = jnp.zeros((page_rows - n_tok * A_KV_HEADS, HEAD_DIM), BF16)
    rep = A_HEADS // A_KV_HEADS
    grp = (lax.broadcasted_iota(I32, (rows, 1), 0) % A_HEADS) // rep
    own_group = (lax.broadcasted_iota(I32, (rows, n_cols), 1) % A_KV_HEADS) == grp
    member = mask_ref[...]

    def tiles(page_refs, new_ref, e):
        new = jnp.concatenate([new_ref[e].astype(BF16), pad], axis=0)
        return [r[...].astype(BF16) for r in page_refs[e * n_pages:(e + 1) * n_pages]] + [new]

    els = range(per)
    qs = [(_rms(q_ref[e], qg_ref[...]) * HEAD_DIM ** -0.5).astype(BF16) for e in els]
    valid = [(jnp.concatenate(
        [jnp.broadcast_to(member[e * n_tok + t:e * n_tok + t + 1, :], (A_HEADS, n_cols)) for t in range(n_tok)],
        axis=0) > 0.5) & own_group for e in els]
    logits = [jnp.concatenate(
        [lax.dot_general(qs[e], kt, (((1,), (1,)), ((), ())), preferred_element_type=F32)
         for kt in tiles(k_refs, kn_ref, e)], axis=1) for e in els]
    logits = [jnp.where(valid[e], logits[e] + bias_ref[...], NEG_BIG) for e in els]
    ms = [jnp.max(logits[e], axis=-1, keepdims=True) for e in els]
    ps = [jnp.exp(logits[e] - ms[e]) for e in els]
    dens = [jnp.sum(ps[e], axis=-1, keepdims=True) for e in els]
    pbs = [ps[e].astype(BF16) for e in els]
    outs = [jnp.zeros((rows, HEAD_DIM), F32) for _ in els]
    vts = [tiles(v_refs, vn_ref, e) for e in els]
    for j in range(len(vts[0])):
        for e in els:
            outs[e] = outs[e] + jnp.dot(pbs[e][:, j * page_rows:(j + 1) * page_rows], vts[e][j],
                                        preferred_element_type=F32)
    for e in els:
        o_ref[e] = (outs[e] / dens[e]).astype(o_ref.dtype)


def dsa_sample_attend(cache_k, cache_v, layer, page_table, q_rows, k_new, v_new, mask, bias_tab, q_norm_g):
    DB, n_pages = page_table.shape
    n_tok = k_new.shape[1] // A_KV_HEADS
    rows = n_tok * A_HEADS
    rows_pad = SUBLANES
    per = rows_pad // n_tok
    n_cols = mask.shape[1]
    page_rows = PAGE_SIZE * A_KV_HEADS
    page_specs = [pl.BlockSpec((None, None, page_rows, HEAD_DIM), functools.partial(
        lambda b, pt, e, p: (layer, pt[b * per + e, p], 0, 0), e=e, p=p))
        for e in range(per) for p in range(n_pages)]
    grid_spec = pltpu.PrefetchScalarGridSpec(
        num_scalar_prefetch=1,
        grid=(DB // per,),
        in_specs=page_specs + page_specs + [
            pl.BlockSpec((per, rows, HEAD_DIM), lambda b, pt: (b, 0, 0)),
            pl.BlockSpec((per, n_tok * A_KV_HEADS, HEAD_DIM), lambda b, pt: (b, 0, 0)),
            pl.BlockSpec((per, n_tok * A_KV_HEADS, HEAD_DIM), lambda b, pt: (b, 0, 0)),
            pl.BlockSpec((rows_pad, n_cols), lambda b, pt: (b, 0)),
            pl.BlockSpec((rows, n_cols), lambda b, pt: (0, 0), pipeline_mode=pl.Buffered(1)),
            pl.BlockSpec((1, HEAD_DIM), lambda b, pt: (0, 0), pipeline_mode=pl.Buffered(1))],
        out_specs=pl.BlockSpec((per, rows, HEAD_DIM), lambda b, pt: (b, 0, 0)),
    )
    return pl.pallas_call(
        functools.partial(_dsa_sample_attend_kernel, n_pages=n_pages, n_tok=n_tok, rows_pad=rows_pad),
        out_shape=jax.ShapeDtypeStruct((DB, rows, HEAD_DIM), BF16),
        grid_spec=grid_spec,
        compiler_params=_cparams(("parallel",)),
        name="dsa_sample_attend",
    )(page_table, *([cache_k] * (per * n_pages)), *([cache_v] * (per * n_pages)), q_rows, k_new, v_new, mask,
      bias_tab, q_norm_g.reshape(1, HEAD_DIM))


def _log_sigmoid(z):
    return jnp.minimum(z, 0.0) - jnp.log(1.0 + jnp.exp(-jnp.abs(z)))


def _seg_masks(seg):
    r = lax.broadcasted_iota(I32, (TILE, TILE), 0)
    c = lax.broadcasted_iota(I32, (TILE, TILE), 1)
    return r, c, (r // seg) == (c // seg)


def _gla_levels(seg):
    w, out = seg // 2, []
    while w >= 1:
        out.append(w)
        w //= 2
    return out


def _gla_sum_matrices(seg):
    r = jnp.arange(TILE)[:, None]
    c = jnp.arange(TILE)[None, :]
    mats = []
    for w in _gla_levels(seg):
        same = (r // (2 * w)) == (c // (2 * w))
        r_right = (r % (2 * w)) >= w
        c_right = (c % (2 * w)) >= w
        mats.append(same & r_right & c_right & (c <= r))
    for w in _gla_levels(seg):
        same = (r // (2 * w)) == (c // (2 * w))
        r_right = (r % (2 * w)) >= w
        c_right = (c % (2 * w)) >= w
        mats.append(same & (~r_right) & (~c_right) & (c > r))
    same_seg = (r // seg) == (c // seg)
    mats.append(same_seg & (c <= r))
    mats.append(same_seg & (c > r))
    return jnp.concatenate(mats, axis=0).astype(BF16)


def _bdot(a, b):
    return jnp.dot(a.astype(BF16), b.astype(BF16), preferred_element_type=F32)


def _bdot_nt(a, b):
    return lax.dot_general(a.astype(BF16), b.astype(BF16), (((1,), (1,)), ((), ())), preferred_element_type=F32)


def _gla_common(tiles, wg_ref, bg_ref, mats_ref, seg):
    n = len(tiles)
    kw = B_HEADS * B_DK
    las = []
    for _, _, misc_ref in tiles:
        gb = misc_ref[:, MISC_GB:MISC_GB + GATE_RANK]
        z = jnp.dot(gb, wg_ref[...], precision=HIGHEST, preferred_element_type=F32) + bg_ref[...]
        las.append(_log_sigmoid(z) / GATE_TEMP)
    la = jnp.concatenate(las, axis=1)
    la_hi = la.astype(BF16)
    la_lo = (la - la_hi.astype(F32)).astype(BF16)
    mats = mats_ref[...]
    sums = (jnp.dot(mats, la_hi, preferred_element_type=F32) + jnp.dot(mats, la_lo, preferred_element_type=F32))
    levels = _gla_levels(seg)
    nl = len(levels)
    qs = [qb_ref[...] * B_DK ** -0.5 for qb_ref, _, _ in tiles]
    ks = [kb_ref[...] for _, kb_ref, _ in tiles]
    r, c, _ = _seg_masks(seg)
    atts = [[jnp.where(r == c, _bdot_nt(qs[i][:, h * B_DK:(h + 1) * B_DK], ks[i][:, h * B_DK:(h + 1) * B_DK]), 0.0)
             for h in range(B_HEADS)] for i in range(n)]
    for li, w in enumerate(levels):
        pair = ((r // (2 * w)) == (c // (2 * w))) & ((r % (2 * w)) >= w) & ((c % (2 * w)) < w)
        qd = [(qs[i] * jnp.exp(sums[li * TILE:(li + 1) * TILE, i * kw:(i + 1) * kw])).astype(BF16) for i in range(n)]
        kd = [(ks[i] * jnp.exp(sums[(nl + li) * TILE:(nl + li + 1) * TILE, i * kw:(i + 1) * kw])).astype(BF16)
              for i in range(n)]
        for h in range(B_HEADS):
            hs = slice(h * B_DK, (h + 1) * B_DK)
            for i in range(n):
                atts[i][h] = atts[i][h] + jnp.where(pair, _bdot_nt(qd[i][:, hs], kd[i][:, hs]), 0.0)
    out = []
    for i in range(n):
        cs = slice(i * kw, (i + 1) * kw)
        b_cum = sums[2 * nl * TILE:(2 * nl + 1) * TILE, cs]
        rem = sums[(2 * nl + 1) * TILE:(2 * nl + 2) * TILE, cs]
        out.append((qs[i], ks[i], atts[i], b_cum, rem))
    return out


def _gla_finish(o_heads, rb_ref, go_ref, o_ref):
    go = go_ref[...]
    for h in range(B_HEADS):
        vs = slice(h * B_DV, (h + 1) * B_DV)
        rb = rb_ref[:, vs]
        o_ref[:, vs] = (_rms(o_heads[h], go) * (rb * jax.nn.sigmoid(rb))).astype(o_ref.dtype)


def _gla_prompt_kernel(qb_ref, kb_ref, vb_ref, rb_ref, misc_ref, wg_ref, bg_ref, go_ref, mats_ref,
                       o_ref, s_ref, state_ref):
    ci = pl.program_id(0)

    @pl.when(ci == 0)
    def _():
        state_ref[...] = jnp.zeros_like(state_ref)

    nb = qb_ref.shape[0]
    common = _gla_common([(qb_ref.at[b], kb_ref.at[b], misc_ref.at[b]) for b in range(nb)],
                         wg_ref, bg_ref, mats_ref, TILE)
    vals = [vb_ref[b] for b in range(nb)]
    states = [state_ref[b] for b in range(nb)]
    qes = [common[b][0] * jnp.exp(common[b][3]) for b in range(nb)]
    o_heads = [[] for _ in range(nb)]
    for h in range(B_HEADS):
        ks = slice(h * B_DK, (h + 1) * B_DK)
        vs = slice(h * B_DV, (h + 1) * B_DV)
        for b in range(nb):
            o_heads[b].append(_bdot(qes[b][:, ks], states[b][ks, :]) + _bdot(common[b][2][h], vals[b][:, vs]))
    for b in range(nb):
        _gla_finish(o_heads[b], rb_ref.at[b], go_ref, o_ref.at[b])

    ke_ts = [(common[b][1] * jnp.exp(common[b][4])).T for b in range(nb)]
    e_cols = [jnp.broadcast_to(jnp.exp(common[b][3][TILE - 1:TILE, :]), (TILE, B_HEADS * B_DK)).T[:, 0:1]
              for b in range(nb)]
    for b in range(nb):
        upd = jnp.concatenate(
            [_bdot(ke_ts[b][h * B_DK:(h + 1) * B_DK, :], vals[b][:, h * B_DV:(h + 1) * B_DV])
             for h in range(B_HEADS)], axis=0)
        new_state = states[b] * e_cols[b] + upd
        state_ref[b] = new_state
        s_ref[b] = new_state


def gla_prompt(proj, w_gate, b_gate, g_out, n_batch, seq):
    nc = seq // TILE
    kwid = B_HEADS * B_DK
    vwid = B_HEADS * B_DV
    mats = _gla_sum_matrices(TILE)
    proj3 = proj.reshape(n_batch, seq, proj.shape[1])
    o, s = pl.pallas_call(
        _gla_prompt_kernel,
        out_shape=(jax.ShapeDtypeStruct((n_batch, seq, vwid), BF16),
                   jax.ShapeDtypeStruct((n_batch, kwid, B_DV), F32)),
        grid=(nc,),
        in_specs=[pl.BlockSpec((n_batch, TILE, kwid), lambda c: (0, c, QB_OFF // kwid)),
                  pl.BlockSpec((n_batch, TILE, kwid), lambda c: (0, c, KB_OFF // kwid)),
                  pl.BlockSpec((n_batch, TILE, vwid), lambda c: (0, c, VB_OFF // vwid)),
                  pl.BlockSpec((n_batch, TILE, vwid), lambda c: (0, c, RB_OFF // vwid)),
                  pl.BlockSpec((n_batch, TILE, LANES), lambda c: (0, c, MISC_OFF // LANES)),
                  _resident((GATE_RANK, kwid)),
                  _resident((1, kwid)),
                  _resident((1, B_DV)),
                  _resident(mats.shape)],
        out_specs=(pl.BlockSpec((n_batch, TILE, vwid), lambda c: (0, c, 0)),
                   pl.BlockSpec((n_batch, kwid, B_DV), lambda c: (0, 0, 0))),
        scratch_shapes=[pltpu.VMEM((n_batch, kwid, B_DV), F32)],
        compiler_params=_cparams(("arbitrary",)),
        name="gla_prompt",
    )(proj3, proj3, proj3, proj3, proj3, w_gate, b_gate.reshape(1, kwid), g_out.reshape(1, B_DV), mats)
    return o.reshape(n_batch * seq, vwid), s.reshape(n_batch, B_HEADS, B_DK, B_DV)


def _gla_sample_kernel(qb_ref, kb_ref, vb_ref, rb_ref, misc_ref, wg_ref, bg_ref, go_ref, mats_ref, s0_ref,
                       o_ref, s_ref, *, seg):
    nbt = TILE // seg
    (q, k, att, b_cum, rem), = _gla_common([(qb_ref, kb_ref, misc_ref)], wg_ref, bg_ref, mats_ref, seg)
    v = vb_ref[...]
    qe = q * jnp.exp(b_cum)
    ke = k * jnp.exp(rem)
    r1 = lax.broadcasted_iota(I32, (TILE, 1), 0)
    e_last = jnp.where(r1 % seg == seg - 1, jnp.exp(b_cum), 0.0)
    wide = nbt * B_DK
    mq = (lax.broadcasted_iota(I32, (TILE, wide), 0) // seg) == (lax.broadcasted_iota(I32, (TILE, wide), 1) // B_DK)
    mk = (lax.broadcasted_iota(I32, (wide, TILE), 0) // B_DK) == (lax.broadcasted_iota(I32, (wide, TILE), 1) // seg)
    o_heads = []
    for h in range(B_HEADS):
        ks = slice(h * B_DK, (h + 1) * B_DK)
        vs = slice(h * B_DV, (h + 1) * B_DV)
        state = s0_ref[:, h].reshape(wide, B_DV)
        q_bd = jnp.where(mq, jnp.concatenate([qe[:, ks]] * nbt, axis=1), 0.0)
        o_heads.append(_bdot(q_bd, state) + _bdot(att[h], v[:, vs]))
        pair_t = jnp.concatenate([ke[:, ks], e_last[:, ks]], axis=1).T
        k_bd = jnp.where(mk, jnp.concatenate([pair_t[:B_DK]] * nbt, axis=0), 0.0)
        e_bd = jnp.where(mk, jnp.concatenate([pair_t[B_DK:]] * nbt, axis=0), 0.0)
        e_col = jnp.sum(e_bd, axis=-1, keepdims=True)
        new_state = state * e_col + _bdot(k_bd, v[:, vs])
        s_ref[:, h] = new_state.reshape(nbt, B_DK, B_DV)
    _gla_finish(o_heads, rb_ref, go_ref, o_ref)


def gla_sample(proj, w_gate, b_gate, g_out, s0, layer, n_tok):
    T = proj.shape[0]
    nbt = TILE // n_tok
    kwid = B_HEADS * B_DK
    vwid = B_HEADS * B_DV
    mats = _gla_sum_matrices(n_tok)
    return pl.pallas_call(
        functools.partial(_gla_sample_kernel, seg=n_tok),
        out_shape=(jax.ShapeDtypeStruct((T, vwid), BF16),
                   jax.ShapeDtypeStruct(s0.shape[1:], F32)),
        grid=(T // TILE,),
        in_specs=[pl.BlockSpec((TILE, kwid), lambda i: (i, QB_OFF // kwid)),
                  pl.BlockSpec((TILE, kwid), lambda i: (i, KB_OFF // kwid)),
                  pl.BlockSpec((TILE, vwid), lambda i: (i, VB_OFF // vwid)),
                  pl.BlockSpec((TILE, vwid), lambda i: (i, RB_OFF // vwid)),
                  pl.BlockSpec((TILE, LANES), lambda i: (i, MISC_OFF // LANES)),
                  _resident((GATE_RANK, kwid)),
                  _resident((1, kwid)),
                  _resident((1, B_DV)),
                  _resident(mats.shape),
                  pl.BlockSpec((None, nbt, B_HEADS, B_DK, B_DV), lambda i: (layer, i, 0, 0, 0))],
        out_specs=(pl.BlockSpec((TILE, vwid), lambda i: (i, 0)),
                   pl.BlockSpec((nbt, B_HEADS, B_DK, B_DV), lambda i: (i, 0, 0, 0))),
        compiler_params=_cparams(("parallel",)),
        name="gla_sample",
    )(proj, proj, proj, proj, proj, w_gate, b_gate.reshape(1, kwid), g_out.reshape(1, B_DV), mats, s0)


def _gelu(x):
    return jax.nn.gelu(x)


def _gmlp_rows(uc_ref, vc_ref, gv_ref, ws_ref, bcol_ref, o_ref, vn_ref, seg):
    r, c, same_seg = _seg_masks(seg)
    keep = same_seg & (c <= r)
    for t in range(uc_ref.shape[0] // TILE):
        rows = slice(t * TILE, (t + 1) * TILE)
        u = _gelu(uc_ref[rows, :])
        vg = _gelu(vc_ref[rows, :])
        for g in range(C_GROUPS):
            gs = slice(g * C_GROUP_DIM, (g + 1) * C_GROUP_DIM)
            vn = _rms(vg[:, gs], gv_ref[:, gs])
            if vn_ref is not None:
                vn_ref[rows, gs] = vn
            w = jnp.where(keep, ws_ref[g], 0.0).astype(BF16)
            s = jnp.dot(w, vn.astype(BF16), preferred_element_type=F32) + bcol_ref[:, g:g + 1]
            o_ref[rows, gs] = (u[:, gs] * s).astype(o_ref.dtype)


def _ffn_kernel(h_ref, oa_ref, ob_ref, oc_ref, wo_ref, g_ref, wg_ref, wu_ref, wd_ref, o_ref, n_ref):
    j = pl.program_id(1)

    @pl.when(j == 0)
    def _():
        aw = oa_ref.shape[1]
        bw = ob_ref.shape[1]
        h = h_ref[...] + jnp.dot(oa_ref[...], wo_ref[0:aw, :], preferred_element_type=F32)
        h = h + jnp.dot(ob_ref[...], wo_ref[aw:aw + bw, :], preferred_element_type=F32)
        h = h + jnp.dot(oc_ref[...], wo_ref[aw + bw:, :], preferred_element_type=F32)
        n_ref[...] = _rms(h, g_ref[...]).astype(BF16)
        o_ref[...] = h

    n = n_ref[...]
    a = jnp.dot(n, wg_ref[...], preferred_element_type=F32)
    u = jnp.dot(n, wu_ref[...], preferred_element_type=F32)
    act = (a * jax.nn.sigmoid(a) * u).astype(BF16)
    o_ref[...] += jnp.dot(act, wd_ref[...], preferred_element_type=F32)


def out_proj_ffn(h, o_a, o_b, o_c, w_out, g, w_gate, w_up, w_down, tm, tf):
    T, D = h.shape
    tm = min(tm, T)
    FF = w_gate.shape[1]
    return pl.pallas_call(
        _ffn_kernel,
        out_shape=jax.ShapeDtypeStruct((T, D), F32),
        grid=(T // tm, FF // tf),
        in_specs=[pl.BlockSpec((tm, D), lambda i, j: (i, 0)),
                  pl.BlockSpec((tm, o_a.shape[1]), lambda i, j: (i, 0)),
                  pl.BlockSpec((tm, o_b.shape[1]), lambda i, j: (i, 0)),
                  pl.BlockSpec((tm, o_c.shape[1]), lambda i, j: (i, 0)),
                  _resident(w_out.shape),
                  _resident((1, D)),
                  pl.BlockSpec((D, tf), lambda i, j: (0, j)),
                  pl.BlockSpec((D, tf), lambda i, j: (0, j)),
                  pl.BlockSpec((tf, D), lambda i, j: (j, 0))],
        out_specs=pl.BlockSpec((tm, D), lambda i, j: (i, 0)),
        scratch_shapes=[pltpu.VMEM((tm, D), BF16)],
        compiler_params=_cparams(("parallel", "arbitrary")),
        name="out_proj_ffn",
    )(h, o_a, o_b, o_c, w_out, g.reshape(1, D), w_gate, w_up, w_down)


def _ple_kernel(h_ref, p_ref, g_ref, wgate_ref, wproj_ref, o_ref):
    h = h_ref[...]
    n = _rms(h, g_ref[...]).astype(BF16)
    gate = jax.nn.sigmoid(jnp.dot(n, wgate_ref[...], preferred_element_type=F32))
    emb = jnp.dot(p_ref[...].astype(BF16), wproj_ref[...], preferred_element_type=F32)
    o_ref[...] = h + gate * emb


def ple(h, p, g, w_gate, w_proj, layer, tm):
    T, D = h.shape
    tm = min(tm, T)
    P = p.shape[2]
    return pl.pallas_call(
        _ple_kernel,
        out_shape=jax.ShapeDtypeStruct((T, D), F32),
        grid=(T // tm,),
        in_specs=[pl.BlockSpec((tm, D), lambda i: (i, 0)),
                  pl.BlockSpec((None, tm, P), lambda i: (layer, i, 0)),
                  _resident((1, D)),
                  _resident(w_gate.shape),
                  _resident(w_proj.shape)],
        out_specs=pl.BlockSpec((tm, D), lambda i: (i, 0)),
        compiler_params=_cparams(("parallel",)),
        name="ple",
    )(h, p, g.reshape(1, D), w_gate, w_proj)


_W_IN_SEGMENTS = (("q", 1024), ("k", 256), ("v", 256), ("qi", 1024), ("ki", 64), ("wi", 16), ("qb", 256),
                  ("kb", 256), ("vb", 512), ("gb", 16), ("rb", 512), ("uc", 512), ("vc", 512))
_W_IN_PACKED_ORDER = ("q", "qi", "vb", "rb", "uc", "vc", "k", "v", "qb", "kb", "ki", "wi", "gb")


def _pack_kernel(wt_ref, o_ref):
    src, start = {}, 0
    for name, size in _W_IN_SEGMENTS:
        src[name] = (start, size)
        start += size
    dst = 0
    small = []
    for name in _W_IN_PACKED_ORDER:
        s0, size = src[name]
        if size < LANES:
            small.append(wt_ref[s0:s0 + size, :])
            continue
        o_ref[:, dst:dst + size] = wt_ref[s0:s0 + size, :].T.astype(BF16)
        dst += size
    used = sum(x.shape[0] for x in small)
    small.append(jnp.zeros((LANES - used, wt_ref.shape[1]), F32))
    o_ref[:, dst:dst + LANES] = jnp.concatenate(small, axis=0).T.astype(BF16)


def _pack_w_in(w, tr=256):
    depth, D, N = w.shape
    return pl.pallas_call(
        _pack_kernel,
        out_shape=jax.ShapeDtypeStruct((depth, D, PROJ_PACKED), BF16),
        grid=(depth, D // tr),
        in_specs=[pl.BlockSpec((None, N, tr), lambda l, i: (l, 0, i))],
        out_specs=pl.BlockSpec((None, tr, PROJ_PACKED), lambda l, i: (l, i, 0)),
        compiler_params=_cparams(("parallel", "parallel")),
        name="pack_w_in",
    )(jnp.swapaxes(w, 1, 2))


def _mixer_tail(h, o_a, o_b, o_c, p_all, lw, layer):
    h = out_proj_ffn(h, o_a, o_b, o_c, lw["w_out"], lw["g_ffn"], lw["w_ffn_gate"], lw["w_ffn_up"],
                     lw["w_ffn_down"], TM_FFN, TF_FFN)
    return ple(h, p_all, lw["g_ple"], lw["w_ple_gate"], lw["w_ple_proj"], layer, TM_PLE)


def kernel(x_prompt, x_sample, cache_k, cache_v, cache_idx_k, state_gla, page_table, p_prompt, p_sample,
           g_mix, w_in, q_norm_g, k_norm_g, rel_bias, w_gate_b, b_gate_b, g_out_b, g_v_c, w_spatial,
           b_spatial, w_out, g_ffn, w_ffn_gate, w_ffn_up, w_ffn_down, g_ple, w_ple_gate, w_ple_proj):
    n_batch, seq, d_model = x_prompt.shape
    dec_batch, dec_seq, _ = x_sample.shape
    depth = w_in.shape[0]
    n_pages = page_table.shape[1]
    past = n_pages * PAGE_SIZE
    kw = A_KV_HEADS * HEAD_DIM
    tp, ts = n_batch * seq, dec_batch * dec_seq
    rows_pad = SUBLANES
    l_sample = past + PAGE_SIZE
    k_sel_s = min(TOPK_MAX, (past + dec_seq) // 4)

    bias_p = bias_table_prompt(rel_bias)
    bias_s = bias_table_sample(rel_bias, past, dec_seq, A_KV_HEADS * l_sample)
    cache_ik_t = jnp.swapaxes(cache_idx_k, 2, 3)
    cache_k2 = cache_k.reshape(depth, cache_k.shape[1], PAGE_SIZE * A_KV_HEADS, HEAD_DIM)
    cache_v2 = cache_v.reshape(depth, cache_v.shape[1], PAGE_SIZE * A_KV_HEADS, HEAD_DIM)

    hp = x_prompt.reshape(tp, d_model)
    hs = x_sample.reshape(ts, d_model)
    outs = {k: [] for k in ("kp", "vp", "ikp", "sp", "ks", "vs", "iks", "ss", "cs")}
    per_s = rows_pad // dec_seq
    place_t = (jnp.arange(rows_pad)[:, None, None]
               == jnp.arange(per_s)[None, :, None] * dec_seq + jnp.arange(dec_seq)[None, None, :]
               ).astype(F32)
    w_ple_proj_b = w_ple_proj.astype(BF16)
    w_packed = _pack_w_in(w_in)
    pp_all = p_prompt.reshape(depth, tp, -1)
    ps_all = p_sample.reshape(depth, ts, -1)
    b_cols_p = jnp.swapaxes(b_spatial, 1, 2)
    reps = TILE // dec_seq
    w_tiles_s = jnp.tile(w_spatial[:, :, :dec_seq, :dec_seq], (1, 1, reps, reps))
    b_cols_s = jnp.tile(jnp.swapaxes(b_spatial[:, :, :dec_seq], 1, 2), (1, reps, 1))
    for i in range(depth):
        proj, (kn, vv, ik, o_c, vt, k3, v3, qn, qst), (wo_b, wg_b, wu_b, wd_b, wpg_b) = in_projection(
            hp, g_mix[i], w_packed, k_norm_g[i], q_norm_g[i], (g_v_c[i], w_spatial, b_cols_p, TILE), i, TM_PROJ,
            True, False, cast=(w_out, w_ffn_gate, w_ffn_up, w_ffn_down, w_ple_gate))
        lw = dict(w_out=wo_b, g_ffn=g_ffn[i], w_ffn_gate=wg_b, w_ffn_up=wu_b, w_ffn_down=wd_b,
                  g_ple=g_ple[i], w_ple_gate=wpg_b, w_ple_proj=w_ple_proj_b[i])
        o_a = dsa_prompt(proj, qn, qst, kn, vt, bias_p, n_batch, seq)
        o_b, s_p = gla_prompt(proj, w_gate_b[i], b_gate_b[i], g_out_b[i], n_batch, seq)
        hp = _mixer_tail(hp, o_a, o_b, o_c, pp_all, lw, i)
        outs["kp"].append(k3.reshape(n_batch, seq, A_KV_HEADS, HEAD_DIM))
        outs["vp"].append(v3.reshape(n_batch, seq, A_KV_HEADS, HEAD_DIM))
        outs["ikp"].append(ik.reshape(n_batch, seq, IDX_DIM))
        outs["sp"].append(s_p)

        proj, (kn, vv, ik, o_c, vn), _ = in_projection(
            hs, g_mix[i], w_packed, k_norm_g[i], q_norm_g[i], (g_v_c[i], w_tiles_s, b_cols_s, dec_seq), i, TM_PROJ,
            False, True)
        qi_rows = proj[:, QI_OFF:QI_OFF + IDX_HEADS * IDX_DIM].reshape(dec_batch, dec_seq * IDX_HEADS, IDX_DIM)
        wi = proj[:, MISC_OFF + MISC_WI:MISC_OFF + MISC_WI + IDX_HEADS].reshape(dec_batch, dec_seq, IDX_HEADS)
        wi = wi * (IDX_HEADS ** -0.5 * IDX_DIM ** -0.5)
        wmat = (place_t[None, :, :, :, None] * wi.reshape(dec_batch // per_s, 1, per_s, dec_seq, IDX_HEADS)
                ).reshape(dec_batch // per_s, rows_pad, per_s * dec_seq * IDX_HEADS)
        ki_new_t = jnp.pad(jnp.swapaxes(ik.reshape(dec_batch, dec_seq, IDX_DIM), 1, 2),
                           ((0, 0), (0, 0), (0, PAGE_SIZE - dec_seq)))
        mask = dsa_sample_select(cache_ik_t, i, page_table, qi_rows, wmat, ki_new_t, dec_seq, k_sel_s)
        q_rows = proj[:, Q_OFF:Q_OFF + A_HEADS * HEAD_DIM].reshape(dec_batch, dec_seq * A_HEADS, HEAD_DIM)
        o_a = dsa_sample_attend(cache_k2, cache_v2, i, page_table, q_rows,
                                kn.reshape(dec_batch, dec_seq * A_KV_HEADS, HEAD_DIM),
                                vv.reshape(dec_batch, dec_seq * A_KV_HEADS, HEAD_DIM), mask, bias_s, q_norm_g[i])
        o_a = o_a.reshape(ts, A_HEADS * HEAD_DIM)
        o_b, s_s = gla_sample(proj, w_gate_b[i], b_gate_b[i], g_out_b[i], state_gla, i, dec_seq)
        hs = _mixer_tail(hs, o_a, o_b, o_c, ps_all, lw, i)
        outs["ks"].append(kn.reshape(dec_batch, dec_seq, A_KV_HEADS, HEAD_DIM))
        outs["vs"].append(vv.reshape(dec_batch, dec_seq, A_KV_HEADS, HEAD_DIM))
        outs["iks"].append(ik.reshape(dec_batch, dec_seq, IDX_DIM))
        outs["ss"].append(s_s)
        outs["cs"].append(vn.reshape(dec_batch, dec_seq, -1))

    st = {k: jnp.stack(v) for k, v in outs.items()}
    return (hp.reshape(n_batch, seq, d_model), hs.reshape(dec_batch, dec_seq, d_model),
            st["kp"], st["vp"], st["ikp"], st["sp"], st["ks"], st["vs"], st["iks"], st["ss"], st["cs"])
```

```python
import functools
import math

import jax
import jax.numpy as jnp
from jax import lax
from jax.experimental import pallas as pl
from jax.experimental.pallas import tpu as pltpu

F32 = jnp.float32
BF16 = jnp.bfloat16
I32 = jnp.int32
HIGHEST = lax.Precision.HIGHEST

LANES = 128
SUBLANES = 8
VMEM_LIMIT = 60 * 1024 * 1024

HEAD_DIM = 128
A_HEADS = 8
A_KV_HEADS = 2
IDX_HEADS = 16
IDX_DIM = 64
TOPK_MAX = 256
NUM_BUCKETS = 32
MAX_DISTANCE = 128
B_HEADS = 4
B_DK = 64
B_DV = 128
GATE_RANK = 16
GATE_TEMP = 16.0
C_GROUPS = 4
C_GROUP_DIM = 128
PAGE_SIZE = 128
EPS = 1e-6
NEG_BIG = -1e30
INT_MIN = -(2 ** 31)
NEG_INF_KEY = -2139095041

TILE = 128
QBLK = 256
TM_PROJ = 256
TM_FFN = 512
TF_FFN = 512
TM_PLE = 512

Q_OFF, QI_OFF, VB_OFF, RB_OFF, UC_OFF, VC_OFF = 0, 1024, 2048, 2560, 3072, 3584
K_OFF, V_OFF, QB_OFF, KB_OFF, MISC_OFF = 4096, 4352, 4608, 4864, 5120
PROJ_PACKED = 5248
MISC_KI, MISC_WI, MISC_GB = 0, 64, 80


def _cparams(sem):
    return pltpu.CompilerParams(dimension_semantics=sem, vmem_limit_bytes=VMEM_LIMIT)


def _rms(x, g):
    return x * lax.rsqrt(jnp.mean(x * x, axis=-1, keepdims=True) + EPS) * g


def _resident(shape):
    nd = len(shape)
    return pl.BlockSpec(shape, lambda *_: (0,) * nd, pipeline_mode=pl.Buffered(1))


def _layer_resident(shape, layer):
    nd = len(shape)
    return pl.BlockSpec((None,) + tuple(shape), lambda *_: (layer,) + (0,) * nd, pipeline_mode=pl.Buffered(1))


def _proj_kernel(x_ref, g_ref, w_ref, kg_ref, qg_ref, gv_ref, ws_ref, bcol_ref, *refs, n_cast, with_vt, with_vn,
                 seg):
    cast_in, refs = list(refs[:n_cast]), list(refs[n_cast:])
    o_ref, ko_ref, vo_ref, io_ref, oc_ref = refs[:5]
    refs = refs[5:]
    vt_ref, k3_ref, v3_ref, qn_ref, qst_ref = [refs.pop(0) for _ in range(5)] if with_vt else [None] * 5
    vn_ref = refs.pop(0) if with_vn else None
    cast_out = refs
    n = _rms(x_ref[...], g_ref[...]).astype(BF16)
    ncol = o_ref.shape[1]
    step = 512
    for c0 in range(0, ncol, step):
        c1 = min(c0 + step, ncol)
        o_ref[:, c0:c1] = jnp.dot(n, w_ref[:, c0:c1], preferred_element_type=F32)
    kg = kg_ref[...]
    v = o_ref[:, V_OFF:V_OFF + A_KV_HEADS * HEAD_DIM]
    for hh in range(A_KV_HEADS):
        hs = slice(hh * HEAD_DIM, (hh + 1) * HEAD_DIM)
        kn = _rms(o_ref[:, K_OFF + hh * HEAD_DIM:K_OFF + (hh + 1) * HEAD_DIM], kg)
        ko_ref[:, hs] = kn
        if with_vt:
            k3_ref[:, hh, :] = kn
            v3_ref[:, hh, :] = v[:, hs]
    vo_ref[...] = v
    io_ref[...] = o_ref[:, MISC_OFF + MISC_KI:MISC_OFF + MISC_KI + IDX_DIM]
    if with_vt:
        for blk in range(vt_ref.shape[0]):
            vt_ref[blk] = v[blk * QBLK:(blk + 1) * QBLK, :].T.astype(vt_ref.dtype)
        tm = o_ref.shape[0]
        qg = qg_ref[...]
        for h in range(A_HEADS):
            q = o_ref[:, Q_OFF + h * HEAD_DIM:Q_OFF + (h + 1) * HEAD_DIM]
            qn_ref[h * tm:(h + 1) * tm, :] = (_rms(q, qg) * HEAD_DIM ** -0.5).astype(qn_ref.dtype)
        for h in range(IDX_HEADS):
            qst_ref[h * tm:(h + 1) * tm, :] = o_ref[:, QI_OFF + h * IDX_DIM:QI_OFF + (h + 1) * IDX_DIM].astype(
                qst_ref.dtype)
    cw = C_GROUPS * C_GROUP_DIM
    _gmlp_rows(o_ref.at[:, UC_OFF:UC_OFF + cw], o_ref.at[:, VC_OFF:VC_OFF + cw], gv_ref, ws_ref, bcol_ref,
               oc_ref, vn_ref, seg)
    for src, dst in zip(cast_in, cast_out):
        dst[...] = src[...].astype(dst.dtype)


def in_projection(h, g, w_packed, k_norm_g, q_norm_g, gmlp_params, layer, tm, with_vt, with_vn, cast=()):
    T, D = h.shape
    tm = min(tm, T)
    N = w_packed.shape[2]
    kw = A_KV_HEADS * HEAD_DIM
    cw = C_GROUPS * C_GROUP_DIM
    g_v, w_tiles, b_cols, seg = gmlp_params
    steps = T // tm
    cast_specs_in = [pl.BlockSpec((None, w.shape[1] // steps, w.shape[2]), lambda i: (layer, i, 0)) for w in cast]
    cast_specs_out = [pl.BlockSpec((w.shape[1] // steps, w.shape[2]), lambda i: (i, 0)) for w in cast]
    kv_shape = [jax.ShapeDtypeStruct((T, kw), F32),
                jax.ShapeDtypeStruct((T, kw), F32),
                jax.ShapeDtypeStruct((T, IDX_DIM), F32),
                jax.ShapeDtypeStruct((T, cw), BF16)]
    kv_specs = [pl.BlockSpec((tm, kw), lambda i: (i, 0)),
                pl.BlockSpec((tm, kw), lambda i: (i, 0)),
                pl.BlockSpec((tm, IDX_DIM), lambda i: (i, 0)),
                pl.BlockSpec((tm, cw), lambda i: (i, 0))]
    if with_vt:
        kv_shape.append(jax.ShapeDtypeStruct((T // QBLK, kw, QBLK), BF16))
        kv_specs.append(pl.BlockSpec((tm // QBLK, kw, QBLK), lambda i: (i, 0, 0)))
        for _ in range(2):
            kv_shape.append(jax.ShapeDtypeStruct((T, A_KV_HEADS, HEAD_DIM), F32))
            kv_specs.append(pl.BlockSpec((tm, A_KV_HEADS, HEAD_DIM), lambda i: (i, 0, 0)))
        assert tm == QBLK, "the prompt attention takes one query block per projection step"
        for rows, width in ((A_HEADS * tm, HEAD_DIM), (IDX_HEADS * tm, IDX_DIM)):
            kv_shape.append(jax.ShapeDtypeStruct((steps, rows, width), BF16))
            kv_specs.append(pl.BlockSpec((None, rows, width), lambda i: (i, 0, 0)))
    if with_vn:
        kv_shape.append(jax.ShapeDtypeStruct((T, cw), F32))
        kv_specs.append(pl.BlockSpec((tm, cw), lambda i: (i, 0)))
    outs = pl.pallas_call(
        functools.partial(_proj_kernel, n_cast=len(cast), with_vt=with_vt, with_vn=with_vn, seg=seg),
        out_shape=(jax.ShapeDtypeStruct((T, N), F32),) + tuple(kv_shape) + tuple(
            jax.ShapeDtypeStruct(w.shape[1:], BF16) for w in cast),
        grid=(steps,),
        in_specs=[pl.BlockSpec((tm, D), lambda i: (i, 0)),
                  _resident((1, D)),
                  _layer_resident((D, N), layer),
                  _resident((1, HEAD_DIM)),
                  _resident((1, HEAD_DIM)),
                  _resident((1, cw)),
                  _layer_resident((C_GROUPS, TILE, TILE), layer),
                  _layer_resident((TILE, C_GROUPS), layer)] + cast_specs_in,
        out_specs=(pl.BlockSpec((tm, N), lambda i: (i, 0)),) + tuple(kv_specs) + tuple(cast_specs_out),
        compiler_params=_cparams(("parallel",)),
        name="in_projection",
    )(h, g.reshape(1, D), w_packed, k_norm_g.reshape(1, HEAD_DIM), q_norm_g.reshape(1, HEAD_DIM),
      g_v.reshape(1, cw), w_tiles, b_cols, *cast)
    n_kv = len(kv_shape)
    return outs[0], outs[1:1 + n_kv], outs[1 + n_kv:]


def _bucket(dist):
    n = jnp.maximum(dist, 0)
    max_exact = NUM_BUCKETS // 2
    large = max_exact + (jnp.log(jnp.maximum(n, 1).astype(F32) / max_exact)
                         / math.log(MAX_DISTANCE / max_exact)
                         * (NUM_BUCKETS - max_exact)).astype(I32)
    large = jnp.minimum(large, NUM_BUCKETS - 1)
    return jnp.where(n < max_exact, n, large)


def _bias_prompt_kernel(rb_ref, o_ref):
    c = lax.broadcasted_iota(I32, (TILE, TILE), 0)
    t = lax.broadcasted_iota(I32, (TILE, TILE), 1)
    for z in range(3):
        bucket = _bucket(t - c + (2 - z) * TILE)
        for h in range(A_HEADS):
            acc = jnp.zeros((TILE, TILE), F32)
            for b in range(NUM_BUCKETS):
                acc = jnp.where(bucket == b, rb_ref[b, h], acc)
            o_ref[h, z] = acc


def bias_table_prompt(rel_bias):
    return pl.pallas_call(
        _bias_prompt_kernel,
        out_shape=jax.ShapeDtypeStruct((A_HEADS, 3, TILE, TILE), F32),
        in_specs=[pl.BlockSpec(memory_space=pltpu.SMEM)],
        out_specs=pl.BlockSpec(memory_space=pltpu.VMEM),
        name="bias_table_prompt",
    )(rel_bias)


def _bias_sample_kernel(rbrows_ref, o_ref, *, past, n_tok):
    rows, L = o_ref.shape
    r = lax.broadcasted_iota(I32, (rows, L), 0)
    s = lax.broadcasted_iota(I32, (rows, L), 1) // A_KV_HEADS
    bucket = _bucket(past + r // A_HEADS - s)
    rbrows = rbrows_ref[...]
    acc = jnp.zeros((rows, L), F32)
    for b in range(NUM_BUCKETS):
        acc = jnp.where(bucket == b, rbrows[:, b:b + 1], acc)
    o_ref[...] = acc


def bias_table_sample(rel_bias, past, n_tok, L):
    rows = n_tok * A_HEADS
    rbrows = jnp.tile(rel_bias.T, (n_tok, 1))
    return pl.pallas_call(
        functools.partial(_bias_sample_kernel, past=past, n_tok=n_tok),
        out_shape=jax.ShapeDtypeStruct((rows, L), F32),
        name="bias_table_sample",
    )(rbrows)


def _sortable_key(x):
    b = lax.bitcast_convert_type(x, I32)
    return b ^ ((b >> 31) & 0x7FFFFFFF)


def _topk_member(skey_ref, k_sel):
    R, L = skey_ref.shape

    def body(it, ans):
        bit = 31 - it
        cand = ans | lax.shift_left(jnp.int32(1), bit)
        cand_s = cand ^ INT_MIN
        cnt = jnp.sum(jnp.where(skey_ref[...] >= cand_s, 1.0, 0.0), axis=-1, keepdims=True)
        return jnp.where(cnt >= k_sel, cand, ans)

    ans = lax.fori_loop(0, 32, body, jnp.zeros((R, 1), I32))
    tau = ans ^ INT_MIN
    skey = skey_ref[...]
    gt = skey > tau
    eq = skey == tau
    n_gt = jnp.sum(jnp.where(gt, 1.0, 0.0), axis=-1, keepdims=True)
    room = k_sel - n_gt
    r_i = lax.broadcasted_iota(I32, (LANES, LANES), 0)
    c_i = lax.broadcasted_iota(I32, (LANES, LANES), 1)
    upper = jnp.where(r_i <= c_i, 1.0, 0.0).astype(BF16)
    off = jnp.zeros((R, 1), F32)
    parts = []
    for j in range(L // LANES):
        sl = slice(j * LANES, (j + 1) * LANES)
        eq_j = eq[:, sl]
        run = jnp.dot(jnp.where(eq_j, 1.0, 0.0).astype(BF16), upper, preferred_element_type=F32) + off
        parts.append(gt[:, sl] | (eq_j & (run <= room)))
        off = run[:, LANES - 1:LANES]
    return jnp.concatenate(parts, axis=1)


def _fold8(x, op):
    return op(x.reshape(x.shape[0] // SUBLANES, SUBLANES, x.shape[1]), axis=0)


def _dsa_prompt_kernel(qn_ref, qst_ref, misc_ref, kn_ref, vt_ref, bias_ref, o_ref,
                       skey_ref, madd_ref, lg_ref, acc_ref, *, k_sel):
    i = pl.program_id(1)
    nkb = i + 1
    sub = QBLK // TILE
    rep = A_HEADS // A_KV_HEADS
    row0 = pl.multiple_of(i * QBLK, QBLK)
    s_iota = lax.broadcasted_iota(I32, (QBLK, QBLK), 0)
    t_iota = lax.broadcasted_iota(I32, (QBLK, QBLK), 1)

    def admissible(j):
        return (j * QBLK + s_iota) <= (row0 + t_iota)

    wi_t = misc_ref[pl.ds(row0, QBLK), :].T[MISC_WI:MISC_WI + IDX_HEADS, :]
    wi_t = wi_t * (IDX_HEADS ** -0.5 * IDX_DIM ** -0.5)

    def score_body(j, carry):
        k0 = pl.multiple_of(j * QBLK, QBLK)
        kj = misc_ref[pl.ds(k0, QBLK), MISC_KI:MISC_KI + IDX_DIM].astype(BF16)
        s = lax.dot_general(kj, qst_ref[...], (((1,), (1,)), ((), ())), preferred_element_type=F32)
        score = jnp.zeros((QBLK, QBLK), F32)
        for h in range(IDX_HEADS):
            score = score + jnp.maximum(s[:, h * QBLK:(h + 1) * QBLK], 0.0) * wi_t[h:h + 1, :]
        skey_ref[j] = _sortable_key(jnp.where(admissible(j), score, -jnp.inf))
        return carry

    lax.fori_loop(0, nkb // 2, lambda jj, c: score_body(2 * jj + 1, score_body(2 * jj, c)), 0)

    @pl.when(nkb % 2 == 1)
    def _():
        score_body(nkb - 1, 0)

    def count(pred_fn):
        def hits(j):
            return _fold8(jnp.where(pred_fn(skey_ref[j]), 1.0, 0.0), jnp.sum)

        def body(jj, accs):
            return accs[0] + hits(2 * jj), accs[1] + hits(2 * jj + 1)

        zero = jnp.zeros((SUBLANES, QBLK), F32)
        acc0, acc1 = lax.fori_loop(0, nkb // 2, body, (zero, zero))
        acc = lax.cond(nkb % 2 == 1, lambda: acc0 + acc1 + hits(nkb - 1), lambda: acc0 + acc1)
        return jnp.sum(acc, axis=0, keepdims=True)

    def bit_body(it, ans):
        cand = ans | lax.shift_left(jnp.int32(1), 31 - it)
        cand_s = cand ^ INT_MIN
        cnt = count(lambda key: key >= cand_s)
        return jnp.where(cnt >= k_sel, cand, ans)

    ans = lax.fori_loop(0, 32, bit_body, jnp.zeros((1, QBLK), I32))
    tau = ans ^ INT_MIN
    n_ge = count(lambda key: key >= tau)
    excess = jnp.max(jnp.where((n_ge > k_sel) & (tau != NEG_INF_KEY), 1.0, 0.0))

    @pl.when(excess == 0.0)
    def _():
        def mask_body(j, carry):
            madd_ref[j] = jnp.where((skey_ref[j] >= tau) & admissible(j), 0.0, NEG_BIG)
            return carry

        lax.fori_loop(0, nkb, mask_body, 0)

    @pl.when(excess > 0.0)
    def _():
        room = k_sel - count(lambda key: key > tau)
        lower = jnp.where(t_iota <= s_iota, 1.0, 0.0).astype(BF16)

        def mask_body(j, off):
            key = skey_ref[j]
            eq = key == tau
            run = jnp.dot(lower, jnp.where(eq, 1.0, 0.0).astype(BF16), preferred_element_type=F32) + off
            sel = ((key > tau) | (eq & (run <= room))) & admissible(j)
            madd_ref[j] = jnp.where(sel, 0.0, NEG_BIG)
            return run[QBLK - 1:QBLK, :]

        lax.fori_loop(0, nkb, mask_body, jnp.zeros((1, QBLK), F32))

    wide = rep * QBLK
    for g in range(A_KV_HEADS):
        gs = slice(g * HEAD_DIM, (g + 1) * HEAD_DIM)
        heads = list(range(g * rep, (g + 1) * rep))
        q_stack = qn_ref[g * wide:(g + 1) * wide, :]

        def logit_body(j, mx):
            k0 = pl.multiple_of(j * QBLK, QBLK)
            kj = kn_ref[pl.ds(k0, QBLK), gs].astype(BF16)
            lg = lax.dot_general(kj, q_stack, (((1,), (1,)), ((), ())), preferred_element_type=F32)
            madd = madd_ref[j]
            parts = []
            for r, h in enumerate(heads):
                quads = []
                for c in range(sub):
                    quads.append(jnp.concatenate(
                        [bias_ref[h, jnp.clip(2 - ((i - j) * sub + u - c), 0, 2)] for u in range(sub)], axis=1))
                parts.append(lg[:, r * QBLK:(r + 1) * QBLK] + jnp.concatenate(quads, axis=0) + madd)
            lg = jnp.concatenate(parts, axis=1)
            lg_ref[j] = lg
            return jnp.maximum(mx, _fold8(lg, jnp.max))

        mx = lax.fori_loop(0, nkb // 2, lambda jj, mx_: logit_body(2 * jj + 1, logit_body(2 * jj, mx_)),
                           jnp.full((SUBLANES, wide), NEG_BIG, F32))
        mx = lax.cond(nkb % 2 == 1, lambda mx_: logit_body(nkb - 1, mx_), lambda mx_: mx_, mx)
        m = jnp.max(mx, axis=0, keepdims=True)
        acc_ref[...] = jnp.zeros(acc_ref.shape, F32)

        def pv_body(blocks, sm):
            ps = [jnp.exp(lg_ref[j] - m) for j in blocks]
            acc_ref[...] += sum(jnp.dot(vt_ref[j, gs, :], p.astype(BF16), preferred_element_type=F32)
                                for j, p in zip(blocks, ps))
            return sm + sum(_fold8(p, jnp.sum) for p in ps)

        sm = lax.fori_loop(0, nkb // 2, lambda jj, sm_: pv_body((2 * jj, 2 * jj + 1), sm_),
                           jnp.zeros((SUBLANES, wide), F32))
        sm = lax.cond(nkb % 2 == 1, lambda sm_: pv_body((nkb - 1,), sm_), lambda sm_: sm_, sm)
        den = jnp.sum(sm, axis=0, keepdims=True)
        o = (acc_ref[...] / den).T
        for r, h in enumerate(heads):
            o_ref[:, h * HEAD_DIM:(h + 1) * HEAD_DIM] = o[r * QBLK:(r + 1) * QBLK, :].astype(o_ref.dtype)


def dsa_prompt(proj, qn, qst, kn, vt, bias_tab, n_batch, seq):
    T = proj.shape[0]
    nb = seq // QBLK
    k_sel = min(TOPK_MAX, seq // 4)
    aw = A_HEADS * HEAD_DIM
    kw = A_KV_HEADS * HEAD_DIM
    rep = A_HEADS // A_KV_HEADS
    return pl.pallas_call(
        functools.partial(_dsa_prompt_kernel, k_sel=k_sel),
        out_shape=jax.ShapeDtypeStruct((T, aw), BF16),
        grid=(n_batch, nb),
        in_specs=[pl.BlockSpec((None,) + qn.shape[1:], lambda b, i: (b * nb + i, 0, 0)),
                  pl.BlockSpec((None,) + qst.shape[1:], lambda b, i: (b * nb + i, 0, 0)),
                  pl.BlockSpec((seq, LANES), lambda b, i: (b, MISC_OFF // LANES)),
                  pl.BlockSpec((seq, kw), lambda b, i: (b, 0)),
                  pl.BlockSpec((nb, kw, QBLK), lambda b, i: (b, 0, 0)),
                  _resident((A_HEADS, 3, TILE, TILE))],
        out_specs=pl.BlockSpec((QBLK, aw), lambda b, i: (b * nb + i, 0)),
        scratch_shapes=[pltpu.VMEM((nb, QBLK, QBLK), I32),
                        pltpu.VMEM((nb, QBLK, QBLK), F32),
                        pltpu.VMEM((nb, QBLK, rep * QBLK), F32),
                        pltpu.VMEM((HEAD_DIM, rep * QBLK), F32)],
        compiler_params=_cparams(("parallel", "arbitrary")),
        name="dsa_prompt",
    )(qn, qst, proj, kn, vt, bias_tab)


def _dsa_sample_select_kernel(pt_ref, *refs, n_pages, n_tok, k_sel, rows_pad):
    del pt_ref
    per = rows_pad // n_tok
    page_refs = refs[:per * n_pages]
    qi_ref, wm_ref, kin_ref, mask_ref, sc_ref, skey_ref = refs[per * n_pages:]
    b = pl.program_id(0)
    nb = pl.num_programs(0)
    L = sc_ref.shape[1]
    past = n_pages * PAGE_SIZE

    relu_s = []
    for e in range(per):
        kt_all = jnp.concatenate([r[...].astype(BF16) for r in page_refs[e * n_pages:(e + 1) * n_pages]]
                                 + [kin_ref[e].astype(BF16)], axis=1)
        relu_s.append(jnp.maximum(jnp.dot(qi_ref[e].astype(BF16), kt_all, preferred_element_type=F32), 0.0))
    relu_cat = jnp.concatenate(relu_s, axis=0)
    wm = wm_ref[0]
    r_hi, w_hi = relu_cat.astype(BF16), wm.astype(BF16)
    r_lo, w_lo = (relu_cat - r_hi.astype(F32)).astype(BF16), (wm - w_hi.astype(F32)).astype(BF16)
    score = (jnp.dot(w_hi, r_hi, preferred_element_type=F32) + jnp.dot(w_hi, r_lo, preferred_element_type=F32)
             + jnp.dot(w_lo, r_hi, preferred_element_type=F32))
    r0 = pl.multiple_of(b * rows_pad, rows_pad)
    sc_ref[pl.ds(r0, rows_pad), :] = score

    @pl.when(b == nb - 1)
    def _():
        n_blocks = sc_ref.shape[0] // TILE
        n_tiles = L // PAGE_SIZE
        tp = past + lax.broadcasted_iota(I32, (TILE, L), 0) % n_tok
        sp = lax.broadcasted_iota(I32, (TILE, L), 1)
        adm_blk = sp <= tp
        d_r = lax.broadcasted_iota(I32, (PAGE_SIZE, PAGE_SIZE * A_KV_HEADS), 0)
        d_c = lax.broadcasted_iota(I32, (PAGE_SIZE, PAGE_SIZE * A_KV_HEADS), 1)
        dup = jnp.where(d_c // A_KV_HEADS == d_r, 1.0, 0.0).astype(BF16)
        for rb in range(n_blocks):
            rows = slice(rb * TILE, (rb + 1) * TILE)
            skey_ref[...] = _sortable_key(jnp.where(adm_blk, sc_ref[rows, :], -jnp.inf))
            sel = jnp.where(_topk_member(skey_ref, k_sel) & adm_blk, 1.0, 0.0).astype(BF16)
            stacked = jnp.concatenate([sel[:, j * PAGE_SIZE:(j + 1) * PAGE_SIZE] for j in range(n_tiles)], axis=0)
            stacked = jnp.dot(stacked, dup, preferred_element_type=F32)
            mask_ref[rows, :] = jnp.concatenate(
                [stacked[j * TILE:(j + 1) * TILE, :] for j in range(n_tiles)], axis=1)


def dsa_sample_select(cache_ik_t, layer, page_table, qi_rows, wmat, ki_new_t, n_tok, k_sel):
    DB, n_pages = page_table.shape
    rows_pad = wmat.shape[1]
    per = rows_pad // n_tok
    n_rows = DB // per * rows_pad
    L = (n_pages + 1) * PAGE_SIZE
    page_specs = [pl.BlockSpec((None, None, IDX_DIM, PAGE_SIZE), functools.partial(
        lambda b, pt, e, p: (layer, pt[b * per + e, p], 0, 0), e=e, p=p))
        for e in range(per) for p in range(n_pages)]
    grid_spec = pltpu.PrefetchScalarGridSpec(
        num_scalar_prefetch=1,
        grid=(DB // per,),
        in_specs=page_specs + [
            pl.BlockSpec((per,) + qi_rows.shape[1:], lambda b, pt: (b, 0, 0)),
            pl.BlockSpec((1,) + wmat.shape[1:], lambda b, pt: (b, 0, 0)),
            pl.BlockSpec((per, IDX_DIM, PAGE_SIZE), lambda b, pt: (b, 0, 0))],
        out_specs=pl.BlockSpec((n_rows, A_KV_HEADS * L), lambda b, pt: (0, 0)),
        scratch_shapes=[pltpu.VMEM((n_rows, L), F32),
                        pltpu.VMEM((TILE, L), I32)],
    )
    return pl.pallas_call(
        functools.partial(_dsa_sample_select_kernel, n_pages=n_pages, n_tok=n_tok, k_sel=k_sel,
                          rows_pad=rows_pad),
        out_shape=jax.ShapeDtypeStruct((n_rows, A_KV_HEADS * L), F32),
        grid_spec=grid_spec,
        compiler_params=_cparams(("arbitrary",)),
        name="dsa_sample_select",
    )(page_table, *([cache_ik_t] * (per * n_pages)), qi_rows, wmat, ki_new_t)


def _dsa_sample_attend_kernel(pt_ref, *refs, n_pages, n_tok, rows_pad):
    del pt_ref
    per = rows_pad // n_tok
    k_refs = refs[:per * n_pages]
    v_refs = refs[per * n_pages:2 * per * n_pages]
    q_ref, kn_ref, vn_ref, mask_ref, bias_ref, qg_ref, o_ref = refs[2 * per * n_pages:]
    rows = n_tok * A_HEADS
    page_rows = PAGE_SIZE * A_KV_HEADS
    n_cols = mask_ref.shape[1]
    pad = jnp.zeros((page_rows - n_tok * A_KV_HEADS, HEAD_DIM), BF16)
    rep = A_HEADS // A_KV_HEADS
    grp = (lax.broadcasted_iota(I32, (rows, 1), 0) % A_HEADS) // rep
    own_group = (lax.broadcasted_iota(I32, (rows, n_cols), 1) % A_KV_HEADS) == grp
    member = mask_ref[...]

    def tiles(page_refs, new_ref, e):
        new = jnp.concatenate([new_ref[e].astype(BF16), pad], axis=0)
        return [r[...].astype(BF16) for r in page_refs[e * n_pages:(e + 1) * n_pages]] + [new]

    els = range(per)
    qs = [(_rms(q_ref[e], qg_ref[...]) * HEAD_DIM ** -0.5).astype(BF16) for e in els]
    valid = [(jnp.concatenate(
        [jnp.broadcast_to(member[e * n_tok + t:e * n_tok + t + 1, :], (A_HEADS, n_cols)) for t in range(n_tok)],
        axis=0) > 0.5) & own_group for e in els]
    logits = [jnp.concatenate(
        [lax.dot_general(qs[e], kt, (((1,), (1,)), ((), ())), preferred_element_type=F32)
         for kt in tiles(k_refs, kn_ref, e)], axis=1) for e in els]
    logits = [jnp.where(valid[e], logits[e] + bias_ref[...], NEG_BIG) for e in els]
    ms = [jnp.max(logits[e], axis=-1, keepdims=True) for e in els]
    ps = [jnp.exp(logits[e] - ms[e]) for e in els]
    dens = [jnp.sum(ps[e], axis=-1, keepdims=True) for e in els]
    pbs = [ps[e].astype(BF16) for e in els]
    outs = [jnp.zeros((rows, HEAD_DIM), F32) for _ in els]
    vts = [tiles(v_refs, vn_ref, e) for e in els]
    for j in range(len(vts[0])):
        for e in els:
            outs[e] = outs[e] + jnp.dot(pbs[e][:, j * page_rows:(j + 1) * page_rows], vts[e][j],
                                        preferred_element_type=F32)
    for e in els:
        o_ref[e] = (outs[e] / dens[e]).astype(o_ref.dtype)


def dsa_sample_attend(cache_k, cache_v, layer, page_table, q_rows, k_new, v_new, mask, bias_tab, q_norm_g):
    DB, n_pages = page_table.shape
    n_tok = k_new.shape[1] // A_KV_HEADS
    rows = n_tok * A_HEADS
    rows_pad = SUBLANES
    per = rows_pad // n_tok
    n_cols = mask.shape[1]
    page_rows = PAGE_SIZE * A_KV_HEADS
    page_specs = [pl.BlockSpec((None, None, page_rows, HEAD_DIM), functools.partial(
        lambda b, pt, e, p: (layer, pt[b * per + e, p], 0, 0), e=e, p=p))
        for e in range(per) for p in range(n_pages)]
    grid_spec = pltpu.PrefetchScalarGridSpec(
        num_scalar_prefetch=1,
        grid=(DB // per,),
        in_specs=page_specs + page_specs + [
            pl.BlockSpec((per, rows, HEAD_DIM), lambda b, pt: (b, 0, 0)),
            pl.BlockSpec((per, n_tok * A_KV_HEADS, HEAD_DIM), lambda b, pt: (b, 0, 0)),
            pl.BlockSpec((per, n_tok * A_KV_HEADS, HEAD_DIM), lambda b, pt: (b, 0, 0)),
            pl.BlockSpec((rows_pad, n_cols), lambda b, pt: (b, 0)),
            pl.BlockSpec((rows, n_cols), lambda b, pt: (0, 0), pipeline_mode=pl.Buffered(1)),
            pl.BlockSpec((1, HEAD_DIM), lambda b, pt: (0, 0), pipeline_mode=pl.Buffered(1))],
        out_specs=pl.BlockSpec((per, rows, HEAD_DIM), lambda b, pt: (b, 0, 0)),
    )
    return pl.pallas_call(
        functools.partial(_dsa_sample_attend_kernel, n_pages=n_pages, n_tok=n_tok, rows_pad=rows_pad),
        out_shape=jax.ShapeDtypeStruct((DB, rows, HEAD_DIM), BF16),
        grid_spec=grid_spec,
        compiler_params=_cparams(("parallel",)),
        name="dsa_sample_attend",
    )(page_table, *([cache_k] * (per * n_pages)), *([cache_v] * (per * n_pages)), q_rows, k_new, v_new, mask,
      bias_tab, q_norm_g.reshape(1, HEAD_DIM))


def _log_sigmoid(z):
    return jnp.minimum(z, 0.0) - jnp.log(1.0 + jnp.exp(-jnp.abs(z)))


def _seg_masks(seg):
    r = lax.broadcasted_iota(I32, (TILE, TILE), 0)
    c = lax.broadcasted_iota(I32, (TILE, TILE), 1)
    return r, c, (r // seg) == (c // seg)


def _gla_levels(seg):
    w, out = seg // 2, []
    while w >= 1:
        out.append(w)
        w //= 2
    return out


def _gla_sum_matrices(seg):
    r = jnp.arange(TILE)[:, None]
    c = jnp.arange(TILE)[None, :]
    mats = []
    for w in _gla_levels(seg):
        same = (r // (2 * w)) == (c // (2 * w))
        r_right = (r % (2 * w)) >= w
        c_right = (c % (2 * w)) >= w
        mats.append(same & r_right & c_right & (c <= r))
    for w in _gla_levels(seg):
        same = (r // (2 * w)) == (c // (2 * w))
        r_right = (r % (2 * w)) >= w
        c_right = (c % (2 * w)) >= w
        mats.append(same & (~r_right) & (~c_right) & (c > r))
    same_seg = (r // seg) == (c // seg)
    mats.append(same_seg & (c <= r))
    mats.append(same_seg & (c > r))
    return jnp.concatenate(mats, axis=0).astype(BF16)


def _bdot(a, b):
    return jnp.dot(a.astype(BF16), b.astype(BF16), preferred_element_type=F32)


def _bdot_nt(a, b):
    return lax.dot_general(a.astype(BF16), b.astype(BF16), (((1,), (1,)), ((), ())), preferred_element_type=F32)


def _head_pair_scores(q2, k2):
    k2 = k2.astype(BF16)
    first = lax.broadcasted_iota(I32, k2.shape, 1) < B_DK
    zero = jnp.zeros_like(k2)
    k_bd = jnp.concatenate([jnp.where(first, k2, zero), jnp.where(first, zero, k2)], axis=0)
    return lax.dot_general(q2.astype(BF16), k_bd, (((1,), (1,)), ((), ())), preferred_element_type=F32)


def _gla_common(tiles, wg_ref, bg_ref, mats_ref, seg):
    n = len(tiles)
    kw = B_HEADS * B_DK
    las = []
    for _, _, misc_ref in tiles:
        gb = misc_ref[:, MISC_GB:MISC_GB + GATE_RANK]
        z = jnp.dot(gb, wg_ref[...], precision=HIGHEST, preferred_element_type=F32) + bg_ref[...]
        las.append(_log_sigmoid(z) / GATE_TEMP)
    la = jnp.concatenate(las, axis=1)
    la_hi = la.astype(BF16)
    la_lo = (la - la_hi.astype(F32)).astype(BF16)
    mats = mats_ref[...]
    sums = (jnp.dot(mats, la_hi, preferred_element_type=F32) + jnp.dot(mats, la_lo, preferred_element_type=F32))
    levels = _gla_levels(seg)
    nl = len(levels)
    qs = [qb_ref[...] * B_DK ** -0.5 for qb_ref, _, _ in tiles]
    ks = [kb_ref[...] for _, kb_ref, _ in tiles]
    r, c, _ = _seg_masks(seg)
    atts = [[jnp.where(r == c, _bdot_nt(qs[i][:, h * B_DK:(h + 1) * B_DK], ks[i][:, h * B_DK:(h + 1) * B_DK]), 0.0)
             for h in range(B_HEADS)] for i in range(n)]
    for li, w in enumerate(levels):
        pair = ((r // (2 * w)) == (c // (2 * w))) & ((r % (2 * w)) >= w) & ((c % (2 * w)) < w)
        qd = [(qs[i] * jnp.exp(sums[li * TILE:(li + 1) * TILE, i * kw:(i + 1) * kw])).astype(BF16) for i in range(n)]
        kd = [(ks[i] * jnp.exp(sums[(nl + li) * TILE:(nl + li + 1) * TILE, i * kw:(i + 1) * kw])).astype(BF16)
              for i in range(n)]
        for hp in range(B_HEADS // 2):
            ps = slice(2 * hp * B_DK, (2 * hp + 2) * B_DK)
            for i in range(n):
                a = _head_pair_scores(qd[i][:, ps], kd[i][:, ps])
                atts[i][2 * hp] = atts[i][2 * hp] + jnp.where(pair, a[:, :TILE], 0.0)
                atts[i][2 * hp + 1] = atts[i][2 * hp + 1] + jnp.where(pair, a[:, TILE:], 0.0)
    out = []
    for i in range(n):
        cs = slice(i * kw, (i + 1) * kw)
        b_cum = sums[2 * nl * TILE:(2 * nl + 1) * TILE, cs]
        rem = sums[(2 * nl + 1) * TILE:(2 * nl + 2) * TILE, cs]
        out.append((qs[i], ks[i], atts[i], b_cum, rem))
    return out


def _gla_finish(o_heads, rb_ref, go_ref, o_ref):
    go = go_ref[...]
    for h in range(B_HEADS):
        vs = slice(h * B_DV, (h + 1) * B_DV)
        rb = rb_ref[:, vs]
        o_ref[:, vs] = (_rms(o_heads[h], go) * (rb * jax.nn.sigmoid(rb))).astype(o_ref.dtype)


def _gla_prompt_kernel(qb_ref, kb_ref, vb_ref, rb_ref, misc_ref, wg_ref, bg_ref, go_ref, mats_ref,
                       o_ref, s_ref, state_ref):
    ci = pl.program_id(0)

    @pl.when(ci == 0)
    def _():
        state_ref[...] = jnp.zeros_like(state_ref)

    nb = qb_ref.shape[0]
    common = _gla_common([(qb_ref.at[b], kb_ref.at[b], misc_ref.at[b]) for b in range(nb)],
                         wg_ref, bg_ref, mats_ref, TILE)
    vals = [vb_ref[b] for b in range(nb)]
    states = [state_ref[b] for b in range(nb)]
    qes = [common[b][0] * jnp.exp(common[b][3]) for b in range(nb)]
    o_heads = [[] for _ in range(nb)]
    for h in range(B_HEADS):
        ks = slice(h * B_DK, (h + 1) * B_DK)
        vs = slice(h * B_DV, (h + 1) * B_DV)
        for b in range(nb):
            o_heads[b].append(_bdot(qes[b][:, ks], states[b][ks, :]) + _bdot(common[b][2][h], vals[b][:, vs]))
    for b in range(nb):
        _gla_finish(o_heads[b], rb_ref.at[b], go_ref, o_ref.at[b])

    ke_ts = [(common[b][1] * jnp.exp(common[b][4])).T for b in range(nb)]
    e_cols = [jnp.broadcast_to(jnp.exp(common[b][3][TILE - 1:TILE, :]), (TILE, B_HEADS * B_DK)).T[:, 0:1]
              for b in range(nb)]
    for b in range(nb):
        upd = jnp.concatenate(
            [_bdot(ke_ts[b][h * B_DK:(h + 1) * B_DK, :], vals[b][:, h * B_DV:(h + 1) * B_DV])
             for h in range(B_HEADS)], axis=0)
        new_state = states[b] * e_cols[b] + upd
        state_ref[b] = new_state
        s_ref[b] = new_state


def gla_prompt(proj, w_gate, b_gate, g_out, n_batch, seq):
    nc = seq // TILE
    kwid = B_HEADS * B_DK
    vwid = B_HEADS * B_DV
    mats = _gla_sum_matrices(TILE)
    proj3 = proj.reshape(n_batch, seq, proj.shape[1])
    o, s = pl.pallas_call(
        _gla_prompt_kernel,
        out_shape=(jax.ShapeDtypeStruct((n_batch, seq, vwid), BF16),
                   jax.ShapeDtypeStruct((n_batch, kwid, B_DV), F32)),
        grid=(nc,),
        in_specs=[pl.BlockSpec((n_batch, TILE, kwid), lambda c: (0, c, QB_OFF // kwid)),
                  pl.BlockSpec((n_batch, TILE, kwid), lambda c: (0, c, KB_OFF // kwid)),
                  pl.BlockSpec((n_batch, TILE, vwid), lambda c: (0, c, VB_OFF // vwid)),
                  pl.BlockSpec((n_batch, TILE, vwid), lambda c: (0, c, RB_OFF // vwid)),
                  pl.BlockSpec((n_batch, TILE, LANES), lambda c: (0, c, MISC_OFF // LANES)),
                  _resident((GATE_RANK, kwid)),
                  _resident((1, kwid)),
                  _resident((1, B_DV)),
                  _resident(mats.shape)],
        out_specs=(pl.BlockSpec((n_batch, TILE, vwid), lambda c: (0, c, 0)),
                   pl.BlockSpec((n_batch, kwid, B_DV), lambda c: (0, 0, 0))),
        scratch_shapes=[pltpu.VMEM((n_batch, kwid, B_DV), F32)],
        compiler_params=_cparams(("arbitrary",)),
        name="gla_prompt",
    )(proj3, proj3, proj3, proj3, proj3, w_gate, b_gate.reshape(1, kwid), g_out.reshape(1, B_DV), mats)
    return o.reshape(n_batch * seq, vwid), s.reshape(n_batch, B_HEADS, B_DK, B_DV)


def _gla_sample_kernel(qb_ref, kb_ref, vb_ref, rb_ref, misc_ref, wg_ref, bg_ref, go_ref, mats_ref, s0_ref,
                       o_ref, s_ref, *, seg):
    nbt = TILE // seg
    (q, k, att, b_cum, rem), = _gla_common([(qb_ref, kb_ref, misc_ref)], wg_ref, bg_ref, mats_ref, seg)
    v = vb_ref[...]
    qe = q * jnp.exp(b_cum)
    ke = k * jnp.exp(rem)
    r1 = lax.broadcasted_iota(I32, (TILE, 1), 0)
    e_last = jnp.where(r1 % seg == seg - 1, jnp.exp(b_cum), 0.0)
    wide = nbt * B_DK
    mq = (lax.broadcasted_iota(I32, (TILE, wide), 0) // seg) == (lax.broadcasted_iota(I32, (TILE, wide), 1) // B_DK)
    mk = (lax.broadcasted_iota(I32, (wide, TILE), 0) // B_DK) == (lax.broadcasted_iota(I32, (wide, TILE), 1) // seg)
    o_heads = []
    for h in range(B_HEADS):
        ks = slice(h * B_DK, (h + 1) * B_DK)
        vs = slice(h * B_DV, (h + 1) * B_DV)
        state = s0_ref[:, h].reshape(wide, B_DV)
        q_bd = jnp.where(mq, jnp.concatenate([qe[:, ks]] * nbt, axis=1), 0.0)
        o_heads.append(_bdot(q_bd, state) + _bdot(att[h], v[:, vs]))
        pair_t = jnp.concatenate([ke[:, ks], e_last[:, ks]], axis=1).T
        k_bd = jnp.where(mk, jnp.concatenate([pair_t[:B_DK]] * nbt, axis=0), 0.0)
        e_bd = jnp.where(mk, jnp.concatenate([pair_t[B_DK:]] * nbt, axis=0), 0.0)
        e_col = jnp.sum(e_bd, axis=-1, keepdims=True)
        new_state = state * e_col + _bdot(k_bd, v[:, vs])
        s_ref[:, h] = new_state.reshape(nbt, B_DK, B_DV)
    _gla_finish(o_heads, rb_ref, go_ref, o_ref)


def gla_sample(proj, w_gate, b_gate, g_out, s0, layer, n_tok):
    T = proj.shape[0]
    nbt = TILE // n_tok
    kwid = B_HEADS * B_DK
    vwid = B_HEADS * B_DV
    mats = _gla_sum_matrices(n_tok)
    return pl.pallas_call(
        functools.partial(_gla_sample_kernel, seg=n_tok),
        out_shape=(jax.ShapeDtypeStruct((T, vwid), BF16),
                   jax.ShapeDtypeStruct(s0.shape[1:], F32)),
        grid=(T // TILE,),
        in_specs=[pl.BlockSpec((TILE, kwid), lambda i: (i, QB_OFF // kwid)),
                  pl.BlockSpec((TILE, kwid), lambda i: (i, KB_OFF // kwid)),
                  pl.BlockSpec((TILE, vwid), lambda i: (i, VB_OFF // vwid)),
                  pl.BlockSpec((TILE, vwid), lambda i: (i, RB_OFF // vwid)),
                  pl.BlockSpec((TILE, LANES), lambda i: (i, MISC_OFF // LANES)),
                  _resident((GATE_RANK, kwid)),
                  _resident((1, kwid)),
                  _resident((1, B_DV)),
                  _resident(mats.shape),
                  pl.BlockSpec((None, nbt, B_HEADS, B_DK, B_DV), lambda i: (layer, i, 0, 0, 0))],
        out_specs=(pl.BlockSpec((TILE, vwid), lambda i: (i, 0)),
                   pl.BlockSpec((nbt, B_HEADS, B_DK, B_DV), lambda i: (i, 0, 0, 0))),
        compiler_params=_cparams(("parallel",)),
        name="gla_sample",
    )(proj, proj, proj, proj, proj, w_gate, b_gate.reshape(1, kwid), g_out.reshape(1, B_DV), mats, s0)


def _gelu(x):
    return jax.nn.gelu(x)


def _gmlp_rows(uc_ref, vc_ref, gv_ref, ws_ref, bcol_ref, o_ref, vn_ref, seg):
    r, c, same_seg = _seg_masks(seg)
    keep = same_seg & (c <= r)
    for t in range(uc_ref.shape[0] // TILE):
        rows = slice(t * TILE, (t + 1) * TILE)
        u = _gelu(uc_ref[rows, :])
        vg = _gelu(vc_ref[rows, :])
        for g in range(C_GROUPS):
            gs = slice(g * C_GROUP_DIM, (g + 1) * C_GROUP_DIM)
            vn = _rms(vg[:, gs], gv_ref[:, gs])
            if vn_ref is not None:
                vn_ref[rows, gs] = vn
            w = jnp.where(keep, ws_ref[g], 0.0).astype(BF16)
            s = jnp.dot(w, vn.astype(BF16), preferred_element_type=F32) + bcol_ref[:, g:g + 1]
            o_ref[rows, gs] = (u[:, gs] * s).astype(o_ref.dtype)


def _ffn_kernel(h_ref, oa_ref, ob_ref, oc_ref, wo_ref, g_ref, wg_ref, wu_ref, wd_ref, o_ref, n_ref):
    j = pl.program_id(1)

    @pl.when(j == 0)
    def _():
        aw = oa_ref.shape[1]
        bw = ob_ref.shape[1]
        h = h_ref[...] + jnp.dot(oa_ref[...], wo_ref[0:aw, :], preferred_element_type=F32)
        h = h + jnp.dot(ob_ref[...], wo_ref[aw:aw + bw, :], preferred_element_type=F32)
        h = h + jnp.dot(oc_ref[...], wo_ref[aw + bw:, :], preferred_element_type=F32)
        n_ref[...] = _rms(h, g_ref[...]).astype(BF16)
        o_ref[...] = h

    n = n_ref[...]
    a = jnp.dot(n, wg_ref[...], preferred_element_type=F32)
    u = jnp.dot(n, wu_ref[...], preferred_element_type=F32)
    act = (a * jax.nn.sigmoid(a) * u).astype(BF16)
    o_ref[...] += jnp.dot(act, wd_ref[...], preferred_element_type=F32)


def out_proj_ffn(h, o_a, o_b, o_c, w_out, g, w_gate, w_up, w_down, tm, tf):
    T, D = h.shape
    tm = min(tm, T)
    FF = w_gate.shape[1]
    return pl.pallas_call(
        _ffn_kernel,
        out_shape=jax.ShapeDtypeStruct((T, D), F32),
        grid=(T // tm, FF // tf),
        in_specs=[pl.BlockSpec((tm, D), lambda i, j: (i, 0)),
                  pl.BlockSpec((tm, o_a.shape[1]), lambda i, j: (i, 0)),
                  pl.BlockSpec((tm, o_b.shape[1]), lambda i, j: (i, 0)),
                  pl.BlockSpec((tm, o_c.shape[1]), lambda i, j: (i, 0)),
                  _resident(w_out.shape),
                  _resident((1, D)),
                  pl.BlockSpec((D, tf), lambda i, j: (0, j)),
                  pl.BlockSpec((D, tf), lambda i, j: (0, j)),
                  pl.BlockSpec((tf, D), lambda i, j: (j, 0))],
        out_specs=pl.BlockSpec((tm, D), lambda i, j: (i, 0)),
        scratch_shapes=[pltpu.VMEM((tm, D), BF16)],
        compiler_params=_cparams(("parallel", "arbitrary")),
        name="out_proj_ffn",
    )(h, o_a, o_b, o_c, w_out, g.reshape(1, D), w_gate, w_up, w_down)


def _ple_kernel(h_ref, p_ref, g_ref, wgate_ref, wproj_ref, o_ref):
    h = h_ref[...]
    n = _rms(h, g_ref[...]).astype(BF16)
    gate = jax.nn.sigmoid(jnp.dot(n, wgate_ref[...], preferred_element_type=F32))
    emb = jnp.dot(p_ref[...].astype(BF16), wproj_ref[...], preferred_element_type=F32)
    o_ref[...] = h + gate * emb


def ple(h, p, g, w_gate, w_proj, layer, tm):
    T, D = h.shape
    tm = min(tm, T)
    P = p.shape[2]
    return pl.pallas_call(
        _ple_kernel,
        out_shape=jax.ShapeDtypeStruct((T, D), F32),
        grid=(T // tm,),
        in_specs=[pl.BlockSpec((tm, D), lambda i: (i, 0)),
                  pl.BlockSpec((None, tm, P), lambda i: (layer, i, 0)),
                  _resident((1, D)),
                  _resident(w_gate.shape),
                  _resident(w_proj.shape)],
        out_specs=pl.BlockSpec((tm, D), lambda i: (i, 0)),
        compiler_params=_cparams(("parallel",)),
        name="ple",
    )(h, p, g.reshape(1, D), w_gate, w_proj)


_W_IN_SEGMENTS = (("q", 1024), ("k", 256), ("v", 256), ("qi", 1024), ("ki", 64), ("wi", 16), ("qb", 256),
                  ("kb", 256), ("vb", 512), ("gb", 16), ("rb", 512), ("uc", 512), ("vc", 512))
_W_IN_PACKED_ORDER = ("q", "qi", "vb", "rb", "uc", "vc", "k", "v", "qb", "kb", "ki", "wi", "gb")


def _pack_kernel(wt_ref, o_ref):
    src, start = {}, 0
    for name, size in _W_IN_SEGMENTS:
        src[name] = (start, size)
        start += size
    dst = 0
    small = []
    for name in _W_IN_PACKED_ORDER:
        s0, size = src[name]
        if size < LANES:
            small.append(wt_ref[s0:s0 + size, :])
            continue
        o_ref[:, dst:dst + size] = wt_ref[s0:s0 + size, :].T.astype(BF16)
        dst += size
    used = sum(x.shape[0] for x in small)
    small.append(jnp.zeros((LANES - used, wt_ref.shape[1]), F32))
    o_ref[:, dst:dst + LANES] = jnp.concatenate(small, axis=0).T.astype(BF16)


def _pack_w_in(w, tr=256):
    depth, D, N = w.shape
    return pl.pallas_call(
        _pack_kernel,
        out_shape=jax.ShapeDtypeStruct((depth, D, PROJ_PACKED), BF16),
        grid=(depth, D // tr),
        in_specs=[pl.BlockSpec((None, N, tr), lambda l, i: (l, 0, i))],
        out_specs=pl.BlockSpec((None, tr, PROJ_PACKED), lambda l, i: (l, i, 0)),
        compiler_params=_cparams(("parallel", "parallel")),
        name="pack_w_in",
    )(jnp.swapaxes(w, 1, 2))


def _mixer_tail(h, o_a, o_b, o_c, p_all, lw, layer):
    h = out_proj_ffn(h, o_a, o_b, o_c, lw["w_out"], lw["g_ffn"], lw["w_ffn_gate"], lw["w_ffn_up"],
                     lw["w_ffn_down"], TM_FFN, TF_FFN)
    return ple(h, p_all, lw["g_ple"], lw["w_ple_gate"], lw["w_ple_proj"], layer, TM_PLE)


def kernel(x_prompt, x_sample, cache_k, cache_v, cache_idx_k, state_gla, page_table, p_prompt, p_sample,
           g_mix, w_in, q_norm_g, k_norm_g, rel_bias, w_gate_b, b_gate_b, g_out_b, g_v_c, w_spatial,
           b_spatial, w_out, g_ffn, w_ffn_gate, w_ffn_up, w_ffn_down, g_ple, w_ple_gate, w_ple_proj):
    n_batch, seq, d_model = x_prompt.shape
    dec_batch, dec_seq, _ = x_sample.shape
    depth = w_in.shape[0]
    n_pages = page_table.shape[1]
    past = n_pages * PAGE_SIZE
    kw = A_KV_HEADS * HEAD_DIM
    tp, ts = n_batch * seq, dec_batch * dec_seq
    rows_pad = SUBLANES
    l_sample = past + PAGE_SIZE
    k_sel_s = min(TOPK_MAX, (past + dec_seq) // 4)

    bias_p = bias_table_prompt(rel_bias)
    bias_s = bias_table_sample(rel_bias, past, dec_seq, A_KV_HEADS * l_sample)
    cache_ik_t = jnp.swapaxes(cache_idx_k, 2, 3)
    cache_k2 = cache_k.reshape(depth, cache_k.shape[1], PAGE_SIZE * A_KV_HEADS, HEAD_DIM)
    cache_v2 = cache_v.reshape(depth, cache_v.shape[1], PAGE_SIZE * A_KV_HEADS, HEAD_DIM)

    hp = x_prompt.reshape(tp, d_model)
    hs = x_sample.reshape(ts, d_model)
    outs = {k: [] for k in ("kp", "vp", "ikp", "sp", "ks", "vs", "iks", "ss", "cs")}
    per_s = rows_pad // dec_seq
    place_t = (jnp.arange(rows_pad)[:, None, None]
               == jnp.arange(per_s)[None, :, None] * dec_seq + jnp.arange(dec_seq)[None, None, :]
               ).astype(F32)
    w_ple_proj_b = w_ple_proj.astype(BF16)
    w_packed = _pack_w_in(w_in)
    pp_all = p_prompt.reshape(depth, tp, -1)
    ps_all = p_sample.reshape(depth, ts, -1)
    b_cols_p = jnp.swapaxes(b_spatial, 1, 2)
    reps = TILE // dec_seq
    w_tiles_s = jnp.tile(w_spatial[:, :, :dec_seq, :dec_seq], (1, 1, reps, reps))
    b_cols_s = jnp.tile(jnp.swapaxes(b_spatial[:, :, :dec_seq], 1, 2), (1, reps, 1))
    for i in range(depth):
        proj, (kn, vv, ik, o_c, vt, k3, v3, qn, qst), (wo_b, wg_b, wu_b, wd_b, wpg_b) = in_projection(
            hp, g_mix[i], w_packed, k_norm_g[i], q_norm_g[i], (g_v_c[i], w_spatial, b_cols_p, TILE), i, TM_PROJ,
            True, False, cast=(w_out, w_ffn_gate, w_ffn_up, w_ffn_down, w_ple_gate))
        lw = dict(w_out=wo_b, g_ffn=g_ffn[i], w_ffn_gate=wg_b, w_ffn_up=wu_b, w_ffn_down=wd_b,
                  g_ple=g_ple[i], w_ple_gate=wpg_b, w_ple_proj=w_ple_proj_b[i])
        o_a = dsa_prompt(proj, qn, qst, kn, vt, bias_p, n_batch, seq)
        o_b, s_p = gla_prompt(proj, w_gate_b[i], b_gate_b[i], g_out_b[i], n_batch, seq)
        hp = _mixer_tail(hp, o_a, o_b, o_c, pp_all, lw, i)
        outs["kp"].append(k3.reshape(n_batch, seq, A_KV_HEADS, HEAD_DIM))
        outs["vp"].append(v3.reshape(n_batch, seq, A_KV_HEADS, HEAD_DIM))
        outs["ikp"].append(ik.reshape(n_batch, seq, IDX_DIM))
        outs["sp"].append(s_p)

        proj, (kn, vv, ik, o_c, vn), _ = in_projection(
            hs, g_mix[i], w_packed, k_norm_g[i], q_norm_g[i], (g_v_c[i], w_tiles_s, b_cols_s, dec_seq), i, TM_PROJ,
            False, True)
        qi_rows = proj[:, QI_OFF:QI_OFF + IDX_HEADS * IDX_DIM].reshape(dec_batch, dec_seq * IDX_HEADS, IDX_DIM)
        wi = proj[:, MISC_OFF + MISC_WI:MISC_OFF + MISC_WI + IDX_HEADS].reshape(dec_batch, dec_seq, IDX_HEADS)
        wi = wi * (IDX_HEADS ** -0.5 * IDX_DIM ** -0.5)
        wmat = (place_t[None, :, :, :, None] * wi.reshape(dec_batch // per_s, 1, per_s, dec_seq, IDX_HEADS)
                ).reshape(dec_batch // per_s, rows_pad, per_s * dec_seq * IDX_HEADS)
        ki_new_t = jnp.pad(jnp.swapaxes(ik.reshape(dec_batch, dec_seq, IDX_DIM), 1, 2),
                           ((0, 0), (0, 0), (0, PAGE_SIZE - dec_seq)))
        mask = dsa_sample_select(cache_ik_t, i, page_table, qi_rows, wmat, ki_new_t, dec_seq, k_sel_s)
        q_rows = proj[:, Q_OFF:Q_OFF + A_HEADS * HEAD_DIM].reshape(dec_batch, dec_seq * A_HEADS, HEAD_DIM)
        o_a = dsa_sample_attend(cache_k2, cache_v2, i, page_table, q_rows,
                                kn.reshape(dec_batch, dec_seq * A_KV_HEADS, HEAD_DIM),
                                vv.reshape(dec_batch, dec_seq * A_KV_HEADS, HEAD_DIM), mask, bias_s, q_norm_g[i])
        o_a = o_a.reshape(ts, A_HEADS * HEAD_DIM)
        o_b, s_s = gla_sample(proj, w_gate_b[i], b_gate_b[i], g_out_b[i], state_gla, i, dec_seq)
        hs = _mixer_tail(hs, o_a, o_b, o_c, ps_all, lw, i)
        outs["ks"].append(kn.reshape(dec_batch, dec_seq, A_KV_HEADS, HEAD_DIM))
        outs["vs"].append(vv.reshape(dec_batch, dec_seq, A_KV_HEADS, HEAD_DIM))
        outs["iks"].append(ik.reshape(dec_batch, dec_seq, IDX_DIM))
        outs["ss"].append(s_s)
        outs["cs"].append(vn.reshape(dec_batch, dec_seq, -1))

    st = {k: jnp.stack(v) for k, v in outs.items()}
    return (hp.reshape(n_batch, seq, d_model), hs.reshape(dec_batch, dec_seq, d_model),
            st["kp"], st["vp"], st["ikp"], st["sp"], st["ks"], st["vs"], st["iks"], st["ss"], st["cs"])
```
